```python
import math
import jax, jax.numpy as jnp
from jax import lax
import numpy as np

D_MODEL = 2048
BATCH = 4
SEQ = 2048
DEPTH = 2
DEC_BATCH = 8
DEC_SEQ = 4
PAST_LEN = 16384
PAGE_SIZE = 128

D_MLSTM = D_MODEL // 4
D_NSA = D_MODEL // 2
D_S5 = D_MODEL - D_MLSTM - D_NSA
MLSTM_HEAD_DIM = 128
MLSTM_HEADS = D_MLSTM // MLSTM_HEAD_DIM
MLSTM_CHUNK = 64
NSA_HEAD_DIM = 128
NSA_HEADS = D_NSA // NSA_HEAD_DIM
NSA_KV_HEADS = 2
NSA_GROUP = NSA_HEADS // NSA_KV_HEADS
N_KV_SLOTS = 6
CMP_BLOCK = 32
CMP_STRIDE = 16
CMP_RATIO = CMP_BLOCK // CMP_STRIDE
SEL_BLOCK = 64
SEL_TOPK = 16
FORCE_SCORE = 1e4
WINDOW = 512
NSA_Q_BLOCK = 64
WIN_Q_BLOCK = 128
S5_GROUP_WIDTH = 16
S5_GROUPS = D_S5 // S5_GROUP_WIDTH
S5_STATE = 64
NUM_BUCKETS = 32
MAX_DISTANCE = 128
D_FF = ((8 * D_MODEL // 3 + 127) // 128) * 128
CONV_W = 3
LN_EPS = 1e-5
DEEPNORM_ALPHA = (2 * DEPTH) ** 0.25
DEEPNORM_BETA = (8 * DEPTH) ** -0.25

SPLIT_SIZES = (D_MLSTM, D_MLSTM, D_MLSTM, D_MLSTM, MLSTM_HEADS, MLSTM_HEADS,
               D_NSA, N_KV_SLOTS * NSA_KV_HEADS * NSA_HEAD_DIM, 3 * NSA_HEADS, D_S5)
SPLIT_POINTS = tuple(int(s) for s in np.cumsum(SPLIT_SIZES)[:-1])
IN_COLS = sum(SPLIT_SIZES)

kernel_name = 'hybrid_mlstm_nsa_s5_deepnorm_step'


def layer_norm(x, w, b):
    xf = x.astype(jnp.float32)
    mu = xf.mean(-1, keepdims=True)
    var = jnp.square(xf - mu).mean(-1, keepdims=True)
    return ((xf - mu) * lax.rsqrt(var + LN_EPS) * w + b).astype(x.dtype)


def t5_bucket(dist):
    n = jnp.maximum(dist, 0)
    exact = NUM_BUCKETS // 2
    nf = jnp.maximum(n, 1).astype(jnp.float32)
    large = exact + (jnp.log(nf / exact) / math.log(MAX_DISTANCE / exact) * (NUM_BUCKETS - exact)).astype(jnp.int32)
    return jnp.where(n < exact, n, jnp.minimum(large, NUM_BUCKETS - 1))


def masked_probs(s, mask):
    s = jnp.where(mask, s.astype(jnp.float32), -jnp.inf)
    m = jnp.max(s, -1, keepdims=True)
    m = jnp.where(jnp.isfinite(m), m, 0.0)
    p = jnp.exp(s - m)
    den = p.sum(-1, keepdims=True)
    return p / jnp.where(den > 0, den, 1.0)


def mlstm_chunk(carry, inp):
    c, n, m = carry
    q, k, v, ig, lf = inp
    L = q.shape[1]
    b = jnp.cumsum(lf, axis=1)
    causal = jnp.tril(jnp.ones((L, L), bool))
    dmat = b[:, :, None, :] - b[:, None, :, :] + ig[:, None, :, :]
    dmat = jnp.where(causal[None, :, :, None], dmat, -jnp.inf)
    g = b + m[:, None, :]
    m_row = jnp.maximum(dmat.max(axis=2), g)
    a = jnp.exp(dmat - m_row[:, :, None, :]) * jnp.einsum('bthd,bshd->btsh', q, k)
    w_inter = jnp.exp(g - m_row)
    num = jnp.einsum('btsh,bshd->bthd', a, v) + w_inter[..., None] * jnp.einsum('bthk,bhkv->bthv', q, c)
    den = a.sum(axis=2) + w_inter * jnp.einsum('bthk,bhk->bth', q, n)
    h = num / jnp.maximum(jnp.abs(den), jnp.exp(-m_row))[..., None]
    f_tot = b[:, -1, :]
    w_s = f_tot[:, None, :] - b + ig
    m_new = jnp.maximum(f_tot + m, w_s.max(axis=1))
    ws = jnp.exp(w_s - m_new[:, None, :])
    decay = jnp.exp(f_tot + m - m_new)
    c_new = decay[..., None, None] * c + jnp.einsum('bsh,bshk,bshv->bhkv', ws, k, v)
    n_new = decay[..., None] * n + jnp.einsum('bsh,bshk->bhk', ws, k)
    return (c_new, n_new, m_new), h


def mlstm_mix(q, k, v, ig, lf, c0, n0, m0):
    B, T, H, Dh = q.shape
    L = math.gcd(T, MLSTM_CHUNK)
    nc = T // L

    def chunks(a):
        a = a.astype(jnp.float32)
        return jnp.moveaxis(a.reshape((B, nc, L) + a.shape[2:]), 1, 0)

    init = (c0.astype(jnp.float32), n0.astype(jnp.float32), m0.astype(jnp.float32))
    (c, n, m), h = lax.scan(mlstm_chunk, init, (chunks(q), chunks(k), chunks(v), chunks(ig), chunks(lf)))
    return jnp.moveaxis(h, 0, 1).reshape(B, T, H, Dh), c, n, m


def nsa_compress(rows, w1, b1, w2):
    B, T = rows.shape[:2]
    n_chunks = T // CMP_STRIDE
    n_cmp = n_chunks - CMP_RATIO + 1
    chunks = rows[:, :n_chunks * CMP_STRIDE].reshape(B, n_chunks, CMP_STRIDE, NSA_KV_HEADS, NSA_HEAD_DIM)
    w1c = w1.reshape(CMP_RATIO, CMP_STRIDE, NSA_HEAD_DIM, NSA_HEAD_DIM)
    hid = b1
    for o in range(CMP_RATIO):
        hid = hid + jnp.einsum('bnlgd,lde->bnge', chunks[:, o:o + n_cmp], w1c[o])
    return jnp.einsum('bnge,ed->bngd', jax.nn.gelu(hid), w2)


def to_sel_blocks(rows):
    B, T = rows.shape[:2]
    ns = -(-T // SEL_BLOCK)
    rows = jnp.pad(rows, ((0, 0), (0, ns * SEL_BLOCK - T), (0, 0), (0, 0)))
    return rows.reshape(B, ns, SEL_BLOCK, NSA_KV_HEADS, NSA_HEAD_DIM).transpose(0, 3, 1, 2, 4)


def nsa_cmp_sel(q, q_pos, kc, vc, kb, vb, tb):
    f32 = jnp.float32
    B, Q = q.shape[:2]
    G, J, Dh = NSA_KV_HEADS, NSA_GROUP, NSA_HEAD_DIM
    n_cmp, n_sel = kc.shape[1], kb.shape[2]
    qg = q.reshape(B, Q, G, J, Dh)
    c_start = jnp.arange(n_cmp) * CMP_STRIDE
    dist_c = q_pos[:, None] - (c_start + CMP_BLOCK - 1)[None, :]
    bias_c = jnp.moveaxis(tb[t5_bucket(dist_c)], 1, -1)
    s_c = jnp.einsum('bqgjd,bngd->bqgjn', qg, kc) + bias_c
    p_c = masked_probs(s_c, (dist_c >= 0)[None, :, None, None, :])
    o_c = jnp.einsum('bqgjn,bngd->bqgjd', p_c, vc.astype(f32))
    s_start = jnp.arange(n_sel) * SEL_BLOCK
    overlap = ((c_start[:, None] < s_start[None, :] + SEL_BLOCK)
               & (c_start[:, None] + CMP_BLOCK > s_start[None, :])).astype(f32)
    imp = jnp.einsum('bqgjn,ns->bqgs', p_c, overlap)
    cur = (q_pos // SEL_BLOCK)[:, None]
    blk = jnp.arange(n_sel)[None, :]
    forced = (blk == 0) | (blk == cur) | (blk == cur - 1)
    imp = jnp.where(forced[None, :, None, :], FORCE_SCORE, imp)
    imp = jnp.where((blk > cur)[None, :, None, :], -1.0, imp)
    _, idx = lax.top_k(imp, min(SEL_TOPK, n_sel))
    idx = jnp.moveaxis(idx, 1, 2)
    gather = jax.vmap(jax.vmap(lambda blocks, ix: blocks[ix]))
    ks = gather(kb, idx)
    vs = gather(vb, idx)
    k_pos = idx[..., None] * SEL_BLOCK + jnp.arange(SEL_BLOCK)
    dist_s = q_pos[None, None, :, None, None] - k_pos
    bias_s = tb[t5_bucket(dist_s), jnp.arange(G)[None, :, None, None, None]]
    s_s = jnp.einsum('bqgjd,bgqksd->bgqjks', qg, ks) + jnp.moveaxis(bias_s, -1, 3)
    n_k = idx.shape[-1] * SEL_BLOCK
    p_s = masked_probs(s_s.reshape(B, G, Q, J, n_k), (dist_s >= 0).reshape(B, G, Q, 1, n_k))
    o_s = jnp.einsum('bgqjm,bgqmd->bqgjd', p_s, vs.reshape(B, G, Q, n_k, Dh).astype(f32))
    return o_c.reshape(B, Q, G * J, Dh), o_s.reshape(B, Q, G * J, Dh)


def nsa_long_branches(q, q_pos, kv4, w1, b1, w2, tb):
    B, Q = q.shape[:2]
    kc = nsa_compress(kv4[:, :, 0], w1[0], b1[0], w2[0])
    vc = nsa_compress(kv4[:, :, 1], w1[1], b1[1], w2[1])
    kb = to_sel_blocks(kv4[:, :, 2])
    vb = to_sel_blocks(kv4[:, :, 3])
    qb = math.gcd(Q, NSA_Q_BLOCK)
    nq = Q // qb
    qs = jnp.moveaxis(q.reshape(B, nq, qb, NSA_HEADS, NSA_HEAD_DIM), 1, 0)
    ps = q_pos.reshape(nq, qb)
    o_c, o_s = lax.map(lambda a: nsa_cmp_sel(a[0], a[1], kc, vc, kb, vb, tb), (qs, ps))
    unblock = lambda o: jnp.moveaxis(o, 0, 1).reshape(B, Q, NSA_HEADS, NSA_HEAD_DIM)
    return unblock(o_c), unblock(o_s)


def window_attend(q, k, v, q_pos, k_pos, tb):
    B, N, Q = q.shape[:3]
    qg = q.reshape(B, N, Q, NSA_KV_HEADS, NSA_GROUP, NSA_HEAD_DIM)
    dist = q_pos[:, :, None] - k_pos[:, None, :]
    mask = (dist >= 0) & (dist <= WINDOW) & (k_pos[:, None, :] >= 0)
    bias = jnp.moveaxis(tb[t5_bucket(dist)], 2, -1)
    s = jnp.einsum('bnqgjd,bnkgd->bnqgjk', qg, k) + bias
    p = masked_probs(s, mask[None, :, :, None, None, :])
    o = jnp.einsum('bnqgjk,bnkgd->bnqgjd', p, v.astype(jnp.float32))
    return o.reshape(B, N, Q, NSA_HEADS, NSA_HEAD_DIM)


def window_prompt(q, k, v, tb):
    B, T = q.shape[:2]
    qb = math.gcd(T, WIN_Q_BLOCK)
    nb = T // qb
    nw = -(-WINDOW // qb)
    pad = nw * qb

    def band(a):
        ap = jnp.pad(a, ((0, 0), (pad, 0), (0, 0), (0, 0))).reshape(B, nb + nw, qb, NSA_KV_HEADS, NSA_HEAD_DIM)
        return jnp.concatenate([ap[:, o:o + nb] for o in range(nw + 1)], axis=2)

    q_pos = jnp.arange(T).reshape(nb, qb)
    k_pos = (jnp.arange(nb) * qb - pad)[:, None] + jnp.arange((nw + 1) * qb)[None, :]
    o = window_attend(q.reshape(B, nb, qb, NSA_HEADS, NSA_HEAD_DIM), band(k), band(v), q_pos, k_pos, tb)
    return o.reshape(B, T, NSA_HEADS, NSA_HEAD_DIM)


def ssm_combine(e1, e2):
    a1, b1 = e1
    a2, b2 = e2
    return a1 * a2, a2 * b1 + b2


def s5_mix(u, x_re, x_im, a_re, a_im, b_re, b_im, c_re, c_im, d, log_step, glu_w, glu_b):
    f32 = jnp.float32
    B, T = u.shape[:2]
    uf = u.astype(f32).reshape(B, T, S5_GROUPS, S5_GROUP_WIDTH)
    lam = lax.complex(a_re.astype(f32), a_im.astype(f32))
    lam_bar = jnp.exp(lam * jnp.exp(log_step.astype(f32))[:, None])
    b_bar = ((lam_bar - 1.0) / lam)[..., None] * lax.complex(b_re.astype(f32), b_im.astype(f32))
    c_mat = lax.complex(c_re.astype(f32), c_im.astype(f32))
    bu = jnp.einsum('gph,btgh->btgp', b_bar, uf.astype(jnp.complex64))
    bu = bu.at[:, 0].add(lam_bar * lax.complex(x_re.astype(f32), x_im.astype(f32)))
    _, xs = lax.associative_scan(ssm_combine, (jnp.broadcast_to(lam_bar, bu.shape), bu), axis=1)
    y = jnp.einsum('ghp,btgp->btgh', c_mat, xs).real + d.astype(f32).reshape(S5_GROUPS, S5_GROUP_WIDTH) * uf
    zz = jax.nn.gelu(y.reshape(B, T, D_S5))
    out = zz * jax.nn.sigmoid(zz @ glu_w.astype(f32) + glu_b.astype(f32))
    return out, xs[:, -1].real, xs[:, -1].imag


def conv_ffn(x, conv_state, w_up, conv_w, conv_b, w_down):
    T = x.shape[1]
    up = x @ w_up
    ext = jnp.concatenate([conv_state.astype(up.dtype), up], axis=1)
    conv = conv_b + sum(ext[:, i:i + T] * conv_w[i] for i in range(CONV_W))
    a, g = jnp.split(conv, 2, axis=-1)
    return (jax.nn.silu(a) * g) @ w_down, ext[:, ext.shape[1] - (CONV_W - 1):]


def layer(x, lw, tb, past):
    f32 = jnp.float32
    B, T, _ = x.shape
    z = x @ lw['w_in']
    mq, mk, mv, mo, mi, mf, nq, nkv, ng, su = jnp.split(z, SPLIT_POINTS, axis=-1)
    if past is None:
        pos0 = 0
        c0 = jnp.zeros((B, MLSTM_HEADS, MLSTM_HEAD_DIM, MLSTM_HEAD_DIM), f32)
        n0 = jnp.zeros((B, MLSTM_HEADS, MLSTM_HEAD_DIM), f32)
        m0 = jnp.zeros((B, MLSTM_HEADS), f32)
        s_re = jnp.zeros((B, S5_GROUPS, S5_STATE), f32)
        s_im = jnp.zeros((B, S5_GROUPS, S5_STATE), f32)
        conv0 = jnp.zeros((B, CONV_W - 1, 2 * D_FF), x.dtype)
    else:
        pos0 = past['nsa_kv'].shape[1]
        c0, n0, m0 = past['mlstm']
        s_re, s_im = past['s5']
        conv0 = past['conv']
    q_pos = pos0 + jnp.arange(T)

    hd = (B, T, MLSTM_HEADS, MLSTM_HEAD_DIM)
    ig = mi.astype(f32) + lw['mlstm_gate_b'][0]
    lf = jax.nn.log_sigmoid(mf.astype(f32) + lw['mlstm_gate_b'][1])
    h, c1, n1, m1 = mlstm_mix(mq.reshape(hd), mk.reshape(hd) * MLSTM_HEAD_DIM ** -0.5, mv.reshape(hd), ig, lf, c0, n0, m0)
    mu = h.mean(-1, keepdims=True)
    var = jnp.square(h - mu).mean(-1, keepdims=True)
    hn = (h - mu) * lax.rsqrt(var + LN_EPS) * lw['mlstm_norm_w'].reshape(MLSTM_HEADS, MLSTM_HEAD_DIM)
    o_mlstm = (jax.nn.sigmoid(mo.astype(f32)) * hn.reshape(B, T, D_MLSTM)).astype(x.dtype)

    q = nq.reshape(B, T, NSA_HEADS, NSA_HEAD_DIM) * NSA_HEAD_DIM ** -0.5
    kv = nkv.reshape(B, T, N_KV_SLOTS, NSA_KV_HEADS, NSA_HEAD_DIM)
    if past is None:
        kv4 = kv[:, :, :4]
        o_w = window_prompt(q, kv[:, :, 4], kv[:, :, 5], tb)
        win_rows = kv[:, T - min(WINDOW, T):, 4:]
    else:
        kv4 = jnp.concatenate([past['nsa_kv'].astype(kv.dtype), kv[:, :, :4]], axis=1)
        wkv = jnp.concatenate([past['win_kv'].astype(kv.dtype), kv[:, :, 4:]], axis=1)
        wb = past['win_kv'].shape[1]
        k_pos = pos0 - wb + jnp.arange(wb + T)
        o_w = window_attend(q[:, None], wkv[:, None, :, 0], wkv[:, None, :, 1], q_pos[None], k_pos[None], tb)[:, 0]
        win_rows = kv[:, :, 4:]
    o_c, o_s = nsa_long_branches(q, q_pos, kv4, lw['cmp_w1'], lw['cmp_b1'], lw['cmp_w2'], tb)
    gates = jax.nn.sigmoid(ng.reshape(B, T, NSA_HEADS, 3).astype(f32) + lw['nsa_gate_b'])
    o_nsa = gates[..., 0:1] * o_c + gates[..., 1:2] * o_s + gates[..., 2:3] * o_w
    o_nsa = o_nsa.reshape(B, T, D_NSA).astype(x.dtype)

    o_s5, s_re1, s_im1 = s5_mix(su, s_re, s_im, lw['s5_a_re'], lw['s5_a_im'], lw['s5_b_re'], lw['s5_b_im'],
                                lw['s5_c_re'], lw['s5_c_im'], lw['s5_d'], lw['s5_log_step'], lw['s5_glu_w'], lw['s5_glu_b'])

    mix = jnp.concatenate([o_mlstm, o_nsa, o_s5.astype(x.dtype)], axis=-1) @ lw['w_out']
    x = layer_norm(DEEPNORM_ALPHA * x + mix, lw['ln1_w'], lw['ln1_b'])
    f, conv1 = conv_ffn(x, conv0, lw['ffn_w_up'], lw['ffn_conv_w'], lw['ffn_conv_b'], lw['ffn_w_down'])
    x = layer_norm(DEEPNORM_ALPHA * x + f, lw['ln2_w'], lw['ln2_b'])
    return x, (kv[:, :, :4], win_rows, c1, n1, m1, s_re1, s_im1, conv1)


def setup_inputs(seed: int = 0) -> dict:
    key = jax.random.key(seed)
    keys = iter(jax.random.split(key, 64))

    def nrm(shape, scale=1.0):
        return jax.random.normal(next(keys), shape, jnp.float32) * scale

    n_pages = PAST_LEN // PAGE_SIZE
    n_used = DEC_BATCH * n_pages
    n_pool = n_used + n_used // 4
    win_buf = min(WINDOW, PAST_LEN)
    KVH, HD = NSA_KV_HEADS, NSA_HEAD_DIM
    MH, MD = MLSTM_HEADS, MLSTM_HEAD_DIM
    page_table = jax.random.permutation(next(keys), n_pool)[:n_used].reshape(DEC_BATCH, n_pages).astype(jnp.int32)
    gate_b = jnp.stack([nrm((DEPTH, MH), 0.1),
                        jnp.linspace(3.0, 6.0, MH, dtype=jnp.float32) + nrm((DEPTH, MH), 0.1)], axis=1)
    a_re = -0.5 * jnp.exp(nrm((DEPTH, S5_GROUPS, S5_STATE), 0.02))
    a_im = jnp.pi * jnp.arange(S5_STATE, dtype=jnp.float32) + nrm((DEPTH, S5_GROUPS, S5_STATE), 0.01)
    log_step = jax.random.uniform(next(keys), (DEPTH, S5_GROUPS), jnp.float32, math.log(1e-3), math.log(1e-1))
    return {
        'x_prompt': nrm((BATCH, SEQ, D_MODEL)),
        'x_sample': nrm((DEC_BATCH, DEC_SEQ, D_MODEL)),
        'cache_nsa_kv': nrm((DEPTH, n_pool, PAGE_SIZE, 4, KVH, HD)),
        'cache_win_kv': nrm((DEPTH, DEC_BATCH, win_buf, 2, KVH, HD)),
        'state_mlstm_c': nrm((DEPTH, DEC_BATCH, MH, MD, MD), 0.1),
        'state_mlstm_n': nrm((DEPTH, DEC_BATCH, MH, MD), 0.5),
        'state_mlstm_m': nrm((DEPTH, DEC_BATCH, MH)),
        'state_s5_re': nrm((DEPTH, DEC_BATCH, S5_GROUPS, S5_STATE), 0.1),
        'state_s5_im': nrm((DEPTH, DEC_BATCH, S5_GROUPS, S5_STATE), 0.1),
        'state_ffn_conv': nrm((DEPTH, DEC_BATCH, CONV_W - 1, 2 * D_FF)),
        'page_table': page_table,
        'w_in': nrm((DEPTH, D_MODEL, IN_COLS), D_MODEL ** -0.5),
        'mlstm_gate_b': gate_b,
        'mlstm_norm_w': 1.0 + nrm((DEPTH, D_MLSTM), 0.02),
        'nsa_gate_b': nrm((DEPTH, NSA_HEADS, 3), 0.02),
        'cmp_w1': nrm((DEPTH, 2, CMP_BLOCK, HD, HD), (CMP_BLOCK * HD) ** -0.5),
        'cmp_b1': nrm((DEPTH, 2, HD), 0.02),
        'cmp_w2': nrm((DEPTH, 2, HD, HD), HD ** -0.5),
        'rel_bias': nrm((NUM_BUCKETS, NSA_HEADS), 0.1),
        's5_a_re': a_re,
        's5_a_im': a_im,
        's5_b_re': nrm((DEPTH, S5_GROUPS, S5_STATE, S5_GROUP_WIDTH), (2 * S5_GROUP_WIDTH) ** -0.5),
        's5_b_im': nrm((DEPTH, S5_GROUPS, S5_STATE, S5_GROUP_WIDTH), (2 * S5_GROUP_WIDTH) ** -0.5),
        's5_c_re': nrm((DEPTH, S5_GROUPS, S5_GROUP_WIDTH, S5_STATE), S5_STATE ** -0.5),
        's5_c_im': nrm((DEPTH, S5_GROUPS, S5_GROUP_WIDTH, S5_STATE), S5_STATE ** -0.5),
        's5_d': nrm((DEPTH, D_S5), 0.5),
        's5_log_step': log_step,
        's5_glu_w': nrm((DEPTH, D_S5, D_S5), D_S5 ** -0.5),
        's5_glu_b': nrm((DEPTH, D_S5), 0.02),
        'w_out': nrm((DEPTH, D_MODEL, D_MODEL), D_MODEL ** -0.5 * DEEPNORM_BETA),
        'ln1_w': 1.0 + nrm((DEPTH, D_MODEL), 0.02),
        'ln1_b': nrm((DEPTH, D_MODEL), 0.02),
        'ffn_w_up': nrm((DEPTH, D_MODEL, 2 * D_FF), D_MODEL ** -0.5),
        'ffn_conv_w': nrm((DEPTH, CONV_W, 2 * D_FF), CONV_W ** -0.5),
        'ffn_conv_b': nrm((DEPTH, 2 * D_FF), 0.02),
        'ffn_w_down': nrm((DEPTH, D_FF, D_MODEL), D_FF ** -0.5 * DEEPNORM_BETA),
        'ln2_w': 1.0 + nrm((DEPTH, D_MODEL), 0.02),
        'ln2_b': nrm((DEPTH, D_MODEL), 0.02),
    }


def reference(x_prompt, x_sample, cache_nsa_kv, cache_win_kv, state_mlstm_c, state_mlstm_n, state_mlstm_m,
              state_s5_re, state_s5_im, state_ffn_conv, page_table, w_in, mlstm_gate_b, mlstm_norm_w,
              nsa_gate_b, cmp_w1, cmp_b1, cmp_w2, rel_bias, s5_a_re, s5_a_im, s5_b_re, s5_b_im, s5_c_re,
              s5_c_im, s5_d, s5_log_step, s5_glu_w, s5_glu_b, w_out, ln1_w, ln1_b, ffn_w_up, ffn_conv_w,
              ffn_conv_b, ffn_w_down, ln2_w, ln2_b):
    tb = rel_bias.reshape(NUM_BUCKETS, NSA_KV_HEADS, NSA_GROUP)
    n_seq = page_table.shape[0]
    xp, xs = x_prompt, x_sample
    p_st, s_st = [], []
    for l in range(DEPTH):
        lw = {
            'w_in': w_in[l], 'mlstm_gate_b': mlstm_gate_b[l], 'mlstm_norm_w': mlstm_norm_w[l],
            'nsa_gate_b': nsa_gate_b[l], 'cmp_w1': cmp_w1[l], 'cmp_b1': cmp_b1[l], 'cmp_w2': cmp_w2[l],
            's5_a_re': s5_a_re[l], 's5_a_im': s5_a_im[l], 's5_b_re': s5_b_re[l], 's5_b_im': s5_b_im[l],
            's5_c_re': s5_c_re[l], 's5_c_im': s5_c_im[l], 's5_d': s5_d[l], 's5_log_step': s5_log_step[l],
            's5_glu_w': s5_glu_w[l], 's5_glu_b': s5_glu_b[l], 'w_out': w_out[l],
            'ln1_w': ln1_w[l], 'ln1_b': ln1_b[l], 'ffn_w_up': ffn_w_up[l], 'ffn_conv_w': ffn_conv_w[l],
            'ffn_conv_b': ffn_conv_b[l], 'ffn_w_down': ffn_w_down[l], 'ln2_w': ln2_w[l], 'ln2_b': ln2_b[l],
        }
        past = {
            'nsa_kv': cache_nsa_kv[l, page_table].reshape((n_seq, -1) + cache_nsa_kv.shape[3:]),
            'win_kv': cache_win_kv[l],
            'mlstm': (state_mlstm_c[l], state_mlstm_n[l], state_mlstm_m[l]),
            's5': (state_s5_re[l], state_s5_im[l]),
            'conv': state_ffn_conv[l],
        }
        xp, sp = layer(xp, lw, tb, None)
        xs, ss = layer(xs, lw, tb, past)
        p_st.append(sp)
        s_st.append(ss)
    stk = lambda states, i: jnp.stack([s[i] for s in states])
    return (xp, xs,
            stk(p_st, 0), stk(s_st, 0), stk(p_st, 1), stk(s_st, 1),
            stk(p_st, 2), stk(s_st, 2), stk(p_st, 3), stk(s_st, 3), stk(p_st, 4), stk(s_st, 4),
            stk(p_st, 5), stk(s_st, 5), stk(p_st, 6), stk(s_st, 6), stk(p_st, 7), stk(s_st, 7))
```

```python
import functools
import math

import jax
import jax.numpy as jnp
from jax import lax
from jax.experimental import pallas as pl
from jax.experimental.pallas import tpu as pltpu

F32 = jnp.float32
BF16 = jnp.bfloat16
NEG_INF = float("-inf")
M_INIT = -1e30

D_MODEL = 2048
PAGE_SIZE = 128
D_MLSTM = D_MODEL // 4
D_NSA = D_MODEL // 2
D_S5 = D_MODEL - D_MLSTM - D_NSA
HEAD_DIM = 128
MLSTM_HEADS = D_MLSTM // HEAD_DIM
NSA_HEADS = D_NSA // HEAD_DIM
KV_HEADS = 2
GROUP = NSA_HEADS // KV_HEADS
N_KV_SLOTS = 6
CMP_BLOCK = 32
CMP_STRIDE = 16
SEL_BLOCK = 64
SEL_TOPK = 16
FORCE_SCORE = 1e4
WINDOW = 512
S5_GROUP_WIDTH = 16
S5_GROUPS = D_S5 // S5_GROUP_WIDTH
S5_STATE = 64
S5_CH = S5_GROUPS * S5_STATE
NUM_BUCKETS = 32
MAX_DISTANCE = 128
D_FF = ((8 * D_MODEL // 3 + 127) // 128) * 128
CONV_W = 3
LN_EPS = 1e-5
DEPTH = 2
DEEPNORM_ALPHA = (2 * DEPTH) ** 0.25
QK_SCALE = HEAD_DIM ** -0.5

LANES = 128
SUBLANES = 8
V7X_VMEM_LIMIT = 56 * 2 ** 20

C_MQ, C_MK, C_MV, C_MO = 0, 512, 1024, 1536
C_NQ = 2048
C_NKV = 3072
C_SU = 4608
C_GATE = 5120
N_IN = 5376
GATE_MI, GATE_MF, GATE_NG = 0, 4, 8
D_FFP = 5632
MLSTM_L = 256
TQ = 128
TK = 256
N_TZ = 4
S5_LC = 64
CMP_PAGES = 32
SEL_PAGES = 8


def _cparams(*sem):
    return pltpu.CompilerParams(dimension_semantics=sem, vmem_limit_bytes=V7X_VMEM_LIMIT)


def _dot(a, b):
    return jnp.dot(a, b, preferred_element_type=F32)


def _dot_nt(a, b):
    return lax.dot_general(a, b, (((1,), (1,)), ((), ())), preferred_element_type=F32)


def _layer_norm(y, w, b):
    mu = jnp.mean(y, axis=-1, keepdims=True)
    d = y - mu
    var = jnp.mean(d * d, axis=-1, keepdims=True)
    return d * lax.rsqrt(var + LN_EPS) * w + b


def _mm_kernel(x_ref, w_ref, o_ref):
    o_ref[...] = _dot(x_ref[...].astype(BF16), w_ref[...]).astype(o_ref.dtype)


def _matmul(x, w, *, tm, tn):
    m, k = x.shape
    n = w.shape[1]
    tm = min(tm, m)
    return pl.pallas_call(
        _mm_kernel,
        grid=(m // tm, n // tn),
        in_specs=[pl.BlockSpec((tm, k), lambda i, j: (i, 0)), pl.BlockSpec((k, tn), lambda i, j: (0, j))],
        out_specs=pl.BlockSpec((tm, tn), lambda i, j: (i, j)),
        out_shape=jax.ShapeDtypeStruct((m, n), F32),
        compiler_params=_cparams("parallel", "parallel"),
        name="proj",
    )(x, w)


def _wout_kernel(om_ref, on_ref, os_ref, x_ref, w_ref, lw_ref, lb_ref, o_ref):
    acc = _dot(om_ref[...].astype(BF16), w_ref[0:D_MLSTM, :])
    acc += _dot(on_ref[...].astype(BF16), w_ref[D_MLSTM:D_MLSTM + D_NSA, :])
    acc += _dot(os_ref[...].astype(BF16), w_ref[D_MLSTM + D_NSA:D_MODEL, :])
    o_ref[...] = _layer_norm(DEEPNORM_ALPHA * x_ref[...] + acc, lw_ref[...], lb_ref[...])


def _wout_ln(om, on, os_, x, w, lw, lb, *, tm):
    m = x.shape[0]
    tm = min(tm, m)
    row = lambda width: pl.BlockSpec((tm, width), lambda i: (i, 0))
    full = lambda a: pl.BlockSpec(a.shape, lambda i: (0, 0))
    return pl.pallas_call(
        _wout_kernel,
        grid=(m // tm,),
        in_specs=[row(D_MLSTM), row(D_NSA), row(D_S5), row(D_MODEL), full(w), full(lw), full(lb)],
        out_specs=row(D_MODEL),
        out_shape=jax.ShapeDtypeStruct((m, D_MODEL), F32),
        compiler_params=_cparams("parallel"),
        name="wout_ln",
    )(om, on, os_, x, w, lw, lb)


def _down_kernel(h_ref, w_ref, x_ref, lw_ref, lb_ref, o_ref, acc_ref, *, nk):
    k = pl.program_id(1)

    @pl.when(k == 0)
    def _():
        acc_ref[...] = jnp.zeros_like(acc_ref)

    acc_ref[...] += _dot(h_ref[...], w_ref[...])

    @pl.when(k == nk - 1)
    def _():
        o_ref[...] = _layer_norm(DEEPNORM_ALPHA * x_ref[...] + acc_ref[...], lw_ref[...], lb_ref[...])


def _down_ln(h, w, x, lw, lb, *, tm, tk):
    m, kk = h.shape
    tm = min(tm, m)
    nk = kk // tk
    return pl.pallas_call(
        functools.partial(_down_kernel, nk=nk),
        grid=(m // tm, nk),
        in_specs=[
            pl.BlockSpec((tm, tk), lambda i, k: (i, k)),
            pl.BlockSpec((tk, D_MODEL), lambda i, k: (k, 0)),
            pl.BlockSpec((tm, D_MODEL), lambda i, k: (i, 0)),
            pl.BlockSpec((1, D_MODEL), lambda i, k: (0, 0)),
            pl.BlockSpec((1, D_MODEL), lambda i, k: (0, 0)),
        ],
        out_specs=pl.BlockSpec((tm, D_MODEL), lambda i, k: (i, 0)),
        out_shape=jax.ShapeDtypeStruct((m, D_MODEL), F32),
        scratch_shapes=[pltpu.VMEM((tm, D_MODEL), F32)],
        compiler_params=_cparams("parallel", "arbitrary"),
        name="down_ln",
    )(h, w, x, lw, lb)


def _convgate_kernel(a_ref, g_ref, ha_ref, hg_ref, sa_ref, sg_ref, wa_ref, wg_ref, ba_ref, bg_ref, o_ref):
    first = pl.program_id(1) == 0
    tt = a_ref.shape[0]
    rows = lax.broadcasted_iota(jnp.int32, (tt, 1), 0)

    def conv(cur_ref, halo_ref, st_ref, w_ref, b_ref):
        cur = cur_ref[...]
        prev = jnp.where(first, st_ref[...], halo_ref[...])
        p1 = prev[7:8, :]
        p2 = prev[6:7, :]
        x1 = jnp.where(rows == 0, p1, pltpu.roll(cur, 1, 0))
        x2 = jnp.where(rows == 0, p2, jnp.where(rows == 1, p1, pltpu.roll(cur, 2, 0)))
        w = w_ref[...]
        return b_ref[...] + w[0:1, :] * x2 + w[1:2, :] * x1 + w[2:3, :] * cur

    a = conv(a_ref, ha_ref, sa_ref, wa_ref, ba_ref)
    g = conv(g_ref, hg_ref, sg_ref, wg_ref, bg_ref)
    o_ref[...] = (a * jax.nn.sigmoid(a) * g).astype(o_ref.dtype)


def _convgate(up, state8, conv_w8, conv_b, *, tt, tf):
    b, t, _ = up.shape
    tt = min(tt, t)
    nf = D_FFP // tf
    hb = tt // SUBLANES
    cur_a = pl.BlockSpec((None, tt, tf), lambda bi, ti, fi: (bi, ti, fi))
    cur_g = pl.BlockSpec((None, tt, tf), lambda bi, ti, fi: (bi, ti, fi + nf))
    halo_a = pl.BlockSpec((None, SUBLANES, tf), lambda bi, ti, fi: (bi, jnp.maximum(ti * hb - 1, 0), fi))
    halo_g = pl.BlockSpec((None, SUBLANES, tf), lambda bi, ti, fi: (bi, jnp.maximum(ti * hb - 1, 0), fi + nf))
    st_a = pl.BlockSpec((None, SUBLANES, tf), lambda bi, ti, fi: (bi, 0, fi))
    st_g = pl.BlockSpec((None, SUBLANES, tf), lambda bi, ti, fi: (bi, 0, fi + nf))
    w_a = pl.BlockSpec((SUBLANES, tf), lambda bi, ti, fi: (0, fi))
    w_g = pl.BlockSpec((SUBLANES, tf), lambda bi, ti, fi: (0, fi + nf))
    b_a = pl.BlockSpec((1, tf), lambda bi, ti, fi: (0, fi))
    b_g = pl.BlockSpec((1, tf), lambda bi, ti, fi: (0, fi + nf))
    return pl.pallas_call(
        _convgate_kernel,
        grid=(b, t // tt, nf),
        in_specs=[cur_a, cur_g, halo_a, halo_g, st_a, st_g, w_a, w_g, b_a, b_g],
        out_specs=pl.BlockSpec((None, tt, tf), lambda bi, ti, fi: (bi, ti, fi)),
        out_shape=jax.ShapeDtypeStruct((b, t, D_FFP), BF16),
        compiler_params=_cparams("parallel", "parallel", "parallel"),
        name="convgate",
    )(up, up, up, up, state8, state8, conv_w8, conv_w8, conv_b, conv_b)


def _cumsum_rows(x):
    n = x.shape[0]
    rows = lax.broadcasted_iota(jnp.int32, (n, 1), 0)
    d = 1
    while d < n:
        x = x + jnp.where(rows >= d, pltpu.roll(x, d, 0), 0.0)
        d *= 2
    return x


def _log_sigmoid(x):
    return jnp.minimum(x, 0.0) - jnp.log1p(jnp.exp(-jnp.abs(x)))


def _mlstm_kernel(q_ref, k_ref, v_ref, og_ref, g_ref, gb_ref, nw_ref, c0_ref, n0_ref, m0_ref,
                  out_ref, c_ref, n_ref, m_ref, *, L, t_valid):
    ci = pl.program_id(1)
    lin = q_ref.shape[0]

    @pl.when(ci == 0)
    def _():
        c_ref[...] = c0_ref[...]
        n_ref[...] = n0_ref[...]
        m_ref[...] = m0_ref[...]

    def rows_of(ref):
        x = ref[...]
        if lin < L:
            x = jnp.concatenate([x, jnp.zeros((L - lin, x.shape[1]), x.dtype)], axis=0)
        return x

    rows = lax.broadcasted_iota(jnp.int32, (L, 1), 0)
    valid = (ci * L + rows) < t_valid
    pre = rows_of(g_ref) + gb_ref[...]
    lf = jnp.where(valid, _log_sigmoid(pre), 0.0)
    ig = jnp.where(valid, pre, NEG_INF)
    bcum = _cumsum_rows(lf)
    dt = (pltpu.roll(ig, GATE_MF - GATE_MI, 1) - bcum).T
    q_all, k_all, v_all, og_all = rows_of(q_ref), rows_of(k_ref), rows_of(v_ref), rows_of(og_ref)
    tri = lax.broadcasted_iota(jnp.int32, (L, L), 0) >= lax.broadcasted_iota(jnp.int32, (L, L), 1)
    nw = nw_ref[...]

    for h in range(MLSTM_HEADS):
        sl = slice(h * HEAD_DIM, (h + 1) * HEAD_DIM)
        q = q_all[:, sl]
        k = k_all[:, sl] * QK_SCALE
        v = v_all[:, sl]
        qb, kb, vb = q.astype(BF16), k.astype(BF16), v.astype(BF16)
        b_col = bcum[:, GATE_MF + h:GATE_MF + h + 1]
        ig_col = ig[:, GATE_MI + h:GATE_MI + h + 1]
        d_row = dt[GATE_MF + h:GATE_MF + h + 1, :]
        c_prev = c_ref[h]
        n_prev = n_ref[h:h + 1, :]
        m_prev = m_ref[h:h + 1, 0:1]

        dmat = jnp.where(tri, b_col + d_row, NEG_INF)
        g_col = b_col + m_prev
        m_row = jnp.maximum(jnp.max(dmat, axis=1, keepdims=True), g_col)
        a = jnp.exp(dmat - m_row) * _dot_nt(qb, kb)
        w_inter = jnp.exp(g_col - m_row)
        num = _dot(a.astype(BF16), vb) + w_inter * _dot(qb, c_prev.astype(BF16))
        den = jnp.sum(a, axis=1, keepdims=True) + w_inter * jnp.sum(q * n_prev, axis=1, keepdims=True)
        hid = num / jnp.maximum(jnp.abs(den), jnp.exp(-m_row))
        mu = jnp.mean(hid, axis=1, keepdims=True)
        dlt = hid - mu
        var = jnp.mean(dlt * dlt, axis=1, keepdims=True)
        hn = dlt * lax.rsqrt(var + LN_EPS) * nw[:, sl]
        res = jax.nn.sigmoid(og_all[:, sl]) * hn
        out_ref[:, sl] = res[0:lin, :]

        f_tot = b_col[L - 1:L, :]
        w_s = f_tot - b_col + ig_col
        m_new = jnp.maximum(f_tot + m_prev, jnp.max(w_s, axis=0, keepdims=True))
        ws = jnp.exp(w_s - m_new)
        decay = jnp.exp(f_tot + m_prev - m_new)
        c_ref[h] = decay * c_prev + _dot(k.T.astype(BF16), (ws * v).astype(BF16))
        n_ref[h:h + 1, :] = decay * n_prev + jnp.sum(ws * k, axis=0, keepdims=True)
        m_ref[h:h + 1, :] = jnp.broadcast_to(m_new, (1, LANES))


def _mlstm(z, gate_b, norm_w, c0, n0, m0, *, L, lin, t_valid):
    b, tz, _ = z.shape
    nchunks = tz // lin
    qblk = lambda col: pl.BlockSpec((None, lin, D_MLSTM), lambda bi, ci: (bi, ci, col // D_MLSTM))
    st4 = pl.BlockSpec((None, MLSTM_HEADS, HEAD_DIM, HEAD_DIM), lambda bi, ci: (bi, 0, 0, 0))
    st3 = pl.BlockSpec((None, MLSTM_HEADS, LANES), lambda bi, ci: (bi, 0, 0))
    return pl.pallas_call(
        functools.partial(_mlstm_kernel, L=L, t_valid=t_valid),
        grid=(b, nchunks),
        in_specs=[qblk(C_MQ), qblk(C_MK), qblk(C_MV), qblk(C_MO),
                  pl.BlockSpec((None, lin, LANES), lambda bi, ci: (bi, ci, C_GATE // LANES)),
                  pl.BlockSpec((1, LANES), lambda bi, ci: (0, 0)),
                  pl.BlockSpec((1, D_MLSTM), lambda bi, ci: (0, 0)),
                  st4, st3, st3],
        out_specs=[pl.BlockSpec((None, lin, D_MLSTM), lambda bi, ci: (bi, ci, 0)), st4, st3, st3],
        out_shape=[jax.ShapeDtypeStruct((b, tz, D_MLSTM), F32),
                   jax.ShapeDtypeStruct((b, MLSTM_HEADS, HEAD_DIM, HEAD_DIM), F32),
                   jax.ShapeDtypeStruct((b, MLSTM_HEADS, LANES), F32),
                   jax.ShapeDtypeStruct((b, MLSTM_HEADS, LANES), F32)],
        compiler_params=_cparams("parallel", "arbitrary"),
        name="mlstm",
    )(z, z, z, z, z, gate_b, norm_w, c0, n0, m0)


def _s5_kernel(u_ref, wb_ref, wc_ref, lr_ref, li_ref, d_ref, gw_ref, gb_ref, x0r_ref, x0i_ref,
               o_ref, xr_ref, xi_ref, sr_ref, si_ref, *, lc):
    @pl.when(pl.program_id(0) == 0)
    def _():
        xr_ref[...] = x0r_ref[...]
        xi_ref[...] = x0i_ref[...]

    u = u_ref[...]
    bu = _dot(u.astype(BF16), wb_ref[...])
    sr_ref[...] = bu[:, 0:S5_CH]
    si_ref[...] = bu[:, S5_CH:2 * S5_CH]
    lam_r = jnp.broadcast_to(lr_ref[...], (SUBLANES, S5_CH))
    lam_i = jnp.broadcast_to(li_ref[...], (SUBLANES, S5_CH))

    def step(t, carry):
        xr, xi = carry
        r0 = pl.multiple_of(t * SUBLANES, SUBLANES)
        nr = lam_r * xr - lam_i * xi + sr_ref[pl.ds(r0, SUBLANES), :]
        ni = lam_r * xi + lam_i * xr + si_ref[pl.ds(r0, SUBLANES), :]
        sr_ref[pl.ds(r0, SUBLANES), :] = nr
        si_ref[pl.ds(r0, SUBLANES), :] = ni
        return nr, ni

    xr, xi = lax.fori_loop(0, lc, step, (xr_ref[...], xi_ref[...]))
    xr_ref[...] = xr
    xi_ref[...] = xi
    y = _dot(sr_ref[...].astype(BF16), wc_ref[0:S5_CH, :]) + _dot(si_ref[...].astype(BF16), wc_ref[S5_CH:2 * S5_CH, :])
    y = y + d_ref[...] * u
    zz = jax.nn.gelu(y)
    o_ref[...] = zz * jax.nn.sigmoid(_dot(zz.astype(BF16), gw_ref[...]) + gb_ref[...])


def _s5(u_tm, wb, wc, lam_r, lam_i, d, glu_w, glu_b, x0r, x0i, *, lc):
    rows = u_tm.shape[0]
    t = rows // SUBLANES
    lc = min(lc, t)
    full = lambda a: pl.BlockSpec(a.shape, lambda i: (0,) * a.ndim)
    st = pl.BlockSpec((SUBLANES, S5_CH), lambda i: (0, 0))
    return pl.pallas_call(
        functools.partial(_s5_kernel, lc=lc),
        grid=(t // lc,),
        in_specs=[pl.BlockSpec((lc * SUBLANES, D_S5), lambda i: (i, 0)),
                  full(wb), full(wc), full(lam_r), full(lam_i), full(d), full(glu_w), full(glu_b), st, st],
        out_specs=[pl.BlockSpec((lc * SUBLANES, D_S5), lambda i: (i, 0)), st, st],
        out_shape=[jax.ShapeDtypeStruct((rows, D_S5), F32),
                   jax.ShapeDtypeStruct((SUBLANES, S5_CH), F32),
                   jax.ShapeDtypeStruct((SUBLANES, S5_CH), F32)],
        scratch_shapes=[pltpu.VMEM((lc * SUBLANES, S5_CH), F32), pltpu.VMEM((lc * SUBLANES, S5_CH), F32)],
        compiler_params=_cparams("arbitrary"),
        name="s5",
    )(u_tm, wb, wc, lam_r, lam_i, d, glu_w, glu_b, x0r, x0i)


N_CMB = 2 * KV_HEADS


def _cmp_project(rows_of, w_ref, o_ref):
    nch = o_ref.shape[0]
    for cmb in range(N_CMB):
        slot = cmb // KV_HEADS
        rows_ref = rows_of(cmb)
        acc = jnp.zeros((nch, 2 * HEAD_DIM), F32)
        for l in range(CMP_STRIDE):
            x = rows_ref[pl.ds(l, nch, stride=CMP_STRIDE), :]
            acc += _dot(x.astype(BF16), w_ref[slot, l])
        o_ref[:, cmb * 2 * HEAD_DIM:(cmb + 1) * 2 * HEAD_DIM] = acc


def _cmp_p_kernel(r0_ref, r1_ref, r2_ref, r3_ref, w_ref, o_ref):
    rows = (r0_ref, r1_ref, r2_ref, r3_ref)
    _cmp_project(lambda cmb: rows[cmb], w_ref, o_ref)


def _cmp_project_prompt(z, wcat):
    b, t, _ = z.shape
    nch = t // CMP_STRIDE
    rows = lambda cmb: pl.BlockSpec((None, t, HEAD_DIM), lambda bi: (bi, 0, C_NKV // HEAD_DIM + cmb))
    return pl.pallas_call(
        _cmp_p_kernel,
        grid=(b,),
        in_specs=[rows(cmb) for cmb in range(N_CMB)] + [pl.BlockSpec(wcat.shape, lambda bi: (0, 0, 0, 0))],
        out_specs=pl.BlockSpec((None, nch, N_CMB * 2 * HEAD_DIM), lambda bi: (bi, 0, 0)),
        out_shape=jax.ShapeDtypeStruct((b, nch, N_CMB * 2 * HEAD_DIM), F32),
        compiler_params=_cparams("parallel"),
        name="cmp_project_prompt",
    )(z, z, z, z, wcat)


def _cmp_s_kernel(pt_ref, *refs):
    del pt_ref
    page_refs = refs[:CMP_PAGES]
    w_ref, o_ref, rows_ref = refs[CMP_PAGES:]
    for p, ref in enumerate(page_refs):
        for cmb in range(N_CMB):
            rows_ref[cmb, p * PAGE_SIZE:(p + 1) * PAGE_SIZE, :] = ref[:, cmb * HEAD_DIM:(cmb + 1) * HEAD_DIM]
    _cmp_project(lambda cmb: rows_ref.at[cmb], w_ref, o_ref)


def _cmp_project_sample(cache2d, page_table, layer, wcat):
    b, npages = page_table.shape
    steps = npages // CMP_PAGES
    nch = CMP_PAGES * PAGE_SIZE // CMP_STRIDE
    width = N_CMB * HEAD_DIM

    def page_spec(p):
        return pl.BlockSpec((None, None, PAGE_SIZE, width),
                            lambda bi, si, pt: (layer, pt[bi, si * CMP_PAGES + p], 0, 0))

    grid_spec = pltpu.PrefetchScalarGridSpec(
        num_scalar_prefetch=1,
        grid=(b, steps),
        in_specs=[page_spec(p) for p in range(CMP_PAGES)] + [pl.BlockSpec(wcat.shape, lambda bi, si, pt: (0, 0, 0, 0))],
        out_specs=pl.BlockSpec((None, nch, N_CMB * 2 * HEAD_DIM), lambda bi, si, pt: (bi, si, 0)),
        scratch_shapes=[pltpu.VMEM((N_CMB, CMP_PAGES * PAGE_SIZE, HEAD_DIM), F32)],
    )
    return pl.pallas_call(
        _cmp_s_kernel,
        grid_spec=grid_spec,
        out_shape=jax.ShapeDtypeStruct((b, steps * nch, N_CMB * 2 * HEAD_DIM), F32),
        compiler_params=_cparams("parallel", "arbitrary"),
        name="cmp_project_sample",
    )(page_table, *([cache2d] * CMP_PAGES), wcat)


def _cmp_fin_kernel(p_ref, b1_ref, w2_ref, o_ref):
    nch = p_ref.shape[0]
    for cmb in range(N_CMB):
        slot = cmb // KV_HEADS
        c0 = cmb * 2 * HEAD_DIM
        first = p_ref[:, c0:c0 + HEAD_DIM]
        second = pltpu.roll(p_ref[:, c0 + HEAD_DIM:c0 + 2 * HEAD_DIM], nch - 1, 0)
        hid = b1_ref[slot:slot + 1, :] + first + second
        o_ref[cmb] = _dot(jax.nn.gelu(hid).astype(BF16), w2_ref[slot])


def _cmp_finish(p, b1, w2):
    b, nch, _ = p.shape
    return pl.pallas_call(
        _cmp_fin_kernel,
        grid=(b,),
        in_specs=[pl.BlockSpec((None, nch, p.shape[2]), lambda bi: (bi, 0, 0)),
                  pl.BlockSpec(b1.shape, lambda bi: (0, 0)),
                  pl.BlockSpec(w2.shape, lambda bi: (0, 0, 0))],
        out_specs=pl.BlockSpec((None, N_CMB, nch, HEAD_DIM), lambda bi: (bi, 0, 0, 0)),
        out_shape=jax.ShapeDtypeStruct((b, N_CMB, nch, HEAD_DIM), F32),
        compiler_params=_cparams("parallel"),
        name="cmp_finish",
    )(p, b1, w2)


def _softmax_init(m_scr, l_scr, acc_scr):
    m_scr[...] = jnp.full(m_scr.shape, M_INIT, F32)
    l_scr[...] = jnp.zeros(l_scr.shape, F32)
    acc_scr[...] = jnp.zeros(acc_scr.shape, F32)


def _softmax_update(s, mask, vb, m_scr, l_scr, acc_scr):
    s = jnp.where(mask, s, NEG_INF)
    m_prev = m_scr[...]
    m_new = jnp.maximum(m_prev, jnp.max(s, axis=1, keepdims=True))
    alpha = jnp.exp(m_prev - m_new)
    p = jnp.exp(s - m_new)
    l_scr[...] = alpha * l_scr[...] + jnp.sum(p, axis=1, keepdims=True)
    acc_scr[...] = alpha * acc_scr[...] + _dot(p.astype(BF16), vb)
    m_scr[...] = m_new


def _softmax_result(l_scr, acc_scr):
    l = l_scr[...]
    return acc_scr[...] / jnp.where(l > 0, l, 1.0)


def _masked_probs(s, mask):
    s = jnp.where(mask, s, NEG_INF)
    m = jnp.max(s, axis=1, keepdims=True)
    m = jnp.where(m > NEG_INF, m, 0.0)
    p = jnp.exp(s - m)
    den = jnp.sum(p, axis=1, keepdims=True)
    return p / jnp.where(den > 0, den, 1.0)


def _stack_heads(q):
    return jnp.concatenate([q[:, j * HEAD_DIM:(j + 1) * HEAD_DIM] for j in range(GROUP)], axis=0)


def _select_blocks(pc, cur, n_cmp, n_sel):
    ncp = pc.shape[1]
    nsp = -(-n_sel // LANES) * LANES
    c_start = lax.broadcasted_iota(jnp.int32, (ncp, nsp), 0) * CMP_STRIDE
    s_start = lax.broadcasted_iota(jnp.int32, (ncp, nsp), 1) * SEL_BLOCK
    overlap = ((c_start < s_start + SEL_BLOCK) & (c_start + CMP_BLOCK > s_start)
               & (c_start < n_cmp * CMP_STRIDE)).astype(F32)
    imp = jnp.dot(pc, overlap, preferred_element_type=F32, precision=lax.Precision.HIGHEST)
    blk = lax.broadcasted_iota(jnp.int32, (1, nsp), 1)
    forced = (blk == 0) | (blk == cur) | (blk == cur - 1)
    imp = jnp.where(forced, FORCE_SCORE, imp)
    imp = jnp.where(blk > cur, -1.0, imp)

    def body(sp, count):
        col = jnp.sum(jnp.where(blk == sp, imp, 0.0), axis=1, keepdims=True)
        ahead = (col > imp) | ((col == imp) & (sp < blk))
        return count + ahead.astype(F32)

    count = lax.fori_loop(0, n_sel, body, jnp.zeros(imp.shape, F32))
    return ((count < min(SEL_TOPK, n_sel)) & (blk < n_sel)).astype(F32)


def _gate(gt, col):
    lane = lax.broadcasted_iota(jnp.int32, (1, LANES), 1)
    return jax.nn.sigmoid(jnp.sum(jnp.where(lane == col, gt, 0.0), axis=1, keepdims=True))


def _nsa_prompt_kernel(q_ref, ks_ref, vs_ref, kw_ref, vw_ref, kc_ref, vc_ref, bc_ref, tz_ref, gt_ref, gb_ref,
                       o_ref, m_scr, l_scr, acc_scr, *, t_len):
    g = pl.program_id(1)
    qt = pl.program_id(2)
    q0 = qt * TQ
    n_cmp = t_len // CMP_STRIDE - (CMP_BLOCK // CMP_STRIDE) + 1
    n_sel = t_len // SEL_BLOCK
    ncp = kc_ref.shape[0]
    rows4 = GROUP * TQ
    qs = _stack_heads(q_ref[...] * QK_SCALE).astype(BF16)
    qpos = q0 + (lax.broadcasted_iota(jnp.int32, (rows4, 1), 0) & (TQ - 1))

    n_idx = lax.broadcasted_iota(jnp.int32, (1, ncp), 1)
    s_c = _dot_nt(qs, kc_ref[...].astype(BF16)) + bc_ref[...].reshape(rows4, ncp)
    cmask = (qpos - (n_idx * CMP_STRIDE + CMP_BLOCK - 1) >= 0) & (n_idx < n_cmp)
    p_c = _masked_probs(s_c, cmask)
    o_c = _dot(p_c.astype(BF16), vc_ref[...].astype(BF16))
    pc = p_c[0:TQ] + p_c[TQ:2 * TQ] + p_c[2 * TQ:3 * TQ] + p_c[3 * TQ:4 * TQ]
    cur = jnp.right_shift(qpos[0:TQ], 6)
    sel = _select_blocks(pc, cur, n_cmp, n_sel).astype(BF16)
    sel4 = jnp.concatenate([sel] * GROUP, axis=0)

    kt_hi = (q0 + TQ - 1) // TK + 1
    kidx = lax.broadcasted_iota(jnp.int32, (1, TK), 1)
    srow = lax.broadcasted_iota(jnp.int32, (LANES, TK), 0)

    def tile_bias(k0):
        idx = jnp.clip((q0 - k0) // TQ, 0, N_TZ - 1)
        return tz_ref[idx].reshape(rows4, TK)

    _softmax_init(m_scr, l_scr, acc_scr)

    def sel_body(kt, carry):
        k0 = pl.multiple_of(kt * TK, TK)
        kpos = k0 + kidx
        s = _dot_nt(qs, ks_ref[pl.ds(k0, TK), :].astype(BF16)) + tile_bias(k0)
        expand = (jnp.right_shift(k0 + lax.broadcasted_iota(jnp.int32, (LANES, TK), 1), 6) == srow).astype(BF16)
        chosen = _dot(sel4, expand) > 0.5
        _softmax_update(s, chosen & (kpos <= qpos), vs_ref[pl.ds(k0, TK), :].astype(BF16), m_scr, l_scr, acc_scr)
        return carry

    lax.fori_loop(0, kt_hi, sel_body, 0)
    o_s = _softmax_result(l_scr, acc_scr)

    _softmax_init(m_scr, l_scr, acc_scr)

    def win_body(kt, carry):
        k0 = pl.multiple_of(kt * TK, TK)
        dist = qpos - (k0 + kidx)
        s = _dot_nt(qs, kw_ref[pl.ds(k0, TK), :].astype(BF16)) + tile_bias(k0)
        _softmax_update(s, (dist >= 0) & (dist <= WINDOW), vw_ref[pl.ds(k0, TK), :].astype(BF16), m_scr, l_scr, acc_scr)
        return carry

    lax.fori_loop(jnp.maximum(q0 - WINDOW, 0) // TK, kt_hi, win_body, 0)
    o_w = _softmax_result(l_scr, acc_scr)

    gt = gt_ref[...] + gb_ref[...]
    for j in range(GROUP):
        col = GATE_NG + (g * GROUP + j) * 3
        r = slice(j * TQ, (j + 1) * TQ)
        o_ref[:, j * HEAD_DIM:(j + 1) * HEAD_DIM] = (
            _gate(gt, col) * o_c[r] + _gate(gt, col + 1) * o_s[r] + _gate(gt, col + 2) * o_w[r])


def _nsa_prompt(z, kcvc, bias_c, tz, gate_b):
    b, t, _ = z.shape
    ncp = kcvc.shape[2]
    nq = t // TQ
    kv = lambda slot: pl.BlockSpec((None, t, HEAD_DIM), lambda bi, g, qi: (bi, 0, C_NKV // HEAD_DIM + slot * KV_HEADS + g))
    cmp_blk = lambda slot: pl.BlockSpec((None, None, ncp, HEAD_DIM), lambda bi, g, qi: (bi, slot * KV_HEADS + g, 0, 0))
    return pl.pallas_call(
        functools.partial(_nsa_prompt_kernel, t_len=t),
        grid=(b, KV_HEADS, nq),
        in_specs=[pl.BlockSpec((None, TQ, GROUP * HEAD_DIM), lambda bi, g, qi: (bi, qi, C_NQ // (GROUP * HEAD_DIM) + g)),
                  kv(2), kv(3), kv(4), kv(5), cmp_blk(0), cmp_blk(1),
                  pl.BlockSpec((GROUP, TQ, ncp), lambda bi, g, qi: (g, qi, 0)),
                  pl.BlockSpec((N_TZ, GROUP, TQ, TK), lambda bi, g, qi: (0, g, 0, 0)),
                  pl.BlockSpec((None, TQ, LANES), lambda bi, g, qi: (bi, qi, C_GATE // LANES)),
                  pl.BlockSpec((1, LANES), lambda bi, g, qi: (0, 0))],
        out_specs=pl.BlockSpec((None, TQ, GROUP * HEAD_DIM), lambda bi, g, qi: (bi, qi, g)),
        out_shape=jax.ShapeDtypeStruct((b, t, D_NSA), F32),
        scratch_shapes=[pltpu.VMEM((GROUP * TQ, 1), F32), pltpu.VMEM((GROUP * TQ, 1), F32),
                        pltpu.VMEM((GROUP * TQ, HEAD_DIM), F32)],
        compiler_params=_cparams("parallel", "parallel", "arbitrary"),
        name="nsa_prompt",
    )(z, z, z, z, z, kcvc, kcvc, bias_c, tz, z, gate_b)


TS = SUBLANES
ROWS_S = GROUP * TS


def _nsa_s_cmp_kernel(q_ref, kc_ref, vc_ref, bc_ref, oc_ref, sel_ref, *, pos0, n_cmp, n_sel):
    nch = kc_ref.shape[0]
    qs = _stack_heads(q_ref[...] * QK_SCALE).astype(BF16)
    qpos = pos0 + (lax.broadcasted_iota(jnp.int32, (ROWS_S, 1), 0) & (TS - 1))
    n_idx = lax.broadcasted_iota(jnp.int32, (1, nch), 1)
    s_c = _dot_nt(qs, kc_ref[...].astype(BF16)) + bc_ref[...].reshape(ROWS_S, nch)
    cmask = (qpos - (n_idx * CMP_STRIDE + CMP_BLOCK - 1) >= 0) & (n_idx < n_cmp)
    p_c = _masked_probs(s_c, cmask)
    oc_ref[...] = _dot(p_c.astype(BF16), vc_ref[...].astype(BF16))
    pc = p_c[0:TS] + p_c[TS:2 * TS] + p_c[2 * TS:3 * TS] + p_c[3 * TS:4 * TS]
    sel_ref[...] = _select_blocks(pc, jnp.right_shift(qpos[0:TS], 6), n_cmp, n_sel)


def _nsa_sample_cmp(z8, kcvc, bias_c, *, pos0, n_cmp, n_sel):
    b = z8.shape[0]
    nch = kcvc.shape[2]
    nsp = -(-n_sel // LANES) * LANES
    cmp_blk = lambda slot: pl.BlockSpec((None, None, nch, HEAD_DIM), lambda bi, g: (bi, slot * KV_HEADS + g, 0, 0))
    return pl.pallas_call(
        functools.partial(_nsa_s_cmp_kernel, pos0=pos0, n_cmp=n_cmp, n_sel=n_sel),
        grid=(b, KV_HEADS),
        in_specs=[pl.BlockSpec((None, TS, GROUP * HEAD_DIM), lambda bi, g: (bi, 0, C_NQ // (GROUP * HEAD_DIM) + g)),
                  cmp_blk(0), cmp_blk(1),
                  pl.BlockSpec((GROUP, TS, nch), lambda bi, g: (g, 0, 0))],
        out_specs=[pl.BlockSpec((None, None, ROWS_S, HEAD_DIM), lambda bi, g: (bi, g, 0, 0)),
                   pl.BlockSpec((None, None, TS, nsp), lambda bi, g: (bi, g, 0, 0))],
        out_shape=[jax.ShapeDtypeStruct((b, KV_HEADS, ROWS_S, HEAD_DIM), F32),
                   jax.ShapeDtypeStruct((b, KV_HEADS, TS, nsp), F32)],
        compiler_params=_cparams("parallel", "parallel"),
        name="nsa_sample_cmp",
    )(z8, kcvc, kcvc, bias_c)


def _pad_rows(x, n):
    return jnp.concatenate([x, jnp.zeros((n - x.shape[0], x.shape[1]), x.dtype)], axis=0)


def _nsa_s_sel_kernel(pt_ref, q_ref, sel_ref, kn_ref, vn_ref, bt_ref, bf_ref, *refs, pos0, npages):
    del pt_ref
    k_refs = refs[:SEL_PAGES]
    v_refs = refs[SEL_PAGES:2 * SEL_PAGES]
    o_ref, m_scr, l_scr, acc_scr = refs[2 * SEL_PAGES:]
    step = pl.program_id(2)
    nsteps = npages // SEL_PAGES

    @pl.when(step == 0)
    def _():
        _softmax_init(m_scr, l_scr, acc_scr)

    qs = _stack_heads(q_ref[...] * QK_SCALE).astype(BF16)
    qpos = pos0 + (lax.broadcasted_iota(jnp.int32, (ROWS_S, 1), 0) & (TS - 1))
    sel = sel_ref[...]
    sel4 = jnp.concatenate([sel] * GROUP, axis=0)
    blk = lax.broadcasted_iota(jnp.int32, (1, sel.shape[1]), 1)
    kidx = lax.broadcasted_iota(jnp.int32, (1, PAGE_SIZE), 1)
    bias_far = bf_ref[...].reshape(ROWS_S, PAGE_SIZE)
    bias_tail = bt_ref[...].reshape(ROWS_S, 2 * PAGE_SIZE)

    def chosen_cols(first_block):
        c0 = jnp.sum(jnp.where(blk == first_block, sel4, 0.0), axis=1, keepdims=True)
        c1 = jnp.sum(jnp.where(blk == first_block + 1, sel4, 0.0), axis=1, keepdims=True)
        return jnp.where(kidx < SEL_BLOCK, c0, c1) > 0.5

    for i in range(SEL_PAGES):
        page = step * SEL_PAGES + i
        bias = jnp.where(page == npages - 1, bias_tail[:, 0:PAGE_SIZE], bias_far)
        s = _dot_nt(qs, k_refs[i][...].astype(BF16)) + bias
        kpos = page * PAGE_SIZE + kidx
        mask = chosen_cols(page * (PAGE_SIZE // SEL_BLOCK)) & (kpos <= qpos)
        _softmax_update(s, mask, v_refs[i][...].astype(BF16), m_scr, l_scr, acc_scr)

    @pl.when(step == nsteps - 1)
    def _():
        kn = _pad_rows(kn_ref[...], PAGE_SIZE).astype(BF16)
        vn = _pad_rows(vn_ref[...], PAGE_SIZE).astype(BF16)
        s = _dot_nt(qs, kn) + bias_tail[:, PAGE_SIZE:2 * PAGE_SIZE]
        kpos = pos0 + kidx
        mask = chosen_cols(npages * (PAGE_SIZE // SEL_BLOCK)) & (kpos <= qpos) & (kidx < TS)
        _softmax_update(s, mask, vn, m_scr, l_scr, acc_scr)
        o_ref[...] = _softmax_result(l_scr, acc_scr)


def _nsa_sample_sel(z8, sel, bias_tail, bias_far, cache2d, page_table, layer, *, pos0):
    b, npages = page_table.shape
    nsp = sel.shape[-1]
    steps = npages // SEL_PAGES
    zcol = lambda col: (lambda bi, g, si, pt: (bi, 0, col // HEAD_DIM + g))

    def page_spec(slot, p):
        return pl.BlockSpec((None, None, PAGE_SIZE, HEAD_DIM),
                            lambda bi, g, si, pt: (layer, pt[bi, si * SEL_PAGES + p], 0, slot * KV_HEADS + g))

    grid_spec = pltpu.PrefetchScalarGridSpec(
        num_scalar_prefetch=1,
        grid=(b, KV_HEADS, steps),
        in_specs=[pl.BlockSpec((None, TS, GROUP * HEAD_DIM), lambda bi, g, si, pt: (bi, 0, C_NQ // (GROUP * HEAD_DIM) + g)),
                  pl.BlockSpec((None, None, TS, nsp), lambda bi, g, si, pt: (bi, g, 0, 0)),
                  pl.BlockSpec((None, TS, HEAD_DIM), zcol(C_NKV + 2 * KV_HEADS * HEAD_DIM)),
                  pl.BlockSpec((None, TS, HEAD_DIM), zcol(C_NKV + 3 * KV_HEADS * HEAD_DIM)),
                  pl.BlockSpec((GROUP, TS, 2 * PAGE_SIZE), lambda bi, g, si, pt: (g, 0, 0)),
                  pl.BlockSpec((GROUP, TS, PAGE_SIZE), lambda bi, g, si, pt: (g, 0, 0))]
                 + [page_spec(2, p) for p in range(SEL_PAGES)] + [page_spec(3, p) for p in range(SEL_PAGES)],
        out_specs=pl.BlockSpec((None, None, ROWS_S, HEAD_DIM), lambda bi, g, si, pt: (bi, g, 0, 0)),
        scratch_shapes=[pltpu.VMEM((ROWS_S, 1), F32), pltpu.VMEM((ROWS_S, 1), F32), pltpu.VMEM((ROWS_S, HEAD_DIM), F32)],
    )
    return pl.pallas_call(
        functools.partial(_nsa_s_sel_kernel, pos0=pos0, npages=npages),
        grid_spec=grid_spec,
        out_shape=jax.ShapeDtypeStruct((b, KV_HEADS, ROWS_S, HEAD_DIM), F32),
        compiler_params=_cparams("parallel", "parallel", "arbitrary"),
        name="nsa_sample_sel",
    )(page_table, z8, sel, z8, z8, bias_tail, bias_far, *([cache2d] * (2 * SEL_PAGES)))


def _nsa_s_win_kernel(q_ref, kw_ref, vw_ref, kn_ref, vn_ref, bw_ref, oc_ref, os_ref, gt_ref, gb_ref, o_ref,
                      m_scr, l_scr, acc_scr, *, pos0):
    g = pl.program_id(1)
    wb = kw_ref.shape[0]
    qs = _stack_heads(q_ref[...] * QK_SCALE).astype(BF16)
    qpos = pos0 + (lax.broadcasted_iota(jnp.int32, (ROWS_S, 1), 0) & (TS - 1))
    bias = bw_ref[...].reshape(ROWS_S, wb + PAGE_SIZE)
    _softmax_init(m_scr, l_scr, acc_scr)
    dist = qpos - (pos0 - wb + lax.broadcasted_iota(jnp.int32, (1, wb), 1))
    s = _dot_nt(qs, kw_ref[...].astype(BF16)) + bias[:, 0:wb]
    _softmax_update(s, (dist >= 0) & (dist <= WINDOW), vw_ref[...].astype(BF16), m_scr, l_scr, acc_scr)
    kidx = lax.broadcasted_iota(jnp.int32, (1, PAGE_SIZE), 1)
    dist = qpos - (pos0 + kidx)
    s = _dot_nt(qs, _pad_rows(kn_ref[...], PAGE_SIZE).astype(BF16)) + bias[:, wb:wb + PAGE_SIZE]
    _softmax_update(s, (dist >= 0) & (dist <= WINDOW) & (kidx < TS), _pad_rows(vn_ref[...], PAGE_SIZE).astype(BF16),
                    m_scr, l_scr, acc_scr)
    o_w = _softmax_result(l_scr, acc_scr)
    o_c = oc_ref[...]
    o_s = os_ref[...]
    gt = gt_ref[...] + gb_ref[...]
    for j in range(GROUP):
        col = GATE_NG + (g * GROUP + j) * 3
        r = slice(j * TS, (j + 1) * TS)
        o_ref[:, j * HEAD_DIM:(j + 1) * HEAD_DIM] = (
            _gate(gt, col) * o_c[r] + _gate(gt, col + 1) * o_s[r] + _gate(gt, col + 2) * o_w[r])


def _nsa_sample_win(z8, win2d, bias_w, o_c, o_s, gate_b, *, pos0):
    b, wb, _ = win2d.shape
    zcol = lambda col: (lambda bi, g: (bi, 0, col // HEAD_DIM + g))
    part = pl.BlockSpec((None, None, ROWS_S, HEAD_DIM), lambda bi, g: (bi, g, 0, 0))
    return pl.pallas_call(
        functools.partial(_nsa_s_win_kernel, pos0=pos0),
        grid=(b, KV_HEADS),
        in_specs=[pl.BlockSpec((None, TS, GROUP * HEAD_DIM), lambda bi, g: (bi, 0, C_NQ // (GROUP * HEAD_DIM) + g)),
                  pl.BlockSpec((None, wb, HEAD_DIM), lambda bi, g: (bi, 0, g)),
                  pl.BlockSpec((None, wb, HEAD_DIM), lambda bi, g: (bi, 0, KV_HEADS + g)),
                  pl.BlockSpec((None, TS, HEAD_DIM), zcol(C_NKV + 4 * KV_HEADS * HEAD_DIM)),
                  pl.BlockSpec((None, TS, HEAD_DIM), zcol(C_NKV + 5 * KV_HEADS * HEAD_DIM)),
                  pl.BlockSpec((GROUP, TS, wb + PAGE_SIZE), lambda bi, g: (g, 0, 0)),
                  part, part,
                  pl.BlockSpec((None, TS, LANES), lambda bi, g: (bi, 0, C_GATE // LANES)),
                  pl.BlockSpec((1, LANES), lambda bi, g: (0, 0))],
        out_specs=pl.BlockSpec((None, TS, GROUP * HEAD_DIM), lambda bi, g: (bi, 0, g)),
        out_shape=jax.ShapeDtypeStruct((b, TS, D_NSA), F32),
        scratch_shapes=[pltpu.VMEM((ROWS_S, 1), F32), pltpu.VMEM((ROWS_S, 1), F32), pltpu.VMEM((ROWS_S, HEAD_DIM), F32)],
        compiler_params=_cparams("parallel", "parallel"),
        name="nsa_sample_win",
    )(z8, win2d, win2d, z8, z8, bias_w, o_c, o_s, z8, gate_b)


def _t5_bucket(dist):
    n = jnp.maximum(dist, 0)
    exact = NUM_BUCKETS // 2
    nf = jnp.maximum(n, 1).astype(F32)
    large = exact + (jnp.log(nf / exact) / math.log(MAX_DISTANCE / exact) * (NUM_BUCKETS - exact)).astype(jnp.int32)
    return jnp.where(n < exact, n, jnp.minimum(large, NUM_BUCKETS - 1))


def _bias_table(rel_bias, dist):
    return jnp.moveaxis(rel_bias[_t5_bucket(dist)], -1, 0)


def _pack_layer(p, l):
    w_in = p["w_in"][l]
    pad = jnp.zeros((D_MODEL, N_IN - C_GATE - 32), F32)
    w_in_p = jnp.concatenate([w_in[:, 0:2048], w_in[:, 2056:3080], w_in[:, 3080:4616], w_in[:, 4640:5152],
                              w_in[:, 2048:2056], w_in[:, 4616:4640], pad], axis=1).astype(BF16)
    gb = p["mlstm_gate_b"][l]
    mlstm_gate_b = jnp.zeros((1, LANES), F32).at[0, GATE_MI:GATE_MI + 4].set(gb[0]).at[0, GATE_MF:GATE_MF + 4].set(gb[1])
    nsa_gate_b = jnp.zeros((1, LANES), F32).at[0, GATE_NG:GATE_NG + 3 * NSA_HEADS].set(p["nsa_gate_b"][l].reshape(-1))
    w1 = p["cmp_w1"][l]
    wcat = jnp.concatenate([w1[:, 0:CMP_STRIDE], w1[:, CMP_STRIDE:CMP_BLOCK]], axis=-1).astype(BF16)
    lam = lax.complex(p["s5_a_re"][l], p["s5_a_im"][l])
    lam_bar = jnp.exp(lam * jnp.exp(p["s5_log_step"][l])[:, None])
    b_bar = ((lam_bar - 1.0) / lam)[..., None] * lax.complex(p["s5_b_re"][l], p["s5_b_im"][l])
    eye = jnp.eye(S5_GROUPS, dtype=F32)
    bd_in = lambda m: jnp.einsum("gph,gk->ghkp", m, eye).reshape(D_S5, S5_CH)
    wb = jnp.concatenate([bd_in(b_bar.real), bd_in(b_bar.imag)], axis=1).astype(BF16)
    bd_out = lambda m: jnp.einsum("ghp,gk->gpkh", m, eye).reshape(S5_CH, D_S5)
    wc = jnp.concatenate([bd_out(p["s5_c_re"][l]), -bd_out(p["s5_c_im"][l])], axis=0).astype(BF16)
    half = lambda w: jnp.pad(w, ((0, 0), (0, D_FFP - D_FF)))
    w_up = p["ffn_w_up"][l]
    conv_w = p["ffn_conv_w"][l]
    conv_b = p["ffn_conv_b"][l][None, :]
    return dict(
        w_in=w_in_p, mlstm_gate_b=mlstm_gate_b, mlstm_norm_w=p["mlstm_norm_w"][l][None, :], nsa_gate_b=nsa_gate_b,
        wcat=wcat, cmp_b1=p["cmp_b1"][l], cmp_w2=p["cmp_w2"][l].astype(BF16),
        s5_wb=wb, s5_wc=wc, s5_lam_r=lam_bar.real.reshape(1, S5_CH), s5_lam_i=lam_bar.imag.reshape(1, S5_CH),
        s5_d=p["s5_d"][l][None, :], s5_glu_w=p["s5_glu_w"][l].astype(BF16), s5_glu_b=p["s5_glu_b"][l][None, :],
        w_out=p["w_out"][l].astype(BF16), ln1_w=p["ln1_w"][l][None, :], ln1_b=p["ln1_b"][l][None, :],
        w_up=jnp.concatenate([half(w_up[:, :D_FF]), half(w_up[:, D_FF:])], axis=1).astype(BF16),
        conv_w8=jnp.pad(jnp.concatenate([half(conv_w[:, :D_FF]), half(conv_w[:, D_FF:])], axis=1),
                        ((0, SUBLANES - CONV_W), (0, 0))),
        conv_b=jnp.concatenate([half(conv_b[:, :D_FF]), half(conv_b[:, D_FF:])], axis=1),
        w_down=jnp.pad(p["ffn_w_down"][l], ((0, D_FFP - D_FF), (0, 0))).astype(BF16),
        ln2_w=p["ln2_w"][l][None, :], ln2_b=p["ln2_b"][l][None, :],
    )


def _unpad_ff(x):
    return jnp.concatenate([x[..., :D_FF], x[..., D_FFP:D_FFP + D_FF]], axis=-1)


def _pad_ff(x):
    pad = [(0, 0)] * (x.ndim - 1) + [(0, D_FFP - D_FF)]
    return jnp.concatenate([jnp.pad(x[..., :D_FF], pad), jnp.pad(x[..., D_FF:], pad)], axis=-1)


def _time_major(x, b, t, t_use):
    x = x.reshape(b, t, -1)[:, :t_use].transpose(1, 0, 2)
    return jnp.pad(x, ((0, 0), (0, SUBLANES - b), (0, 0))).reshape(t_use * SUBLANES, -1)


def _batch_major(x, b, t, t_use):
    x = x.reshape(t_use, SUBLANES, -1)[:, :b].transpose(1, 0, 2)
    return jnp.pad(x, ((0, 0), (0, t - t_use), (0, 0))).reshape(b * t, -1)


def _mixer_tail(x2d, z, b, t, t_use, lw, o_mlstm, o_nsa, s5_state, conv_state8, tm):
    pad8 = lambda s: jnp.pad(s.reshape(b, S5_CH), ((0, SUBLANES - b), (0, 0)))
    u_tm = _time_major(z[:, C_SU:C_SU + D_S5], b, t, t_use)
    o_s5_tm, xr, xi = _s5(u_tm, lw["s5_wb"], lw["s5_wc"], lw["s5_lam_r"], lw["s5_lam_i"], lw["s5_d"],
                          lw["s5_glu_w"], lw["s5_glu_b"], pad8(s5_state[0]), pad8(s5_state[1]), lc=S5_LC)
    o_s5 = _batch_major(o_s5_tm, b, t, t_use)
    x1 = _wout_ln(o_mlstm.reshape(b * t, D_MLSTM), o_nsa.reshape(b * t, D_NSA), o_s5, x2d,
                  lw["w_out"], lw["ln1_w"], lw["ln1_b"], tm=256)
    up = _matmul(x1, lw["w_up"], tm=tm, tn=512)
    hgate = _convgate(up.reshape(b, t, 2 * D_FFP), conv_state8, lw["conv_w8"], lw["conv_b"], tt=256, tf=512)
    x2 = _down_ln(hgate.reshape(b * t, D_FFP), lw["w_down"], x1, lw["ln2_w"], lw["ln2_b"], tm=512, tk=1408)
    s5_new = (xr[:b].reshape(b, S5_GROUPS, S5_STATE), xi[:b].reshape(b, S5_GROUPS, S5_STATE))
    conv_new = _unpad_ff(up.reshape(b, t, 2 * D_FFP)[:, t_use - (CONV_W - 1):t_use])
    return x2, s5_new, conv_new


def _prompt_layer(x2d, b, t, lw, rel_bias):
    z = _matmul(x2d, lw["w_in"], tm=1024, tn=768)
    z3 = z.reshape(b, t, N_IN)
    zeros = lambda *s: jnp.zeros(s, F32)
    o_mlstm, c1, n1, m1 = _mlstm(z3, lw["mlstm_gate_b"], lw["mlstm_norm_w"],
                                 zeros(b, MLSTM_HEADS, HEAD_DIM, HEAD_DIM), zeros(b, MLSTM_HEADS, LANES),
                                 zeros(b, MLSTM_HEADS, LANES), L=MLSTM_L, lin=MLSTM_L, t_valid=t)
    kcvc = _cmp_finish(_cmp_project_prompt(z3, lw["wcat"]), lw["cmp_b1"], lw["cmp_w2"])
    ncp = t // CMP_STRIDE
    tpos = jnp.arange(t, dtype=jnp.int32)
    bias_c = _bias_table(rel_bias, tpos[:, None] - (jnp.arange(ncp, dtype=jnp.int32) * CMP_STRIDE + CMP_BLOCK - 1)[None, :])
    ti = jnp.arange(TQ, dtype=jnp.int32)[:, None] - jnp.arange(TK, dtype=jnp.int32)[None, :]
    tz = jnp.stack([_bias_table(rel_bias, d * TQ + ti) for d in range(N_TZ)])
    o_nsa = _nsa_prompt(z3, kcvc, bias_c, tz, lw["nsa_gate_b"])
    x2, s5_new, conv_new = _mixer_tail(
        x2d, z, b, t, t, lw, o_mlstm, o_nsa, (zeros(b, S5_GROUPS, S5_STATE), zeros(b, S5_GROUPS, S5_STATE)),
        zeros(b, SUBLANES, 2 * D_FFP), tm=1024)
    nkv = z3[:, :, C_NKV:C_NKV + N_KV_SLOTS * KV_HEADS * HEAD_DIM].reshape(b, t, N_KV_SLOTS, KV_HEADS, HEAD_DIM)
    wrows = min(WINDOW, t)
    state = (nkv[:, :, :4], nkv[:, t - wrows:, 4:], c1, n1[:, :, :], m1[:, :, 0], s5_new[0], s5_new[1], conv_new)
    return x2, state


def _sample_layer(x2d, b, tn, lw, rel_bias, layer, cache2d, page_table, win2d, mlstm_state, s5_state, conv_state):
    npages = page_table.shape[1]
    pos0 = npages * PAGE_SIZE
    z = _matmul(x2d, lw["w_in"], tm=1024, tn=768)
    z8 = z.reshape(b, TS, N_IN)
    c0, n0, m0 = mlstm_state
    o_mlstm, c1, n1, m1 = _mlstm(z8, lw["mlstm_gate_b"], lw["mlstm_norm_w"], c0, n0,
                                 jnp.broadcast_to(m0[:, :, None], (b, MLSTM_HEADS, LANES)),
                                 L=LANES, lin=TS, t_valid=tn)
    n_chunks = (pos0 + tn) // CMP_STRIDE
    n_cmp = n_chunks - CMP_BLOCK // CMP_STRIDE + 1
    n_sel = -(-(pos0 + tn) // SEL_BLOCK)
    kcvc = _cmp_finish(_cmp_project_sample(cache2d, page_table, layer, lw["wcat"]), lw["cmp_b1"], lw["cmp_w2"])
    qpos = pos0 + jnp.arange(TS, dtype=jnp.int32)[:, None]
    bias_c = _bias_table(rel_bias, qpos - (jnp.arange(n_chunks, dtype=jnp.int32) * CMP_STRIDE + CMP_BLOCK - 1)[None, :])
    o_c, sel = _nsa_sample_cmp(z8, kcvc, bias_c, pos0=pos0, n_cmp=n_cmp, n_sel=n_sel)
    kk = jnp.arange(PAGE_SIZE, dtype=jnp.int32)[None, :]
    bias_tail = _bias_table(rel_bias, jnp.concatenate([qpos - (pos0 - PAGE_SIZE + kk), qpos - (pos0 + kk)], axis=1))
    bias_far = _bias_table(rel_bias, jnp.broadcast_to(qpos - (pos0 - 2 * PAGE_SIZE), (TS, PAGE_SIZE)))
    o_s = _nsa_sample_sel(z8, sel, bias_tail, bias_far, cache2d, page_table, layer, pos0=pos0)
    wb = win2d.shape[1]
    wk = jnp.arange(wb, dtype=jnp.int32)[None, :]
    bias_w = _bias_table(rel_bias, jnp.concatenate([qpos - (pos0 - wb + wk), qpos - (pos0 + kk)], axis=1))
    o_nsa = _nsa_sample_win(z8, win2d, bias_w, o_c, o_s, lw["nsa_gate_b"], pos0=pos0)
    conv_state8 = jnp.pad(_pad_ff(conv_state), ((0, 0), (SUBLANES - (CONV_W - 1), 0), (0, 0)))
    x2, s5_new, conv_new = _mixer_tail(x2d, z, b, TS, tn, lw, o_mlstm, o_nsa, s5_state, conv_state8, tm=1024)
    nkv = z8[:, :tn, C_NKV:C_NKV + N_KV_SLOTS * KV_HEADS * HEAD_DIM].reshape(b, tn, N_KV_SLOTS, KV_HEADS, HEAD_DIM)
    state = (nkv[:, :, :4], nkv[:, :, 4:], c1, n1, m1[:, :, 0], s5_new[0], s5_new[1], conv_new)
    return x2, state


def kernel(x_prompt, x_sample, cache_nsa_kv, cache_win_kv, state_mlstm_c, state_mlstm_n, state_mlstm_m,
           state_s5_re, state_s5_im, state_ffn_conv, page_table, w_in, mlstm_gate_b, mlstm_norm_w,
           nsa_gate_b, cmp_w1, cmp_b1, cmp_w2, rel_bias, s5_a_re, s5_a_im, s5_b_re, s5_b_im, s5_c_re,
           s5_c_im, s5_d, s5_log_step, s5_glu_w, s5_glu_b, w_out, ln1_w, ln1_b, ffn_w_up, ffn_conv_w,
           ffn_conv_b, ffn_w_down, ln2_w, ln2_b):
    params = dict(w_in=w_in, mlstm_gate_b=mlstm_gate_b, mlstm_norm_w=mlstm_norm_w, nsa_gate_b=nsa_gate_b,
                  cmp_w1=cmp_w1, cmp_b1=cmp_b1, cmp_w2=cmp_w2, s5_a_re=s5_a_re, s5_a_im=s5_a_im, s5_b_re=s5_b_re,
                  s5_b_im=s5_b_im, s5_c_re=s5_c_re, s5_c_im=s5_c_im, s5_d=s5_d, s5_log_step=s5_log_step,
                  s5_glu_w=s5_glu_w, s5_glu_b=s5_glu_b, w_out=w_out, ln1_w=ln1_w, ln1_b=ln1_b, ffn_w_up=ffn_w_up,
                  ffn_conv_w=ffn_conv_w, ffn_conv_b=ffn_conv_b, ffn_w_down=ffn_w_down, ln2_w=ln2_w, ln2_b=ln2_b)
    depth = w_in.shape[0]
    bp, tp, _ = x_prompt.shape
    bs, tn, _ = x_sample.shape
    assert tp % TK == 0 and tn < CMP_STRIDE and tn <= TS and bp <= SUBLANES and bs <= SUBLANES
    assert page_table.shape[1] % CMP_PAGES == 0 and cache_nsa_kv.shape[2] == PAGE_SIZE
    cache2d = cache_nsa_kv.reshape(depth, cache_nsa_kv.shape[1], PAGE_SIZE, 4 * KV_HEADS * HEAD_DIM)
    win2d = cache_win_kv.reshape(depth, bs, cache_win_kv.shape[2], 2 * KV_HEADS * HEAD_DIM)
    xp = x_prompt.reshape(bp * tp, D_MODEL)
    xs = jnp.pad(x_sample, ((0, 0), (0, TS - tn), (0, 0))).reshape(bs * TS, D_MODEL)
    p_states, s_states = [], []
    for l in range(depth):
        lw = _pack_layer(params, l)
        xp, sp = _prompt_layer(xp, bp, tp, lw, rel_bias)
        xs, ss = _sample_layer(xs, bs, tn, lw, rel_bias, l, cache2d, page_table, win2d[l],
                               (state_mlstm_c[l], jnp.pad(state_mlstm_n[l], ((0, 0), (0, 0), (0, LANES - HEAD_DIM))),
                                state_mlstm_m[l]),
                               (state_s5_re[l], state_s5_im[l]), state_ffn_conv[l])
        p_states.append(sp)
        s_states.append(ss)
    stk = lambda states, i: jnp.stack([s[i] for s in states])
    y_prompt = xp.reshape(bp, tp, D_MODEL)
    y_sample = xs.reshape(bs, TS, D_MODEL)[:, :tn]
    return (y_prompt, y_sample,
            stk(p_states, 0), stk(s_states, 0), stk(p_states, 1), stk(s_states, 1),
            stk(p_states, 2), stk(s_states, 2), stk(p_states, 3), stk(s_states, 3), stk(p_states, 4), stk(s_states, 4),
            stk(p_states, 5), stk(s_states, 5), stk(p_states, 6), stk(s_states, 6), stk(p_states, 7), stk(s_states, 7))
```

```python
import functools
import math

import jax
import jax.numpy as jnp
import numpy as np
from jax import lax
from jax.experimental import pallas as pl
from jax.experimental.pallas import tpu as pltpu

F32 = jnp.float32
BF16 = jnp.bfloat16
NEG_INF = float("-inf")
M_INIT = -1e30
MASKED = -1e30

D_MODEL = 2048
PAGE_SIZE = 128
D_MLSTM = D_MODEL // 4
D_NSA = D_MODEL // 2
D_S5 = D_MODEL - D_MLSTM - D_NSA
HEAD_DIM = 128
MLSTM_HEADS = D_MLSTM // HEAD_DIM
NSA_HEADS = D_NSA // HEAD_DIM
KV_HEADS = 2
GROUP = NSA_HEADS // KV_HEADS
N_KV_SLOTS = 6
CMP_BLOCK = 32
CMP_STRIDE = 16
SEL_BLOCK = 64
SEL_TOPK = 16
FORCE_SCORE = 1e4
WINDOW = 512
S5_GROUP_WIDTH = 16
S5_GROUPS = D_S5 // S5_GROUP_WIDTH
S5_STATE = 64
S5_CH = S5_GROUPS * S5_STATE
NUM_BUCKETS = 32
MAX_DISTANCE = 128
D_FF = ((8 * D_MODEL // 3 + 127) // 128) * 128
CONV_W = 3
LN_EPS = 1e-5
DEPTH = 2
DEEPNORM_ALPHA = (2 * DEPTH) ** 0.25
QK_SCALE = HEAD_DIM ** -0.5

LANES = 128
SUBLANES = 8
V7X_VMEM_LIMIT = 56 * 2 ** 20

C_MQ, C_MK, C_MV, C_MO = 0, 512, 1024, 1536
C_NQ = 2048
C_NKV = 3072
C_SU = 4608
C_GATE = 5120
N_IN = 5376
GATE_MI, GATE_MF, GATE_NG = 0, 4, 8
D_FFP = 5632
MLSTM_L = 256
TQ = 128
TK = 256
N_TZ = 3
S5_LC = 64
CMP_PAGES = 32
SEL_PAGES = 32
SEL_SHIFT = SEL_BLOCK.bit_length() - 1


def _cparams(*sem):
    return pltpu.CompilerParams(dimension_semantics=sem, vmem_limit_bytes=V7X_VMEM_LIMIT)


def _dot(a, b):
    return jnp.dot(a, b, preferred_element_type=F32)


def _dot_nt(a, b):
    return lax.dot_general(a, b, (((1,), (1,)), ((), ())), preferred_element_type=F32)


def _layer_norm(y, w, b):
    mu = jnp.mean(y, axis=-1, keepdims=True)
    d = y - mu
    var = jnp.mean(d * d, axis=-1, keepdims=True)
    return d * lax.rsqrt(var + LN_EPS) * w + b


def _mm_kernel(x_ref, w_ref, o_ref):
    o_ref[...] = _dot(x_ref[...].astype(BF16), w_ref[...]).astype(o_ref.dtype)


def _matmul(x, w, *, tm, tn):
    m, k = x.shape
    n = w.shape[1]
    tm = min(tm, m)
    return pl.pallas_call(
        _mm_kernel,
        grid=(m // tm, n // tn),
        in_specs=[pl.BlockSpec((tm, k), lambda i, j: (i, 0)), pl.BlockSpec((k, tn), lambda i, j: (0, j))],
        out_specs=pl.BlockSpec((tm, tn), lambda i, j: (i, j)),
        out_shape=jax.ShapeDtypeStruct((m, n), F32),
        compiler_params=_cparams("parallel", "parallel"),
        name="proj",
    )(x, w)


def _wout_kernel(om_ref, on_ref, os_ref, x_ref, w_ref, lw_ref, lb_ref, o_ref):
    acc = _dot(om_ref[...].astype(BF16), w_ref[0:D_MLSTM, :])
    acc += _dot(on_ref[...].astype(BF16), w_ref[D_MLSTM:D_MLSTM + D_NSA, :])
    acc += _dot(os_ref[...].astype(BF16), w_ref[D_MLSTM + D_NSA:D_MODEL, :])
    o_ref[...] = _layer_norm(DEEPNORM_ALPHA * x_ref[...] + acc, lw_ref[...], lb_ref[...])


def _wout_ln(om, on, os_, x, w, lw, lb, *, tm):
    m = x.shape[0]
    tm = min(tm, m)
    row = lambda width: pl.BlockSpec((tm, width), lambda i: (i, 0))
    full = lambda a: pl.BlockSpec(a.shape, lambda i: (0, 0))
    return pl.pallas_call(
        _wout_kernel,
        grid=(m // tm,),
        in_specs=[row(D_MLSTM), row(D_NSA), row(D_S5), row(D_MODEL), full(w), full(lw), full(lb)],
        out_specs=row(D_MODEL),
        out_shape=jax.ShapeDtypeStruct((m, D_MODEL), F32),
        compiler_params=_cparams("parallel"),
        name="wout_ln",
    )(om, on, os_, x, w, lw, lb)


def _down_kernel(h_ref, w_ref, x_ref, lw_ref, lb_ref, o_ref, acc_ref, *, nk):
    k = pl.program_id(1)

    @pl.when(k == 0)
    def _():
        acc_ref[...] = jnp.zeros_like(acc_ref)

    acc_ref[...] += _dot(h_ref[...], w_ref[...])

    @pl.when(k == nk - 1)
    def _():
        o_ref[...] = _layer_norm(DEEPNORM_ALPHA * x_ref[...] + acc_ref[...], lw_ref[...], lb_ref[...])


def _down_ln(h, w, x, lw, lb, *, tm, tk):
    m, kk = h.shape
    tm = min(tm, m)
    nk = kk // tk
    return pl.pallas_call(
        functools.partial(_down_kernel, nk=nk),
        grid=(m // tm, nk),
        in_specs=[
            pl.BlockSpec((tm, tk), lambda i, k: (i, k)),
            pl.BlockSpec((tk, D_MODEL), lambda i, k: (k, 0)),
            pl.BlockSpec((tm, D_MODEL), lambda i, k: (i, 0)),
            pl.BlockSpec((1, D_MODEL), lambda i, k: (0, 0)),
            pl.BlockSpec((1, D_MODEL), lambda i, k: (0, 0)),
        ],
        out_specs=pl.BlockSpec((tm, D_MODEL), lambda i, k: (i, 0)),
        out_shape=jax.ShapeDtypeStruct((m, D_MODEL), F32),
        scratch_shapes=[pltpu.VMEM((tm, D_MODEL), F32)],
        compiler_params=_cparams("parallel", "arbitrary"),
        name="down_ln",
    )(h, w, x, lw, lb)


def _convgate_kernel(a_ref, g_ref, ha_ref, hg_ref, sa_ref, sg_ref, wa_ref, wg_ref, ba_ref, bg_ref, o_ref):
    first = pl.program_id(1) == 0
    tt = a_ref.shape[0]
    rows = lax.broadcasted_iota(jnp.int32, (tt, 1), 0)

    def conv(cur_ref, halo_ref, st_ref, w_ref, b_ref):
        cur = cur_ref[...]
        prev = jnp.where(first, st_ref[...], halo_ref[...])
        p1 = prev[7:8, :]
        p2 = prev[6:7, :]
        x1 = jnp.where(rows == 0, p1, pltpu.roll(cur, 1, 0))
        x2 = jnp.where(rows == 0, p2, jnp.where(rows == 1, p1, pltpu.roll(cur, 2, 0)))
        w = w_ref[...]
        return b_ref[...] + w[0:1, :] * x2 + w[1:2, :] * x1 + w[2:3, :] * cur

    a = conv(a_ref, ha_ref, sa_ref, wa_ref, ba_ref)
    g = conv(g_ref, hg_ref, sg_ref, wg_ref, bg_ref)
    o_ref[...] = (a * jax.nn.sigmoid(a) * g).astype(o_ref.dtype)


def _convgate(up, state8, conv_w8, conv_b, *, tt, tf):
    b, t, _ = up.shape
    tt = min(tt, t)
    nf = D_FFP // tf
    hb = tt // SUBLANES
    cur_a = pl.BlockSpec((None, tt, tf), lambda bi, ti, fi: (bi, ti, fi))
    cur_g = pl.BlockSpec((None, tt, tf), lambda bi, ti, fi: (bi, ti, fi + nf))
    halo_a = pl.BlockSpec((None, SUBLANES, tf), lambda bi, ti, fi: (bi, jnp.maximum(ti * hb - 1, 0), fi))
    halo_g = pl.BlockSpec((None, SUBLANES, tf), lambda bi, ti, fi: (bi, jnp.maximum(ti * hb - 1, 0), fi + nf))
    st_a = pl.BlockSpec((None, SUBLANES, tf), lambda bi, ti, fi: (bi, 0, fi))
    st_g = pl.BlockSpec((None, SUBLANES, tf), lambda bi, ti, fi: (bi, 0, fi + nf))
    w_a = pl.BlockSpec((SUBLANES, tf), lambda bi, ti, fi: (0, fi))
    w_g = pl.BlockSpec((SUBLANES, tf), lambda bi, ti, fi: (0, fi + nf))
    b_a = pl.BlockSpec((1, tf), lambda bi, ti, fi: (0, fi))
    b_g = pl.BlockSpec((1, tf), lambda bi, ti, fi: (0, fi + nf))
    return pl.pallas_call(
        _convgate_kernel,
        grid=(b, t // tt, nf),
        in_specs=[cur_a, cur_g, halo_a, halo_g, st_a, st_g, w_a, w_g, b_a, b_g],
        out_specs=pl.BlockSpec((None, tt, tf), lambda bi, ti, fi: (bi, ti, fi)),
        out_shape=jax.ShapeDtypeStruct((b, t, D_FFP), BF16),
        compiler_params=_cparams("parallel", "parallel", "parallel"),
        name="convgate",
    )(up, up, up, up, state8, state8, conv_w8, conv_w8, conv_b, conv_b)


def _cumsum_rows(x):
    n = x.shape[0]
    rows = lax.broadcasted_iota(jnp.int32, (n, 1), 0)
    d = 1
    while d < n:
        x = x + jnp.where(rows >= d, pltpu.roll(x, d, 0), 0.0)
        d *= 2
    return x


def _log_sigmoid(x):
    return jnp.minimum(x, 0.0) - jnp.log1p(jnp.exp(-jnp.abs(x)))


def _mlstm_kernel(q_ref, k_ref, v_ref, og_ref, g_ref, gb_ref, nw_ref, c0_ref, n0_ref, m0_ref,
                  out_ref, c_ref, n_ref, m_ref, *, L, t_valid):
    ci = pl.program_id(1)
    lin = q_ref.shape[0]

    @pl.when(ci == 0)
    def _():
        c_ref[...] = c0_ref[...]
        n_ref[...] = n0_ref[...]
        m_ref[...] = m0_ref[...]

    def rows_of(ref):
        x = ref[...]
        if lin < L:
            x = jnp.concatenate([x, jnp.zeros((L - lin, x.shape[1]), x.dtype)], axis=0)
        return x

    rows = lax.broadcasted_iota(jnp.int32, (L, 1), 0)
    valid = (ci * L + rows) < t_valid
    pre = rows_of(g_ref) + gb_ref[...]
    lf = jnp.where(valid, _log_sigmoid(pre), 0.0)
    ig = jnp.where(valid, pre, NEG_INF)
    bcum = _cumsum_rows(lf)
    dt = (pltpu.roll(ig, GATE_MF - GATE_MI, 1) - bcum).T
    q_all, k_all, v_all, og_all = rows_of(q_ref), rows_of(k_ref), rows_of(v_ref), rows_of(og_ref)
    tri = lax.broadcasted_iota(jnp.int32, (L, L), 0) >= lax.broadcasted_iota(jnp.int32, (L, L), 1)
    nw = nw_ref[...]

    for h in range(MLSTM_HEADS):
        sl = slice(h * HEAD_DIM, (h + 1) * HEAD_DIM)
        q = q_all[:, sl]
        k = k_all[:, sl] * QK_SCALE
        v = v_all[:, sl]
        qb, kb, vb = q.astype(BF16), k.astype(BF16), v.astype(BF16)
        b_col = bcum[:, GATE_MF + h:GATE_MF + h + 1]
        ig_col = ig[:, GATE_MI + h:GATE_MI + h + 1]
        d_row = dt[GATE_MF + h:GATE_MF + h + 1, :]
        c_prev = c_ref[h]
        n_prev = n_ref[h:h + 1, :]
        m_prev = m_ref[h:h + 1, 0:1]

        dmat = jnp.where(tri, b_col + d_row, NEG_INF)
        g_col = b_col + m_prev
        m_row = jnp.maximum(jnp.max(dmat, axis=1, keepdims=True), g_col)
        a = jnp.exp(dmat - m_row) * _dot_nt(qb, kb)
        w_inter = jnp.exp(g_col - m_row)
        num = _dot(a.astype(BF16), vb) + w_inter * _dot(qb, c_prev.astype(BF16))
        den = jnp.sum(a, axis=1, keepdims=True) + w_inter * jnp.sum(q * n_prev, axis=1, keepdims=True)
        hid = num / jnp.maximum(jnp.abs(den), jnp.exp(-m_row))
        mu = jnp.mean(hid, axis=1, keepdims=True)
        dlt = hid - mu
        var = jnp.mean(dlt * dlt, axis=1, keepdims=True)
        hn = dlt * lax.rsqrt(var + LN_EPS) * nw[:, sl]
        res = jax.nn.sigmoid(og_all[:, sl]) * hn
        out_ref[:, sl] = res[0:lin, :]

        f_tot = b_col[L - 1:L, :]
        w_s = f_tot - b_col + ig_col
        m_new = jnp.maximum(f_tot + m_prev, jnp.max(w_s, axis=0, keepdims=True))
        ws = jnp.exp(w_s - m_new)
        decay = jnp.exp(f_tot + m_prev - m_new)
        c_ref[h] = decay * c_prev + _dot(k.T.astype(BF16), (ws * v).astype(BF16))
        n_ref[h:h + 1, :] = decay * n_prev + jnp.sum(ws * k, axis=0, keepdims=True)
        m_ref[h:h + 1, :] = jnp.broadcast_to(m_new, (1, LANES))


def _mlstm(z, gate_b, norm_w, c0, n0, m0, *, L, lin, t_valid):
    b, tz, _ = z.shape
    nchunks = tz // lin
    qblk = lambda col: pl.BlockSpec((None, lin, D_MLSTM), lambda bi, ci: (bi, ci, col // D_MLSTM))
    st4 = pl.BlockSpec((None, MLSTM_HEADS, HEAD_DIM, HEAD_DIM), lambda bi, ci: (bi, 0, 0, 0))
    st3 = pl.BlockSpec((None, MLSTM_HEADS, LANES), lambda bi, ci: (bi, 0, 0))
    return pl.pallas_call(
        functools.partial(_mlstm_kernel, L=L, t_valid=t_valid),
        grid=(b, nchunks),
        in_specs=[qblk(C_MQ), qblk(C_MK), qblk(C_MV), qblk(C_MO),
                  pl.BlockSpec((None, lin, LANES), lambda bi, ci: (bi, ci, C_GATE // LANES)),
                  pl.BlockSpec((1, LANES), lambda bi, ci: (0, 0)),
                  pl.BlockSpec((1, D_MLSTM), lambda bi, ci: (0, 0)),
                  st4, st3, st3],
        out_specs=[pl.BlockSpec((None, lin, D_MLSTM), lambda bi, ci: (bi, ci, 0)), st4, st3, st3],
        out_shape=[jax.ShapeDtypeStruct((b, tz, D_MLSTM), F32),
                   jax.ShapeDtypeStruct((b, MLSTM_HEADS, HEAD_DIM, HEAD_DIM), F32),
                   jax.ShapeDtypeStruct((b, MLSTM_HEADS, LANES), F32),
                   jax.ShapeDtypeStruct((b, MLSTM_HEADS, LANES), F32)],
        compiler_params=_cparams("parallel", "arbitrary"),
        name="mlstm",
    )(z, z, z, z, z, gate_b, norm_w, c0, n0, m0)


def _s5_kernel(u_ref, wb_ref, wc_ref, lr_ref, li_ref, d_ref, gw_ref, gb_ref, x0r_ref, x0i_ref,
               o_ref, xr_ref, xi_ref, sr_ref, si_ref, *, lc):
    @pl.when(pl.program_id(0) == 0)
    def _():
        xr_ref[...] = x0r_ref[...]
        xi_ref[...] = x0i_ref[...]

    u = u_ref[...]
    bu = _dot(u.astype(BF16), wb_ref[...])
    sr_ref[...] = bu[:, 0:S5_CH]
    si_ref[...] = bu[:, S5_CH:2 * S5_CH]
    lam_r = jnp.broadcast_to(lr_ref[...], (SUBLANES, S5_CH))
    lam_i = jnp.broadcast_to(li_ref[...], (SUBLANES, S5_CH))

    def step(t, carry):
        xr, xi = carry
        r0 = pl.multiple_of(t * SUBLANES, SUBLANES)
        nr = lam_r * xr - lam_i * xi + sr_ref[pl.ds(r0, SUBLANES), :]
        ni = lam_r * xi + lam_i * xr + si_ref[pl.ds(r0, SUBLANES), :]
        sr_ref[pl.ds(r0, SUBLANES), :] = nr
        si_ref[pl.ds(r0, SUBLANES), :] = ni
        return nr, ni

    xr, xi = lax.fori_loop(0, lc, step, (xr_ref[...], xi_ref[...]))
    xr_ref[...] = xr
    xi_ref[...] = xi
    y = _dot(sr_ref[...].astype(BF16), wc_ref[0:S5_CH, :]) + _dot(si_ref[...].astype(BF16), wc_ref[S5_CH:2 * S5_CH, :])
    y = y + d_ref[...] * u
    zz = jax.nn.gelu(y)
    o_ref[...] = zz * jax.nn.sigmoid(_dot(zz.astype(BF16), gw_ref[...]) + gb_ref[...])


def _s5(u_tm, wb, wc, lam_r, lam_i, d, glu_w, glu_b, x0r, x0i, *, lc):
    rows = u_tm.shape[0]
    t = rows // SUBLANES
    lc = min(lc, t)
    full = lambda a: pl.BlockSpec(a.shape, lambda i: (0,) * a.ndim)
    st = pl.BlockSpec((SUBLANES, S5_CH), lambda i: (0, 0))
    return pl.pallas_call(
        functools.partial(_s5_kernel, lc=lc),
        grid=(t // lc,),
        in_specs=[pl.BlockSpec((lc * SUBLANES, D_S5), lambda i: (i, 0)),
                  full(wb), full(wc), full(lam_r), full(lam_i), full(d), full(glu_w), full(glu_b), st, st],
        out_specs=[pl.BlockSpec((lc * SUBLANES, D_S5), lambda i: (i, 0)), st, st],
        out_shape=[jax.ShapeDtypeStruct((rows, D_S5), F32),
                   jax.ShapeDtypeStruct((SUBLANES, S5_CH), F32),
                   jax.ShapeDtypeStruct((SUBLANES, S5_CH), F32)],
        scratch_shapes=[pltpu.VMEM((lc * SUBLANES, S5_CH), F32), pltpu.VMEM((lc * SUBLANES, S5_CH), F32)],
        compiler_params=_cparams("arbitrary"),
        name="s5",
    )(u_tm, wb, wc, lam_r, lam_i, d, glu_w, glu_b, x0r, x0i)


N_CMB = 2 * KV_HEADS


def _cmp_project(rows_of, w_ref, o_ref):
    nch = o_ref.shape[0]
    for cmb in range(N_CMB):
        slot = cmb // KV_HEADS
        rows_ref = rows_of(cmb)
        acc = jnp.zeros((nch, 2 * HEAD_DIM), F32)
        for l in range(CMP_STRIDE):
            x = rows_ref[pl.ds(l, nch, stride=CMP_STRIDE), :]
            acc += _dot(x.astype(BF16), w_ref[slot, l])
        o_ref[:, cmb * 2 * HEAD_DIM:(cmb + 1) * 2 * HEAD_DIM] = acc


def _cmp_p_kernel(r0_ref, r1_ref, r2_ref, r3_ref, w_ref, o_ref):
    rows = (r0_ref, r1_ref, r2_ref, r3_ref)
    _cmp_project(lambda cmb: rows[cmb], w_ref, o_ref)


def _cmp_project_prompt(z, wcat):
    b, t, _ = z.shape
    nch = t // CMP_STRIDE
    rows = lambda cmb: pl.BlockSpec((None, t, HEAD_DIM), lambda bi: (bi, 0, C_NKV // HEAD_DIM + cmb))
    return pl.pallas_call(
        _cmp_p_kernel,
        grid=(b,),
        in_specs=[rows(cmb) for cmb in range(N_CMB)] + [pl.BlockSpec(wcat.shape, lambda bi: (0, 0, 0, 0))],
        out_specs=pl.BlockSpec((None, nch, N_CMB * 2 * HEAD_DIM), lambda bi: (bi, 0, 0)),
        out_shape=jax.ShapeDtypeStruct((b, nch, N_CMB * 2 * HEAD_DIM), F32),
        compiler_params=_cparams("parallel"),
        name="cmp_project_prompt",
    )(z, z, z, z, wcat)


def _cmp_s_kernel(pt_ref, *refs):
    del pt_ref
    page_refs = refs[:2 * CMP_PAGES]
    w_ref, o_ref, rows_ref = refs[2 * CMP_PAGES:]
    for slot in range(2):
        for p in range(CMP_PAGES):
            ref = page_refs[slot * CMP_PAGES + p]
            for g in range(KV_HEADS):
                rows_ref[slot * KV_HEADS + g, p * PAGE_SIZE:(p + 1) * PAGE_SIZE, :] = ref[:, g, :]
    _cmp_project(lambda cmb: rows_ref.at[cmb], w_ref, o_ref)


def _page_spec(layer, slot, p, pages_per_step, grid_rank):
    def index_map(*idx):
        bi, si, pt = idx[0], idx[grid_rank - 1], idx[grid_rank]
        return (layer, pt[bi, si * pages_per_step + p], 0, slot, 0, 0)

    return pl.BlockSpec((None, None, PAGE_SIZE, None, KV_HEADS, HEAD_DIM), index_map)


def _cmp_project_sample(cache6, page_table, layer, wcat):
    b, npages = page_table.shape
    steps = npages // CMP_PAGES
    nch = CMP_PAGES * PAGE_SIZE // CMP_STRIDE
    pages = [_page_spec(layer, slot, p, CMP_PAGES, 2) for slot in range(2) for p in range(CMP_PAGES)]

    grid_spec = pltpu.PrefetchScalarGridSpec(
        num_scalar_prefetch=1,
        grid=(b, steps),
        in_specs=pages + [pl.BlockSpec(wcat.shape, lambda bi, si, pt: (0, 0, 0, 0))],
        out_specs=pl.BlockSpec((None, nch, N_CMB * 2 * HEAD_DIM), lambda bi, si, pt: (bi, si, 0)),
        scratch_shapes=[pltpu.VMEM((N_CMB, CMP_PAGES * PAGE_SIZE, HEAD_DIM), F32)],
    )
    return pl.pallas_call(
        _cmp_s_kernel,
        grid_spec=grid_spec,
        out_shape=jax.ShapeDtypeStruct((b, steps * nch, N_CMB * 2 * HEAD_DIM), F32),
        compiler_params=_cparams("parallel", "arbitrary"),
        name="cmp_project_sample",
    )(page_table, *([cache6] * (2 * CMP_PAGES)), wcat)


def _cmp_fin_kernel(p_ref, b1_ref, w2_ref, o_ref):
    nch = p_ref.shape[0]
    for cmb in range(N_CMB):
        slot = cmb // KV_HEADS
        c0 = cmb * 2 * HEAD_DIM
        first = p_ref[:, c0:c0 + HEAD_DIM]
        second = pltpu.roll(p_ref[:, c0 + HEAD_DIM:c0 + 2 * HEAD_DIM], nch - 1, 0)
        hid = b1_ref[slot:slot + 1, :] + first + second
        o_ref[cmb] = _dot(jax.nn.gelu(hid).astype(BF16), w2_ref[slot])


def _cmp_finish(p, b1, w2):
    b, nch, _ = p.shape
    return pl.pallas_call(
        _cmp_fin_kernel,
        grid=(b,),
        in_specs=[pl.BlockSpec((None, nch, p.shape[2]), lambda bi: (bi, 0, 0)),
                  pl.BlockSpec(b1.shape, lambda bi: (0, 0)),
                  pl.BlockSpec(w2.shape, lambda bi: (0, 0, 0))],
        out_specs=pl.BlockSpec((None, N_CMB, nch, HEAD_DIM), lambda bi: (bi, 0, 0, 0)),
        out_shape=jax.ShapeDtypeStruct((b, N_CMB, nch, HEAD_DIM), F32),
        compiler_params=_cparams("parallel"),
        name="cmp_finish",
    )(p, b1, w2)


def _softmax_init(m_scr, l_scr, acc_scr):
    m_scr[...] = jnp.full(m_scr.shape, M_INIT, F32)
    l_scr[...] = jnp.zeros(l_scr.shape, F32)
    acc_scr[...] = jnp.zeros(acc_scr.shape, F32)


def _softmax_update(s, mask, vb, m_scr, l_scr, acc_scr):
    s = jnp.where(mask, s, NEG_INF)
    m_prev = m_scr[...]
    m_new = jnp.maximum(m_prev, jnp.max(s, axis=1, keepdims=True))
    alpha = jnp.exp(m_prev - m_new)
    p = jnp.exp(s - m_new)
    l_scr[...] = alpha * l_scr[...] + jnp.sum(p, axis=1, keepdims=True)
    acc_scr[...] = alpha * acc_scr[...] + _dot(p.astype(BF16), vb)
    m_scr[...] = m_new


def _softmax_result(l_scr, acc_scr):
    l = l_scr[...]
    return acc_scr[...] / jnp.where(l > 0, l, 1.0)


def _masked_probs(s, mask):
    s = jnp.where(mask, s, NEG_INF)
    m = jnp.max(s, axis=1, keepdims=True)
    m = jnp.where(m > NEG_INF, m, 0.0)
    p = jnp.exp(s - m)
    den = jnp.sum(p, axis=1, keepdims=True)
    return p / jnp.where(den > 0, den, 1.0)


def _stack_heads(q):
    return jnp.concatenate([q[:, j * HEAD_DIM:(j + 1) * HEAD_DIM] for j in range(GROUP)], axis=0)


def _block_importance(pc, n_cmp, n_sel):
    ncp = pc.shape[1]
    nsp = -(-n_sel // LANES) * LANES
    c_start = lax.broadcasted_iota(jnp.int32, (ncp, nsp), 0) * CMP_STRIDE
    s_start = lax.broadcasted_iota(jnp.int32, (ncp, nsp), 1) * SEL_BLOCK
    overlap = ((c_start < s_start + SEL_BLOCK) & (c_start + CMP_BLOCK > s_start)
               & (c_start < n_cmp * CMP_STRIDE)).astype(F32)
    return jnp.dot(pc, overlap, preferred_element_type=F32, precision=lax.Precision.HIGHEST)


def _force_blocks(imp, blk, cur):
    forced = (blk == 0) | (blk == cur) | (blk == cur - 1)
    imp = jnp.where(forced, FORCE_SCORE, imp)
    return jnp.where(blk > cur, -1.0, imp)


def _select_blocks_cols(pc, cur, n_cmp, n_sel):
    imp = _block_importance(pc, n_cmp, n_sel)
    blk = lax.broadcasted_iota(jnp.int32, (1, imp.shape[1]), 1)
    imp = _force_blocks(imp, blk, cur)

    def body(sp, count):
        col = jnp.sum(jnp.where(blk == sp, imp, 0.0), axis=1, keepdims=True)
        ahead = (col > imp) | ((col == imp) & (sp < blk))
        return count + ahead.astype(F32)

    count = lax.fori_loop(0, n_sel, body, jnp.zeros(imp.shape, F32), unroll=8)
    return ((count < min(SEL_TOPK, n_sel)) & (blk < n_sel)).astype(F32)


def _select_blocks_rows(pc, cur_row, n_cmp, n_sel):
    rows = pc.shape[0]
    nrow = -(-n_sel // SUBLANES) * SUBLANES
    imp_t = _block_importance(pc, n_cmp, n_sel).T[0:nrow, :]
    blk = lax.broadcasted_iota(jnp.int32, (nrow, 1), 0)
    imp_t = _force_blocks(imp_t, blk, cur_row)
    count = jnp.zeros(imp_t.shape, F32)
    for sp in range(n_sel):
        row = imp_t[sp:sp + 1, :]
        count += ((row > imp_t) | ((row == imp_t) & (sp < blk))).astype(F32)
    sel_t = ((count < min(SEL_TOPK, n_sel)) & (blk < n_sel)).astype(F32)
    sel_t = jnp.concatenate([sel_t, jnp.zeros((LANES - nrow, rows), F32)], axis=0)
    return sel_t.T


def _gate(gt, col):
    lane = lax.broadcasted_iota(jnp.int32, (1, LANES), 1)
    return jax.nn.sigmoid(jnp.sum(jnp.where(lane == col, gt, 0.0), axis=1, keepdims=True))


def _nsa_prompt_kernel(q_ref, ks_ref, vs_ref, kw_ref, vw_ref, kc_ref, vc_ref, bc_ref, tz_ref, gt_ref, gb_ref,
                       o_ref, s_scr, mx_scr, l_scr, acc_scr, *, t_len):
    g = pl.program_id(1)
    qt = pl.program_id(2)
    q0 = qt * TQ
    n_cmp = t_len // CMP_STRIDE - (CMP_BLOCK // CMP_STRIDE) + 1
    n_sel = t_len // SEL_BLOCK
    ncp = kc_ref.shape[0]
    rows4 = GROUP * TQ
    qs = _stack_heads(q_ref[...] * QK_SCALE).astype(BF16)
    qpos = q0 + (lax.broadcasted_iota(jnp.int32, (rows4, 1), 0) & (TQ - 1))

    n_idx = lax.broadcasted_iota(jnp.int32, (1, ncp), 1)
    s_c = _dot_nt(qs, kc_ref[...].astype(BF16)) + bc_ref[...].reshape(rows4, ncp)
    cmask = (qpos - (n_idx * CMP_STRIDE + CMP_BLOCK - 1) >= 0) & (n_idx < n_cmp)
    p_c = _masked_probs(s_c, cmask)
    o_c = _dot(p_c.astype(BF16), vc_ref[...].astype(BF16))
    pc = p_c[0:TQ] + p_c[TQ:2 * TQ] + p_c[2 * TQ:3 * TQ] + p_c[3 * TQ:4 * TQ]
    cur_row = jnp.right_shift(q0 + lax.broadcasted_iota(jnp.int32, (1, TQ), 1), SEL_SHIFT)
    sel = _select_blocks_rows(pc, cur_row, n_cmp, n_sel)
    sel4 = jnp.concatenate([((sel - 1.0) * -MASKED).astype(BF16)] * GROUP, axis=0)

    kt_hi = (q0 + TQ - 1) // TK + 1
    kt_far = jnp.maximum(q0 - (FAR_DIST - 1), 0) // TK
    kidx = lax.broadcasted_iota(jnp.int32, (1, TK), 1)
    srow = lax.broadcasted_iota(jnp.int32, (LANES, TK), 0)

    def tile_bias(k0):
        return tz_ref[jnp.clip((q0 - k0) // TQ, 0, N_TZ - 1)].reshape(rows4, TK)

    def attend(k_ref, v_ref, kt_lo, mask_tile):
        mx_scr[...] = jnp.full(mx_scr.shape, NEG_INF, F32)

        def scores(near):
            def body(kt, carry):
                k0 = pl.multiple_of(kt * TK, TK)
                s = _dot_nt(qs, k_ref[pl.ds(k0, TK), :].astype(BF16))
                if near:
                    s = s + tile_bias(k0)
                s = mask_tile(k0, s, near)
                s_scr[kt] = s
                mx_scr[...] = jnp.maximum(mx_scr[...], jnp.maximum(s[:, 0:LANES], s[:, LANES:TK]))
                return carry
            return body

        kt_mid = jnp.maximum(kt_far, kt_lo)
        lax.fori_loop(kt_lo, kt_mid, scores(False), 0)
        lax.fori_loop(kt_mid, kt_hi, scores(True), 0)
        m = jnp.max(mx_scr[...], axis=1, keepdims=True)
        mx_scr[...] = jnp.broadcast_to(jnp.where(m > NEG_INF, m, 0.0), mx_scr.shape)
        l_scr[...] = jnp.zeros(l_scr.shape, F32)
        acc_scr[...] = jnp.zeros(acc_scr.shape, F32)

        def probs(kt, carry):
            k0 = pl.multiple_of(kt * TK, TK)
            m_rep = mx_scr[...]
            p = jnp.exp(s_scr[kt] - jnp.concatenate([m_rep, m_rep], axis=1))
            l_scr[...] += p[:, 0:LANES] + p[:, LANES:TK]
            acc_scr[...] += _dot(p.astype(BF16), v_ref[pl.ds(k0, TK), :].astype(BF16))
            return carry

        lax.fori_loop(kt_lo, kt_hi, probs, 0)
        l = jnp.sum(l_scr[...], axis=1, keepdims=True)
        return acc_scr[...] / jnp.where(l > 0, l, 1.0)

    def sel_mask(k0, s, near):
        expand = (jnp.right_shift(k0 + lax.broadcasted_iota(jnp.int32, (LANES, TK), 1), SEL_SHIFT) == srow).astype(BF16)
        s = s + _dot(sel4, expand)
        return jnp.where(k0 + kidx <= qpos, s, NEG_INF) if near else s

    o_s = attend(ks_ref, vs_ref, 0, sel_mask)

    def win_mask(k0, s, near):
        dist = qpos - (k0 + kidx)
        return jnp.where((dist >= 0) & (dist <= WINDOW), s, NEG_INF)

    o_w = attend(kw_ref, vw_ref, jnp.maximum(q0 - WINDOW, 0) // TK, win_mask)

    gt = gt_ref[...] + gb_ref[...]
    for j in range(GROUP):
        col = GATE_NG + (g * GROUP + j) * 3
        r = slice(j * TQ, (j + 1) * TQ)
        o_ref[:, j * HEAD_DIM:(j + 1) * HEAD_DIM] = (
            _gate(gt, col) * o_c[r] + _gate(gt, col + 1) * o_s[r] + _gate(gt, col + 2) * o_w[r])


def _nsa_prompt(z, kcvc, bias_c, tz, gate_b):
    b, t, _ = z.shape
    ncp = kcvc.shape[2]
    nq = t // TQ
    kv = lambda slot: pl.BlockSpec((None, t, HEAD_DIM), lambda bi, g, qi: (bi, 0, C_NKV // HEAD_DIM + slot * KV_HEADS + g))
    cmp_blk = lambda slot: pl.BlockSpec((None, None, ncp, HEAD_DIM), lambda bi, g, qi: (bi, slot * KV_HEADS + g, 0, 0))
    return pl.pallas_call(
        functools.partial(_nsa_prompt_kernel, t_len=t),
        grid=(b, KV_HEADS, nq),
        in_specs=[pl.BlockSpec((None, TQ, GROUP * HEAD_DIM), lambda bi, g, qi: (bi, qi, C_NQ // (GROUP * HEAD_DIM) + g)),
                  kv(2), kv(3), kv(4), kv(5), cmp_blk(0), cmp_blk(1),
                  pl.BlockSpec((GROUP, TQ, ncp), lambda bi, g, qi: (g, qi, 0)),
                  pl.BlockSpec((N_TZ, GROUP, TQ, TK), lambda bi, g, qi: (0, g, 0, 0)),
                  pl.BlockSpec((None, TQ, LANES), lambda bi, g, qi: (bi, qi, C_GATE // LANES)),
                  pl.BlockSpec((1, LANES), lambda bi, g, qi: (0, 0))],
        out_specs=pl.BlockSpec((None, TQ, GROUP * HEAD_DIM), lambda bi, g, qi: (bi, qi, g)),
        out_shape=jax.ShapeDtypeStruct((b, t, D_NSA), F32),
        scratch_shapes=[pltpu.VMEM((t // TK, GROUP * TQ, TK), F32), pltpu.VMEM((GROUP * TQ, LANES), F32),
                        pltpu.VMEM((GROUP * TQ, LANES), F32), pltpu.VMEM((GROUP * TQ, HEAD_DIM), F32)],
        compiler_params=_cparams("parallel", "parallel", "arbitrary"),
        name="nsa_prompt",
    )(z, z, z, z, z, kcvc, kcvc, bias_c, tz, z, gate_b)


TS = SUBLANES
ROWS_S = GROUP * TS


def _nsa_s_cmp_kernel(q_ref, kc_ref, vc_ref, bc_ref, oc_ref, sel_ref, *, pos0, n_cmp, n_sel):
    nch = kc_ref.shape[0]
    qs = _stack_heads(q_ref[...] * QK_SCALE).astype(BF16)
    qpos = pos0 + (lax.broadcasted_iota(jnp.int32, (ROWS_S, 1), 0) & (TS - 1))
    n_idx = lax.broadcasted_iota(jnp.int32, (1, nch), 1)
    s_c = _dot_nt(qs, kc_ref[...].astype(BF16)) + bc_ref[...].reshape(ROWS_S, nch)
    cmask = (qpos - (n_idx * CMP_STRIDE + CMP_BLOCK - 1) >= 0) & (n_idx < n_cmp)
    p_c = _masked_probs(s_c, cmask)
    oc_ref[...] = _dot(p_c.astype(BF16), vc_ref[...].astype(BF16))
    pc = p_c[0:TS] + p_c[TS:2 * TS] + p_c[2 * TS:3 * TS] + p_c[3 * TS:4 * TS]
    sel_ref[...] = _select_blocks_cols(pc, jnp.right_shift(qpos[0:TS], SEL_SHIFT), n_cmp, n_sel)


def _nsa_sample_cmp(z8, kcvc, bias_c, *, pos0, n_cmp, n_sel):
    b = z8.shape[0]
    nch = kcvc.shape[2]
    nsp = -(-n_sel // LANES) * LANES
    cmp_blk = lambda slot: pl.BlockSpec((None, None, nch, HEAD_DIM), lambda bi, g: (bi, slot * KV_HEADS + g, 0, 0))
    return pl.pallas_call(
        functools.partial(_nsa_s_cmp_kernel, pos0=pos0, n_cmp=n_cmp, n_sel=n_sel),
        grid=(b, KV_HEADS),
        in_specs=[pl.BlockSpec((None, TS, GROUP * HEAD_DIM), lambda bi, g: (bi, 0, C_NQ // (GROUP * HEAD_DIM) + g)),
                  cmp_blk(0), cmp_blk(1),
                  pl.BlockSpec((GROUP, TS, nch), lambda bi, g: (g, 0, 0))],
        out_specs=[pl.BlockSpec((None, None, ROWS_S, HEAD_DIM), lambda bi, g: (bi, g, 0, 0)),
                   pl.BlockSpec((None, None, TS, nsp), lambda bi, g: (bi, g, 0, 0))],
        out_shape=[jax.ShapeDtypeStruct((b, KV_HEADS, ROWS_S, HEAD_DIM), F32),
                   jax.ShapeDtypeStruct((b, KV_HEADS, TS, nsp), F32)],
        compiler_params=_cparams("parallel", "parallel"),
        name="nsa_sample_cmp",
    )(z8, kcvc, kcvc, bias_c)


def _pad_rows(x, n):
    return jnp.concatenate([x, jnp.zeros((n - x.shape[0], x.shape[1]), x.dtype)], axis=0)


def _nsa_s_sel_kernel(pt_ref, q_ref, sel_ref, kn_ref, vn_ref, bt_ref, bf_ref, *refs, pos0, npages):
    del pt_ref
    k_refs = refs[:SEL_PAGES]
    v_refs = refs[SEL_PAGES:2 * SEL_PAGES]
    o_ref, kbuf, vbuf, m_scr, l_scr, acc_scr = refs[2 * SEL_PAGES:]
    step = pl.program_id(1)
    nsteps = npages // SEL_PAGES
    nk = SEL_PAGES * PAGE_SIZE
    nblk = nk // SEL_BLOCK

    @pl.when(step == 0)
    def _():
        _softmax_init(m_scr, l_scr, acc_scr)

    nsp = sel_ref.shape[-1]
    qpos = pos0 + (lax.broadcasted_iota(jnp.int32, (ROWS_S, 1), 0) & (TS - 1))
    kidx = lax.broadcasted_iota(jnp.int32, (1, nk), 1)
    last = step == nsteps - 1
    blk_r = lax.broadcasted_iota(jnp.int32, (nsp, LANES), 0)
    blk_c = lax.broadcasted_iota(jnp.int32, (nsp, LANES), 1)
    window = ((blk_r == step * nblk + blk_c) & (blk_c < nblk)).astype(BF16)
    expand = (jnp.right_shift(lax.broadcasted_iota(jnp.int32, (LANES, nk), 1), SEL_SHIFT)
              == lax.broadcasted_iota(jnp.int32, (LANES, nk), 0)).astype(BF16)
    kidx_new = lax.broadcasted_iota(jnp.int32, (1, PAGE_SIZE), 1)
    new_blk = lax.broadcasted_iota(jnp.int32, (1, nsp), 1) == npages * (PAGE_SIZE // SEL_BLOCK)

    for g in range(KV_HEADS):
        rows = slice(g * ROWS_S, (g + 1) * ROWS_S)
        ms, ls, accs = m_scr.at[rows], l_scr.at[rows], acc_scr.at[rows]
        for p in range(SEL_PAGES):
            kbuf[p * PAGE_SIZE:(p + 1) * PAGE_SIZE, :] = k_refs[p][:, g, :].astype(BF16)
            vbuf[p * PAGE_SIZE:(p + 1) * PAGE_SIZE, :] = v_refs[p][:, g, :].astype(BF16)
        qs = _stack_heads(q_ref[:, g * GROUP * HEAD_DIM:(g + 1) * GROUP * HEAD_DIM] * QK_SCALE).astype(BF16)
        sel4 = jnp.concatenate([sel_ref[g]] * GROUP, axis=0)
        heads = slice(g * GROUP, (g + 1) * GROUP)
        bias_far = bf_ref[heads].reshape(ROWS_S, PAGE_SIZE)
        bias_tail = bt_ref[heads].reshape(ROWS_S, 2 * PAGE_SIZE)
        bias = jnp.concatenate([jnp.broadcast_to(bias_far[:, 0:1], (ROWS_S, nk - PAGE_SIZE)),
                                jnp.where(last, bias_tail[:, 0:PAGE_SIZE], bias_far)], axis=1)
        s = _dot_nt(qs, kbuf[...]) + bias
        chosen = _dot(_dot(sel4.astype(BF16), window).astype(BF16), expand) > 0.5
        _softmax_update(s, chosen & (step * nk + kidx <= qpos), vbuf[...], ms, ls, accs)

        @pl.when(last)
        def _():
            kn = _pad_rows(kn_ref[:, g * HEAD_DIM:(g + 1) * HEAD_DIM], PAGE_SIZE).astype(BF16)
            vn = _pad_rows(vn_ref[:, g * HEAD_DIM:(g + 1) * HEAD_DIM], PAGE_SIZE).astype(BF16)
            s_new = _dot_nt(qs, kn) + bias_tail[:, PAGE_SIZE:2 * PAGE_SIZE]
            flag = jnp.sum(jnp.where(new_blk, sel4, 0.0), axis=1, keepdims=True) > 0.5
            mask = flag & (pos0 + kidx_new <= qpos) & (kidx_new < TS)
            _softmax_update(s_new, mask, vn, ms, ls, accs)
            o_ref[g] = _softmax_result(ls, accs)


def _nsa_sample_sel(z8, sel, bias_tail, bias_far, cache6, page_table, layer, *, pos0):
    b, npages = page_table.shape
    nsp = sel.shape[-1]
    steps = npages // SEL_PAGES
    slot_cols = KV_HEADS * HEAD_DIM
    new_rows = lambda slot: pl.BlockSpec((None, TS, slot_cols),
                                         lambda bi, si, pt: (bi, 0, (C_NKV + slot * slot_cols) // slot_cols))
    whole = lambda a: pl.BlockSpec(a.shape, lambda bi, si, pt: (0,) * a.ndim)
    grid_spec = pltpu.PrefetchScalarGridSpec(
        num_scalar_prefetch=1,
        grid=(b, steps),
        in_specs=[pl.BlockSpec((None, TS, D_NSA), lambda bi, si, pt: (bi, 0, C_NQ // D_NSA)),
                  pl.BlockSpec((None, KV_HEADS, TS, nsp), lambda bi, si, pt: (bi, 0, 0, 0)),
                  new_rows(2), new_rows(3), whole(bias_tail), whole(bias_far)]
                 + [_page_spec(layer, 2, p, SEL_PAGES, 2) for p in range(SEL_PAGES)]
                 + [_page_spec(layer, 3, p, SEL_PAGES, 2) for p in range(SEL_PAGES)],
        out_specs=pl.BlockSpec((None, KV_HEADS, ROWS_S, HEAD_DIM), lambda bi, si, pt: (bi, 0, 0, 0)),
        scratch_shapes=[pltpu.VMEM((SEL_PAGES * PAGE_SIZE, HEAD_DIM), BF16),
                        pltpu.VMEM((SEL_PAGES * PAGE_SIZE, HEAD_DIM), BF16),
                        pltpu.VMEM((KV_HEADS * ROWS_S, 1), F32), pltpu.VMEM((KV_HEADS * ROWS_S, 1), F32),
                        pltpu.VMEM((KV_HEADS * ROWS_S, HEAD_DIM), F32)],
    )
    return pl.pallas_call(
        functools.partial(_nsa_s_sel_kernel, pos0=pos0, npages=npages),
        grid_spec=grid_spec,
        out_shape=jax.ShapeDtypeStruct((b, KV_HEADS, ROWS_S, HEAD_DIM), F32),
        compiler_params=_cparams("parallel", "arbitrary"),
        name="nsa_sample_sel",
    )(page_table, z8, sel, z8, z8, bias_tail, bias_far, *([cache6] * (2 * SEL_PAGES)))


def _nsa_s_win_kernel(q_ref, kw_ref, vw_ref, kn_ref, vn_ref, bw_ref, oc_ref, os_ref, gt_ref, gb_ref, o_ref,
                      m_scr, l_scr, acc_scr, *, pos0):
    g = pl.program_id(1)
    wb = kw_ref.shape[0]
    qs = _stack_heads(q_ref[...] * QK_SCALE).astype(BF16)
    qpos = pos0 + (lax.broadcasted_iota(jnp.int32, (ROWS_S, 1), 0) & (TS - 1))
    bias = bw_ref[...].reshape(ROWS_S, wb + PAGE_SIZE)
    _softmax_init(m_scr, l_scr, acc_scr)
    dist = qpos - (pos0 - wb + lax.broadcasted_iota(jnp.int32, (1, wb), 1))
    s = _dot_nt(qs, kw_ref[...].astype(BF16)) + bias[:, 0:wb]
    _softmax_update(s, (dist >= 0) & (dist <= WINDOW), vw_ref[...].astype(BF16), m_scr, l_scr, acc_scr)
    kidx = lax.broadcasted_iota(jnp.int32, (1, PAGE_SIZE), 1)
    dist = qpos - (pos0 + kidx)
    s = _dot_nt(qs, _pad_rows(kn_ref[...], PAGE_SIZE).astype(BF16)) + bias[:, wb:wb + PAGE_SIZE]
    _softmax_update(s, (dist >= 0) & (dist <= WINDOW) & (kidx < TS), _pad_rows(vn_ref[...], PAGE_SIZE).astype(BF16),
                    m_scr, l_scr, acc_scr)
    o_w = _softmax_result(l_scr, acc_scr)
    o_c = oc_ref[...]
    o_s = os_ref[...]
    gt = gt_ref[...] + gb_ref[...]
    for j in range(GROUP):
        col = GATE_NG + (g * GROUP + j) * 3
        r = slice(j * TS, (j + 1) * TS)
        o_ref[:, j * HEAD_DIM:(j + 1) * HEAD_DIM] = (
            _gate(gt, col) * o_c[r] + _gate(gt, col + 1) * o_s[r] + _gate(gt, col + 2) * o_w[r])


def _nsa_sample_win(z8, win2d, bias_w, o_c, o_s, gate_b, *, pos0):
    b, wb, _ = win2d.shape
    zcol = lambda col: (lambda bi, g: (bi, 0, col // HEAD_DIM + g))
    part = pl.BlockSpec((None, None, ROWS_S, HEAD_DIM), lambda bi, g: (bi, g, 0, 0))
    return pl.pallas_call(
        functools.partial(_nsa_s_win_kernel, pos0=pos0),
        grid=(b, KV_HEADS),
        in_specs=[pl.BlockSpec((None, TS, GROUP * HEAD_DIM), lambda bi, g: (bi, 0, C_NQ // (GROUP * HEAD_DIM) + g)),
                  pl.BlockSpec((None, wb, HEAD_DIM), lambda bi, g: (bi, 0, g)),
                  pl.BlockSpec((None, wb, HEAD_DIM), lambda bi, g: (bi, 0, KV_HEADS + g)),
                  pl.BlockSpec((None, TS, HEAD_DIM), zcol(C_NKV + 4 * KV_HEADS * HEAD_DIM)),
                  pl.BlockSpec((None, TS, HEAD_DIM), zcol(C_NKV + 5 * KV_HEADS * HEAD_DIM)),
                  pl.BlockSpec((GROUP, TS, wb + PAGE_SIZE), lambda bi, g: (g, 0, 0)),
                  part, part,
                  pl.BlockSpec((None, TS, LANES), lambda bi, g: (bi, 0, C_GATE // LANES)),
                  pl.BlockSpec((1, LANES), lambda bi, g: (0, 0))],
        out_specs=pl.BlockSpec((None, TS, GROUP * HEAD_DIM), lambda bi, g: (bi, 0, g)),
        out_shape=jax.ShapeDtypeStruct((b, TS, D_NSA), F32),
        scratch_shapes=[pltpu.VMEM((ROWS_S, 1), F32), pltpu.VMEM((ROWS_S, 1), F32), pltpu.VMEM((ROWS_S, HEAD_DIM), F32)],
        compiler_params=_cparams("parallel", "parallel"),
        name="nsa_sample_win",
    )(z8, win2d, win2d, z8, z8, bias_w, o_c, o_s, z8, gate_b)


def _t5_bucket(dist):
    n = np.maximum(dist, 0)
    exact = NUM_BUCKETS // 2
    nf = np.maximum(n, 1).astype(np.float32)
    large = exact + (np.log(nf / np.float32(exact)) / np.float32(math.log(MAX_DISTANCE / exact))
                     * np.float32(NUM_BUCKETS - exact)).astype(np.int32)
    return np.where(n < exact, n, np.minimum(large, NUM_BUCKETS - 1)).astype(np.int32)


FAR_DIST = int(np.max(np.nonzero(_t5_bucket(np.arange(4 * MAX_DISTANCE)) < NUM_BUCKETS - 1)[0])) + 1


def _bias_table(rel_bias, dist):
    onehot = jax.nn.one_hot(jnp.asarray(_t5_bucket(dist).astype(np.int8)), NUM_BUCKETS, dtype=F32)
    return jnp.einsum("rcb,bh->hrc", onehot, rel_bias, precision=lax.Precision.HIGHEST)


def _pack_layer(p, l):
    w_in = p["w_in"][l]
    pad = jnp.zeros((D_MODEL, N_IN - C_GATE - 32), F32)
    w_in_p = jnp.concatenate([w_in[:, 0:2048], w_in[:, 2056:3080], w_in[:, 3080:4616], w_in[:, 4640:5152],
                              w_in[:, 2048:2056], w_in[:, 4616:4640], pad], axis=1).astype(BF16)
    gb = p["mlstm_gate_b"][l]
    mlstm_gate_b = jnp.zeros((1, LANES), F32).at[0, GATE_MI:GATE_MI + 4].set(gb[0]).at[0, GATE_MF:GATE_MF + 4].set(gb[1])
    nsa_gate_b = jnp.zeros((1, LANES), F32).at[0, GATE_NG:GATE_NG + 3 * NSA_HEADS].set(p["nsa_gate_b"][l].reshape(-1))
    w1 = p["cmp_w1"][l]
    wcat = jnp.concatenate([w1[:, 0:CMP_STRIDE], w1[:, CMP_STRIDE:CMP_BLOCK]], axis=-1).astype(BF16)
    lam = lax.complex(p["s5_a_re"][l], p["s5_a_im"][l])
    lam_bar = jnp.exp(lam * jnp.exp(p["s5_log_step"][l])[:, None])
    b_bar = ((lam_bar - 1.0) / lam)[..., None] * lax.complex(p["s5_b_re"][l], p["s5_b_im"][l])
    eye = jnp.eye(S5_GROUPS, dtype=F32)
    bd_in = lambda m: jnp.einsum("gph,gk->ghkp", m, eye).reshape(D_S5, S5_CH)
    wb = jnp.concatenate([bd_in(b_bar.real), bd_in(b_bar.imag)], axis=1).astype(BF16)
    bd_out = lambda m: jnp.einsum("ghp,gk->gpkh", m, eye).reshape(S5_CH, D_S5)
    wc = jnp.concatenate([bd_out(p["s5_c_re"][l]), -bd_out(p["s5_c_im"][l])], axis=0).astype(BF16)
    half = lambda w: jnp.pad(w, ((0, 0), (0, D_FFP - D_FF)))
    w_up = p["ffn_w_up"][l]
    conv_w = p["ffn_conv_w"][l]
    conv_b = p["ffn_conv_b"][l][None, :]
    return dict(
        w_in=w_in_p, mlstm_gate_b=mlstm_gate_b, mlstm_norm_w=p["mlstm_norm_w"][l][None, :], nsa_gate_b=nsa_gate_b,
        wcat=wcat, cmp_b1=p["cmp_b1"][l], cmp_w2=p["cmp_w2"][l].astype(BF16),
        s5_wb=wb, s5_wc=wc, s5_lam_r=lam_bar.real.reshape(1, S5_CH), s5_lam_i=lam_bar.imag.reshape(1, S5_CH),
        s5_d=p["s5_d"][l][None, :], s5_glu_w=p["s5_glu_w"][l].astype(BF16), s5_glu_b=p["s5_glu_b"][l][None, :],
        w_out=p["w_out"][l].astype(BF16), ln1_w=p["ln1_w"][l][None, :], ln1_b=p["ln1_b"][l][None, :],
        w_up=jnp.concatenate([half(w_up[:, :D_FF]), half(w_up[:, D_FF:])], axis=1).astype(BF16),
        conv_w8=jnp.pad(jnp.concatenate([half(conv_w[:, :D_FF]), half(conv_w[:, D_FF:])], axis=1),
                        ((0, SUBLANES - CONV_W), (0, 0))),
        conv_b=jnp.concatenate([half(conv_b[:, :D_FF]), half(conv_b[:, D_FF:])], axis=1),
        w_down=jnp.pad(p["ffn_w_down"][l], ((0, D_FFP - D_FF), (0, 0))).astype(BF16),
        ln2_w=p["ln2_w"][l][None, :], ln2_b=p["ln2_b"][l][None, :],
    )


def _unpad_ff(x):
    return jnp.concatenate([x[..., :D_FF], x[..., D_FFP:D_FFP + D_FF]], axis=-1)


def _pad_ff(x):
    pad = [(0, 0)] * (x.ndim - 1) + [(0, D_FFP - D_FF)]
    return jnp.concatenate([jnp.pad(x[..., :D_FF], pad), jnp.pad(x[..., D_FF:], pad)], axis=-1)


def _time_major(x, b, t, t_use):
    x = x.reshape(b, t, -1)[:, :t_use].transpose(1, 0, 2)
    return jnp.pad(x, ((0, 0), (0, SUBLANES - b), (0, 0))).reshape(t_use * SUBLANES, -1)


def _batch_major(x, b, t, t_use):
    x = x.reshape(t_use, SUBLANES, -1)[:, :b].transpose(1, 0, 2)
    return jnp.pad(x, ((0, 0), (0, t - t_use), (0, 0))).reshape(b * t, -1)


def _mixer_tail(x2d, z, b, t, t_use, lw, o_mlstm, o_nsa, s5_state, conv_state8, tm):
    pad8 = lambda s: jnp.pad(s.reshape(b, S5_CH), ((0, SUBLANES - b), (0, 0)))
    u_tm = _time_major(z[:, C_SU:C_SU + D_S5], b, t, t_use)
    o_s5_tm, xr, xi = _s5(u_tm, lw["s5_wb"], lw["s5_wc"], lw["s5_lam_r"], lw["s5_lam_i"], lw["s5_d"],
                          lw["s5_glu_w"], lw["s5_glu_b"], pad8(s5_state[0]), pad8(s5_state[1]), lc=S5_LC)
    o_s5 = _batch_major(o_s5_tm, b, t, t_use)
    x1 = _wout_ln(o_mlstm.reshape(b * t, D_MLSTM), o_nsa.reshape(b * t, D_NSA), o_s5, x2d,
                  lw["w_out"], lw["ln1_w"], lw["ln1_b"], tm=256)
    up = _matmul(x1, lw["w_up"], tm=tm, tn=512)
    hgate = _convgate(up.reshape(b, t, 2 * D_FFP), conv_state8, lw["conv_w8"], lw["conv_b"], tt=256, tf=512)
    x2 = _down_ln(hgate.reshape(b * t, D_FFP), lw["w_down"], x1, lw["ln2_w"], lw["ln2_b"], tm=512, tk=1408)
    s5_new = (xr[:b].reshape(b, S5_GROUPS, S5_STATE), xi[:b].reshape(b, S5_GROUPS, S5_STATE))
    conv_new = _unpad_ff(up.reshape(b, t, 2 * D_FFP)[:, t_use - (CONV_W - 1):t_use])
    return x2, s5_new, conv_new


def _prompt_layer(x2d, b, t, lw, rel_bias):
    z = _matmul(x2d, lw["w_in"], tm=1024, tn=768)
    z3 = z.reshape(b, t, N_IN)
    zeros = lambda *s: jnp.zeros(s, F32)
    o_mlstm, c1, n1, m1 = _mlstm(z3, lw["mlstm_gate_b"], lw["mlstm_norm_w"],
                                 zeros(b, MLSTM_HEADS, HEAD_DIM, HEAD_DIM), zeros(b, MLSTM_HEADS, LANES),
                                 zeros(b, MLSTM_HEADS, LANES), L=MLSTM_L, lin=MLSTM_L, t_valid=t)
    kcvc = _cmp_finish(_cmp_project_prompt(z3, lw["wcat"]), lw["cmp_b1"], lw["cmp_w2"])
    ncp = t // CMP_STRIDE
    bias_c = _bias_table(rel_bias, np.arange(t)[:, None] - (np.arange(ncp) * CMP_STRIDE + CMP_BLOCK - 1)[None, :])
    ti = np.arange(TQ)[:, None] - np.arange(TK)[None, :]
    far = _bias_table(rel_bias, np.full((1, 1), FAR_DIST))
    tz = jnp.stack([_bias_table(rel_bias, d * TQ + ti) - far for d in range(N_TZ)])
    o_nsa = _nsa_prompt(z3, kcvc, bias_c, tz, lw["nsa_gate_b"])
    x2, s5_new, conv_new = _mixer_tail(
        x2d, z, b, t, t, lw, o_mlstm, o_nsa, (zeros(b, S5_GROUPS, S5_STATE), zeros(b, S5_GROUPS, S5_STATE)),
        zeros(b, SUBLANES, 2 * D_FFP), tm=1024)
    nkv = z3[:, :, C_NKV:C_NKV + N_KV_SLOTS * KV_HEADS * HEAD_DIM].reshape(b, t, N_KV_SLOTS, KV_HEADS, HEAD_DIM)
    wrows = min(WINDOW, t)
    state = (nkv[:, :, :4], nkv[:, t - wrows:, 4:], c1, n1[:, :, :], m1[:, :, 0], s5_new[0], s5_new[1], conv_new)
    return x2, state


def _sample_layer(x2d, b, tn, lw, rel_bias, layer, cache6, page_table, win2d, mlstm_state, s5_state, conv_state):
    npages = page_table.shape[1]
    pos0 = npages * PAGE_SIZE
    z = _matmul(x2d, lw["w_in"], tm=1024, tn=768)
    z8 = z.reshape(b, TS, N_IN)
    c0, n0, m0 = mlstm_state
    o_mlstm, c1, n1, m1 = _mlstm(z8, lw["mlstm_gate_b"], lw["mlstm_norm_w"], c0, n0,
                                 jnp.broadcast_to(m0[:, :, None], (b, MLSTM_HEADS, LANES)),
                                 L=LANES, lin=TS, t_valid=tn)
    n_chunks = (pos0 + tn) // CMP_STRIDE
    n_cmp = n_chunks - CMP_BLOCK // CMP_STRIDE + 1
    n_sel = -(-(pos0 + tn) // SEL_BLOCK)
    kcvc = _cmp_finish(_cmp_project_sample(cache6, page_table, layer, lw["wcat"]), lw["cmp_b1"], lw["cmp_w2"])
    qpos = pos0 + np.arange(TS)[:, None]
    bias_c = _bias_table(rel_bias, qpos - (np.arange(n_chunks) * CMP_STRIDE + CMP_BLOCK - 1)[None, :])
    o_c, sel = _nsa_sample_cmp(z8, kcvc, bias_c, pos0=pos0, n_cmp=n_cmp, n_sel=n_sel)
    kk = np.arange(PAGE_SIZE)[None, :]
    bias_tail = _bias_table(rel_bias, np.concatenate([qpos - (pos0 - PAGE_SIZE + kk), qpos - (pos0 + kk)], axis=1))
    bias_far = _bias_table(rel_bias, np.broadcast_to(qpos - (pos0 - 2 * PAGE_SIZE), (TS, PAGE_SIZE)))
    o_s = _nsa_sample_sel(z8, sel, bias_tail, bias_far, cache6, page_table, layer, pos0=pos0)
    wb = win2d.shape[1]
    wk = np.arange(wb)[None, :]
    bias_w = _bias_table(rel_bias, np.concatenate([qpos - (pos0 - wb + wk), qpos - (pos0 + kk)], axis=1))
    o_nsa = _nsa_sample_win(z8, win2d, bias_w, o_c, o_s, lw["nsa_gate_b"], pos0=pos0)
    conv_state8 = jnp.pad(_pad_ff(conv_state), ((0, 0), (SUBLANES - (CONV_W - 1), 0), (0, 0)))
    x2, s5_new, conv_new = _mixer_tail(x2d, z, b, TS, tn, lw, o_mlstm, o_nsa, s5_state, conv_state8, tm=1024)
    nkv = z8[:, :tn, C_NKV:C_NKV + N_KV_SLOTS * KV_HEADS * HEAD_DIM].reshape(b, tn, N_KV_SLOTS, KV_HEADS, HEAD_DIM)
    state = (nkv[:, :, :4], nkv[:, :, 4:], c1, n1, m1[:, :, 0], s5_new[0], s5_new[1], conv_new)
    return x2, state


def kernel(x_prompt, x_sample, cache_nsa_kv, cache_win_kv, state_mlstm_c, state_mlstm_n, state_mlstm_m,
           state_s5_re, state_s5_im, state_ffn_conv, page_table, w_in, mlstm_gate_b, mlstm_norm_w,
           nsa_gate_b, cmp_w1, cmp_b1, cmp_w2, rel_bias, s5_a_re, s5_a_im, s5_b_re, s5_b_im, s5_c_re,
           s5_c_im, s5_d, s5_log_step, s5_glu_w, s5_glu_b, w_out, ln1_w, ln1_b, ffn_w_up, ffn_conv_w,
           ffn_conv_b, ffn_w_down, ln2_w, ln2_b):
    params = dict(w_in=w_in, mlstm_gate_b=mlstm_gate_b, mlstm_norm_w=mlstm_norm_w, nsa_gate_b=nsa_gate_b,
                  cmp_w1=cmp_w1, cmp_b1=cmp_b1, cmp_w2=cmp_w2, s5_a_re=s5_a_re, s5_a_im=s5_a_im, s5_b_re=s5_b_re,
                  s5_b_im=s5_b_im, s5_c_re=s5_c_re, s5_c_im=s5_c_im, s5_d=s5_d, s5_log_step=s5_log_step,
                  s5_glu_w=s5_glu_w, s5_glu_b=s5_glu_b, w_out=w_out, ln1_w=ln1_w, ln1_b=ln1_b, ffn_w_up=ffn_w_up,
                  ffn_conv_w=ffn_conv_w, ffn_conv_b=ffn_conv_b, ffn_w_down=ffn_w_down, ln2_w=ln2_w, ln2_b=ln2_b)
    depth = w_in.shape[0]
    bp, tp, _ = x_prompt.shape
    bs, tn, _ = x_sample.shape
    assert tp % TK == 0 and tn < CMP_STRIDE and tn <= TS and bp <= SUBLANES and bs <= SUBLANES
    assert page_table.shape[1] % CMP_PAGES == 0 and cache_nsa_kv.shape[2] == PAGE_SIZE
    win2d = cache_win_kv.reshape(depth, bs, cache_win_kv.shape[2], 2 * KV_HEADS * HEAD_DIM)
    xp = x_prompt.reshape(bp * tp, D_MODEL)
    xs = jnp.pad(x_sample, ((0, 0), (0, TS - tn), (0, 0))).reshape(bs * TS, D_MODEL)
    p_states, s_states = [], []
    for l in range(depth):
        lw = _pack_layer(params, l)
        xp, sp = _prompt_layer(xp, bp, tp, lw, rel_bias)
        xs, ss = _sample_layer(xs, bs, tn, lw, rel_bias, l, cache_nsa_kv, page_table, win2d[l],
                               (state_mlstm_c[l], jnp.pad(state_mlstm_n[l], ((0, 0), (0, 0), (0, LANES - HEAD_DIM))),
                                state_mlstm_m[l]),
                               (state_s5_re[l], state_s5_im[l]), state_ffn_conv[l])
        p_states.append(sp)
        s_states.append(ss)
    stk = lambda states, i: jnp.stack([s[i] for s in states])
    y_prompt = xp.reshape(bp, tp, D_MODEL)
    y_sample = xs.reshape(bs, TS, D_MODEL)[:, :tn]
    return (y_prompt, y_sample,
            stk(p_states, 0), stk(s_states, 0), stk(p_states, 1), stk(s_states, 1),
            stk(p_states, 2), stk(s_states, 2), stk(p_states, 3), stk(s_states, 3), stk(p_states, 4), stk(s_states, 4),
            stk(p_states, 5), stk(s_states, 5), stk(p_states, 6), stk(s_states, 6), stk(p_states, 7), stk(s_states, 7))
```

```python
import functools
import math

import jax
import jax.numpy as jnp
import numpy as np
from jax import lax
from jax.experimental import pallas as pl
from jax.experimental.pallas import tpu as pltpu

F32 = jnp.float32
BF16 = jnp.bfloat16
NEG_INF = float("-inf")
M_INIT = -1e30
MASKED = -1e30

D_MODEL = 2048
PAGE_SIZE = 128
D_MLSTM = D_MODEL // 4
D_NSA = D_MODEL // 2
D_S5 = D_MODEL - D_MLSTM - D_NSA
HEAD_DIM = 128
MLSTM_HEADS = D_MLSTM // HEAD_DIM
NSA_HEADS = D_NSA // HEAD_DIM
KV_HEADS = 2
GROUP = NSA_HEADS // KV_HEADS
N_KV_SLOTS = 6
CMP_BLOCK = 32
CMP_STRIDE = 16
SEL_BLOCK = 64
SEL_TOPK = 16
FORCE_SCORE = 1e4
WINDOW = 512
S5_GROUP_WIDTH = 16
S5_GROUPS = D_S5 // S5_GROUP_WIDTH
S5_STATE = 64
S5_CH = S5_GROUPS * S5_STATE
NUM_BUCKETS = 32
MAX_DISTANCE = 128
D_FF = ((8 * D_MODEL // 3 + 127) // 128) * 128
CONV_W = 3
LN_EPS = 1e-5
DEPTH = 2
DEEPNORM_ALPHA = (2 * DEPTH) ** 0.25
QK_SCALE = HEAD_DIM ** -0.5

LANES = 128
SUBLANES = 8
V7X_VMEM_LIMIT = 56 * 2 ** 20

C_MQ, C_MK, C_MV, C_MO = 0, 512, 1024, 1536
C_NQ = 2048
C_NKV = 3072
C_SU = 4608
C_GATE = 5120
N_IN = 5376
GATE_MI, GATE_MF, GATE_NG = 0, 4, 8
D_FFP = 5632
MLSTM_L = 256
TQ = 128
TK = 256
N_TZ = 3
S5_LC = 128
FFN_TT = 512
FFN_TF = 512
CMP_PAGES = 32
SEL_PAGES = 32
SEL_TILE = 16
SEL_SHIFT = SEL_BLOCK.bit_length() - 1


def _cparams(*sem):
    return pltpu.CompilerParams(dimension_semantics=sem, vmem_limit_bytes=V7X_VMEM_LIMIT)


def _dot(a, b):
    return jnp.dot(a, b, preferred_element_type=F32)


def _dot_nt(a, b):
    return lax.dot_general(a, b, (((1,), (1,)), ((), ())), preferred_element_type=F32)


def _layer_norm(y, w, b):
    mu = jnp.mean(y, axis=-1, keepdims=True)
    d = y - mu
    var = jnp.mean(d * d, axis=-1, keepdims=True)
    return d * lax.rsqrt(var + LN_EPS) * w + b


def _mm_kernel(x_ref, w_ref, o_ref):
    o_ref[...] = _dot(x_ref[...].astype(BF16), w_ref[...]).astype(o_ref.dtype)


def _matmul(x, w, *, tm, tn):
    m, k = x.shape
    n = w.shape[1]
    tm = min(tm, m)
    return pl.pallas_call(
        _mm_kernel,
        grid=(m // tm, n // tn),
        in_specs=[pl.BlockSpec((tm, k), lambda i, j: (i, 0)), pl.BlockSpec((k, tn), lambda i, j: (0, j))],
        out_specs=pl.BlockSpec((tm, tn), lambda i, j: (i, j)),
        out_shape=jax.ShapeDtypeStruct((m, n), F32),
        compiler_params=_cparams("parallel", "parallel"),
        name="proj",
    )(x, w)


KV4_GROUPS = 4 * KV_HEADS
WIN_GROUPS = 2 * KV_HEADS


def _proj_in_kernel(x_ref, w_ref, z_ref, kv_ref, win_ref, *, tn):
    j = pl.program_id(1)
    acc = _dot(x_ref[...].astype(BF16), w_ref[...])
    z_ref[...] = acc
    tm = acc.shape[0]
    per_tile = tn // HEAD_DIM
    first_tile = C_NKV // tn
    for tile in range(first_tile, first_tile + N_KV_SLOTS * KV_HEADS // per_tile):
        @pl.when(j == tile)
        def _(tile=tile):
            for c in range(per_tile):
                grp = (tile - first_tile) * per_tile + c
                val = acc[:, c * HEAD_DIM:(c + 1) * HEAD_DIM]
                if grp < KV4_GROUPS:
                    kv_ref[pl.ds(grp, tm, stride=KV4_GROUPS), :] = val
                else:
                    win_ref[pl.ds(grp - KV4_GROUPS, tm, stride=WIN_GROUPS), :] = val


def _proj_in(x, w, *, tm, tn):
    m, k = x.shape
    n = w.shape[1]
    assert C_NKV % tn == 0 and (N_KV_SLOTS * KV_HEADS * HEAD_DIM) % tn == 0
    return pl.pallas_call(
        functools.partial(_proj_in_kernel, tn=tn),
        grid=(m // tm, n // tn),
        in_specs=[pl.BlockSpec((tm, k), lambda i, j: (i, 0)), pl.BlockSpec((k, tn), lambda i, j: (0, j))],
        out_specs=[pl.BlockSpec((tm, tn), lambda i, j: (i, j)),
                   pl.BlockSpec((tm * KV4_GROUPS, HEAD_DIM), lambda i, j: (i, 0)),
                   pl.BlockSpec((tm * WIN_GROUPS, HEAD_DIM), lambda i, j: (i, 0))],
        out_shape=[jax.ShapeDtypeStruct((m, n), F32),
                   jax.ShapeDtypeStruct((m * KV4_GROUPS, HEAD_DIM), F32),
                   jax.ShapeDtypeStruct((m * WIN_GROUPS, HEAD_DIM), F32)],
        compiler_params=_cparams("parallel", "arbitrary"),
        name="proj_in",
    )(x, w)


def _wout_kernel(om_ref, on_ref, os_ref, x_ref, w_ref, lw_ref, lb_ref, o_ref):
    acc = _dot(om_ref[...].astype(BF16), w_ref[0:D_MLSTM, :])
    acc += _dot(on_ref[...].astype(BF16), w_ref[D_MLSTM:D_MLSTM + D_NSA, :])
    acc += _dot(os_ref[...].astype(BF16), w_ref[D_MLSTM + D_NSA:D_MODEL, :])
    o_ref[...] = _layer_norm(DEEPNORM_ALPHA * x_ref[...] + acc, lw_ref[...], lb_ref[...])


def _wout_ln(om, on, os_, x, w, lw, lb, *, tm):
    m = x.shape[0]
    tm = min(tm, m)
    row = lambda width: pl.BlockSpec((tm, width), lambda i: (i, 0))
    full = lambda a: pl.BlockSpec(a.shape, lambda i: (0, 0))
    return pl.pallas_call(
        _wout_kernel,
        grid=(m // tm,),
        in_specs=[row(D_MLSTM), row(D_NSA), row(D_S5), row(D_MODEL), full(w), full(lw), full(lb)],
        out_specs=row(D_MODEL),
        out_shape=jax.ShapeDtypeStruct((m, D_MODEL), F32),
        compiler_params=_cparams("parallel"),
        name="wout_ln",
    )(om, on, os_, x, w, lw, lb)


def _down_kernel(h_ref, w_ref, x_ref, lw_ref, lb_ref, o_ref, acc_ref, *, nk):
    k = pl.program_id(1)

    @pl.when(k == 0)
    def _():
        acc_ref[...] = jnp.zeros_like(acc_ref)

    acc_ref[...] += _dot(h_ref[...], w_ref[...])

    @pl.when(k == nk - 1)
    def _():
        o_ref[...] = _layer_norm(DEEPNORM_ALPHA * x_ref[...] + acc_ref[...], lw_ref[...], lb_ref[...])


def _down_ln(h, w, x, lw, lb, *, tm, tk):
    m, kk = h.shape
    tm = min(tm, m)
    nk = kk // tk
    return pl.pallas_call(
        functools.partial(_down_kernel, nk=nk),
        grid=(m // tm, nk),
        in_specs=[
            pl.BlockSpec((tm, tk), lambda i, k: (i, k)),
            pl.BlockSpec((tk, D_MODEL), lambda i, k: (k, 0)),
            pl.BlockSpec((tm, D_MODEL), lambda i, k: (i, 0)),
            pl.BlockSpec((1, D_MODEL), lambda i, k: (0, 0)),
            pl.BlockSpec((1, D_MODEL), lambda i, k: (0, 0)),
        ],
        out_specs=pl.BlockSpec((tm, D_MODEL), lambda i, k: (i, 0)),
        out_shape=jax.ShapeDtypeStruct((m, D_MODEL), F32),
        scratch_shapes=[pltpu.VMEM((tm, D_MODEL), F32)],
        compiler_params=_cparams("parallel", "arbitrary"),
        name="down_ln",
    )(h, w, x, lw, lb)


def _convgate_kernel(a_ref, g_ref, ha_ref, hg_ref, sa_ref, sg_ref, wa_ref, wg_ref, ba_ref, bg_ref, o_ref):
    first = pl.program_id(1) == 0
    tt = a_ref.shape[0]
    rows = lax.broadcasted_iota(jnp.int32, (tt, 1), 0)

    def conv(cur_ref, halo_ref, st_ref, w_ref, b_ref):
        cur = cur_ref[...]
        prev = jnp.where(first, st_ref[...], halo_ref[...])
        p1 = prev[7:8, :]
        p2 = prev[6:7, :]
        x1 = jnp.where(rows == 0, p1, pltpu.roll(cur, 1, 0))
        x2 = jnp.where(rows == 0, p2, jnp.where(rows == 1, p1, pltpu.roll(cur, 2, 0)))
        w = w_ref[...]
        return b_ref[...] + w[0:1, :] * x2 + w[1:2, :] * x1 + w[2:3, :] * cur

    a = conv(a_ref, ha_ref, sa_ref, wa_ref, ba_ref)
    g = conv(g_ref, hg_ref, sg_ref, wg_ref, bg_ref)
    o_ref[...] = (a * jax.nn.sigmoid(a) * g).astype(o_ref.dtype)


def _convgate(up, state8, conv_w8, conv_b, *, tt, tf):
    b, t, _ = up.shape
    tt = min(tt, t)
    nf = D_FFP // tf
    hb = tt // SUBLANES
    cur_a = pl.BlockSpec((None, tt, tf), lambda bi, ti, fi: (bi, ti, fi))
    cur_g = pl.BlockSpec((None, tt, tf), lambda bi, ti, fi: (bi, ti, fi + nf))
    halo_a = pl.BlockSpec((None, SUBLANES, tf), lambda bi, ti, fi: (bi, jnp.maximum(ti * hb - 1, 0), fi))
    halo_g = pl.BlockSpec((None, SUBLANES, tf), lambda bi, ti, fi: (bi, jnp.maximum(ti * hb - 1, 0), fi + nf))
    st_a = pl.BlockSpec((None, SUBLANES, tf), lambda bi, ti, fi: (bi, 0, fi))
    st_g = pl.BlockSpec((None, SUBLANES, tf), lambda bi, ti, fi: (bi, 0, fi + nf))
    w_a = pl.BlockSpec((SUBLANES, tf), lambda bi, ti, fi: (0, fi))
    w_g = pl.BlockSpec((SUBLANES, tf), lambda bi, ti, fi: (0, fi + nf))
    b_a = pl.BlockSpec((1, tf), lambda bi, ti, fi: (0, fi))
    b_g = pl.BlockSpec((1, tf), lambda bi, ti, fi: (0, fi + nf))
    return pl.pallas_call(
        _convgate_kernel,
        grid=(b, t // tt, nf),
        in_specs=[cur_a, cur_g, halo_a, halo_g, st_a, st_g, w_a, w_g, b_a, b_g],
        out_specs=pl.BlockSpec((None, tt, tf), lambda bi, ti, fi: (bi, ti, fi)),
        out_shape=jax.ShapeDtypeStruct((b, t, D_FFP), BF16),
        compiler_params=_cparams("parallel", "parallel", "parallel"),
        name="convgate",
    )(up, up, up, up, state8, state8, conv_w8, conv_w8, conv_b, conv_b)


def _ffn_kernel(x_ref, xh_ref, wa_ref, wg_ref, wd_ref, cwa_ref, cwg_ref, cba_ref, cbg_ref, lw_ref, lb_ref,
                o_ref, sa_ref, sg_ref, acc_ref, *, nf):
    first = pl.program_id(1) == 0
    f = pl.program_id(2)
    tt = x_ref.shape[0]

    @pl.when(f == 0)
    def _():
        acc_ref[...] = jnp.zeros_like(acc_ref)

    x = x_ref[...]
    xe = jnp.concatenate([xh_ref[...], x], axis=0).astype(BF16)
    rows = lax.broadcasted_iota(jnp.int32, (tt + SUBLANES, 1), 0)
    before_start = first & (rows < SUBLANES)

    def branch(w_ref, cw_ref, cb_ref, s_ref):
        up = jnp.where(before_start, 0.0, _dot(xe, w_ref[...]))
        s_ref[...] = up[tt:tt + SUBLANES, :]
        w = cw_ref[...]
        x1 = pltpu.roll(up, 1, 0)[SUBLANES:, :]
        x2 = pltpu.roll(up, 2, 0)[SUBLANES:, :]
        return cb_ref[...] + w[0:1, :] * x2 + w[1:2, :] * x1 + w[2:3, :] * up[SUBLANES:, :]

    a = branch(wa_ref, cwa_ref, cba_ref, sa_ref)
    g = branch(wg_ref, cwg_ref, cbg_ref, sg_ref)
    acc_ref[...] += _dot((a * jax.nn.sigmoid(a) * g).astype(BF16), wd_ref[...])

    @pl.when(f == nf - 1)
    def _():
        o_ref[...] = _layer_norm(DEEPNORM_ALPHA * x + acc_ref[...], lw_ref[...], lb_ref[...])


def _ffn_prompt(x3, w_up, w_down, conv_w8, conv_b, lw, lb, *, tt, tf):
    b, t, _ = x3.shape
    nf = D_FFP // tf
    hb = tt // SUBLANES
    half_a = lambda rows: pl.BlockSpec((rows, tf), lambda bi, ti, fi: (0, fi))
    half_g = lambda rows: pl.BlockSpec((rows, tf), lambda bi, ti, fi: (0, fi + nf))
    vec = pl.BlockSpec((1, D_MODEL), lambda bi, ti, fi: (0, 0))
    tail = pl.BlockSpec((None, SUBLANES, tf), lambda bi, ti, fi: (bi, 0, fi))
    return pl.pallas_call(
        functools.partial(_ffn_kernel, nf=nf),
        grid=(b, t // tt, nf),
        in_specs=[pl.BlockSpec((None, tt, D_MODEL), lambda bi, ti, fi: (bi, ti, 0)),
                  pl.BlockSpec((None, SUBLANES, D_MODEL), lambda bi, ti, fi: (bi, jnp.maximum(ti * hb - 1, 0), 0)),
                  half_a(D_MODEL), half_g(D_MODEL),
                  pl.BlockSpec((tf, D_MODEL), lambda bi, ti, fi: (fi, 0)),
                  half_a(SUBLANES), half_g(SUBLANES), half_a(1), half_g(1), vec, vec],
        out_specs=[pl.BlockSpec((None, tt, D_MODEL), lambda bi, ti, fi: (bi, ti, 0)), tail, tail],
        out_shape=[jax.ShapeDtypeStruct((b, t, D_MODEL), F32),
                   jax.ShapeDtypeStruct((b, SUBLANES, D_FFP), F32),
                   jax.ShapeDtypeStruct((b, SUBLANES, D_FFP), F32)],
        scratch_shapes=[pltpu.VMEM((tt, D_MODEL), F32)],
        compiler_params=_cparams("parallel", "arbitrary", "arbitrary"),
        name="ffn_prompt",
    )(x3, x3, w_up, w_up, w_down, conv_w8, conv_w8, conv_b, conv_b, lw, lb)


def _cumsum_rows(x):
    n = x.shape[0]
    rows = lax.broadcasted_iota(jnp.int32, (n, 1), 0)
    d = 1
    while d < n:
        x = x + jnp.where(rows >= d, pltpu.roll(x, d, 0), 0.0)
        d *= 2
    return x


def _log_sigmoid(x):
    return jnp.minimum(x, 0.0) - jnp.log1p(jnp.exp(-jnp.abs(x)))


def _mlstm_kernel(q_ref, k_ref, v_ref, og_ref, g_ref, gb_ref, nw_ref, c0_ref, n0_ref, m0_ref,
                  out_ref, c_ref, n_ref, m_ref, *, L, t_valid):
    ci = pl.program_id(1)
    lin = q_ref.shape[0]

    @pl.when(ci == 0)
    def _():
        c_ref[...] = c0_ref[...]
        n_ref[...] = n0_ref[...]
        m_ref[...] = m0_ref[...]

    def rows_of(ref):
        x = ref[...]
        if lin < L:
            x = jnp.concatenate([x, jnp.zeros((L - lin, x.shape[1]), x.dtype)], axis=0)
        return x

    rows = lax.broadcasted_iota(jnp.int32, (L, 1), 0)
    valid = (ci * L + rows) < t_valid
    pre = rows_of(g_ref) + gb_ref[...]
    lf = jnp.where(valid, _log_sigmoid(pre), 0.0)
    ig = jnp.where(valid, pre, NEG_INF)
    bcum = _cumsum_rows(lf)
    dt = (pltpu.roll(ig, GATE_MF - GATE_MI, 1) - bcum).T
    q_all, k_all, v_all, og_all = rows_of(q_ref), rows_of(k_ref), rows_of(v_ref), rows_of(og_ref)
    tri = lax.broadcasted_iota(jnp.int32, (L, L), 0) >= lax.broadcasted_iota(jnp.int32, (L, L), 1)
    nw = nw_ref[...]

    for h in range(MLSTM_HEADS):
        sl = slice(h * HEAD_DIM, (h + 1) * HEAD_DIM)
        q = q_all[:, sl]
        k = k_all[:, sl] * QK_SCALE
        v = v_all[:, sl]
        qb, kb, vb = q.astype(BF16), k.astype(BF16), v.astype(BF16)
        b_col = bcum[:, GATE_MF + h:GATE_MF + h + 1]
        ig_col = ig[:, GATE_MI + h:GATE_MI + h + 1]
        d_row = dt[GATE_MF + h:GATE_MF + h + 1, :]
        c_prev = c_ref[h]
        n_prev = n_ref[h:h + 1, :]
        m_prev = m_ref[h:h + 1, 0:1]

        dmat = jnp.where(tri, b_col + d_row, NEG_INF)
        g_col = b_col + m_prev
        m_row = jnp.maximum(jnp.max(dmat, axis=1, keepdims=True), g_col)
        a = jnp.exp(dmat - m_row) * _dot_nt(qb, kb)
        w_inter = jnp.exp(g_col - m_row)
        num = _dot(a.astype(BF16), vb) + w_inter * _dot(qb, c_prev.astype(BF16))
        den = jnp.sum(a, axis=1, keepdims=True) + w_inter * jnp.sum(q * n_prev, axis=1, keepdims=True)
        hid = num / jnp.maximum(jnp.abs(den), jnp.exp(-m_row))
        mu = jnp.mean(hid, axis=1, keepdims=True)
        dlt = hid - mu
        var = jnp.mean(dlt * dlt, axis=1, keepdims=True)
        hn = dlt * lax.rsqrt(var + LN_EPS) * nw[:, sl]
        res = jax.nn.sigmoid(og_all[:, sl]) * hn
        out_ref[:, sl] = res[0:lin, :]

        f_tot = b_col[L - 1:L, :]
        w_s = f_tot - b_col + ig_col
        m_new = jnp.maximum(f_tot + m_prev, jnp.max(w_s, axis=0, keepdims=True))
        ws = jnp.exp(w_s - m_new)
        decay = jnp.exp(f_tot + m_prev - m_new)
        c_ref[h] = decay * c_prev + _dot(k.T.astype(BF16), (ws * v).astype(BF16))
        n_ref[h:h + 1, :] = decay * n_prev + jnp.sum(ws * k, axis=0, keepdims=True)
        m_ref[h:h + 1, :] = jnp.broadcast_to(m_new, (1, LANES))


def _mlstm(z, gate_b, norm_w, c0, n0, m0, *, L, lin, t_valid):
    b, tz, _ = z.shape
    nchunks = tz // lin
    qblk = lambda col: pl.BlockSpec((None, lin, D_MLSTM), lambda bi, ci: (bi, ci, col // D_MLSTM))
    st4 = pl.BlockSpec((None, MLSTM_HEADS, HEAD_DIM, HEAD_DIM), lambda bi, ci: (bi, 0, 0, 0))
    st3 = pl.BlockSpec((None, MLSTM_HEADS, LANES), lambda bi, ci: (bi, 0, 0))
    return pl.pallas_call(
        functools.partial(_mlstm_kernel, L=L, t_valid=t_valid),
        grid=(b, nchunks),
        in_specs=[qblk(C_MQ), qblk(C_MK), qblk(C_MV), qblk(C_MO),
                  pl.BlockSpec((None, lin, LANES), lambda bi, ci: (bi, ci, C_GATE // LANES)),
                  pl.BlockSpec((1, LANES), lambda bi, ci: (0, 0)),
                  pl.BlockSpec((1, D_MLSTM), lambda bi, ci: (0, 0)),
                  st4, st3, st3],
        out_specs=[pl.BlockSpec((None, lin, D_MLSTM), lambda bi, ci: (bi, ci, 0)), st4, st3, st3],
        out_shape=[jax.ShapeDtypeStruct((b, tz, D_MLSTM), F32),
                   jax.ShapeDtypeStruct((b, MLSTM_HEADS, HEAD_DIM, HEAD_DIM), F32),
                   jax.ShapeDtypeStruct((b, MLSTM_HEADS, LANES), F32),
                   jax.ShapeDtypeStruct((b, MLSTM_HEADS, LANES), F32)],
        compiler_params=_cparams("parallel", "arbitrary"),
        name="mlstm",
    )(z, z, z, z, z, gate_b, norm_w, c0, n0, m0)


S5_IN_TILES = S5_CH // 256
S5_IN_FEATS = D_S5 // S5_IN_TILES
S5_OUT_TILES = D_S5 // 256
S5_OUT_CH = S5_CH // S5_OUT_TILES


def _s5_kernel(u_ref, wb_ref, wc_ref, lr_ref, li_ref, d_ref, gw_ref, gb_ref, x0r_ref, x0i_ref,
               o_ref, xr_ref, xi_ref, sr_ref, si_ref, *, lc, rb):
    @pl.when(pl.program_id(0) == 0)
    def _():
        xr_ref[...] = x0r_ref[...]
        xi_ref[...] = x0i_ref[...]

    u = u_ref[...]
    ub = u.astype(BF16)
    for c in range(S5_IN_TILES):
        bu = _dot(ub[:, c * S5_IN_FEATS:(c + 1) * S5_IN_FEATS], wb_ref[c])
        sr_ref[:, c * 256:(c + 1) * 256] = bu[:, 0:256]
        si_ref[:, c * 256:(c + 1) * 256] = bu[:, 256:512]
    lam_r = jnp.broadcast_to(lr_ref[...], (SUBLANES, S5_CH))
    lam_i = jnp.broadcast_to(li_ref[...], (SUBLANES, S5_CH))
    row8 = lax.broadcasted_iota(jnp.int32, (SUBLANES, 1), 0)
    per_group = SUBLANES // rb

    def group(i, carry):
        xr, xi = carry
        r0 = pl.multiple_of(i * SUBLANES, SUBLANES)
        br = sr_ref[pl.ds(r0, SUBLANES), :]
        bi = si_ref[pl.ds(r0, SUBLANES), :]
        out_r, out_i = br, bi
        for k in range(per_group):
            nr = lam_r * xr - lam_i * xi + br
            ni = lam_r * xi + lam_i * xr + bi
            here = (row8 >= k * rb) & (row8 < (k + 1) * rb)
            out_r = jnp.where(here, nr, out_r)
            out_i = jnp.where(here, ni, out_i)
            if per_group > 1:
                xr = pltpu.roll(nr, rb, 0)
                xi = pltpu.roll(ni, rb, 0)
            else:
                xr, xi = nr, ni
        sr_ref[pl.ds(r0, SUBLANES), :] = out_r
        si_ref[pl.ds(r0, SUBLANES), :] = out_i
        return xr, xi

    xr, xi = lax.fori_loop(0, lc * rb // SUBLANES, group, (xr_ref[...], xi_ref[...]))
    xr_ref[...] = xr
    xi_ref[...] = xi
    for j in range(S5_OUT_TILES):
        ch = slice(j * S5_OUT_CH, (j + 1) * S5_OUT_CH)
        cols = slice(j * 256, (j + 1) * 256)
        y = _dot(sr_ref[:, ch].astype(BF16), wc_ref[0, j]) + _dot(si_ref[:, ch].astype(BF16), wc_ref[1, j])
        o_ref[:, cols] = y + d_ref[:, cols] * u[:, cols]
    zz = jax.nn.gelu(o_ref[...])
    o_ref[...] = zz * jax.nn.sigmoid(_dot(zz.astype(BF16), gw_ref[...]) + gb_ref[...])


def _s5(u_tm, wb, wc, lam_r, lam_i, d, glu_w, glu_b, x0r, x0i, *, lc, rb):
    rows = u_tm.shape[0]
    t = rows // rb
    lc = min(lc, t)
    full = lambda a: pl.BlockSpec(a.shape, lambda i: (0,) * a.ndim)
    st = pl.BlockSpec((SUBLANES, S5_CH), lambda i: (0, 0))
    return pl.pallas_call(
        functools.partial(_s5_kernel, lc=lc, rb=rb),
        grid=(t // lc,),
        in_specs=[pl.BlockSpec((lc * rb, D_S5), lambda i: (i, 0)),
                  full(wb), full(wc), full(lam_r), full(lam_i), full(d), full(glu_w), full(glu_b), st, st],
        out_specs=[pl.BlockSpec((lc * rb, D_S5), lambda i: (i, 0)), st, st],
        out_shape=[jax.ShapeDtypeStruct((rows, D_S5), F32),
                   jax.ShapeDtypeStruct((SUBLANES, S5_CH), F32),
                   jax.ShapeDtypeStruct((SUBLANES, S5_CH), F32)],
        scratch_shapes=[pltpu.VMEM((lc * rb, S5_CH), F32), pltpu.VMEM((lc * rb, S5_CH), F32)],
        compiler_params=_cparams("arbitrary"),
        name="s5",
    )(u_tm, wb, wc, lam_r, lam_i, d, glu_w, glu_b, x0r, x0i)


N_CMB = 2 * KV_HEADS


def _cmp_project(rows_of, w_ref, o_ref):
    nch = o_ref.shape[0]
    for cmb in range(N_CMB):
        slot = cmb // KV_HEADS
        rows_ref = rows_of(cmb)
        acc = jnp.zeros((nch, 2 * HEAD_DIM), F32)
        for l in range(CMP_STRIDE):
            x = rows_ref[pl.ds(l, nch, stride=CMP_STRIDE), :]
            acc += _dot(x.astype(BF16), w_ref[slot, l])
        o_ref[:, cmb * 2 * HEAD_DIM:(cmb + 1) * 2 * HEAD_DIM] = acc


def _cmp_p_kernel(r0_ref, r1_ref, r2_ref, r3_ref, w_ref, o_ref):
    rows = (r0_ref, r1_ref, r2_ref, r3_ref)
    _cmp_project(lambda cmb: rows[cmb], w_ref, o_ref)


def _cmp_project_prompt(z, wcat):
    b, t, _ = z.shape
    nch = t // CMP_STRIDE
    rows = lambda cmb: pl.BlockSpec((None, t, HEAD_DIM), lambda bi: (bi, 0, C_NKV // HEAD_DIM + cmb))
    return pl.pallas_call(
        _cmp_p_kernel,
        grid=(b,),
        in_specs=[rows(cmb) for cmb in range(N_CMB)] + [pl.BlockSpec(wcat.shape, lambda bi: (0, 0, 0, 0))],
        out_specs=pl.BlockSpec((None, nch, N_CMB * 2 * HEAD_DIM), lambda bi: (bi, 0, 0)),
        out_shape=jax.ShapeDtypeStruct((b, nch, N_CMB * 2 * HEAD_DIM), F32),
        compiler_params=_cparams("parallel"),
        name="cmp_project_prompt",
    )(z, z, z, z, wcat)


def _cmp_s_kernel(pt_ref, *refs):
    del pt_ref
    page_refs = [r.reshape(PAGE_SIZE * KV_HEADS, HEAD_DIM) for r in refs[:2 * CMP_PAGES]]
    w_ref, o_ref = refs[2 * CMP_PAGES:]
    per_page = PAGE_SIZE // CMP_STRIDE
    for slot in range(2):
        for g in range(KV_HEADS):
            cmb = slot * KV_HEADS + g
            acc = jnp.zeros((o_ref.shape[0], 2 * HEAD_DIM), F32)
            for l in range(CMP_STRIDE):
                x = jnp.concatenate(
                    [page_refs[slot * CMP_PAGES + p][pl.ds(l * KV_HEADS + g, per_page, stride=CMP_STRIDE * KV_HEADS), :]
                     for p in range(CMP_PAGES)], axis=0)
                acc += _dot(x.astype(BF16), w_ref[slot, l])
            o_ref[:, cmb * 2 * HEAD_DIM:(cmb + 1) * 2 * HEAD_DIM] = acc


def _page_spec(layer, slot, p, pages_per_step, grid_rank):
    def index_map(*idx):
        bi, si, pt = idx[0], idx[grid_rank - 1], idx[grid_rank]
        return (layer, pt[bi, si * pages_per_step + p], 0, slot, 0, 0)

    return pl.BlockSpec((None, None, PAGE_SIZE, None, KV_HEADS, HEAD_DIM), index_map)


def _cmp_project_sample(cache6, page_table, layer, wcat):
    b, npages = page_table.shape
    steps = npages // CMP_PAGES
    nch = CMP_PAGES * PAGE_SIZE // CMP_STRIDE
    pages = [_page_spec(layer, slot, p, CMP_PAGES, 2) for slot in range(2) for p in range(CMP_PAGES)]

    grid_spec = pltpu.PrefetchScalarGridSpec(
        num_scalar_prefetch=1,
        grid=(b, steps),
        in_specs=pages + [pl.BlockSpec(wcat.shape, lambda bi, si, pt: (0, 0, 0, 0))],
        out_specs=pl.BlockSpec((None, nch, N_CMB * 2 * HEAD_DIM), lambda bi, si, pt: (bi, si, 0)),
    )
    return pl.pallas_call(
        _cmp_s_kernel,
        grid_spec=grid_spec,
        out_shape=jax.ShapeDtypeStruct((b, steps * nch, N_CMB * 2 * HEAD_DIM), F32),
        compiler_params=_cparams("parallel", "arbitrary"),
        name="cmp_project_sample",
    )(page_table, *([cache6] * (2 * CMP_PAGES)), wcat)


def _cmp_fin_kernel(p_ref, b1_ref, w2_ref, o_ref):
    nch = p_ref.shape[0]
    for cmb in range(N_CMB):
        slot = cmb // KV_HEADS
        c0 = cmb * 2 * HEAD_DIM
        first = p_ref[:, c0:c0 + HEAD_DIM]
        second = pltpu.roll(p_ref[:, c0 + HEAD_DIM:c0 + 2 * HEAD_DIM], nch - 1, 0)
        hid = b1_ref[slot:slot + 1, :] + first + second
        o_ref[cmb] = _dot(jax.nn.gelu(hid).astype(BF16), w2_ref[slot])


def _cmp_finish(p, b1, w2):
    b, nch, _ = p.shape
    return pl.pallas_call(
        _cmp_fin_kernel,
        grid=(b,),
        in_specs=[pl.BlockSpec((None, nch, p.shape[2]), lambda bi: (bi, 0, 0)),
                  pl.BlockSpec(b1.shape, lambda bi: (0, 0)),
                  pl.BlockSpec(w2.shape, lambda bi: (0, 0, 0))],
        out_specs=pl.BlockSpec((None, N_CMB, nch, HEAD_DIM), lambda bi: (bi, 0, 0, 0)),
        out_shape=jax.ShapeDtypeStruct((b, N_CMB, nch, HEAD_DIM), F32),
        compiler_params=_cparams("parallel"),
        name="cmp_finish",
    )(p, b1, w2)


def _softmax_init(m_scr, l_scr, acc_scr):
    m_scr[...] = jnp.full(m_scr.shape, M_INIT, F32)
    l_scr[...] = jnp.zeros(l_scr.shape, F32)
    acc_scr[...] = jnp.zeros(acc_scr.shape, F32)


def _softmax_update(s, mask, vb, m_scr, l_scr, acc_scr):
    s = jnp.where(mask, s, NEG_INF)
    m_prev = m_scr[...]
    m_new = jnp.maximum(m_prev, jnp.max(s, axis=1, keepdims=True))
    alpha = jnp.exp(m_prev - m_new)
    p = jnp.exp(s - m_new)
    l_scr[...] = alpha * l_scr[...] + jnp.sum(p, axis=1, keepdims=True)
    acc_scr[...] = alpha * acc_scr[...] + _dot(p.astype(BF16), vb)
    m_scr[...] = m_new


def _softmax_result(l_scr, acc_scr):
    l = l_scr[...]
    return acc_scr[...] / jnp.where(l > 0, l, 1.0)


def _masked_probs(s, mask):
    s = jnp.where(mask, s, NEG_INF)
    m = jnp.max(s, axis=1, keepdims=True)
    m = jnp.where(m > NEG_INF, m, 0.0)
    p = jnp.exp(s - m)
    den = jnp.sum(p, axis=1, keepdims=True)
    return p / jnp.where(den > 0, den, 1.0)


def _stack_heads(q):
    return jnp.concatenate([q[:, j * HEAD_DIM:(j + 1) * HEAD_DIM] for j in range(GROUP)], axis=0)


def _block_importance(pc, n_cmp, n_sel):
    ncp = pc.shape[1]
    nsp = -(-n_sel // LANES) * LANES
    c_start = lax.broadcasted_iota(jnp.int32, (ncp, nsp), 0) * CMP_STRIDE
    s_start = lax.broadcasted_iota(jnp.int32, (ncp, nsp), 1) * SEL_BLOCK
    overlap = ((c_start < s_start + SEL_BLOCK) & (c_start + CMP_BLOCK > s_start)
               & (c_start < n_cmp * CMP_STRIDE)).astype(F32)
    return jnp.dot(pc, overlap, preferred_element_type=F32, precision=lax.Precision.HIGHEST)


def _force_blocks(imp, blk, cur):
    forced = (blk == 0) | (blk == cur) | (blk == cur - 1)
    imp = jnp.where(forced, FORCE_SCORE, imp)
    return jnp.where(blk > cur, -1.0, imp)


def _select_blocks_cols(pc, cur, n_cmp, n_sel):
    imp = _block_importance(pc, n_cmp, n_sel)
    blk = lax.broadcasted_iota(jnp.int32, (1, imp.shape[1]), 1)
    imp = _force_blocks(imp, blk, cur)

    def body(sp, count):
        col = jnp.sum(jnp.where(blk == sp, imp, 0.0), axis=1, keepdims=True)
        ahead = (col > imp) | ((col == imp) & (sp < blk))
        return count + ahead.astype(F32)

    count = lax.fori_loop(0, n_sel, body, jnp.zeros(imp.shape, F32), unroll=8)
    return ((count < min(SEL_TOPK, n_sel)) & (blk < n_sel)).astype(F32)


def _select_blocks_rows(pc, cur_row, n_cmp, n_sel):
    rows = pc.shape[0]
    nrow = -(-n_sel // SUBLANES) * SUBLANES
    imp_t = _block_importance(pc, n_cmp, n_sel).T[0:nrow, :]
    blk = lax.broadcasted_iota(jnp.int32, (nrow, 1), 0)
    imp_t = _force_blocks(imp_t, blk, cur_row)
    count = jnp.zeros(imp_t.shape, F32)
    for sp in range(n_sel):
        row = imp_t[sp:sp + 1, :]
        count += ((row > imp_t) | ((row == imp_t) & (sp < blk))).astype(F32)
    sel_t = ((count < min(SEL_TOPK, n_sel)) & (blk < n_sel)).astype(F32)
    sel_t = jnp.concatenate([sel_t, jnp.zeros((LANES - nrow, rows), F32)], axis=0)
    return sel_t.T


def _gate(gt, col):
    lane = lax.broadcasted_iota(jnp.int32, (1, LANES), 1)
    return jax.nn.sigmoid(jnp.sum(jnp.where(lane == col, gt, 0.0), axis=1, keepdims=True))


def _nsa_prompt_kernel(q_ref, ks_ref, vs_ref, kw_ref, vw_ref, kc_ref, vc_ref, bc_ref, tz_ref, gt_ref, gb_ref,
                       o_ref, s_scr, mx_scr, l_scr, acc_scr, *, t_len):
    g = pl.program_id(1)
    qt = pl.program_id(2)
    q0 = qt * TQ
    n_cmp = t_len // CMP_STRIDE - (CMP_BLOCK // CMP_STRIDE) + 1
    n_sel = t_len // SEL_BLOCK
    ncp = kc_ref.shape[0]
    rows4 = GROUP * TQ
    qs = _stack_heads(q_ref[...] * QK_SCALE).astype(BF16)
    qpos = q0 + (lax.broadcasted_iota(jnp.int32, (rows4, 1), 0) & (TQ - 1))

    n_idx = lax.broadcasted_iota(jnp.int32, (1, ncp), 1)
    s_c = _dot_nt(qs, kc_ref[...].astype(BF16)) + bc_ref[...].reshape(rows4, ncp)
    cmask = (qpos - (n_idx * CMP_STRIDE + CMP_BLOCK - 1) >= 0) & (n_idx < n_cmp)
    p_c = _masked_probs(s_c, cmask)
    o_c = _dot(p_c.astype(BF16), vc_ref[...].astype(BF16))
    pc = p_c[0:TQ] + p_c[TQ:2 * TQ] + p_c[2 * TQ:3 * TQ] + p_c[3 * TQ:4 * TQ]
    cur_row = jnp.right_shift(q0 + lax.broadcasted_iota(jnp.int32, (1, TQ), 1), SEL_SHIFT)
    sel = _select_blocks_rows(pc, cur_row, n_cmp, n_sel)
    sel4 = jnp.concatenate([((sel - 1.0) * -MASKED).astype(BF16)] * GROUP, axis=0)

    kt_hi = (q0 + TQ - 1) // TK + 1
    kt_far = jnp.maximum(q0 - (FAR_DIST - 1), 0) // TK
    kidx = lax.broadcasted_iota(jnp.int32, (1, TK), 1)
    srow = lax.broadcasted_iota(jnp.int32, (LANES, TK), 0)

    def tile_bias(k0):
        return tz_ref[jnp.clip((q0 - k0) // TQ, 0, N_TZ - 1)].reshape(rows4, TK)

    def attend(k_ref, v_ref, kt_lo, mask_tile):
        mx_scr[...] = jnp.full(mx_scr.shape, NEG_INF, F32)

        def scores(near):
            def body(kt, carry):
                k0 = pl.multiple_of(kt * TK, TK)
                s = _dot_nt(qs, k_ref[pl.ds(k0, TK), :].astype(BF16))
                if near:
                    s = s + tile_bias(k0)
                s = mask_tile(k0, s, near)
                s_scr[kt] = s
                mx_scr[...] = jnp.maximum(mx_scr[...], jnp.maximum(s[:, 0:LANES], s[:, LANES:TK]))
                return carry
            return body

        kt_mid = jnp.maximum(kt_far, kt_lo)
        lax.fori_loop(kt_lo, kt_mid, scores(False), 0)
        lax.fori_loop(kt_mid, kt_hi, scores(True), 0)
        m = jnp.max(mx_scr[...], axis=1, keepdims=True)
        mx_scr[...] = jnp.broadcast_to(jnp.where(m > NEG_INF, m, 0.0), mx_scr.shape)
        l_scr[...] = jnp.zeros(l_scr.shape, F32)
        acc_scr[...] = jnp.zeros(acc_scr.shape, F32)

        def probs(kt, carry):
            k0 = pl.multiple_of(kt * TK, TK)
            m_rep = mx_scr[...]
            p = jnp.exp(s_scr[kt] - jnp.concatenate([m_rep, m_rep], axis=1))
            l_scr[...] += p[:, 0:LANES] + p[:, LANES:TK]
            acc_scr[...] += _dot(p.astype(BF16), v_ref[pl.ds(k0, TK), :].astype(BF16))
            return carry

        lax.fori_loop(kt_lo, kt_hi, probs, 0)
        l = jnp.sum(l_scr[...], axis=1, keepdims=True)
        return acc_scr[...] / jnp.where(l > 0, l, 1.0)

    def sel_mask(k0, s, near):
        expand = (jnp.right_shift(k0 + lax.broadcasted_iota(jnp.int32, (LANES, TK), 1), SEL_SHIFT) == srow).astype(BF16)
        s = s + _dot(sel4, expand)
        return jnp.where(k0 + kidx <= qpos, s, NEG_INF) if near else s

    o_s = attend(ks_ref, vs_ref, 0, sel_mask)

    def win_mask(k0, s, near):
        dist = qpos - (k0 + kidx)
        return jnp.where((dist >= 0) & (dist <= WINDOW), s, NEG_INF)

    o_w = attend(kw_ref, vw_ref, jnp.maximum(q0 - WINDOW, 0) // TK, win_mask)

    gt = gt_ref[...] + gb_ref[...]
    for j in range(GROUP):
        col = GATE_NG + (g * GROUP + j) * 3
        r = slice(j * TQ, (j + 1) * TQ)
        o_ref[:, j * HEAD_DIM:(j + 1) * HEAD_DIM] = (
            _gate(gt, col) * o_c[r] + _gate(gt, col + 1) * o_s[r] + _gate(gt, col + 2) * o_w[r])


def _nsa_prompt(z, kcvc, bias_c, tz, gate_b):
    b, t, _ = z.shape
    ncp = kcvc.shape[2]
    nq = t // TQ
    kv = lambda slot: pl.BlockSpec((None, t, HEAD_DIM), lambda bi, g, qi: (bi, 0, C_NKV // HEAD_DIM + slot * KV_HEADS + g))
    cmp_blk = lambda slot: pl.BlockSpec((None, None, ncp, HEAD_DIM), lambda bi, g, qi: (bi, slot * KV_HEADS + g, 0, 0))
    return pl.pallas_call(
        functools.partial(_nsa_prompt_kernel, t_len=t),
        grid=(b, KV_HEADS, nq),
        in_specs=[pl.BlockSpec((None, TQ, GROUP * HEAD_DIM), lambda bi, g, qi: (bi, qi, C_NQ // (GROUP * HEAD_DIM) + g)),
                  kv(2), kv(3), kv(4), kv(5), cmp_blk(0), cmp_blk(1),
                  pl.BlockSpec((GROUP, TQ, ncp), lambda bi, g, qi: (g, qi, 0)),
                  pl.BlockSpec((N_TZ, GROUP, TQ, TK), lambda bi, g, qi: (0, g, 0, 0)),
                  pl.BlockSpec((None, TQ, LANES), lambda bi, g, qi: (bi, qi, C_GATE // LANES)),
                  pl.BlockSpec((1, LANES), lambda bi, g, qi: (0, 0))],
        out_specs=pl.BlockSpec((None, TQ, GROUP * HEAD_DIM), lambda bi, g, qi: (bi, qi, g)),
        out_shape=jax.ShapeDtypeStruct((b, t, D_NSA), F32),
        scratch_shapes=[pltpu.VMEM((t // TK, GROUP * TQ, TK), F32), pltpu.VMEM((GROUP * TQ, LANES), F32),
                        pltpu.VMEM((GROUP * TQ, LANES), F32), pltpu.VMEM((GROUP * TQ, HEAD_DIM), F32)],
        compiler_params=_cparams("parallel", "parallel", "arbitrary"),
        name="nsa_prompt",
    )(z, z, z, z, z, kcvc, kcvc, bias_c, tz, z, gate_b)


TS = SUBLANES
ROWS_S = GROUP * TS


def _nsa_s_cmp_kernel(q_ref, kc_ref, vc_ref, bc_ref, oc_ref, sel_ref, *, pos0, n_cmp, n_sel):
    nch = kc_ref.shape[0]
    qs = _stack_heads(q_ref[...] * QK_SCALE).astype(BF16)
    qpos = pos0 + (lax.broadcasted_iota(jnp.int32, (ROWS_S, 1), 0) & (TS - 1))
    n_idx = lax.broadcasted_iota(jnp.int32, (1, nch), 1)
    s_c = _dot_nt(qs, kc_ref[...].astype(BF16)) + bc_ref[...].reshape(ROWS_S, nch)
    cmask = (qpos - (n_idx * CMP_STRIDE + CMP_BLOCK - 1) >= 0) & (n_idx < n_cmp)
    p_c = _masked_probs(s_c, cmask)
    oc_ref[...] = _dot(p_c.astype(BF16), vc_ref[...].astype(BF16))
    pc = p_c[0:TS] + p_c[TS:2 * TS] + p_c[2 * TS:3 * TS] + p_c[3 * TS:4 * TS]
    sel_ref[...] = _select_blocks_cols(pc, jnp.right_shift(qpos[0:TS], SEL_SHIFT), n_cmp, n_sel)


def _nsa_sample_cmp(z8, kcvc, bias_c, *, pos0, n_cmp, n_sel):
    b = z8.shape[0]
    nch = kcvc.shape[2]
    nsp = -(-n_sel // LANES) * LANES
    cmp_blk = lambda slot: pl.BlockSpec((None, None, nch, HEAD_DIM), lambda bi, g: (bi, slot * KV_HEADS + g, 0, 0))
    return pl.pallas_call(
        functools.partial(_nsa_s_cmp_kernel, pos0=pos0, n_cmp=n_cmp, n_sel=n_sel),
        grid=(b, KV_HEADS),
        in_specs=[pl.BlockSpec((None, TS, GROUP * HEAD_DIM), lambda bi, g: (bi, 0, C_NQ // (GROUP * HEAD_DIM) + g)),
                  cmp_blk(0), cmp_blk(1),
                  pl.BlockSpec((GROUP, TS, nch), lambda bi, g: (g, 0, 0))],
        out_specs=[pl.BlockSpec((None, None, ROWS_S, HEAD_DIM), lambda bi, g: (bi, g, 0, 0)),
                   pl.BlockSpec((None, None, TS, nsp), lambda bi, g: (bi, g, 0, 0))],
        out_shape=[jax.ShapeDtypeStruct((b, KV_HEADS, ROWS_S, HEAD_DIM), F32),
                   jax.ShapeDtypeStruct((b, KV_HEADS, TS, nsp), F32)],
        compiler_params=_cparams("parallel", "parallel"),
        name="nsa_sample_cmp",
    )(z8, kcvc, kcvc, bias_c)


def _pad_rows(x, n):
    return jnp.concatenate([x, jnp.zeros((n - x.shape[0], x.shape[1]), x.dtype)], axis=0)


def _nsa_s_sel_kernel(pt_ref, q_ref, sel_ref, kn_ref, vn_ref, bt_ref, bn_ref, bf_ref, *refs, pos0, npages):
    del pt_ref
    rows_pp = PAGE_SIZE * KV_HEADS
    k_refs = [r.reshape(rows_pp, HEAD_DIM) for r in refs[:SEL_PAGES]]
    v_refs = [r.reshape(rows_pp, HEAD_DIM) for r in refs[SEL_PAGES:2 * SEL_PAGES]]
    o_ref, kbuf, vbuf, m_scr, l_scr, acc_scr = refs[2 * SEL_PAGES:]
    step = pl.program_id(1)
    last = step == npages // SEL_PAGES - 1
    rows = KV_HEADS * ROWS_S
    ncol = SEL_TILE * rows_pp
    head_shift = KV_HEADS.bit_length() - 1
    blk_per_page = PAGE_SIZE // SEL_BLOCK

    @pl.when(step == 0)
    def _():
        _softmax_init(m_scr, l_scr, acc_scr)

    nsp = sel_ref.shape[-1]
    ridx = lax.broadcasted_iota(jnp.int32, (rows, 1), 0)
    qpos = pos0 + (ridx & (TS - 1))
    qs = jnp.concatenate([_stack_heads(q_ref[:, g * GROUP * HEAD_DIM:(g + 1) * GROUP * HEAD_DIM] * QK_SCALE)
                          for g in range(KV_HEADS)], axis=0).astype(BF16)
    sel_rows = jnp.concatenate([sel_ref[g] for g in range(KV_HEADS) for _ in range(GROUP)], axis=0)
    bias_far = bf_ref[...].reshape(rows, PAGE_SIZE)[:, 0:1]
    col = lax.broadcasted_iota(jnp.int32, (1, ncol), 1)
    own_head = (col & (KV_HEADS - 1)) == ridx // ROWS_S
    expand = (jnp.right_shift(lax.broadcasted_iota(jnp.int32, (LANES, ncol), 1), SEL_SHIFT + head_shift)
              == lax.broadcasted_iota(jnp.int32, (LANES, ncol), 0)).astype(BF16)
    blk_r = lax.broadcasted_iota(jnp.int32, (nsp, LANES), 0)
    blk_c = lax.broadcasted_iota(jnp.int32, (nsp, LANES), 1)
    n_tiles = SEL_PAGES // SEL_TILE

    for tile in range(n_tiles):
        for i in range(SEL_TILE):
            p = tile * SEL_TILE + i
            kbuf[i * rows_pp:(i + 1) * rows_pp, :] = k_refs[p][...].astype(BF16)
            vbuf[i * rows_pp:(i + 1) * rows_pp, :] = v_refs[p][...].astype(BF16)
        page0 = step * SEL_PAGES + tile * SEL_TILE
        window = ((blk_r == page0 * blk_per_page + blk_c) & (blk_c < SEL_TILE * blk_per_page)).astype(BF16)
        chosen = _dot(_dot(sel_rows.astype(BF16), window).astype(BF16), expand) > 0.5
        s = _dot_nt(qs, kbuf[...]) + bias_far
        if tile == n_tiles - 1:
            tail = jnp.where(last, bt_ref[...].reshape(rows, rows_pp) - bias_far, 0.0)
            s = s + jnp.concatenate([jnp.zeros((rows, ncol - rows_pp), F32), tail], axis=1)
        key = page0 * PAGE_SIZE + jnp.right_shift(col, head_shift)
        _softmax_update(s, chosen & own_head & (key <= qpos), vbuf[...], m_scr, l_scr, acc_scr)

    @pl.when(last)
    def _():
        kidx = lax.broadcasted_iota(jnp.int32, (1, PAGE_SIZE), 1)
        new_blk = lax.broadcasted_iota(jnp.int32, (1, nsp), 1) == npages * blk_per_page
        flag = jnp.sum(jnp.where(new_blk, sel_rows, 0.0), axis=1, keepdims=True) > 0.5
        bias_new = bn_ref[...].reshape(rows, PAGE_SIZE)
        for g in range(KV_HEADS):
            r = slice(g * ROWS_S, (g + 1) * ROWS_S)
            kn = _pad_rows(kn_ref[:, g * HEAD_DIM:(g + 1) * HEAD_DIM], PAGE_SIZE).astype(BF16)
            vn = _pad_rows(vn_ref[:, g * HEAD_DIM:(g + 1) * HEAD_DIM], PAGE_SIZE).astype(BF16)
            mask = flag[r] & (pos0 + kidx <= qpos[r]) & (kidx < TS)
            _softmax_update(_dot_nt(qs[r], kn) + bias_new[r], mask, vn, m_scr.at[r], l_scr.at[r], acc_scr.at[r])
            o_ref[g] = _softmax_result(l_scr.at[r], acc_scr.at[r])


def _nsa_sample_sel(z8, sel, bias_tail, bias_new, bias_far, cache6, page_table, layer, *, pos0):
    b, npages = page_table.shape
    nsp = sel.shape[-1]
    steps = npages // SEL_PAGES
    slot_cols = KV_HEADS * HEAD_DIM
    new_rows = lambda slot: pl.BlockSpec((None, TS, slot_cols),
                                         lambda bi, si, pt: (bi, 0, (C_NKV + slot * slot_cols) // slot_cols))
    whole = lambda a: pl.BlockSpec(a.shape, lambda bi, si, pt: (0,) * a.ndim)
    grid_spec = pltpu.PrefetchScalarGridSpec(
        num_scalar_prefetch=1,
        grid=(b, steps),
        in_specs=[pl.BlockSpec((None, TS, D_NSA), lambda bi, si, pt: (bi, 0, C_NQ // D_NSA)),
                  pl.BlockSpec((None, KV_HEADS, TS, nsp), lambda bi, si, pt: (bi, 0, 0, 0)),
                  new_rows(2), new_rows(3), whole(bias_tail), whole(bias_new), whole(bias_far)]
                 + [_page_spec(layer, 2, p, SEL_PAGES, 2) for p in range(SEL_PAGES)]
                 + [_page_spec(layer, 3, p, SEL_PAGES, 2) for p in range(SEL_PAGES)],
        out_specs=pl.BlockSpec((None, KV_HEADS, ROWS_S, HEAD_DIM), lambda bi, si, pt: (bi, 0, 0, 0)),
        scratch_shapes=[pltpu.VMEM((SEL_TILE * PAGE_SIZE * KV_HEADS, HEAD_DIM), BF16),
                        pltpu.VMEM((SEL_TILE * PAGE_SIZE * KV_HEADS, HEAD_DIM), BF16),
                        pltpu.VMEM((KV_HEADS * ROWS_S, 1), F32), pltpu.VMEM((KV_HEADS * ROWS_S, 1), F32),
                        pltpu.VMEM((KV_HEADS * ROWS_S, HEAD_DIM), F32)],
    )
    return pl.pallas_call(
        functools.partial(_nsa_s_sel_kernel, pos0=pos0, npages=npages),
        grid_spec=grid_spec,
        out_shape=jax.ShapeDtypeStruct((b, KV_HEADS, ROWS_S, HEAD_DIM), F32),
        compiler_params=_cparams("parallel", "arbitrary"),
        name="nsa_sample_sel",
    )(page_table, z8, sel, z8, z8, bias_tail, bias_new, bias_far, *([cache6] * (2 * SEL_PAGES)))


def _nsa_s_win_kernel(q_ref, kw_ref, vw_ref, kn_ref, vn_ref, bw_ref, oc_ref, os_ref, gt_ref, gb_ref, o_ref,
                      m_scr, l_scr, acc_scr, *, pos0):
    g = pl.program_id(1)
    wb = kw_ref.shape[0]
    qs = _stack_heads(q_ref[...] * QK_SCALE).astype(BF16)
    qpos = pos0 + (lax.broadcasted_iota(jnp.int32, (ROWS_S, 1), 0) & (TS - 1))
    bias = bw_ref[...].reshape(ROWS_S, wb + PAGE_SIZE)
    _softmax_init(m_scr, l_scr, acc_scr)
    dist = qpos - (pos0 - wb + lax.broadcasted_iota(jnp.int32, (1, wb), 1))
    s = _dot_nt(qs, kw_ref[...].astype(BF16)) + bias[:, 0:wb]
    _softmax_update(s, (dist >= 0) & (dist <= WINDOW), vw_ref[...].astype(BF16), m_scr, l_scr, acc_scr)
    kidx = lax.broadcasted_iota(jnp.int32, (1, PAGE_SIZE), 1)
    dist = qpos - (pos0 + kidx)
    s = _dot_nt(qs, _pad_rows(kn_ref[...], PAGE_SIZE).astype(BF16)) + bias[:, wb:wb + PAGE_SIZE]
    _softmax_update(s, (dist >= 0) & (dist <= WINDOW) & (kidx < TS), _pad_rows(vn_ref[...], PAGE_SIZE).astype(BF16),
                    m_scr, l_scr, acc_scr)
    o_w = _softmax_result(l_scr, acc_scr)
    o_c = oc_ref[...]
    o_s = os_ref[...]
    gt = gt_ref[...] + gb_ref[...]
    for j in range(GROUP):
        col = GATE_NG + (g * GROUP + j) * 3
        r = slice(j * TS, (j + 1) * TS)
        o_ref[:, j * HEAD_DIM:(j + 1) * HEAD_DIM] = (
            _gate(gt, col) * o_c[r] + _gate(gt, col + 1) * o_s[r] + _gate(gt, col + 2) * o_w[r])


def _nsa_sample_win(z8, win2d, bias_w, o_c, o_s, gate_b, *, pos0):
    b, wb, _ = win2d.shape
    zcol = lambda col: (lambda bi, g: (bi, 0, col // HEAD_DIM + g))
    part = pl.BlockSpec((None, None, ROWS_S, HEAD_DIM), lambda bi, g: (bi, g, 0, 0))
    return pl.pallas_call(
        functools.partial(_nsa_s_win_kernel, pos0=pos0),
        grid=(b, KV_HEADS),
        in_specs=[pl.BlockSpec((None, TS, GROUP * HEAD_DIM), lambda bi, g: (bi, 0, C_NQ // (GROUP * HEAD_DIM) + g)),
                  pl.BlockSpec((None, wb, HEAD_DIM), lambda bi, g: (bi, 0, g)),
                  pl.BlockSpec((None, wb, HEAD_DIM), lambda bi, g: (bi, 0, KV_HEADS + g)),
                  pl.BlockSpec((None, TS, HEAD_DIM), zcol(C_NKV + 4 * KV_HEADS * HEAD_DIM)),
                  pl.BlockSpec((None, TS, HEAD_DIM), zcol(C_NKV + 5 * KV_HEADS * HEAD_DIM)),
                  pl.BlockSpec((GROUP, TS, wb + PAGE_SIZE), lambda bi, g: (g, 0, 0)),
                  part, part,
                  pl.BlockSpec((None, TS, LANES), lambda bi, g: (bi, 0, C_GATE // LANES)),
                  pl.BlockSpec((1, LANES), lambda bi, g: (0, 0))],
        out_specs=pl.BlockSpec((None, TS, GROUP * HEAD_DIM), lambda bi, g: (bi, 0, g)),
        out_shape=jax.ShapeDtypeStruct((b, TS, D_NSA), F32),
        scratch_shapes=[pltpu.VMEM((ROWS_S, 1), F32), pltpu.VMEM((ROWS_S, 1), F32), pltpu.VMEM((ROWS_S, HEAD_DIM), F32)],
        compiler_params=_cparams("parallel", "parallel"),
        name="nsa_sample_win",
    )(z8, win2d, win2d, z8, z8, bias_w, o_c, o_s, z8, gate_b)


def _t5_bucket(dist):
    n = np.maximum(dist, 0)
    exact = NUM_BUCKETS // 2
    nf = np.maximum(n, 1).astype(np.float32)
    large = exact + (np.log(nf / np.float32(exact)) / np.float32(math.log(MAX_DISTANCE / exact))
                     * np.float32(NUM_BUCKETS - exact)).astype(np.int32)
    return np.where(n < exact, n, np.minimum(large, NUM_BUCKETS - 1)).astype(np.int32)


FAR_DIST = int(np.max(np.nonzero(_t5_bucket(np.arange(4 * MAX_DISTANCE)) < NUM_BUCKETS - 1)[0])) + 1


def _bias_table(rel_bias, dist):
    onehot = jax.nn.one_hot(jnp.asarray(_t5_bucket(dist).astype(np.int8)), NUM_BUCKETS, dtype=F32)
    return jnp.einsum("rcb,bh->hrc", onehot, rel_bias, precision=lax.Precision.HIGHEST)


def _pack_layer(p, l):
    w_in = p["w_in"][l]
    pad = jnp.zeros((D_MODEL, N_IN - C_GATE - 32), F32)
    w_in_p = jnp.concatenate([w_in[:, 0:2048], w_in[:, 2056:3080], w_in[:, 3080:4616], w_in[:, 4640:5152],
                              w_in[:, 2048:2056], w_in[:, 4616:4640], pad], axis=1).astype(BF16)
    gb = p["mlstm_gate_b"][l]
    mlstm_gate_b = jnp.zeros((1, LANES), F32).at[0, GATE_MI:GATE_MI + 4].set(gb[0]).at[0, GATE_MF:GATE_MF + 4].set(gb[1])
    nsa_gate_b = jnp.zeros((1, LANES), F32).at[0, GATE_NG:GATE_NG + 3 * NSA_HEADS].set(p["nsa_gate_b"][l].reshape(-1))
    w1 = p["cmp_w1"][l]
    wcat = jnp.concatenate([w1[:, 0:CMP_STRIDE], w1[:, CMP_STRIDE:CMP_BLOCK]], axis=-1).astype(BF16)
    lam = lax.complex(p["s5_a_re"][l], p["s5_a_im"][l])
    lam_bar = jnp.exp(lam * jnp.exp(p["s5_log_step"][l])[:, None])
    b_bar = ((lam_bar - 1.0) / lam)[..., None] * lax.complex(p["s5_b_re"][l], p["s5_b_im"][l])
    gi = S5_GROUPS // S5_IN_TILES
    bd_in = lambda m: jnp.einsum("cgph,gk->cghkp", m.reshape(S5_IN_TILES, gi, S5_STATE, S5_GROUP_WIDTH),
                                 jnp.eye(gi, dtype=F32)).reshape(S5_IN_TILES, S5_IN_FEATS, 256)
    wb = jnp.concatenate([bd_in(b_bar.real), bd_in(b_bar.imag)], axis=2).astype(BF16)
    go = S5_GROUPS // S5_OUT_TILES
    bd_out = lambda m: jnp.einsum("jghp,gk->jgpkh", m.reshape(S5_OUT_TILES, go, S5_GROUP_WIDTH, S5_STATE),
                                  jnp.eye(go, dtype=F32)).reshape(S5_OUT_TILES, S5_OUT_CH, 256)
    wc = jnp.stack([bd_out(p["s5_c_re"][l]), -bd_out(p["s5_c_im"][l])]).astype(BF16)
    half = lambda w: jnp.pad(w, ((0, 0), (0, D_FFP - D_FF)))
    w_up = p["ffn_w_up"][l]
    conv_w = p["ffn_conv_w"][l]
    conv_b = p["ffn_conv_b"][l][None, :]
    return dict(
        w_in=w_in_p, mlstm_gate_b=mlstm_gate_b, mlstm_norm_w=p["mlstm_norm_w"][l][None, :], nsa_gate_b=nsa_gate_b,
        wcat=wcat, cmp_b1=p["cmp_b1"][l], cmp_w2=p["cmp_w2"][l].astype(BF16),
        s5_wb=wb, s5_wc=wc, s5_lam_r=lam_bar.real.reshape(1, S5_CH), s5_lam_i=lam_bar.imag.reshape(1, S5_CH),
        s5_d=p["s5_d"][l][None, :], s5_glu_w=p["s5_glu_w"][l].astype(BF16), s5_glu_b=p["s5_glu_b"][l][None, :],
        w_out=p["w_out"][l].astype(BF16), ln1_w=p["ln1_w"][l][None, :], ln1_b=p["ln1_b"][l][None, :],
        w_up=jnp.concatenate([half(w_up[:, :D_FF]), half(w_up[:, D_FF:])], axis=1).astype(BF16),
        conv_w8=jnp.pad(jnp.concatenate([half(conv_w[:, :D_FF]), half(conv_w[:, D_FF:])], axis=1),
                        ((0, SUBLANES - CONV_W), (0, 0))),
        conv_b=jnp.concatenate([half(conv_b[:, :D_FF]), half(conv_b[:, D_FF:])], axis=1),
        w_down=jnp.pad(p["ffn_w_down"][l], ((0, D_FFP - D_FF), (0, 0))).astype(BF16),
        ln2_w=p["ln2_w"][l][None, :], ln2_b=p["ln2_b"][l][None, :],
    )


def _unpad_ff(x):
    return jnp.concatenate([x[..., :D_FF], x[..., D_FFP:D_FFP + D_FF]], axis=-1)


def _pad_ff(x):
    pad = [(0, 0)] * (x.ndim - 1) + [(0, D_FFP - D_FF)]
    return jnp.concatenate([jnp.pad(x[..., :D_FF], pad), jnp.pad(x[..., D_FF:], pad)], axis=-1)


def _time_major(x, b, t, t_use):
    return x.reshape(b, t, -1)[:, :t_use].transpose(1, 0, 2).reshape(t_use * b, -1)


def _batch_major(x, b, t, t_use):
    x = x.reshape(t_use, b, -1).transpose(1, 0, 2)
    return jnp.pad(x, ((0, 0), (0, t - t_use), (0, 0))).reshape(b * t, -1)


def _mixer_tail(x2d, z, b, t, t_use, lw, o_mlstm, o_nsa, s5_state, conv_state8, tm):
    pad8 = lambda s: jnp.pad(s.reshape(b, S5_CH), ((0, SUBLANES - b), (0, 0)))
    u_tm = _time_major(z[:, C_SU:C_SU + D_S5], b, t, t_use)
    o_s5_tm, xr, xi = _s5(u_tm, lw["s5_wb"], lw["s5_wc"], lw["s5_lam_r"], lw["s5_lam_i"], lw["s5_d"],
                          lw["s5_glu_w"], lw["s5_glu_b"], pad8(s5_state[0]), pad8(s5_state[1]), lc=S5_LC, rb=b)
    o_s5 = _batch_major(o_s5_tm, b, t, t_use)
    x1 = _wout_ln(o_mlstm.reshape(b * t, D_MLSTM), o_nsa.reshape(b * t, D_NSA), o_s5, x2d,
                  lw["w_out"], lw["ln1_w"], lw["ln1_b"], tm=256)
    s5_new = (xr[:b].reshape(b, S5_GROUPS, S5_STATE), xi[:b].reshape(b, S5_GROUPS, S5_STATE))
    if conv_state8 is None:
        x2, tail_a, tail_g = _ffn_prompt(x1.reshape(b, t, D_MODEL), lw["w_up"], lw["w_down"], lw["conv_w8"],
                                         lw["conv_b"], lw["ln2_w"], lw["ln2_b"], tt=FFN_TT, tf=FFN_TF)
        keep = slice(SUBLANES - (CONV_W - 1), SUBLANES)
        conv_new = jnp.concatenate([tail_a[:, keep, :D_FF], tail_g[:, keep, :D_FF]], axis=-1)
        return x2.reshape(b * t, D_MODEL), s5_new, conv_new
    up = _matmul(x1, lw["w_up"], tm=tm, tn=512)
    hgate = _convgate(up.reshape(b, t, 2 * D_FFP), conv_state8, lw["conv_w8"], lw["conv_b"], tt=256, tf=512)
    x2 = _down_ln(hgate.reshape(b * t, D_FFP), lw["w_down"], x1, lw["ln2_w"], lw["ln2_b"], tm=512, tk=1408)
    conv_new = _unpad_ff(up.reshape(b, t, 2 * D_FFP)[:, t_use - (CONV_W - 1):t_use])
    return x2, s5_new, conv_new


def _prompt_layer(x2d, b, t, lw, rel_bias):
    z, kv_rows, win_rows = _proj_in(x2d, lw["w_in"], tm=min(1024, b * t), tn=768)
    z3 = z.reshape(b, t, N_IN)
    zeros = lambda *s: jnp.zeros(s, F32)
    o_mlstm, c1, n1, m1 = _mlstm(z3, lw["mlstm_gate_b"], lw["mlstm_norm_w"],
                                 zeros(b, MLSTM_HEADS, HEAD_DIM, HEAD_DIM), zeros(b, MLSTM_HEADS, LANES),
                                 zeros(b, MLSTM_HEADS, LANES), L=MLSTM_L, lin=MLSTM_L, t_valid=t)
    kcvc = _cmp_finish(_cmp_project_prompt(z3, lw["wcat"]), lw["cmp_b1"], lw["cmp_w2"])
    ncp = t // CMP_STRIDE
    bias_c = _bias_table(rel_bias, np.arange(t)[:, None] - (np.arange(ncp) * CMP_STRIDE + CMP_BLOCK - 1)[None, :])
    ti = np.arange(TQ)[:, None] - np.arange(TK)[None, :]
    far = _bias_table(rel_bias, np.full((1, 1), FAR_DIST))
    tz = jnp.stack([_bias_table(rel_bias, d * TQ + ti) - far for d in range(N_TZ)])
    o_nsa = _nsa_prompt(z3, kcvc, bias_c, tz, lw["nsa_gate_b"])
    x2, s5_new, conv_new = _mixer_tail(
        x2d, z, b, t, t, lw, o_mlstm, o_nsa, (zeros(b, S5_GROUPS, S5_STATE), zeros(b, S5_GROUPS, S5_STATE)),
        None, tm=1024)
    wrows = min(WINDOW, t)
    state = (kv_rows.reshape(b, t, 4, KV_HEADS, HEAD_DIM), win_rows.reshape(b, t, 2, KV_HEADS, HEAD_DIM)[:, t - wrows:],
             c1, n1, m1[:, :, 0], s5_new[0], s5_new[1], conv_new)
    return x2, state


def _sample_layer(x2d, b, tn, lw, rel_bias, layer, cache6, page_table, win2d, mlstm_state, s5_state, conv_state):
    npages = page_table.shape[1]
    pos0 = npages * PAGE_SIZE
    z = _matmul(x2d, lw["w_in"], tm=1024, tn=768)
    z8 = z.reshape(b, TS, N_IN)
    c0, n0, m0 = mlstm_state
    o_mlstm, c1, n1, m1 = _mlstm(z8, lw["mlstm_gate_b"], lw["mlstm_norm_w"], c0, n0,
                                 jnp.broadcast_to(m0[:, :, None], (b, MLSTM_HEADS, LANES)),
                                 L=LANES, lin=TS, t_valid=tn)
    n_chunks = (pos0 + tn) // CMP_STRIDE
    n_cmp = n_chunks - CMP_BLOCK // CMP_STRIDE + 1
    n_sel = -(-(pos0 + tn) // SEL_BLOCK)
    kcvc = _cmp_finish(_cmp_project_sample(cache6, page_table, layer, lw["wcat"]), lw["cmp_b1"], lw["cmp_w2"])
    qpos = pos0 + np.arange(TS)[:, None]
    bias_c = _bias_table(rel_bias, qpos - (np.arange(n_chunks) * CMP_STRIDE + CMP_BLOCK - 1)[None, :])
    o_c, sel = _nsa_sample_cmp(z8, kcvc, bias_c, pos0=pos0, n_cmp=n_cmp, n_sel=n_sel)
    kk = np.arange(PAGE_SIZE)[None, :]
    bias_tail = _bias_table(rel_bias, np.repeat(qpos - (pos0 - PAGE_SIZE + kk), KV_HEADS, axis=1))
    bias_new = _bias_table(rel_bias, qpos - (pos0 + kk))
    bias_far = _bias_table(rel_bias, np.broadcast_to(qpos - (pos0 - 2 * PAGE_SIZE), (TS, PAGE_SIZE)))
    o_s = _nsa_sample_sel(z8, sel, bias_tail, bias_new, bias_far, cache6, page_table, layer, pos0=pos0)
    wb = win2d.shape[1]
    wk = np.arange(wb)[None, :]
    bias_w = _bias_table(rel_bias, np.concatenate([qpos - (pos0 - wb + wk), qpos - (pos0 + kk)], axis=1))
    o_nsa = _nsa_sample_win(z8, win2d, bias_w, o_c, o_s, lw["nsa_gate_b"], pos0=pos0)
    conv_state8 = jnp.pad(_pad_ff(conv_state), ((0, 0), (SUBLANES - (CONV_W - 1), 0), (0, 0)))
    x2, s5_new, conv_new = _mixer_tail(x2d, z, b, TS, tn, lw, o_mlstm, o_nsa, s5_state, conv_state8, tm=1024)
    nkv = z8[:, :tn, C_NKV:C_NKV + N_KV_SLOTS * KV_HEADS * HEAD_DIM].reshape(b, tn, N_KV_SLOTS, KV_HEADS, HEAD_DIM)
    state = (nkv[:, :, :4], nkv[:, :, 4:], c1, n1, m1[:, :, 0], s5_new[0], s5_new[1], conv_new)
    return x2, state


def kernel(x_prompt, x_sample, cache_nsa_kv, cache_win_kv, state_mlstm_c, state_mlstm_n, state_mlstm_m,
           state_s5_re, state_s5_im, state_ffn_conv, page_table, w_in, mlstm_gate_b, mlstm_norm_w,
           nsa_gate_b, cmp_w1, cmp_b1, cmp_w2, rel_bias, s5_a_re, s5_a_im, s5_b_re, s5_b_im, s5_c_re,
           s5_c_im, s5_d, s5_log_step, s5_glu_w, s5_glu_b, w_out, ln1_w, ln1_b, ffn_w_up, ffn_conv_w,
           ffn_conv_b, ffn_w_down, ln2_w, ln2_b):
    params = dict(w_in=w_in, mlstm_gate_b=mlstm_gate_b, mlstm_norm_w=mlstm_norm_w, nsa_gate_b=nsa_gate_b,
                  cmp_w1=cmp_w1, cmp_b1=cmp_b1, cmp_w2=cmp_w2, s5_a_re=s5_a_re, s5_a_im=s5_a_im, s5_b_re=s5_b_re,
                  s5_b_im=s5_b_im, s5_c_re=s5_c_re, s5_c_im=s5_c_im, s5_d=s5_d, s5_log_step=s5_log_step,
                  s5_glu_w=s5_glu_w, s5_glu_b=s5_glu_b, w_out=w_out, ln1_w=ln1_w, ln1_b=ln1_b, ffn_w_up=ffn_w_up,
                  ffn_conv_w=ffn_conv_w, ffn_conv_b=ffn_conv_b, ffn_w_down=ffn_w_down, ln2_w=ln2_w, ln2_b=ln2_b)
    depth = w_in.shape[0]
    bp, tp, _ = x_prompt.shape
    bs, tn, _ = x_sample.shape
    assert tp % TK == 0 and tn < CMP_STRIDE and tn <= TS and SUBLANES % bp == 0 and SUBLANES % bs == 0
    assert (tn * bs) % SUBLANES == 0 and tp % FFN_TT == 0
    assert page_table.shape[1] % CMP_PAGES == 0 and cache_nsa_kv.shape[2] == PAGE_SIZE
    win2d = cache_win_kv.reshape(depth, bs, cache_win_kv.shape[2], 2 * KV_HEADS * HEAD_DIM)
    xp = x_prompt.reshape(bp * tp, D_MODEL)
    xs = jnp.pad(x_sample, ((0, 0), (0, TS - tn), (0, 0))).reshape(bs * TS, D_MODEL)
    p_states, s_states = [], []
    for l in range(depth):
        lw = _pack_layer(params, l)
        xp, sp = _prompt_layer(xp, bp, tp, lw, rel_bias)
        xs, ss = _sample_layer(xs, bs, tn, lw, rel_bias, l, cache_nsa_kv, page_table, win2d[l],
                               (state_mlstm_c[l], jnp.pad(state_mlstm_n[l], ((0, 0), (0, 0), (0, LANES - HEAD_DIM))),
                                state_mlstm_m[l]),
                               (state_s5_re[l], state_s5_im[l]), state_ffn_conv[l])
        p_states.append(sp)
        s_states.append(ss)
    stk = lambda states, i: jnp.stack([s[i] for s in states])
    y_prompt = xp.reshape(bp, tp, D_MODEL)
    y_sample = xs.reshape(bs, TS, D_MODEL)[:, :tn]
    return (y_prompt, y_sample,
            stk(p_states, 0), stk(s_states, 0), stk(p_states, 1), stk(s_states, 1),
            stk(p_states, 2), stk(s_states, 2), stk(p_states, 3), stk(s_states, 3), stk(p_states, 4), stk(s_states, 4),
            stk(p_states, 5), stk(s_states, 5), stk(p_states, 6), stk(s_states, 6), stk(p_states, 7), stk(s_states, 7))
```

```python
import functools
import math

import jax
import jax.numpy as jnp
import numpy as np
from jax import lax
from jax.experimental import pallas as pl
from jax.experimental.pallas import tpu as pltpu

F32 = jnp.float32
BF16 = jnp.bfloat16
NEG_INF = float("-inf")
M_INIT = -1e30
MASKED = -1e30

D_MODEL = 2048
PAGE_SIZE = 128
D_MLSTM = D_MODEL // 4
D_NSA = D_MODEL // 2
D_S5 = D_MODEL - D_MLSTM - D_NSA
HEAD_DIM = 128
MLSTM_HEADS = D_MLSTM // HEAD_DIM
NSA_HEADS = D_NSA // HEAD_DIM
KV_HEADS = 2
GROUP = NSA_HEADS // KV_HEADS
N_KV_SLOTS = 6
CMP_BLOCK = 32
CMP_STRIDE = 16
SEL_BLOCK = 64
SEL_TOPK = 16
FORCE_SCORE = 1e4
WINDOW = 512
S5_GROUP_WIDTH = 16
S5_GROUPS = D_S5 // S5_GROUP_WIDTH
S5_STATE = 64
S5_CH = S5_GROUPS * S5_STATE
NUM_BUCKETS = 32
MAX_DISTANCE = 128
D_FF = ((8 * D_MODEL // 3 + 127) // 128) * 128
CONV_W = 3
LN_EPS = 1e-5
DEPTH = 2
DEEPNORM_ALPHA = (2 * DEPTH) ** 0.25
QK_SCALE = HEAD_DIM ** -0.5

LANES = 128
SUBLANES = 8
V7X_VMEM_LIMIT = 56 * 2 ** 20

C_MQ, C_MK, C_MV, C_MO = 0, 512, 1024, 1536
C_NQ = 2048
C_NKV = 3072
C_SU = 4608
C_GATE = 5120
N_IN = 5376
GATE_MI, GATE_MF, GATE_NG = 0, 4, 8
D_FFP = 5632
MLSTM_L = 256
TQ = 256
TK = 256
S5_LC = 128
FFN_TT = 1024
FFN_TF = 512
FFN_HALO = 16
CMP_PAGES = 32
SEL_PAGES = 32
SEL_TILE = 16
SEL_SHIFT = SEL_BLOCK.bit_length() - 1


def _cparams(*sem):
    return pltpu.CompilerParams(dimension_semantics=sem, vmem_limit_bytes=V7X_VMEM_LIMIT)


def _dot(a, b):
    return jnp.dot(a, b, preferred_element_type=F32)


def _dot_nt(a, b):
    return lax.dot_general(a, b, (((1,), (1,)), ((), ())), preferred_element_type=F32)


def _layer_norm(y, w, b):
    mu = jnp.mean(y, axis=-1, keepdims=True)
    d = y - mu
    var = jnp.mean(d * d, axis=-1, keepdims=True)
    return d * lax.rsqrt(var + LN_EPS) * w + b


def _mm_kernel(x_ref, w_ref, o_ref):
    o_ref[...] = _dot(x_ref[...].astype(BF16), w_ref[...]).astype(o_ref.dtype)


def _matmul(x, w, *, tm, tn):
    m, k = x.shape
    n = w.shape[1]
    tm = min(tm, m)
    return pl.pallas_call(
        _mm_kernel,
        grid=(m // tm, n // tn),
        in_specs=[pl.BlockSpec((tm, k), lambda i, j: (i, 0)), pl.BlockSpec((k, tn), lambda i, j: (0, j))],
        out_specs=pl.BlockSpec((tm, tn), lambda i, j: (i, j)),
        out_shape=jax.ShapeDtypeStruct((m, n), F32),
        compiler_params=_cparams("parallel", "parallel"),
        name="proj",
    )(x, w)


KV4_GROUPS = 4 * KV_HEADS
WIN_GROUPS = 2 * KV_HEADS


def _proj_in_kernel(x_ref, w_ref, z_ref, kv_ref, win_ref, *, tn):
    j = pl.program_id(1)
    acc = _dot(x_ref[...].astype(BF16), w_ref[...])
    z_ref[...] = acc
    tm = acc.shape[0]
    per_tile = tn // HEAD_DIM
    first_tile = C_NKV // tn
    for tile in range(first_tile, first_tile + N_KV_SLOTS * KV_HEADS // per_tile):
        @pl.when(j == tile)
        def _(tile=tile):
            for c in range(per_tile):
                grp = (tile - first_tile) * per_tile + c
                val = acc[:, c * HEAD_DIM:(c + 1) * HEAD_DIM]
                if grp < KV4_GROUPS:
                    kv_ref[pl.ds(grp, tm, stride=KV4_GROUPS), :] = val
                else:
                    win_ref[pl.ds(grp - KV4_GROUPS, tm, stride=WIN_GROUPS), :] = val


def _proj_in(x, w, *, tm, tn):
    m, k = x.shape
    n = w.shape[1]
    assert C_NKV % tn == 0 and (N_KV_SLOTS * KV_HEADS * HEAD_DIM) % tn == 0
    return pl.pallas_call(
        functools.partial(_proj_in_kernel, tn=tn),
        grid=(m // tm, n // tn),
        in_specs=[pl.BlockSpec((tm, k), lambda i, j: (i, 0)), pl.BlockSpec((k, tn), lambda i, j: (0, j))],
        out_specs=[pl.BlockSpec((tm, tn), lambda i, j: (i, j)),
                   pl.BlockSpec((tm * KV4_GROUPS, HEAD_DIM), lambda i, j: (i, 0)),
                   pl.BlockSpec((tm * WIN_GROUPS, HEAD_DIM), lambda i, j: (i, 0))],
        out_shape=[jax.ShapeDtypeStruct((m, n), F32),
                   jax.ShapeDtypeStruct((m * KV4_GROUPS, HEAD_DIM), F32),
                   jax.ShapeDtypeStruct((m * WIN_GROUPS, HEAD_DIM), F32)],
        compiler_params=_cparams("parallel", "arbitrary"),
        name="proj_in",
    )(x, w)


def _wout_kernel(om_ref, on_ref, os_ref, x_ref, w_ref, lw_ref, lb_ref, o_ref):
    acc = _dot(om_ref[...].astype(BF16), w_ref[0:D_MLSTM, :])
    acc += _dot(on_ref[...].astype(BF16), w_ref[D_MLSTM:D_MLSTM + D_NSA, :])
    acc += _dot(os_ref[...].astype(BF16), w_ref[D_MLSTM + D_NSA:D_MODEL, :])
    o_ref[...] = _layer_norm(DEEPNORM_ALPHA * x_ref[...] + acc, lw_ref[...], lb_ref[...])


def _wout_ln(om, on, os_, x, w, lw, lb, *, tm):
    m = x.shape[0]
    tm = min(tm, m)
    row = lambda width: pl.BlockSpec((tm, width), lambda i: (i, 0))
    full = lambda a: pl.BlockSpec(a.shape, lambda i: (0, 0))
    return pl.pallas_call(
        _wout_kernel,
        grid=(m // tm,),
        in_specs=[row(D_MLSTM), row(D_NSA), row(D_S5), row(D_MODEL), full(w), full(lw), full(lb)],
        out_specs=row(D_MODEL),
        out_shape=jax.ShapeDtypeStruct((m, D_MODEL), F32),
        compiler_params=_cparams("parallel"),
        name="wout_ln",
    )(om, on, os_, x, w, lw, lb)


def _down_kernel(h_ref, w_ref, x_ref, lw_ref, lb_ref, o_ref, acc_ref, *, nk):
    k = pl.program_id(1)

    @pl.when(k == 0)
    def _():
        acc_ref[...] = jnp.zeros_like(acc_ref)

    acc_ref[...] += _dot(h_ref[...], w_ref[...])

    @pl.when(k == nk - 1)
    def _():
        o_ref[...] = _layer_norm(DEEPNORM_ALPHA * x_ref[...] + acc_ref[...], lw_ref[...], lb_ref[...])


def _down_ln(h, w, x, lw, lb, *, tm, tk):
    m, kk = h.shape
    tm = min(tm, m)
    nk = kk // tk
    return pl.pallas_call(
        functools.partial(_down_kernel, nk=nk),
        grid=(m // tm, nk),
        in_specs=[
            pl.BlockSpec((tm, tk), lambda i, k: (i, k)),
            pl.BlockSpec((tk, D_MODEL), lambda i, k: (k, 0)),
            pl.BlockSpec((tm, D_MODEL), lambda i, k: (i, 0)),
            pl.BlockSpec((1, D_MODEL), lambda i, k: (0, 0)),
            pl.BlockSpec((1, D_MODEL), lambda i, k: (0, 0)),
        ],
        out_specs=pl.BlockSpec((tm, D_MODEL), lambda i, k: (i, 0)),
        out_shape=jax.ShapeDtypeStruct((m, D_MODEL), F32),
        scratch_shapes=[pltpu.VMEM((tm, D_MODEL), F32)],
        compiler_params=_cparams("parallel", "arbitrary"),
        name="down_ln",
    )(h, w, x, lw, lb)


def _convgate_kernel(a_ref, g_ref, ha_ref, hg_ref, sa_ref, sg_ref, wa_ref, wg_ref, ba_ref, bg_ref, o_ref):
    first = pl.program_id(1) == 0
    tt = a_ref.shape[0]
    rows = lax.broadcasted_iota(jnp.int32, (tt, 1), 0)

    def conv(cur_ref, halo_ref, st_ref, w_ref, b_ref):
        cur = cur_ref[...]
        prev = jnp.where(first, st_ref[...], halo_ref[...])
        p1 = prev[7:8, :]
        p2 = prev[6:7, :]
        x1 = jnp.where(rows == 0, p1, pltpu.roll(cur, 1, 0))
        x2 = jnp.where(rows == 0, p2, jnp.where(rows == 1, p1, pltpu.roll(cur, 2, 0)))
        w = w_ref[...]
        return b_ref[...] + w[0:1, :] * x2 + w[1:2, :] * x1 + w[2:3, :] * cur

    a = conv(a_ref, ha_ref, sa_ref, wa_ref, ba_ref)
    g = conv(g_ref, hg_ref, sg_ref, wg_ref, bg_ref)
    o_ref[...] = (a * jax.nn.sigmoid(a) * g).astype(o_ref.dtype)


def _convgate(up, state8, conv_w8, conv_b, *, tt, tf):
    b, t, _ = up.shape
    tt = min(tt, t)
    nf = D_FFP // tf
    hb = tt // SUBLANES
    cur_a = pl.BlockSpec((None, tt, tf), lambda bi, ti, fi: (bi, ti, fi))
    cur_g = pl.BlockSpec((None, tt, tf), lambda bi, ti, fi: (bi, ti, fi + nf))
    halo_a = pl.BlockSpec((None, SUBLANES, tf), lambda bi, ti, fi: (bi, jnp.maximum(ti * hb - 1, 0), fi))
    halo_g = pl.BlockSpec((None, SUBLANES, tf), lambda bi, ti, fi: (bi, jnp.maximum(ti * hb - 1, 0), fi + nf))
    st_a = pl.BlockSpec((None, SUBLANES, tf), lambda bi, ti, fi: (bi, 0, fi))
    st_g = pl.BlockSpec((None, SUBLANES, tf), lambda bi, ti, fi: (bi, 0, fi + nf))
    w_a = pl.BlockSpec((SUBLANES, tf), lambda bi, ti, fi: (0, fi))
    w_g = pl.BlockSpec((SUBLANES, tf), lambda bi, ti, fi: (0, fi + nf))
    b_a = pl.BlockSpec((1, tf), lambda bi, ti, fi: (0, fi))
    b_g = pl.BlockSpec((1, tf), lambda bi, ti, fi: (0, fi + nf))
    return pl.pallas_call(
        _convgate_kernel,
        grid=(b, t // tt, nf),
        in_specs=[cur_a, cur_g, halo_a, halo_g, st_a, st_g, w_a, w_g, b_a, b_g],
        out_specs=pl.BlockSpec((None, tt, tf), lambda bi, ti, fi: (bi, ti, fi)),
        out_shape=jax.ShapeDtypeStruct((b, t, D_FFP), BF16),
        compiler_params=_cparams("parallel", "parallel", "parallel"),
        name="convgate",
    )(up, up, up, up, state8, state8, conv_w8, conv_w8, conv_b, conv_b)


def _ffn_kernel(x_ref, xh_ref, wa_ref, wg_ref, wd_ref, cwa_ref, cwg_ref, cba_ref, cbg_ref, lw_ref, lb_ref,
                o_ref, sa_ref, sg_ref, xe_ref, *, nf):
    first = pl.program_id(1) == 0
    f = pl.program_id(2)
    tt = x_ref.shape[0]

    @pl.when(f == 0)
    def _():
        o_ref[...] = jnp.zeros_like(o_ref)
        halo = jnp.where(first, 0.0, xh_ref[...])
        xe_ref[0:FFN_HALO, :] = halo.astype(BF16)
        xe_ref[FFN_HALO:, :] = x_ref[...].astype(BF16)

    xe = xe_ref[...]

    def branch(w_ref, cw_ref, cb_ref, s_ref):
        up = _dot(xe, w_ref[...])
        s_ref[...] = up[FFN_HALO + tt - SUBLANES:FFN_HALO + tt, :]
        w = cw_ref[...]
        x1 = pltpu.roll(up, 1, 0)[FFN_HALO:, :]
        x2 = pltpu.roll(up, 2, 0)[FFN_HALO:, :]
        return cb_ref[...] + w[0:1, :] * x2 + w[1:2, :] * x1 + w[2:3, :] * up[FFN_HALO:, :]

    a = branch(wa_ref, cwa_ref, cba_ref, sa_ref)
    g = branch(wg_ref, cwg_ref, cbg_ref, sg_ref)
    o_ref[...] += _dot((a * jax.nn.sigmoid(a) * g).astype(BF16), wd_ref[...])

    @pl.when(f == nf - 1)
    def _():
        o_ref[...] = _layer_norm(DEEPNORM_ALPHA * x_ref[...] + o_ref[...], lw_ref[...], lb_ref[...])


def _ffn_prompt(x3, w_up, w_down, conv_w8, conv_b, lw, lb, *, tt, tf):
    b, t, _ = x3.shape
    nf = D_FFP // tf
    tt = min(tt, t)
    hb = tt // FFN_HALO
    half_a = lambda rows: pl.BlockSpec((rows, tf), lambda bi, ti, fi: (0, fi))
    half_g = lambda rows: pl.BlockSpec((rows, tf), lambda bi, ti, fi: (0, fi + nf))
    vec = pl.BlockSpec((1, D_MODEL), lambda bi, ti, fi: (0, 0))
    tail = pl.BlockSpec((None, SUBLANES, tf), lambda bi, ti, fi: (bi, 0, fi))
    return pl.pallas_call(
        functools.partial(_ffn_kernel, nf=nf),
        grid=(b, t // tt, nf),
        in_specs=[pl.BlockSpec((None, tt, D_MODEL), lambda bi, ti, fi: (bi, ti, 0), pipeline_mode=pl.Buffered(1)),
                  pl.BlockSpec((None, FFN_HALO, D_MODEL), lambda bi, ti, fi: (bi, jnp.maximum(ti * hb - 1, 0), 0)),
                  half_a(D_MODEL), half_g(D_MODEL),
                  pl.BlockSpec((tf, D_MODEL), lambda bi, ti, fi: (fi, 0)),
                  half_a(SUBLANES), half_g(SUBLANES), half_a(1), half_g(1), vec, vec],
        out_specs=[pl.BlockSpec((None, tt, D_MODEL), lambda bi, ti, fi: (bi, ti, 0)), tail, tail],
        out_shape=[jax.ShapeDtypeStruct((b, t, D_MODEL), F32),
                   jax.ShapeDtypeStruct((b, SUBLANES, D_FFP), F32),
                   jax.ShapeDtypeStruct((b, SUBLANES, D_FFP), F32)],
        scratch_shapes=[pltpu.VMEM((tt + FFN_HALO, D_MODEL), BF16)],
        compiler_params=_cparams("parallel", "arbitrary", "arbitrary"),
        name="ffn_prompt",
    )(x3, x3, w_up, w_up, w_down, conv_w8, conv_w8, conv_b, conv_b, lw, lb)


def _cumsum_rows(x):
    n = x.shape[0]
    rows = lax.broadcasted_iota(jnp.int32, (n, 1), 0)
    d = 1
    while d < n:
        x = x + jnp.where(rows >= d, pltpu.roll(x, d, 0), 0.0)
        d *= 2
    return x


def _log_sigmoid(x):
    return jnp.minimum(x, 0.0) - jnp.log1p(jnp.exp(-jnp.abs(x)))


def _mlstm_kernel(q_ref, k_ref, v_ref, og_ref, g_ref, gb_ref, nw_ref, c0_ref, n0_ref, m0_ref,
                  out_ref, c_ref, n_ref, m_ref, *, L, t_valid):
    ci = pl.program_id(1)
    lin = q_ref.shape[0]

    @pl.when(ci == 0)
    def _():
        c_ref[...] = c0_ref[...]
        n_ref[...] = n0_ref[...]
        m_ref[...] = m0_ref[...]

    def rows_of(ref):
        x = ref[...]
        if lin < L:
            x = jnp.concatenate([x, jnp.zeros((L - lin, x.shape[1]), x.dtype)], axis=0)
        return x

    rows = lax.broadcasted_iota(jnp.int32, (L, 1), 0)
    valid = (ci * L + rows) < t_valid
    pre = rows_of(g_ref) + gb_ref[...]
    lf = jnp.where(valid, _log_sigmoid(pre), 0.0)
    ig = jnp.where(valid, pre, NEG_INF)
    bcum = _cumsum_rows(lf)
    dt = (pltpu.roll(ig, GATE_MF - GATE_MI, 1) - bcum).T
    q_all, k_all, v_all, og_all = rows_of(q_ref), rows_of(k_ref), rows_of(v_ref), rows_of(og_ref)
    tri = lax.broadcasted_iota(jnp.int32, (L, L), 0) >= lax.broadcasted_iota(jnp.int32, (L, L), 1)
    nw = nw_ref[...]

    for h in range(MLSTM_HEADS):
        sl = slice(h * HEAD_DIM, (h + 1) * HEAD_DIM)
        q = q_all[:, sl]
        k = k_all[:, sl] * QK_SCALE
        v = v_all[:, sl]
        qb, kb, vb = q.astype(BF16), k.astype(BF16), v.astype(BF16)
        b_col = bcum[:, GATE_MF + h:GATE_MF + h + 1]
        ig_col = ig[:, GATE_MI + h:GATE_MI + h + 1]
        d_row = dt[GATE_MF + h:GATE_MF + h + 1, :]
        c_prev = c_ref[h]
        n_prev = n_ref[h:h + 1, :]
        m_prev = m_ref[h:h + 1, 0:1]

        dmat = jnp.where(tri, b_col + d_row, NEG_INF)
        g_col = b_col + m_prev
        m_row = jnp.maximum(jnp.max(dmat, axis=1, keepdims=True), g_col)
        a = jnp.exp(dmat - m_row) * _dot_nt(qb, kb)
        w_inter = jnp.exp(g_col - m_row)
        num = _dot(a.astype(BF16), vb) + w_inter * _dot(qb, c_prev.astype(BF16))
        den = jnp.sum(a, axis=1, keepdims=True) + w_inter * jnp.sum(q * n_prev, axis=1, keepdims=True)
        hid = num / jnp.maximum(jnp.abs(den), jnp.exp(-m_row))
        mu = jnp.mean(hid, axis=1, keepdims=True)
        dlt = hid - mu
        var = jnp.mean(dlt * dlt, axis=1, keepdims=True)
        hn = dlt * lax.rsqrt(var + LN_EPS) * nw[:, sl]
        res = jax.nn.sigmoid(og_all[:, sl]) * hn
        out_ref[:, sl] = res[0:lin, :]

        f_tot = b_col[L - 1:L, :]
        w_s = f_tot - b_col + ig_col
        m_new = jnp.maximum(f_tot + m_prev, jnp.max(w_s, axis=0, keepdims=True))
        ws = jnp.exp(w_s - m_new)
        decay = jnp.exp(f_tot + m_prev - m_new)
        c_ref[h] = decay * c_prev + _dot(k.T.astype(BF16), (ws * v).astype(BF16))
        n_ref[h:h + 1, :] = decay * n_prev + jnp.sum(ws * k, axis=0, keepdims=True)
        m_ref[h:h + 1, :] = jnp.broadcast_to(m_new, (1, LANES))


def _mlstm(z, gate_b, norm_w, c0, n0, m0, *, L, lin, t_valid):
    b, tz, _ = z.shape
    nchunks = tz // lin
    qblk = lambda col: pl.BlockSpec((None, lin, D_MLSTM), lambda bi, ci: (bi, ci, col // D_MLSTM))
    st4 = pl.BlockSpec((None, MLSTM_HEADS, HEAD_DIM, HEAD_DIM), lambda bi, ci: (bi, 0, 0, 0))
    st3 = pl.BlockSpec((None, MLSTM_HEADS, LANES), lambda bi, ci: (bi, 0, 0))
    return pl.pallas_call(
        functools.partial(_mlstm_kernel, L=L, t_valid=t_valid),
        grid=(b, nchunks),
        in_specs=[qblk(C_MQ), qblk(C_MK), qblk(C_MV), qblk(C_MO),
                  pl.BlockSpec((None, lin, LANES), lambda bi, ci: (bi, ci, C_GATE // LANES)),
                  pl.BlockSpec((1, LANES), lambda bi, ci: (0, 0)),
                  pl.BlockSpec((1, D_MLSTM), lambda bi, ci: (0, 0)),
                  st4, st3, st3],
        out_specs=[pl.BlockSpec((None, lin, D_MLSTM), lambda bi, ci: (bi, ci, 0)), st4, st3, st3],
        out_shape=[jax.ShapeDtypeStruct((b, tz, D_MLSTM), F32),
                   jax.ShapeDtypeStruct((b, MLSTM_HEADS, HEAD_DIM, HEAD_DIM), F32),
                   jax.ShapeDtypeStruct((b, MLSTM_HEADS, LANES), F32),
                   jax.ShapeDtypeStruct((b, MLSTM_HEADS, LANES), F32)],
        compiler_params=_cparams("parallel", "arbitrary"),
        name="mlstm",
    )(z, z, z, z, z, gate_b, norm_w, c0, n0, m0)


S5_IN_TILES = S5_CH // 256
S5_IN_FEATS = D_S5 // S5_IN_TILES
S5_OUT_TILES = D_S5 // 256
S5_OUT_CH = S5_CH // S5_OUT_TILES


def _s5_kernel(u_ref, wb_ref, wc_ref, lr_ref, li_ref, d_ref, gw_ref, gb_ref, x0r_ref, x0i_ref,
               o_ref, xr_ref, xi_ref, sr_ref, si_ref, *, lc, rb):
    @pl.when(pl.program_id(0) == 0)
    def _():
        xr_ref[...] = x0r_ref[...]
        xi_ref[...] = x0i_ref[...]

    u = u_ref[...]
    ub = u.astype(BF16)
    for c in range(S5_IN_TILES):
        bu = _dot(ub[:, c * S5_IN_FEATS:(c + 1) * S5_IN_FEATS], wb_ref[c])
        sr_ref[:, c * 256:(c + 1) * 256] = bu[:, 0:256]
        si_ref[:, c * 256:(c + 1) * 256] = bu[:, 256:512]
    lam_r = jnp.broadcast_to(lr_ref[...], (SUBLANES, S5_CH))
    lam_i = jnp.broadcast_to(li_ref[...], (SUBLANES, S5_CH))
    row8 = lax.broadcasted_iota(jnp.int32, (SUBLANES, 1), 0)
    per_group = SUBLANES // rb

    def group(i, carry):
        xr, xi = carry
        r0 = pl.multiple_of(i * SUBLANES, SUBLANES)
        br = sr_ref[pl.ds(r0, SUBLANES), :]
        bi = si_ref[pl.ds(r0, SUBLANES), :]
        out_r, out_i = br, bi
        for k in range(per_group):
            nr = lam_r * xr - lam_i * xi + br
            ni = lam_r * xi + lam_i * xr + bi
            here = (row8 >= k * rb) & (row8 < (k + 1) * rb)
            out_r = jnp.where(here, nr, out_r)
            out_i = jnp.where(here, ni, out_i)
            if per_group > 1:
                xr = pltpu.roll(nr, rb, 0)
                xi = pltpu.roll(ni, rb, 0)
            else:
                xr, xi = nr, ni
        sr_ref[pl.ds(r0, SUBLANES), :] = out_r
        si_ref[pl.ds(r0, SUBLANES), :] = out_i
        return xr, xi

    xr, xi = lax.fori_loop(0, lc * rb // SUBLANES, group, (xr_ref[...], xi_ref[...]))
    xr_ref[...] = xr
    xi_ref[...] = xi
    for j in range(S5_OUT_TILES):
        ch = slice(j * S5_OUT_CH, (j + 1) * S5_OUT_CH)
        cols = slice(j * 256, (j + 1) * 256)
        y = _dot(sr_ref[:, ch].astype(BF16), wc_ref[0, j]) + _dot(si_ref[:, ch].astype(BF16), wc_ref[1, j])
        o_ref[:, cols] = y + d_ref[:, cols] * u[:, cols]
    zz = jax.nn.gelu(o_ref[...])
    o_ref[...] = zz * jax.nn.sigmoid(_dot(zz.astype(BF16), gw_ref[...]) + gb_ref[...])


def _s5(u_tm, wb, wc, lam_r, lam_i, d, glu_w, glu_b, x0r, x0i, *, lc, rb):
    rows = u_tm.shape[0]
    t = rows // rb
    lc = min(lc, t)
    full = lambda a: pl.BlockSpec(a.shape, lambda i: (0,) * a.ndim)
    st = pl.BlockSpec((SUBLANES, S5_CH), lambda i: (0, 0))
    return pl.pallas_call(
        functools.partial(_s5_kernel, lc=lc, rb=rb),
        grid=(t // lc,),
        in_specs=[pl.BlockSpec((lc * rb, D_S5), lambda i: (i, 0)),
                  full(wb), full(wc), full(lam_r), full(lam_i), full(d), full(glu_w), full(glu_b), st, st],
        out_specs=[pl.BlockSpec((lc * rb, D_S5), lambda i: (i, 0)), st, st],
        out_shape=[jax.ShapeDtypeStruct((rows, D_S5), F32),
                   jax.ShapeDtypeStruct((SUBLANES, S5_CH), F32),
                   jax.ShapeDtypeStruct((SUBLANES, S5_CH), F32)],
        scratch_shapes=[pltpu.VMEM((lc * rb, S5_CH), F32), pltpu.VMEM((lc * rb, S5_CH), F32)],
        compiler_params=_cparams("arbitrary"),
        name="s5",
    )(u_tm, wb, wc, lam_r, lam_i, d, glu_w, glu_b, x0r, x0i)


N_CMB = 2 * KV_HEADS


def _cmp_project(rows_of, w_ref, o_ref):
    nch = o_ref.shape[0]
    for cmb in range(N_CMB):
        slot = cmb // KV_HEADS
        rows_ref = rows_of(cmb)
        acc = jnp.zeros((nch, 2 * HEAD_DIM), F32)
        for l in range(CMP_STRIDE):
            x = rows_ref[pl.ds(l, nch, stride=CMP_STRIDE), :]
            acc += _dot(x.astype(BF16), w_ref[slot, l])
        o_ref[:, cmb * 2 * HEAD_DIM:(cmb + 1) * 2 * HEAD_DIM] = acc


def _cmp_p_kernel(r0_ref, r1_ref, r2_ref, r3_ref, w_ref, o_ref):
    rows = (r0_ref, r1_ref, r2_ref, r3_ref)
    _cmp_project(lambda cmb: rows[cmb], w_ref, o_ref)


def _cmp_project_prompt(z, wcat):
    b, t, _ = z.shape
    nch = t // CMP_STRIDE
    rows = lambda cmb: pl.BlockSpec((None, t, HEAD_DIM), lambda bi: (bi, 0, C_NKV // HEAD_DIM + cmb))
    return pl.pallas_call(
        _cmp_p_kernel,
        grid=(b,),
        in_specs=[rows(cmb) for cmb in range(N_CMB)] + [pl.BlockSpec(wcat.shape, lambda bi: (0, 0, 0, 0))],
        out_specs=pl.BlockSpec((None, nch, N_CMB * 2 * HEAD_DIM), lambda bi: (bi, 0, 0)),
        out_shape=jax.ShapeDtypeStruct((b, nch, N_CMB * 2 * HEAD_DIM), F32),
        compiler_params=_cparams("parallel"),
        name="cmp_project_prompt",
    )(z, z, z, z, wcat)


def _cmp_s_kernel(pt_ref, *refs):
    del pt_ref
    page_refs = [r.reshape(PAGE_SIZE * KV_HEADS, HEAD_DIM) for r in refs[:2 * CMP_PAGES]]
    w_ref, o_ref = refs[2 * CMP_PAGES:]
    per_page = PAGE_SIZE // CMP_STRIDE
    for slot in range(2):
        for g in range(KV_HEADS):
            cmb = slot * KV_HEADS + g
            acc = jnp.zeros((o_ref.shape[0], 2 * HEAD_DIM), F32)
            for l in range(CMP_STRIDE):
                x = jnp.concatenate(
                    [page_refs[slot * CMP_PAGES + p][pl.ds(l * KV_HEADS + g, per_page, stride=CMP_STRIDE * KV_HEADS), :]
                     for p in range(CMP_PAGES)], axis=0)
                acc += _dot(x.astype(BF16), w_ref[slot, l])
            o_ref[:, cmb * 2 * HEAD_DIM:(cmb + 1) * 2 * HEAD_DIM] = acc


def _page_spec(layer, slot, p, pages_per_step, grid_rank):
    def index_map(*idx):
        bi, si, pt = idx[0], idx[grid_rank - 1], idx[grid_rank]
        return (layer, pt[bi, si * pages_per_step + p], 0, slot, 0, 0)

    return pl.BlockSpec((None, None, PAGE_SIZE, None, KV_HEADS, HEAD_DIM), index_map)


def _cmp_project_sample(cache6, page_table, layer, wcat):
    b, npages = page_table.shape
    steps = npages // CMP_PAGES
    nch = CMP_PAGES * PAGE_SIZE // CMP_STRIDE
    pages = [_page_spec(layer, slot, p, CMP_PAGES, 2) for slot in range(2) for p in range(CMP_PAGES)]

    grid_spec = pltpu.PrefetchScalarGridSpec(
        num_scalar_prefetch=1,
        grid=(b, steps),
        in_specs=pages + [pl.BlockSpec(wcat.shape, lambda bi, si, pt: (0, 0, 0, 0))],
        out_specs=pl.BlockSpec((None, nch, N_CMB * 2 * HEAD_DIM), lambda bi, si, pt: (bi, si, 0)),
    )
    return pl.pallas_call(
        _cmp_s_kernel,
        grid_spec=grid_spec,
        out_shape=jax.ShapeDtypeStruct((b, steps * nch, N_CMB * 2 * HEAD_DIM), F32),
        compiler_params=_cparams("parallel", "arbitrary"),
        name="cmp_project_sample",
    )(page_table, *([cache6] * (2 * CMP_PAGES)), wcat)


def _cmp_fin_kernel(p_ref, b1_ref, w2_ref, o_ref):
    nch = p_ref.shape[0]
    for cmb in range(N_CMB):
        slot = cmb // KV_HEADS
        c0 = cmb * 2 * HEAD_DIM
        first = p_ref[:, c0:c0 + HEAD_DIM]
        second = pltpu.roll(p_ref[:, c0 + HEAD_DIM:c0 + 2 * HEAD_DIM], nch - 1, 0)
        hid = b1_ref[slot:slot + 1, :] + first + second
        o_ref[cmb] = _dot(jax.nn.gelu(hid).astype(BF16), w2_ref[slot])


def _cmp_finish(p, b1, w2):
    b, nch, _ = p.shape
    return pl.pallas_call(
        _cmp_fin_kernel,
        grid=(b,),
        in_specs=[pl.BlockSpec((None, nch, p.shape[2]), lambda bi: (bi, 0, 0)),
                  pl.BlockSpec(b1.shape, lambda bi: (0, 0)),
                  pl.BlockSpec(w2.shape, lambda bi: (0, 0, 0))],
        out_specs=pl.BlockSpec((None, N_CMB, nch, HEAD_DIM), lambda bi: (bi, 0, 0, 0)),
        out_shape=jax.ShapeDtypeStruct((b, N_CMB, nch, HEAD_DIM), F32),
        compiler_params=_cparams("parallel"),
        name="cmp_finish",
    )(p, b1, w2)


def _softmax_init(m_scr, l_scr, acc_scr):
    m_scr[...] = jnp.full(m_scr.shape, M_INIT, F32)
    l_scr[...] = jnp.zeros(l_scr.shape, F32)
    acc_scr[...] = jnp.zeros(acc_scr.shape, F32)


def _softmax_update(s, mask, vb, m_scr, l_scr, acc_scr):
    s = jnp.where(mask, s, NEG_INF)
    m_prev = m_scr[...]
    m_new = jnp.maximum(m_prev, jnp.max(s, axis=1, keepdims=True))
    alpha = jnp.exp(m_prev - m_new)
    p = jnp.exp(s - m_new)
    l_scr[...] = alpha * l_scr[...] + jnp.sum(p, axis=1, keepdims=True)
    acc_scr[...] = alpha * acc_scr[...] + _dot(p.astype(BF16), vb)
    m_scr[...] = m_new


def _softmax_result(l_scr, acc_scr):
    l = l_scr[...]
    return acc_scr[...] / jnp.where(l > 0, l, 1.0)


def _masked_probs(s, mask):
    s = jnp.where(mask, s, NEG_INF)
    m = jnp.max(s, axis=1, keepdims=True)
    m = jnp.where(m > NEG_INF, m, 0.0)
    p = jnp.exp(s - m)
    den = jnp.sum(p, axis=1, keepdims=True)
    return p / jnp.where(den > 0, den, 1.0)


def _stack_heads(q):
    return jnp.concatenate([q[:, j * HEAD_DIM:(j + 1) * HEAD_DIM] for j in range(GROUP)], axis=0)


def _block_importance(pc, n_cmp, n_sel):
    ncp = pc.shape[1]
    nsp = -(-n_sel // LANES) * LANES
    c_start = lax.broadcasted_iota(jnp.int32, (ncp, nsp), 0) * CMP_STRIDE
    s_start = lax.broadcasted_iota(jnp.int32, (ncp, nsp), 1) * SEL_BLOCK
    overlap = ((c_start < s_start + SEL_BLOCK) & (c_start + CMP_BLOCK > s_start)
               & (c_start < n_cmp * CMP_STRIDE)).astype(F32)
    return jnp.dot(pc, overlap, preferred_element_type=F32, precision=lax.Precision.HIGHEST)


def _force_blocks(imp, blk, cur):
    forced = (blk == 0) | (blk == cur) | (blk == cur - 1)
    imp = jnp.where(forced, FORCE_SCORE, imp)
    return jnp.where(blk > cur, -1.0, imp)


def _select_blocks_cols(pc, cur, n_cmp, n_sel):
    imp = _block_importance(pc, n_cmp, n_sel)
    blk = lax.broadcasted_iota(jnp.int32, (1, imp.shape[1]), 1)
    imp = _force_blocks(imp, blk, cur)

    def body(sp, count):
        col = jnp.sum(jnp.where(blk == sp, imp, 0.0), axis=1, keepdims=True)
        ahead = (col > imp) | ((col == imp) & (sp < blk))
        return count + ahead.astype(F32)

    count = lax.fori_loop(0, n_sel, body, jnp.zeros(imp.shape, F32), unroll=8)
    return ((count < min(SEL_TOPK, n_sel)) & (blk < n_sel)).astype(F32)


def _select_blocks_rows(pc, cur_row, n_cmp, n_sel):
    rows = pc.shape[0]
    nrow = -(-n_sel // SUBLANES) * SUBLANES
    imp_t = _block_importance(pc, n_cmp, n_sel).T[0:nrow, :]
    blk = lax.broadcasted_iota(jnp.int32, (nrow, 1), 0)
    imp_t = _force_blocks(imp_t, blk, cur_row)
    count = jnp.zeros(imp_t.shape, F32)
    for sp in range(n_sel):
        row = imp_t[sp:sp + 1, :]
        count += ((row > imp_t) | ((row == imp_t) & (sp < blk))).astype(F32)
    sel_t = ((count < min(SEL_TOPK, n_sel)) & (blk < n_sel)).astype(F32)
    sel_t = jnp.concatenate([sel_t, jnp.zeros((LANES - nrow, rows), F32)], axis=0)
    return sel_t.T


def _gate(gt, col):
    lane = lax.broadcasted_iota(jnp.int32, (1, LANES), 1)
    return jax.nn.sigmoid(jnp.sum(jnp.where(lane == col, gt, 0.0), axis=1, keepdims=True))


def _nsa_prompt_kernel(q_ref, ks_ref, vs_ref, kw_ref, vw_ref, kc_ref, vc_ref, bc_ref, tz_ref, gt_ref, gb_ref,
                       o_ref, s_scr, selb_scr, mx_scr, l_scr, acc_scr, *, t_len):
    g = pl.program_id(1)
    qt = pl.program_id(2)
    q0 = qt * TQ
    n_cmp = t_len // CMP_STRIDE - (CMP_BLOCK // CMP_STRIDE) + 1
    n_sel = t_len // SEL_BLOCK
    ncp = kc_ref.shape[0]
    rows4 = GROUP * TQ
    qs = _stack_heads(q_ref[...] * QK_SCALE).astype(BF16)
    qpos = q0 + (lax.broadcasted_iota(jnp.int32, (rows4, 1), 0) & (TQ - 1))

    n_idx = lax.broadcasted_iota(jnp.int32, (1, ncp), 1)
    s_c = _dot_nt(qs, kc_ref[...].astype(BF16)) + bc_ref[...].reshape(rows4, ncp)
    cmask = (qpos - (n_idx * CMP_STRIDE + CMP_BLOCK - 1) >= 0) & (n_idx < n_cmp)
    p_c = _masked_probs(s_c, cmask)
    o_c = _dot(p_c.astype(BF16), vc_ref[...].astype(BF16))
    pc = p_c[0:TQ] + p_c[TQ:2 * TQ] + p_c[2 * TQ:3 * TQ] + p_c[3 * TQ:4 * TQ]
    cur_row = jnp.right_shift(q0 + lax.broadcasted_iota(jnp.int32, (1, TQ), 1), SEL_SHIFT)
    sel = _select_blocks_rows(pc, cur_row, n_cmp, n_sel)
    sel_add = ((sel - 1.0) * -MASKED).astype(BF16)

    kt_hi = (q0 + TQ - 1) // TK + 1
    kt_far = jnp.maximum(q0 - (FAR_DIST - 1), 0) // TK
    kidx = lax.broadcasted_iota(jnp.int32, (1, TK), 1)
    srow = lax.broadcasted_iota(jnp.int32, (LANES, TK), 0)

    def spread_selection(kt, carry):
        k0 = kt * TK
        expand = (jnp.right_shift(k0 + lax.broadcasted_iota(jnp.int32, (LANES, TK), 1), SEL_SHIFT) == srow).astype(BF16)
        selb_scr[kt] = _dot(sel_add, expand)
        return carry

    lax.fori_loop(0, kt_hi, spread_selection, 0)

    def tile_bias(k0):
        return tz_ref[jnp.clip((q0 - k0) // TQ, 0, N_TZ - 1)].reshape(rows4, TK)

    def attend(k_ref, v_ref, kt_lo, mask_tile):
        mx_scr[...] = jnp.full(mx_scr.shape, NEG_INF, F32)

        def scores(near):
            def body(kt, carry):
                k0 = pl.multiple_of(kt * TK, TK)
                s = _dot_nt(qs, k_ref[pl.ds(k0, TK), :].astype(BF16))
                if near:
                    s = s + tile_bias(k0)
                s = mask_tile(k0, s, near)
                s_scr[kt] = s
                mx_scr[...] = jnp.maximum(mx_scr[...], jnp.maximum(s[:, 0:LANES], s[:, LANES:TK]))
                return carry
            return body

        kt_mid = jnp.maximum(kt_far, kt_lo)
        lax.fori_loop(kt_lo, kt_mid, scores(False), 0)
        lax.fori_loop(kt_mid, kt_hi, scores(True), 0)
        m = jnp.max(mx_scr[...], axis=1, keepdims=True)
        mx_scr[...] = jnp.broadcast_to(jnp.where(m > NEG_INF, m, 0.0), mx_scr.shape)
        l_scr[...] = jnp.zeros(l_scr.shape, F32)
        acc_scr[...] = jnp.zeros(acc_scr.shape, F32)

        def probs(kt, carry):
            k0 = pl.multiple_of(kt * TK, TK)
            m_rep = mx_scr[...]
            p = jnp.exp(s_scr[kt] - jnp.concatenate([m_rep, m_rep], axis=1))
            l_scr[...] += p[:, 0:LANES] + p[:, LANES:TK]
            acc_scr[...] += _dot(p.astype(BF16), v_ref[pl.ds(k0, TK), :].astype(BF16))
            return carry

        lax.fori_loop(kt_lo, kt_hi, probs, 0)
        l = jnp.sum(l_scr[...], axis=1, keepdims=True)
        return acc_scr[...] / jnp.where(l > 0, l, 1.0)

    def sel_mask(k0, s, near):
        s = (s.reshape(GROUP, TQ, TK) + selb_scr[k0 // TK][None]).reshape(rows4, TK)
        return jnp.where(k0 + kidx <= qpos, s, NEG_INF) if near else s

    o_s = attend(ks_ref, vs_ref, 0, sel_mask)

    def win_mask(k0, s, near):
        dist = qpos - (k0 + kidx)
        return jnp.where((dist >= 0) & (dist <= WINDOW), s, NEG_INF)

    o_w = attend(kw_ref, vw_ref, jnp.maximum(q0 - WINDOW, 0) // TK, win_mask)

    gt = gt_ref[...] + gb_ref[...]
    for j in range(GROUP):
        col = GATE_NG + (g * GROUP + j) * 3
        r = slice(j * TQ, (j + 1) * TQ)
        o_ref[:, j * HEAD_DIM:(j + 1) * HEAD_DIM] = (
            _gate(gt, col) * o_c[r] + _gate(gt, col + 1) * o_s[r] + _gate(gt, col + 2) * o_w[r])


def _nsa_prompt(z, kcvc, bias_c, tz, gate_b):
    b, t, _ = z.shape
    ncp = kcvc.shape[2]
    nq = t // TQ
    kv = lambda slot: pl.BlockSpec((None, t, HEAD_DIM), lambda bi, g, qi: (bi, 0, C_NKV // HEAD_DIM + slot * KV_HEADS + g))
    cmp_blk = lambda slot: pl.BlockSpec((None, None, ncp, HEAD_DIM), lambda bi, g, qi: (bi, slot * KV_HEADS + g, 0, 0))
    return pl.pallas_call(
        functools.partial(_nsa_prompt_kernel, t_len=t),
        grid=(b, KV_HEADS, nq),
        in_specs=[pl.BlockSpec((None, TQ, GROUP * HEAD_DIM), lambda bi, g, qi: (bi, qi, C_NQ // (GROUP * HEAD_DIM) + g)),
                  kv(2), kv(3), kv(4), kv(5), cmp_blk(0), cmp_blk(1),
                  pl.BlockSpec((GROUP, TQ, ncp), lambda bi, g, qi: (g, qi, 0)),
                  pl.BlockSpec((N_TZ, GROUP, TQ, TK), lambda bi, g, qi: (0, g, 0, 0)),
                  pl.BlockSpec((None, TQ, LANES), lambda bi, g, qi: (bi, qi, C_GATE // LANES)),
                  pl.BlockSpec((1, LANES), lambda bi, g, qi: (0, 0))],
        out_specs=pl.BlockSpec((None, TQ, GROUP * HEAD_DIM), lambda bi, g, qi: (bi, qi, g)),
        out_shape=jax.ShapeDtypeStruct((b, t, D_NSA), F32),
        scratch_shapes=[pltpu.VMEM((t // TK, GROUP * TQ, TK), F32), pltpu.VMEM((t // TK, TQ, TK), F32),
                        pltpu.VMEM((GROUP * TQ, LANES), F32),
                        pltpu.VMEM((GROUP * TQ, LANES), F32), pltpu.VMEM((GROUP * TQ, HEAD_DIM), F32)],
        compiler_params=_cparams("parallel", "parallel", "arbitrary"),
        name="nsa_prompt",
    )(z, z, z, z, z, kcvc, kcvc, bias_c, tz, z, gate_b)


TS = SUBLANES
ROWS_S = GROUP * TS


def _nsa_s_cmp_kernel(q_ref, kc_ref, vc_ref, bc_ref, oc_ref, sel_ref, *, pos0, n_cmp, n_sel):
    nch = kc_ref.shape[0]
    qs = _stack_heads(q_ref[...] * QK_SCALE).astype(BF16)
    qpos = pos0 + (lax.broadcasted_iota(jnp.int32, (ROWS_S, 1), 0) & (TS - 1))
    n_idx = lax.broadcasted_iota(jnp.int32, (1, nch), 1)
    s_c = _dot_nt(qs, kc_ref[...].astype(BF16)) + bc_ref[...].reshape(ROWS_S, nch)
    cmask = (qpos - (n_idx * CMP_STRIDE + CMP_BLOCK - 1) >= 0) & (n_idx < n_cmp)
    p_c = _masked_probs(s_c, cmask)
    oc_ref[...] = _dot(p_c.astype(BF16), vc_ref[...].astype(BF16))
    pc = p_c[0:TS] + p_c[TS:2 * TS] + p_c[2 * TS:3 * TS] + p_c[3 * TS:4 * TS]
    sel_ref[...] = _select_blocks_cols(pc, jnp.right_shift(qpos[0:TS], SEL_SHIFT), n_cmp, n_sel)


def _nsa_sample_cmp(z8, kcvc, bias_c, *, pos0, n_cmp, n_sel):
    b = z8.shape[0]
    nch = kcvc.shape[2]
    nsp = -(-n_sel // LANES) * LANES
    cmp_blk = lambda slot: pl.BlockSpec((None, None, nch, HEAD_DIM), lambda bi, g: (bi, slot * KV_HEADS + g, 0, 0))
    return pl.pallas_call(
        functools.partial(_nsa_s_cmp_kernel, pos0=pos0, n_cmp=n_cmp, n_sel=n_sel),
        grid=(b, KV_HEADS),
        in_specs=[pl.BlockSpec((None, TS, GROUP * HEAD_DIM), lambda bi, g: (bi, 0, C_NQ // (GROUP * HEAD_DIM) + g)),
                  cmp_blk(0), cmp_blk(1),
                  pl.BlockSpec((GROUP, TS, nch), lambda bi, g: (g, 0, 0))],
        out_specs=[pl.BlockSpec((None, None, ROWS_S, HEAD_DIM), lambda bi, g: (bi, g, 0, 0)),
                   pl.BlockSpec((None, None, TS, nsp), lambda bi, g: (bi, g, 0, 0))],
        out_shape=[jax.ShapeDtypeStruct((b, KV_HEADS, ROWS_S, HEAD_DIM), F32),
                   jax.ShapeDtypeStruct((b, KV_HEADS, TS, nsp), F32)],
        compiler_params=_cparams("parallel", "parallel"),
        name="nsa_sample_cmp",
    )(z8, kcvc, kcvc, bias_c)


def _pad_rows(x, n):
    return jnp.concatenate([x, jnp.zeros((n - x.shape[0], x.shape[1]), x.dtype)], axis=0)


def _nsa_s_sel_kernel(pt_ref, q_ref, sel_ref, kn_ref, vn_ref, bt_ref, bn_ref, bf_ref, *refs, pos0, npages):
    del pt_ref
    rows_pp = PAGE_SIZE * KV_HEADS
    k_refs = [r.reshape(rows_pp, HEAD_DIM) for r in refs[:SEL_PAGES]]
    v_refs = [r.reshape(rows_pp, HEAD_DIM) for r in refs[SEL_PAGES:2 * SEL_PAGES]]
    o_ref, kbuf, vbuf, m_scr, l_scr, acc_scr = refs[2 * SEL_PAGES:]
    step = pl.program_id(1)
    last = step == npages // SEL_PAGES - 1
    rows = KV_HEADS * ROWS_S
    ncol = SEL_TILE * rows_pp
    head_shift = KV_HEADS.bit_length() - 1
    blk_per_page = PAGE_SIZE // SEL_BLOCK

    @pl.when(step == 0)
    def _():
        _softmax_init(m_scr, l_scr, acc_scr)

    nsp = sel_ref.shape[-1]
    ridx = lax.broadcasted_iota(jnp.int32, (rows, 1), 0)
    qpos = pos0 + (ridx & (TS - 1))
    qs = jnp.concatenate([_stack_heads(q_ref[:, g * GROUP * HEAD_DIM:(g + 1) * GROUP * HEAD_DIM] * QK_SCALE)
                          for g in range(KV_HEADS)], axis=0).astype(BF16)
    sel_rows = jnp.concatenate([sel_ref[g] for g in range(KV_HEADS) for _ in range(GROUP)], axis=0)
    bias_far = bf_ref[...].reshape(rows, PAGE_SIZE)[:, 0:1]
    col = lax.broadcasted_iota(jnp.int32, (1, ncol), 1)
    own_head = (col & (KV_HEADS - 1)) == ridx // ROWS_S
    expand = (jnp.right_shift(lax.broadcasted_iota(jnp.int32, (LANES, ncol), 1), SEL_SHIFT + head_shift)
              == lax.broadcasted_iota(jnp.int32, (LANES, ncol), 0)).astype(BF16)
    blk_r = lax.broadcasted_iota(jnp.int32, (nsp, LANES), 0)
    blk_c = lax.broadcasted_iota(jnp.int32, (nsp, LANES), 1)
    n_tiles = SEL_PAGES // SEL_TILE

    for tile in range(n_tiles):
        for i in range(SEL_TILE):
            p = tile * SEL_TILE + i
            kbuf[i * rows_pp:(i + 1) * rows_pp, :] = k_refs[p][...].astype(BF16)
            vbuf[i * rows_pp:(i + 1) * rows_pp, :] = v_refs[p][...].astype(BF16)
        page0 = step * SEL_PAGES + tile * SEL_TILE
        window = ((blk_r == page0 * blk_per_page + blk_c) & (blk_c < SEL_TILE * blk_per_page)).astype(BF16)
        chosen = _dot(_dot(sel_rows.astype(BF16), window).astype(BF16), expand) > 0.5
        s = _dot_nt(qs, kbuf[...]) + bias_far
        if tile == n_tiles - 1:
            tail = jnp.where(last, bt_ref[...].reshape(rows, rows_pp) - bias_far, 0.0)
            s = s + jnp.concatenate([jnp.zeros((rows, ncol - rows_pp), F32), tail], axis=1)
        key = page0 * PAGE_SIZE + jnp.right_shift(col, head_shift)
        _softmax_update(s, chosen & own_head & (key <= qpos), vbuf[...], m_scr, l_scr, acc_scr)

    @pl.when(last)
    def _():
        kidx = lax.broadcasted_iota(jnp.int32, (1, PAGE_SIZE), 1)
        new_blk = lax.broadcasted_iota(jnp.int32, (1, nsp), 1) == npages * blk_per_page
        flag = jnp.sum(jnp.where(new_blk, sel_rows, 0.0), axis=1, keepdims=True) > 0.5
        bias_new = bn_ref[...].reshape(rows, PAGE_SIZE)
        for g in range(KV_HEADS):
            r = slice(g * ROWS_S, (g + 1) * ROWS_S)
            kn = _pad_rows(kn_ref[:, g * HEAD_DIM:(g + 1) * HEAD_DIM], PAGE_SIZE).astype(BF16)
            vn = _pad_rows(vn_ref[:, g * HEAD_DIM:(g + 1) * HEAD_DIM], PAGE_SIZE).astype(BF16)
            mask = flag[r] & (pos0 + kidx <= qpos[r]) & (kidx < TS)
            _softmax_update(_dot_nt(qs[r], kn) + bias_new[r], mask, vn, m_scr.at[r], l_scr.at[r], acc_scr.at[r])
            o_ref[g] = _softmax_result(l_scr.at[r], acc_scr.at[r])


def _nsa_sample_sel(z8, sel, bias_tail, bias_new, bias_far, cache6, page_table, layer, *, pos0):
    b, npages = page_table.shape
    nsp = sel.shape[-1]
    steps = npages // SEL_PAGES
    slot_cols = KV_HEADS * HEAD_DIM
    new_rows = lambda slot: pl.BlockSpec((None, TS, slot_cols),
                                         lambda bi, si, pt: (bi, 0, (C_NKV + slot * slot_cols) // slot_cols))
    whole = lambda a: pl.BlockSpec(a.shape, lambda bi, si, pt: (0,) * a.ndim)
    grid_spec = pltpu.PrefetchScalarGridSpec(
        num_scalar_prefetch=1,
        grid=(b, steps),
        in_specs=[pl.BlockSpec((None, TS, D_NSA), lambda bi, si, pt: (bi, 0, C_NQ // D_NSA)),
                  pl.BlockSpec((None, KV_HEADS, TS, nsp), lambda bi, si, pt: (bi, 0, 0, 0)),
                  new_rows(2), new_rows(3), whole(bias_tail), whole(bias_new), whole(bias_far)]
                 + [_page_spec(layer, 2, p, SEL_PAGES, 2) for p in range(SEL_PAGES)]
                 + [_page_spec(layer, 3, p, SEL_PAGES, 2) for p in range(SEL_PAGES)],
        out_specs=pl.BlockSpec((None, KV_HEADS, ROWS_S, HEAD_DIM), lambda bi, si, pt: (bi, 0, 0, 0)),
        scratch_shapes=[pltpu.VMEM((SEL_TILE * PAGE_SIZE * KV_HEADS, HEAD_DIM), BF16),
                        pltpu.VMEM((SEL_TILE * PAGE_SIZE * KV_HEADS, HEAD_DIM), BF16),
                        pltpu.VMEM((KV_HEADS * ROWS_S, 1), F32), pltpu.VMEM((KV_HEADS * ROWS_S, 1), F32),
                        pltpu.VMEM((KV_HEADS * ROWS_S, HEAD_DIM), F32)],
    )
    return pl.pallas_call(
        functools.partial(_nsa_s_sel_kernel, pos0=pos0, npages=npages),
        grid_spec=grid_spec,
        out_shape=jax.ShapeDtypeStruct((b, KV_HEADS, ROWS_S, HEAD_DIM), F32),
        compiler_params=_cparams("parallel", "arbitrary"),
        name="nsa_sample_sel",
    )(page_table, z8, sel, z8, z8, bias_tail, bias_new, bias_far, *([cache6] * (2 * SEL_PAGES)))


def _nsa_s_win_kernel(q_ref, kw_ref, vw_ref, kn_ref, vn_ref, bw_ref, oc_ref, os_ref, gt_ref, gb_ref, o_ref,
                      m_scr, l_scr, acc_scr, *, pos0):
    g = pl.program_id(1)
    wb = kw_ref.shape[0]
    qs = _stack_heads(q_ref[...] * QK_SCALE).astype(BF16)
    qpos = pos0 + (lax.broadcasted_iota(jnp.int32, (ROWS_S, 1), 0) & (TS - 1))
    bias = bw_ref[...].reshape(ROWS_S, wb + PAGE_SIZE)
    _softmax_init(m_scr, l_scr, acc_scr)
    dist = qpos - (pos0 - wb + lax.broadcasted_iota(jnp.int32, (1, wb), 1))
    s = _dot_nt(qs, kw_ref[...].astype(BF16)) + bias[:, 0:wb]
    _softmax_update(s, (dist >= 0) & (dist <= WINDOW), vw_ref[...].astype(BF16), m_scr, l_scr, acc_scr)
    kidx = lax.broadcasted_iota(jnp.int32, (1, PAGE_SIZE), 1)
    dist = qpos - (pos0 + kidx)
    s = _dot_nt(qs, _pad_rows(kn_ref[...], PAGE_SIZE).astype(BF16)) + bias[:, wb:wb + PAGE_SIZE]
    _softmax_update(s, (dist >= 0) & (dist <= WINDOW) & (kidx < TS), _pad_rows(vn_ref[...], PAGE_SIZE).astype(BF16),
                    m_scr, l_scr, acc_scr)
    o_w = _softmax_result(l_scr, acc_scr)
    o_c = oc_ref[...]
    o_s = os_ref[...]
    gt = gt_ref[...] + gb_ref[...]
    for j in range(GROUP):
        col = GATE_NG + (g * GROUP + j) * 3
        r = slice(j * TS, (j + 1) * TS)
        o_ref[:, j * HEAD_DIM:(j + 1) * HEAD_DIM] = (
            _gate(gt, col) * o_c[r] + _gate(gt, col + 1) * o_s[r] + _gate(gt, col + 2) * o_w[r])


def _nsa_sample_win(z8, win2d, bias_w, o_c, o_s, gate_b, *, pos0):
    b, wb, _ = win2d.shape
    zcol = lambda col: (lambda bi, g: (bi, 0, col // HEAD_DIM + g))
    part = pl.BlockSpec((None, None, ROWS_S, HEAD_DIM), lambda bi, g: (bi, g, 0, 0))
    return pl.pallas_call(
        functools.partial(_nsa_s_win_kernel, pos0=pos0),
        grid=(b, KV_HEADS),
        in_specs=[pl.BlockSpec((None, TS, GROUP * HEAD_DIM), lambda bi, g: (bi, 0, C_NQ // (GROUP * HEAD_DIM) + g)),
                  pl.BlockSpec((None, wb, HEAD_DIM), lambda bi, g: (bi, 0, g)),
                  pl.BlockSpec((None, wb, HEAD_DIM), lambda bi, g: (bi, 0, KV_HEADS + g)),
                  pl.BlockSpec((None, TS, HEAD_DIM), zcol(C_NKV + 4 * KV_HEADS * HEAD_DIM)),
                  pl.BlockSpec((None, TS, HEAD_DIM), zcol(C_NKV + 5 * KV_HEADS * HEAD_DIM)),
                  pl.BlockSpec((GROUP, TS, wb + PAGE_SIZE), lambda bi, g: (g, 0, 0)),
                  part, part,
                  pl.BlockSpec((None, TS, LANES), lambda bi, g: (bi, 0, C_GATE // LANES)),
                  pl.BlockSpec((1, LANES), lambda bi, g: (0, 0))],
        out_specs=pl.BlockSpec((None, TS, GROUP * HEAD_DIM), lambda bi, g: (bi, 0, g)),
        out_shape=jax.ShapeDtypeStruct((b, TS, D_NSA), F32),
        scratch_shapes=[pltpu.VMEM((ROWS_S, 1), F32), pltpu.VMEM((ROWS_S, 1), F32), pltpu.VMEM((ROWS_S, HEAD_DIM), F32)],
        compiler_params=_cparams("parallel", "parallel"),
        name="nsa_sample_win",
    )(z8, win2d, win2d, z8, z8, bias_w, o_c, o_s, z8, gate_b)


def _t5_bucket(dist):
    n = np.maximum(dist, 0)
    exact = NUM_BUCKETS // 2
    nf = np.maximum(n, 1).astype(np.float32)
    large = exact + (np.log(nf / np.float32(exact)) / np.float32(math.log(MAX_DISTANCE / exact))
                     * np.float32(NUM_BUCKETS - exact)).astype(np.int32)
    return np.where(n < exact, n, np.minimum(large, NUM_BUCKETS - 1)).astype(np.int32)


FAR_DIST = int(np.max(np.nonzero(_t5_bucket(np.arange(4 * MAX_DISTANCE)) < NUM_BUCKETS - 1)[0])) + 1
N_TZ = (FAR_DIST + TK - 2) // TQ + 1


def _bias_table(rel_bias, dist):
    onehot = jax.nn.one_hot(jnp.asarray(_t5_bucket(dist).astype(np.int8)), NUM_BUCKETS, dtype=F32)
    return jnp.einsum("rcb,bh->hrc", onehot, rel_bias, precision=lax.Precision.HIGHEST)


def _pack_layer(p, l):
    w_in = p["w_in"][l]
    pad = jnp.zeros((D_MODEL, N_IN - C_GATE - 32), F32)
    w_in_p = jnp.concatenate([w_in[:, 0:2048], w_in[:, 2056:3080], w_in[:, 3080:4616], w_in[:, 4640:5152],
                              w_in[:, 2048:2056], w_in[:, 4616:4640], pad], axis=1).astype(BF16)
    gb = p["mlstm_gate_b"][l]
    mlstm_gate_b = jnp.zeros((1, LANES), F32).at[0, GATE_MI:GATE_MI + 4].set(gb[0]).at[0, GATE_MF:GATE_MF + 4].set(gb[1])
    nsa_gate_b = jnp.zeros((1, LANES), F32).at[0, GATE_NG:GATE_NG + 3 * NSA_HEADS].set(p["nsa_gate_b"][l].reshape(-1))
    w1 = p["cmp_w1"][l]
    wcat = jnp.concatenate([w1[:, 0:CMP_STRIDE], w1[:, CMP_STRIDE:CMP_BLOCK]], axis=-1).astype(BF16)
    lam = lax.complex(p["s5_a_re"][l], p["s5_a_im"][l])
    lam_bar = jnp.exp(lam * jnp.exp(p["s5_log_step"][l])[:, None])
    b_bar = ((lam_bar - 1.0) / lam)[..., None] * lax.complex(p["s5_b_re"][l], p["s5_b_im"][l])
    gi = S5_GROUPS // S5_IN_TILES
    bd_in = lambda m: jnp.einsum("cgph,gk->cghkp", m.reshape(S5_IN_TILES, gi, S5_STATE, S5_GROUP_WIDTH),
                                 jnp.eye(gi, dtype=F32)).reshape(S5_IN_TILES, S5_IN_FEATS, 256)
    wb = jnp.concatenate([bd_in(b_bar.real), bd_in(b_bar.imag)], axis=2).astype(BF16)
    go = S5_GROUPS // S5_OUT_TILES
    bd_out = lambda m: jnp.einsum("jghp,gk->jgpkh", m.reshape(S5_OUT_TILES, go, S5_GROUP_WIDTH, S5_STATE),
                                  jnp.eye(go, dtype=F32)).reshape(S5_OUT_TILES, S5_OUT_CH, 256)
    wc = jnp.stack([bd_out(p["s5_c_re"][l]), -bd_out(p["s5_c_im"][l])]).astype(BF16)
    half = lambda w: jnp.pad(w, ((0, 0), (0, D_FFP - D_FF)))
    w_up = p["ffn_w_up"][l]
    conv_w = p["ffn_conv_w"][l]
    conv_b = p["ffn_conv_b"][l][None, :]
    return dict(
        w_in=w_in_p, mlstm_gate_b=mlstm_gate_b, mlstm_norm_w=p["mlstm_norm_w"][l][None, :], nsa_gate_b=nsa_gate_b,
        wcat=wcat, cmp_b1=p["cmp_b1"][l], cmp_w2=p["cmp_w2"][l].astype(BF16),
        s5_wb=wb, s5_wc=wc, s5_lam_r=lam_bar.real.reshape(1, S5_CH), s5_lam_i=lam_bar.imag.reshape(1, S5_CH),
        s5_d=p["s5_d"][l][None, :], s5_glu_w=p["s5_glu_w"][l].astype(BF16), s5_glu_b=p["s5_glu_b"][l][None, :],
        w_out=p["w_out"][l].astype(BF16), ln1_w=p["ln1_w"][l][None, :], ln1_b=p["ln1_b"][l][None, :],
        w_up=jnp.concatenate([half(w_up[:, :D_FF]), half(w_up[:, D_FF:])], axis=1).astype(BF16),
        conv_w8=jnp.pad(jnp.concatenate([half(conv_w[:, :D_FF]), half(conv_w[:, D_FF:])], axis=1),
                        ((0, SUBLANES - CONV_W), (0, 0))),
        conv_b=jnp.concatenate([half(conv_b[:, :D_FF]), half(conv_b[:, D_FF:])], axis=1),
        w_down=jnp.pad(p["ffn_w_down"][l], ((0, D_FFP - D_FF), (0, 0))).astype(BF16),
        ln2_w=p["ln2_w"][l][None, :], ln2_b=p["ln2_b"][l][None, :],
    )


def _unpad_ff(x):
    return jnp.concatenate([x[..., :D_FF], x[..., D_FFP:D_FFP + D_FF]], axis=-1)


def _pad_ff(x):
    pad = [(0, 0)] * (x.ndim - 1) + [(0, D_FFP - D_FF)]
    return jnp.concatenate([jnp.pad(x[..., :D_FF], pad), jnp.pad(x[..., D_FF:], pad)], axis=-1)


def _time_major(x, b, t, t_use):
    return x.reshape(b, t, -1)[:, :t_use].transpose(1, 0, 2).reshape(t_use * b, -1)


def _batch_major(x, b, t, t_use):
    x = x.reshape(t_use, b, -1).transpose(1, 0, 2)
    return jnp.pad(x, ((0, 0), (0, t - t_use), (0, 0))).reshape(b * t, -1)


def _mixer_tail(x2d, z, b, t, t_use, lw, o_mlstm, o_nsa, s5_state, conv_state8, tm):
    pad8 = lambda s: jnp.pad(s.reshape(b, S5_CH), ((0, SUBLANES - b), (0, 0)))
    u_tm = _time_major(z[:, C_SU:C_SU + D_S5], b, t, t_use)
    o_s5_tm, xr, xi = _s5(u_tm, lw["s5_wb"], lw["s5_wc"], lw["s5_lam_r"], lw["s5_lam_i"], lw["s5_d"],
                          lw["s5_glu_w"], lw["s5_glu_b"], pad8(s5_state[0]), pad8(s5_state[1]), lc=S5_LC, rb=b)
    o_s5 = _batch_major(o_s5_tm, b, t, t_use)
    x1 = _wout_ln(o_mlstm.reshape(b * t, D_MLSTM), o_nsa.reshape(b * t, D_NSA), o_s5, x2d,
                  lw["w_out"], lw["ln1_w"], lw["ln1_b"], tm=256)
    s5_new = (xr[:b].reshape(b, S5_GROUPS, S5_STATE), xi[:b].reshape(b, S5_GROUPS, S5_STATE))
    if conv_state8 is None:
        x2, tail_a, tail_g = _ffn_prompt(x1.reshape(b, t, D_MODEL), lw["w_up"], lw["w_down"], lw["conv_w8"],
                                         lw["conv_b"], lw["ln2_w"], lw["ln2_b"], tt=FFN_TT, tf=FFN_TF)
        keep = slice(SUBLANES - (CONV_W - 1), SUBLANES)
        conv_new = jnp.concatenate([tail_a[:, keep, :D_FF], tail_g[:, keep, :D_FF]], axis=-1)
        return x2.reshape(b * t, D_MODEL), s5_new, conv_new
    up = _matmul(x1, lw["w_up"], tm=tm, tn=512)
    hgate = _convgate(up.reshape(b, t, 2 * D_FFP), conv_state8, lw["conv_w8"], lw["conv_b"], tt=256, tf=512)
    x2 = _down_ln(hgate.reshape(b * t, D_FFP), lw["w_down"], x1, lw["ln2_w"], lw["ln2_b"], tm=512, tk=1408)
    conv_new = _unpad_ff(up.reshape(b, t, 2 * D_FFP)[:, t_use - (CONV_W - 1):t_use])
    return x2, s5_new, conv_new


def _prompt_layer(x2d, b, t, lw, rel_bias):
    z, kv_rows, win_rows = _proj_in(x2d, lw["w_in"], tm=min(1024, b * t), tn=768)
    z3 = z.reshape(b, t, N_IN)
    zeros = lambda *s: jnp.zeros(s, F32)
    o_mlstm, c1, n1, m1 = _mlstm(z3, lw["mlstm_gate_b"], lw["mlstm_norm_w"],
                                 zeros(b, MLSTM_HEADS, HEAD_DIM, HEAD_DIM), zeros(b, MLSTM_HEADS, LANES),
                                 zeros(b, MLSTM_HEADS, LANES), L=MLSTM_L, lin=MLSTM_L, t_valid=t)
    kcvc = _cmp_finish(_cmp_project_prompt(z3, lw["wcat"]), lw["cmp_b1"], lw["cmp_w2"])
    ncp = t // CMP_STRIDE
    bias_c = _bias_table(rel_bias, np.arange(t)[:, None] - (np.arange(ncp) * CMP_STRIDE + CMP_BLOCK - 1)[None, :])
    ti = np.arange(TQ)[:, None] - np.arange(TK)[None, :]
    far = _bias_table(rel_bias, np.full((1, 1), FAR_DIST))
    tz = jnp.stack([_bias_table(rel_bias, d * TQ + ti) - far for d in range(N_TZ)])
    o_nsa = _nsa_prompt(z3, kcvc, bias_c, tz, lw["nsa_gate_b"])
    x2, s5_new, conv_new = _mixer_tail(
        x2d, z, b, t, t, lw, o_mlstm, o_nsa, (zeros(b, S5_GROUPS, S5_STATE), zeros(b, S5_GROUPS, S5_STATE)),
        None, tm=1024)
    wrows = min(WINDOW, t)
    state = (kv_rows.reshape(b, t, 4, KV_HEADS, HEAD_DIM), win_rows.reshape(b, t, 2, KV_HEADS, HEAD_DIM)[:, t - wrows:],
             c1, n1, m1[:, :, 0], s5_new[0], s5_new[1], conv_new)
    return x2, state


def _sample_layer(x2d, b, tn, lw, rel_bias, layer, cache6, page_table, win2d, mlstm_state, s5_state, conv_state):
    npages = page_table.shape[1]
    pos0 = npages * PAGE_SIZE
    z = _matmul(x2d, lw["w_in"], tm=1024, tn=768)
    z8 = z.reshape(b, TS, N_IN)
    c0, n0, m0 = mlstm_state
    o_mlstm, c1, n1, m1 = _mlstm(z8, lw["mlstm_gate_b"], lw["mlstm_norm_w"], c0, n0,
                                 jnp.broadcast_to(m0[:, :, None], (b, MLSTM_HEADS, LANES)),
                                 L=LANES, lin=TS, t_valid=tn)
    n_chunks = (pos0 + tn) // CMP_STRIDE
    n_cmp = n_chunks - CMP_BLOCK // CMP_STRIDE + 1
    n_sel = -(-(pos0 + tn) // SEL_BLOCK)
    kcvc = _cmp_finish(_cmp_project_sample(cache6, page_table, layer, lw["wcat"]), lw["cmp_b1"], lw["cmp_w2"])
    qpos = pos0 + np.arange(TS)[:, None]
    bias_c = _bias_table(rel_bias, qpos - (np.arange(n_chunks) * CMP_STRIDE + CMP_BLOCK - 1)[None, :])
    o_c, sel = _nsa_sample_cmp(z8, kcvc, bias_c, pos0=pos0, n_cmp=n_cmp, n_sel=n_sel)
    kk = np.arange(PAGE_SIZE)[None, :]
    bias_tail = _bias_table(rel_bias, np.repeat(qpos - (pos0 - PAGE_SIZE + kk), KV_HEADS, axis=1))
    bias_new = _bias_table(rel_bias, qpos - (pos0 + kk))
    bias_far = _bias_table(rel_bias, np.broadcast_to(qpos - (pos0 - 2 * PAGE_SIZE), (TS, PAGE_SIZE)))
    o_s = _nsa_sample_sel(z8, sel, bias_tail, bias_new, bias_far, cache6, page_table, layer, pos0=pos0)
    wb = win2d.shape[1]
    wk = np.arange(wb)[None, :]
    bias_w = _bias_table(rel_bias, np.concatenate([qpos - (pos0 - wb + wk), qpos - (pos0 + kk)], axis=1))
    o_nsa = _nsa_sample_win(z8, win2d, bias_w, o_c, o_s, lw["nsa_gate_b"], pos0=pos0)
    conv_state8 = jnp.pad(_pad_ff(conv_state), ((0, 0), (SUBLANES - (CONV_W - 1), 0), (0, 0)))
    x2, s5_new, conv_new = _mixer_tail(x2d, z, b, TS, tn, lw, o_mlstm, o_nsa, s5_state, conv_state8, tm=1024)
    nkv = z8[:, :tn, C_NKV:C_NKV + N_KV_SLOTS * KV_HEADS * HEAD_DIM].reshape(b, tn, N_KV_SLOTS, KV_HEADS, HEAD_DIM)
    state = (nkv[:, :, :4], nkv[:, :, 4:], c1, n1, m1[:, :, 0], s5_new[0], s5_new[1], conv_new)
    return x2, state


def kernel(x_prompt, x_sample, cache_nsa_kv, cache_win_kv, state_mlstm_c, state_mlstm_n, state_mlstm_m,
           state_s5_re, state_s5_im, state_ffn_conv, page_table, w_in, mlstm_gate_b, mlstm_norm_w,
           nsa_gate_b, cmp_w1, cmp_b1, cmp_w2, rel_bias, s5_a_re, s5_a_im, s5_b_re, s5_b_im, s5_c_re,
           s5_c_im, s5_d, s5_log_step, s5_glu_w, s5_glu_b, w_out, ln1_w, ln1_b, ffn_w_up, ffn_conv_w,
           ffn_conv_b, ffn_w_down, ln2_w, ln2_b):
    params = dict(w_in=w_in, mlstm_gate_b=mlstm_gate_b, mlstm_norm_w=mlstm_norm_w, nsa_gate_b=nsa_gate_b,
                  cmp_w1=cmp_w1, cmp_b1=cmp_b1, cmp_w2=cmp_w2, s5_a_re=s5_a_re, s5_a_im=s5_a_im, s5_b_re=s5_b_re,
                  s5_b_im=s5_b_im, s5_c_re=s5_c_re, s5_c_im=s5_c_im, s5_d=s5_d, s5_log_step=s5_log_step,
                  s5_glu_w=s5_glu_w, s5_glu_b=s5_glu_b, w_out=w_out, ln1_w=ln1_w, ln1_b=ln1_b, ffn_w_up=ffn_w_up,
                  ffn_conv_w=ffn_conv_w, ffn_conv_b=ffn_conv_b, ffn_w_down=ffn_w_down, ln2_w=ln2_w, ln2_b=ln2_b)
    depth = w_in.shape[0]
    bp, tp, _ = x_prompt.shape
    bs, tn, _ = x_sample.shape
    assert tp % TK == 0 and tn < CMP_STRIDE and tn <= TS and SUBLANES % bp == 0 and SUBLANES % bs == 0
    assert (tn * bs) % SUBLANES == 0 and tp % min(FFN_TT, tp) == 0
    assert page_table.shape[1] % CMP_PAGES == 0 and cache_nsa_kv.shape[2] == PAGE_SIZE
    win2d = cache_win_kv.reshape(depth, bs, cache_win_kv.shape[2], 2 * KV_HEADS * HEAD_DIM)
    xp = x_prompt.reshape(bp * tp, D_MODEL)
    xs = jnp.pad(x_sample, ((0, 0), (0, TS - tn), (0, 0))).reshape(bs * TS, D_MODEL)
    p_states, s_states = [], []
    for l in range(depth):
        lw = _pack_layer(params, l)
        xp, sp = _prompt_layer(xp, bp, tp, lw, rel_bias)
        xs, ss = _sample_layer(xs, bs, tn, lw, rel_bias, l, cache_nsa_kv, page_table, win2d[l],
                               (state_mlstm_c[l], jnp.pad(state_mlstm_n[l], ((0, 0), (0, 0), (0, LANES - HEAD_DIM))),
                                state_mlstm_m[l]),
                               (state_s5_re[l], state_s5_im[l]), state_ffn_conv[l])
        p_states.append(sp)
        s_states.append(ss)
    stk = lambda states, i: jnp.stack([s[i] for s in states])
    y_prompt = xp.reshape(bp, tp, D_MODEL)
    y_sample = xs.reshape(bs, TS, D_MODEL)[:, :tn]
    return (y_prompt, y_sample,
            stk(p_states, 0), stk(s_states, 0), stk(p_states, 1), stk(s_states, 1),
            stk(p_states, 2), stk(s_states, 2), stk(p_states, 3), stk(s_states, 3), stk(p_states, 4), stk(s_states, 4),
            stk(p_states, 5), stk(s_states, 5), stk(p_states, 6), stk(s_states, 6), stk(p_states, 7), stk(s_states, 7))
```

```python
import functools
import math

import jax
import jax.numpy as jnp
import numpy as np
from jax import lax
from jax.experimental import pallas as pl
from jax.experimental.pallas import tpu as pltpu

F32 = jnp.float32
BF16 = jnp.bfloat16
NEG_INF = float("-inf")
M_INIT = -1e30
MASKED = -1e30

D_MODEL = 2048
PAGE_SIZE = 128
D_MLSTM = D_MODEL // 4
D_NSA = D_MODEL // 2
D_S5 = D_MODEL - D_MLSTM - D_NSA
HEAD_DIM = 128
MLSTM_HEADS = D_MLSTM // HEAD_DIM
NSA_HEADS = D_NSA // HEAD_DIM
KV_HEADS = 2
GROUP = NSA_HEADS // KV_HEADS
N_KV_SLOTS = 6
CMP_BLOCK = 32
CMP_STRIDE = 16
SEL_BLOCK = 64
SEL_TOPK = 16
FORCE_SCORE = 1e4
WINDOW = 512
S5_GROUP_WIDTH = 16
S5_GROUPS = D_S5 // S5_GROUP_WIDTH
S5_STATE = 64
S5_CH = S5_GROUPS * S5_STATE
NUM_BUCKETS = 32
MAX_DISTANCE = 128
D_FF = ((8 * D_MODEL // 3 + 127) // 128) * 128
CONV_W = 3
LN_EPS = 1e-5
DEPTH = 2
DEEPNORM_ALPHA = (2 * DEPTH) ** 0.25
QK_SCALE = HEAD_DIM ** -0.5

LANES = 128
SUBLANES = 8
V7X_VMEM_LIMIT = 56 * 2 ** 20

C_MQ, C_MK, C_MV, C_MO = 0, 512, 1024, 1536
C_NQ = 2048
C_NKV = 3072
C_SU = 4608
C_GATE = 5120
N_IN = 5376
IN_TN = 768
GATE_MI, GATE_MF, GATE_NG = 0, 4, 8
D_FFP = 5632
MLSTM_L = 256
TQ = 256
TK = 256
S5_LC = 128
FFN_TT = 1024
FFN_TF = 512
FFN_HALO = 16
CMP_PAGES = 32
SEL_PAGES = 32
SEL_TILE = 16
SEL_SHIFT = SEL_BLOCK.bit_length() - 1


def _cparams(*sem):
    return pltpu.CompilerParams(dimension_semantics=sem, vmem_limit_bytes=V7X_VMEM_LIMIT)


def _dot(a, b):
    return jnp.dot(a, b, preferred_element_type=F32)


def _dot_nt(a, b):
    return lax.dot_general(a, b, (((1,), (1,)), ((), ())), preferred_element_type=F32)


def _layer_norm(y, w, b):
    mu = jnp.mean(y, axis=-1, keepdims=True)
    d = y - mu
    var = jnp.mean(d * d, axis=-1, keepdims=True)
    return d * lax.rsqrt(var + LN_EPS) * w + b


def _mm_kernel(x_ref, w_ref, o_ref):
    o_ref[...] = _dot(x_ref[...].astype(BF16), w_ref[...]).astype(o_ref.dtype)


def _col_tiles(w, tn):
    k, n = w.shape
    return w.reshape(k, n // tn, tn).transpose(1, 0, 2)


def _matmul(x, w, *, tm):
    m, k = x.shape
    nt, _, tn = w.shape
    tm = min(tm, m)
    return pl.pallas_call(
        _mm_kernel,
        grid=(m // tm, nt),
        in_specs=[pl.BlockSpec((tm, k), lambda i, j: (i, 0)), pl.BlockSpec((None, k, tn), lambda i, j: (j, 0, 0))],
        out_specs=pl.BlockSpec((tm, tn), lambda i, j: (i, j)),
        out_shape=jax.ShapeDtypeStruct((m, nt * tn), F32),
        compiler_params=_cparams("parallel", "parallel"),
        name="proj",
    )(x, w)


KV4_GROUPS = 4 * KV_HEADS
WIN_GROUPS = 2 * KV_HEADS


def _proj_in_kernel(x_ref, w_ref, z_ref, kv_ref, win_ref, *, tn):
    j = pl.program_id(1)
    acc = _dot(x_ref[...].astype(BF16), w_ref[...])
    z_ref[...] = acc
    tm = acc.shape[0]
    per_tile = tn // HEAD_DIM
    first_tile = C_NKV // tn
    for tile in range(first_tile, first_tile + N_KV_SLOTS * KV_HEADS // per_tile):
        @pl.when(j == tile)
        def _(tile=tile):
            for c in range(per_tile):
                grp = (tile - first_tile) * per_tile + c
                val = acc[:, c * HEAD_DIM:(c + 1) * HEAD_DIM]
                if grp < KV4_GROUPS:
                    kv_ref[pl.ds(grp, tm, stride=KV4_GROUPS), :] = val
                else:
                    win_ref[pl.ds(grp - KV4_GROUPS, tm, stride=WIN_GROUPS), :] = val


def _proj_in(x, w, *, tm):
    m, k = x.shape
    nt, _, tn = w.shape
    n = nt * tn
    assert C_NKV % tn == 0 and (N_KV_SLOTS * KV_HEADS * HEAD_DIM) % tn == 0
    return pl.pallas_call(
        functools.partial(_proj_in_kernel, tn=tn),
        grid=(m // tm, nt),
        in_specs=[pl.BlockSpec((tm, k), lambda i, j: (i, 0)), pl.BlockSpec((None, k, tn), lambda i, j: (j, 0, 0))],
        out_specs=[pl.BlockSpec((tm, tn), lambda i, j: (i, j)),
                   pl.BlockSpec((tm * KV4_GROUPS, HEAD_DIM), lambda i, j: (i, 0)),
                   pl.BlockSpec((tm * WIN_GROUPS, HEAD_DIM), lambda i, j: (i, 0))],
        out_shape=[jax.ShapeDtypeStruct((m, n), F32),
                   jax.ShapeDtypeStruct((m * KV4_GROUPS, HEAD_DIM), F32),
                   jax.ShapeDtypeStruct((m * WIN_GROUPS, HEAD_DIM), F32)],
        compiler_params=_cparams("parallel", "arbitrary"),
        name="proj_in",
    )(x, w)


def _wout_kernel(om_ref, on_ref, os_ref, x_ref, w_ref, lw_ref, lb_ref, o_ref):
    acc = _dot(om_ref[...].astype(BF16), w_ref[0:D_MLSTM, :])
    acc += _dot(on_ref[...].astype(BF16), w_ref[D_MLSTM:D_MLSTM + D_NSA, :])
    acc += _dot(os_ref[...].astype(BF16), w_ref[D_MLSTM + D_NSA:D_MODEL, :])
    o_ref[...] = _layer_norm(DEEPNORM_ALPHA * x_ref[...] + acc, lw_ref[...], lb_ref[...])


def _wout_ln(om, on, os_, x, w, lw, lb, *, tm):
    m = x.shape[0]
    tm = min(tm, m)
    row = lambda width: pl.BlockSpec((tm, width), lambda i: (i, 0))
    full = lambda a: pl.BlockSpec(a.shape, lambda i: (0, 0))
    return pl.pallas_call(
        _wout_kernel,
        grid=(m // tm,),
        in_specs=[row(D_MLSTM), row(D_NSA), row(D_S5), row(D_MODEL), full(w), full(lw), full(lb)],
        out_specs=row(D_MODEL),
        out_shape=jax.ShapeDtypeStruct((m, D_MODEL), F32),
        compiler_params=_cparams("parallel"),
        name="wout_ln",
    )(om, on, os_, x, w, lw, lb)


def _down_kernel(h_ref, w_ref, x_ref, lw_ref, lb_ref, o_ref, acc_ref, *, nk):
    k = pl.program_id(1)

    @pl.when(k == 0)
    def _():
        acc_ref[...] = jnp.zeros_like(acc_ref)

    acc_ref[...] += _dot(h_ref[...], w_ref[...])

    @pl.when(k == nk - 1)
    def _():
        o_ref[...] = _layer_norm(DEEPNORM_ALPHA * x_ref[...] + acc_ref[...], lw_ref[...], lb_ref[...])


def _down_ln(h, w, x, lw, lb, *, tm, tk):
    m, kk = h.shape
    tm = min(tm, m)
    nk = kk // tk
    return pl.pallas_call(
        functools.partial(_down_kernel, nk=nk),
        grid=(m // tm, nk),
        in_specs=[
            pl.BlockSpec((tm, tk), lambda i, k: (i, k)),
            pl.BlockSpec((tk, D_MODEL), lambda i, k: (k, 0)),
            pl.BlockSpec((tm, D_MODEL), lambda i, k: (i, 0)),
            pl.BlockSpec((1, D_MODEL), lambda i, k: (0, 0)),
            pl.BlockSpec((1, D_MODEL), lambda i, k: (0, 0)),
        ],
        out_specs=pl.BlockSpec((tm, D_MODEL), lambda i, k: (i, 0)),
        out_shape=jax.ShapeDtypeStruct((m, D_MODEL), F32),
        scratch_shapes=[pltpu.VMEM((tm, D_MODEL), F32)],
        compiler_params=_cparams("parallel", "arbitrary"),
        name="down_ln",
    )(h, w, x, lw, lb)


def _convgate_kernel(a_ref, g_ref, ha_ref, hg_ref, sa_ref, sg_ref, wa_ref, wg_ref, ba_ref, bg_ref, o_ref):
    first = pl.program_id(1) == 0
    tt = a_ref.shape[0]
    rows = lax.broadcasted_iota(jnp.int32, (tt, 1), 0)

    def conv(cur_ref, halo_ref, st_ref, w_ref, b_ref):
        cur = cur_ref[...]
        prev = jnp.where(first, st_ref[...], halo_ref[...])
        p1 = prev[7:8, :]
        p2 = prev[6:7, :]
        x1 = jnp.where(rows == 0, p1, pltpu.roll(cur, 1, 0))
        x2 = jnp.where(rows == 0, p2, jnp.where(rows == 1, p1, pltpu.roll(cur, 2, 0)))
        w = w_ref[...]
        return b_ref[...] + w[0:1, :] * x2 + w[1:2, :] * x1 + w[2:3, :] * cur

    a = conv(a_ref, ha_ref, sa_ref, wa_ref, ba_ref)
    g = conv(g_ref, hg_ref, sg_ref, wg_ref, bg_ref)
    o_ref[...] = (a * jax.nn.sigmoid(a) * g).astype(o_ref.dtype)


def _convgate(up, state8, conv_w8, conv_b, *, tt, tf):
    b, t, _ = up.shape
    tt = min(tt, t)
    nf = D_FFP // tf
    hb = tt // SUBLANES
    cur_a = pl.BlockSpec((None, tt, tf), lambda bi, ti, fi: (bi, ti, fi))
    cur_g = pl.BlockSpec((None, tt, tf), lambda bi, ti, fi: (bi, ti, fi + nf))
    halo_a = pl.BlockSpec((None, SUBLANES, tf), lambda bi, ti, fi: (bi, jnp.maximum(ti * hb - 1, 0), fi))
    halo_g = pl.BlockSpec((None, SUBLANES, tf), lambda bi, ti, fi: (bi, jnp.maximum(ti * hb - 1, 0), fi + nf))
    st_a = pl.BlockSpec((None, SUBLANES, tf), lambda bi, ti, fi: (bi, 0, fi))
    st_g = pl.BlockSpec((None, SUBLANES, tf), lambda bi, ti, fi: (bi, 0, fi + nf))
    w_a = pl.BlockSpec((SUBLANES, tf), lambda bi, ti, fi: (0, fi))
    w_g = pl.BlockSpec((SUBLANES, tf), lambda bi, ti, fi: (0, fi + nf))
    b_a = pl.BlockSpec((1, tf), lambda bi, ti, fi: (0, fi))
    b_g = pl.BlockSpec((1, tf), lambda bi, ti, fi: (0, fi + nf))
    return pl.pallas_call(
        _convgate_kernel,
        grid=(b, t // tt, nf),
        in_specs=[cur_a, cur_g, halo_a, halo_g, st_a, st_g, w_a, w_g, b_a, b_g],
        out_specs=pl.BlockSpec((None, tt, tf), lambda bi, ti, fi: (bi, ti, fi)),
        out_shape=jax.ShapeDtypeStruct((b, t, D_FFP), BF16),
        compiler_params=_cparams("parallel", "parallel", "parallel"),
        name="convgate",
    )(up, up, up, up, state8, state8, conv_w8, conv_w8, conv_b, conv_b)


def _ffn_kernel(x_ref, xh_ref, wa_ref, wg_ref, wd_ref, cwa_ref, cwg_ref, cba_ref, cbg_ref, lw_ref, lb_ref,
                o_ref, sa_ref, sg_ref, xe_ref, *, nf):
    first = pl.program_id(1) == 0
    f = pl.program_id(2)
    tt = x_ref.shape[0]

    @pl.when(f == 0)
    def _():
        o_ref[...] = jnp.zeros_like(o_ref)
        halo = jnp.where(first, 0.0, xh_ref[...])
        xe_ref[0:FFN_HALO, :] = halo.astype(BF16)
        xe_ref[FFN_HALO:, :] = x_ref[...].astype(BF16)

    xe = xe_ref[...]

    def branch(w_ref, cw_ref, cb_ref, s_ref):
        up = _dot(xe, w_ref[...])
        s_ref[...] = up[FFN_HALO + tt - SUBLANES:FFN_HALO + tt, :]
        w = cw_ref[...]
        x1 = pltpu.roll(up, 1, 0)[FFN_HALO:, :]
        x2 = pltpu.roll(up, 2, 0)[FFN_HALO:, :]
        return cb_ref[...] + w[0:1, :] * x2 + w[1:2, :] * x1 + w[2:3, :] * up[FFN_HALO:, :]

    a = branch(wa_ref, cwa_ref, cba_ref, sa_ref)
    g = branch(wg_ref, cwg_ref, cbg_ref, sg_ref)
    o_ref[...] += _dot((a * jax.nn.sigmoid(a) * g).astype(BF16), wd_ref[...])

    @pl.when(f == nf - 1)
    def _():
        o_ref[...] = _layer_norm(DEEPNORM_ALPHA * x_ref[...] + o_ref[...], lw_ref[...], lb_ref[...])


def _ffn_prompt(x3, w_up, w_down, conv_w8, conv_b, lw, lb, *, tt, tf):
    b, t, _ = x3.shape
    nf = D_FFP // tf
    tt = min(tt, t)
    hb = tt // FFN_HALO
    half_a = lambda rows: pl.BlockSpec((rows, tf), lambda bi, ti, fi: (0, fi))
    half_g = lambda rows: pl.BlockSpec((rows, tf), lambda bi, ti, fi: (0, fi + nf))
    w_a = pl.BlockSpec((None, D_MODEL, tf), lambda bi, ti, fi: (fi, 0, 0))
    w_g = pl.BlockSpec((None, D_MODEL, tf), lambda bi, ti, fi: (fi + nf, 0, 0))
    vec =pl.BlockSpec((1, D_MODEL), lambda bi, ti, fi: (0, 0))
    tail = pl.BlockSpec((None, SUBLANES, tf), lambda bi, ti, fi: (bi, 0, fi))
    return pl.pallas_call(
        functools.partial(_ffn_kernel, nf=nf),
        grid=(b, t // tt, nf),
        in_specs=[pl.BlockSpec((None, tt, D_MODEL), lambda bi, ti, fi: (bi, ti, 0), pipeline_mode=pl.Buffered(1)),
                  pl.BlockSpec((None, FFN_HALO, D_MODEL), lambda bi, ti, fi: (bi, jnp.maximum(ti * hb - 1, 0), 0)),
                  w_a, w_g,
                  pl.BlockSpec((tf, D_MODEL), lambda bi, ti, fi: (fi, 0)),
                  half_a(SUBLANES), half_g(SUBLANES), half_a(1), half_g(1), vec, vec],
        out_specs=[pl.BlockSpec((None, tt, D_MODEL), lambda bi, ti, fi: (bi, ti, 0)), tail, tail],
        out_shape=[jax.ShapeDtypeStruct((b, t, D_MODEL), F32),
                   jax.ShapeDtypeStruct((b, SUBLANES, D_FFP), F32),
                   jax.ShapeDtypeStruct((b, SUBLANES, D_FFP), F32)],
        scratch_shapes=[pltpu.VMEM((tt + FFN_HALO, D_MODEL), BF16)],
        compiler_params=_cparams("parallel", "arbitrary", "arbitrary"),
        name="ffn_prompt",
    )(x3, x3, w_up, w_up, w_down, conv_w8, conv_w8, conv_b, conv_b, lw, lb)


def _cumsum_rows(x):
    n = x.shape[0]
    rows = lax.broadcasted_iota(jnp.int32, (n, 1), 0)
    d = 1
    while d < n:
        x = x + jnp.where(rows >= d, pltpu.roll(x, d, 0), 0.0)
        d *= 2
    return x


def _log_sigmoid(x):
    return jnp.minimum(x, 0.0) - jnp.log1p(jnp.exp(-jnp.abs(x)))


def _mlstm_kernel(q_ref, k_ref, v_ref, og_ref, g_ref, gb_ref, nw_ref, c0_ref, n0_ref, m0_ref,
                  out_ref, c_ref, n_ref, m_ref, *, L, t_valid):
    ci = pl.program_id(1)
    lin = q_ref.shape[0]

    @pl.when(ci == 0)
    def _():
        c_ref[...] = c0_ref[...]
        n_ref[...] = n0_ref[...]
        m_ref[...] = m0_ref[...]

    def rows_of(ref):
        x = ref[...]
        if lin < L:
            x = jnp.concatenate([x, jnp.zeros((L - lin, x.shape[1]), x.dtype)], axis=0)
        return x

    rows = lax.broadcasted_iota(jnp.int32, (L, 1), 0)
    valid = (ci * L + rows) < t_valid
    pre = rows_of(g_ref) + gb_ref[...]
    lf = jnp.where(valid, _log_sigmoid(pre), 0.0)
    ig = jnp.where(valid, pre, NEG_INF)
    bcum = _cumsum_rows(lf)
    dt = (pltpu.roll(ig, GATE_MF - GATE_MI, 1) - bcum).T
    q_all, k_all, v_all, og_all = rows_of(q_ref), rows_of(k_ref), rows_of(v_ref), rows_of(og_ref)
    tri = lax.broadcasted_iota(jnp.int32, (L, L), 0) >= lax.broadcasted_iota(jnp.int32, (L, L), 1)
    nw = nw_ref[...]

    for h in range(MLSTM_HEADS):
        sl = slice(h * HEAD_DIM, (h + 1) * HEAD_DIM)
        q = q_all[:, sl]
        k = k_all[:, sl] * QK_SCALE
        v = v_all[:, sl]
        qb, kb, vb = q.astype(BF16), k.astype(BF16), v.astype(BF16)
        b_col = bcum[:, GATE_MF + h:GATE_MF + h + 1]
        ig_col = ig[:, GATE_MI + h:GATE_MI + h + 1]
        d_row = dt[GATE_MF + h:GATE_MF + h + 1, :]
        c_prev = c_ref[h]
        n_prev = n_ref[h:h + 1, :]
        m_prev = m_ref[h:h + 1, 0:1]

        dmat = jnp.where(tri, b_col + d_row, NEG_INF)
        g_col = b_col + m_prev
        m_row = jnp.maximum(jnp.max(dmat, axis=1, keepdims=True), g_col)
        a = jnp.exp(dmat - m_row) * _dot_nt(qb, kb)
        w_inter = jnp.exp(g_col - m_row)
        num = _dot(a.astype(BF16), vb) + w_inter * _dot(qb, c_prev.astype(BF16))
        den = jnp.sum(a, axis=1, keepdims=True) + w_inter * jnp.sum(q * n_prev, axis=1, keepdims=True)
        hid = num / jnp.maximum(jnp.abs(den), jnp.exp(-m_row))
        mu = jnp.mean(hid, axis=1, keepdims=True)
        dlt = hid - mu
        var = jnp.mean(dlt * dlt, axis=1, keepdims=True)
        hn = dlt * lax.rsqrt(var + LN_EPS) * nw[:, sl]
        res = jax.nn.sigmoid(og_all[:, sl]) * hn
        out_ref[:, sl] = res[0:lin, :]

        f_tot = b_col[L - 1:L, :]
        w_s = f_tot - b_col + ig_col
        m_new = jnp.maximum(f_tot + m_prev, jnp.max(w_s, axis=0, keepdims=True))
        ws = jnp.exp(w_s - m_new)
        decay = jnp.exp(f_tot + m_prev - m_new)
        c_ref[h] = decay * c_prev + _dot(k.T.astype(BF16), (ws * v).astype(BF16))
        n_ref[h:h + 1, :] = decay * n_prev + jnp.sum(ws * k, axis=0, keepdims=True)
        m_ref[h:h + 1, :] = jnp.broadcast_to(m_new, (1, LANES))


def _mlstm(z, gate_b, norm_w, c0, n0, m0, *, L, lin, t_valid):
    b, tz, _ = z.shape
    nchunks = tz // lin
    qblk = lambda col: pl.BlockSpec((None, lin, D_MLSTM), lambda bi, ci: (bi, ci, col // D_MLSTM))
    st4 = pl.BlockSpec((None, MLSTM_HEADS, HEAD_DIM, HEAD_DIM), lambda bi, ci: (bi, 0, 0, 0))
    st3 = pl.BlockSpec((None, MLSTM_HEADS, LANES), lambda bi, ci: (bi, 0, 0))
    return pl.pallas_call(
        functools.partial(_mlstm_kernel, L=L, t_valid=t_valid),
        grid=(b, nchunks),
        in_specs=[qblk(C_MQ), qblk(C_MK), qblk(C_MV), qblk(C_MO),
                  pl.BlockSpec((None, lin, LANES), lambda bi, ci: (bi, ci, C_GATE // LANES)),
                  pl.BlockSpec((1, LANES), lambda bi, ci: (0, 0)),
                  pl.BlockSpec((1, D_MLSTM), lambda bi, ci: (0, 0)),
                  st4, st3, st3],
        out_specs=[pl.BlockSpec((None, lin, D_MLSTM), lambda bi, ci: (bi, ci, 0)), st4, st3, st3],
        out_shape=[jax.ShapeDtypeStruct((b, tz, D_MLSTM), F32),
                   jax.ShapeDtypeStruct((b, MLSTM_HEADS, HEAD_DIM, HEAD_DIM), F32),
                   jax.ShapeDtypeStruct((b, MLSTM_HEADS, LANES), F32),
                   jax.ShapeDtypeStruct((b, MLSTM_HEADS, LANES), F32)],
        compiler_params=_cparams("parallel", "arbitrary"),
        name="mlstm",
    )(z, z, z, z, z, gate_b, norm_w, c0, n0, m0)


S5_IN_TILES = S5_CH // 256
S5_IN_FEATS = D_S5 // S5_IN_TILES
S5_OUT_TILES = D_S5 // 256
S5_OUT_CH = S5_CH // S5_OUT_TILES


def _s5_kernel(u_ref, wb_ref, wc_ref, lr_ref, li_ref, d_ref, gw_ref, gb_ref, x0r_ref, x0i_ref,
               o_ref, xr_ref, xi_ref, sr_ref, si_ref, *, lc, rb):
    @pl.when(pl.program_id(0) == 0)
    def _():
        xr_ref[...] = x0r_ref[...]
        xi_ref[...] = x0i_ref[...]

    u = u_ref[...]
    ub = u.astype(BF16)
    for c in range(S5_IN_TILES):
        bu = _dot(ub[:, c * S5_IN_FEATS:(c + 1) * S5_IN_FEATS], wb_ref[c])
        sr_ref[:, c * 256:(c + 1) * 256] = bu[:, 0:256]
        si_ref[:, c * 256:(c + 1) * 256] = bu[:, 256:512]
    lam_r = jnp.broadcast_to(lr_ref[...], (SUBLANES, S5_CH))
    lam_i = jnp.broadcast_to(li_ref[...], (SUBLANES, S5_CH))
    row8 = lax.broadcasted_iota(jnp.int32, (SUBLANES, 1), 0)
    per_group = SUBLANES // rb

    def group(i, carry):
        xr, xi = carry
        r0 = pl.multiple_of(i * SUBLANES, SUBLANES)
        br = sr_ref[pl.ds(r0, SUBLANES), :]
        bi = si_ref[pl.ds(r0, SUBLANES), :]
        out_r, out_i = br, bi
        for k in range(per_group):
            nr = lam_r * xr - lam_i * xi + br
            ni = lam_r * xi + lam_i * xr + bi
            here = (row8 >= k * rb) & (row8 < (k + 1) * rb)
            out_r = jnp.where(here, nr, out_r)
            out_i = jnp.where(here, ni, out_i)
            if per_group > 1:
                xr = pltpu.roll(nr, rb, 0)
                xi = pltpu.roll(ni, rb, 0)
            else:
                xr, xi = nr, ni
        sr_ref[pl.ds(r0, SUBLANES), :] = out_r
        si_ref[pl.ds(r0, SUBLANES), :] = out_i
        return xr, xi

    xr, xi = lax.fori_loop(0, lc * rb // SUBLANES, group, (xr_ref[...], xi_ref[...]))
    xr_ref[...] = xr
    xi_ref[...] = xi
    for j in range(S5_OUT_TILES):
        ch = slice(j * S5_OUT_CH, (j + 1) * S5_OUT_CH)
        cols = slice(j * 256, (j + 1) * 256)
        y = _dot(sr_ref[:, ch].astype(BF16), wc_ref[0, j]) + _dot(si_ref[:, ch].astype(BF16), wc_ref[1, j])
        o_ref[:, cols] = y + d_ref[:, cols] * u[:, cols]
    zz = jax.nn.gelu(o_ref[...])
    o_ref[...] = zz * jax.nn.sigmoid(_dot(zz.astype(BF16), gw_ref[...]) + gb_ref[...])


def _s5(u_tm, wb, wc, lam_r, lam_i, d, glu_w, glu_b, x0r, x0i, *, lc, rb):
    rows = u_tm.shape[0]
    t = rows // rb
    lc = min(lc, t)
    full = lambda a: pl.BlockSpec(a.shape, lambda i: (0,) * a.ndim)
    st = pl.BlockSpec((SUBLANES, S5_CH), lambda i: (0, 0))
    return pl.pallas_call(
        functools.partial(_s5_kernel, lc=lc, rb=rb),
        grid=(t // lc,),
        in_specs=[pl.BlockSpec((lc * rb, D_S5), lambda i: (i, 0)),
                  full(wb), full(wc), full(lam_r), full(lam_i), full(d), full(glu_w), full(glu_b), st, st],
        out_specs=[pl.BlockSpec((lc * rb, D_S5), lambda i: (i, 0)), st, st],
        out_shape=[jax.ShapeDtypeStruct((rows, D_S5), F32),
                   jax.ShapeDtypeStruct((SUBLANES, S5_CH), F32),
                   jax.ShapeDtypeStruct((SUBLANES, S5_CH), F32)],
        scratch_shapes=[pltpu.VMEM((lc * rb, S5_CH), F32), pltpu.VMEM((lc * rb, S5_CH), F32)],
        compiler_params=_cparams("arbitrary"),
        name="s5",
    )(u_tm, wb, wc, lam_r, lam_i, d, glu_w, glu_b, x0r, x0i)


N_CMB = 2 * KV_HEADS


def _cmp_project(rows_of, w_ref, o_ref):
    nch = o_ref.shape[0]
    for cmb in range(N_CMB):
        slot = cmb // KV_HEADS
        rows_ref = rows_of(cmb)
        acc = jnp.zeros((nch, 2 * HEAD_DIM), F32)
        for l in range(CMP_STRIDE):
            x = rows_ref[pl.ds(l, nch, stride=CMP_STRIDE), :]
            acc += _dot(x.astype(BF16), w_ref[slot, l])
        o_ref[:, cmb * 2 * HEAD_DIM:(cmb + 1) * 2 * HEAD_DIM] = acc


def _cmp_p_kernel(r0_ref, r1_ref, r2_ref, r3_ref, w_ref, o_ref):
    rows = (r0_ref, r1_ref, r2_ref, r3_ref)
    _cmp_project(lambda cmb: rows[cmb], w_ref, o_ref)


def _cmp_project_prompt(z, wcat):
    b, t, _ = z.shape
    nch = t // CMP_STRIDE
    rows = lambda cmb: pl.BlockSpec((None, t, HEAD_DIM), lambda bi: (bi, 0, C_NKV // HEAD_DIM + cmb))
    return pl.pallas_call(
        _cmp_p_kernel,
        grid=(b,),
        in_specs=[rows(cmb) for cmb in range(N_CMB)] + [pl.BlockSpec(wcat.shape, lambda bi: (0, 0, 0, 0))],
        out_specs=pl.BlockSpec((None, nch, N_CMB * 2 * HEAD_DIM), lambda bi: (bi, 0, 0)),
        out_shape=jax.ShapeDtypeStruct((b, nch, N_CMB * 2 * HEAD_DIM), F32),
        compiler_params=_cparams("parallel"),
        name="cmp_project_prompt",
    )(z, z, z, z, wcat)


def _cmp_s_kernel(pt_ref, *refs):
    del pt_ref
    page_refs = [r.reshape(PAGE_SIZE * KV_HEADS, HEAD_DIM) for r in refs[:2 * CMP_PAGES]]
    w_ref, o_ref = refs[2 * CMP_PAGES:]
    per_page = PAGE_SIZE // CMP_STRIDE
    nch = o_ref.shape[0]

    def rows_of(slot, l, g):
        return jnp.concatenate(
            [page_refs[slot * CMP_PAGES + p][pl.ds(l * KV_HEADS + g, per_page, stride=CMP_STRIDE * KV_HEADS), :]
             for p in range(CMP_PAGES)], axis=0)

    for slot in range(2):
        acc = jnp.zeros((KV_HEADS * nch, 2 * HEAD_DIM), F32)
        for lp in range(CMP_STRIDE // 2):
            x = jnp.concatenate(
                [jnp.concatenate([rows_of(slot, 2 * lp + d, g) for g in range(KV_HEADS)], axis=0) for d in range(2)],
                axis=1)
            acc += _dot(x.astype(BF16), w_ref[slot, lp])
        for g in range(KV_HEADS):
            cmb = slot * KV_HEADS + g
            o_ref[:, cmb * 2 * HEAD_DIM:(cmb + 1) * 2 * HEAD_DIM] = acc[g * nch:(g + 1) * nch, :]


def _page_spec(layer, slot, p, pages_per_step, grid_rank):
    def index_map(*idx):
        bi, si, pt = idx[0], idx[grid_rank - 1], idx[grid_rank]
        return (layer, pt[bi, si * pages_per_step + p], 0, slot, 0, 0)

    return pl.BlockSpec((None, None, PAGE_SIZE, None, KV_HEADS, HEAD_DIM), index_map)


def _cmp_project_sample(cache6, page_table, layer, wcat):
    b, npages = page_table.shape
    steps = npages // CMP_PAGES
    nch = CMP_PAGES * PAGE_SIZE // CMP_STRIDE
    wcat = wcat.reshape(2, CMP_STRIDE // 2, 2 * HEAD_DIM, 2 * HEAD_DIM)
    pages = [_page_spec(layer, slot, p, CMP_PAGES, 2) for slot in range(2) for p in range(CMP_PAGES)]

    grid_spec = pltpu.PrefetchScalarGridSpec(
        num_scalar_prefetch=1,
        grid=(b, steps),
        in_specs=pages + [pl.BlockSpec(wcat.shape, lambda bi, si, pt: (0, 0, 0, 0))],
        out_specs=pl.BlockSpec((None, nch, N_CMB * 2 * HEAD_DIM), lambda bi, si, pt: (bi, si, 0)),
    )
    return pl.pallas_call(
        _cmp_s_kernel,
        grid_spec=grid_spec,
        out_shape=jax.ShapeDtypeStruct((b, steps * nch, N_CMB * 2 * HEAD_DIM), F32),
        compiler_params=_cparams("parallel", "arbitrary"),
        name="cmp_project_sample",
    )(page_table, *([cache6] * (2 * CMP_PAGES)), wcat)


def _cmp_fin_kernel(p_ref, b1_ref, w2_ref, o_ref):
    nch = p_ref.shape[0]
    for cmb in range(N_CMB):
        slot = cmb // KV_HEADS
        c0 = cmb * 2 * HEAD_DIM
        first = p_ref[:, c0:c0 + HEAD_DIM]
        second = pltpu.roll(p_ref[:, c0 + HEAD_DIM:c0 + 2 * HEAD_DIM], nch - 1, 0)
        hid = b1_ref[slot:slot + 1, :] + first + second
        o_ref[cmb] = _dot(jax.nn.gelu(hid).astype(BF16), w2_ref[slot])


def _cmp_finish(p, b1, w2):
    b, nch, _ = p.shape
    return pl.pallas_call(
        _cmp_fin_kernel,
        grid=(b,),
        in_specs=[pl.BlockSpec((None, nch, p.shape[2]), lambda bi: (bi, 0, 0)),
                  pl.BlockSpec(b1.shape, lambda bi: (0, 0)),
                  pl.BlockSpec(w2.shape, lambda bi: (0, 0, 0))],
        out_specs=pl.BlockSpec((None, N_CMB, nch, HEAD_DIM), lambda bi: (bi, 0, 0, 0)),
        out_shape=jax.ShapeDtypeStruct((b, N_CMB, nch, HEAD_DIM), F32),
        compiler_params=_cparams("parallel"),
        name="cmp_finish",
    )(p, b1, w2)


def _softmax_init(m_scr, l_scr, acc_scr):
    m_scr[...] = jnp.full(m_scr.shape, M_INIT, F32)
    l_scr[...] = jnp.zeros(l_scr.shape, F32)
    acc_scr[...] = jnp.zeros(acc_scr.shape, F32)


def _softmax_update(s, mask, vb, m_scr, l_scr, acc_scr):
    s = jnp.where(mask, s, NEG_INF)
    m_prev = m_scr[...]
    m_new = jnp.maximum(m_prev, jnp.max(s, axis=1, keepdims=True))
    alpha = jnp.exp(m_prev - m_new)
    p = jnp.exp(s - m_new)
    l_scr[...] = alpha * l_scr[...] + jnp.sum(p, axis=1, keepdims=True)
    acc_scr[...] = alpha * acc_scr[...] + _dot(p.astype(BF16), vb)
    m_scr[...] = m_new


def _softmax_result(l_scr, acc_scr):
    l = l_scr[...]
    return acc_scr[...] / jnp.where(l > 0, l, 1.0)


def _masked_probs(s, mask):
    s = jnp.where(mask, s, NEG_INF)
    m = jnp.max(s, axis=1, keepdims=True)
    m = jnp.where(m > NEG_INF, m, 0.0)
    p = jnp.exp(s - m)
    den = jnp.sum(p, axis=1, keepdims=True)
    return p / jnp.where(den > 0, den, 1.0)


def _stack_heads(q):
    return jnp.concatenate([q[:, j * HEAD_DIM:(j + 1) * HEAD_DIM] for j in range(GROUP)], axis=0)


def _block_importance(pc, n_cmp, n_sel):
    ncp = pc.shape[1]
    nsp = -(-n_sel // LANES) * LANES
    c_start = lax.broadcasted_iota(jnp.int32, (ncp, nsp), 0) * CMP_STRIDE
    s_start = lax.broadcasted_iota(jnp.int32, (ncp, nsp), 1) * SEL_BLOCK
    overlap = ((c_start < s_start + SEL_BLOCK) & (c_start + CMP_BLOCK > s_start)
               & (c_start < n_cmp * CMP_STRIDE)).astype(F32)
    return jnp.dot(pc, overlap, preferred_element_type=F32, precision=lax.Precision.HIGHEST)


def _force_blocks(imp, blk, cur):
    forced = (blk == 0) | (blk == cur) | (blk == cur - 1)
    imp = jnp.where(forced, FORCE_SCORE, imp)
    return jnp.where(blk > cur, -1.0, imp)


def _select_blocks_cols(pc, cur, n_cmp, n_sel):
    imp = _block_importance(pc, n_cmp, n_sel)
    blk = lax.broadcasted_iota(jnp.int32, (1, imp.shape[1]), 1)
    imp = _force_blocks(imp, blk, cur)

    def body(sp, count):
        col = jnp.sum(jnp.where(blk == sp, imp, 0.0), axis=1, keepdims=True)
        ahead = (col > imp) | ((col == imp) & (sp < blk))
        return count + ahead.astype(F32)

    count = lax.fori_loop(0, n_sel, body, jnp.zeros(imp.shape, F32), unroll=8)
    return ((count < min(SEL_TOPK, n_sel)) & (blk < n_sel)).astype(F32)


def _select_blocks_rows(pc, cur_row, n_cmp, n_sel):
    rows = pc.shape[0]
    nrow = -(-n_sel // SUBLANES) * SUBLANES
    imp_t = _block_importance(pc, n_cmp, n_sel).T[0:nrow, :]
    blk = lax.broadcasted_iota(jnp.int32, (nrow, 1), 0)
    imp_t = _force_blocks(imp_t, blk, cur_row)
    count = jnp.zeros(imp_t.shape, F32)
    for sp in range(n_sel):
        row = imp_t[sp:sp + 1, :]
        count += ((row > imp_t) | ((row == imp_t) & (sp < blk))).astype(F32)
    sel_t = ((count < min(SEL_TOPK, n_sel)) & (blk < n_sel)).astype(F32)
    sel_t = jnp.concatenate([sel_t, jnp.zeros((LANES - nrow, rows), F32)], axis=0)
    return sel_t.T


def _gate(gt, col):
    lane = lax.broadcasted_iota(jnp.int32, (1, LANES), 1)
    return jax.nn.sigmoid(jnp.sum(jnp.where(lane == col, gt, 0.0), axis=1, keepdims=True))


def _nsa_prompt_kernel(q_ref, ks_ref, vs_ref, kw_ref, vw_ref, kc_ref, vc_ref, bc_ref, tz_ref, gt_ref, gb_ref,
                       o_ref, s_scr, selb_scr, mx_scr, l_scr, acc_scr, *, t_len):
    g = pl.program_id(1)
    qt = pl.program_id(2)
    q0 = qt * TQ
    n_cmp = t_len // CMP_STRIDE - (CMP_BLOCK // CMP_STRIDE) + 1
    n_sel = t_len // SEL_BLOCK
    ncp = kc_ref.shape[0]
    rows4 = GROUP * TQ
    qs = _stack_heads(q_ref[...] * QK_SCALE).astype(BF16)
    qpos = q0 + (lax.broadcasted_iota(jnp.int32, (rows4, 1), 0) & (TQ - 1))

    n_idx = lax.broadcasted_iota(jnp.int32, (1, ncp), 1)
    s_c = _dot_nt(qs, kc_ref[...].astype(BF16)) + bc_ref[...].reshape(rows4, ncp)
    cmask = (qpos - (n_idx * CMP_STRIDE + CMP_BLOCK - 1) >= 0) & (n_idx < n_cmp)
    p_c = _masked_probs(s_c, cmask)
    o_c = _dot(p_c.astype(BF16), vc_ref[...].astype(BF16))
    pc = p_c[0:TQ] + p_c[TQ:2 * TQ] + p_c[2 * TQ:3 * TQ] + p_c[3 * TQ:4 * TQ]
    cur_row = jnp.right_shift(q0 + lax.broadcasted_iota(jnp.int32, (1, TQ), 1), SEL_SHIFT)
    sel = _select_blocks_rows(pc, cur_row, n_cmp, n_sel)
    sel_add = ((sel - 1.0) * -MASKED).astype(BF16)

    kt_hi = (q0 + TQ - 1) // TK + 1
    kt_far = jnp.maximum(q0 - (FAR_DIST - 1), 0) // TK
    kidx = lax.broadcasted_iota(jnp.int32, (1, TK), 1)
    srow = lax.broadcasted_iota(jnp.int32, (LANES, TK), 0)

    def spread_selection(kt, carry):
        k0 = kt * TK
        expand = (jnp.right_shift(k0 + lax.broadcasted_iota(jnp.int32, (LANES, TK), 1), SEL_SHIFT) == srow).astype(BF16)
        selb_scr[kt] = _dot(sel_add, expand)
        return carry

    lax.fori_loop(0, kt_hi, spread_selection, 0)

    def tile_bias(k0):
        return tz_ref[jnp.clip((q0 - k0) // TQ, 0, N_TZ - 1)].reshape(rows4, TK)

    def attend(k_ref, v_ref, kt_lo, mask_tile):
        mx_scr[...] = jnp.full(mx_scr.shape, NEG_INF, F32)

        def scores(near):
            def body(kt, carry):
                k0 = pl.multiple_of(kt * TK, TK)
                s = _dot_nt(qs, k_ref[pl.ds(k0, TK), :].astype(BF16))
                if near:
                    s = s + tile_bias(k0)
                s = mask_tile(k0, s, near)
                s_scr[kt] = s
                mx_scr[...] = jnp.maximum(mx_scr[...], jnp.maximum(s[:, 0:LANES], s[:, LANES:TK]))
                return carry
            return body

        kt_mid = jnp.maximum(kt_far, kt_lo)
        lax.fori_loop(kt_lo, kt_mid, scores(False), 0)
        lax.fori_loop(kt_mid, kt_hi, scores(True), 0)
        m = jnp.max(mx_scr[...], axis=1, keepdims=True)
        mx_scr[...] = jnp.broadcast_to(jnp.where(m > NEG_INF, m, 0.0), mx_scr.shape)
        l_scr[...] = jnp.zeros(l_scr.shape, F32)
        acc_scr[...] = jnp.zeros(acc_scr.shape, F32)

        def probs(kt, carry):
            k0 = pl.multiple_of(kt * TK, TK)
            m_rep = mx_scr[...]
            p = jnp.exp(s_scr[kt] - jnp.concatenate([m_rep, m_rep], axis=1))
            l_scr[...] += p[:, 0:LANES] + p[:, LANES:TK]
            acc_scr[...] += _dot(p.astype(BF16), v_ref[pl.ds(k0, TK), :].astype(BF16))
            return carry

        lax.fori_loop(kt_lo, kt_hi, probs, 0)
        l = jnp.sum(l_scr[...], axis=1, keepdims=True)
        return acc_scr[...] / jnp.where(l > 0, l, 1.0)

    def sel_mask(k0, s, near):
        s = (s.reshape(GROUP, TQ, TK) + selb_scr[k0 // TK][None]).reshape(rows4, TK)
        return jnp.where(k0 + kidx <= qpos, s, NEG_INF) if near else s

    o_s = attend(ks_ref, vs_ref, 0, sel_mask)

    def win_mask(k0, s, near):
        dist = qpos - (k0 + kidx)
        return jnp.where((dist >= 0) & (dist <= WINDOW), s, NEG_INF)

    o_w = attend(kw_ref, vw_ref, jnp.maximum(q0 - WINDOW, 0) // TK, win_mask)

    gt = gt_ref[...] + gb_ref[...]
    for j in range(GROUP):
        col = GATE_NG + (g * GROUP + j) * 3
        r = slice(j * TQ, (j + 1) * TQ)
        o_ref[:, j * HEAD_DIM:(j + 1) * HEAD_DIM] = (
            _gate(gt, col) * o_c[r] + _gate(gt, col + 1) * o_s[r] + _gate(gt, col + 2) * o_w[r])


def _nsa_prompt(z, kcvc, bias_c, tz, gate_b):
    b, t, _ = z.shape
    ncp = kcvc.shape[2]
    nq = t // TQ
    kv = lambda slot: pl.BlockSpec((None, t, HEAD_DIM), lambda bi, g, qi: (bi, 0, C_NKV // HEAD_DIM + slot * KV_HEADS + g))
    cmp_blk = lambda slot: pl.BlockSpec((None, None, ncp, HEAD_DIM), lambda bi, g, qi: (bi, slot * KV_HEADS + g, 0, 0))
    return pl.pallas_call(
        functools.partial(_nsa_prompt_kernel, t_len=t),
        grid=(b, KV_HEADS, nq),
        in_specs=[pl.BlockSpec((None, TQ, GROUP * HEAD_DIM), lambda bi, g, qi: (bi, qi, C_NQ // (GROUP * HEAD_DIM) + g)),
                  kv(2), kv(3), kv(4), kv(5), cmp_blk(0), cmp_blk(1),
                  pl.BlockSpec((GROUP, TQ, ncp), lambda bi, g, qi: (g, qi, 0)),
                  pl.BlockSpec((N_TZ, GROUP, TQ, TK), lambda bi, g, qi: (0, g, 0, 0)),
                  pl.BlockSpec((None, TQ, LANES), lambda bi, g, qi: (bi, qi, C_GATE // LANES)),
                  pl.BlockSpec((1, LANES), lambda bi, g, qi: (0, 0))],
        out_specs=pl.BlockSpec((None, TQ, GROUP * HEAD_DIM), lambda bi, g, qi: (bi, qi, g)),
        out_shape=jax.ShapeDtypeStruct((b, t, D_NSA), F32),
        scratch_shapes=[pltpu.VMEM((t // TK, GROUP * TQ, TK), F32), pltpu.VMEM((t // TK, TQ, TK), F32),
                        pltpu.VMEM((GROUP * TQ, LANES), F32),
                        pltpu.VMEM((GROUP * TQ, LANES), F32), pltpu.VMEM((GROUP * TQ, HEAD_DIM), F32)],
        compiler_params=_cparams("parallel", "parallel", "arbitrary"),
        name="nsa_prompt",
    )(z, z, z, z, z, kcvc, kcvc, bias_c, tz, z, gate_b)


TS = SUBLANES
ROWS_S = GROUP * TS


def _nsa_s_cmp_kernel(q_ref, kc_ref, vc_ref, bc_ref, oc_ref, sel_ref, *, pos0, n_cmp, n_sel):
    nch = kc_ref.shape[0]
    qs = _stack_heads(q_ref[...] * QK_SCALE).astype(BF16)
    qpos = pos0 + (lax.broadcasted_iota(jnp.int32, (ROWS_S, 1), 0) & (TS - 1))
    n_idx = lax.broadcasted_iota(jnp.int32, (1, nch), 1)
    s_c = _dot_nt(qs, kc_ref[...].astype(BF16)) + bc_ref[...].reshape(ROWS_S, nch)
    cmask = (qpos - (n_idx * CMP_STRIDE + CMP_BLOCK - 1) >= 0) & (n_idx < n_cmp)
    p_c = _masked_probs(s_c, cmask)
    oc_ref[...] = _dot(p_c.astype(BF16), vc_ref[...].astype(BF16))
    pc = p_c[0:TS] + p_c[TS:2 * TS] + p_c[2 * TS:3 * TS] + p_c[3 * TS:4 * TS]
    sel_ref[...] = _select_blocks_cols(pc, jnp.right_shift(qpos[0:TS], SEL_SHIFT), n_cmp, n_sel)


def _nsa_sample_cmp(z8, kcvc, bias_c, *, pos0, n_cmp, n_sel):
    b = z8.shape[0]
    nch = kcvc.shape[2]
    nsp = -(-n_sel // LANES) * LANES
    cmp_blk = lambda slot: pl.BlockSpec((None, None, nch, HEAD_DIM), lambda bi, g: (bi, slot * KV_HEADS + g, 0, 0))
    return pl.pallas_call(
        functools.partial(_nsa_s_cmp_kernel, pos0=pos0, n_cmp=n_cmp, n_sel=n_sel),
        grid=(b, KV_HEADS),
        in_specs=[pl.BlockSpec((None, TS, GROUP * HEAD_DIM), lambda bi, g: (bi, 0, C_NQ // (GROUP * HEAD_DIM) + g)),
                  cmp_blk(0), cmp_blk(1),
                  pl.BlockSpec((GROUP, TS, nch), lambda bi, g: (g, 0, 0))],
        out_specs=[pl.BlockSpec((None, None, ROWS_S, HEAD_DIM), lambda bi, g: (bi, g, 0, 0)),
                   pl.BlockSpec((None, None, TS, nsp), lambda bi, g: (bi, g, 0, 0))],
        out_shape=[jax.ShapeDtypeStruct((b, KV_HEADS, ROWS_S, HEAD_DIM), F32),
                   jax.ShapeDtypeStruct((b, KV_HEADS, TS, nsp), F32)],
        compiler_params=_cparams("parallel", "parallel"),
        name="nsa_sample_cmp",
    )(z8, kcvc, kcvc, bias_c)


def _pad_rows(x, n):
    return jnp.concatenate([x, jnp.zeros((n - x.shape[0], x.shape[1]), x.dtype)], axis=0)


def _nsa_s_sel_kernel(pt_ref, q_ref, sel_ref, kn_ref, vn_ref, bt_ref, bn_ref, bf_ref, *refs, pos0, npages):
    del pt_ref
    rows_pp = PAGE_SIZE * KV_HEADS
    k_refs = [r.reshape(rows_pp, HEAD_DIM) for r in refs[:SEL_PAGES]]
    v_refs = [r.reshape(rows_pp, HEAD_DIM) for r in refs[SEL_PAGES:2 * SEL_PAGES]]
    o_ref, kbuf, vbuf, m_scr, l_scr, acc_scr = refs[2 * SEL_PAGES:]
    step = pl.program_id(1)
    last = step == npages // SEL_PAGES - 1
    rows = KV_HEADS * ROWS_S
    ncol = SEL_TILE * rows_pp
    head_shift = KV_HEADS.bit_length() - 1
    blk_per_page = PAGE_SIZE // SEL_BLOCK

    @pl.when(step == 0)
    def _():
        _softmax_init(m_scr, l_scr, acc_scr)

    nsp = sel_ref.shape[-1]
    ridx = lax.broadcasted_iota(jnp.int32, (rows, 1), 0)
    qpos = pos0 + (ridx & (TS - 1))
    qs = jnp.concatenate([_stack_heads(q_ref[:, g * GROUP * HEAD_DIM:(g + 1) * GROUP * HEAD_DIM] * QK_SCALE)
                          for g in range(KV_HEADS)], axis=0).astype(BF16)
    sel_rows = jnp.concatenate([sel_ref[g] for g in range(KV_HEADS) for _ in range(GROUP)], axis=0)
    bias_far = bf_ref[...].reshape(rows, PAGE_SIZE)[:, 0:1]
    col = lax.broadcasted_iota(jnp.int32, (1, ncol), 1)
    own_head = (col & (KV_HEADS - 1)) == ridx // ROWS_S
    expand = (jnp.right_shift(lax.broadcasted_iota(jnp.int32, (LANES, ncol), 1), SEL_SHIFT + head_shift)
              == lax.broadcasted_iota(jnp.int32, (LANES, ncol), 0)).astype(BF16)
    blk_r = lax.broadcasted_iota(jnp.int32, (nsp, LANES), 0)
    blk_c = lax.broadcasted_iota(jnp.int32, (nsp, LANES), 1)
    n_tiles = SEL_PAGES // SEL_TILE

    for tile in range(n_tiles):
        for i in range(SEL_TILE):
            p = tile * SEL_TILE + i
            kbuf[i * rows_pp:(i + 1) * rows_pp, :] = k_refs[p][...].astype(BF16)
            vbuf[i * rows_pp:(i + 1) * rows_pp, :] = v_refs[p][...].astype(BF16)
        page0 = step * SEL_PAGES + tile * SEL_TILE
        window = ((blk_r == page0 * blk_per_page + blk_c) & (blk_c < SEL_TILE * blk_per_page)).astype(BF16)
        chosen = _dot(_dot(sel_rows.astype(BF16), window).astype(BF16), expand) > 0.5
        s = _dot_nt(qs, kbuf[...]) + bias_far
        if tile == n_tiles - 1:
            tail = jnp.where(last, bt_ref[...].reshape(rows, rows_pp) - bias_far, 0.0)
            s = s + jnp.concatenate([jnp.zeros((rows, ncol - rows_pp), F32), tail], axis=1)
        key = page0 * PAGE_SIZE + jnp.right_shift(col, head_shift)
        _softmax_update(s, chosen & own_head & (key <= qpos), vbuf[...], m_scr, l_scr, acc_scr)

    @pl.when(last)
    def _():
        kidx = lax.broadcasted_iota(jnp.int32, (1, PAGE_SIZE), 1)
        new_blk = lax.broadcasted_iota(jnp.int32, (1, nsp), 1) == npages * blk_per_page
        flag = jnp.sum(jnp.where(new_blk, sel_rows, 0.0), axis=1, keepdims=True) > 0.5
        bias_new = bn_ref[...].reshape(rows, PAGE_SIZE)
        for g in range(KV_HEADS):
            r = slice(g * ROWS_S, (g + 1) * ROWS_S)
            kn = _pad_rows(kn_ref[:, g * HEAD_DIM:(g + 1) * HEAD_DIM], PAGE_SIZE).astype(BF16)
            vn = _pad_rows(vn_ref[:, g * HEAD_DIM:(g + 1) * HEAD_DIM], PAGE_SIZE).astype(BF16)
            mask = flag[r] & (pos0 + kidx <= qpos[r]) & (kidx < TS)
            _softmax_update(_dot_nt(qs[r], kn) + bias_new[r], mask, vn, m_scr.at[r], l_scr.at[r], acc_scr.at[r])
            o_ref[g] = _softmax_result(l_scr.at[r], acc_scr.at[r])


def _nsa_sample_sel(z8, sel, bias_tail, bias_new, bias_far, cache6, page_table, layer, *, pos0):
    b, npages = page_table.shape
    nsp = sel.shape[-1]
    steps = npages // SEL_PAGES
    slot_cols = KV_HEADS * HEAD_DIM
    new_rows = lambda slot: pl.BlockSpec((None, TS, slot_cols),
                                         lambda bi, si, pt: (bi, 0, (C_NKV + slot * slot_cols) // slot_cols))
    whole = lambda a: pl.BlockSpec(a.shape, lambda bi, si, pt: (0,) * a.ndim)
    grid_spec = pltpu.PrefetchScalarGridSpec(
        num_scalar_prefetch=1,
        grid=(b, steps),
        in_specs=[pl.BlockSpec((None, TS, D_NSA), lambda bi, si, pt: (bi, 0, C_NQ // D_NSA)),
                  pl.BlockSpec((None, KV_HEADS, TS, nsp), lambda bi, si, pt: (bi, 0, 0, 0)),
                  new_rows(2), new_rows(3), whole(bias_tail), whole(bias_new), whole(bias_far)]
                 + [_page_spec(layer, 2, p, SEL_PAGES, 2) for p in range(SEL_PAGES)]
                 + [_page_spec(layer, 3, p, SEL_PAGES, 2) for p in range(SEL_PAGES)],
        out_specs=pl.BlockSpec((None, KV_HEADS, ROWS_S, HEAD_DIM), lambda bi, si, pt: (bi, 0, 0, 0)),
        scratch_shapes=[pltpu.VMEM((SEL_TILE * PAGE_SIZE * KV_HEADS, HEAD_DIM), BF16),
                        pltpu.VMEM((SEL_TILE * PAGE_SIZE * KV_HEADS, HEAD_DIM), BF16),
                        pltpu.VMEM((KV_HEADS * ROWS_S, 1), F32), pltpu.VMEM((KV_HEADS * ROWS_S, 1), F32),
                        pltpu.VMEM((KV_HEADS * ROWS_S, HEAD_DIM), F32)],
    )
    return pl.pallas_call(
        functools.partial(_nsa_s_sel_kernel, pos0=pos0, npages=npages),
        grid_spec=grid_spec,
        out_shape=jax.ShapeDtypeStruct((b, KV_HEADS, ROWS_S, HEAD_DIM), F32),
        compiler_params=_cparams("parallel", "arbitrary"),
        name="nsa_sample_sel",
    )(page_table, z8, sel, z8, z8, bias_tail, bias_new, bias_far, *([cache6] * (2 * SEL_PAGES)))


def _nsa_s_win_kernel(q_ref, kw_ref, vw_ref, kn_ref, vn_ref, bw_ref, oc_ref, os_ref, gt_ref, gb_ref, o_ref,
                      m_scr, l_scr, acc_scr, *, pos0):
    g = pl.program_id(1)
    wb = kw_ref.shape[0]
    qs = _stack_heads(q_ref[...] * QK_SCALE).astype(BF16)
    qpos = pos0 + (lax.broadcasted_iota(jnp.int32, (ROWS_S, 1), 0) & (TS - 1))
    bias = bw_ref[...].reshape(ROWS_S, wb + PAGE_SIZE)
    _softmax_init(m_scr, l_scr, acc_scr)
    dist = qpos - (pos0 - wb + lax.broadcasted_iota(jnp.int32, (1, wb), 1))
    s = _dot_nt(qs, kw_ref[...].astype(BF16)) + bias[:, 0:wb]
    _softmax_update(s, (dist >= 0) & (dist <= WINDOW), vw_ref[...].astype(BF16), m_scr, l_scr, acc_scr)
    kidx = lax.broadcasted_iota(jnp.int32, (1, PAGE_SIZE), 1)
    dist = qpos - (pos0 + kidx)
    s = _dot_nt(qs, _pad_rows(kn_ref[...], PAGE_SIZE).astype(BF16)) + bias[:, wb:wb + PAGE_SIZE]
    _softmax_update(s, (dist >= 0) & (dist <= WINDOW) & (kidx < TS), _pad_rows(vn_ref[...], PAGE_SIZE).astype(BF16),
                    m_scr, l_scr, acc_scr)
    o_w = _softmax_result(l_scr, acc_scr)
    o_c = oc_ref[...]
    o_s = os_ref[...]
    gt = gt_ref[...] + gb_ref[...]
    for j in range(GROUP):
        col = GATE_NG + (g * GROUP + j) * 3
        r = slice(j * TS, (j + 1) * TS)
        o_ref[:, j * HEAD_DIM:(j + 1) * HEAD_DIM] = (
            _gate(gt, col) * o_c[r] + _gate(gt, col + 1) * o_s[r] + _gate(gt, col + 2) * o_w[r])


def _nsa_sample_win(z8, win2d, bias_w, o_c, o_s, gate_b, *, pos0):
    b, wb, _ = win2d.shape
    zcol = lambda col: (lambda bi, g: (bi, 0, col // HEAD_DIM + g))
    part = pl.BlockSpec((None, None, ROWS_S, HEAD_DIM), lambda bi, g: (bi, g, 0, 0))
    return pl.pallas_call(
        functools.partial(_nsa_s_win_kernel, pos0=pos0),
        grid=(b, KV_HEADS),
        in_specs=[pl.BlockSpec((None, TS, GROUP * HEAD_DIM), lambda bi, g: (bi, 0, C_NQ // (GROUP * HEAD_DIM) + g)),
                  pl.BlockSpec((None, wb, HEAD_DIM), lambda bi, g: (bi, 0, g)),
                  pl.BlockSpec((None, wb, HEAD_DIM), lambda bi, g: (bi, 0, KV_HEADS + g)),
                  pl.BlockSpec((None, TS, HEAD_DIM), zcol(C_NKV + 4 * KV_HEADS * HEAD_DIM)),
                  pl.BlockSpec((None, TS, HEAD_DIM), zcol(C_NKV + 5 * KV_HEADS * HEAD_DIM)),
                  pl.BlockSpec((GROUP, TS, wb + PAGE_SIZE), lambda bi, g: (g, 0, 0)),
                  part, part,
                  pl.BlockSpec((None, TS, LANES), lambda bi, g: (bi, 0, C_GATE // LANES)),
                  pl.BlockSpec((1, LANES), lambda bi, g: (0, 0))],
        out_specs=pl.BlockSpec((None, TS, GROUP * HEAD_DIM), lambda bi, g: (bi, 0, g)),
        out_shape=jax.ShapeDtypeStruct((b, TS, D_NSA), F32),
        scratch_shapes=[pltpu.VMEM((ROWS_S, 1), F32), pltpu.VMEM((ROWS_S, 1), F32), pltpu.VMEM((ROWS_S, HEAD_DIM), F32)],
        compiler_params=_cparams("parallel", "parallel"),
        name="nsa_sample_win",
    )(z8, win2d, win2d, z8, z8, bias_w, o_c, o_s, z8, gate_b)


def _t5_bucket(dist):
    n = np.maximum(dist, 0)
    exact = NUM_BUCKETS // 2
    nf = np.maximum(n, 1).astype(np.float32)
    large = exact + (np.log(nf / np.float32(exact)) / np.float32(math.log(MAX_DISTANCE / exact))
                     * np.float32(NUM_BUCKETS - exact)).astype(np.int32)
    return np.where(n < exact, n, np.minimum(large, NUM_BUCKETS - 1)).astype(np.int32)


FAR_DIST = int(np.max(np.nonzero(_t5_bucket(np.arange(4 * MAX_DISTANCE)) < NUM_BUCKETS - 1)[0])) + 1
N_TZ = (FAR_DIST + TK - 2) // TQ + 1


def _bias_table(rel_bias, dist):
    onehot = jax.nn.one_hot(jnp.asarray(_t5_bucket(dist).astype(np.int8)), NUM_BUCKETS, dtype=F32)
    return jnp.einsum("rcb,bh->hrc", onehot, rel_bias, precision=lax.Precision.HIGHEST)


def _pack_layer(p, l):
    w_in = p["w_in"][l]
    pad = jnp.zeros((D_MODEL, N_IN - C_GATE - 32), F32)
    w_in_p = jnp.concatenate([w_in[:, 0:2048], w_in[:, 2056:3080], w_in[:, 3080:4616], w_in[:, 4640:5152],
                              w_in[:, 2048:2056], w_in[:, 4616:4640], pad], axis=1).astype(BF16)
    w_in_p = _col_tiles(w_in_p, IN_TN)
    gb = p["mlstm_gate_b"][l]
    mlstm_gate_b = jnp.zeros((1, LANES), F32).at[0, GATE_MI:GATE_MI + 4].set(gb[0]).at[0, GATE_MF:GATE_MF + 4].set(gb[1])
    nsa_gate_b = jnp.zeros((1, LANES), F32).at[0, GATE_NG:GATE_NG + 3 * NSA_HEADS].set(p["nsa_gate_b"][l].reshape(-1))
    w1 = p["cmp_w1"][l]
    wcat = jnp.concatenate([w1[:, 0:CMP_STRIDE], w1[:, CMP_STRIDE:CMP_BLOCK]], axis=-1).astype(BF16)
    lam = lax.complex(p["s5_a_re"][l], p["s5_a_im"][l])
    lam_bar = jnp.exp(lam * jnp.exp(p["s5_log_step"][l])[:, None])
    b_bar = ((lam_bar - 1.0) / lam)[..., None] * lax.complex(p["s5_b_re"][l], p["s5_b_im"][l])
    gi = S5_GROUPS // S5_IN_TILES
    bd_in = lambda m: jnp.einsum("cgph,gk->cghkp", m.reshape(S5_IN_TILES, gi, S5_STATE, S5_GROUP_WIDTH),
                                 jnp.eye(gi, dtype=F32)).reshape(S5_IN_TILES, S5_IN_FEATS, 256)
    wb = jnp.concatenate([bd_in(b_bar.real), bd_in(b_bar.imag)], axis=2).astype(BF16)
    go = S5_GROUPS // S5_OUT_TILES
    bd_out = lambda m: jnp.einsum("jghp,gk->jgpkh", m.reshape(S5_OUT_TILES, go, S5_GROUP_WIDTH, S5_STATE),
                                  jnp.eye(go, dtype=F32)).reshape(S5_OUT_TILES, S5_OUT_CH, 256)
    wc = jnp.stack([bd_out(p["s5_c_re"][l]), -bd_out(p["s5_c_im"][l])]).astype(BF16)
    half = lambda w: jnp.pad(w, ((0, 0), (0, D_FFP - D_FF)))
    w_up = p["ffn_w_up"][l]
    conv_w = p["ffn_conv_w"][l]
    conv_b = p["ffn_conv_b"][l][None, :]
    return dict(
        w_in=w_in_p, mlstm_gate_b=mlstm_gate_b, mlstm_norm_w=p["mlstm_norm_w"][l][None, :], nsa_gate_b=nsa_gate_b,
        wcat=wcat, cmp_b1=p["cmp_b1"][l], cmp_w2=p["cmp_w2"][l].astype(BF16),
        s5_wb=wb, s5_wc=wc, s5_lam_r=lam_bar.real.reshape(1, S5_CH), s5_lam_i=lam_bar.imag.reshape(1, S5_CH),
        s5_d=p["s5_d"][l][None, :], s5_glu_w=p["s5_glu_w"][l].astype(BF16), s5_glu_b=p["s5_glu_b"][l][None, :],
        w_out=p["w_out"][l].astype(BF16), ln1_w=p["ln1_w"][l][None, :], ln1_b=p["ln1_b"][l][None, :],
        w_up=_col_tiles(jnp.concatenate([half(w_up[:, :D_FF]), half(w_up[:, D_FF:])], axis=1).astype(BF16), FFN_TF),
        conv_w8=jnp.pad(jnp.concatenate([half(conv_w[:, :D_FF]), half(conv_w[:, D_FF:])], axis=1),
                        ((0, SUBLANES - CONV_W), (0, 0))),
        conv_b=jnp.concatenate([half(conv_b[:, :D_FF]), half(conv_b[:, D_FF:])], axis=1),
        w_down=jnp.pad(p["ffn_w_down"][l], ((0, D_FFP - D_FF), (0, 0))).astype(BF16),
        ln2_w=p["ln2_w"][l][None, :], ln2_b=p["ln2_b"][l][None, :],
    )


def _unpad_ff(x):
    return jnp.concatenate([x[..., :D_FF], x[..., D_FFP:D_FFP + D_FF]], axis=-1)


def _pad_ff(x):
    pad = [(0, 0)] * (x.ndim - 1) + [(0, D_FFP - D_FF)]
    return jnp.concatenate([jnp.pad(x[..., :D_FF], pad), jnp.pad(x[..., D_FF:], pad)], axis=-1)


def _time_major(x, b, t, t_use):
    return x.reshape(b, t, -1)[:, :t_use].transpose(1, 0, 2).reshape(t_use * b, -1)


def _batch_major(x, b, t, t_use):
    x = x.reshape(t_use, b, -1).transpose(1, 0, 2)
    return jnp.pad(x, ((0, 0), (0, t - t_use), (0, 0))).reshape(b * t, -1)


def _mixer_tail(x2d, z, b, t, t_use, lw, o_mlstm, o_nsa, s5_state, conv_state8, tm):
    pad8 = lambda s: jnp.pad(s.reshape(b, S5_CH), ((0, SUBLANES - b), (0, 0)))
    u_tm = _time_major(z[:, C_SU:C_SU + D_S5], b, t, t_use)
    o_s5_tm, xr, xi = _s5(u_tm, lw["s5_wb"], lw["s5_wc"], lw["s5_lam_r"], lw["s5_lam_i"], lw["s5_d"],
                          lw["s5_glu_w"], lw["s5_glu_b"], pad8(s5_state[0]), pad8(s5_state[1]), lc=S5_LC, rb=b)
    o_s5 = _batch_major(o_s5_tm, b, t, t_use)
    x1 = _wout_ln(o_mlstm.reshape(b * t, D_MLSTM), o_nsa.reshape(b * t, D_NSA), o_s5, x2d,
                  lw["w_out"], lw["ln1_w"], lw["ln1_b"], tm=256)
    s5_new = (xr[:b].reshape(b, S5_GROUPS, S5_STATE), xi[:b].reshape(b, S5_GROUPS, S5_STATE))
    if conv_state8 is None:
        x2, tail_a, tail_g = _ffn_prompt(x1.reshape(b, t, D_MODEL), lw["w_up"], lw["w_down"], lw["conv_w8"],
                                         lw["conv_b"], lw["ln2_w"], lw["ln2_b"], tt=FFN_TT, tf=FFN_TF)
        keep = slice(SUBLANES - (CONV_W - 1), SUBLANES)
        conv_new = jnp.concatenate([tail_a[:, keep, :D_FF], tail_g[:, keep, :D_FF]], axis=-1)
        return x2.reshape(b * t, D_MODEL), s5_new, conv_new
    up = _matmul(x1, lw["w_up"], tm=tm)
    hgate = _convgate(up.reshape(b, t, 2 * D_FFP), conv_state8, lw["conv_w8"], lw["conv_b"], tt=256, tf=512)
    x2 = _down_ln(hgate.reshape(b * t, D_FFP), lw["w_down"], x1, lw["ln2_w"], lw["ln2_b"], tm=512, tk=1408)
    conv_new = _unpad_ff(up.reshape(b, t, 2 * D_FFP)[:, t_use - (CONV_W - 1):t_use])
    return x2, s5_new, conv_new


def _prompt_layer(x2d, b, t, lw, rel_bias):
    z, kv_rows, win_rows = _proj_in(x2d, lw["w_in"], tm=min(1024, b * t))
    z3 = z.reshape(b, t, N_IN)
    zeros = lambda *s: jnp.zeros(s, F32)
    o_mlstm, c1, n1, m1 = _mlstm(z3, lw["mlstm_gate_b"], lw["mlstm_norm_w"],
                                 zeros(b, MLSTM_HEADS, HEAD_DIM, HEAD_DIM), zeros(b, MLSTM_HEADS, LANES),
                                 zeros(b, MLSTM_HEADS, LANES), L=MLSTM_L, lin=MLSTM_L, t_valid=t)
    kcvc = _cmp_finish(_cmp_project_prompt(z3, lw["wcat"]), lw["cmp_b1"], lw["cmp_w2"])
    ncp = t // CMP_STRIDE
    bias_c = _bias_table(rel_bias, np.arange(t)[:, None] - (np.arange(ncp) * CMP_STRIDE + CMP_BLOCK - 1)[None, :])
    ti = np.arange(TQ)[:, None] - np.arange(TK)[None, :]
    far = _bias_table(rel_bias, np.full((1, 1), FAR_DIST))
    tz = jnp.stack([_bias_table(rel_bias, d * TQ + ti) - far for d in range(N_TZ)])
    o_nsa = _nsa_prompt(z3, kcvc, bias_c, tz, lw["nsa_gate_b"])
    x2, s5_new, conv_new = _mixer_tail(
        x2d, z, b, t, t, lw, o_mlstm, o_nsa, (zeros(b, S5_GROUPS, S5_STATE), zeros(b, S5_GROUPS, S5_STATE)),
        None, tm=1024)
    wrows = min(WINDOW, t)
    state = (kv_rows.reshape(b, t, 4, KV_HEADS, HEAD_DIM), win_rows.reshape(b, t, 2, KV_HEADS, HEAD_DIM)[:, t - wrows:],
             c1, n1, m1[:, :, 0], s5_new[0], s5_new[1], conv_new)
    return x2, state


def _sample_layer(x2d, b, tn, lw, rel_bias, layer, cache6, page_table, win2d, mlstm_state, s5_state, conv_state):
    npages = page_table.shape[1]
    pos0 = npages * PAGE_SIZE
    z = _matmul(x2d, lw["w_in"], tm=1024)
    z8 = z.reshape(b, TS, N_IN)
    c0, n0, m0 = mlstm_state
    o_mlstm, c1, n1, m1 = _mlstm(z8, lw["mlstm_gate_b"], lw["mlstm_norm_w"], c0, n0,
                                 jnp.broadcast_to(m0[:, :, None], (b, MLSTM_HEADS, LANES)),
                                 L=LANES, lin=TS, t_valid=tn)
    n_chunks = (pos0 + tn) // CMP_STRIDE
    n_cmp = n_chunks - CMP_BLOCK // CMP_STRIDE + 1
    n_sel = -(-(pos0 + tn) // SEL_BLOCK)
    kcvc = _cmp_finish(_cmp_project_sample(cache6, page_table, layer, lw["wcat"]), lw["cmp_b1"], lw["cmp_w2"])
    qpos = pos0 + np.arange(TS)[:, None]
    bias_c = _bias_table(rel_bias, qpos - (np.arange(n_chunks) * CMP_STRIDE + CMP_BLOCK - 1)[None, :])
    o_c, sel = _nsa_sample_cmp(z8, kcvc, bias_c, pos0=pos0, n_cmp=n_cmp, n_sel=n_sel)
    kk = np.arange(PAGE_SIZE)[None, :]
    bias_tail = _bias_table(rel_bias, np.repeat(qpos - (pos0 - PAGE_SIZE + kk), KV_HEADS, axis=1))
    bias_new = _bias_table(rel_bias, qpos - (pos0 + kk))
    bias_far = _bias_table(rel_bias, np.broadcast_to(qpos - (pos0 - 2 * PAGE_SIZE), (TS, PAGE_SIZE)))
    o_s = _nsa_sample_sel(z8, sel, bias_tail, bias_new, bias_far, cache6, page_table, layer, pos0=pos0)
    wb = win2d.shape[1]
    wk = np.arange(wb)[None, :]
    bias_w = _bias_table(rel_bias, np.concatenate([qpos - (pos0 - wb + wk), qpos - (pos0 + kk)], axis=1))
    o_nsa = _nsa_sample_win(z8, win2d, bias_w, o_c, o_s, lw["nsa_gate_b"], pos0=pos0)
    conv_state8 = jnp.pad(_pad_ff(conv_state), ((0, 0), (SUBLANES - (CONV_W - 1), 0), (0, 0)))
    x2, s5_new, conv_new = _mixer_tail(x2d, z, b, TS, tn, lw, o_mlstm, o_nsa, s5_state, conv_state8, tm=1024)
    nkv = z8[:, :tn, C_NKV:C_NKV + N_KV_SLOTS * KV_HEADS * HEAD_DIM].reshape(b, tn, N_KV_SLOTS, KV_HEADS, HEAD_DIM)
    state = (nkv[:, :, :4], nkv[:, :, 4:], c1, n1, m1[:, :, 0], s5_new[0], s5_new[1], conv_new)
    return x2, state


def kernel(x_prompt, x_sample, cache_nsa_kv, cache_win_kv, state_mlstm_c, state_mlstm_n, state_mlstm_m,
           state_s5_re, state_s5_im, state_ffn_conv, page_table, w_in, mlstm_gate_b, mlstm_norm_w,
           nsa_gate_b, cmp_w1, cmp_b1, cmp_w2, rel_bias, s5_a_re, s5_a_im, s5_b_re, s5_b_im, s5_c_re,
           s5_c_im, s5_d, s5_log_step, s5_glu_w, s5_glu_b, w_out, ln1_w, ln1_b, ffn_w_up, ffn_conv_w,
           ffn_conv_b, ffn_w_down, ln2_w, ln2_b):
    params = dict(w_in=w_in, mlstm_gate_b=mlstm_gate_b, mlstm_norm_w=mlstm_norm_w, nsa_gate_b=nsa_gate_b,
                  cmp_w1=cmp_w1, cmp_b1=cmp_b1, cmp_w2=cmp_w2, s5_a_re=s5_a_re, s5_a_im=s5_a_im, s5_b_re=s5_b_re,
                  s5_b_im=s5_b_im, s5_c_re=s5_c_re, s5_c_im=s5_c_im, s5_d=s5_d, s5_log_step=s5_log_step,
                  s5_glu_w=s5_glu_w, s5_glu_b=s5_glu_b, w_out=w_out, ln1_w=ln1_w, ln1_b=ln1_b, ffn_w_up=ffn_w_up,
                  ffn_conv_w=ffn_conv_w, ffn_conv_b=ffn_conv_b, ffn_w_down=ffn_w_down, ln2_w=ln2_w, ln2_b=ln2_b)
    depth = w_in.shape[0]
    bp, tp, _ = x_prompt.shape
    bs, tn, _ = x_sample.shape
    assert tp % TK == 0 and tn < CMP_STRIDE and tn <= TS and SUBLANES % bp == 0 and SUBLANES % bs == 0
    assert (tn * bs) % SUBLANES == 0 and tp % min(FFN_TT, tp) == 0
    assert page_table.shape[1] % CMP_PAGES == 0 and cache_nsa_kv.shape[2] == PAGE_SIZE
    win2d = cache_win_kv.reshape(depth, bs, cache_win_kv.shape[2], 2 * KV_HEADS * HEAD_DIM)
    xp = x_prompt.reshape(bp * tp, D_MODEL)
    xs = jnp.pad(x_sample, ((0, 0), (0, TS - tn), (0, 0))).reshape(bs * TS, D_MODEL)
    p_states, s_states = [], []
    for l in range(depth):
        lw = _pack_layer(params, l)
        xp, sp = _prompt_layer(xp, bp, tp, lw, rel_bias)
        xs, ss = _sample_layer(xs, bs, tn, lw, rel_bias, l, cache_nsa_kv, page_table, win2d[l],
                               (state_mlstm_c[l], jnp.pad(state_mlstm_n[l], ((0, 0), (0, 0), (0, LANES - HEAD_DIM))),
                                state_mlstm_m[l]),
                               (state_s5_re[l], state_s5_im[l]), state_ffn_conv[l])
        p_states.append(sp)
        s_states.append(ss)
    stk = lambda states, i: jnp.stack([s[i] for s in states])
    y_prompt = xp.reshape(bp, tp, D_MODEL)
    y_sample = xs.reshape(bs, TS, D_MODEL)[:, :tn]
    return (y_prompt, y_sample,
            stk(p_states, 0), stk(s_states, 0), stk(p_states, 1), stk(s_states, 1),
            stk(p_states, 2), stk(s_states, 2), stk(p_states, 3), stk(s_states, 3), stk(p_states, 4), stk(s_states, 4),
            stk(p_states, 5), stk(s_states, 5), stk(p_states, 6), stk(s_states, 6), stk(p_states, 7), stk(s_states, 7))
```

```python
import functools
import math

import jax
import jax.numpy as jnp
import numpy as np
from jax import lax
from jax.experimental import pallas as pl
from jax.experimental.pallas import tpu as pltpu

F32 = jnp.float32
BF16 = jnp.bfloat16
NEG_INF = float("-inf")
M_INIT = -1e30
MASKED = -1e30

D_MODEL = 2048
PAGE_SIZE = 128
D_MLSTM = D_MODEL // 4
D_NSA = D_MODEL // 2
D_S5 = D_MODEL - D_MLSTM - D_NSA
HEAD_DIM = 128
MLSTM_HEADS = D_MLSTM // HEAD_DIM
NSA_HEADS = D_NSA // HEAD_DIM
KV_HEADS = 2
GROUP = NSA_HEADS // KV_HEADS
N_KV_SLOTS = 6
CMP_BLOCK = 32
CMP_STRIDE = 16
SEL_BLOCK = 64
SEL_TOPK = 16
FORCE_SCORE = 1e4
WINDOW = 512
S5_GROUP_WIDTH = 16
S5_GROUPS = D_S5 // S5_GROUP_WIDTH
S5_STATE = 64
S5_CH = S5_GROUPS * S5_STATE
NUM_BUCKETS = 32
MAX_DISTANCE = 128
D_FF = ((8 * D_MODEL // 3 + 127) // 128) * 128
CONV_W = 3
LN_EPS = 1e-5
DEPTH = 2
DEEPNORM_ALPHA = (2 * DEPTH) ** 0.25
QK_SCALE = HEAD_DIM ** -0.5

LANES = 128
SUBLANES = 8
V7X_VMEM_LIMIT = 56 * 2 ** 20

C_MQ, C_MK, C_MV, C_MO = 0, 512, 1024, 1536
C_NQ = 2048
C_NKV = 3072
C_SU = 4608
C_GATE = 5120
N_IN = 5376
GATE_MI, GATE_MF, GATE_NG = 0, 4, 8
D_FFP = 5632
MLSTM_L = 256
TQ = 256
TK = 256
S5_LC = 128
FFN_TT = 1024
FFN_TF = 512
FFN_HALO = 16
CMP_PAGES = 32
SEL_PAGES = 32
SEL_TILE = 16
SEL_SHIFT = SEL_BLOCK.bit_length() - 1


def _cparams(*sem):
    return pltpu.CompilerParams(dimension_semantics=sem, vmem_limit_bytes=V7X_VMEM_LIMIT)


def _dot(a, b):
    return jnp.dot(a, b, preferred_element_type=F32)


def _dot_nt(a, b):
    return lax.dot_general(a, b, (((1,), (1,)), ((), ())), preferred_element_type=F32)


def _layer_norm(y, w, b):
    mu = jnp.mean(y, axis=-1, keepdims=True)
    d = y - mu
    var = jnp.mean(d * d, axis=-1, keepdims=True)
    return d * lax.rsqrt(var + LN_EPS) * w + b


def _mm_kernel(x_ref, w_ref, o_ref):
    o_ref[...] = _dot(x_ref[...].astype(BF16), w_ref[...]).astype(o_ref.dtype)


def _matmul(x, w, *, tm, tn):
    m, k = x.shape
    n = w.shape[1]
    tm = min(tm, m)
    return pl.pallas_call(
        _mm_kernel,
        grid=(m // tm, n // tn),
        in_specs=[pl.BlockSpec((tm, k), lambda i, j: (i, 0)), pl.BlockSpec((k, tn), lambda i, j: (0, j))],
        out_specs=pl.BlockSpec((tm, tn), lambda i, j: (i, j)),
        out_shape=jax.ShapeDtypeStruct((m, n), F32),
        compiler_params=_cparams("parallel", "parallel"),
        name="proj",
    )(x, w)


KV4_GROUPS = 4 * KV_HEADS
WIN_GROUPS = 2 * KV_HEADS


def _proj_in_kernel(x_ref, w_ref, z_ref, kv_ref, win_ref, *, tn):
    j = pl.program_id(1)
    acc = _dot(x_ref[...].astype(BF16), w_ref[...])
    z_ref[...] = acc
    tm = acc.shape[0]
    per_tile = tn // HEAD_DIM
    first_tile = C_NKV // tn
    for tile in range(first_tile, first_tile + N_KV_SLOTS * KV_HEADS // per_tile):
        @pl.when(j == tile)
        def _(tile=tile):
            for c in range(per_tile):
                grp = (tile - first_tile) * per_tile + c
                val = acc[:, c * HEAD_DIM:(c + 1) * HEAD_DIM]
                if grp < KV4_GROUPS:
                    kv_ref[pl.ds(grp, tm, stride=KV4_GROUPS), :] = val
                else:
                    win_ref[pl.ds(grp - KV4_GROUPS, tm, stride=WIN_GROUPS), :] = val


def _proj_in(x, w, *, tm, tn):
    m, k = x.shape
    n = w.shape[1]
    assert C_NKV % tn == 0 and (N_KV_SLOTS * KV_HEADS * HEAD_DIM) % tn == 0
    return pl.pallas_call(
        functools.partial(_proj_in_kernel, tn=tn),
        grid=(m // tm, n // tn),
        in_specs=[pl.BlockSpec((tm, k), lambda i, j: (i, 0)), pl.BlockSpec((k, tn), lambda i, j: (0, j))],
        out_specs=[pl.BlockSpec((tm, tn), lambda i, j: (i, j)),
                   pl.BlockSpec((tm * KV4_GROUPS, HEAD_DIM), lambda i, j: (i, 0)),
                   pl.BlockSpec((tm * WIN_GROUPS, HEAD_DIM), lambda i, j: (i, 0))],
        out_shape=[jax.ShapeDtypeStruct((m, n), F32),
                   jax.ShapeDtypeStruct((m * KV4_GROUPS, HEAD_DIM), F32),
                   jax.ShapeDtypeStruct((m * WIN_GROUPS, HEAD_DIM), F32)],
        compiler_params=_cparams("parallel", "arbitrary"),
        name="proj_in",
    )(x, w)


def _wout_kernel(om_ref, on_ref, os_ref, x_ref, w_ref, lw_ref, lb_ref, o_ref):
    acc = _dot(om_ref[...].astype(BF16), w_ref[0:D_MLSTM, :])
    acc += _dot(on_ref[...].astype(BF16), w_ref[D_MLSTM:D_MLSTM + D_NSA, :])
    acc += _dot(os_ref[...].astype(BF16), w_ref[D_MLSTM + D_NSA:D_MODEL, :])
    o_ref[...] = _layer_norm(DEEPNORM_ALPHA * x_ref[...] + acc, lw_ref[...], lb_ref[...])


def _wout_ln(om, on, os_, x, w, lw, lb, *, tm):
    m = x.shape[0]
    tm = min(tm, m)
    row = lambda width: pl.BlockSpec((tm, width), lambda i: (i, 0))
    full = lambda a: pl.BlockSpec(a.shape, lambda i: (0, 0))
    return pl.pallas_call(
        _wout_kernel,
        grid=(m // tm,),
        in_specs=[row(D_MLSTM), row(D_NSA), row(D_S5), row(D_MODEL), full(w), full(lw), full(lb)],
        out_specs=row(D_MODEL),
        out_shape=jax.ShapeDtypeStruct((m, D_MODEL), F32),
        compiler_params=_cparams("parallel"),
        name="wout_ln",
    )(om, on, os_, x, w, lw, lb)


def _valid_rows(w_ref, tile):
    rows = w_ref.shape[0]
    row = tile * rows + lax.broadcasted_iota(jnp.int32, (rows, 1), 0)
    w = w_ref[...]
    return jnp.where(row < D_FF, w, jnp.zeros_like(w))


def _down_kernel(h_ref, w_ref, x_ref, lw_ref, lb_ref, o_ref, acc_ref, *, nk):
    k = pl.program_id(1)

    @pl.when(k == 0)
    def _():
        acc_ref[...] = jnp.zeros_like(acc_ref)

    acc_ref[...] += _dot(h_ref[...], _valid_rows(w_ref, k))

    @pl.when(k == nk - 1)
    def _():
        o_ref[...] = _layer_norm(DEEPNORM_ALPHA * x_ref[...] + acc_ref[...], lw_ref[...], lb_ref[...])


def _down_ln(h, w, x, lw, lb, *, tm, tk):
    m, kk = h.shape
    tm = min(tm, m)
    nk = kk // tk
    return pl.pallas_call(
        functools.partial(_down_kernel, nk=nk),
        grid=(m // tm, nk),
        in_specs=[
            pl.BlockSpec((tm, tk), lambda i, k: (i, k)),
            pl.BlockSpec((tk, D_MODEL), lambda i, k: (k, 0)),
            pl.BlockSpec((tm, D_MODEL), lambda i, k: (i, 0)),
            pl.BlockSpec((1, D_MODEL), lambda i, k: (0, 0)),
            pl.BlockSpec((1, D_MODEL), lambda i, k: (0, 0)),
        ],
        out_specs=pl.BlockSpec((tm, D_MODEL), lambda i, k: (i, 0)),
        out_shape=jax.ShapeDtypeStruct((m, D_MODEL), F32),
        scratch_shapes=[pltpu.VMEM((tm, D_MODEL), F32)],
        compiler_params=_cparams("parallel", "arbitrary"),
        name="down_ln",
    )(h, w, x, lw, lb)


def _convgate_kernel(a_ref, g_ref, ha_ref, hg_ref, sa_ref, sg_ref, wa_ref, wg_ref, ba_ref, bg_ref, o_ref):
    first = pl.program_id(1) == 0
    tt = a_ref.shape[0]
    rows = lax.broadcasted_iota(jnp.int32, (tt, 1), 0)

    def conv(cur_ref, halo_ref, st_ref, w_ref, b_ref):
        cur = cur_ref[...]
        prev = jnp.where(first, st_ref[...], halo_ref[...])
        p1 = prev[7:8, :]
        p2 = prev[6:7, :]
        x1 = jnp.where(rows == 0, p1, pltpu.roll(cur, 1, 0))
        x2 = jnp.where(rows == 0, p2, jnp.where(rows == 1, p1, pltpu.roll(cur, 2, 0)))
        w = w_ref[...]
        return b_ref[...] + w[0:1, :] * x2 + w[1:2, :] * x1 + w[2:3, :] * cur

    a = conv(a_ref, ha_ref, sa_ref, wa_ref, ba_ref)
    g = conv(g_ref, hg_ref, sg_ref, wg_ref, bg_ref)
    o_ref[...] = (a * jax.nn.sigmoid(a) * g).astype(o_ref.dtype)


def _convgate(up, state8, conv_w8, conv_b, *, tt, tf):
    b, t, _ = up.shape
    tt = min(tt, t)
    nf = D_FFP // tf
    hb = tt // SUBLANES
    cur_a = pl.BlockSpec((None, tt, tf), lambda bi, ti, fi: (bi, ti, fi))
    cur_g = pl.BlockSpec((None, tt, tf), lambda bi, ti, fi: (bi, ti, fi + nf))
    halo_a = pl.BlockSpec((None, SUBLANES, tf), lambda bi, ti, fi: (bi, jnp.maximum(ti * hb - 1, 0), fi))
    halo_g = pl.BlockSpec((None, SUBLANES, tf), lambda bi, ti, fi: (bi, jnp.maximum(ti * hb - 1, 0), fi + nf))
    st_a = pl.BlockSpec((None, SUBLANES, tf), lambda bi, ti, fi: (bi, 0, fi))
    st_g = pl.BlockSpec((None, SUBLANES, tf), lambda bi, ti, fi: (bi, 0, fi + nf))
    w_a = pl.BlockSpec((SUBLANES, tf), lambda bi, ti, fi: (0, fi))
    w_g = pl.BlockSpec((SUBLANES, tf), lambda bi, ti, fi: (0, fi + nf))
    b_a = pl.BlockSpec((1, tf), lambda bi, ti, fi: (0, fi))
    b_g = pl.BlockSpec((1, tf), lambda bi, ti, fi: (0, fi + nf))
    return pl.pallas_call(
        _convgate_kernel,
        grid=(b, t // tt, nf),
        in_specs=[cur_a, cur_g, halo_a, halo_g, st_a, st_g, w_a, w_g, b_a, b_g],
        out_specs=pl.BlockSpec((None, tt, tf), lambda bi, ti, fi: (bi, ti, fi)),
        out_shape=jax.ShapeDtypeStruct((b, t, D_FFP), BF16),
        compiler_params=_cparams("parallel", "parallel", "parallel"),
        name="convgate",
    )(up, up, up, up, state8, state8, conv_w8, conv_w8, conv_b, conv_b)


def _ffn_kernel(x_ref, xh_ref, wa_ref, wg_ref, wd_ref, cwa_ref, cwg_ref, cba_ref, cbg_ref, lw_ref, lb_ref,
                o_ref, sa_ref, sg_ref, xe_ref, *, nf):
    first = pl.program_id(1) == 0
    f = pl.program_id(2)
    tt = x_ref.shape[0]

    @pl.when(f == 0)
    def _():
        o_ref[...] = jnp.zeros_like(o_ref)
        halo = jnp.where(first, 0.0, xh_ref[...])
        xe_ref[0:FFN_HALO, :] = halo.astype(BF16)
        xe_ref[FFN_HALO:, :] = x_ref[...].astype(BF16)

    xe = xe_ref[...]

    def branch(w_ref, cw_ref, cb_ref, s_ref):
        up = _dot(xe, w_ref[...])
        s_ref[...] = up[FFN_HALO + tt - SUBLANES:FFN_HALO + tt, :]
        w = cw_ref[...]
        x1 = pltpu.roll(up, 1, 0)[FFN_HALO:, :]
        x2 = pltpu.roll(up, 2, 0)[FFN_HALO:, :]
        return cb_ref[...] + w[0:1, :] * x2 + w[1:2, :] * x1 + w[2:3, :] * up[FFN_HALO:, :]

    a = branch(wa_ref, cwa_ref, cba_ref, sa_ref)
    g = branch(wg_ref, cwg_ref, cbg_ref, sg_ref)
    o_ref[...] += _dot((a * jax.nn.sigmoid(a) * g).astype(BF16), _valid_rows(wd_ref, f))

    @pl.when(f == nf - 1)
    def _():
        o_ref[...] = _layer_norm(DEEPNORM_ALPHA * x_ref[...] + o_ref[...], lw_ref[...], lb_ref[...])


def _ffn_prompt(x3, w_up, w_down, conv_w8, conv_b, lw, lb, *, tt, tf):
    b, t, _ = x3.shape
    nf = D_FFP // tf
    tt = min(tt, t)
    hb = tt // FFN_HALO
    half_a = lambda rows: pl.BlockSpec((rows, tf), lambda bi, ti, fi: (0, fi))
    half_g = lambda rows: pl.BlockSpec((rows, tf), lambda bi, ti, fi: (0, fi + nf))
    vec = pl.BlockSpec((1, D_MODEL), lambda bi, ti, fi: (0, 0))
    tail = pl.BlockSpec((None, SUBLANES, tf), lambda bi, ti, fi: (bi, 0, fi))
    return pl.pallas_call(
        functools.partial(_ffn_kernel, nf=nf),
        grid=(b, t // tt, nf),
        in_specs=[pl.BlockSpec((None, tt, D_MODEL), lambda bi, ti, fi: (bi, ti, 0), pipeline_mode=pl.Buffered(1)),
                  pl.BlockSpec((None, FFN_HALO, D_MODEL), lambda bi, ti, fi: (bi, jnp.maximum(ti * hb - 1, 0), 0)),
                  half_a(D_MODEL), half_g(D_MODEL),
                  pl.BlockSpec((tf, D_MODEL), lambda bi, ti, fi: (fi, 0)),
                  half_a(SUBLANES), half_g(SUBLANES), half_a(1), half_g(1), vec, vec],
        out_specs=[pl.BlockSpec((None, tt, D_MODEL), lambda bi, ti, fi: (bi, ti, 0)), tail, tail],
        out_shape=[jax.ShapeDtypeStruct((b, t, D_MODEL), F32),
                   jax.ShapeDtypeStruct((b, SUBLANES, D_FFP), F32),
                   jax.ShapeDtypeStruct((b, SUBLANES, D_FFP), F32)],
        scratch_shapes=[pltpu.VMEM((tt + FFN_HALO, D_MODEL), BF16)],
        compiler_params=_cparams("parallel", "arbitrary", "arbitrary"),
        name="ffn_prompt",
    )(x3, x3, w_up, w_up, w_down, conv_w8, conv_w8, conv_b, conv_b, lw, lb)


def _cumsum_rows(x):
    n = x.shape[0]
    rows = lax.broadcasted_iota(jnp.int32, (n, 1), 0)
    d = 1
    while d < n:
        x = x + jnp.where(rows >= d, pltpu.roll(x, d, 0), 0.0)
        d *= 2
    return x


def _log_sigmoid(x):
    return jnp.minimum(x, 0.0) - jnp.log1p(jnp.exp(-jnp.abs(x)))


def _mlstm_kernel(q_ref, k_ref, v_ref, og_ref, g_ref, gb_ref, nw_ref, c0_ref, n0_ref, m0_ref,
                  out_ref, c_ref, n_ref, m_ref, *, L, t_valid):
    ci = pl.program_id(1)
    lin = q_ref.shape[0]

    @pl.when(ci == 0)
    def _():
        c_ref[...] = c0_ref[...]
        n_ref[...] = n0_ref[...]
        m_ref[...] = m0_ref[...]

    def rows_of(ref):
        x = ref[...]
        if lin < L:
            x = jnp.concatenate([x, jnp.zeros((L - lin, x.shape[1]), x.dtype)], axis=0)
        return x

    rows = lax.broadcasted_iota(jnp.int32, (L, 1), 0)
    valid = (ci * L + rows) < t_valid
    pre = rows_of(g_ref) + gb_ref[...]
    lf = jnp.where(valid, _log_sigmoid(pre), 0.0)
    ig = jnp.where(valid, pre, NEG_INF)
    bcum = _cumsum_rows(lf)
    dt = (pltpu.roll(ig, GATE_MF - GATE_MI, 1) - bcum).T
    q_all, k_all, v_all, og_all = rows_of(q_ref), rows_of(k_ref), rows_of(v_ref), rows_of(og_ref)
    tri = lax.broadcasted_iota(jnp.int32, (L, L), 0) >= lax.broadcasted_iota(jnp.int32, (L, L), 1)
    nw = nw_ref[...]

    for h in range(MLSTM_HEADS):
        sl = slice(h * HEAD_DIM, (h + 1) * HEAD_DIM)
        q = q_all[:, sl]
        k = k_all[:, sl] * QK_SCALE
        v = v_all[:, sl]
        qb, kb, vb = q.astype(BF16), k.astype(BF16), v.astype(BF16)
        b_col = bcum[:, GATE_MF + h:GATE_MF + h + 1]
        ig_col = ig[:, GATE_MI + h:GATE_MI + h + 1]
        d_row = dt[GATE_MF + h:GATE_MF + h + 1, :]
        c_prev = c_ref[h]
        n_prev = n_ref[h:h + 1, :]
        m_prev = m_ref[h:h + 1, 0:1]

        dmat = jnp.where(tri, b_col + d_row, NEG_INF)
        g_col = b_col + m_prev
        m_row = jnp.maximum(jnp.max(dmat, axis=1, keepdims=True), g_col)
        a = jnp.exp(dmat - m_row) * _dot_nt(qb, kb)
        w_inter = jnp.exp(g_col - m_row)
        num = _dot(a.astype(BF16), vb) + w_inter * _dot(qb, c_prev.astype(BF16))
        den = jnp.sum(a, axis=1, keepdims=True) + w_inter * jnp.sum(q * n_prev, axis=1, keepdims=True)
        hid = num / jnp.maximum(jnp.abs(den), jnp.exp(-m_row))
        mu = jnp.mean(hid, axis=1, keepdims=True)
        dlt = hid - mu
        var = jnp.mean(dlt * dlt, axis=1, keepdims=True)
        hn = dlt * lax.rsqrt(var + LN_EPS) * nw[:, sl]
        res = jax.nn.sigmoid(og_all[:, sl]) * hn
        out_ref[:, sl] = res[0:lin, :]

        f_tot = b_col[L - 1:L, :]
        w_s = f_tot - b_col + ig_col
        m_new = jnp.maximum(f_tot + m_prev, jnp.max(w_s, axis=0, keepdims=True))
        ws = jnp.exp(w_s - m_new)
        decay = jnp.exp(f_tot + m_prev - m_new)
        c_ref[h] = decay * c_prev + _dot(k.T.astype(BF16), (ws * v).astype(BF16))
        n_ref[h:h + 1, :] = decay * n_prev + jnp.sum(ws * k, axis=0, keepdims=True)
        m_ref[h:h + 1, :] = jnp.broadcast_to(m_new, (1, LANES))


def _mlstm(z, gate_b, norm_w, c0, n0, m0, *, L, lin, t_valid):
    b, tz, _ = z.shape
    nchunks = tz // lin
    qblk = lambda col: pl.BlockSpec((None, lin, D_MLSTM), lambda bi, ci: (bi, ci, col // D_MLSTM))
    st4 = pl.BlockSpec((None, MLSTM_HEADS, HEAD_DIM, HEAD_DIM), lambda bi, ci: (bi, 0, 0, 0))
    st3 = pl.BlockSpec((None, MLSTM_HEADS, LANES), lambda bi, ci: (bi, 0, 0))
    return pl.pallas_call(
        functools.partial(_mlstm_kernel, L=L, t_valid=t_valid),
        grid=(b, nchunks),
        in_specs=[qblk(C_MQ), qblk(C_MK), qblk(C_MV), qblk(C_MO),
                  pl.BlockSpec((None, lin, LANES), lambda bi, ci: (bi, ci, C_GATE // LANES)),
                  pl.BlockSpec((1, LANES), lambda bi, ci: (0, 0)),
                  pl.BlockSpec((1, D_MLSTM), lambda bi, ci: (0, 0)),
                  st4, st3, st3],
        out_specs=[pl.BlockSpec((None, lin, D_MLSTM), lambda bi, ci: (bi, ci, 0)), st4, st3, st3],
        out_shape=[jax.ShapeDtypeStruct((b, tz, D_MLSTM), F32),
                   jax.ShapeDtypeStruct((b, MLSTM_HEADS, HEAD_DIM, HEAD_DIM), F32),
                   jax.ShapeDtypeStruct((b, MLSTM_HEADS, LANES), F32),
                   jax.ShapeDtypeStruct((b, MLSTM_HEADS, LANES), F32)],
        compiler_params=_cparams("parallel", "arbitrary"),
        name="mlstm",
    )(z, z, z, z, z, gate_b, norm_w, c0, n0, m0)


S5_IN_TILES = S5_CH // 256
S5_IN_FEATS = D_S5 // S5_IN_TILES
S5_OUT_TILES = D_S5 // 256
S5_OUT_CH = S5_CH // S5_OUT_TILES


S5_LT = S5_CH // LANES


def _lane_tiles(x):
    return jnp.stack([x[:, k * LANES:(k + 1) * LANES] for k in range(x.shape[1] // LANES)])


def _lane_untile(x):
    return jnp.concatenate([x[k] for k in range(x.shape[0])], axis=1)


def _s5_kernel(u_ref, wb_ref, wc_ref, lr_ref, li_ref, d_ref, gw_ref, gb_ref, x0r_ref, x0i_ref,
               o_ref, xr_ref, xi_ref, sr_ref, si_ref, y_ref, *, steps):
    nb, lc, _ = u_ref.shape

    @pl.when(pl.program_id(0) == 0)
    def _():
        xr_ref[...] = x0r_ref[...]
        xi_ref[...] = x0i_ref[...]

    for b in range(nb):
        ub = u_ref[b].astype(BF16)
        for c in range(S5_IN_TILES):
            bu = _dot(ub[:, c * S5_IN_FEATS:(c + 1) * S5_IN_FEATS], wb_ref[c])
            for k in range(256 // LANES):
                rows = pl.ds(b, lc, stride=nb)
                sr_ref.at[c * (256 // LANES) + k][rows, :] = bu[:, k * LANES:(k + 1) * LANES]
                si_ref.at[c * (256 // LANES) + k][rows, :] = bu[:, 256 + k * LANES:256 + (k + 1) * LANES]
    lam_r = _lane_tiles(jnp.broadcast_to(lr_ref[...], (SUBLANES, S5_CH)))
    lam_i = _lane_tiles(jnp.broadcast_to(li_ref[...], (SUBLANES, S5_CH)))
    row8 = lax.broadcasted_iota(jnp.int32, (1, SUBLANES, 1), 1)
    per_group = SUBLANES // nb

    def group(i, carry):
        xr, xi = carry
        r0 = pl.multiple_of(i * SUBLANES, SUBLANES)
        br = sr_ref[:, pl.ds(r0, SUBLANES), :]
        bi = si_ref[:, pl.ds(r0, SUBLANES), :]
        out_r, out_i = br, bi
        for k in range(per_group):
            nr = lam_r * xr - lam_i * xi + br
            ni = lam_r * xi + lam_i * xr + bi
            here = (row8 >= k * nb) & (row8 < (k + 1) * nb)
            out_r = jnp.where(here, nr, out_r)
            out_i = jnp.where(here, ni, out_i)
            if per_group > 1:
                xr = pltpu.roll(nr, nb, 1)
                xi = pltpu.roll(ni, nb, 1)
            else:
                xr, xi = nr, ni
        sr_ref[:, pl.ds(r0, SUBLANES), :] = out_r
        si_ref[:, pl.ds(r0, SUBLANES), :] = out_i
        return xr, xi

    xr, xi = lax.fori_loop(0, steps * nb // SUBLANES, group, (_lane_tiles(xr_ref[...]), _lane_tiles(xi_ref[...])))
    xr_ref[...] = _lane_untile(xr)
    xi_ref[...] = _lane_untile(xi)
    per_out = S5_OUT_CH // LANES
    for j in range(S5_OUT_TILES):
        lhs_r = jnp.concatenate([sr_ref[j * per_out + k] for k in range(per_out)], axis=1).astype(BF16)
        lhs_i = jnp.concatenate([si_ref[j * per_out + k] for k in range(per_out)], axis=1).astype(BF16)
        y = _dot(lhs_r, wc_ref[0, j]) + _dot(lhs_i, wc_ref[1, j])
        for k in range(256 // LANES):
            y_ref[j * (256 // LANES) + k] = y[:, k * LANES:(k + 1) * LANES]
    for b in range(nb):
        rows = pl.ds(b, lc, stride=nb)
        y = jnp.concatenate([y_ref.at[k][rows, :] for k in range(D_S5 // LANES)], axis=1) + d_ref[...] * u_ref[b]
        zz = jax.nn.gelu(y)
        o_ref[b] = zz * jax.nn.sigmoid(_dot(zz.astype(BF16), gw_ref[...]) + gb_ref[...])


def _s5(z3, wb, wc, lam_r, lam_i, d, glu_w, glu_b, x0r, x0i, *, lc, steps):
    b, tz, _ = z3.shape
    lc = min(lc, tz)
    steps = min(steps, lc)
    assert steps == lc or tz == lc
    full = lambda a: pl.BlockSpec(a.shape, lambda i: (0,) * a.ndim)
    st = pl.BlockSpec((SUBLANES, S5_CH), lambda i: (0, 0))
    return pl.pallas_call(
        functools.partial(_s5_kernel, steps=steps),
        grid=(tz // lc,),
        in_specs=[pl.BlockSpec((b, lc, D_S5), lambda i: (0, i, C_SU // D_S5)),
                  full(wb), full(wc), full(lam_r), full(lam_i), full(d), full(glu_w), full(glu_b), st, st],
        out_specs=[pl.BlockSpec((b, lc, D_S5), lambda i: (0, i, 0)), st, st],
        out_shape=[jax.ShapeDtypeStruct((b, tz, D_S5), F32),
                   jax.ShapeDtypeStruct((SUBLANES, S5_CH), F32),
                   jax.ShapeDtypeStruct((SUBLANES, S5_CH), F32)],
        scratch_shapes=[pltpu.VMEM((S5_LT, lc * b, LANES), F32), pltpu.VMEM((S5_LT, lc * b, LANES), F32),
                        pltpu.VMEM((D_S5 // LANES, lc * b, LANES), F32)],
        compiler_params=_cparams("arbitrary"),
        name="s5",
    )(z3, wb, wc, lam_r, lam_i, d, glu_w, glu_b, x0r, x0i)


N_CMB = 2 * KV_HEADS


def _cmp_project(rows_of, w_ref, o_ref):
    nch = o_ref.shape[0]
    for cmb in range(N_CMB):
        slot = cmb // KV_HEADS
        rows_ref = rows_of(cmb)
        acc = jnp.zeros((nch, 2 * HEAD_DIM), F32)
        for l in range(CMP_STRIDE):
            x = rows_ref[pl.ds(l, nch, stride=CMP_STRIDE), :]
            acc += _dot(x.astype(BF16), w_ref[slot, l])
        o_ref[:, cmb * 2 * HEAD_DIM:(cmb + 1) * 2 * HEAD_DIM] = acc


def _cmp_p_kernel(r0_ref, r1_ref, r2_ref, r3_ref, w_ref, o_ref):
    rows = (r0_ref, r1_ref, r2_ref, r3_ref)
    _cmp_project(lambda cmb: rows[cmb], w_ref, o_ref)


def _cmp_project_prompt(z, wcat):
    b, t, _ = z.shape
    nch = t // CMP_STRIDE
    rows = lambda cmb: pl.BlockSpec((None, t, HEAD_DIM), lambda bi: (bi, 0, C_NKV // HEAD_DIM + cmb))
    return pl.pallas_call(
        _cmp_p_kernel,
        grid=(b,),
        in_specs=[rows(cmb) for cmb in range(N_CMB)] + [pl.BlockSpec(wcat.shape, lambda bi: (0, 0, 0, 0))],
        out_specs=pl.BlockSpec((None, nch, N_CMB * 2 * HEAD_DIM), lambda bi: (bi, 0, 0)),
        out_shape=jax.ShapeDtypeStruct((b, nch, N_CMB * 2 * HEAD_DIM), F32),
        compiler_params=_cparams("parallel"),
        name="cmp_project_prompt",
    )(z, z, z, z, wcat)


def _cmp_s_kernel(pt_ref, *refs):
    del pt_ref
    page_refs = [r.reshape(PAGE_SIZE * KV_HEADS, HEAD_DIM) for r in refs[:2 * CMP_PAGES]]
    w_ref, o_ref = refs[2 * CMP_PAGES:]
    per_page = PAGE_SIZE // CMP_STRIDE
    for slot in range(2):
        for g in range(KV_HEADS):
            cmb = slot * KV_HEADS + g
            acc = jnp.zeros((o_ref.shape[0], 2 * HEAD_DIM), F32)
            for l in range(CMP_STRIDE):
                x = jnp.concatenate(
                    [page_refs[slot * CMP_PAGES + p][pl.ds(l * KV_HEADS + g, per_page, stride=CMP_STRIDE * KV_HEADS), :]
                     for p in range(CMP_PAGES)], axis=0)
                acc += _dot(x.astype(BF16), w_ref[slot, l])
            o_ref[:, cmb * 2 * HEAD_DIM:(cmb + 1) * 2 * HEAD_DIM] = acc


def _page_spec(layer, slot, p, pages_per_step, grid_rank):
    def index_map(*idx):
        bi, si, pt = idx[0], idx[grid_rank - 1], idx[grid_rank]
        return (layer, pt[bi, si * pages_per_step + p], 0, slot, 0, 0)

    return pl.BlockSpec((None, None, PAGE_SIZE, None, KV_HEADS, HEAD_DIM), index_map)


def _cmp_project_sample(cache6, page_table, layer, wcat):
    b, npages = page_table.shape
    steps = npages // CMP_PAGES
    nch = CMP_PAGES * PAGE_SIZE // CMP_STRIDE
    pages = [_page_spec(layer, slot, p, CMP_PAGES, 2) for slot in range(2) for p in range(CMP_PAGES)]

    grid_spec = pltpu.PrefetchScalarGridSpec(
        num_scalar_prefetch=1,
        grid=(b, steps),
        in_specs=pages + [pl.BlockSpec(wcat.shape, lambda bi, si, pt: (0, 0, 0, 0))],
        out_specs=pl.BlockSpec((None, nch, N_CMB * 2 * HEAD_DIM), lambda bi, si, pt: (bi, si, 0)),
    )
    return pl.pallas_call(
        _cmp_s_kernel,
        grid_spec=grid_spec,
        out_shape=jax.ShapeDtypeStruct((b, steps * nch, N_CMB * 2 * HEAD_DIM), F32),
        compiler_params=_cparams("parallel", "arbitrary"),
        name="cmp_project_sample",
    )(page_table, *([cache6] * (2 * CMP_PAGES)), wcat)


def _cmp_fin_kernel(p_ref, b1_ref, w2_ref, o_ref):
    nch = p_ref.shape[0]
    for cmb in range(N_CMB):
        slot = cmb // KV_HEADS
        c0 = cmb * 2 * HEAD_DIM
        first = p_ref[:, c0:c0 + HEAD_DIM]
        second = pltpu.roll(p_ref[:, c0 + HEAD_DIM:c0 + 2 * HEAD_DIM], nch - 1, 0)
        hid = b1_ref[slot:slot + 1, :] + first + second
        o_ref[cmb] = _dot(jax.nn.gelu(hid).astype(BF16), w2_ref[slot])


def _cmp_finish(p, b1, w2):
    b, nch, _ = p.shape
    return pl.pallas_call(
        _cmp_fin_kernel,
        grid=(b,),
        in_specs=[pl.BlockSpec((None, nch, p.shape[2]), lambda bi: (bi, 0, 0)),
                  pl.BlockSpec(b1.shape, lambda bi: (0, 0)),
                  pl.BlockSpec(w2.shape, lambda bi: (0, 0, 0))],
        out_specs=pl.BlockSpec((None, N_CMB, nch, HEAD_DIM), lambda bi: (bi, 0, 0, 0)),
        out_shape=jax.ShapeDtypeStruct((b, N_CMB, nch, HEAD_DIM), F32),
        compiler_params=_cparams("parallel"),
        name="cmp_finish",
    )(p, b1, w2)


def _softmax_init(m_scr, l_scr, acc_scr):
    m_scr[...] = jnp.full(m_scr.shape, M_INIT, F32)
    l_scr[...] = jnp.zeros(l_scr.shape, F32)
    acc_scr[...] = jnp.zeros(acc_scr.shape, F32)


def _softmax_update(s, mask, vb, m_scr, l_scr, acc_scr):
    s = jnp.where(mask, s, NEG_INF)
    m_prev = m_scr[...]
    m_new = jnp.maximum(m_prev, jnp.max(s, axis=1, keepdims=True))
    alpha = jnp.exp(m_prev - m_new)
    p = jnp.exp(s - m_new)
    l_scr[...] = alpha * l_scr[...] + jnp.sum(p, axis=1, keepdims=True)
    acc_scr[...] = alpha * acc_scr[...] + _dot(p.astype(BF16), vb)
    m_scr[...] = m_new


def _softmax_result(l_scr, acc_scr):
    l = l_scr[...]
    return acc_scr[...] / jnp.where(l > 0, l, 1.0)


def _masked_probs(s, mask):
    s = jnp.where(mask, s, NEG_INF)
    m = jnp.max(s, axis=1, keepdims=True)
    m = jnp.where(m > NEG_INF, m, 0.0)
    p = jnp.exp(s - m)
    den = jnp.sum(p, axis=1, keepdims=True)
    return p / jnp.where(den > 0, den, 1.0)


def _stack_heads(q):
    return jnp.concatenate([q[:, j * HEAD_DIM:(j + 1) * HEAD_DIM] for j in range(GROUP)], axis=0)


def _block_importance(pc, n_cmp, n_sel):
    ncp = pc.shape[1]
    nsp = -(-n_sel // LANES) * LANES
    c_start = lax.broadcasted_iota(jnp.int32, (ncp, nsp), 0) * CMP_STRIDE
    s_start = lax.broadcasted_iota(jnp.int32, (ncp, nsp), 1) * SEL_BLOCK
    overlap = ((c_start < s_start + SEL_BLOCK) & (c_start + CMP_BLOCK > s_start)
               & (c_start < n_cmp * CMP_STRIDE)).astype(F32)
    return jnp.dot(pc, overlap, preferred_element_type=F32, precision=lax.Precision.HIGHEST)


def _force_blocks(imp, blk, cur):
    forced = (blk == 0) | (blk == cur) | (blk == cur - 1)
    imp = jnp.where(forced, FORCE_SCORE, imp)
    return jnp.where(blk > cur, -1.0, imp)


def _select_blocks_cols(pc, cur, n_cmp, n_sel):
    imp = _block_importance(pc, n_cmp, n_sel)
    blk = lax.broadcasted_iota(jnp.int32, (1, imp.shape[1]), 1)
    imp = _force_blocks(imp, blk, cur)

    def body(sp, count):
        col = jnp.sum(jnp.where(blk == sp, imp, 0.0), axis=1, keepdims=True)
        ahead = (col > imp) | ((col == imp) & (sp < blk))
        return count + ahead.astype(F32)

    count = lax.fori_loop(0, n_sel, body, jnp.zeros(imp.shape, F32), unroll=8)
    return ((count < min(SEL_TOPK, n_sel)) & (blk < n_sel)).astype(F32)


def _select_blocks_rows(pc, cur_row, n_cmp, n_sel):
    rows = pc.shape[0]
    nrow = -(-n_sel // SUBLANES) * SUBLANES
    imp_t = _block_importance(pc, n_cmp, n_sel).T[0:nrow, :]
    blk = lax.broadcasted_iota(jnp.int32, (nrow, 1), 0)
    imp_t = _force_blocks(imp_t, blk, cur_row)
    count = jnp.zeros(imp_t.shape, F32)
    for sp in range(n_sel):
        row = imp_t[sp:sp + 1, :]
        count += ((row > imp_t) | ((row == imp_t) & (sp < blk))).astype(F32)
    sel_t = ((count < min(SEL_TOPK, n_sel)) & (blk < n_sel)).astype(F32)
    sel_t = jnp.concatenate([sel_t, jnp.zeros((LANES - nrow, rows), F32)], axis=0)
    return sel_t.T


def _gate(gt, col):
    lane = lax.broadcasted_iota(jnp.int32, (1, LANES), 1)
    return jax.nn.sigmoid(jnp.sum(jnp.where(lane == col, gt, 0.0), axis=1, keepdims=True))


def _nsa_prompt_kernel(q_ref, ks_ref, vs_ref, kw_ref, vw_ref, kc_ref, vc_ref, bc_ref, tz_ref, gt_ref, gb_ref,
                       o_ref, s_scr, selb_scr, mx_scr, l_scr, acc_scr, *, t_len):
    g = pl.program_id(1)
    qt = pl.program_id(2)
    q0 = qt * TQ
    n_cmp = t_len // CMP_STRIDE - (CMP_BLOCK // CMP_STRIDE) + 1
    n_sel = t_len // SEL_BLOCK
    ncp = kc_ref.shape[0]
    rows4 = GROUP * TQ
    qs = _stack_heads(q_ref[...] * QK_SCALE).astype(BF16)
    qpos = q0 + (lax.broadcasted_iota(jnp.int32, (rows4, 1), 0) & (TQ - 1))

    n_idx = lax.broadcasted_iota(jnp.int32, (1, ncp), 1)
    s_c = _dot_nt(qs, kc_ref[...].astype(BF16)) + bc_ref[...].reshape(rows4, ncp)
    cmask = (qpos - (n_idx * CMP_STRIDE + CMP_BLOCK - 1) >= 0) & (n_idx < n_cmp)
    p_c = _masked_probs(s_c, cmask)
    o_c = _dot(p_c.astype(BF16), vc_ref[...].astype(BF16))
    pc = p_c[0:TQ] + p_c[TQ:2 * TQ] + p_c[2 * TQ:3 * TQ] + p_c[3 * TQ:4 * TQ]
    cur_row = jnp.right_shift(q0 + lax.broadcasted_iota(jnp.int32, (1, TQ), 1), SEL_SHIFT)
    sel = _select_blocks_rows(pc, cur_row, n_cmp, n_sel)
    sel_add = ((sel - 1.0) * -MASKED).astype(BF16)

    kt_hi = (q0 + TQ - 1) // TK + 1
    kt_far = jnp.maximum(q0 - (FAR_DIST - 1), 0) // TK
    kidx = lax.broadcasted_iota(jnp.int32, (1, TK), 1)
    srow = lax.broadcasted_iota(jnp.int32, (LANES, TK), 0)

    def spread_selection(kt, carry):
        k0 = kt * TK
        expand = (jnp.right_shift(k0 + lax.broadcasted_iota(jnp.int32, (LANES, TK), 1), SEL_SHIFT) == srow).astype(BF16)
        selb_scr[kt] = _dot(sel_add, expand)
        return carry

    lax.fori_loop(0, kt_hi, spread_selection, 0)

    def tile_bias(k0):
        return tz_ref[jnp.clip((q0 - k0) // TQ, 0, N_TZ - 1)].reshape(rows4, TK)

    def attend(k_ref, v_ref, kt_lo, mask_tile):
        mx_scr[...] = jnp.full(mx_scr.shape, NEG_INF, F32)

        def scores(near):
            def body(kt, carry):
                k0 = pl.multiple_of(kt * TK, TK)
                s = _dot_nt(qs, k_ref[pl.ds(k0, TK), :].astype(BF16))
                if near:
                    s = s + tile_bias(k0)
                s = mask_tile(k0, s, near)
                s_scr[kt] = s
                mx_scr[...] = jnp.maximum(mx_scr[...], jnp.maximum(s[:, 0:LANES], s[:, LANES:TK]))
                return carry
            return body

        kt_mid = jnp.maximum(kt_far, kt_lo)
        lax.fori_loop(kt_lo, kt_mid, scores(False), 0)
        lax.fori_loop(kt_mid, kt_hi, scores(True), 0)
        m = jnp.max(mx_scr[...], axis=1, keepdims=True)
        mx_scr[...] = jnp.broadcast_to(jnp.where(m > NEG_INF, m, 0.0), mx_scr.shape)
        l_scr[...] = jnp.zeros(l_scr.shape, F32)
        acc_scr[...] = jnp.zeros(acc_scr.shape, F32)

        def probs(kt, carry):
            k0 = pl.multiple_of(kt * TK, TK)
            m_rep = mx_scr[...]
            p = jnp.exp(s_scr[kt] - jnp.concatenate([m_rep, m_rep], axis=1))
            l_scr[...] += p[:, 0:LANES] + p[:, LANES:TK]
            acc_scr[...] += _dot(p.astype(BF16), v_ref[pl.ds(k0, TK), :].astype(BF16))
            return carry

        lax.fori_loop(kt_lo, kt_hi, probs, 0)
        l = jnp.sum(l_scr[...], axis=1, keepdims=True)
        return acc_scr[...] / jnp.where(l > 0, l, 1.0)

    def sel_mask(k0, s, near):
        s = (s.reshape(GROUP, TQ, TK) + selb_scr[k0 // TK][None]).reshape(rows4, TK)
        return jnp.where(k0 + kidx <= qpos, s, NEG_INF) if near else s

    o_s = attend(ks_ref, vs_ref, 0, sel_mask)

    def win_mask(k0, s, near):
        dist = qpos - (k0 + kidx)
        return jnp.where((dist >= 0) & (dist <= WINDOW), s, NEG_INF)

    o_w = attend(kw_ref, vw_ref, jnp.maximum(q0 - WINDOW, 0) // TK, win_mask)

    gt = gt_ref[...] + gb_ref[...]
    for j in range(GROUP):
        col = GATE_NG + (g * GROUP + j) * 3
        r = slice(j * TQ, (j + 1) * TQ)
        o_ref[:, j * HEAD_DIM:(j + 1) * HEAD_DIM] = (
            _gate(gt, col) * o_c[r] + _gate(gt, col + 1) * o_s[r] + _gate(gt, col + 2) * o_w[r])


def _nsa_prompt(z, kcvc, bias_c, tz, gate_b):
    b, t, _ = z.shape
    ncp = kcvc.shape[2]
    nq = t // TQ
    kv = lambda slot: pl.BlockSpec((None, t, HEAD_DIM), lambda bi, g, qi: (bi, 0, C_NKV // HEAD_DIM + slot * KV_HEADS + g))
    cmp_blk = lambda slot: pl.BlockSpec((None, None, ncp, HEAD_DIM), lambda bi, g, qi: (bi, slot * KV_HEADS + g, 0, 0))
    return pl.pallas_call(
        functools.partial(_nsa_prompt_kernel, t_len=t),
        grid=(b, KV_HEADS, nq),
        in_specs=[pl.BlockSpec((None, TQ, GROUP * HEAD_DIM), lambda bi, g, qi: (bi, qi, C_NQ // (GROUP * HEAD_DIM) + g)),
                  kv(2), kv(3), kv(4), kv(5), cmp_blk(0), cmp_blk(1),
                  pl.BlockSpec((GROUP, TQ, ncp), lambda bi, g, qi: (g, qi, 0)),
                  pl.BlockSpec((N_TZ, GROUP, TQ, TK), lambda bi, g, qi: (0, g, 0, 0)),
                  pl.BlockSpec((None, TQ, LANES), lambda bi, g, qi: (bi, qi, C_GATE // LANES)),
                  pl.BlockSpec((1, LANES), lambda bi, g, qi: (0, 0))],
        out_specs=pl.BlockSpec((None, TQ, GROUP * HEAD_DIM), lambda bi, g, qi: (bi, qi, g)),
        out_shape=jax.ShapeDtypeStruct((b, t, D_NSA), F32),
        scratch_shapes=[pltpu.VMEM((t // TK, GROUP * TQ, TK), F32), pltpu.VMEM((t // TK, TQ, TK), F32),
                        pltpu.VMEM((GROUP * TQ, LANES), F32),
                        pltpu.VMEM((GROUP * TQ, LANES), F32), pltpu.VMEM((GROUP * TQ, HEAD_DIM), F32)],
        compiler_params=_cparams("parallel", "parallel", "arbitrary"),
        name="nsa_prompt",
    )(z, z, z, z, z, kcvc, kcvc, bias_c, tz, z, gate_b)


TS = SUBLANES
ROWS_S = GROUP * TS


def _nsa_s_cmp_kernel(q_ref, kc_ref, vc_ref, bc_ref, oc_ref, sel_ref, *, pos0, n_cmp, n_sel):
    nch = kc_ref.shape[0]
    qs = _stack_heads(q_ref[...] * QK_SCALE).astype(BF16)
    qpos = pos0 + (lax.broadcasted_iota(jnp.int32, (ROWS_S, 1), 0) & (TS - 1))
    n_idx = lax.broadcasted_iota(jnp.int32, (1, nch), 1)
    s_c = _dot_nt(qs, kc_ref[...].astype(BF16)) + bc_ref[...].reshape(ROWS_S, nch)
    cmask = (qpos - (n_idx * CMP_STRIDE + CMP_BLOCK - 1) >= 0) & (n_idx < n_cmp)
    p_c = _masked_probs(s_c, cmask)
    oc_ref[...] = _dot(p_c.astype(BF16), vc_ref[...].astype(BF16))
    pc = p_c[0:TS] + p_c[TS:2 * TS] + p_c[2 * TS:3 * TS] + p_c[3 * TS:4 * TS]
    sel_ref[...] = _select_blocks_cols(pc, jnp.right_shift(qpos[0:TS], SEL_SHIFT), n_cmp, n_sel)


def _nsa_sample_cmp(z8, kcvc, bias_c, *, pos0, n_cmp, n_sel):
    b = z8.shape[0]
    nch = kcvc.shape[2]
    nsp = -(-n_sel // LANES) * LANES
    cmp_blk = lambda slot: pl.BlockSpec((None, None, nch, HEAD_DIM), lambda bi, g: (bi, slot * KV_HEADS + g, 0, 0))
    return pl.pallas_call(
        functools.partial(_nsa_s_cmp_kernel, pos0=pos0, n_cmp=n_cmp, n_sel=n_sel),
        grid=(b, KV_HEADS),
        in_specs=[pl.BlockSpec((None, TS, GROUP * HEAD_DIM), lambda bi, g: (bi, 0, C_NQ // (GROUP * HEAD_DIM) + g)),
                  cmp_blk(0), cmp_blk(1),
                  pl.BlockSpec((GROUP, TS, nch), lambda bi, g: (g, 0, 0))],
        out_specs=[pl.BlockSpec((None, None, ROWS_S, HEAD_DIM), lambda bi, g: (bi, g, 0, 0)),
                   pl.BlockSpec((None, None, TS, nsp), lambda bi, g: (bi, g, 0, 0))],
        out_shape=[jax.ShapeDtypeStruct((b, KV_HEADS, ROWS_S, HEAD_DIM), F32),
                   jax.ShapeDtypeStruct((b, KV_HEADS, TS, nsp), F32)],
        compiler_params=_cparams("parallel", "parallel"),
        name="nsa_sample_cmp",
    )(z8, kcvc, kcvc, bias_c)


def _pad_rows(x, n):
    return jnp.concatenate([x, jnp.zeros((n - x.shape[0], x.shape[1]), x.dtype)], axis=0)


def _nsa_s_sel_kernel(pt_ref, q_ref, sel_ref, kn_ref, vn_ref, bt_ref, bn_ref, bf_ref, *refs, pos0, npages):
    del pt_ref
    rows_pp = PAGE_SIZE * KV_HEADS
    k_refs = [r.reshape(rows_pp, HEAD_DIM) for r in refs[:SEL_PAGES]]
    v_refs = [r.reshape(rows_pp, HEAD_DIM) for r in refs[SEL_PAGES:2 * SEL_PAGES]]
    o_ref, kbuf, vbuf, m_scr, l_scr, acc_scr = refs[2 * SEL_PAGES:]
    step = pl.program_id(1)
    last = step == npages // SEL_PAGES - 1
    rows = KV_HEADS * ROWS_S
    ncol = SEL_TILE * rows_pp
    head_shift = KV_HEADS.bit_length() - 1
    blk_per_page = PAGE_SIZE // SEL_BLOCK

    @pl.when(step == 0)
    def _():
        _softmax_init(m_scr, l_scr, acc_scr)

    nsp = sel_ref.shape[-1]
    ridx = lax.broadcasted_iota(jnp.int32, (rows, 1), 0)
    qpos = pos0 + (ridx & (TS - 1))
    qs = jnp.concatenate([_stack_heads(q_ref[:, g * GROUP * HEAD_DIM:(g + 1) * GROUP * HEAD_DIM] * QK_SCALE)
                          for g in range(KV_HEADS)], axis=0).astype(BF16)
    sel_rows = jnp.concatenate([sel_ref[g] for g in range(KV_HEADS) for _ in range(GROUP)], axis=0)
    bias_far = bf_ref[...].reshape(rows, PAGE_SIZE)[:, 0:1]
    col = lax.broadcasted_iota(jnp.int32, (1, ncol), 1)
    own_head = (col & (KV_HEADS - 1)) == ridx // ROWS_S
    expand = (jnp.right_shift(lax.broadcasted_iota(jnp.int32, (LANES, ncol), 1), SEL_SHIFT + head_shift)
              == lax.broadcasted_iota(jnp.int32, (LANES, ncol), 0)).astype(BF16)
    blk_r = lax.broadcasted_iota(jnp.int32, (nsp, LANES), 0)
    blk_c = lax.broadcasted_iota(jnp.int32, (nsp, LANES), 1)
    n_tiles = SEL_PAGES // SEL_TILE

    for tile in range(n_tiles):
        for i in range(SEL_TILE):
            p = tile * SEL_TILE + i
            kbuf[i * rows_pp:(i + 1) * rows_pp, :] = k_refs[p][...].astype(BF16)
            vbuf[i * rows_pp:(i + 1) * rows_pp, :] = v_refs[p][...].astype(BF16)
        page0 = step * SEL_PAGES + tile * SEL_TILE
        window = ((blk_r == page0 * blk_per_page + blk_c) & (blk_c < SEL_TILE * blk_per_page)).astype(BF16)
        chosen = _dot(_dot(sel_rows.astype(BF16), window).astype(BF16), expand) > 0.5
        s = _dot_nt(qs, kbuf[...]) + bias_far
        if tile == n_tiles - 1:
            tail = jnp.where(last, bt_ref[...].reshape(rows, rows_pp) - bias_far, 0.0)
            s = s + jnp.concatenate([jnp.zeros((rows, ncol - rows_pp), F32), tail], axis=1)
        key = page0 * PAGE_SIZE + jnp.right_shift(col, head_shift)
        _softmax_update(s, chosen & own_head & (key <= qpos), vbuf[...], m_scr, l_scr, acc_scr)

    @pl.when(last)
    def _():
        kidx = lax.broadcasted_iota(jnp.int32, (1, PAGE_SIZE), 1)
        new_blk = lax.broadcasted_iota(jnp.int32, (1, nsp), 1) == npages * blk_per_page
        flag = jnp.sum(jnp.where(new_blk, sel_rows, 0.0), axis=1, keepdims=True) > 0.5
        bias_new = bn_ref[...].reshape(rows, PAGE_SIZE)
        for g in range(KV_HEADS):
            r = slice(g * ROWS_S, (g + 1) * ROWS_S)
            kn = _pad_rows(kn_ref[:, g * HEAD_DIM:(g + 1) * HEAD_DIM], PAGE_SIZE).astype(BF16)
            vn = _pad_rows(vn_ref[:, g * HEAD_DIM:(g + 1) * HEAD_DIM], PAGE_SIZE).astype(BF16)
            mask = flag[r] & (pos0 + kidx <= qpos[r]) & (kidx < TS)
            _softmax_update(_dot_nt(qs[r], kn) + bias_new[r], mask, vn, m_scr.at[r], l_scr.at[r], acc_scr.at[r])
            o_ref[g] = _softmax_result(l_scr.at[r], acc_scr.at[r])


def _nsa_sample_sel(z8, sel, bias_tail, bias_new, bias_far, cache6, page_table, layer, *, pos0):
    b, npages = page_table.shape
    nsp = sel.shape[-1]
    steps = npages // SEL_PAGES
    slot_cols = KV_HEADS * HEAD_DIM
    new_rows = lambda slot: pl.BlockSpec((None, TS, slot_cols),
                                         lambda bi, si, pt: (bi, 0, (C_NKV + slot * slot_cols) // slot_cols))
    whole = lambda a: pl.BlockSpec(a.shape, lambda bi, si, pt: (0,) * a.ndim)
    grid_spec = pltpu.PrefetchScalarGridSpec(
        num_scalar_prefetch=1,
        grid=(b, steps),
        in_specs=[pl.BlockSpec((None, TS, D_NSA), lambda bi, si, pt: (bi, 0, C_NQ // D_NSA)),
                  pl.BlockSpec((None, KV_HEADS, TS, nsp), lambda bi, si, pt: (bi, 0, 0, 0)),
                  new_rows(2), new_rows(3), whole(bias_tail), whole(bias_new), whole(bias_far)]
                 + [_page_spec(layer, 2, p, SEL_PAGES, 2) for p in range(SEL_PAGES)]
                 + [_page_spec(layer, 3, p, SEL_PAGES, 2) for p in range(SEL_PAGES)],
        out_specs=pl.BlockSpec((None, KV_HEADS, ROWS_S, HEAD_DIM), lambda bi, si, pt: (bi, 0, 0, 0)),
        scratch_shapes=[pltpu.VMEM((SEL_TILE * PAGE_SIZE * KV_HEADS, HEAD_DIM), BF16),
                        pltpu.VMEM((SEL_TILE * PAGE_SIZE * KV_HEADS, HEAD_DIM), BF16),
                        pltpu.VMEM((KV_HEADS * ROWS_S, 1), F32), pltpu.VMEM((KV_HEADS * ROWS_S, 1), F32),
                        pltpu.VMEM((KV_HEADS * ROWS_S, HEAD_DIM), F32)],
    )
    return pl.pallas_call(
        functools.partial(_nsa_s_sel_kernel, pos0=pos0, npages=npages),
        grid_spec=grid_spec,
        out_shape=jax.ShapeDtypeStruct((b, KV_HEADS, ROWS_S, HEAD_DIM), F32),
        compiler_params=_cparams("parallel", "arbitrary"),
        name="nsa_sample_sel",
    )(page_table, z8, sel, z8, z8, bias_tail, bias_new, bias_far, *([cache6] * (2 * SEL_PAGES)))


def _nsa_s_win_kernel(q_ref, kw_ref, vw_ref, kn_ref, vn_ref, bw_ref, oc_ref, os_ref, gt_ref, gb_ref, o_ref,
                      m_scr, l_scr, acc_scr, *, pos0):
    g = pl.program_id(1)
    wb = kw_ref.shape[0]
    own = pl.ds(g, wb, stride=KV_HEADS)
    kw = kw_ref.reshape(wb * KV_HEADS, HEAD_DIM)[own, :]
    vw = vw_ref.reshape(wb * KV_HEADS, HEAD_DIM)[own, :]
    qs = _stack_heads(q_ref[...] * QK_SCALE).astype(BF16)
    qpos = pos0 + (lax.broadcasted_iota(jnp.int32, (ROWS_S, 1), 0) & (TS - 1))
    bias = bw_ref[...].reshape(ROWS_S, wb + PAGE_SIZE)
    _softmax_init(m_scr, l_scr, acc_scr)
    dist = qpos - (pos0 - wb + lax.broadcasted_iota(jnp.int32, (1, wb), 1))
    s = _dot_nt(qs, kw.astype(BF16)) + bias[:, 0:wb]
    _softmax_update(s, (dist >= 0) & (dist <= WINDOW), vw.astype(BF16), m_scr, l_scr, acc_scr)
    kidx = lax.broadcasted_iota(jnp.int32, (1, PAGE_SIZE), 1)
    dist = qpos - (pos0 + kidx)
    s = _dot_nt(qs, _pad_rows(kn_ref[...], PAGE_SIZE).astype(BF16)) + bias[:, wb:wb + PAGE_SIZE]
    _softmax_update(s, (dist >= 0) & (dist <= WINDOW) & (kidx < TS), _pad_rows(vn_ref[...], PAGE_SIZE).astype(BF16),
                    m_scr, l_scr, acc_scr)
    o_w = _softmax_result(l_scr, acc_scr)
    o_c = oc_ref[...]
    o_s = os_ref[...]
    gt = gt_ref[...] + gb_ref[...]
    for j in range(GROUP):
        col = GATE_NG + (g * GROUP + j) * 3
        r = slice(j * TS, (j + 1) * TS)
        o_ref[:, j * HEAD_DIM:(j + 1) * HEAD_DIM] = (
            _gate(gt, col) * o_c[r] + _gate(gt, col + 1) * o_s[r] + _gate(gt, col + 2) * o_w[r])


def _nsa_sample_win(z8, win6, layer, bias_w, o_c, o_s, gate_b, *, pos0):
    _, b, wb = win6.shape[:3]
    cached = lambda slot: pl.BlockSpec((None, None, wb, None, KV_HEADS, HEAD_DIM),
                                       lambda bi, g: (layer, bi, 0, slot, 0, 0))
    zcol = lambda col: (lambda bi, g: (bi, 0, col // HEAD_DIM + g))
    part = pl.BlockSpec((None, None, ROWS_S, HEAD_DIM), lambda bi, g: (bi, g, 0, 0))
    return pl.pallas_call(
        functools.partial(_nsa_s_win_kernel, pos0=pos0),
        grid=(b, KV_HEADS),
        in_specs=[pl.BlockSpec((None, TS, GROUP * HEAD_DIM), lambda bi, g: (bi, 0, C_NQ // (GROUP * HEAD_DIM) + g)),
                  cached(0), cached(1),
                  pl.BlockSpec((None, TS, HEAD_DIM), zcol(C_NKV + 4 * KV_HEADS * HEAD_DIM)),
                  pl.BlockSpec((None, TS, HEAD_DIM), zcol(C_NKV + 5 * KV_HEADS * HEAD_DIM)),
                  pl.BlockSpec((GROUP, TS, wb + PAGE_SIZE), lambda bi, g: (g, 0, 0)),
                  part, part,
                  pl.BlockSpec((None, TS, LANES), lambda bi, g: (bi, 0, C_GATE // LANES)),
                  pl.BlockSpec((1, LANES), lambda bi, g: (0, 0))],
        out_specs=pl.BlockSpec((None, TS, GROUP * HEAD_DIM), lambda bi, g: (bi, 0, g)),
        out_shape=jax.ShapeDtypeStruct((b, TS, D_NSA), F32),
        scratch_shapes=[pltpu.VMEM((ROWS_S, 1), F32), pltpu.VMEM((ROWS_S, 1), F32), pltpu.VMEM((ROWS_S, HEAD_DIM), F32)],
        compiler_params=_cparams("parallel", "parallel"),
        name="nsa_sample_win",
    )(z8, win6, win6, z8, z8, bias_w, o_c, o_s, z8, gate_b)


def _t5_bucket(dist):
    n = np.maximum(dist, 0)
    exact = NUM_BUCKETS // 2
    nf = np.maximum(n, 1).astype(np.float32)
    large = exact + (np.log(nf / np.float32(exact)) / np.float32(math.log(MAX_DISTANCE / exact))
                     * np.float32(NUM_BUCKETS - exact)).astype(np.int32)
    return np.where(n < exact, n, np.minimum(large, NUM_BUCKETS - 1)).astype(np.int32)


FAR_DIST = int(np.max(np.nonzero(_t5_bucket(np.arange(4 * MAX_DISTANCE)) < NUM_BUCKETS - 1)[0])) + 1
N_TZ = (FAR_DIST + TK - 2) // TQ + 1


def _bias_table(rel_bias, dist):
    onehot = jax.nn.one_hot(jnp.asarray(_t5_bucket(dist).astype(np.int8)), NUM_BUCKETS, dtype=F32)
    return jnp.einsum("rcb,bh->hrc", onehot, rel_bias, precision=lax.Precision.HIGHEST)


def _pack_layer(p, l):
    w_in = p["w_in"][l]
    pad = jnp.zeros((D_MODEL, N_IN - C_GATE - 32), F32)
    w_in_p = jnp.concatenate([w_in[:, 0:2048], w_in[:, 2056:3080], w_in[:, 3080:4616], w_in[:, 4640:5152],
                              w_in[:, 2048:2056], w_in[:, 4616:4640], pad], axis=1).astype(BF16)
    gb = p["mlstm_gate_b"][l]
    mlstm_gate_b = jnp.zeros((1, LANES), F32).at[0, GATE_MI:GATE_MI + 4].set(gb[0]).at[0, GATE_MF:GATE_MF + 4].set(gb[1])
    nsa_gate_b = jnp.zeros((1, LANES), F32).at[0, GATE_NG:GATE_NG + 3 * NSA_HEADS].set(p["nsa_gate_b"][l].reshape(-1))
    w1 = p["cmp_w1"][l]
    wcat = jnp.concatenate([w1[:, 0:CMP_STRIDE], w1[:, CMP_STRIDE:CMP_BLOCK]], axis=-1).astype(BF16)
    lam = lax.complex(p["s5_a_re"][l], p["s5_a_im"][l])
    lam_bar = jnp.exp(lam * jnp.exp(p["s5_log_step"][l])[:, None])
    b_bar = ((lam_bar - 1.0) / lam)[..., None] * lax.complex(p["s5_b_re"][l], p["s5_b_im"][l])
    gi = S5_GROUPS // S5_IN_TILES
    bd_in = lambda m: jnp.einsum("cgph,gk->cghkp", m.reshape(S5_IN_TILES, gi, S5_STATE, S5_GROUP_WIDTH),
                                 jnp.eye(gi, dtype=F32)).reshape(S5_IN_TILES, S5_IN_FEATS, 256)
    wb = jnp.concatenate([bd_in(b_bar.real), bd_in(b_bar.imag)], axis=2).astype(BF16)
    go = S5_GROUPS // S5_OUT_TILES
    bd_out = lambda m: jnp.einsum("jghp,gk->jgpkh", m.reshape(S5_OUT_TILES, go, S5_GROUP_WIDTH, S5_STATE),
                                  jnp.eye(go, dtype=F32)).reshape(S5_OUT_TILES, S5_OUT_CH, 256)
    wc = jnp.stack([bd_out(p["s5_c_re"][l]), -bd_out(p["s5_c_im"][l])]).astype(BF16)
    half = lambda w: jnp.pad(w, ((0, 0), (0, D_FFP - D_FF)))
    w_up = p["ffn_w_up"][l]
    conv_w = p["ffn_conv_w"][l]
    conv_b = p["ffn_conv_b"][l][None, :]
    return dict(
        w_in=w_in_p, mlstm_gate_b=mlstm_gate_b, mlstm_norm_w=p["mlstm_norm_w"][l][None, :], nsa_gate_b=nsa_gate_b,
        wcat=wcat, cmp_b1=p["cmp_b1"][l], cmp_w2=p["cmp_w2"][l].astype(BF16),
        s5_wb=wb, s5_wc=wc, s5_lam_r=lam_bar.real.reshape(1, S5_CH), s5_lam_i=lam_bar.imag.reshape(1, S5_CH),
        s5_d=p["s5_d"][l][None, :], s5_glu_w=p["s5_glu_w"][l].astype(BF16), s5_glu_b=p["s5_glu_b"][l][None, :],
        w_out=p["w_out"][l].astype(BF16), ln1_w=p["ln1_w"][l][None, :], ln1_b=p["ln1_b"][l][None, :],
        w_up=jnp.concatenate([half(w_up[:, :D_FF]), half(w_up[:, D_FF:])], axis=1).astype(BF16),
        conv_w8=jnp.pad(jnp.concatenate([half(conv_w[:, :D_FF]), half(conv_w[:, D_FF:])], axis=1),
                        ((0, SUBLANES - CONV_W), (0, 0))),
        conv_b=jnp.concatenate([half(conv_b[:, :D_FF]), half(conv_b[:, D_FF:])], axis=1),
        w_down=p["ffn_w_down"][l].astype(BF16),
        ln2_w=p["ln2_w"][l][None, :], ln2_b=p["ln2_b"][l][None, :],
    )


def _unpad_ff(x):
    return jnp.concatenate([x[..., :D_FF], x[..., D_FFP:D_FFP + D_FF]], axis=-1)


def _pad_ff(x):
    pad = [(0, 0)] * (x.ndim - 1) + [(0, D_FFP - D_FF)]
    return jnp.concatenate([jnp.pad(x[..., :D_FF], pad), jnp.pad(x[..., D_FF:], pad)], axis=-1)


def _mixer_tail(x2d, z, b, t, t_use, lw, o_mlstm, o_nsa, s5_state, conv_state8, tm):
    pad8 = lambda s: jnp.pad(s.reshape(b, S5_CH), ((0, SUBLANES - b), (0, 0)))
    o_s5, xr, xi = _s5(z.reshape(b, t, N_IN), lw["s5_wb"], lw["s5_wc"], lw["s5_lam_r"], lw["s5_lam_i"], lw["s5_d"],
                       lw["s5_glu_w"], lw["s5_glu_b"], pad8(s5_state[0]), pad8(s5_state[1]), lc=S5_LC, steps=t_use)
    o_s5 = o_s5.reshape(b * t, D_S5)
    x1 =_wout_ln(o_mlstm.reshape(b * t, D_MLSTM), o_nsa.reshape(b * t, D_NSA), o_s5, x2d,
                  lw["w_out"], lw["ln1_w"], lw["ln1_b"], tm=256)
    s5_new = (xr[:b].reshape(b, S5_GROUPS, S5_STATE), xi[:b].reshape(b, S5_GROUPS, S5_STATE))
    if conv_state8 is None:
        x2, tail_a, tail_g = _ffn_prompt(x1.reshape(b, t, D_MODEL), lw["w_up"], lw["w_down"], lw["conv_w8"],
                                         lw["conv_b"], lw["ln2_w"], lw["ln2_b"], tt=FFN_TT, tf=FFN_TF)
        keep = slice(SUBLANES - (CONV_W - 1), SUBLANES)
        conv_new = jnp.concatenate([tail_a[:, keep, :D_FF], tail_g[:, keep, :D_FF]], axis=-1)
        return x2.reshape(b * t, D_MODEL), s5_new, conv_new
    up = _matmul(x1, lw["w_up"], tm=tm, tn=512)
    hgate = _convgate(up.reshape(b, t, 2 * D_FFP), conv_state8, lw["conv_w8"], lw["conv_b"], tt=256, tf=512)
    x2 = _down_ln(hgate.reshape(b * t, D_FFP), lw["w_down"], x1, lw["ln2_w"], lw["ln2_b"], tm=512, tk=1408)
    conv_new = _unpad_ff(up.reshape(b, t, 2 * D_FFP)[:, t_use - (CONV_W - 1):t_use])
    return x2, s5_new, conv_new


def _prompt_layer(x2d, b, t, lw, rel_bias):
    z, kv_rows, win_rows = _proj_in(x2d, lw["w_in"], tm=min(1024, b * t), tn=768)
    z3 = z.reshape(b, t, N_IN)
    zeros = lambda *s: jnp.zeros(s, F32)
    o_mlstm, c1, n1, m1 = _mlstm(z3, lw["mlstm_gate_b"], lw["mlstm_norm_w"],
                                 zeros(b, MLSTM_HEADS, HEAD_DIM, HEAD_DIM), zeros(b, MLSTM_HEADS, LANES),
                                 zeros(b, MLSTM_HEADS, LANES), L=MLSTM_L, lin=MLSTM_L, t_valid=t)
    kcvc = _cmp_finish(_cmp_project_prompt(z3, lw["wcat"]), lw["cmp_b1"], lw["cmp_w2"])
    ncp = t // CMP_STRIDE
    bias_c = _bias_table(rel_bias, np.arange(t)[:, None] - (np.arange(ncp) * CMP_STRIDE + CMP_BLOCK - 1)[None, :])
    ti = np.arange(TQ)[:, None] - np.arange(TK)[None, :]
    far = _bias_table(rel_bias, np.full((1, 1), FAR_DIST))
    tz = jnp.stack([_bias_table(rel_bias, d * TQ + ti) - far for d in range(N_TZ)])
    o_nsa = _nsa_prompt(z3, kcvc, bias_c, tz, lw["nsa_gate_b"])
    x2, s5_new, conv_new = _mixer_tail(
        x2d, z, b, t, t, lw, o_mlstm, o_nsa, (zeros(b, S5_GROUPS, S5_STATE), zeros(b, S5_GROUPS, S5_STATE)),
        None, tm=1024)
    wrows = min(WINDOW, t)
    state = (kv_rows.reshape(b, t, 4, KV_HEADS, HEAD_DIM), win_rows.reshape(b, t, 2, KV_HEADS, HEAD_DIM)[:, t - wrows:],
             c1, n1, m1[:, :, 0], s5_new[0], s5_new[1], conv_new)
    return x2, state


def _sample_layer(x2d, b, tn, lw, rel_bias, layer, cache6, page_table, win6, mlstm_state, s5_state, conv_state):
    npages = page_table.shape[1]
    pos0 = npages * PAGE_SIZE
    z = _matmul(x2d, lw["w_in"], tm=1024, tn=768)
    z8 = z.reshape(b, TS, N_IN)
    c0, n0, m0 = mlstm_state
    o_mlstm, c1, n1, m1 = _mlstm(z8, lw["mlstm_gate_b"], lw["mlstm_norm_w"], c0, n0,
                                 jnp.broadcast_to(m0[:, :, None], (b, MLSTM_HEADS, LANES)),
                                 L=LANES, lin=TS, t_valid=tn)
    n_chunks = (pos0 + tn) // CMP_STRIDE
    n_cmp = n_chunks - CMP_BLOCK // CMP_STRIDE + 1
    n_sel = -(-(pos0 + tn) // SEL_BLOCK)
    kcvc = _cmp_finish(_cmp_project_sample(cache6, page_table, layer, lw["wcat"]), lw["cmp_b1"], lw["cmp_w2"])
    qpos = pos0 + np.arange(TS)[:, None]
    bias_c = _bias_table(rel_bias, qpos - (np.arange(n_chunks) * CMP_STRIDE + CMP_BLOCK - 1)[None, :])
    o_c, sel = _nsa_sample_cmp(z8, kcvc, bias_c, pos0=pos0, n_cmp=n_cmp, n_sel=n_sel)
    kk = np.arange(PAGE_SIZE)[None, :]
    bias_tail = _bias_table(rel_bias, np.repeat(qpos - (pos0 - PAGE_SIZE + kk), KV_HEADS, axis=1))
    bias_new = _bias_table(rel_bias, qpos - (pos0 + kk))
    bias_far = _bias_table(rel_bias, np.broadcast_to(qpos - (pos0 - 2 * PAGE_SIZE), (TS, PAGE_SIZE)))
    o_s = _nsa_sample_sel(z8, sel, bias_tail, bias_new, bias_far, cache6, page_table, layer, pos0=pos0)
    wb = win6.shape[2]
    wk = np.arange(wb)[None, :]
    bias_w = _bias_table(rel_bias, np.concatenate([qpos - (pos0 - wb + wk), qpos - (pos0 + kk)], axis=1))
    o_nsa = _nsa_sample_win(z8, win6, layer, bias_w, o_c, o_s, lw["nsa_gate_b"], pos0=pos0)
    conv_state8 = jnp.pad(_pad_ff(conv_state), ((0, 0), (SUBLANES - (CONV_W - 1), 0), (0, 0)))
    x2, s5_new, conv_new = _mixer_tail(x2d, z, b, TS, tn, lw, o_mlstm, o_nsa, s5_state, conv_state8, tm=1024)
    nkv = z8[:, :tn, C_NKV:C_NKV + N_KV_SLOTS * KV_HEADS * HEAD_DIM].reshape(b, tn, N_KV_SLOTS, KV_HEADS, HEAD_DIM)
    state = (nkv[:, :, :4], nkv[:, :, 4:], c1, n1, m1[:, :, 0], s5_new[0], s5_new[1], conv_new)
    return x2, state


def kernel(x_prompt, x_sample, cache_nsa_kv, cache_win_kv, state_mlstm_c, state_mlstm_n, state_mlstm_m,
           state_s5_re, state_s5_im, state_ffn_conv, page_table, w_in, mlstm_gate_b, mlstm_norm_w,
           nsa_gate_b, cmp_w1, cmp_b1, cmp_w2, rel_bias, s5_a_re, s5_a_im, s5_b_re, s5_b_im, s5_c_re,
           s5_c_im, s5_d, s5_log_step, s5_glu_w, s5_glu_b, w_out, ln1_w, ln1_b, ffn_w_up, ffn_conv_w,
           ffn_conv_b, ffn_w_down, ln2_w, ln2_b):
    params = dict(w_in=w_in, mlstm_gate_b=mlstm_gate_b, mlstm_norm_w=mlstm_norm_w, nsa_gate_b=nsa_gate_b,
                  cmp_w1=cmp_w1, cmp_b1=cmp_b1, cmp_w2=cmp_w2, s5_a_re=s5_a_re, s5_a_im=s5_a_im, s5_b_re=s5_b_re,
                  s5_b_im=s5_b_im, s5_c_re=s5_c_re, s5_c_im=s5_c_im, s5_d=s5_d, s5_log_step=s5_log_step,
                  s5_glu_w=s5_glu_w, s5_glu_b=s5_glu_b, w_out=w_out, ln1_w=ln1_w, ln1_b=ln1_b, ffn_w_up=ffn_w_up,
                  ffn_conv_w=ffn_conv_w, ffn_conv_b=ffn_conv_b, ffn_w_down=ffn_w_down, ln2_w=ln2_w, ln2_b=ln2_b)
    depth = w_in.shape[0]
    bp, tp, _ = x_prompt.shape
    bs, tn, _ = x_sample.shape
    assert tp % TK == 0 and tn < CMP_STRIDE and tn <= TS and SUBLANES % bp == 0 and SUBLANES % bs == 0
    assert (tn * bs) % SUBLANES == 0 and tp % min(FFN_TT, tp) == 0
    assert page_table.shape[1] % CMP_PAGES == 0 and cache_nsa_kv.shape[2] == PAGE_SIZE
    xp = x_prompt.reshape(bp * tp, D_MODEL)
    xs = jnp.pad(x_sample, ((0, 0), (0, TS - tn), (0, 0))).reshape(bs * TS, D_MODEL)
    p_states, s_states = [], []
    for l in range(depth):
        lw = _pack_layer(params, l)
        xp, sp = _prompt_layer(xp, bp, tp, lw, rel_bias)
        xs, ss = _sample_layer(xs, bs, tn, lw, rel_bias, l, cache_nsa_kv, page_table, cache_win_kv,
                               (state_mlstm_c[l], jnp.pad(state_mlstm_n[l], ((0, 0), (0, 0), (0, LANES - HEAD_DIM))),
                                state_mlstm_m[l]),
                               (state_s5_re[l], state_s5_im[l]), state_ffn_conv[l])
        p_states.append(sp)
        s_states.append(ss)
    stk = lambda states, i: jnp.stack([s[i] for s in states])
    y_prompt = xp.reshape(bp, tp, D_MODEL)
    y_sample = xs.reshape(bs, TS, D_MODEL)[:, :tn]
    return (y_prompt, y_sample,
            stk(p_states, 0), stk(s_states, 0), stk(p_states, 1), stk(s_states, 1),
            stk(p_states, 2), stk(s_states, 2), stk(p_states, 3), stk(s_states, 3), stk(p_states, 4), stk(s_states, 4),
            stk(p_states, 5), stk(s_states, 5), stk(p_states, 6), stk(s_states, 6), stk(p_states, 7), stk(s_states, 7))
```

```python
import functools
import math

import jax
import jax.numpy as jnp
import numpy as np
from jax import lax
from jax.experimental import pallas as pl
from jax.experimental.pallas import tpu as pltpu

F32 = jnp.float32
BF16 = jnp.bfloat16
NEG_INF = float("-inf")
M_INIT = -1e30
MASKED = -1e30

D_MODEL = 2048
PAGE_SIZE = 128
D_MLSTM = D_MODEL // 4
D_NSA = D_MODEL // 2
D_S5 = D_MODEL - D_MLSTM - D_NSA
HEAD_DIM = 128
MLSTM_HEADS = D_MLSTM // HEAD_DIM
NSA_HEADS = D_NSA // HEAD_DIM
KV_HEADS = 2
GROUP = NSA_HEADS // KV_HEADS
N_KV_SLOTS = 6
CMP_BLOCK = 32
CMP_STRIDE = 16
SEL_BLOCK = 64
SEL_TOPK = 16
FORCE_SCORE = 1e4
WINDOW = 512
S5_GROUP_WIDTH = 16
S5_GROUPS = D_S5 // S5_GROUP_WIDTH
S5_STATE = 64
S5_CH = S5_GROUPS * S5_STATE
NUM_BUCKETS = 32
MAX_DISTANCE = 128
D_FF = ((8 * D_MODEL // 3 + 127) // 128) * 128
CONV_W = 3
LN_EPS = 1e-5
DEPTH = 2
DEEPNORM_ALPHA = (2 * DEPTH) ** 0.25
QK_SCALE = HEAD_DIM ** -0.5

LANES = 128
SUBLANES = 8
V7X_VMEM_LIMIT = 56 * 2 ** 20

C_MQ, C_MK, C_MV, C_MO = 0, 512, 1024, 1536
C_NQ = 2048
C_NKV = 3072
C_SU = 4608
C_GATE = 5120
N_IN = 5376
GATE_MI, GATE_MF, GATE_NG = 0, 4, 8
D_FFP = 5632
MLSTM_L = 256
TQ = 256
TK = 256
S5_LC = 128
FFN_TT = 1024
FFN_TF = 512
FFN_HALO = 16
CMP_PAGES = 32
SEL_PAGES = 32
SEL_TILE = 16
SEL_SHIFT = SEL_BLOCK.bit_length() - 1


def _cparams(*sem):
    return pltpu.CompilerParams(dimension_semantics=sem, vmem_limit_bytes=V7X_VMEM_LIMIT)


def _dot(a, b):
    return jnp.dot(a, b, preferred_element_type=F32)


def _dot_nt(a, b):
    return lax.dot_general(a, b, (((1,), (1,)), ((), ())), preferred_element_type=F32)


def _layer_norm(y, w, b):
    mu = jnp.mean(y, axis=-1, keepdims=True)
    d = y - mu
    var = jnp.mean(d * d, axis=-1, keepdims=True)
    return d * lax.rsqrt(var + LN_EPS) * w + b


def _mm_kernel(x_ref, w_ref, o_ref):
    o_ref[...] = _dot(x_ref[...].astype(BF16), w_ref[...]).astype(o_ref.dtype)


def _up_cols(tf):
    nf, per = D_FFP // tf, tf // LANES
    return lambda j: (j * per + (j >= nf).astype(jnp.int32) * (D_FF // LANES - nf * per)) * LANES


def _matmul(x, w, *, tm, tn, n=None, w_cols=None):
    m, k = x.shape
    n = w.shape[1] if n is None else n
    tm = min(tm, m)
    if w_cols is None:
        w_spec = pl.BlockSpec((k, tn), lambda i, j: (0, j))
    else:
        w_spec = pl.BlockSpec((pl.Element(k), pl.Element(tn)), lambda i, j: (0, w_cols(j)))
    return pl.pallas_call(
        _mm_kernel,
        grid=(m // tm, n // tn),
        in_specs=[pl.BlockSpec((tm, k), lambda i, j: (i, 0)), w_spec],
        out_specs=pl.BlockSpec((tm, tn), lambda i, j: (i, j)),
        out_shape=jax.ShapeDtypeStruct((m, n), F32),
        compiler_params=_cparams("parallel", "parallel"),
        name="proj",
    )(x, w)


KV4_GROUPS = 4 * KV_HEADS
WIN_GROUPS = 2 * KV_HEADS


def _proj_in_kernel(x_ref, w_ref, z_ref, kv_ref, win_ref, *, tn):
    j = pl.program_id(1)
    acc = _dot(x_ref[...].astype(BF16), w_ref[...])
    z_ref[...] = acc
    tm = acc.shape[0]
    per_tile = tn // HEAD_DIM
    first_tile = C_NKV // tn
    for tile in range(first_tile, first_tile + N_KV_SLOTS * KV_HEADS // per_tile):
        @pl.when(j == tile)
        def _(tile=tile):
            for c in range(per_tile):
                grp = (tile - first_tile) * per_tile + c
                val = acc[:, c * HEAD_DIM:(c + 1) * HEAD_DIM]
                if grp < KV4_GROUPS:
                    kv_ref[pl.ds(grp, tm, stride=KV4_GROUPS), :] = val
                else:
                    win_ref[pl.ds(grp - KV4_GROUPS, tm, stride=WIN_GROUPS), :] = val


def _proj_in(x, w, *, tm, tn):
    m, k = x.shape
    n = w.shape[1]
    assert C_NKV % tn == 0 and (N_KV_SLOTS * KV_HEADS * HEAD_DIM) % tn == 0
    return pl.pallas_call(
        functools.partial(_proj_in_kernel, tn=tn),
        grid=(m // tm, n // tn),
        in_specs=[pl.BlockSpec((tm, k), lambda i, j: (i, 0)), pl.BlockSpec((k, tn), lambda i, j: (0, j))],
        out_specs=[pl.BlockSpec((tm, tn), lambda i, j: (i, j)),
                   pl.BlockSpec((tm * KV4_GROUPS, HEAD_DIM), lambda i, j: (i, 0)),
                   pl.BlockSpec((tm * WIN_GROUPS, HEAD_DIM), lambda i, j: (i, 0))],
        out_shape=[jax.ShapeDtypeStruct((m, n), F32),
                   jax.ShapeDtypeStruct((m * KV4_GROUPS, HEAD_DIM), F32),
                   jax.ShapeDtypeStruct((m * WIN_GROUPS, HEAD_DIM), F32)],
        compiler_params=_cparams("parallel", "arbitrary"),
        name="proj_in",
    )(x, w)


def _wout_kernel(om_ref, on_ref, os_ref, x_ref, w_ref, lw_ref, lb_ref, o_ref):
    acc = _dot(om_ref[...].astype(BF16), w_ref[0:D_MLSTM, :])
    acc += _dot(on_ref[...].astype(BF16), w_ref[D_MLSTM:D_MLSTM + D_NSA, :])
    acc += _dot(os_ref[...].astype(BF16), w_ref[D_MLSTM + D_NSA:D_MODEL, :])
    o_ref[...] = _layer_norm(DEEPNORM_ALPHA * x_ref[...] + acc, lw_ref[...], lb_ref[...])


def _wout_ln(om, on, os_, x, w, lw, lb, *, tm):
    m = x.shape[0]
    tm = min(tm, m)
    row = lambda width: pl.BlockSpec((tm, width), lambda i: (i, 0))
    full = lambda a: pl.BlockSpec(a.shape, lambda i: (0, 0))
    return pl.pallas_call(
        _wout_kernel,
        grid=(m // tm,),
        in_specs=[row(D_MLSTM), row(D_NSA), row(D_S5), row(D_MODEL), full(w), full(lw), full(lb)],
        out_specs=row(D_MODEL),
        out_shape=jax.ShapeDtypeStruct((m, D_MODEL), F32),
        compiler_params=_cparams("parallel"),
        name="wout_ln",
    )(om, on, os_, x, w, lw, lb)


def _valid_rows(w_ref, tile):
    rows = w_ref.shape[0]
    row = tile * rows + lax.broadcasted_iota(jnp.int32, (rows, 1), 0)
    w = w_ref[...]
    return jnp.where(row < D_FF, w, jnp.zeros_like(w))


def _down_kernel(h_ref, w_ref, x_ref, lw_ref, lb_ref, o_ref, acc_ref, *, nk):
    k = pl.program_id(1)

    @pl.when(k == 0)
    def _():
        acc_ref[...] = jnp.zeros_like(acc_ref)

    acc_ref[...] += _dot(h_ref[...], _valid_rows(w_ref, k))

    @pl.when(k == nk - 1)
    def _():
        o_ref[...] = _layer_norm(DEEPNORM_ALPHA * x_ref[...] + acc_ref[...], lw_ref[...], lb_ref[...])


def _down_ln(h, w, x, lw, lb, *, tm, tk):
    m, kk = h.shape
    tm = min(tm, m)
    nk = kk // tk
    return pl.pallas_call(
        functools.partial(_down_kernel, nk=nk),
        grid=(m // tm, nk),
        in_specs=[
            pl.BlockSpec((tm, tk), lambda i, k: (i, k)),
            pl.BlockSpec((tk, D_MODEL), lambda i, k: (k, 0)),
            pl.BlockSpec((tm, D_MODEL), lambda i, k: (i, 0)),
            pl.BlockSpec((1, D_MODEL), lambda i, k: (0, 0)),
            pl.BlockSpec((1, D_MODEL), lambda i, k: (0, 0)),
        ],
        out_specs=pl.BlockSpec((tm, D_MODEL), lambda i, k: (i, 0)),
        out_shape=jax.ShapeDtypeStruct((m, D_MODEL), F32),
        scratch_shapes=[pltpu.VMEM((tm, D_MODEL), F32)],
        compiler_params=_cparams("parallel", "arbitrary"),
        name="down_ln",
    )(h, w, x, lw, lb)


def _convgate_kernel(a_ref, g_ref, ha_ref, hg_ref, sa_ref, sg_ref, wa_ref, wg_ref, ba_ref, bg_ref, o_ref):
    first = pl.program_id(1) == 0
    tt = a_ref.shape[0]
    rows = lax.broadcasted_iota(jnp.int32, (tt, 1), 0)

    def conv(cur_ref, halo_ref, st_ref, w_ref, b_ref):
        cur = cur_ref[...]
        prev = jnp.where(first, st_ref[...], halo_ref[...])
        p1 = prev[7:8, :]
        p2 = prev[6:7, :]
        x1 = jnp.where(rows == 0, p1, pltpu.roll(cur, 1, 0))
        x2 = jnp.where(rows == 0, p2, jnp.where(rows == 1, p1, pltpu.roll(cur, 2, 0)))
        w = w_ref[...]
        return b_ref[...] + w[0:1, :] * x2 + w[1:2, :] * x1 + w[2:3, :] * cur

    a = conv(a_ref, ha_ref, sa_ref, wa_ref, ba_ref)
    g = conv(g_ref, hg_ref, sg_ref, wg_ref, bg_ref)
    o_ref[...] = (a * jax.nn.sigmoid(a) * g).astype(o_ref.dtype)


def _convgate(up, state8, conv_w8, conv_b, *, tt, tf):
    b, t, _ = up.shape
    tt = min(tt, t)
    nf = D_FFP // tf
    hb = tt // SUBLANES
    cur_a = pl.BlockSpec((None, tt, tf), lambda bi, ti, fi: (bi, ti, fi))
    cur_g = pl.BlockSpec((None, tt, tf), lambda bi, ti, fi: (bi, ti, fi + nf))
    halo_a = pl.BlockSpec((None, SUBLANES, tf), lambda bi, ti, fi: (bi, jnp.maximum(ti * hb - 1, 0), fi))
    halo_g = pl.BlockSpec((None, SUBLANES, tf), lambda bi, ti, fi: (bi, jnp.maximum(ti * hb - 1, 0), fi + nf))
    st_a = pl.BlockSpec((None, SUBLANES, tf), lambda bi, ti, fi: (bi, 0, fi))
    st_g = pl.BlockSpec((None, SUBLANES, tf), lambda bi, ti, fi: (bi, 0, fi + nf))
    w_a = pl.BlockSpec((SUBLANES, tf), lambda bi, ti, fi: (0, fi))
    w_g = pl.BlockSpec((SUBLANES, tf), lambda bi, ti, fi: (0, fi + nf))
    b_a = pl.BlockSpec((1, tf), lambda bi, ti, fi: (0, fi))
    b_g = pl.BlockSpec((1, tf), lambda bi, ti, fi: (0, fi + nf))
    return pl.pallas_call(
        _convgate_kernel,
        grid=(b, t // tt, nf),
        in_specs=[cur_a, cur_g, halo_a, halo_g, st_a, st_g, w_a, w_g, b_a, b_g],
        out_specs=pl.BlockSpec((None, tt, tf), lambda bi, ti, fi: (bi, ti, fi)),
        out_shape=jax.ShapeDtypeStruct((b, t, D_FFP), BF16),
        compiler_params=_cparams("parallel", "parallel", "parallel"),
        name="convgate",
    )(up, up, up, up, state8, state8, conv_w8, conv_w8, conv_b, conv_b)


def _ffn_kernel(x_ref, xh_ref, wa_ref, wg_ref, wd_ref, cwa_ref, cwg_ref, cba_ref, cbg_ref, lw_ref, lb_ref,
                o_ref, sa_ref, sg_ref, xe_ref, *, nf):
    first = pl.program_id(1) == 0
    f = pl.program_id(2)
    tt = x_ref.shape[0]

    @pl.when(f == 0)
    def _():
        o_ref[...] = jnp.zeros_like(o_ref)
        halo = jnp.where(first, 0.0, xh_ref[...])
        xe_ref[0:FFN_HALO, :] = halo.astype(BF16)
        xe_ref[FFN_HALO:, :] = x_ref[...].astype(BF16)

    xe = xe_ref[...]

    def branch(w_ref, cw_ref, cb_ref, s_ref):
        up = _dot(xe, w_ref[...])
        s_ref[...] = up[FFN_HALO + tt - SUBLANES:FFN_HALO + tt, :]
        w = cw_ref[...]
        x1 = pltpu.roll(up, 1, 0)[FFN_HALO:, :]
        x2 = pltpu.roll(up, 2, 0)[FFN_HALO:, :]
        return cb_ref[...] + w[0:1, :] * x2 + w[1:2, :] * x1 + w[2:3, :] * up[FFN_HALO:, :]

    a = branch(wa_ref, cwa_ref, cba_ref, sa_ref)
    g = branch(wg_ref, cwg_ref, cbg_ref, sg_ref)
    o_ref[...] += _dot((a * jax.nn.sigmoid(a) * g).astype(BF16), _valid_rows(wd_ref, f))

    @pl.when(f == nf - 1)
    def _():
        o_ref[...] = _layer_norm(DEEPNORM_ALPHA * x_ref[...] + o_ref[...], lw_ref[...], lb_ref[...])


def _ffn_prompt(x3, w_up, w_down, conv_w8, conv_b, lw, lb, *, tt, tf):
    b, t, _ = x3.shape
    nf = D_FFP // tf
    tt = min(tt, t)
    hb = tt // FFN_HALO
    half_a = lambda rows: pl.BlockSpec((rows, tf), lambda bi, ti, fi: (0, fi))
    half_g = lambda rows: pl.BlockSpec((rows, tf), lambda bi, ti, fi: (0, fi + nf))
    cols = _up_cols(tf)
    w_a = pl.BlockSpec((pl.Element(D_MODEL), pl.Element(tf)), lambda bi, ti, fi: (0, cols(fi)))
    w_g = pl.BlockSpec((pl.Element(D_MODEL), pl.Element(tf)), lambda bi, ti, fi: (0, cols(fi + nf)))
    vec = pl.BlockSpec((1, D_MODEL), lambda bi, ti, fi: (0, 0))
    tail = pl.BlockSpec((None, SUBLANES, tf), lambda bi, ti, fi: (bi, 0, fi))
    return pl.pallas_call(
        functools.partial(_ffn_kernel, nf=nf),
        grid=(b, t // tt, nf),
        in_specs=[pl.BlockSpec((None, tt, D_MODEL), lambda bi, ti, fi: (bi, ti, 0), pipeline_mode=pl.Buffered(1)),
                  pl.BlockSpec((None, FFN_HALO, D_MODEL), lambda bi, ti, fi: (bi, jnp.maximum(ti * hb - 1, 0), 0)),
                  w_a, w_g,
                  pl.BlockSpec((tf, D_MODEL), lambda bi, ti, fi: (fi, 0)),
                  half_a(SUBLANES), half_g(SUBLANES), half_a(1), half_g(1), vec, vec],
        out_specs=[pl.BlockSpec((None, tt, D_MODEL), lambda bi, ti, fi: (bi, ti, 0)), tail, tail],
        out_shape=[jax.ShapeDtypeStruct((b, t, D_MODEL), F32),
                   jax.ShapeDtypeStruct((b, SUBLANES, D_FFP), F32),
                   jax.ShapeDtypeStruct((b, SUBLANES, D_FFP), F32)],
        scratch_shapes=[pltpu.VMEM((tt + FFN_HALO, D_MODEL), BF16)],
        compiler_params=_cparams("parallel", "arbitrary", "arbitrary"),
        name="ffn_prompt",
    )(x3, x3, w_up, w_up, w_down, conv_w8, conv_w8, conv_b, conv_b, lw, lb)


def _cumsum_rows(x):
    n = x.shape[0]
    rows = lax.broadcasted_iota(jnp.int32, (n, 1), 0)
    d = 1
    while d < n:
        x = x + jnp.where(rows >= d, pltpu.roll(x, d, 0), 0.0)
        d *= 2
    return x


def _log_sigmoid(x):
    return jnp.minimum(x, 0.0) - jnp.log1p(jnp.exp(-jnp.abs(x)))


def _mlstm_kernel(q_ref, k_ref, v_ref, og_ref, g_ref, gb_ref, nw_ref, c0_ref, n0_ref, m0_ref,
                  out_ref, c_ref, n_ref, m_ref, *, L, t_valid):
    ci = pl.program_id(1)
    lin = q_ref.shape[0]

    @pl.when(ci == 0)
    def _():
        c_ref[...] = c0_ref[...]
        n_ref[...] = n0_ref[...]
        m_ref[...] = m0_ref[...]

    def rows_of(ref):
        x = ref[...]
        if lin < L:
            x = jnp.concatenate([x, jnp.zeros((L - lin, x.shape[1]), x.dtype)], axis=0)
        return x

    rows = lax.broadcasted_iota(jnp.int32, (L, 1), 0)
    valid = (ci * L + rows) < t_valid
    pre = rows_of(g_ref) + gb_ref[...]
    lf = jnp.where(valid, _log_sigmoid(pre), 0.0)
    ig = jnp.where(valid, pre, NEG_INF)
    bcum = _cumsum_rows(lf)
    dt = (pltpu.roll(ig, GATE_MF - GATE_MI, 1) - bcum).T
    q_all, k_all, v_all, og_all = rows_of(q_ref), rows_of(k_ref), rows_of(v_ref), rows_of(og_ref)
    tri = lax.broadcasted_iota(jnp.int32, (L, L), 0) >= lax.broadcasted_iota(jnp.int32, (L, L), 1)
    nw = nw_ref[...]

    for h in range(MLSTM_HEADS):
        sl = slice(h * HEAD_DIM, (h + 1) * HEAD_DIM)
        q = q_all[:, sl]
        k = k_all[:, sl] * QK_SCALE
        v = v_all[:, sl]
        qb, kb, vb = q.astype(BF16), k.astype(BF16), v.astype(BF16)
        b_col = bcum[:, GATE_MF + h:GATE_MF + h + 1]
        ig_col = ig[:, GATE_MI + h:GATE_MI + h + 1]
        d_row = dt[GATE_MF + h:GATE_MF + h + 1, :]
        c_prev = c_ref[h]
        n_prev = n_ref[h:h + 1, :]
        m_prev = m_ref[h:h + 1, 0:1]

        dmat = jnp.where(tri, b_col + d_row, NEG_INF)
        g_col = b_col + m_prev
        m_row = jnp.maximum(jnp.max(dmat, axis=1, keepdims=True), g_col)
        a = jnp.exp(dmat - m_row) * _dot_nt(qb, kb)
        w_inter = jnp.exp(g_col - m_row)
        num = _dot(a.astype(BF16), vb) + w_inter * _dot(qb, c_prev.astype(BF16))
        den = jnp.sum(a, axis=1, keepdims=True) + w_inter * jnp.sum(q * n_prev, axis=1, keepdims=True)
        hid = num / jnp.maximum(jnp.abs(den), jnp.exp(-m_row))
        mu = jnp.mean(hid, axis=1, keepdims=True)
        dlt = hid - mu
        var = jnp.mean(dlt * dlt, axis=1, keepdims=True)
        hn = dlt * lax.rsqrt(var + LN_EPS) * nw[:, sl]
        res = jax.nn.sigmoid(og_all[:, sl]) * hn
        out_ref[:, sl] = res[0:lin, :]

        f_tot = b_col[L - 1:L, :]
        w_s = f_tot - b_col + ig_col
        m_new = jnp.maximum(f_tot + m_prev, jnp.max(w_s, axis=0, keepdims=True))
        ws = jnp.exp(w_s - m_new)
        decay = jnp.exp(f_tot + m_prev - m_new)
        c_ref[h] = decay * c_prev + _dot(k.T.astype(BF16), (ws * v).astype(BF16))
        n_ref[h:h + 1, :] = decay * n_prev + jnp.sum(ws * k, axis=0, keepdims=True)
        m_ref[h:h + 1, :] = jnp.broadcast_to(m_new, (1, LANES))


def _mlstm(z, gate_b, norm_w, c0, n0, m0, *, L, lin, t_valid):
    b, tz, _ = z.shape
    nchunks = tz // lin
    qblk = lambda col: pl.BlockSpec((None, lin, D_MLSTM), lambda bi, ci: (bi, ci, col // D_MLSTM))
    st4 = pl.BlockSpec((None, MLSTM_HEADS, HEAD_DIM, HEAD_DIM), lambda bi, ci: (bi, 0, 0, 0))
    st3 = pl.BlockSpec((None, MLSTM_HEADS, LANES), lambda bi, ci: (bi, 0, 0))
    return pl.pallas_call(
        functools.partial(_mlstm_kernel, L=L, t_valid=t_valid),
        grid=(b, nchunks),
        in_specs=[qblk(C_MQ), qblk(C_MK), qblk(C_MV), qblk(C_MO),
                  pl.BlockSpec((None, lin, LANES), lambda bi, ci: (bi, ci, C_GATE // LANES)),
                  pl.BlockSpec((1, LANES), lambda bi, ci: (0, 0)),
                  pl.BlockSpec((1, D_MLSTM), lambda bi, ci: (0, 0)),
                  st4, st3, st3],
        out_specs=[pl.BlockSpec((None, lin, D_MLSTM), lambda bi, ci: (bi, ci, 0)), st4, st3, st3],
        out_shape=[jax.ShapeDtypeStruct((b, tz, D_MLSTM), F32),
                   jax.ShapeDtypeStruct((b, MLSTM_HEADS, HEAD_DIM, HEAD_DIM), F32),
                   jax.ShapeDtypeStruct((b, MLSTM_HEADS, LANES), F32),
                   jax.ShapeDtypeStruct((b, MLSTM_HEADS, LANES), F32)],
        compiler_params=_cparams("parallel", "arbitrary"),
        name="mlstm",
    )(z, z, z, z, z, gate_b, norm_w, c0, n0, m0)


S5_IN_TILES = S5_CH // 256
S5_IN_FEATS = D_S5 // S5_IN_TILES
S5_OUT_TILES = D_S5 // 256
S5_OUT_CH = S5_CH // S5_OUT_TILES


S5_LT = S5_CH // LANES


def _lane_tiles(x):
    return jnp.stack([x[:, k * LANES:(k + 1) * LANES] for k in range(x.shape[1] // LANES)])


def _lane_untile(x):
    return jnp.concatenate([x[k] for k in range(x.shape[0])], axis=1)


def _s5_kernel(u_ref, wb_ref, wc_ref, lr_ref, li_ref, d_ref, gw_ref, gb_ref, x0r_ref, x0i_ref,
               o_ref, xr_ref, xi_ref, sr_ref, si_ref, y_ref, *, steps):
    nb, lc, _ = u_ref.shape

    @pl.when(pl.program_id(0) == 0)
    def _():
        xr_ref[...] = x0r_ref[...]
        xi_ref[...] = x0i_ref[...]

    for b in range(nb):
        ub = u_ref[b].astype(BF16)
        for c in range(S5_IN_TILES):
            bu = _dot(ub[:, c * S5_IN_FEATS:(c + 1) * S5_IN_FEATS], wb_ref[c])
            for k in range(256 // LANES):
                rows = pl.ds(b, lc, stride=nb)
                sr_ref.at[c * (256 // LANES) + k][rows, :] = bu[:, k * LANES:(k + 1) * LANES]
                si_ref.at[c * (256 // LANES) + k][rows, :] = bu[:, 256 + k * LANES:256 + (k + 1) * LANES]
    lam_r = _lane_tiles(jnp.broadcast_to(lr_ref[...], (SUBLANES, S5_CH)))
    lam_i = _lane_tiles(jnp.broadcast_to(li_ref[...], (SUBLANES, S5_CH)))
    row8 = lax.broadcasted_iota(jnp.int32, (1, SUBLANES, 1), 1)
    per_group = SUBLANES // nb

    def group(i, carry):
        xr, xi = carry
        r0 = pl.multiple_of(i * SUBLANES, SUBLANES)
        br = sr_ref[:, pl.ds(r0, SUBLANES), :]
        bi = si_ref[:, pl.ds(r0, SUBLANES), :]
        out_r, out_i = br, bi
        for k in range(per_group):
            nr = lam_r * xr - lam_i * xi + br
            ni = lam_r * xi + lam_i * xr + bi
            here = (row8 >= k * nb) & (row8 < (k + 1) * nb)
            out_r = jnp.where(here, nr, out_r)
            out_i = jnp.where(here, ni, out_i)
            if per_group > 1:
                xr = pltpu.roll(nr, nb, 1)
                xi = pltpu.roll(ni, nb, 1)
            else:
                xr, xi = nr, ni
        sr_ref[:, pl.ds(r0, SUBLANES), :] = out_r
        si_ref[:, pl.ds(r0, SUBLANES), :] = out_i
        return xr, xi

    xr, xi = lax.fori_loop(0, steps * nb // SUBLANES, group, (_lane_tiles(xr_ref[...]), _lane_tiles(xi_ref[...])))
    xr_ref[...] = _lane_untile(xr)
    xi_ref[...] = _lane_untile(xi)
    per_out = S5_OUT_CH // LANES
    for j in range(S5_OUT_TILES):
        lhs_r = jnp.concatenate([sr_ref[j * per_out + k] for k in range(per_out)], axis=1).astype(BF16)
        lhs_i = jnp.concatenate([si_ref[j * per_out + k] for k in range(per_out)], axis=1).astype(BF16)
        y = _dot(lhs_r, wc_ref[0, j]) + _dot(lhs_i, wc_ref[1, j])
        for k in range(256 // LANES):
            y_ref[j * (256 // LANES) + k] = y[:, k * LANES:(k + 1) * LANES]
    for b in range(nb):
        rows = pl.ds(b, lc, stride=nb)
        y = jnp.concatenate([y_ref.at[k][rows, :] for k in range(D_S5 // LANES)], axis=1) + d_ref[...] * u_ref[b]
        zz = jax.nn.gelu(y)
        o_ref[b] = zz * jax.nn.sigmoid(_dot(zz.astype(BF16), gw_ref[...]) + gb_ref[...])


def _s5(z3, wb, wc, lam_r, lam_i, d, glu_w, glu_b, x0r, x0i, *, lc, steps):
    b, tz, _ = z3.shape
    lc = min(lc, tz)
    steps = min(steps, lc)
    assert steps == lc or tz == lc
    full = lambda a: pl.BlockSpec(a.shape, lambda i: (0,) * a.ndim)
    st = pl.BlockSpec((SUBLANES, S5_CH), lambda i: (0, 0))
    return pl.pallas_call(
        functools.partial(_s5_kernel, steps=steps),
        grid=(tz // lc,),
        in_specs=[pl.BlockSpec((b, lc, D_S5), lambda i: (0, i, C_SU // D_S5)),
                  full(wb), full(wc), full(lam_r), full(lam_i), full(d), full(glu_w), full(glu_b), st, st],
        out_specs=[pl.BlockSpec((b, lc, D_S5), lambda i: (0, i, 0)), st, st],
        out_shape=[jax.ShapeDtypeStruct((b, tz, D_S5), F32),
                   jax.ShapeDtypeStruct((SUBLANES, S5_CH), F32),
                   jax.ShapeDtypeStruct((SUBLANES, S5_CH), F32)],
        scratch_shapes=[pltpu.VMEM((S5_LT, lc * b, LANES), F32), pltpu.VMEM((S5_LT, lc * b, LANES), F32),
                        pltpu.VMEM((D_S5 // LANES, lc * b, LANES), F32)],
        compiler_params=_cparams("arbitrary"),
        name="s5",
    )(z3, wb, wc, lam_r, lam_i, d, glu_w, glu_b, x0r, x0i)


N_CMB = 2 * KV_HEADS


def _cmp_project(rows_of, w_ref, o_ref):
    nch = o_ref.shape[0]
    for cmb in range(N_CMB):
        slot = cmb // KV_HEADS
        rows_ref = rows_of(cmb)
        acc = jnp.zeros((nch, 2 * HEAD_DIM), F32)
        for l in range(CMP_STRIDE):
            x = rows_ref[pl.ds(l, nch, stride=CMP_STRIDE), :]
            acc += _dot(x.astype(BF16), w_ref[slot, l])
        o_ref[:, cmb * 2 * HEAD_DIM:(cmb + 1) * 2 * HEAD_DIM] = acc


def _cmp_p_kernel(r0_ref, r1_ref, r2_ref, r3_ref, w_ref, o_ref):
    rows = (r0_ref, r1_ref, r2_ref, r3_ref)
    _cmp_project(lambda cmb: rows[cmb], w_ref, o_ref)


def _cmp_project_prompt(z, wcat):
    b, t, _ = z.shape
    nch = t // CMP_STRIDE
    rows = lambda cmb: pl.BlockSpec((None, t, HEAD_DIM), lambda bi: (bi, 0, C_NKV // HEAD_DIM + cmb))
    return pl.pallas_call(
        _cmp_p_kernel,
        grid=(b,),
        in_specs=[rows(cmb) for cmb in range(N_CMB)] + [pl.BlockSpec(wcat.shape, lambda bi: (0, 0, 0, 0))],
        out_specs=pl.BlockSpec((None, nch, N_CMB * 2 * HEAD_DIM), lambda bi: (bi, 0, 0)),
        out_shape=jax.ShapeDtypeStruct((b, nch, N_CMB * 2 * HEAD_DIM), F32),
        compiler_params=_cparams("parallel"),
        name="cmp_project_prompt",
    )(z, z, z, z, wcat)


def _cmp_s_kernel(pt_ref, *refs):
    del pt_ref
    per_pos = 2 * KV_HEADS
    page_refs = [r.reshape(PAGE_SIZE * per_pos, HEAD_DIM) for r in refs[:CMP_PAGES]]
    w_ref, o_ref = refs[CMP_PAGES:]
    per_page = PAGE_SIZE // CMP_STRIDE
    for cmb in range(N_CMB):
        acc = jnp.zeros((o_ref.shape[0], 2 * HEAD_DIM), F32)
        for l in range(CMP_STRIDE):
            x = jnp.concatenate(
                [page_refs[p][pl.ds(l * per_pos + cmb, per_page, stride=CMP_STRIDE * per_pos), :]
                 for p in range(CMP_PAGES)], axis=0)
            acc += _dot(x.astype(BF16), w_ref[cmb // KV_HEADS, l])
        o_ref[:, cmb * 2 * HEAD_DIM:(cmb + 1) * 2 * HEAD_DIM] = acc


def _page_spec(layer, slot, p, pages_per_step, grid_rank):
    def index_map(*idx):
        bi, si, pt = idx[0], idx[grid_rank - 1], idx[grid_rank]
        return (layer, pt[bi, si * pages_per_step + p], 0, slot, 0, 0)

    return pl.BlockSpec((None, None, PAGE_SIZE, None, KV_HEADS, HEAD_DIM), index_map)


def _cmp_project_sample(cache6, page_table, layer, wcat):
    b, npages = page_table.shape
    steps = npages // CMP_PAGES
    nch = CMP_PAGES * PAGE_SIZE // CMP_STRIDE
    def page_pair(p):
        return pl.BlockSpec((None, None, PAGE_SIZE, 2, KV_HEADS, HEAD_DIM),
                            lambda bi, si, pt: (layer, pt[bi, si * CMP_PAGES + p], 0, 0, 0, 0))

    pages = [page_pair(p) for p in range(CMP_PAGES)]

    grid_spec = pltpu.PrefetchScalarGridSpec(
        num_scalar_prefetch=1,
        grid=(b, steps),
        in_specs=pages + [pl.BlockSpec(wcat.shape, lambda bi, si, pt: (0, 0, 0, 0))],
        out_specs=pl.BlockSpec((None, nch, N_CMB * 2 * HEAD_DIM), lambda bi, si, pt: (bi, si, 0)),
    )
    return pl.pallas_call(
        _cmp_s_kernel,
        grid_spec=grid_spec,
        out_shape=jax.ShapeDtypeStruct((b, steps * nch, N_CMB * 2 * HEAD_DIM), F32),
        compiler_params=_cparams("parallel", "arbitrary"),
        name="cmp_project_sample",
    )(page_table, *([cache6] * CMP_PAGES), wcat)


def _cmp_fin_kernel(p_ref, b1_ref, w2_ref, o_ref):
    nch = p_ref.shape[0]
    for cmb in range(N_CMB):
        slot = cmb // KV_HEADS
        c0 = cmb * 2 * HEAD_DIM
        first = p_ref[:, c0:c0 + HEAD_DIM]
        second = pltpu.roll(p_ref[:, c0 + HEAD_DIM:c0 + 2 * HEAD_DIM], nch - 1, 0)
        hid = b1_ref[slot:slot + 1, :] + first + second
        o_ref[cmb] = _dot(jax.nn.gelu(hid).astype(BF16), w2_ref[slot])


def _cmp_finish(p, b1, w2):
    b, nch, _ = p.shape
    return pl.pallas_call(
        _cmp_fin_kernel,
        grid=(b,),
        in_specs=[pl.BlockSpec((None, nch, p.shape[2]), lambda bi: (bi, 0, 0)),
                  pl.BlockSpec(b1.shape, lambda bi: (0, 0)),
                  pl.BlockSpec(w2.shape, lambda bi: (0, 0, 0))],
        out_specs=pl.BlockSpec((None, N_CMB, nch, HEAD_DIM), lambda bi: (bi, 0, 0, 0)),
        out_shape=jax.ShapeDtypeStruct((b, N_CMB, nch, HEAD_DIM), F32),
        compiler_params=_cparams("parallel"),
        name="cmp_finish",
    )(p, b1, w2)


def _softmax_init(m_scr, l_scr, acc_scr):
    m_scr[...] = jnp.full(m_scr.shape, M_INIT, F32)
    l_scr[...] = jnp.zeros(l_scr.shape, F32)
    acc_scr[...] = jnp.zeros(acc_scr.shape, F32)


def _softmax_update(s, mask, vb, m_scr, l_scr, acc_scr):
    s = jnp.where(mask, s, NEG_INF)
    m_prev = m_scr[...]
    m_new = jnp.maximum(m_prev, jnp.max(s, axis=1, keepdims=True))
    alpha = jnp.exp(m_prev - m_new)
    p = jnp.exp(s - m_new)
    l_scr[...] = alpha * l_scr[...] + jnp.sum(p, axis=1, keepdims=True)
    acc_scr[...] = alpha * acc_scr[...] + _dot(p.astype(BF16), vb)
    m_scr[...] = m_new


def _softmax_result(l_scr, acc_scr):
    l = l_scr[...]
    return acc_scr[...] / jnp.where(l > 0, l, 1.0)


def _masked_probs(s, mask):
    s = jnp.where(mask, s, NEG_INF)
    m = jnp.max(s, axis=1, keepdims=True)
    m = jnp.where(m > NEG_INF, m, 0.0)
    p = jnp.exp(s - m)
    den = jnp.sum(p, axis=1, keepdims=True)
    return p / jnp.where(den > 0, den, 1.0)


def _stack_heads(q):
    return jnp.concatenate([q[:, j * HEAD_DIM:(j + 1) * HEAD_DIM] for j in range(GROUP)], axis=0)


def _block_importance(pc, n_cmp, n_sel):
    ncp = pc.shape[1]
    nsp = -(-n_sel // LANES) * LANES
    c_start = lax.broadcasted_iota(jnp.int32, (ncp, nsp), 0) * CMP_STRIDE
    s_start = lax.broadcasted_iota(jnp.int32, (ncp, nsp), 1) * SEL_BLOCK
    overlap = ((c_start < s_start + SEL_BLOCK) & (c_start + CMP_BLOCK > s_start)
               & (c_start < n_cmp * CMP_STRIDE)).astype(F32)
    return jnp.dot(pc, overlap, preferred_element_type=F32, precision=lax.Precision.HIGHEST)


def _force_blocks(imp, blk, cur):
    forced = (blk == 0) | (blk == cur) | (blk == cur - 1)
    imp = jnp.where(forced, FORCE_SCORE, imp)
    return jnp.where(blk > cur, -1.0, imp)


def _select_blocks_cols(pc, cur, n_cmp, n_sel):
    imp = _block_importance(pc, n_cmp, n_sel)
    blk = lax.broadcasted_iota(jnp.int32, (1, imp.shape[1]), 1)
    imp = _force_blocks(imp, blk, cur)

    def body(sp, count):
        col = jnp.sum(jnp.where(blk == sp, imp, 0.0), axis=1, keepdims=True)
        ahead = (col > imp) | ((col == imp) & (sp < blk))
        return count + ahead.astype(F32)

    count = lax.fori_loop(0, n_sel, body, jnp.zeros(imp.shape, F32), unroll=8)
    return ((count < min(SEL_TOPK, n_sel)) & (blk < n_sel)).astype(F32)


def _select_blocks_rows(pc, cur_row, n_cmp, n_sel):
    rows = pc.shape[0]
    nrow = -(-n_sel // SUBLANES) * SUBLANES
    imp_t = _block_importance(pc, n_cmp, n_sel).T[0:nrow, :]
    blk = lax.broadcasted_iota(jnp.int32, (nrow, 1), 0)
    imp_t = _force_blocks(imp_t, blk, cur_row)
    count = jnp.zeros(imp_t.shape, F32)
    for sp in range(n_sel):
        row = imp_t[sp:sp + 1, :]
        count += ((row > imp_t) | ((row == imp_t) & (sp < blk))).astype(F32)
    sel_t = ((count < min(SEL_TOPK, n_sel)) & (blk < n_sel)).astype(F32)
    sel_t = jnp.concatenate([sel_t, jnp.zeros((LANES - nrow, rows), F32)], axis=0)
    return sel_t.T


def _gate(gt, col):
    lane = lax.broadcasted_iota(jnp.int32, (1, LANES), 1)
    return jax.nn.sigmoid(jnp.sum(jnp.where(lane == col, gt, 0.0), axis=1, keepdims=True))


def _nsa_prompt_kernel(q_ref, ks_ref, vs_ref, kw_ref, vw_ref, kc_ref, vc_ref, bc_ref, tz_ref, gt_ref, gb_ref,
                       o_ref, s_scr, selb_scr, mx_scr, l_scr, acc_scr, *, t_len):
    g = pl.program_id(1)
    qt = pl.program_id(2)
    q0 = qt * TQ
    n_cmp = t_len // CMP_STRIDE - (CMP_BLOCK // CMP_STRIDE) + 1
    n_sel = t_len // SEL_BLOCK
    ncp = kc_ref.shape[0]
    rows4 = GROUP * TQ
    qs = _stack_heads(q_ref[...] * QK_SCALE).astype(BF16)
    qpos = q0 + (lax.broadcasted_iota(jnp.int32, (rows4, 1), 0) & (TQ - 1))

    n_idx = lax.broadcasted_iota(jnp.int32, (1, ncp), 1)
    s_c = _dot_nt(qs, kc_ref[...].astype(BF16)) + bc_ref[...].reshape(rows4, ncp)
    cmask = (qpos - (n_idx * CMP_STRIDE + CMP_BLOCK - 1) >= 0) & (n_idx < n_cmp)
    p_c = _masked_probs(s_c, cmask)
    o_c = _dot(p_c.astype(BF16), vc_ref[...].astype(BF16))
    pc = p_c[0:TQ] + p_c[TQ:2 * TQ] + p_c[2 * TQ:3 * TQ] + p_c[3 * TQ:4 * TQ]
    cur_row = jnp.right_shift(q0 + lax.broadcasted_iota(jnp.int32, (1, TQ), 1), SEL_SHIFT)
    sel = _select_blocks_rows(pc, cur_row, n_cmp, n_sel)
    sel_add = ((sel - 1.0) * -MASKED).astype(BF16)

    kt_hi = (q0 + TQ - 1) // TK + 1
    kt_far = jnp.maximum(q0 - (FAR_DIST - 1), 0) // TK
    kidx = lax.broadcasted_iota(jnp.int32, (1, TK), 1)
    srow = lax.broadcasted_iota(jnp.int32, (LANES, TK), 0)

    def spread_selection(kt, carry):
        k0 = kt * TK
        expand = (jnp.right_shift(k0 + lax.broadcasted_iota(jnp.int32, (LANES, TK), 1), SEL_SHIFT) == srow).astype(BF16)
        selb_scr[kt] = _dot(sel_add, expand)
        return carry

    lax.fori_loop(0, kt_hi, spread_selection, 0)

    def tile_bias(k0):
        return tz_ref[jnp.clip((q0 - k0) // TQ, 0, N_TZ - 1)].reshape(rows4, TK)

    def attend(k_ref, v_ref, kt_lo, mask_tile):
        mx_scr[...] = jnp.full(mx_scr.shape, NEG_INF, F32)

        def scores(near):
            def body(kt, carry):
                k0 = pl.multiple_of(kt * TK, TK)
                s = _dot_nt(qs, k_ref[pl.ds(k0, TK), :].astype(BF16))
                if near:
                    s = s + tile_bias(k0)
                s = mask_tile(k0, s, near)
                s_scr[kt] = s
                mx_scr[...] = jnp.maximum(mx_scr[...], jnp.maximum(s[:, 0:LANES], s[:, LANES:TK]))
                return carry
            return body

        kt_mid = jnp.maximum(kt_far, kt_lo)
        lax.fori_loop(kt_lo, kt_mid, scores(False), 0)
        lax.fori_loop(kt_mid, kt_hi, scores(True), 0)
        m = jnp.max(mx_scr[...], axis=1, keepdims=True)
        mx_scr[...] = jnp.broadcast_to(jnp.where(m > NEG_INF, m, 0.0), mx_scr.shape)
        l_scr[...] = jnp.zeros(l_scr.shape, F32)
        acc_scr[...] = jnp.zeros(acc_scr.shape, F32)

        def probs(kt, carry):
            k0 = pl.multiple_of(kt * TK, TK)
            m_rep = mx_scr[...]
            p = jnp.exp(s_scr[kt] - jnp.concatenate([m_rep, m_rep], axis=1))
            l_scr[...] += p[:, 0:LANES] + p[:, LANES:TK]
            acc_scr[...] += _dot(p.astype(BF16), v_ref[pl.ds(k0, TK), :].astype(BF16))
            return carry

        lax.fori_loop(kt_lo, kt_hi, probs, 0)
        l = jnp.sum(l_scr[...], axis=1, keepdims=True)
        return acc_scr[...] / jnp.where(l > 0, l, 1.0)

    def sel_mask(k0, s, near):
        s = (s.reshape(GROUP, TQ, TK) + selb_scr[k0 // TK][None]).reshape(rows4, TK)
        return jnp.where(k0 + kidx <= qpos, s, NEG_INF) if near else s

    o_s = attend(ks_ref, vs_ref, 0, sel_mask)

    def win_mask(k0, s, near):
        dist = qpos - (k0 + kidx)
        return jnp.where((dist >= 0) & (dist <= WINDOW), s, NEG_INF)

    o_w = attend(kw_ref, vw_ref, jnp.maximum(q0 - WINDOW, 0) // TK, win_mask)

    gt = gt_ref[...] + gb_ref[...]
    for j in range(GROUP):
        col = GATE_NG + (g * GROUP + j) * 3
        r = slice(j * TQ, (j + 1) * TQ)
        o_ref[:, j * HEAD_DIM:(j + 1) * HEAD_DIM] = (
            _gate(gt, col) * o_c[r] + _gate(gt, col + 1) * o_s[r] + _gate(gt, col + 2) * o_w[r])


def _nsa_prompt(z, kcvc, bias_c, tz, gate_b):
    b, t, _ = z.shape
    ncp = kcvc.shape[2]
    nq = t // TQ
    kv = lambda slot: pl.BlockSpec((None, t, HEAD_DIM), lambda bi, g, qi: (bi, 0, C_NKV // HEAD_DIM + slot * KV_HEADS + g))
    cmp_blk = lambda slot: pl.BlockSpec((None, None, ncp, HEAD_DIM), lambda bi, g, qi: (bi, slot * KV_HEADS + g, 0, 0))
    return pl.pallas_call(
        functools.partial(_nsa_prompt_kernel, t_len=t),
        grid=(b, KV_HEADS, nq),
        in_specs=[pl.BlockSpec((None, TQ, GROUP * HEAD_DIM), lambda bi, g, qi: (bi, qi, C_NQ // (GROUP * HEAD_DIM) + g)),
                  kv(2), kv(3), kv(4), kv(5), cmp_blk(0), cmp_blk(1),
                  pl.BlockSpec((GROUP, TQ, ncp), lambda bi, g, qi: (g, qi, 0)),
                  pl.BlockSpec((N_TZ, GROUP, TQ, TK), lambda bi, g, qi: (0, g, 0, 0)),
                  pl.BlockSpec((None, TQ, LANES), lambda bi, g, qi: (bi, qi, C_GATE // LANES)),
                  pl.BlockSpec((1, LANES), lambda bi, g, qi: (0, 0))],
        out_specs=pl.BlockSpec((None, TQ, GROUP * HEAD_DIM), lambda bi, g, qi: (bi, qi, g)),
        out_shape=jax.ShapeDtypeStruct((b, t, D_NSA), F32),
        scratch_shapes=[pltpu.VMEM((t // TK, GROUP * TQ, TK), F32), pltpu.VMEM((t // TK, TQ, TK), F32),
                        pltpu.VMEM((GROUP * TQ, LANES), F32),
                        pltpu.VMEM((GROUP * TQ, LANES), F32), pltpu.VMEM((GROUP * TQ, HEAD_DIM), F32)],
        compiler_params=_cparams("parallel", "parallel", "arbitrary"),
        name="nsa_prompt",
    )(z, z, z, z, z, kcvc, kcvc, bias_c, tz, z, gate_b)


TS = SUBLANES
ROWS_S = GROUP * TS


def _nsa_s_cmp_kernel(q_ref, kc_ref, vc_ref, bc_ref, oc_ref, sel_ref, *, pos0, n_cmp, n_sel):
    nch = kc_ref.shape[0]
    qs = _stack_heads(q_ref[...] * QK_SCALE).astype(BF16)
    qpos = pos0 + (lax.broadcasted_iota(jnp.int32, (ROWS_S, 1), 0) & (TS - 1))
    n_idx = lax.broadcasted_iota(jnp.int32, (1, nch), 1)
    s_c = _dot_nt(qs, kc_ref[...].astype(BF16)) + bc_ref[...].reshape(ROWS_S, nch)
    cmask = (qpos - (n_idx * CMP_STRIDE + CMP_BLOCK - 1) >= 0) & (n_idx < n_cmp)
    p_c = _masked_probs(s_c, cmask)
    oc_ref[...] = _dot(p_c.astype(BF16), vc_ref[...].astype(BF16))
    pc = p_c[0:TS] + p_c[TS:2 * TS] + p_c[2 * TS:3 * TS] + p_c[3 * TS:4 * TS]
    sel_ref[...] = _select_blocks_cols(pc, jnp.right_shift(qpos[0:TS], SEL_SHIFT), n_cmp, n_sel)


def _nsa_sample_cmp(z8, kcvc, bias_c, *, pos0, n_cmp, n_sel):
    b = z8.shape[0]
    nch = kcvc.shape[2]
    nsp = -(-n_sel // LANES) * LANES
    cmp_blk = lambda slot: pl.BlockSpec((None, None, nch, HEAD_DIM), lambda bi, g: (bi, slot * KV_HEADS + g, 0, 0))
    return pl.pallas_call(
        functools.partial(_nsa_s_cmp_kernel, pos0=pos0, n_cmp=n_cmp, n_sel=n_sel),
        grid=(b, KV_HEADS),
        in_specs=[pl.BlockSpec((None, TS, GROUP * HEAD_DIM), lambda bi, g: (bi, 0, C_NQ // (GROUP * HEAD_DIM) + g)),
                  cmp_blk(0), cmp_blk(1),
                  pl.BlockSpec((GROUP, TS, nch), lambda bi, g: (g, 0, 0))],
        out_specs=[pl.BlockSpec((None, None, ROWS_S, HEAD_DIM), lambda bi, g: (bi, g, 0, 0)),
                   pl.BlockSpec((None, None, TS, nsp), lambda bi, g: (bi, g, 0, 0))],
        out_shape=[jax.ShapeDtypeStruct((b, KV_HEADS, ROWS_S, HEAD_DIM), F32),
                   jax.ShapeDtypeStruct((b, KV_HEADS, TS, nsp), F32)],
        compiler_params=_cparams("parallel", "parallel"),
        name="nsa_sample_cmp",
    )(z8, kcvc, kcvc, bias_c)


def _pad_rows(x, n):
    return jnp.concatenate([x, jnp.zeros((n - x.shape[0], x.shape[1]), x.dtype)], axis=0)


def _nsa_s_sel_kernel(pt_ref, q_ref, sel_ref, kn_ref, vn_ref, bt_ref, bn_ref, bf_ref, *refs, pos0, npages):
    del pt_ref
    rows_pp = PAGE_SIZE * KV_HEADS
    k_refs = [r.reshape(rows_pp, HEAD_DIM) for r in refs[:SEL_PAGES]]
    v_refs = [r.reshape(rows_pp, HEAD_DIM) for r in refs[SEL_PAGES:2 * SEL_PAGES]]
    o_ref, kbuf, vbuf, m_scr, l_scr, acc_scr = refs[2 * SEL_PAGES:]
    step = pl.program_id(1)
    last = step == npages // SEL_PAGES - 1
    rows = KV_HEADS * ROWS_S
    ncol = SEL_TILE * rows_pp
    head_shift = KV_HEADS.bit_length() - 1
    blk_per_page = PAGE_SIZE // SEL_BLOCK

    @pl.when(step == 0)
    def _():
        _softmax_init(m_scr, l_scr, acc_scr)

    nsp = sel_ref.shape[-1]
    ridx = lax.broadcasted_iota(jnp.int32, (rows, 1), 0)
    qpos = pos0 + (ridx & (TS - 1))
    qs = jnp.concatenate([_stack_heads(q_ref[:, g * GROUP * HEAD_DIM:(g + 1) * GROUP * HEAD_DIM] * QK_SCALE)
                          for g in range(KV_HEADS)], axis=0).astype(BF16)
    sel_rows = jnp.concatenate([sel_ref[g] for g in range(KV_HEADS) for _ in range(GROUP)], axis=0)
    bias_far = bf_ref[...].reshape(rows, PAGE_SIZE)[:, 0:1]
    col = lax.broadcasted_iota(jnp.int32, (1, ncol), 1)
    own_head = (col & (KV_HEADS - 1)) == ridx // ROWS_S
    expand = (jnp.right_shift(lax.broadcasted_iota(jnp.int32, (LANES, ncol), 1), SEL_SHIFT + head_shift)
              == lax.broadcasted_iota(jnp.int32, (LANES, ncol), 0)).astype(BF16)
    blk_r = lax.broadcasted_iota(jnp.int32, (nsp, LANES), 0)
    blk_c = lax.broadcasted_iota(jnp.int32, (nsp, LANES), 1)
    n_tiles = SEL_PAGES // SEL_TILE

    for tile in range(n_tiles):
        for i in range(SEL_TILE):
            p = tile * SEL_TILE + i
            kbuf[i * rows_pp:(i + 1) * rows_pp, :] = k_refs[p][...].astype(BF16)
            vbuf[i * rows_pp:(i + 1) * rows_pp, :] = v_refs[p][...].astype(BF16)
        page0 = step * SEL_PAGES + tile * SEL_TILE
        window = ((blk_r == page0 * blk_per_page + blk_c) & (blk_c < SEL_TILE * blk_per_page)).astype(BF16)
        chosen = _dot(_dot(sel_rows.astype(BF16), window).astype(BF16), expand) > 0.5
        s = _dot_nt(qs, kbuf[...]) + bias_far
        if tile == n_tiles - 1:
            tail = jnp.where(last, bt_ref[...].reshape(rows, rows_pp) - bias_far, 0.0)
            s = s + jnp.concatenate([jnp.zeros((rows, ncol - rows_pp), F32), tail], axis=1)
        key = page0 * PAGE_SIZE + jnp.right_shift(col, head_shift)
        _softmax_update(s, chosen & own_head & (key <= qpos), vbuf[...], m_scr, l_scr, acc_scr)

    @pl.when(last)
    def _():
        kidx = lax.broadcasted_iota(jnp.int32, (1, PAGE_SIZE), 1)
        new_blk = lax.broadcasted_iota(jnp.int32, (1, nsp), 1) == npages * blk_per_page
        flag = jnp.sum(jnp.where(new_blk, sel_rows, 0.0), axis=1, keepdims=True) > 0.5
        bias_new = bn_ref[...].reshape(rows, PAGE_SIZE)
        for g in range(KV_HEADS):
            r = slice(g * ROWS_S, (g + 1) * ROWS_S)
            kn = _pad_rows(kn_ref[:, g * HEAD_DIM:(g + 1) * HEAD_DIM], PAGE_SIZE).astype(BF16)
            vn = _pad_rows(vn_ref[:, g * HEAD_DIM:(g + 1) * HEAD_DIM], PAGE_SIZE).astype(BF16)
            mask = flag[r] & (pos0 + kidx <= qpos[r]) & (kidx < TS)
            _softmax_update(_dot_nt(qs[r], kn) + bias_new[r], mask, vn, m_scr.at[r], l_scr.at[r], acc_scr.at[r])
            o_ref[g] = _softmax_result(l_scr.at[r], acc_scr.at[r])


def _nsa_sample_sel(z8, sel, bias_tail, bias_new, bias_far, cache6, page_table, layer, *, pos0):
    b, npages = page_table.shape
    nsp = sel.shape[-1]
    steps = npages // SEL_PAGES
    slot_cols = KV_HEADS * HEAD_DIM
    new_rows = lambda slot: pl.BlockSpec((None, TS, slot_cols),
                                         lambda bi, si, pt: (bi, 0, (C_NKV + slot * slot_cols) // slot_cols))
    whole = lambda a: pl.BlockSpec(a.shape, lambda bi, si, pt: (0,) * a.ndim)
    grid_spec = pltpu.PrefetchScalarGridSpec(
        num_scalar_prefetch=1,
        grid=(b, steps),
        in_specs=[pl.BlockSpec((None, TS, D_NSA), lambda bi, si, pt: (bi, 0, C_NQ // D_NSA)),
                  pl.BlockSpec((None, KV_HEADS, TS, nsp), lambda bi, si, pt: (bi, 0, 0, 0)),
                  new_rows(2), new_rows(3), whole(bias_tail), whole(bias_new), whole(bias_far)]
                 + [_page_spec(layer, 2, p, SEL_PAGES, 2) for p in range(SEL_PAGES)]
                 + [_page_spec(layer, 3, p, SEL_PAGES, 2) for p in range(SEL_PAGES)],
        out_specs=pl.BlockSpec((None, KV_HEADS, ROWS_S, HEAD_DIM), lambda bi, si, pt: (bi, 0, 0, 0)),
        scratch_shapes=[pltpu.VMEM((SEL_TILE * PAGE_SIZE * KV_HEADS, HEAD_DIM), BF16),
                        pltpu.VMEM((SEL_TILE * PAGE_SIZE * KV_HEADS, HEAD_DIM), BF16),
                        pltpu.VMEM((KV_HEADS * ROWS_S, 1), F32), pltpu.VMEM((KV_HEADS * ROWS_S, 1), F32),
                        pltpu.VMEM((KV_HEADS * ROWS_S, HEAD_DIM), F32)],
    )
    return pl.pallas_call(
        functools.partial(_nsa_s_sel_kernel, pos0=pos0, npages=npages),
        grid_spec=grid_spec,
        out_shape=jax.ShapeDtypeStruct((b, KV_HEADS, ROWS_S, HEAD_DIM), F32),
        compiler_params=_cparams("parallel", "arbitrary"),
        name="nsa_sample_sel",
    )(page_table, z8, sel, z8, z8, bias_tail, bias_new, bias_far, *([cache6] * (2 * SEL_PAGES)))


def _nsa_s_win_kernel(q_ref, kw_ref, vw_ref, kn_ref, vn_ref, bw_ref, oc_ref, os_ref, gt_ref, gb_ref, o_ref,
                      m_scr, l_scr, acc_scr, *, pos0):
    g = pl.program_id(1)
    wb = kw_ref.shape[0]
    own = pl.ds(g, wb, stride=KV_HEADS)
    kw = kw_ref.reshape(wb * KV_HEADS, HEAD_DIM)[own, :]
    vw = vw_ref.reshape(wb * KV_HEADS, HEAD_DIM)[own, :]
    qs = _stack_heads(q_ref[...] * QK_SCALE).astype(BF16)
    qpos = pos0 + (lax.broadcasted_iota(jnp.int32, (ROWS_S, 1), 0) & (TS - 1))
    bias = bw_ref[...].reshape(ROWS_S, wb + PAGE_SIZE)
    _softmax_init(m_scr, l_scr, acc_scr)
    dist = qpos - (pos0 - wb + lax.broadcasted_iota(jnp.int32, (1, wb), 1))
    s = _dot_nt(qs, kw.astype(BF16)) + bias[:, 0:wb]
    _softmax_update(s, (dist >= 0) & (dist <= WINDOW), vw.astype(BF16), m_scr, l_scr, acc_scr)
    kidx = lax.broadcasted_iota(jnp.int32, (1, PAGE_SIZE), 1)
    dist = qpos - (pos0 + kidx)
    s = _dot_nt(qs, _pad_rows(kn_ref[...], PAGE_SIZE).astype(BF16)) + bias[:, wb:wb + PAGE_SIZE]
    _softmax_update(s, (dist >= 0) & (dist <= WINDOW) & (kidx < TS), _pad_rows(vn_ref[...], PAGE_SIZE).astype(BF16),
                    m_scr, l_scr, acc_scr)
    o_w = _softmax_result(l_scr, acc_scr)
    o_c = oc_ref[...]
    o_s = os_ref[...]
    gt = gt_ref[...] + gb_ref[...]
    for j in range(GROUP):
        col = GATE_NG + (g * GROUP + j) * 3
        r = slice(j * TS, (j + 1) * TS)
        o_ref[:, j * HEAD_DIM:(j + 1) * HEAD_DIM] = (
            _gate(gt, col) * o_c[r] + _gate(gt, col + 1) * o_s[r] + _gate(gt, col + 2) * o_w[r])


def _nsa_sample_win(z8, win6, layer, bias_w, o_c, o_s, gate_b, *, pos0):
    _, b, wb = win6.shape[:3]
    cached = lambda slot: pl.BlockSpec((None, None, wb, None, KV_HEADS, HEAD_DIM),
                                       lambda bi, g: (layer, bi, 0, slot, 0, 0))
    zcol = lambda col: (lambda bi, g: (bi, 0, col // HEAD_DIM + g))
    part = pl.BlockSpec((None, None, ROWS_S, HEAD_DIM), lambda bi, g: (bi, g, 0, 0))
    return pl.pallas_call(
        functools.partial(_nsa_s_win_kernel, pos0=pos0),
        grid=(b, KV_HEADS),
        in_specs=[pl.BlockSpec((None, TS, GROUP * HEAD_DIM), lambda bi, g: (bi, 0, C_NQ // (GROUP * HEAD_DIM) + g)),
                  cached(0), cached(1),
                  pl.BlockSpec((None, TS, HEAD_DIM), zcol(C_NKV + 4 * KV_HEADS * HEAD_DIM)),
                  pl.BlockSpec((None, TS, HEAD_DIM), zcol(C_NKV + 5 * KV_HEADS * HEAD_DIM)),
                  pl.BlockSpec((GROUP, TS, wb + PAGE_SIZE), lambda bi, g: (g, 0, 0)),
                  part, part,
                  pl.BlockSpec((None, TS, LANES), lambda bi, g: (bi, 0, C_GATE // LANES)),
                  pl.BlockSpec((1, LANES), lambda bi, g: (0, 0))],
        out_specs=pl.BlockSpec((None, TS, GROUP * HEAD_DIM), lambda bi, g: (bi, 0, g)),
        out_shape=jax.ShapeDtypeStruct((b, TS, D_NSA), F32),
        scratch_shapes=[pltpu.VMEM((ROWS_S, 1), F32), pltpu.VMEM((ROWS_S, 1), F32), pltpu.VMEM((ROWS_S, HEAD_DIM), F32)],
        compiler_params=_cparams("parallel", "parallel"),
        name="nsa_sample_win",
    )(z8, win6, win6, z8, z8, bias_w, o_c, o_s, z8, gate_b)


def _t5_bucket(dist):
    n = np.maximum(dist, 0)
    exact = NUM_BUCKETS // 2
    nf = np.maximum(n, 1).astype(np.float32)
    large = exact + (np.log(nf / np.float32(exact)) / np.float32(math.log(MAX_DISTANCE / exact))
                     * np.float32(NUM_BUCKETS - exact)).astype(np.int32)
    return np.where(n < exact, n, np.minimum(large, NUM_BUCKETS - 1)).astype(np.int32)


FAR_DIST = int(np.max(np.nonzero(_t5_bucket(np.arange(4 * MAX_DISTANCE)) < NUM_BUCKETS - 1)[0])) + 1
N_TZ = (FAR_DIST + TK - 2) // TQ + 1


def _bias_table(rel_bias, dist):
    onehot = jax.nn.one_hot(jnp.asarray(_t5_bucket(dist).astype(np.int8)), NUM_BUCKETS, dtype=F32)
    return jnp.einsum("rcb,bh->hrc", onehot, rel_bias, precision=lax.Precision.HIGHEST)


def _pack_layer(p, l):
    w_in = p["w_in"][l]
    pad = jnp.zeros((D_MODEL, N_IN - C_GATE - 32), F32)
    w_in_p = jnp.concatenate([w_in[:, 0:2048], w_in[:, 2056:3080], w_in[:, 3080:4616], w_in[:, 4640:5152],
                              w_in[:, 2048:2056], w_in[:, 4616:4640], pad], axis=1).astype(BF16)
    gb = p["mlstm_gate_b"][l]
    mlstm_gate_b = jnp.zeros((1, LANES), F32).at[0, GATE_MI:GATE_MI + 4].set(gb[0]).at[0, GATE_MF:GATE_MF + 4].set(gb[1])
    nsa_gate_b = jnp.zeros((1, LANES), F32).at[0, GATE_NG:GATE_NG + 3 * NSA_HEADS].set(p["nsa_gate_b"][l].reshape(-1))
    w1 = p["cmp_w1"][l]
    wcat = jnp.concatenate([w1[:, 0:CMP_STRIDE], w1[:, CMP_STRIDE:CMP_BLOCK]], axis=-1).astype(BF16)
    lam = lax.complex(p["s5_a_re"][l], p["s5_a_im"][l])
    lam_bar = jnp.exp(lam * jnp.exp(p["s5_log_step"][l])[:, None])
    b_bar = ((lam_bar - 1.0) / lam)[..., None] * lax.complex(p["s5_b_re"][l], p["s5_b_im"][l])
    gi = S5_GROUPS // S5_IN_TILES
    bd_in = lambda m: jnp.einsum("cgph,gk->cghkp", m.reshape(S5_IN_TILES, gi, S5_STATE, S5_GROUP_WIDTH),
                                 jnp.eye(gi, dtype=F32)).reshape(S5_IN_TILES, S5_IN_FEATS, 256)
    wb = jnp.concatenate([bd_in(b_bar.real), bd_in(b_bar.imag)], axis=2).astype(BF16)
    go = S5_GROUPS // S5_OUT_TILES
    bd_out = lambda m: jnp.einsum("jghp,gk->jgpkh", m.reshape(S5_OUT_TILES, go, S5_GROUP_WIDTH, S5_STATE),
                                  jnp.eye(go, dtype=F32)).reshape(S5_OUT_TILES, S5_OUT_CH, 256)
    wc = jnp.stack([bd_out(p["s5_c_re"][l]), -bd_out(p["s5_c_im"][l])]).astype(BF16)
    half = lambda w: jnp.pad(w, ((0, 0), (0, D_FFP - D_FF)))
    w_up = p["ffn_w_up"][l]
    conv_w = p["ffn_conv_w"][l]
    conv_b = p["ffn_conv_b"][l][None, :]
    return dict(
        w_in=w_in_p, mlstm_gate_b=mlstm_gate_b, mlstm_norm_w=p["mlstm_norm_w"][l][None, :], nsa_gate_b=nsa_gate_b,
        wcat=wcat, cmp_b1=p["cmp_b1"][l], cmp_w2=p["cmp_w2"][l].astype(BF16),
        s5_wb=wb, s5_wc=wc, s5_lam_r=lam_bar.real.reshape(1, S5_CH), s5_lam_i=lam_bar.imag.reshape(1, S5_CH),
        s5_d=p["s5_d"][l][None, :], s5_glu_w=p["s5_glu_w"][l].astype(BF16), s5_glu_b=p["s5_glu_b"][l][None, :],
        w_out=p["w_out"][l].astype(BF16), ln1_w=p["ln1_w"][l][None, :], ln1_b=p["ln1_b"][l][None, :],
        w_up=jnp.pad(w_up.astype(BF16), ((0, 0), (0, D_FFP - D_FF))),
        conv_w8=jnp.pad(jnp.concatenate([half(conv_w[:, :D_FF]), half(conv_w[:, D_FF:])], axis=1),
                        ((0, SUBLANES - CONV_W), (0, 0))),
        conv_b=jnp.concatenate([half(conv_b[:, :D_FF]), half(conv_b[:, D_FF:])], axis=1),
        w_down=p["ffn_w_down"][l].astype(BF16),
        ln2_w=p["ln2_w"][l][None, :], ln2_b=p["ln2_b"][l][None, :],
    )


def _unpad_ff(x):
    return jnp.concatenate([x[..., :D_FF], x[..., D_FFP:D_FFP + D_FF]], axis=-1)


def _pad_ff(x):
    pad = [(0, 0)] * (x.ndim - 1) + [(0, D_FFP - D_FF)]
    return jnp.concatenate([jnp.pad(x[..., :D_FF], pad), jnp.pad(x[..., D_FF:], pad)], axis=-1)


def _mixer_tail(x2d, z, b, t, t_use, lw, o_mlstm, o_nsa, s5_state, conv_state8, tm):
    pad8 = lambda s: jnp.pad(s.reshape(b, S5_CH), ((0, SUBLANES - b), (0, 0)))
    o_s5, xr, xi = _s5(z.reshape(b, t, N_IN), lw["s5_wb"], lw["s5_wc"], lw["s5_lam_r"], lw["s5_lam_i"], lw["s5_d"],
                       lw["s5_glu_w"], lw["s5_glu_b"], pad8(s5_state[0]), pad8(s5_state[1]), lc=S5_LC, steps=t_use)
    o_s5 = o_s5.reshape(b * t, D_S5)
    x1 =_wout_ln(o_mlstm.reshape(b * t, D_MLSTM), o_nsa.reshape(b * t, D_NSA), o_s5, x2d,
                  lw["w_out"], lw["ln1_w"], lw["ln1_b"], tm=256)
    s5_new = (xr[:b].reshape(b, S5_GROUPS, S5_STATE), xi[:b].reshape(b, S5_GROUPS, S5_STATE))
    if conv_state8 is None:
        x2, tail_a, tail_g = _ffn_prompt(x1.reshape(b, t, D_MODEL), lw["w_up"], lw["w_down"], lw["conv_w8"],
                                         lw["conv_b"], lw["ln2_w"], lw["ln2_b"], tt=FFN_TT, tf=FFN_TF)
        keep = slice(SUBLANES - (CONV_W - 1), SUBLANES)
        conv_new = jnp.concatenate([tail_a[:, keep, :D_FF], tail_g[:, keep, :D_FF]], axis=-1)
        return x2.reshape(b * t, D_MODEL), s5_new, conv_new
    up = _matmul(x1, lw["w_up"], tm=tm, tn=FFN_TF, n=2 * D_FFP, w_cols=_up_cols(FFN_TF))
    hgate = _convgate(up.reshape(b, t, 2 * D_FFP), conv_state8, lw["conv_w8"], lw["conv_b"], tt=256, tf=512)
    x2 = _down_ln(hgate.reshape(b * t, D_FFP), lw["w_down"], x1, lw["ln2_w"], lw["ln2_b"], tm=512, tk=1408)
    conv_new = _unpad_ff(up.reshape(b, t, 2 * D_FFP)[:, t_use - (CONV_W - 1):t_use])
    return x2, s5_new, conv_new


def _prompt_layer(x2d, b, t, lw, rel_bias):
    z, kv_rows, win_rows = _proj_in(x2d, lw["w_in"], tm=min(1024, b * t), tn=768)
    z3 = z.reshape(b, t, N_IN)
    zeros = lambda *s: jnp.zeros(s, F32)
    o_mlstm, c1, n1, m1 = _mlstm(z3, lw["mlstm_gate_b"], lw["mlstm_norm_w"],
                                 zeros(b, MLSTM_HEADS, HEAD_DIM, HEAD_DIM), zeros(b, MLSTM_HEADS, LANES),
                                 zeros(b, MLSTM_HEADS, LANES), L=MLSTM_L, lin=MLSTM_L, t_valid=t)
    kcvc = _cmp_finish(_cmp_project_prompt(z3, lw["wcat"]), lw["cmp_b1"], lw["cmp_w2"])
    ncp = t // CMP_STRIDE
    bias_c = _bias_table(rel_bias, np.arange(t)[:, None] - (np.arange(ncp) * CMP_STRIDE + CMP_BLOCK - 1)[None, :])
    ti = np.arange(TQ)[:, None] - np.arange(TK)[None, :]
    far = _bias_table(rel_bias, np.full((1, 1), FAR_DIST))
    tz = jnp.stack([_bias_table(rel_bias, d * TQ + ti) - far for d in range(N_TZ)])
    o_nsa = _nsa_prompt(z3, kcvc, bias_c, tz, lw["nsa_gate_b"])
    x2, s5_new, conv_new = _mixer_tail(
        x2d, z, b, t, t, lw, o_mlstm, o_nsa, (zeros(b, S5_GROUPS, S5_STATE), zeros(b, S5_GROUPS, S5_STATE)),
        None, tm=1024)
    wrows = min(WINDOW, t)
    state = (kv_rows.reshape(b, t, 4, KV_HEADS, HEAD_DIM), win_rows.reshape(b, t, 2, KV_HEADS, HEAD_DIM)[:, t - wrows:],
             c1, n1, m1[:, :, 0], s5_new[0], s5_new[1], conv_new)
    return x2, state


def _sample_layer(x2d, b, tn, lw, rel_bias, layer, cache6, page_table, win6, mlstm_state, s5_state, conv_state):
    npages = page_table.shape[1]
    pos0 = npages * PAGE_SIZE
    z = _matmul(x2d, lw["w_in"], tm=1024, tn=768)
    z8 = z.reshape(b, TS, N_IN)
    c0, n0, m0 = mlstm_state
    o_mlstm, c1, n1, m1 = _mlstm(z8, lw["mlstm_gate_b"], lw["mlstm_norm_w"], c0, n0,
                                 jnp.broadcast_to(m0[:, :, None], (b, MLSTM_HEADS, LANES)),
                                 L=LANES, lin=TS, t_valid=tn)
    n_chunks = (pos0 + tn) // CMP_STRIDE
    n_cmp = n_chunks - CMP_BLOCK // CMP_STRIDE + 1
    n_sel = -(-(pos0 + tn) // SEL_BLOCK)
    kcvc = _cmp_finish(_cmp_project_sample(cache6, page_table, layer, lw["wcat"]), lw["cmp_b1"], lw["cmp_w2"])
    qpos = pos0 + np.arange(TS)[:, None]
    bias_c = _bias_table(rel_bias, qpos - (np.arange(n_chunks) * CMP_STRIDE + CMP_BLOCK - 1)[None, :])
    o_c, sel = _nsa_sample_cmp(z8, kcvc, bias_c, pos0=pos0, n_cmp=n_cmp, n_sel=n_sel)
    kk = np.arange(PAGE_SIZE)[None, :]
    bias_tail = _bias_table(rel_bias, np.repeat(qpos - (pos0 - PAGE_SIZE + kk), KV_HEADS, axis=1))
    bias_new = _bias_table(rel_bias, qpos - (pos0 + kk))
    bias_far = _bias_table(rel_bias, np.broadcast_to(qpos - (pos0 - 2 * PAGE_SIZE), (TS, PAGE_SIZE)))
    o_s = _nsa_sample_sel(z8, sel, bias_tail, bias_new, bias_far, cache6, page_table, layer, pos0=pos0)
    wb = win6.shape[2]
    wk = np.arange(wb)[None, :]
    bias_w = _bias_table(rel_bias, np.concatenate([qpos - (pos0 - wb + wk), qpos - (pos0 + kk)], axis=1))
    o_nsa = _nsa_sample_win(z8, win6, layer, bias_w, o_c, o_s, lw["nsa_gate_b"], pos0=pos0)
    conv_state8 = jnp.pad(_pad_ff(conv_state), ((0, 0), (SUBLANES - (CONV_W - 1), 0), (0, 0)))
    x2, s5_new, conv_new = _mixer_tail(x2d, z, b, TS, tn, lw, o_mlstm, o_nsa, s5_state, conv_state8, tm=1024)
    nkv = z8[:, :tn, C_NKV:C_NKV + N_KV_SLOTS * KV_HEADS * HEAD_DIM].reshape(b, tn, N_KV_SLOTS, KV_HEADS, HEAD_DIM)
    state = (nkv[:, :, :4], nkv[:, :, 4:], c1, n1, m1[:, :, 0], s5_new[0], s5_new[1], conv_new)
    return x2, state


def kernel(x_prompt, x_sample, cache_nsa_kv, cache_win_kv, state_mlstm_c, state_mlstm_n, state_mlstm_m,
           state_s5_re, state_s5_im, state_ffn_conv, page_table, w_in, mlstm_gate_b, mlstm_norm_w,
           nsa_gate_b, cmp_w1, cmp_b1, cmp_w2, rel_bias, s5_a_re, s5_a_im, s5_b_re, s5_b_im, s5_c_re,
           s5_c_im, s5_d, s5_log_step, s5_glu_w, s5_glu_b, w_out, ln1_w, ln1_b, ffn_w_up, ffn_conv_w,
           ffn_conv_b, ffn_w_down, ln2_w, ln2_b):
    params = dict(w_in=w_in, mlstm_gate_b=mlstm_gate_b, mlstm_norm_w=mlstm_norm_w, nsa_gate_b=nsa_gate_b,
                  cmp_w1=cmp_w1, cmp_b1=cmp_b1, cmp_w2=cmp_w2, s5_a_re=s5_a_re, s5_a_im=s5_a_im, s5_b_re=s5_b_re,
                  s5_b_im=s5_b_im, s5_c_re=s5_c_re, s5_c_im=s5_c_im, s5_d=s5_d, s5_log_step=s5_log_step,
                  s5_glu_w=s5_glu_w, s5_glu_b=s5_glu_b, w_out=w_out, ln1_w=ln1_w, ln1_b=ln1_b, ffn_w_up=ffn_w_up,
                  ffn_conv_w=ffn_conv_w, ffn_conv_b=ffn_conv_b, ffn_w_down=ffn_w_down, ln2_w=ln2_w, ln2_b=ln2_b)
    depth = w_in.shape[0]
    bp, tp, _ = x_prompt.shape
    bs, tn, _ = x_sample.shape
    assert tp % TK == 0 and tn < CMP_STRIDE and tn <= TS and SUBLANES % bp == 0 and SUBLANES % bs == 0
    assert (tn * bs) % SUBLANES == 0 and tp % min(FFN_TT, tp) == 0
    assert page_table.shape[1] % CMP_PAGES == 0 and cache_nsa_kv.shape[2] == PAGE_SIZE
    xp = x_prompt.reshape(bp * tp, D_MODEL)
    xs = jnp.pad(x_sample, ((0, 0), (0, TS - tn), (0, 0))).reshape(bs * TS, D_MODEL)
    p_states, s_states = [], []
    for l in range(depth):
        lw = _pack_layer(params, l)
        xp, sp = _prompt_layer(xp, bp, tp, lw, rel_bias)
        xs, ss = _sample_layer(xs, bs, tn, lw, rel_bias, l, cache_nsa_kv, page_table, cache_win_kv,
                               (state_mlstm_c[l], jnp.pad(state_mlstm_n[l], ((0, 0), (0, 0), (0, LANES - HEAD_DIM))),
                                state_mlstm_m[l]),
                               (state_s5_re[l], state_s5_im[l]), state_ffn_conv[l])
        p_states.append(sp)
        s_states.append(ss)
    stk = lambda states, i: jnp.stack([s[i] for s in states])
    y_prompt = xp.reshape(bp, tp, D_MODEL)
    y_sample = xs.reshape(bs, TS, D_MODEL)[:, :tn]
    return (y_prompt, y_sample,
            stk(p_states, 0), stk(s_states, 0), stk(p_states, 1), stk(s_states, 1),
            stk(p_states, 2), stk(s_states, 2), stk(p_states, 3), stk(s_states, 3), stk(p_states, 4), stk(s_states, 4),
            stk(p_states, 5), stk(s_states, 5), stk(p_states, 6), stk(s_states, 6), stk(p_states, 7), stk(s_states, 7))
```

```python
import functools
import math

import jax
import jax.numpy as jnp
import numpy as np
from jax import lax
from jax.experimental import pallas as pl
from jax.experimental.pallas import tpu as pltpu

F32 = jnp.float32
BF16 = jnp.bfloat16
NEG_INF = float("-inf")
M_INIT = -1e30
MASKED = -1e30

D_MODEL = 2048
PAGE_SIZE = 128
D_MLSTM = D_MODEL // 4
D_NSA = D_MODEL // 2
D_S5 = D_MODEL - D_MLSTM - D_NSA
HEAD_DIM = 128
MLSTM_HEADS = D_MLSTM // HEAD_DIM
NSA_HEADS = D_NSA // HEAD_DIM
KV_HEADS = 2
GROUP = NSA_HEADS // KV_HEADS
N_KV_SLOTS = 6
CMP_BLOCK = 32
CMP_STRIDE = 16
SEL_BLOCK = 64
SEL_TOPK = 16
FORCE_SCORE = 1e4
WINDOW = 512
S5_GROUP_WIDTH = 16
S5_GROUPS = D_S5 // S5_GROUP_WIDTH
S5_STATE = 64
S5_CH = S5_GROUPS * S5_STATE
NUM_BUCKETS = 32
MAX_DISTANCE = 128
D_FF = ((8 * D_MODEL // 3 + 127) // 128) * 128
CONV_W = 3
LN_EPS = 1e-5
DEPTH = 2
DEEPNORM_ALPHA = (2 * DEPTH) ** 0.25
QK_SCALE = HEAD_DIM ** -0.5

LANES = 128
SUBLANES = 8
V7X_VMEM_LIMIT = 56 * 2 ** 20

C_MQ, C_MK, C_MV, C_MO = 0, 512, 1024, 1536
C_NQ = 2048
C_NKV = 3072
C_SU = 4608
C_GATE = 5120
N_IN = 5376
GATE_MI, GATE_MF, GATE_NG = 0, 4, 8
D_FFP = 5632
MLSTM_L = 128
TQ = 256
TK = 256
S5_LC = 256
FFN_TT = 1024
FFN_TF = 512
FFN_HALO = 16
CMP_PAGES = 32
SEL_PAGES = 32
SEL_TILE = 32
SEL_SHIFT = SEL_BLOCK.bit_length() - 1


def _cparams(*sem):
    return pltpu.CompilerParams(dimension_semantics=sem, vmem_limit_bytes=V7X_VMEM_LIMIT)


def _dot(a, b):
    return jnp.dot(a, b, preferred_element_type=F32)


def _dot_nt(a, b):
    return lax.dot_general(a, b, (((1,), (1,)), ((), ())), preferred_element_type=F32)


def _layer_norm(y, w, b):
    mu = jnp.mean(y, axis=-1, keepdims=True)
    d = y - mu
    var = jnp.mean(d * d, axis=-1, keepdims=True)
    return d * lax.rsqrt(var + LN_EPS) * w + b


def _mm_kernel(x_ref, w_ref, o_ref):
    o_ref[...] = _dot(x_ref[...].astype(BF16), w_ref[...]).astype(o_ref.dtype)


def _up_cols(tf):
    nf, per = D_FFP // tf, tf // LANES
    return lambda j: (j * per + (j >= nf).astype(jnp.int32) * (D_FF // LANES - nf * per)) * LANES


def _matmul(x, w, *, tm, tn, n=None, w_cols=None):
    m, k = x.shape
    n = w.shape[1] if n is None else n
    tm = min(tm, m)
    if w_cols is None:
        w_spec = pl.BlockSpec((k, tn), lambda i, j: (0, j))
    else:
        w_spec = pl.BlockSpec((pl.Element(k), pl.Element(tn)), lambda i, j: (0, w_cols(j)))
    return pl.pallas_call(
        _mm_kernel,
        grid=(m // tm, n // tn),
        in_specs=[pl.BlockSpec((tm, k), lambda i, j: (i, 0)), w_spec],
        out_specs=pl.BlockSpec((tm, tn), lambda i, j: (i, j)),
        out_shape=jax.ShapeDtypeStruct((m, n), F32),
        compiler_params=_cparams("parallel", "parallel"),
        name="proj",
    )(x, w)


KV4_GROUPS = 4 * KV_HEADS
WIN_GROUPS = 2 * KV_HEADS


def _proj_in_kernel(x_ref, w_ref, z_ref, kv_ref, win_ref, *, tn):
    j = pl.program_id(1)
    acc = _dot(x_ref[...].astype(BF16), w_ref[...])
    z_ref[...] = acc
    tm = acc.shape[0]
    per_tile = tn // HEAD_DIM
    first_tile = C_NKV // tn
    for tile in range(first_tile, first_tile + N_KV_SLOTS * KV_HEADS // per_tile):
        @pl.when(j == tile)
        def _(tile=tile):
            for c in range(per_tile):
                grp = (tile - first_tile) * per_tile + c
                val = acc[:, c * HEAD_DIM:(c + 1) * HEAD_DIM]
                if grp < KV4_GROUPS:
                    kv_ref[pl.ds(grp, tm, stride=KV4_GROUPS), :] = val
                else:
                    win_ref[pl.ds(grp - KV4_GROUPS, tm, stride=WIN_GROUPS), :] = val


def _proj_in(x, w, *, tm, tn):
    m, k = x.shape
    n = w.shape[1]
    assert C_NKV % tn == 0 and (N_KV_SLOTS * KV_HEADS * HEAD_DIM) % tn == 0
    return pl.pallas_call(
        functools.partial(_proj_in_kernel, tn=tn),
        grid=(m // tm, n // tn),
        in_specs=[pl.BlockSpec((tm, k), lambda i, j: (i, 0)), pl.BlockSpec((k, tn), lambda i, j: (0, j))],
        out_specs=[pl.BlockSpec((tm, tn), lambda i, j: (i, j)),
                   pl.BlockSpec((tm * KV4_GROUPS, HEAD_DIM), lambda i, j: (i, 0)),
                   pl.BlockSpec((tm * WIN_GROUPS, HEAD_DIM), lambda i, j: (i, 0))],
        out_shape=[jax.ShapeDtypeStruct((m, n), F32),
                   jax.ShapeDtypeStruct((m * KV4_GROUPS, HEAD_DIM), F32),
                   jax.ShapeDtypeStruct((m * WIN_GROUPS, HEAD_DIM), F32)],
        compiler_params=_cparams("parallel", "arbitrary"),
        name="proj_in",
    )(x, w)


def _wout_kernel(om_ref, on_ref, os_ref, x_ref, w_ref, lw_ref, lb_ref, o_ref):
    acc = _dot(om_ref[...].astype(BF16), w_ref[0:D_MLSTM, :])
    acc += _dot(on_ref[...].astype(BF16), w_ref[D_MLSTM:D_MLSTM + D_NSA, :])
    acc += _dot(os_ref[...].astype(BF16), w_ref[D_MLSTM + D_NSA:D_MODEL, :])
    o_ref[...] = _layer_norm(DEEPNORM_ALPHA * x_ref[...] + acc, lw_ref[...], lb_ref[...])


def _wout_ln(om, on, os_, x, w, lw, lb, *, tm):
    m = x.shape[0]
    tm = min(tm, m)
    row = lambda width: pl.BlockSpec((tm, width), lambda i: (i, 0))
    full = lambda a: pl.BlockSpec(a.shape, lambda i: (0, 0))
    return pl.pallas_call(
        _wout_kernel,
        grid=(m // tm,),
        in_specs=[row(D_MLSTM), row(D_NSA), row(D_S5), row(D_MODEL), full(w), full(lw), full(lb)],
        out_specs=row(D_MODEL),
        out_shape=jax.ShapeDtypeStruct((m, D_MODEL), F32),
        compiler_params=_cparams("parallel"),
        name="wout_ln",
    )(om, on, os_, x, w, lw, lb)


def _valid_rows(w_ref, tile):
    rows = w_ref.shape[0]
    row = tile * rows + lax.broadcasted_iota(jnp.int32, (rows, 1), 0)
    w = w_ref[...]
    return jnp.where(row < D_FF, w, jnp.zeros_like(w))


def _down_kernel(h_ref, w_ref, x_ref, lw_ref, lb_ref, o_ref, acc_ref, *, nk):
    k = pl.program_id(1)

    @pl.when(k == 0)
    def _():
        acc_ref[...] = jnp.zeros_like(acc_ref)

    acc_ref[...] += _dot(h_ref[...], _valid_rows(w_ref, k))

    @pl.when(k == nk - 1)
    def _():
        o_ref[...] = _layer_norm(DEEPNORM_ALPHA * x_ref[...] + acc_ref[...], lw_ref[...], lb_ref[...])


def _down_ln(h, w, x, lw, lb, *, tm, tk):
    m, kk = h.shape
    tm = min(tm, m)
    nk = kk // tk
    return pl.pallas_call(
        functools.partial(_down_kernel, nk=nk),
        grid=(m // tm, nk),
        in_specs=[
            pl.BlockSpec((tm, tk), lambda i, k: (i, k)),
            pl.BlockSpec((tk, D_MODEL), lambda i, k: (k, 0)),
            pl.BlockSpec((tm, D_MODEL), lambda i, k: (i, 0)),
            pl.BlockSpec((1, D_MODEL), lambda i, k: (0, 0)),
            pl.BlockSpec((1, D_MODEL), lambda i, k: (0, 0)),
        ],
        out_specs=pl.BlockSpec((tm, D_MODEL), lambda i, k: (i, 0)),
        out_shape=jax.ShapeDtypeStruct((m, D_MODEL), F32),
        scratch_shapes=[pltpu.VMEM((tm, D_MODEL), F32)],
        compiler_params=_cparams("parallel", "arbitrary"),
        name="down_ln",
    )(h, w, x, lw, lb)


def _convgate_kernel(a_ref, g_ref, ha_ref, hg_ref, sa_ref, sg_ref, wa_ref, wg_ref, ba_ref, bg_ref, o_ref):
    first = pl.program_id(1) == 0
    tt = a_ref.shape[0]
    rows = lax.broadcasted_iota(jnp.int32, (tt, 1), 0)

    def conv(cur_ref, halo_ref, st_ref, w_ref, b_ref):
        cur = cur_ref[...]
        prev = jnp.where(first, st_ref[...], halo_ref[...])
        p1 = prev[7:8, :]
        p2 = prev[6:7, :]
        x1 = jnp.where(rows == 0, p1, pltpu.roll(cur, 1, 0))
        x2 = jnp.where(rows == 0, p2, jnp.where(rows == 1, p1, pltpu.roll(cur, 2, 0)))
        w = w_ref[...]
        return b_ref[...] + w[0:1, :] * x2 + w[1:2, :] * x1 + w[2:3, :] * cur

    a = conv(a_ref, ha_ref, sa_ref, wa_ref, ba_ref)
    g = conv(g_ref, hg_ref, sg_ref, wg_ref, bg_ref)
    o_ref[...] = (a * jax.nn.sigmoid(a) * g).astype(o_ref.dtype)


def _convgate(up, state8, conv_w8, conv_b, *, tt, tf):
    b, t, _ = up.shape
    tt = min(tt, t)
    nf = D_FFP // tf
    hb = tt // SUBLANES
    cur_a = pl.BlockSpec((None, tt, tf), lambda bi, ti, fi: (bi, ti, fi))
    cur_g = pl.BlockSpec((None, tt, tf), lambda bi, ti, fi: (bi, ti, fi + nf))
    halo_a = pl.BlockSpec((None, SUBLANES, tf), lambda bi, ti, fi: (bi, jnp.maximum(ti * hb - 1, 0), fi))
    halo_g = pl.BlockSpec((None, SUBLANES, tf), lambda bi, ti, fi: (bi, jnp.maximum(ti * hb - 1, 0), fi + nf))
    st_a = pl.BlockSpec((None, SUBLANES, tf), lambda bi, ti, fi: (bi, 0, fi))
    st_g = pl.BlockSpec((None, SUBLANES, tf), lambda bi, ti, fi: (bi, 0, fi + nf))
    w_a = pl.BlockSpec((SUBLANES, tf), lambda bi, ti, fi: (0, fi))
    w_g = pl.BlockSpec((SUBLANES, tf), lambda bi, ti, fi: (0, fi + nf))
    b_a = pl.BlockSpec((1, tf), lambda bi, ti, fi: (0, fi))
    b_g = pl.BlockSpec((1, tf), lambda bi, ti, fi: (0, fi + nf))
    return pl.pallas_call(
        _convgate_kernel,
        grid=(b, t // tt, nf),
        in_specs=[cur_a, cur_g, halo_a, halo_g, st_a, st_g, w_a, w_g, b_a, b_g],
        out_specs=pl.BlockSpec((None, tt, tf), lambda bi, ti, fi: (bi, ti, fi)),
        out_shape=jax.ShapeDtypeStruct((b, t, D_FFP), BF16),
        compiler_params=_cparams("parallel", "parallel", "parallel"),
        name="convgate",
    )(up, up, up, up, state8, state8, conv_w8, conv_w8, conv_b, conv_b)


def _ffn_kernel(x_ref, xh_ref, wa_ref, wg_ref, wd_ref, cwa_ref, cwg_ref, cba_ref, cbg_ref, lw_ref, lb_ref,
                o_ref, sa_ref, sg_ref, xe_ref, *, nf):
    first = pl.program_id(1) == 0
    f = pl.program_id(2)
    tt = x_ref.shape[0]

    @pl.when(f == 0)
    def _():
        o_ref[...] = jnp.zeros_like(o_ref)
        halo = jnp.where(first, 0.0, xh_ref[...])
        xe_ref[0:FFN_HALO, :] = halo.astype(BF16)
        xe_ref[FFN_HALO:, :] = x_ref[...].astype(BF16)

    xe = xe_ref[...]

    def branch(w_ref, cw_ref, cb_ref, s_ref):
        up = _dot(xe, w_ref[...])
        s_ref[...] = up[FFN_HALO + tt - SUBLANES:FFN_HALO + tt, :]
        w = cw_ref[...]
        x1 = pltpu.roll(up, 1, 0)[FFN_HALO:, :]
        x2 = pltpu.roll(up, 2, 0)[FFN_HALO:, :]
        return cb_ref[...] + w[0:1, :] * x2 + w[1:2, :] * x1 + w[2:3, :] * up[FFN_HALO:, :]

    a = branch(wa_ref, cwa_ref, cba_ref, sa_ref)
    g = branch(wg_ref, cwg_ref, cbg_ref, sg_ref)
    o_ref[...] += _dot((a * jax.nn.sigmoid(a) * g).astype(BF16), _valid_rows(wd_ref, f))

    @pl.when(f == nf - 1)
    def _():
        o_ref[...] = _layer_norm(DEEPNORM_ALPHA * x_ref[...] + o_ref[...], lw_ref[...], lb_ref[...])


def _ffn_prompt(x3, w_up, w_down, conv_w8, conv_b, lw, lb, *, tt, tf):
    b, t, _ = x3.shape
    nf = D_FFP // tf
    tt = min(tt, t)
    hb = tt // FFN_HALO
    half_a = lambda rows: pl.BlockSpec((rows, tf), lambda bi, ti, fi: (0, fi))
    half_g = lambda rows: pl.BlockSpec((rows, tf), lambda bi, ti, fi: (0, fi + nf))
    cols = _up_cols(tf)
    w_a = pl.BlockSpec((pl.Element(D_MODEL), pl.Element(tf)), lambda bi, ti, fi: (0, cols(fi)))
    w_g = pl.BlockSpec((pl.Element(D_MODEL), pl.Element(tf)), lambda bi, ti, fi: (0, cols(fi + nf)))
    vec = pl.BlockSpec((1, D_MODEL), lambda bi, ti, fi: (0, 0))
    tail = pl.BlockSpec((None, SUBLANES, tf), lambda bi, ti, fi: (bi, 0, fi))
    return pl.pallas_call(
        functools.partial(_ffn_kernel, nf=nf),
        grid=(b, t // tt, nf),
        in_specs=[pl.BlockSpec((None, tt, D_MODEL), lambda bi, ti, fi: (bi, ti, 0), pipeline_mode=pl.Buffered(1)),
                  pl.BlockSpec((None, FFN_HALO, D_MODEL), lambda bi, ti, fi: (bi, jnp.maximum(ti * hb - 1, 0), 0)),
                  w_a, w_g,
                  pl.BlockSpec((tf, D_MODEL), lambda bi, ti, fi: (fi, 0)),
                  half_a(SUBLANES), half_g(SUBLANES), half_a(1), half_g(1), vec, vec],
        out_specs=[pl.BlockSpec((None, tt, D_MODEL), lambda bi, ti, fi: (bi, ti, 0)), tail, tail],
        out_shape=[jax.ShapeDtypeStruct((b, t, D_MODEL), F32),
                   jax.ShapeDtypeStruct((b, SUBLANES, D_FFP), F32),
                   jax.ShapeDtypeStruct((b, SUBLANES, D_FFP), F32)],
        scratch_shapes=[pltpu.VMEM((tt + FFN_HALO, D_MODEL), BF16)],
        compiler_params=_cparams("parallel", "arbitrary", "arbitrary"),
        name="ffn_prompt",
    )(x3, x3, w_up, w_up, w_down, conv_w8, conv_w8, conv_b, conv_b, lw, lb)


def _cumsum_rows(x):
    n = x.shape[0]
    rows = lax.broadcasted_iota(jnp.int32, (n, 1), 0)
    d = 1
    while d < n:
        x = x + jnp.where(rows >= d, pltpu.roll(x, d, 0), 0.0)
        d *= 2
    return x


def _log_sigmoid(x):
    return jnp.minimum(x, 0.0) - jnp.log1p(jnp.exp(-jnp.abs(x)))


def _mlstm_kernel(q_ref, k_ref, v_ref, og_ref, g_ref, gb_ref, nw_ref, c0_ref, n0_ref, m0_ref,
                  out_ref, c_ref, n_ref, m_ref, *, L, t_valid):
    ci = pl.program_id(1)
    lin = q_ref.shape[0]

    @pl.when(ci == 0)
    def _():
        c_ref[...] = c0_ref[...]
        n_ref[...] = n0_ref[...]
        m_ref[...] = m0_ref[...]

    def rows_of(ref):
        x = ref[...]
        if lin < L:
            x = jnp.concatenate([x, jnp.zeros((L - lin, x.shape[1]), x.dtype)], axis=0)
        return x

    rows = lax.broadcasted_iota(jnp.int32, (L, 1), 0)
    valid = (ci * L + rows) < t_valid
    pre = rows_of(g_ref) + gb_ref[...]
    lf = jnp.where(valid, _log_sigmoid(pre), 0.0)
    ig = jnp.where(valid, pre, NEG_INF)
    bcum = _cumsum_rows(lf)
    dt = (pltpu.roll(ig, GATE_MF - GATE_MI, 1) - bcum).T
    q_all, k_all, v_all, og_all = rows_of(q_ref), rows_of(k_ref), rows_of(v_ref), rows_of(og_ref)
    tri = lax.broadcasted_iota(jnp.int32, (L, L), 0) >= lax.broadcasted_iota(jnp.int32, (L, L), 1)
    nw = nw_ref[...]

    for h in range(MLSTM_HEADS):
        sl = slice(h * HEAD_DIM, (h + 1) * HEAD_DIM)
        q = q_all[:, sl]
        k = k_all[:, sl] * QK_SCALE
        v = v_all[:, sl]
        qb, kb, vb = q.astype(BF16), k.astype(BF16), v.astype(BF16)
        b_col = bcum[:, GATE_MF + h:GATE_MF + h + 1]
        ig_col = ig[:, GATE_MI + h:GATE_MI + h + 1]
        d_row = dt[GATE_MF + h:GATE_MF + h + 1, :]
        c_prev = c_ref[h]
        n_prev = n_ref[h:h + 1, :]
        m_prev = m_ref[h:h + 1, 0:1]

        dmat = jnp.where(tri, b_col + d_row, NEG_INF)
        g_col = b_col + m_prev
        m_row = jnp.maximum(jnp.max(dmat, axis=1, keepdims=True), g_col)
        a = jnp.exp(dmat - m_row) * _dot_nt(qb, kb)
        w_inter = jnp.exp(g_col - m_row)
        num = _dot(a.astype(BF16), vb) + w_inter * _dot(qb, c_prev.astype(BF16))
        den = jnp.sum(a, axis=1, keepdims=True) + w_inter * jnp.sum(q * n_prev, axis=1, keepdims=True)
        hid = num / jnp.maximum(jnp.abs(den), jnp.exp(-m_row))
        mu = jnp.mean(hid, axis=1, keepdims=True)
        dlt = hid - mu
        var = jnp.mean(dlt * dlt, axis=1, keepdims=True)
        hn = dlt * lax.rsqrt(var + LN_EPS) * nw[:, sl]
        res = jax.nn.sigmoid(og_all[:, sl]) * hn
        out_ref[:, sl] = res[0:lin, :]

        f_tot = b_col[L - 1:L, :]
        w_s = f_tot - b_col + ig_col
        m_new = jnp.maximum(f_tot + m_prev, jnp.max(w_s, axis=0, keepdims=True))
        ws = jnp.exp(w_s - m_new)
        decay = jnp.exp(f_tot + m_prev - m_new)
        c_ref[h] = decay * c_prev + _dot(k.T.astype(BF16), (ws * v).astype(BF16))
        n_ref[h:h + 1, :] = decay * n_prev + jnp.sum(ws * k, axis=0, keepdims=True)
        m_ref[h:h + 1, :] = jnp.broadcast_to(m_new, (1, LANES))


def _mlstm(z, gate_b, norm_w, c0, n0, m0, *, L, lin, t_valid):
    b, tz, _ = z.shape
    nchunks = tz // lin
    qblk = lambda col: pl.BlockSpec((None, lin, D_MLSTM), lambda bi, ci: (bi, ci, col // D_MLSTM))
    st4 = pl.BlockSpec((None, MLSTM_HEADS, HEAD_DIM, HEAD_DIM), lambda bi, ci: (bi, 0, 0, 0))
    st3 = pl.BlockSpec((None, MLSTM_HEADS, LANES), lambda bi, ci: (bi, 0, 0))
    return pl.pallas_call(
        functools.partial(_mlstm_kernel, L=L, t_valid=t_valid),
        grid=(b, nchunks),
        in_specs=[qblk(C_MQ), qblk(C_MK), qblk(C_MV), qblk(C_MO),
                  pl.BlockSpec((None, lin, LANES), lambda bi, ci: (bi, ci, C_GATE // LANES)),
                  pl.BlockSpec((1, LANES), lambda bi, ci: (0, 0)),
                  pl.BlockSpec((1, D_MLSTM), lambda bi, ci: (0, 0)),
                  st4, st3, st3],
        out_specs=[pl.BlockSpec((None, lin, D_MLSTM), lambda bi, ci: (bi, ci, 0)), st4, st3, st3],
        out_shape=[jax.ShapeDtypeStruct((b, tz, D_MLSTM), F32),
                   jax.ShapeDtypeStruct((b, MLSTM_HEADS, HEAD_DIM, HEAD_DIM), F32),
                   jax.ShapeDtypeStruct((b, MLSTM_HEADS, LANES), F32),
                   jax.ShapeDtypeStruct((b, MLSTM_HEADS, LANES), F32)],
        compiler_params=_cparams("parallel", "arbitrary"),
        name="mlstm",
    )(z, z, z, z, z, gate_b, norm_w, c0, n0, m0)


S5_IN_TILES = S5_CH // 256
S5_IN_FEATS = D_S5 // S5_IN_TILES
S5_OUT_TILES = D_S5 // 256
S5_OUT_CH = S5_CH // S5_OUT_TILES


S5_LT = S5_CH // LANES


def _lane_tiles(x):
    return jnp.stack([x[:, k * LANES:(k + 1) * LANES] for k in range(x.shape[1] // LANES)])


def _lane_untile(x):
    return jnp.concatenate([x[k] for k in range(x.shape[0])], axis=1)


def _s5_kernel(u_ref, wb_ref, wc_ref, lr_ref, li_ref, d_ref, gw_ref, gb_ref, x0r_ref, x0i_ref,
               o_ref, xr_ref, xi_ref, sr_ref, si_ref, y_ref, *, steps):
    nb, lc, _ = u_ref.shape

    @pl.when(pl.program_id(0) == 0)
    def _():
        xr_ref[...] = x0r_ref[...]
        xi_ref[...] = x0i_ref[...]

    for b in range(nb):
        ub = u_ref[b].astype(BF16)
        for c in range(S5_IN_TILES):
            bu = _dot(ub[:, c * S5_IN_FEATS:(c + 1) * S5_IN_FEATS], wb_ref[c])
            for k in range(256 // LANES):
                rows = pl.ds(b, lc, stride=nb)
                sr_ref.at[c * (256 // LANES) + k][rows, :] = bu[:, k * LANES:(k + 1) * LANES]
                si_ref.at[c * (256 // LANES) + k][rows, :] = bu[:, 256 + k * LANES:256 + (k + 1) * LANES]
    lam_r = _lane_tiles(jnp.broadcast_to(lr_ref[...], (SUBLANES, S5_CH)))
    lam_i = _lane_tiles(jnp.broadcast_to(li_ref[...], (SUBLANES, S5_CH)))
    row8 = lax.broadcasted_iota(jnp.int32, (1, SUBLANES, 1), 1)
    per_group = SUBLANES // nb

    def group(i, carry):
        xr, xi = carry
        r0 = pl.multiple_of(i * SUBLANES, SUBLANES)
        br = sr_ref[:, pl.ds(r0, SUBLANES), :]
        bi = si_ref[:, pl.ds(r0, SUBLANES), :]
        out_r, out_i = br, bi
        for k in range(per_group):
            nr = lam_r * xr - lam_i * xi + br
            ni = lam_r * xi + lam_i * xr + bi
            here = (row8 >= k * nb) & (row8 < (k + 1) * nb)
            out_r = jnp.where(here, nr, out_r)
            out_i = jnp.where(here, ni, out_i)
            if per_group > 1:
                xr = pltpu.roll(nr, nb, 1)
                xi = pltpu.roll(ni, nb, 1)
            else:
                xr, xi = nr, ni
        sr_ref[:, pl.ds(r0, SUBLANES), :] = out_r
        si_ref[:, pl.ds(r0, SUBLANES), :] = out_i
        return xr, xi

    xr, xi = lax.fori_loop(0, steps * nb // SUBLANES, group, (_lane_tiles(xr_ref[...]), _lane_tiles(xi_ref[...])))
    xr_ref[...] = _lane_untile(xr)
    xi_ref[...] = _lane_untile(xi)
    per_out = S5_OUT_CH // LANES
    for j in range(S5_OUT_TILES):
        lhs_r = jnp.concatenate([sr_ref[j * per_out + k] for k in range(per_out)], axis=1).astype(BF16)
        lhs_i = jnp.concatenate([si_ref[j * per_out + k] for k in range(per_out)], axis=1).astype(BF16)
        y = _dot(lhs_r, wc_ref[0, j]) + _dot(lhs_i, wc_ref[1, j])
        for k in range(256 // LANES):
            y_ref[j * (256 // LANES) + k] = y[:, k * LANES:(k + 1) * LANES]
    for b in range(nb):
        rows = pl.ds(b, lc, stride=nb)
        y = jnp.concatenate([y_ref.at[k][rows, :] for k in range(D_S5 // LANES)], axis=1) + d_ref[...] * u_ref[b]
        zz = jax.nn.gelu(y)
        o_ref[b] = zz * jax.nn.sigmoid(_dot(zz.astype(BF16), gw_ref[...]) + gb_ref[...])


def _s5(z3, wb, wc, lam_r, lam_i, d, glu_w, glu_b, x0r, x0i, *, lc, steps):
    b, tz, _ = z3.shape
    lc = min(lc, tz)
    steps = min(steps, lc)
    assert steps == lc or tz == lc
    full = lambda a: pl.BlockSpec(a.shape, lambda i: (0,) * a.ndim)
    st = pl.BlockSpec((SUBLANES, S5_CH), lambda i: (0, 0))
    return pl.pallas_call(
        functools.partial(_s5_kernel, steps=steps),
        grid=(tz // lc,),
        in_specs=[pl.BlockSpec((b, lc, D_S5), lambda i: (0, i, C_SU // D_S5)),
                  full(wb), full(wc), full(lam_r), full(lam_i), full(d), full(glu_w), full(glu_b), st, st],
        out_specs=[pl.BlockSpec((b, lc, D_S5), lambda i: (0, i, 0)), st, st],
        out_shape=[jax.ShapeDtypeStruct((b, tz, D_S5), F32),
                   jax.ShapeDtypeStruct((SUBLANES, S5_CH), F32),
                   jax.ShapeDtypeStruct((SUBLANES, S5_CH), F32)],
        scratch_shapes=[pltpu.VMEM((S5_LT, lc * b, LANES), F32), pltpu.VMEM((S5_LT, lc * b, LANES), F32),
                        pltpu.VMEM((D_S5 // LANES, lc * b, LANES), F32)],
        compiler_params=_cparams("arbitrary"),
        name="s5",
    )(z3, wb, wc, lam_r, lam_i, d, glu_w, glu_b, x0r, x0i)


N_CMB = 2 * KV_HEADS


def _cmp_project(rows_of, w_ref, o_ref):
    nch = o_ref.shape[0]
    for cmb in range(N_CMB):
        slot = cmb // KV_HEADS
        rows_ref = rows_of(cmb)
        acc = jnp.zeros((nch, 2 * HEAD_DIM), F32)
        for l in range(CMP_STRIDE):
            x = rows_ref[pl.ds(l, nch, stride=CMP_STRIDE), :]
            acc += _dot(x.astype(BF16), w_ref[slot, l])
        o_ref[:, cmb * 2 * HEAD_DIM:(cmb + 1) * 2 * HEAD_DIM] = acc


def _cmp_p_kernel(r0_ref, r1_ref, r2_ref, r3_ref, w_ref, o_ref):
    rows = (r0_ref, r1_ref, r2_ref, r3_ref)
    _cmp_project(lambda cmb: rows[cmb], w_ref, o_ref)


def _cmp_project_prompt(z, wcat):
    b, t, _ = z.shape
    nch = t // CMP_STRIDE
    rows = lambda cmb: pl.BlockSpec((None, t, HEAD_DIM), lambda bi: (bi, 0, C_NKV // HEAD_DIM + cmb))
    return pl.pallas_call(
        _cmp_p_kernel,
        grid=(b,),
        in_specs=[rows(cmb) for cmb in range(N_CMB)] + [pl.BlockSpec(wcat.shape, lambda bi: (0, 0, 0, 0))],
        out_specs=pl.BlockSpec((None, nch, N_CMB * 2 * HEAD_DIM), lambda bi: (bi, 0, 0)),
        out_shape=jax.ShapeDtypeStruct((b, nch, N_CMB * 2 * HEAD_DIM), F32),
        compiler_params=_cparams("parallel"),
        name="cmp_project_prompt",
    )(z, z, z, z, wcat)


def _cmp_s_kernel(pt_ref, *refs):
    del pt_ref
    per_pos = 2 * KV_HEADS
    page_refs = [r.reshape(PAGE_SIZE * per_pos, HEAD_DIM) for r in refs[:CMP_PAGES]]
    w_ref, o_ref = refs[CMP_PAGES:]
    per_page = PAGE_SIZE // CMP_STRIDE
    for cmb in range(N_CMB):
        acc = jnp.zeros((o_ref.shape[0], 2 * HEAD_DIM), F32)
        for l in range(CMP_STRIDE):
            x = jnp.concatenate(
                [page_refs[p][pl.ds(l * per_pos + cmb, per_page, stride=CMP_STRIDE * per_pos), :]
                 for p in range(CMP_PAGES)], axis=0)
            acc += _dot(x.astype(BF16), w_ref[cmb // KV_HEADS, l])
        o_ref[:, cmb * 2 * HEAD_DIM:(cmb + 1) * 2 * HEAD_DIM] = acc


def _page_spec(layer, slot, p, pages_per_step, grid_rank):
    def index_map(*idx):
        bi, si, pt = idx[0], idx[grid_rank - 1], idx[grid_rank]
        return (layer, pt[bi, si * pages_per_step + p], 0, slot, 0, 0)

    return pl.BlockSpec((None, None, PAGE_SIZE, None, KV_HEADS, HEAD_DIM), index_map)


def _cmp_project_sample(cache6, page_table, layer, wcat):
    b, npages = page_table.shape
    steps = npages // CMP_PAGES
    nch = CMP_PAGES * PAGE_SIZE // CMP_STRIDE
    def page_pair(p):
        return pl.BlockSpec((None, None, PAGE_SIZE, 2, KV_HEADS, HEAD_DIM),
                            lambda bi, si, pt: (layer, pt[bi, si * CMP_PAGES + p], 0, 0, 0, 0))

    pages = [page_pair(p) for p in range(CMP_PAGES)]

    grid_spec = pltpu.PrefetchScalarGridSpec(
        num_scalar_prefetch=1,
        grid=(b, steps),
        in_specs=pages + [pl.BlockSpec(wcat.shape, lambda bi, si, pt: (0, 0, 0, 0))],
        out_specs=pl.BlockSpec((None, nch, N_CMB * 2 * HEAD_DIM), lambda bi, si, pt: (bi, si, 0)),
    )
    return pl.pallas_call(
        _cmp_s_kernel,
        grid_spec=grid_spec,
        out_shape=jax.ShapeDtypeStruct((b, steps * nch, N_CMB * 2 * HEAD_DIM), F32),
        compiler_params=_cparams("parallel", "arbitrary"),
        name="cmp_project_sample",
    )(page_table, *([cache6] * CMP_PAGES), wcat)


def _cmp_fin_kernel(p_ref, b1_ref, w2_ref, o_ref):
    nch = p_ref.shape[0]
    for cmb in range(N_CMB):
        slot = cmb // KV_HEADS
        c0 = cmb * 2 * HEAD_DIM
        first = p_ref[:, c0:c0 + HEAD_DIM]
        second = pltpu.roll(p_ref[:, c0 + HEAD_DIM:c0 + 2 * HEAD_DIM], nch - 1, 0)
        hid = b1_ref[slot:slot + 1, :] + first + second
        o_ref[cmb] = _dot(jax.nn.gelu(hid).astype(BF16), w2_ref[slot])


def _cmp_finish(p, b1, w2):
    b, nch, _ = p.shape
    return pl.pallas_call(
        _cmp_fin_kernel,
        grid=(b,),
        in_specs=[pl.BlockSpec((None, nch, p.shape[2]), lambda bi: (bi, 0, 0)),
                  pl.BlockSpec(b1.shape, lambda bi: (0, 0)),
                  pl.BlockSpec(w2.shape, lambda bi: (0, 0, 0))],
        out_specs=pl.BlockSpec((None, N_CMB, nch, HEAD_DIM), lambda bi: (bi, 0, 0, 0)),
        out_shape=jax.ShapeDtypeStruct((b, N_CMB, nch, HEAD_DIM), F32),
        compiler_params=_cparams("parallel"),
        name="cmp_finish",
    )(p, b1, w2)


def _softmax_init(m_scr, l_scr, acc_scr):
    m_scr[...] = jnp.full(m_scr.shape, M_INIT, F32)
    l_scr[...] = jnp.zeros(l_scr.shape, F32)
    acc_scr[...] = jnp.zeros(acc_scr.shape, F32)


def _softmax_update(s, mask, vb, m_scr, l_scr, acc_scr):
    s = jnp.where(mask, s, NEG_INF)
    m_prev = m_scr[...]
    m_new = jnp.maximum(m_prev, jnp.max(s, axis=1, keepdims=True))
    alpha = jnp.exp(m_prev - m_new)
    p = jnp.exp(s - m_new)
    l_scr[...] = alpha * l_scr[...] + jnp.sum(p, axis=1, keepdims=True)
    acc_scr[...] = alpha * acc_scr[...] + _dot(p.astype(BF16), vb)
    m_scr[...] = m_new


def _softmax_result(l_scr, acc_scr):
    l = l_scr[...]
    return acc_scr[...] / jnp.where(l > 0, l, 1.0)


def _masked_probs(s, mask):
    s = jnp.where(mask, s, NEG_INF)
    m = jnp.max(s, axis=1, keepdims=True)
    m = jnp.where(m > NEG_INF, m, 0.0)
    p = jnp.exp(s - m)
    den = jnp.sum(p, axis=1, keepdims=True)
    return p / jnp.where(den > 0, den, 1.0)


def _stack_heads(q):
    return jnp.concatenate([q[:, j * HEAD_DIM:(j + 1) * HEAD_DIM] for j in range(GROUP)], axis=0)


def _block_importance(pc, n_cmp, n_sel):
    ncp = pc.shape[1]
    nsp = -(-n_sel // LANES) * LANES
    c_start = lax.broadcasted_iota(jnp.int32, (ncp, nsp), 0) * CMP_STRIDE
    s_start = lax.broadcasted_iota(jnp.int32, (ncp, nsp), 1) * SEL_BLOCK
    overlap = ((c_start < s_start + SEL_BLOCK) & (c_start + CMP_BLOCK > s_start)
               & (c_start < n_cmp * CMP_STRIDE)).astype(F32)
    return jnp.dot(pc, overlap, preferred_element_type=F32, precision=lax.Precision.HIGHEST)


def _force_blocks(imp, blk, cur):
    forced = (blk == 0) | (blk == cur) | (blk == cur - 1)
    imp = jnp.where(forced, FORCE_SCORE, imp)
    return jnp.where(blk > cur, -1.0, imp)


def _select_blocks_cols(pc, cur, n_cmp, n_sel):
    imp = _block_importance(pc, n_cmp, n_sel)
    blk = lax.broadcasted_iota(jnp.int32, (1, imp.shape[1]), 1)
    imp = _force_blocks(imp, blk, cur)

    def body(sp, count):
        col = jnp.sum(jnp.where(blk == sp, imp, 0.0), axis=1, keepdims=True)
        ahead = (col > imp) | ((col == imp) & (sp < blk))
        return count + ahead.astype(F32)

    count = lax.fori_loop(0, n_sel, body, jnp.zeros(imp.shape, F32), unroll=32)
    return ((count < min(SEL_TOPK, n_sel)) & (blk < n_sel)).astype(F32)


def _select_blocks_rows(pc, cur_row, n_cmp, n_sel):
    rows = pc.shape[0]
    nrow = -(-n_sel // SUBLANES) * SUBLANES
    imp_t = _block_importance(pc, n_cmp, n_sel).T[0:nrow, :]
    blk = lax.broadcasted_iota(jnp.int32, (nrow, 1), 0)
    imp_t = _force_blocks(imp_t, blk, cur_row)
    count = jnp.zeros(imp_t.shape, F32)
    for sp in range(n_sel):
        row = imp_t[sp:sp + 1, :]
        count += ((row > imp_t) | ((row == imp_t) & (sp < blk))).astype(F32)
    sel_t = ((count < min(SEL_TOPK, n_sel)) & (blk < n_sel)).astype(F32)
    sel_t = jnp.concatenate([sel_t, jnp.zeros((LANES - nrow, rows), F32)], axis=0)
    return sel_t.T


def _gate(gt, col):
    lane = lax.broadcasted_iota(jnp.int32, (1, LANES), 1)
    return jax.nn.sigmoid(jnp.sum(jnp.where(lane == col, gt, 0.0), axis=1, keepdims=True))


def _nsa_prompt_kernel(q_ref, ks_ref, vs_ref, kw_ref, vw_ref, kc_ref, vc_ref, bc_ref, tz_ref, gt_ref, gb_ref,
                       o_ref, s_scr, selb_scr, mx_scr, l_scr, acc_scr, *, t_len):
    g = pl.program_id(1)
    qt = pl.program_id(2)
    q0 = qt * TQ
    n_cmp = t_len // CMP_STRIDE - (CMP_BLOCK // CMP_STRIDE) + 1
    n_sel = t_len // SEL_BLOCK
    ncp = kc_ref.shape[0]
    rows4 = GROUP * TQ
    qs = _stack_heads(q_ref[...] * QK_SCALE).astype(BF16)
    qpos = q0 + (lax.broadcasted_iota(jnp.int32, (rows4, 1), 0) & (TQ - 1))

    n_idx = lax.broadcasted_iota(jnp.int32, (1, ncp), 1)
    s_c = _dot_nt(qs, kc_ref[...].astype(BF16)) + bc_ref[...].reshape(rows4, ncp)
    cmask = (qpos - (n_idx * CMP_STRIDE + CMP_BLOCK - 1) >= 0) & (n_idx < n_cmp)
    p_c = _masked_probs(s_c, cmask)
    o_c = _dot(p_c.astype(BF16), vc_ref[...].astype(BF16))
    pc = p_c[0:TQ] + p_c[TQ:2 * TQ] + p_c[2 * TQ:3 * TQ] + p_c[3 * TQ:4 * TQ]
    cur_row = jnp.right_shift(q0 + lax.broadcasted_iota(jnp.int32, (1, TQ), 1), SEL_SHIFT)
    sel = _select_blocks_rows(pc, cur_row, n_cmp, n_sel)
    sel_add = ((sel - 1.0) * -MASKED).astype(BF16)

    kt_hi = (q0 + TQ - 1) // TK + 1
    kt_far = jnp.maximum(q0 - (FAR_DIST - 1), 0) // TK
    kidx = lax.broadcasted_iota(jnp.int32, (1, TK), 1)
    srow = lax.broadcasted_iota(jnp.int32, (LANES, TK), 0)

    def spread_selection(kt, carry):
        k0 = kt * TK
        expand = (jnp.right_shift(k0 + lax.broadcasted_iota(jnp.int32, (LANES, TK), 1), SEL_SHIFT) == srow).astype(BF16)
        selb_scr[kt] = _dot(sel_add, expand)
        return carry

    lax.fori_loop(0, kt_hi, spread_selection, 0)

    def tile_bias(k0):
        return tz_ref[jnp.clip((q0 - k0) // TQ, 0, N_TZ - 1)].reshape(rows4, TK)

    def attend(k_ref, v_ref, kt_lo, mask_tile):
        mx_scr[...] = jnp.full(mx_scr.shape, NEG_INF, F32)

        def scores(near):
            def body(kt, carry):
                k0 = pl.multiple_of(kt * TK, TK)
                s = _dot_nt(qs, k_ref[pl.ds(k0, TK), :].astype(BF16))
                if near:
                    s = s + tile_bias(k0)
                s = mask_tile(k0, s, near)
                s_scr[kt] = s
                mx_scr[...] = jnp.maximum(mx_scr[...], jnp.maximum(s[:, 0:LANES], s[:, LANES:TK]))
                return carry
            return body

        kt_mid = jnp.maximum(kt_far, kt_lo)
        lax.fori_loop(kt_lo, kt_mid, scores(False), 0)
        lax.fori_loop(kt_mid, kt_hi, scores(True), 0)
        m = jnp.max(mx_scr[...], axis=1, keepdims=True)
        mx_scr[...] = jnp.broadcast_to(jnp.where(m > NEG_INF, m, 0.0), mx_scr.shape)
        l_scr[...] = jnp.zeros(l_scr.shape, F32)
        acc_scr[...] = jnp.zeros(acc_scr.shape, F32)

        def probs(kt, carry):
            k0 = pl.multiple_of(kt * TK, TK)
            m_rep = mx_scr[...]
            p = jnp.exp(s_scr[kt] - jnp.concatenate([m_rep, m_rep], axis=1))
            l_scr[...] += p[:, 0:LANES] + p[:, LANES:TK]
            acc_scr[...] += _dot(p.astype(BF16), v_ref[pl.ds(k0, TK), :].astype(BF16))
            return carry

        lax.fori_loop(kt_lo, kt_hi, probs, 0)
        l = jnp.sum(l_scr[...], axis=1, keepdims=True)
        return acc_scr[...] / jnp.where(l > 0, l, 1.0)

    def sel_mask(k0, s, near):
        s = (s.reshape(GROUP, TQ, TK) + selb_scr[k0 // TK][None]).reshape(rows4, TK)
        return jnp.where(k0 + kidx <= qpos, s, NEG_INF) if near else s

    o_s = attend(ks_ref, vs_ref, 0, sel_mask)

    def win_mask(k0, s, near):
        dist = qpos - (k0 + kidx)
        return jnp.where((dist >= 0) & (dist <= WINDOW), s, NEG_INF)

    o_w = attend(kw_ref, vw_ref, jnp.maximum(q0 - WINDOW, 0) // TK, win_mask)

    gt = gt_ref[...] + gb_ref[...]
    for j in range(GROUP):
        col = GATE_NG + (g * GROUP + j) * 3
        r = slice(j * TQ, (j + 1) * TQ)
        o_ref[:, j * HEAD_DIM:(j + 1) * HEAD_DIM] = (
            _gate(gt, col) * o_c[r] + _gate(gt, col + 1) * o_s[r] + _gate(gt, col + 2) * o_w[r])


def _nsa_prompt(z, kcvc, bias_c, tz, gate_b):
    b, t, _ = z.shape
    ncp = kcvc.shape[2]
    nq = t // TQ
    kv = lambda slot: pl.BlockSpec((None, t, HEAD_DIM), lambda bi, g, qi: (bi, 0, C_NKV // HEAD_DIM + slot * KV_HEADS + g))
    cmp_blk = lambda slot: pl.BlockSpec((None, None, ncp, HEAD_DIM), lambda bi, g, qi: (bi, slot * KV_HEADS + g, 0, 0))
    return pl.pallas_call(
        functools.partial(_nsa_prompt_kernel, t_len=t),
        grid=(b, KV_HEADS, nq),
        in_specs=[pl.BlockSpec((None, TQ, GROUP * HEAD_DIM), lambda bi, g, qi: (bi, qi, C_NQ // (GROUP * HEAD_DIM) + g)),
                  kv(2), kv(3), kv(4), kv(5), cmp_blk(0), cmp_blk(1),
                  pl.BlockSpec((GROUP, TQ, ncp), lambda bi, g, qi: (g, qi, 0)),
                  pl.BlockSpec((N_TZ, GROUP, TQ, TK), lambda bi, g, qi: (0, g, 0, 0)),
                  pl.BlockSpec((None, TQ, LANES), lambda bi, g, qi: (bi, qi, C_GATE // LANES)),
                  pl.BlockSpec((1, LANES), lambda bi, g, qi: (0, 0))],
        out_specs=pl.BlockSpec((None, TQ, GROUP * HEAD_DIM), lambda bi, g, qi: (bi, qi, g)),
        out_shape=jax.ShapeDtypeStruct((b, t, D_NSA), F32),
        scratch_shapes=[pltpu.VMEM((t // TK, GROUP * TQ, TK), F32), pltpu.VMEM((t // TK, TQ, TK), F32),
                        pltpu.VMEM((GROUP * TQ, LANES), F32),
                        pltpu.VMEM((GROUP * TQ, LANES), F32), pltpu.VMEM((GROUP * TQ, HEAD_DIM), F32)],
        compiler_params=_cparams("parallel", "parallel", "arbitrary"),
        name="nsa_prompt",
    )(z, z, z, z, z, kcvc, kcvc, bias_c, tz, z, gate_b)


TS = SUBLANES
ROWS_S = GROUP * TS


def _nsa_s_cmp_kernel(q_ref, kc_ref, vc_ref, bc_ref, oc_ref, sel_ref, *, pos0, n_cmp, n_sel):
    nch = kc_ref.shape[0]
    qs = _stack_heads(q_ref[...] * QK_SCALE).astype(BF16)
    qpos = pos0 + (lax.broadcasted_iota(jnp.int32, (ROWS_S, 1), 0) & (TS - 1))
    n_idx = lax.broadcasted_iota(jnp.int32, (1, nch), 1)
    s_c = _dot_nt(qs, kc_ref[...].astype(BF16)) + bc_ref[...].reshape(ROWS_S, nch)
    cmask = (qpos - (n_idx * CMP_STRIDE + CMP_BLOCK - 1) >= 0) & (n_idx < n_cmp)
    p_c = _masked_probs(s_c, cmask)
    oc_ref[...] = _dot(p_c.astype(BF16), vc_ref[...].astype(BF16))
    pc = p_c[0:TS] + p_c[TS:2 * TS] + p_c[2 * TS:3 * TS] + p_c[3 * TS:4 * TS]
    sel_ref[...] = _select_blocks_cols(pc, jnp.right_shift(qpos[0:TS], SEL_SHIFT), n_cmp, n_sel)


def _nsa_sample_cmp(z8, kcvc, bias_c, *, pos0, n_cmp, n_sel):
    b = z8.shape[0]
    nch = kcvc.shape[2]
    nsp = -(-n_sel // LANES) * LANES
    cmp_blk = lambda slot: pl.BlockSpec((None, None, nch, HEAD_DIM), lambda bi, g: (bi, slot * KV_HEADS + g, 0, 0))
    return pl.pallas_call(
        functools.partial(_nsa_s_cmp_kernel, pos0=pos0, n_cmp=n_cmp, n_sel=n_sel),
        grid=(b, KV_HEADS),
        in_specs=[pl.BlockSpec((None, TS, GROUP * HEAD_DIM), lambda bi, g: (bi, 0, C_NQ // (GROUP * HEAD_DIM) + g)),
                  cmp_blk(0), cmp_blk(1),
                  pl.BlockSpec((GROUP, TS, nch), lambda bi, g: (g, 0, 0))],
        out_specs=[pl.BlockSpec((None, None, ROWS_S, HEAD_DIM), lambda bi, g: (bi, g, 0, 0)),
                   pl.BlockSpec((None, None, TS, nsp), lambda bi, g: (bi, g, 0, 0))],
        out_shape=[jax.ShapeDtypeStruct((b, KV_HEADS, ROWS_S, HEAD_DIM), F32),
                   jax.ShapeDtypeStruct((b, KV_HEADS, TS, nsp), F32)],
        compiler_params=_cparams("parallel", "parallel"),
        name="nsa_sample_cmp",
    )(z8, kcvc, kcvc, bias_c)


def _pad_rows(x, n):
    return jnp.concatenate([x, jnp.zeros((n - x.shape[0], x.shape[1]), x.dtype)], axis=0)


def _nsa_s_sel_kernel(pt_ref, q_ref, sel_ref, kn_ref, vn_ref, bt_ref, bn_ref, bf_ref, *refs, pos0, npages):
    del pt_ref
    rows_pp = PAGE_SIZE * KV_HEADS
    k_refs = [r.reshape(rows_pp, HEAD_DIM) for r in refs[:SEL_PAGES]]
    v_refs = [r.reshape(rows_pp, HEAD_DIM) for r in refs[SEL_PAGES:2 * SEL_PAGES]]
    o_ref, kbuf, vbuf, m_scr, l_scr, acc_scr = refs[2 * SEL_PAGES:]
    step = pl.program_id(1)
    last = step == npages // SEL_PAGES - 1
    rows = KV_HEADS * ROWS_S
    ncol = SEL_TILE * rows_pp
    head_shift = KV_HEADS.bit_length() - 1
    blk_per_page = PAGE_SIZE // SEL_BLOCK

    @pl.when(step == 0)
    def _():
        _softmax_init(m_scr, l_scr, acc_scr)

    nsp = sel_ref.shape[-1]
    ridx = lax.broadcasted_iota(jnp.int32, (rows, 1), 0)
    qpos = pos0 + (ridx & (TS - 1))
    qs = jnp.concatenate([_stack_heads(q_ref[:, g * GROUP * HEAD_DIM:(g + 1) * GROUP * HEAD_DIM] * QK_SCALE)
                          for g in range(KV_HEADS)], axis=0).astype(BF16)
    sel_rows = jnp.concatenate([sel_ref[g] for g in range(KV_HEADS) for _ in range(GROUP)], axis=0)
    bias_far = bf_ref[...].reshape(rows, PAGE_SIZE)[:, 0:1]
    col = lax.broadcasted_iota(jnp.int32, (1, ncol), 1)
    own_head = (col & (KV_HEADS - 1)) == ridx // ROWS_S
    expand = (jnp.right_shift(lax.broadcasted_iota(jnp.int32, (LANES, ncol), 1), SEL_SHIFT + head_shift)
              == lax.broadcasted_iota(jnp.int32, (LANES, ncol), 0)).astype(BF16)
    blk_r = lax.broadcasted_iota(jnp.int32, (nsp, LANES), 0)
    blk_c = lax.broadcasted_iota(jnp.int32, (nsp, LANES), 1)
    n_tiles = SEL_PAGES // SEL_TILE

    for tile in range(n_tiles):
        for i in range(SEL_TILE):
            p = tile * SEL_TILE + i
            kbuf[i * rows_pp:(i + 1) * rows_pp, :] = k_refs[p][...].astype(BF16)
            vbuf[i * rows_pp:(i + 1) * rows_pp, :] = v_refs[p][...].astype(BF16)
        page0 = step * SEL_PAGES + tile * SEL_TILE
        window = ((blk_r == page0 * blk_per_page + blk_c) & (blk_c < SEL_TILE * blk_per_page)).astype(BF16)
        chosen = _dot(_dot(sel_rows.astype(BF16), window).astype(BF16), expand) > 0.5
        s = _dot_nt(qs, kbuf[...]) + bias_far
        if tile == n_tiles - 1:
            tail = jnp.where(last, bt_ref[...].reshape(rows, rows_pp) - bias_far, 0.0)
            s = s + jnp.concatenate([jnp.zeros((rows, ncol - rows_pp), F32), tail], axis=1)
        key = page0 * PAGE_SIZE + jnp.right_shift(col, head_shift)
        _softmax_update(s, chosen & own_head & (key <= qpos), vbuf[...], m_scr, l_scr, acc_scr)

    @pl.when(last)
    def _():
        kidx = lax.broadcasted_iota(jnp.int32, (1, PAGE_SIZE), 1)
        new_blk = lax.broadcasted_iota(jnp.int32, (1, nsp), 1) == npages * blk_per_page
        flag = jnp.sum(jnp.where(new_blk, sel_rows, 0.0), axis=1, keepdims=True) > 0.5
        bias_new = bn_ref[...].reshape(rows, PAGE_SIZE)
        for g in range(KV_HEADS):
            r = slice(g * ROWS_S, (g + 1) * ROWS_S)
            kn = _pad_rows(kn_ref[:, g * HEAD_DIM:(g + 1) * HEAD_DIM], PAGE_SIZE).astype(BF16)
            vn = _pad_rows(vn_ref[:, g * HEAD_DIM:(g + 1) * HEAD_DIM], PAGE_SIZE).astype(BF16)
            mask = flag[r] & (pos0 + kidx <= qpos[r]) & (kidx < TS)
            _softmax_update(_dot_nt(qs[r], kn) + bias_new[r], mask, vn, m_scr.at[r], l_scr.at[r], acc_scr.at[r])
            o_ref[g] = _softmax_result(l_scr.at[r], acc_scr.at[r])


def _nsa_sample_sel(z8, sel, bias_tail, bias_new, bias_far, cache6, page_table, layer, *, pos0):
    b, npages = page_table.shape
    nsp = sel.shape[-1]
    steps = npages // SEL_PAGES
    slot_cols = KV_HEADS * HEAD_DIM
    new_rows = lambda slot: pl.BlockSpec((None, TS, slot_cols),
                                         lambda bi, si, pt: (bi, 0, (C_NKV + slot * slot_cols) // slot_cols))
    whole = lambda a: pl.BlockSpec(a.shape, lambda bi, si, pt: (0,) * a.ndim)
    grid_spec = pltpu.PrefetchScalarGridSpec(
        num_scalar_prefetch=1,
        grid=(b, steps),
        in_specs=[pl.BlockSpec((None, TS, D_NSA), lambda bi, si, pt: (bi, 0, C_NQ // D_NSA)),
                  pl.BlockSpec((None, KV_HEADS, TS, nsp), lambda bi, si, pt: (bi, 0, 0, 0)),
                  new_rows(2), new_rows(3), whole(bias_tail), whole(bias_new), whole(bias_far)]
                 + [_page_spec(layer, 2, p, SEL_PAGES, 2) for p in range(SEL_PAGES)]
                 + [_page_spec(layer, 3, p, SEL_PAGES, 2) for p in range(SEL_PAGES)],
        out_specs=pl.BlockSpec((None, KV_HEADS, ROWS_S, HEAD_DIM), lambda bi, si, pt: (bi, 0, 0, 0)),
        scratch_shapes=[pltpu.VMEM((SEL_TILE * PAGE_SIZE * KV_HEADS, HEAD_DIM), BF16),
                        pltpu.VMEM((SEL_TILE * PAGE_SIZE * KV_HEADS, HEAD_DIM), BF16),
                        pltpu.VMEM((KV_HEADS * ROWS_S, 1), F32), pltpu.VMEM((KV_HEADS * ROWS_S, 1), F32),
                        pltpu.VMEM((KV_HEADS * ROWS_S, HEAD_DIM), F32)],
    )
    return pl.pallas_call(
        functools.partial(_nsa_s_sel_kernel, pos0=pos0, npages=npages),
        grid_spec=grid_spec,
        out_shape=jax.ShapeDtypeStruct((b, KV_HEADS, ROWS_S, HEAD_DIM), F32),
        compiler_params=_cparams("parallel", "arbitrary"),
        name="nsa_sample_sel",
    )(page_table, z8, sel, z8, z8, bias_tail, bias_new, bias_far, *([cache6] * (2 * SEL_PAGES)))


def _nsa_s_win_kernel(q_ref, kw_ref, vw_ref, kn_ref, vn_ref, bw_ref, oc_ref, os_ref, gt_ref, gb_ref, o_ref,
                      m_scr, l_scr, acc_scr, *, pos0):
    g = pl.program_id(1)
    wb = kw_ref.shape[0]
    own = pl.ds(g, wb, stride=KV_HEADS)
    kw = kw_ref.reshape(wb * KV_HEADS, HEAD_DIM)[own, :]
    vw = vw_ref.reshape(wb * KV_HEADS, HEAD_DIM)[own, :]
    qs = _stack_heads(q_ref[...] * QK_SCALE).astype(BF16)
    qpos = pos0 + (lax.broadcasted_iota(jnp.int32, (ROWS_S, 1), 0) & (TS - 1))
    bias = bw_ref[...].reshape(ROWS_S, wb + PAGE_SIZE)
    _softmax_init(m_scr, l_scr, acc_scr)
    dist = qpos - (pos0 - wb + lax.broadcasted_iota(jnp.int32, (1, wb), 1))
    s = _dot_nt(qs, kw.astype(BF16)) + bias[:, 0:wb]
    _softmax_update(s, (dist >= 0) & (dist <= WINDOW), vw.astype(BF16), m_scr, l_scr, acc_scr)
    kidx = lax.broadcasted_iota(jnp.int32, (1, PAGE_SIZE), 1)
    dist = qpos - (pos0 + kidx)
    s = _dot_nt(qs, _pad_rows(kn_ref[...], PAGE_SIZE).astype(BF16)) + bias[:, wb:wb + PAGE_SIZE]
    _softmax_update(s, (dist >= 0) & (dist <= WINDOW) & (kidx < TS), _pad_rows(vn_ref[...], PAGE_SIZE).astype(BF16),
                    m_scr, l_scr, acc_scr)
    o_w = _softmax_result(l_scr, acc_scr)
    o_c = oc_ref[...]
    o_s = os_ref[...]
    gt = gt_ref[...] + gb_ref[...]
    for j in range(GROUP):
        col = GATE_NG + (g * GROUP + j) * 3
        r = slice(j * TS, (j + 1) * TS)
        o_ref[:, j * HEAD_DIM:(j + 1) * HEAD_DIM] = (
            _gate(gt, col) * o_c[r] + _gate(gt, col + 1) * o_s[r] + _gate(gt, col + 2) * o_w[r])


def _nsa_sample_win(z8, win6, layer, bias_w, o_c, o_s, gate_b, *, pos0):
    _, b, wb = win6.shape[:3]
    cached = lambda slot: pl.BlockSpec((None, None, wb, None, KV_HEADS, HEAD_DIM),
                                       lambda bi, g: (layer, bi, 0, slot, 0, 0))
    zcol = lambda col: (lambda bi, g: (bi, 0, col // HEAD_DIM + g))
    part = pl.BlockSpec((None, None, ROWS_S, HEAD_DIM), lambda bi, g: (bi, g, 0, 0))
    return pl.pallas_call(
        functools.partial(_nsa_s_win_kernel, pos0=pos0),
        grid=(b, KV_HEADS),
        in_specs=[pl.BlockSpec((None, TS, GROUP * HEAD_DIM), lambda bi, g: (bi, 0, C_NQ // (GROUP * HEAD_DIM) + g)),
                  cached(0), cached(1),
                  pl.BlockSpec((None, TS, HEAD_DIM), zcol(C_NKV + 4 * KV_HEADS * HEAD_DIM)),
                  pl.BlockSpec((None, TS, HEAD_DIM), zcol(C_NKV + 5 * KV_HEADS * HEAD_DIM)),
                  pl.BlockSpec((GROUP, TS, wb + PAGE_SIZE), lambda bi, g: (g, 0, 0)),
                  part, part,
                  pl.BlockSpec((None, TS, LANES), lambda bi, g: (bi, 0, C_GATE // LANES)),
                  pl.BlockSpec((1, LANES), lambda bi, g: (0, 0))],
        out_specs=pl.BlockSpec((None, TS, GROUP * HEAD_DIM), lambda bi, g: (bi, 0, g)),
        out_shape=jax.ShapeDtypeStruct((b, TS, D_NSA), F32),
        scratch_shapes=[pltpu.VMEM((ROWS_S, 1), F32), pltpu.VMEM((ROWS_S, 1), F32), pltpu.VMEM((ROWS_S, HEAD_DIM), F32)],
        compiler_params=_cparams("parallel", "parallel"),
        name="nsa_sample_win",
    )(z8, win6, win6, z8, z8, bias_w, o_c, o_s, z8, gate_b)


def _t5_bucket(dist):
    n = np.maximum(dist, 0)
    exact = NUM_BUCKETS // 2
    nf = np.maximum(n, 1).astype(np.float32)
    large = exact + (np.log(nf / np.float32(exact)) / np.float32(math.log(MAX_DISTANCE / exact))
                     * np.float32(NUM_BUCKETS - exact)).astype(np.int32)
    return np.where(n < exact, n, np.minimum(large, NUM_BUCKETS - 1)).astype(np.int32)


FAR_DIST = int(np.max(np.nonzero(_t5_bucket(np.arange(4 * MAX_DISTANCE)) < NUM_BUCKETS - 1)[0])) + 1
N_TZ = (FAR_DIST + TK - 2) // TQ + 1


def _bias_table(rel_bias, dist):
    onehot = jax.nn.one_hot(jnp.asarray(_t5_bucket(dist).astype(np.int8)), NUM_BUCKETS, dtype=F32)
    return jnp.einsum("rcb,bh->hrc", onehot, rel_bias, precision=lax.Precision.HIGHEST)


def _pack_layer(p, l):
    w_in = p["w_in"][l]
    pad = jnp.zeros((D_MODEL, N_IN - C_GATE - 32), F32)
    w_in_p = jnp.concatenate([w_in[:, 0:2048], w_in[:, 2056:3080], w_in[:, 3080:4616], w_in[:, 4640:5152],
                              w_in[:, 2048:2056], w_in[:, 4616:4640], pad], axis=1).astype(BF16)
    gb = p["mlstm_gate_b"][l]
    mlstm_gate_b = jnp.zeros((1, LANES), F32).at[0, GATE_MI:GATE_MI + 4].set(gb[0]).at[0, GATE_MF:GATE_MF + 4].set(gb[1])
    nsa_gate_b = jnp.zeros((1, LANES), F32).at[0, GATE_NG:GATE_NG + 3 * NSA_HEADS].set(p["nsa_gate_b"][l].reshape(-1))
    w1 = p["cmp_w1"][l]
    wcat = jnp.concatenate([w1[:, 0:CMP_STRIDE], w1[:, CMP_STRIDE:CMP_BLOCK]], axis=-1).astype(BF16)
    lam = lax.complex(p["s5_a_re"][l], p["s5_a_im"][l])
    lam_bar = jnp.exp(lam * jnp.exp(p["s5_log_step"][l])[:, None])
    b_bar = ((lam_bar - 1.0) / lam)[..., None] * lax.complex(p["s5_b_re"][l], p["s5_b_im"][l])
    gi = S5_GROUPS // S5_IN_TILES
    bd_in = lambda m: jnp.einsum("cgph,gk->cghkp", m.reshape(S5_IN_TILES, gi, S5_STATE, S5_GROUP_WIDTH),
                                 jnp.eye(gi, dtype=F32)).reshape(S5_IN_TILES, S5_IN_FEATS, 256)
    wb = jnp.concatenate([bd_in(b_bar.real), bd_in(b_bar.imag)], axis=2).astype(BF16)
    go = S5_GROUPS // S5_OUT_TILES
    bd_out = lambda m: jnp.einsum("jghp,gk->jgpkh", m.reshape(S5_OUT_TILES, go, S5_GROUP_WIDTH, S5_STATE),
                                  jnp.eye(go, dtype=F32)).reshape(S5_OUT_TILES, S5_OUT_CH, 256)
    wc = jnp.stack([bd_out(p["s5_c_re"][l]), -bd_out(p["s5_c_im"][l])]).astype(BF16)
    half = lambda w: jnp.pad(w, ((0, 0), (0, D_FFP - D_FF)))
    w_up = p["ffn_w_up"][l]
    conv_w = p["ffn_conv_w"][l]
    conv_b = p["ffn_conv_b"][l][None, :]
    return dict(
        w_in=w_in_p, mlstm_gate_b=mlstm_gate_b, mlstm_norm_w=p["mlstm_norm_w"][l][None, :], nsa_gate_b=nsa_gate_b,
        wcat=wcat, cmp_b1=p["cmp_b1"][l], cmp_w2=p["cmp_w2"][l].astype(BF16),
        s5_wb=wb, s5_wc=wc, s5_lam_r=lam_bar.real.reshape(1, S5_CH), s5_lam_i=lam_bar.imag.reshape(1, S5_CH),
        s5_d=p["s5_d"][l][None, :], s5_glu_w=p["s5_glu_w"][l].astype(BF16), s5_glu_b=p["s5_glu_b"][l][None, :],
        w_out=p["w_out"][l].astype(BF16), ln1_w=p["ln1_w"][l][None, :], ln1_b=p["ln1_b"][l][None, :],
        w_up=jnp.pad(w_up.astype(BF16), ((0, 0), (0, D_FFP - D_FF))),
        conv_w8=jnp.pad(jnp.concatenate([half(conv_w[:, :D_FF]), half(conv_w[:, D_FF:])], axis=1),
                        ((0, SUBLANES - CONV_W), (0, 0))),
        conv_b=jnp.concatenate([half(conv_b[:, :D_FF]), half(conv_b[:, D_FF:])], axis=1),
        w_down=p["ffn_w_down"][l].astype(BF16),
        ln2_w=p["ln2_w"][l][None, :], ln2_b=p["ln2_b"][l][None, :],
    )


def _unpad_ff(x):
    return jnp.concatenate([x[..., :D_FF], x[..., D_FFP:D_FFP + D_FF]], axis=-1)


def _pad_ff(x):
    pad = [(0, 0)] * (x.ndim - 1) + [(0, D_FFP - D_FF)]
    return jnp.concatenate([jnp.pad(x[..., :D_FF], pad), jnp.pad(x[..., D_FF:], pad)], axis=-1)


def _mixer_tail(x2d, z, b, t, t_use, lw, o_mlstm, o_nsa, s5_state, conv_state8, tm):
    pad8 = lambda s: jnp.pad(s.reshape(b, S5_CH), ((0, SUBLANES - b), (0, 0)))
    o_s5, xr, xi = _s5(z.reshape(b, t, N_IN), lw["s5_wb"], lw["s5_wc"], lw["s5_lam_r"], lw["s5_lam_i"], lw["s5_d"],
                       lw["s5_glu_w"], lw["s5_glu_b"], pad8(s5_state[0]), pad8(s5_state[1]), lc=S5_LC, steps=t_use)
    o_s5 = o_s5.reshape(b * t, D_S5)
    x1 =_wout_ln(o_mlstm.reshape(b * t, D_MLSTM), o_nsa.reshape(b * t, D_NSA), o_s5, x2d,
                  lw["w_out"], lw["ln1_w"], lw["ln1_b"], tm=256)
    s5_new = (xr[:b].reshape(b, S5_GROUPS, S5_STATE), xi[:b].reshape(b, S5_GROUPS, S5_STATE))
    if conv_state8 is None:
        x2, tail_a, tail_g = _ffn_prompt(x1.reshape(b, t, D_MODEL), lw["w_up"], lw["w_down"], lw["conv_w8"],
                                         lw["conv_b"], lw["ln2_w"], lw["ln2_b"], tt=FFN_TT, tf=FFN_TF)
        keep = slice(SUBLANES - (CONV_W - 1), SUBLANES)
        conv_new = jnp.concatenate([tail_a[:, keep, :D_FF], tail_g[:, keep, :D_FF]], axis=-1)
        return x2.reshape(b * t, D_MODEL), s5_new, conv_new
    up = _matmul(x1, lw["w_up"], tm=tm, tn=FFN_TF, n=2 * D_FFP, w_cols=_up_cols(FFN_TF))
    hgate = _convgate(up.reshape(b, t, 2 * D_FFP), conv_state8, lw["conv_w8"], lw["conv_b"], tt=256, tf=512)
    x2 = _down_ln(hgate.reshape(b * t, D_FFP), lw["w_down"], x1, lw["ln2_w"], lw["ln2_b"], tm=512, tk=1408)
    conv_new = _unpad_ff(up.reshape(b, t, 2 * D_FFP)[:, t_use - (CONV_W - 1):t_use])
    return x2, s5_new, conv_new


def _prompt_layer(x2d, b, t, lw, rel_bias):
    z, kv_rows, win_rows = _proj_in(x2d, lw["w_in"], tm=min(1024, b * t), tn=768)
    z3 = z.reshape(b, t, N_IN)
    zeros = lambda *s: jnp.zeros(s, F32)
    o_mlstm, c1, n1, m1 = _mlstm(z3, lw["mlstm_gate_b"], lw["mlstm_norm_w"],
                                 zeros(b, MLSTM_HEADS, HEAD_DIM, HEAD_DIM), zeros(b, MLSTM_HEADS, LANES),
                                 zeros(b, MLSTM_HEADS, LANES), L=MLSTM_L, lin=MLSTM_L, t_valid=t)
    kcvc = _cmp_finish(_cmp_project_prompt(z3, lw["wcat"]), lw["cmp_b1"], lw["cmp_w2"])
    ncp = t // CMP_STRIDE
    bias_c = _bias_table(rel_bias, np.arange(t)[:, None] - (np.arange(ncp) * CMP_STRIDE + CMP_BLOCK - 1)[None, :])
    ti = np.arange(TQ)[:, None] - np.arange(TK)[None, :]
    far = _bias_table(rel_bias, np.full((1, 1), FAR_DIST))
    tz = jnp.stack([_bias_table(rel_bias, d * TQ + ti) - far for d in range(N_TZ)])
    o_nsa = _nsa_prompt(z3, kcvc, bias_c, tz, lw["nsa_gate_b"])
    x2, s5_new, conv_new = _mixer_tail(
        x2d, z, b, t, t, lw, o_mlstm, o_nsa, (zeros(b, S5_GROUPS, S5_STATE), zeros(b, S5_GROUPS, S5_STATE)),
        None, tm=1024)
    wrows = min(WINDOW, t)
    state = (kv_rows.reshape(b, t, 4, KV_HEADS, HEAD_DIM), win_rows.reshape(b, t, 2, KV_HEADS, HEAD_DIM)[:, t - wrows:],
             c1, n1, m1[:, :, 0], s5_new[0], s5_new[1], conv_new)
    return x2, state


def _sample_layer(x2d, b, tn, lw, rel_bias, layer, cache6, page_table, win6, mlstm_state, s5_state, conv_state):
    npages = page_table.shape[1]
    pos0 = npages * PAGE_SIZE
    z = _matmul(x2d, lw["w_in"], tm=1024, tn=768)
    z8 = z.reshape(b, TS, N_IN)
    c0, n0, m0 = mlstm_state
    o_mlstm, c1, n1, m1 = _mlstm(z8, lw["mlstm_gate_b"], lw["mlstm_norm_w"], c0, n0,
                                 jnp.broadcast_to(m0[:, :, None], (b, MLSTM_HEADS, LANES)),
                                 L=LANES, lin=TS, t_valid=tn)
    n_chunks = (pos0 + tn) // CMP_STRIDE
    n_cmp = n_chunks - CMP_BLOCK // CMP_STRIDE + 1
    n_sel = -(-(pos0 + tn) // SEL_BLOCK)
    kcvc = _cmp_finish(_cmp_project_sample(cache6, page_table, layer, lw["wcat"]), lw["cmp_b1"], lw["cmp_w2"])
    qpos = pos0 + np.arange(TS)[:, None]
    bias_c = _bias_table(rel_bias, qpos - (np.arange(n_chunks) * CMP_STRIDE + CMP_BLOCK - 1)[None, :])
    o_c, sel = _nsa_sample_cmp(z8, kcvc, bias_c, pos0=pos0, n_cmp=n_cmp, n_sel=n_sel)
    kk = np.arange(PAGE_SIZE)[None, :]
    bias_tail = _bias_table(rel_bias, np.repeat(qpos - (pos0 - PAGE_SIZE + kk), KV_HEADS, axis=1))
    bias_new = _bias_table(rel_bias, qpos - (pos0 + kk))
    bias_far = _bias_table(rel_bias, np.broadcast_to(qpos - (pos0 - 2 * PAGE_SIZE), (TS, PAGE_SIZE)))
    o_s = _nsa_sample_sel(z8, sel, bias_tail, bias_new, bias_far, cache6, page_table, layer, pos0=pos0)
    wb = win6.shape[2]
    wk = np.arange(wb)[None, :]
    bias_w = _bias_table(rel_bias, np.concatenate([qpos - (pos0 - wb + wk), qpos - (pos0 + kk)], axis=1))
    o_nsa = _nsa_sample_win(z8, win6, layer, bias_w, o_c, o_s, lw["nsa_gate_b"], pos0=pos0)
    conv_state8 = jnp.pad(_pad_ff(conv_state), ((0, 0), (SUBLANES - (CONV_W - 1), 0), (0, 0)))
    x2, s5_new, conv_new = _mixer_tail(x2d, z, b, TS, tn, lw, o_mlstm, o_nsa, s5_state, conv_state8, tm=1024)
    nkv = z8[:, :tn, C_NKV:C_NKV + N_KV_SLOTS * KV_HEADS * HEAD_DIM].reshape(b, tn, N_KV_SLOTS, KV_HEADS, HEAD_DIM)
    state = (nkv[:, :, :4], nkv[:, :, 4:], c1, n1, m1[:, :, 0], s5_new[0], s5_new[1], conv_new)
    return x2, state


def kernel(x_prompt, x_sample, cache_nsa_kv, cache_win_kv, state_mlstm_c, state_mlstm_n, state_mlstm_m,
           state_s5_re, state_s5_im, state_ffn_conv, page_table, w_in, mlstm_gate_b, mlstm_norm_w,
           nsa_gate_b, cmp_w1, cmp_b1, cmp_w2, rel_bias, s5_a_re, s5_a_im, s5_b_re, s5_b_im, s5_c_re,
           s5_c_im, s5_d, s5_log_step, s5_glu_w, s5_glu_b, w_out, ln1_w, ln1_b, ffn_w_up, ffn_conv_w,
           ffn_conv_b, ffn_w_down, ln2_w, ln2_b):
    params = dict(w_in=w_in, mlstm_gate_b=mlstm_gate_b, mlstm_norm_w=mlstm_norm_w, nsa_gate_b=nsa_gate_b,
                  cmp_w1=cmp_w1, cmp_b1=cmp_b1, cmp_w2=cmp_w2, s5_a_re=s5_a_re, s5_a_im=s5_a_im, s5_b_re=s5_b_re,
                  s5_b_im=s5_b_im, s5_c_re=s5_c_re, s5_c_im=s5_c_im, s5_d=s5_d, s5_log_step=s5_log_step,
                  s5_glu_w=s5_glu_w, s5_glu_b=s5_glu_b, w_out=w_out, ln1_w=ln1_w, ln1_b=ln1_b, ffn_w_up=ffn_w_up,
                  ffn_conv_w=ffn_conv_w, ffn_conv_b=ffn_conv_b, ffn_w_down=ffn_w_down, ln2_w=ln2_w, ln2_b=ln2_b)
    depth = w_in.shape[0]
    bp, tp, _ = x_prompt.shape
    bs, tn, _ = x_sample.shape
    assert tp % TK == 0 and tn < CMP_STRIDE and tn <= TS and SUBLANES % bp == 0 and SUBLANES % bs == 0
    assert (tn * bs) % SUBLANES == 0 and tp % min(FFN_TT, tp) == 0
    assert page_table.shape[1] % CMP_PAGES == 0 and cache_nsa_kv.shape[2] == PAGE_SIZE
    xp = x_prompt.reshape(bp * tp, D_MODEL)
    xs = jnp.pad(x_sample, ((0, 0), (0, TS - tn), (0, 0))).reshape(bs * TS, D_MODEL)
    p_states, s_states = [], []
    for l in range(depth):
        lw = _pack_layer(params, l)
        xp, sp = _prompt_layer(xp, bp, tp, lw, rel_bias)
        xs, ss = _sample_layer(xs, bs, tn, lw, rel_bias, l, cache_nsa_kv, page_table, cache_win_kv,
                               (state_mlstm_c[l], jnp.pad(state_mlstm_n[l], ((0, 0), (0, 0), (0, LANES - HEAD_DIM))),
                                state_mlstm_m[l]),
                               (state_s5_re[l], state_s5_im[l]), state_ffn_conv[l])
        p_states.append(sp)
        s_states.append(ss)
    stk = lambda states, i: jnp.stack([s[i] for s in states])
    y_prompt = xp.reshape(bp, tp, D_MODEL)
    y_sample = xs.reshape(bs, TS, D_MODEL)[:, :tn]
    return (y_prompt, y_sample,
            stk(p_states, 0), stk(s_states, 0), stk(p_states, 1), stk(s_states, 1),
            stk(p_states, 2), stk(s_states, 2), stk(p_states, 3), stk(s_states, 3), stk(p_states, 4), stk(s_states, 4),
            stk(p_states, 5), stk(s_states, 5), stk(p_states, 6), stk(s_states, 6), stk(p_states, 7), stk(s_states, 7))
```

```python
import functools
import math

import jax
import jax.numpy as jnp
import numpy as np
from jax import lax
from jax.experimental import pallas as pl
from jax.experimental.pallas import tpu as pltpu

F32 = jnp.float32
BF16 = jnp.bfloat16
NEG_INF = float("-inf")
M_INIT = -1e30
MASKED = -1e30

D_MODEL = 2048
PAGE_SIZE = 128
D_MLSTM = D_MODEL // 4
D_NSA = D_MODEL // 2
D_S5 = D_MODEL - D_MLSTM - D_NSA
HEAD_DIM = 128
MLSTM_HEADS = D_MLSTM // HEAD_DIM
NSA_HEADS = D_NSA // HEAD_DIM
KV_HEADS = 2
GROUP = NSA_HEADS // KV_HEADS
N_KV_SLOTS = 6
CMP_BLOCK = 32
CMP_STRIDE = 16
SEL_BLOCK = 64
SEL_TOPK = 16
FORCE_SCORE = 1e4
WINDOW = 512
S5_GROUP_WIDTH = 16
S5_GROUPS = D_S5 // S5_GROUP_WIDTH
S5_STATE = 64
S5_CH = S5_GROUPS * S5_STATE
NUM_BUCKETS = 32
MAX_DISTANCE = 128
D_FF = ((8 * D_MODEL // 3 + 127) // 128) * 128
CONV_W = 3
LN_EPS = 1e-5
DEPTH = 2
DEEPNORM_ALPHA = (2 * DEPTH) ** 0.25
QK_SCALE = HEAD_DIM ** -0.5

LANES = 128
SUBLANES = 8
V7X_VMEM_LIMIT = 56 * 2 ** 20

C_MQ, C_MK, C_MV, C_MO = 0, 512, 1024, 1536
C_NQ = 2048
C_NKV = 3072
C_SU = 4608
C_GATE = 5120
N_IN = 5376
GATE_MI, GATE_MF, GATE_NG = 0, 4, 8
D_FFP = 5632
MLSTM_L = 256
TQ = 256
TK = 256
S5_LC = 256
FFN_TT = 1024
FFN_TF = 512
FFN_HALO = 16
CMP_PAGES = 32
SEL_PAGES = 32
SEL_TILE = 32
SEL_SHIFT = SEL_BLOCK.bit_length() - 1


def _cparams(*sem):
    return pltpu.CompilerParams(dimension_semantics=sem, vmem_limit_bytes=V7X_VMEM_LIMIT)


def _dot(a, b):
    return jnp.dot(a, b, preferred_element_type=F32)


def _dot_nt(a, b):
    return lax.dot_general(a, b, (((1,), (1,)), ((), ())), preferred_element_type=F32)


def _layer_norm(y, w, b):
    mu = jnp.mean(y, axis=-1, keepdims=True)
    d = y - mu
    var = jnp.mean(d * d, axis=-1, keepdims=True)
    return d * lax.rsqrt(var + LN_EPS) * w + b


def _mm_kernel(x_ref, w_ref, o_ref):
    o_ref[...] = _dot(x_ref[...].astype(BF16), w_ref[...]).astype(o_ref.dtype)


def _up_cols(tf):
    nf, per = D_FFP // tf, tf // LANES
    return lambda j: (j * per + (j >= nf).astype(jnp.int32) * (D_FF // LANES - nf * per)) * LANES


def _matmul(x, w, layer, *, tm, tn, n=None, w_cols=None):
    m, k = x.shape
    n = w.shape[2] if n is None else n
    tm = min(tm, m)
    if w_cols is None:
        w_spec = pl.BlockSpec((None, k, tn), lambda i, j: (layer, 0, j))
    else:
        w_spec = pl.BlockSpec((None, pl.Element(k), pl.Element(tn)), lambda i, j: (layer, 0, w_cols(j)))
    return pl.pallas_call(
        _mm_kernel,
        grid=(m // tm, n // tn),
        in_specs=[pl.BlockSpec((tm, k), lambda i, j: (i, 0)), w_spec],
        out_specs=pl.BlockSpec((tm, tn), lambda i, j: (i, j)),
        out_shape=jax.ShapeDtypeStruct((m, n), F32),
        compiler_params=_cparams("parallel", "parallel"),
        name="proj",
    )(x, w)


KV4_GROUPS = 4 * KV_HEADS
WIN_GROUPS = 2 * KV_HEADS


def _proj_in_kernel(x_ref, w_ref, z_ref, kv_ref, win_ref, *, tn):
    j = pl.program_id(1)
    acc = _dot(x_ref[...].astype(BF16), w_ref[...])
    z_ref[...] = acc
    tm = acc.shape[0]
    per_tile = tn // HEAD_DIM
    first_tile = C_NKV // tn
    for tile in range(first_tile, first_tile + N_KV_SLOTS * KV_HEADS // per_tile):
        @pl.when(j == tile)
        def _(tile=tile):
            for c in range(per_tile):
                grp = (tile - first_tile) * per_tile + c
                val = acc[:, c * HEAD_DIM:(c + 1) * HEAD_DIM]
                if grp < KV4_GROUPS:
                    kv_ref[pl.ds(grp, tm, stride=KV4_GROUPS), :] = val
                else:
                    win_ref[pl.ds(grp - KV4_GROUPS, tm, stride=WIN_GROUPS), :] = val


def _proj_in(x, w, layer, *, tm, tn):
    m, k = x.shape
    n = w.shape[2]
    assert C_NKV % tn == 0 and (N_KV_SLOTS * KV_HEADS * HEAD_DIM) % tn == 0
    return pl.pallas_call(
        functools.partial(_proj_in_kernel, tn=tn),
        grid=(m // tm, n // tn),
        in_specs=[pl.BlockSpec((tm, k), lambda i, j: (i, 0)), pl.BlockSpec((None, k, tn), lambda i, j: (layer, 0, j))],
        out_specs=[pl.BlockSpec((tm, tn), lambda i, j: (i, j)),
                   pl.BlockSpec((tm * KV4_GROUPS, HEAD_DIM), lambda i, j: (i, 0)),
                   pl.BlockSpec((tm * WIN_GROUPS, HEAD_DIM), lambda i, j: (i, 0))],
        out_shape=[jax.ShapeDtypeStruct((m, n), F32),
                   jax.ShapeDtypeStruct((m * KV4_GROUPS, HEAD_DIM), F32),
                   jax.ShapeDtypeStruct((m * WIN_GROUPS, HEAD_DIM), F32)],
        compiler_params=_cparams("parallel", "arbitrary"),
        name="proj_in",
    )(x, w)


def _wout_kernel(om_ref, on_ref, os_ref, x_ref, w_ref, lw_ref, lb_ref, o_ref):
    acc = _dot(om_ref[...].astype(BF16), w_ref[0:D_MLSTM, :])
    acc += _dot(on_ref[...].astype(BF16), w_ref[D_MLSTM:D_MLSTM + D_NSA, :])
    acc += _dot(os_ref[...].astype(BF16), w_ref[D_MLSTM + D_NSA:D_MODEL, :])
    o_ref[...] = _layer_norm(DEEPNORM_ALPHA * x_ref[...] + acc, lw_ref[...], lb_ref[...])


def _wout_ln(om, on, os_, x, w, layer, lw, lb, *, tm):
    m = x.shape[0]
    tm = min(tm, m)
    row = lambda width: pl.BlockSpec((tm, width), lambda i: (i, 0))
    full = lambda a: pl.BlockSpec(a.shape, lambda i: (0, 0))
    w_spec = pl.BlockSpec((None,) + w.shape[1:], lambda i: (layer, 0, 0))
    return pl.pallas_call(
        _wout_kernel,
        grid=(m // tm,),
        in_specs=[row(D_MLSTM), row(D_NSA), row(D_S5), row(D_MODEL), w_spec, full(lw), full(lb)],
        out_specs=row(D_MODEL),
        out_shape=jax.ShapeDtypeStruct((m, D_MODEL), F32),
        compiler_params=_cparams("parallel"),
        name="wout_ln",
    )(om, on, os_, x, w, lw, lb)


def _valid_rows(w_ref, tile):
    rows = w_ref.shape[0]
    row = tile * rows + lax.broadcasted_iota(jnp.int32, (rows, 1), 0)
    w = w_ref[...]
    return jnp.where(row < D_FF, w, jnp.zeros_like(w))


def _down_kernel(h_ref, w_ref, x_ref, lw_ref, lb_ref, o_ref, acc_ref, *, nk):
    k = pl.program_id(1)

    @pl.when(k == 0)
    def _():
        acc_ref[...] = jnp.zeros_like(acc_ref)

    acc_ref[...] += _dot(h_ref[...], _valid_rows(w_ref, k))

    @pl.when(k == nk - 1)
    def _():
        o_ref[...] = _layer_norm(DEEPNORM_ALPHA * x_ref[...] + acc_ref[...], lw_ref[...], lb_ref[...])


def _down_ln(h, w, layer, x, lw, lb, *, tm, tk):
    m, kk = h.shape
    tm = min(tm, m)
    nk = kk // tk
    return pl.pallas_call(
        functools.partial(_down_kernel, nk=nk),
        grid=(m // tm, nk),
        in_specs=[
            pl.BlockSpec((tm, tk), lambda i, k: (i, k)),
            pl.BlockSpec((None, tk, D_MODEL), lambda i, k: (layer, k, 0)),
            pl.BlockSpec((tm, D_MODEL), lambda i, k: (i, 0)),
            pl.BlockSpec((1, D_MODEL), lambda i, k: (0, 0)),
            pl.BlockSpec((1, D_MODEL), lambda i, k: (0, 0)),
        ],
        out_specs=pl.BlockSpec((tm, D_MODEL), lambda i, k: (i, 0)),
        out_shape=jax.ShapeDtypeStruct((m, D_MODEL), F32),
        scratch_shapes=[pltpu.VMEM((tm, D_MODEL), F32)],
        compiler_params=_cparams("parallel", "arbitrary"),
        name="down_ln",
    )(h, w, x, lw, lb)


def _convgate_kernel(a_ref, g_ref, ha_ref, hg_ref, sa_ref, sg_ref, wa_ref, wg_ref, ba_ref, bg_ref, o_ref):
    first = pl.program_id(1) == 0
    tt = a_ref.shape[0]
    rows = lax.broadcasted_iota(jnp.int32, (tt, 1), 0)

    def conv(cur_ref, halo_ref, st_ref, w_ref, b_ref):
        cur = cur_ref[...]
        prev = jnp.where(first, st_ref[...], halo_ref[...])
        p1 = prev[7:8, :]
        p2 = prev[6:7, :]
        x1 = jnp.where(rows == 0, p1, pltpu.roll(cur, 1, 0))
        x2 = jnp.where(rows == 0, p2, jnp.where(rows == 1, p1, pltpu.roll(cur, 2, 0)))
        w = w_ref[...]
        return b_ref[...] + w[0:1, :] * x2 + w[1:2, :] * x1 + w[2:3, :] * cur

    a = conv(a_ref, ha_ref, sa_ref, wa_ref, ba_ref)
    g = conv(g_ref, hg_ref, sg_ref, wg_ref, bg_ref)
    o_ref[...] = (a * jax.nn.sigmoid(a) * g).astype(o_ref.dtype)


def _convgate(up, state8, conv_w8, conv_b, *, tt, tf):
    b, t, _ = up.shape
    tt = min(tt, t)
    nf = D_FFP // tf
    hb = tt // SUBLANES
    cur_a = pl.BlockSpec((None, tt, tf), lambda bi, ti, fi: (bi, ti, fi))
    cur_g = pl.BlockSpec((None, tt, tf), lambda bi, ti, fi: (bi, ti, fi + nf))
    halo_a = pl.BlockSpec((None, SUBLANES, tf), lambda bi, ti, fi: (bi, jnp.maximum(ti * hb - 1, 0), fi))
    halo_g = pl.BlockSpec((None, SUBLANES, tf), lambda bi, ti, fi: (bi, jnp.maximum(ti * hb - 1, 0), fi + nf))
    st_a = pl.BlockSpec((None, SUBLANES, tf), lambda bi, ti, fi: (bi, 0, fi))
    st_g = pl.BlockSpec((None, SUBLANES, tf), lambda bi, ti, fi: (bi, 0, fi + nf))
    w_a = pl.BlockSpec((SUBLANES, tf), lambda bi, ti, fi: (0, fi))
    w_g = pl.BlockSpec((SUBLANES, tf), lambda bi, ti, fi: (0, fi + nf))
    b_a = pl.BlockSpec((1, tf), lambda bi, ti, fi: (0, fi))
    b_g = pl.BlockSpec((1, tf), lambda bi, ti, fi: (0, fi + nf))
    return pl.pallas_call(
        _convgate_kernel,
        grid=(b, t // tt, nf),
        in_specs=[cur_a, cur_g, halo_a, halo_g, st_a, st_g, w_a, w_g, b_a, b_g],
        out_specs=pl.BlockSpec((None, tt, tf), lambda bi, ti, fi: (bi, ti, fi)),
        out_shape=jax.ShapeDtypeStruct((b, t, D_FFP), BF16),
        compiler_params=_cparams("parallel", "parallel", "parallel"),
        name="convgate",
    )(up, up, up, up, state8, state8, conv_w8, conv_w8, conv_b, conv_b)


def _ffn_kernel(x_ref, xh_ref, wa_ref, wg_ref, wd_ref, cwa_ref, cwg_ref, cba_ref, cbg_ref, lw_ref, lb_ref,
                o_ref, sa_ref, sg_ref, xe_ref, *, nf):
    first = pl.program_id(1) == 0
    f = pl.program_id(2)
    tt = x_ref.shape[0]

    @pl.when(f == 0)
    def _():
        o_ref[...] = jnp.zeros_like(o_ref)
        halo = jnp.where(first, 0.0, xh_ref[...])
        xe_ref[0:FFN_HALO, :] = halo.astype(BF16)
        xe_ref[FFN_HALO:, :] = x_ref[...].astype(BF16)

    xe = xe_ref[...]

    def branch(w_ref, cw_ref, cb_ref, s_ref):
        up = _dot(xe, w_ref[...])
        s_ref[...] = up[FFN_HALO + tt - SUBLANES:FFN_HALO + tt, :]
        w = cw_ref[...]
        x1 = pltpu.roll(up, 1, 0)[FFN_HALO:, :]
        x2 = pltpu.roll(up, 2, 0)[FFN_HALO:, :]
        return cb_ref[...] + w[0:1, :] * x2 + w[1:2, :] * x1 + w[2:3, :] * up[FFN_HALO:, :]

    a = branch(wa_ref, cwa_ref, cba_ref, sa_ref)
    g = branch(wg_ref, cwg_ref, cbg_ref, sg_ref)
    o_ref[...] += _dot((a * jax.nn.sigmoid(a) * g).astype(BF16), _valid_rows(wd_ref, f))

    @pl.when(f == nf - 1)
    def _():
        o_ref[...] = _layer_norm(DEEPNORM_ALPHA * x_ref[...] + o_ref[...], lw_ref[...], lb_ref[...])


def _ffn_prompt(x3, w_up, w_down, layer, conv_w8, conv_b, lw, lb, *, tt, tf):
    b, t, _ = x3.shape
    nf = D_FFP // tf
    tt = min(tt, t)
    hb = tt // FFN_HALO
    half_a = lambda rows: pl.BlockSpec((rows, tf), lambda bi, ti, fi: (0, fi))
    half_g = lambda rows: pl.BlockSpec((rows, tf), lambda bi, ti, fi: (0, fi + nf))
    cols = _up_cols(tf)
    w_a = pl.BlockSpec((None, pl.Element(D_MODEL), pl.Element(tf)), lambda bi, ti, fi: (layer, 0, cols(fi)))
    w_g = pl.BlockSpec((None, pl.Element(D_MODEL), pl.Element(tf)), lambda bi, ti, fi: (layer, 0, cols(fi + nf)))
    vec = pl.BlockSpec((1, D_MODEL), lambda bi, ti, fi: (0, 0))
    tail = pl.BlockSpec((None, SUBLANES, tf), lambda bi, ti, fi: (bi, 0, fi))
    return pl.pallas_call(
        functools.partial(_ffn_kernel, nf=nf),
        grid=(b, t // tt, nf),
        in_specs=[pl.BlockSpec((None, tt, D_MODEL), lambda bi, ti, fi: (bi, ti, 0), pipeline_mode=pl.Buffered(1)),
                  pl.BlockSpec((None, FFN_HALO, D_MODEL), lambda bi, ti, fi: (bi, jnp.maximum(ti * hb - 1, 0), 0)),
                  w_a, w_g,
                  pl.BlockSpec((None, tf, D_MODEL), lambda bi, ti, fi: (layer, fi, 0)),
                  half_a(SUBLANES), half_g(SUBLANES), half_a(1), half_g(1), vec, vec],
        out_specs=[pl.BlockSpec((None, tt, D_MODEL), lambda bi, ti, fi: (bi, ti, 0)), tail, tail],
        out_shape=[jax.ShapeDtypeStruct((b, t, D_MODEL), F32),
                   jax.ShapeDtypeStruct((b, SUBLANES, D_FFP), F32),
                   jax.ShapeDtypeStruct((b, SUBLANES, D_FFP), F32)],
        scratch_shapes=[pltpu.VMEM((tt + FFN_HALO, D_MODEL), BF16)],
        compiler_params=_cparams("parallel", "arbitrary", "arbitrary"),
        name="ffn_prompt",
    )(x3, x3, w_up, w_up, w_down, conv_w8, conv_w8, conv_b, conv_b, lw, lb)


def _cumsum_rows(x):
    n = x.shape[0]
    rows = lax.broadcasted_iota(jnp.int32, (n, 1), 0)
    d = 1
    while d < n:
        x = x + jnp.where(rows >= d, pltpu.roll(x, d, 0), 0.0)
        d *= 2
    return x


def _log_sigmoid(x):
    return jnp.minimum(x, 0.0) - jnp.log1p(jnp.exp(-jnp.abs(x)))


def _mlstm_kernel(q_ref, k_ref, v_ref, og_ref, g_ref, gb_ref, nw_ref, c0_ref, n0_ref, m0_ref,
                  out_ref, c_ref, n_ref, m_ref, *, L, t_valid):
    ci = pl.program_id(1)
    lin = q_ref.shape[0]

    @pl.when(ci == 0)
    def _():
        c_ref[...] = c0_ref[...]
        n_ref[...] = n0_ref[...]
        m_ref[...] = m0_ref[...]

    def rows_of(ref):
        x = ref[...]
        if lin < L:
            x = jnp.concatenate([x, jnp.zeros((L - lin, x.shape[1]), x.dtype)], axis=0)
        return x

    rows = lax.broadcasted_iota(jnp.int32, (L, 1), 0)
    valid = (ci * L + rows) < t_valid
    pre = rows_of(g_ref) + gb_ref[...]
    lf = jnp.where(valid, _log_sigmoid(pre), 0.0)
    ig = jnp.where(valid, pre, NEG_INF)
    bcum = _cumsum_rows(lf)
    dt = (pltpu.roll(ig, GATE_MF - GATE_MI, 1) - bcum).T
    q_all, k_all, v_all, og_all = rows_of(q_ref), rows_of(k_ref), rows_of(v_ref), rows_of(og_ref)
    tri = lax.broadcasted_iota(jnp.int32, (L, L), 0) >= lax.broadcasted_iota(jnp.int32, (L, L), 1)
    nw = nw_ref[...]

    for h in range(MLSTM_HEADS):
        sl = slice(h * HEAD_DIM, (h + 1) * HEAD_DIM)
        q = q_all[:, sl]
        k = k_all[:, sl] * QK_SCALE
        v = v_all[:, sl]
        qb, kb, vb = q.astype(BF16), k.astype(BF16), v.astype(BF16)
        b_col = bcum[:, GATE_MF + h:GATE_MF + h + 1]
        ig_col = ig[:, GATE_MI + h:GATE_MI + h + 1]
        d_row = dt[GATE_MF + h:GATE_MF + h + 1, :]
        c_prev = c_ref[h]
        n_prev = n_ref[h:h + 1, :]
        m_prev = m_ref[h:h + 1, 0:1]

        dmat = jnp.where(tri, b_col + d_row, NEG_INF)
        g_col = b_col + m_prev
        m_row = jnp.maximum(jnp.max(dmat, axis=1, keepdims=True), g_col)
        a = jnp.exp(dmat - m_row) * _dot_nt(qb, kb)
        w_inter = jnp.exp(g_col - m_row)
        num = _dot(a.astype(BF16), vb) + w_inter * _dot(qb, c_prev.astype(BF16))
        den = jnp.sum(a, axis=1, keepdims=True) + w_inter * jnp.sum(q * n_prev, axis=1, keepdims=True)
        hid = num / jnp.maximum(jnp.abs(den), jnp.exp(-m_row))
        mu = jnp.mean(hid, axis=1, keepdims=True)
        dlt = hid - mu
        var = jnp.mean(dlt * dlt, axis=1, keepdims=True)
        hn = dlt * lax.rsqrt(var + LN_EPS) * nw[:, sl]
        res = jax.nn.sigmoid(og_all[:, sl]) * hn
        out_ref[:, sl] = res[0:lin, :]

        f_tot = b_col[L - 1:L, :]
        w_s = f_tot - b_col + ig_col
        m_new = jnp.maximum(f_tot + m_prev, jnp.max(w_s, axis=0, keepdims=True))
        ws = jnp.exp(w_s - m_new)
        decay = jnp.exp(f_tot + m_prev - m_new)
        c_ref[h] = decay * c_prev + _dot(k.T.astype(BF16), (ws * v).astype(BF16))
        n_ref[h:h + 1, :] = decay * n_prev + jnp.sum(ws * k, axis=0, keepdims=True)
        m_ref[h:h + 1, :] = jnp.broadcast_to(m_new, (1, LANES))


def _mlstm(z, gate_b, norm_w, c0, n0, m0, *, L, lin, t_valid):
    b, tz, _ = z.shape
    nchunks = tz // lin
    qblk = lambda col: pl.BlockSpec((None, lin, D_MLSTM), lambda bi, ci: (bi, ci, col // D_MLSTM))
    st4 = pl.BlockSpec((None, MLSTM_HEADS, HEAD_DIM, HEAD_DIM), lambda bi, ci: (bi, 0, 0, 0))
    st3 = pl.BlockSpec((None, MLSTM_HEADS, LANES), lambda bi, ci: (bi, 0, 0))
    return pl.pallas_call(
        functools.partial(_mlstm_kernel, L=L, t_valid=t_valid),
        grid=(b, nchunks),
        in_specs=[qblk(C_MQ), qblk(C_MK), qblk(C_MV), qblk(C_MO),
                  pl.BlockSpec((None, lin, LANES), lambda bi, ci: (bi, ci, C_GATE // LANES)),
                  pl.BlockSpec((1, LANES), lambda bi, ci: (0, 0)),
                  pl.BlockSpec((1, D_MLSTM), lambda bi, ci: (0, 0)),
                  st4, st3, st3],
        out_specs=[pl.BlockSpec((None, lin, D_MLSTM), lambda bi, ci: (bi, ci, 0)), st4, st3, st3],
        out_shape=[jax.ShapeDtypeStruct((b, tz, D_MLSTM), F32),
                   jax.ShapeDtypeStruct((b, MLSTM_HEADS, HEAD_DIM, HEAD_DIM), F32),
                   jax.ShapeDtypeStruct((b, MLSTM_HEADS, LANES), F32),
                   jax.ShapeDtypeStruct((b, MLSTM_HEADS, LANES), F32)],
        compiler_params=_cparams("parallel", "arbitrary"),
        name="mlstm",
    )(z, z, z, z, z, gate_b, norm_w, c0, n0, m0)


S5_IN_TILES = S5_CH // 256
S5_IN_FEATS = D_S5 // S5_IN_TILES
S5_OUT_TILES = D_S5 // 256
S5_OUT_CH = S5_CH // S5_OUT_TILES


S5_LT = S5_CH // LANES


def _lane_tiles(x):
    return jnp.stack([x[:, k * LANES:(k + 1) * LANES] for k in range(x.shape[1] // LANES)])


def _lane_untile(x):
    return jnp.concatenate([x[k] for k in range(x.shape[0])], axis=1)


def _s5_kernel(u_ref, wb_ref, wc_ref, lr_ref, li_ref, d_ref, gw_ref, gb_ref, x0r_ref, x0i_ref,
               o_ref, xr_ref, xi_ref, sr_ref, si_ref, y_ref, *, steps):
    nb, lc, _ = u_ref.shape

    @pl.when(pl.program_id(0) == 0)
    def _():
        xr_ref[...] = x0r_ref[...]
        xi_ref[...] = x0i_ref[...]

    for b in range(nb):
        ub = u_ref[b].astype(BF16)
        for c in range(S5_IN_TILES):
            bu = _dot(ub[:, c * S5_IN_FEATS:(c + 1) * S5_IN_FEATS], wb_ref[c])
            for k in range(256 // LANES):
                rows = pl.ds(b, lc, stride=nb)
                sr_ref.at[c * (256 // LANES) + k][rows, :] = bu[:, k * LANES:(k + 1) * LANES]
                si_ref.at[c * (256 // LANES) + k][rows, :] = bu[:, 256 + k * LANES:256 + (k + 1) * LANES]
    lam_r = _lane_tiles(jnp.broadcast_to(lr_ref[...], (SUBLANES, S5_CH)))
    lam_i = _lane_tiles(jnp.broadcast_to(li_ref[...], (SUBLANES, S5_CH)))
    row8 = lax.broadcasted_iota(jnp.int32, (1, SUBLANES, 1), 1)
    per_group = SUBLANES // nb

    def group(i, carry):
        xr, xi = carry
        r0 = pl.multiple_of(i * SUBLANES, SUBLANES)
        br = sr_ref[:, pl.ds(r0, SUBLANES), :]
        bi = si_ref[:, pl.ds(r0, SUBLANES), :]
        out_r, out_i = br, bi
        for k in range(per_group):
            nr = lam_r * xr - lam_i * xi + br
            ni = lam_r * xi + lam_i * xr + bi
            here = (row8 >= k * nb) & (row8 < (k + 1) * nb)
            out_r = jnp.where(here, nr, out_r)
            out_i = jnp.where(here, ni, out_i)
            if per_group > 1:
                xr = pltpu.roll(nr, nb, 1)
                xi = pltpu.roll(ni, nb, 1)
            else:
                xr, xi = nr, ni
        sr_ref[:, pl.ds(r0, SUBLANES), :] = out_r
        si_ref[:, pl.ds(r0, SUBLANES), :] = out_i
        return xr, xi

    xr, xi = lax.fori_loop(0, steps * nb // SUBLANES, group, (_lane_tiles(xr_ref[...]), _lane_tiles(xi_ref[...])))
    xr_ref[...] = _lane_untile(xr)
    xi_ref[...] = _lane_untile(xi)
    per_out = S5_OUT_CH // LANES
    for j in range(S5_OUT_TILES):
        lhs_r = jnp.concatenate([sr_ref[j * per_out + k] for k in range(per_out)], axis=1).astype(BF16)
        lhs_i = jnp.concatenate([si_ref[j * per_out + k] for k in range(per_out)], axis=1).astype(BF16)
        y = _dot(lhs_r, wc_ref[0, j]) + _dot(lhs_i, wc_ref[1, j])
        for k in range(256 // LANES):
            y_ref[j * (256 // LANES) + k] = y[:, k * LANES:(k + 1) * LANES]
    for b in range(nb):
        rows = pl.ds(b, lc, stride=nb)
        y = jnp.concatenate([y_ref.at[k][rows, :] for k in range(D_S5 // LANES)], axis=1) + d_ref[...] * u_ref[b]
        zz = jax.nn.gelu(y)
        o_ref[b] = zz * jax.nn.sigmoid(_dot(zz.astype(BF16), gw_ref[...]) + gb_ref[...])


def _s5(z3, wb, wc, lam_r, lam_i, d, glu_w, glu_b, x0r, x0i, *, lc, steps):
    b, tz, _ = z3.shape
    lc = min(lc, tz)
    steps = min(steps, lc)
    assert steps == lc or tz == lc
    full = lambda a: pl.BlockSpec(a.shape, lambda i: (0,) * a.ndim)
    st = pl.BlockSpec((SUBLANES, S5_CH), lambda i: (0, 0))
    return pl.pallas_call(
        functools.partial(_s5_kernel, steps=steps),
        grid=(tz // lc,),
        in_specs=[pl.BlockSpec((b, lc, D_S5), lambda i: (0, i, C_SU // D_S5)),
                  full(wb), full(wc), full(lam_r), full(lam_i), full(d), full(glu_w), full(glu_b), st, st],
        out_specs=[pl.BlockSpec((b, lc, D_S5), lambda i: (0, i, 0)), st, st],
        out_shape=[jax.ShapeDtypeStruct((b, tz, D_S5), F32),
                   jax.ShapeDtypeStruct((SUBLANES, S5_CH), F32),
                   jax.ShapeDtypeStruct((SUBLANES, S5_CH), F32)],
        scratch_shapes=[pltpu.VMEM((S5_LT, lc * b, LANES), F32), pltpu.VMEM((S5_LT, lc * b, LANES), F32),
                        pltpu.VMEM((D_S5 // LANES, lc * b, LANES), F32)],
        compiler_params=_cparams("arbitrary"),
        name="s5",
    )(z3, wb, wc, lam_r, lam_i, d, glu_w, glu_b, x0r, x0i)


N_CMB = 2 * KV_HEADS


def _cmp_project(rows_of, w_ref, o_ref):
    nch = o_ref.shape[0]
    for cmb in range(N_CMB):
        slot = cmb // KV_HEADS
        rows_ref = rows_of(cmb)
        acc = jnp.zeros((nch, 2 * HEAD_DIM), F32)
        for l in range(CMP_STRIDE):
            x = rows_ref[pl.ds(l, nch, stride=CMP_STRIDE), :]
            acc += _dot(x.astype(BF16), w_ref[slot, l])
        o_ref[:, cmb * 2 * HEAD_DIM:(cmb + 1) * 2 * HEAD_DIM] = acc


def _cmp_p_kernel(r0_ref, r1_ref, r2_ref, r3_ref, w_ref, o_ref):
    rows = (r0_ref, r1_ref, r2_ref, r3_ref)
    _cmp_project(lambda cmb: rows[cmb], w_ref, o_ref)


def _cmp_project_prompt(z, wcat):
    b, t, _ = z.shape
    nch = t // CMP_STRIDE
    rows = lambda cmb: pl.BlockSpec((None, t, HEAD_DIM), lambda bi: (bi, 0, C_NKV // HEAD_DIM + cmb))
    return pl.pallas_call(
        _cmp_p_kernel,
        grid=(b,),
        in_specs=[rows(cmb) for cmb in range(N_CMB)] + [pl.BlockSpec(wcat.shape, lambda bi: (0, 0, 0, 0))],
        out_specs=pl.BlockSpec((None, nch, N_CMB * 2 * HEAD_DIM), lambda bi: (bi, 0, 0)),
        out_shape=jax.ShapeDtypeStruct((b, nch, N_CMB * 2 * HEAD_DIM), F32),
        compiler_params=_cparams("parallel"),
        name="cmp_project_prompt",
    )(z, z, z, z, wcat)


def _cmp_s_kernel(pt_ref, *refs):
    del pt_ref
    per_pos = 2 * KV_HEADS
    page_refs = [r.reshape(PAGE_SIZE * per_pos, HEAD_DIM) for r in refs[:CMP_PAGES]]
    w_ref, o_ref = refs[CMP_PAGES:]
    per_page = PAGE_SIZE // CMP_STRIDE
    for cmb in range(N_CMB):
        acc = jnp.zeros((o_ref.shape[0], 2 * HEAD_DIM), F32)
        for l in range(CMP_STRIDE):
            x = jnp.concatenate(
                [page_refs[p][pl.ds(l * per_pos + cmb, per_page, stride=CMP_STRIDE * per_pos), :]
                 for p in range(CMP_PAGES)], axis=0)
            acc += _dot(x.astype(BF16), w_ref[cmb // KV_HEADS, l])
        o_ref[:, cmb * 2 * HEAD_DIM:(cmb + 1) * 2 * HEAD_DIM] = acc


def _page_spec(layer, slot, p, pages_per_step, grid_rank):
    def index_map(*idx):
        bi, si, pt = idx[0], idx[grid_rank - 1], idx[grid_rank]
        return (layer, pt[bi, si * pages_per_step + p], 0, slot, 0, 0)

    return pl.BlockSpec((None, None, PAGE_SIZE, None, KV_HEADS, HEAD_DIM), index_map)


def _cmp_project_sample(cache6, page_table, layer, wcat):
    b, npages = page_table.shape
    steps = npages // CMP_PAGES
    nch = CMP_PAGES * PAGE_SIZE // CMP_STRIDE
    def page_pair(p):
        return pl.BlockSpec((None, None, PAGE_SIZE, 2, KV_HEADS, HEAD_DIM),
                            lambda bi, si, pt: (layer, pt[bi, si * CMP_PAGES + p], 0, 0, 0, 0))

    pages = [page_pair(p) for p in range(CMP_PAGES)]

    grid_spec = pltpu.PrefetchScalarGridSpec(
        num_scalar_prefetch=1,
        grid=(b, steps),
        in_specs=pages + [pl.BlockSpec(wcat.shape, lambda bi, si, pt: (0, 0, 0, 0))],
        out_specs=pl.BlockSpec((None, nch, N_CMB * 2 * HEAD_DIM), lambda bi, si, pt: (bi, si, 0)),
    )
    return pl.pallas_call(
        _cmp_s_kernel,
        grid_spec=grid_spec,
        out_shape=jax.ShapeDtypeStruct((b, steps * nch, N_CMB * 2 * HEAD_DIM), F32),
        compiler_params=_cparams("parallel", "arbitrary"),
        name="cmp_project_sample",
    )(page_table, *([cache6] * CMP_PAGES), wcat)


def _cmp_fin_kernel(p_ref, b1_ref, w2_ref, o_ref):
    nch = p_ref.shape[0]
    for cmb in range(N_CMB):
        slot = cmb // KV_HEADS
        c0 = cmb * 2 * HEAD_DIM
        first = p_ref[:, c0:c0 + HEAD_DIM]
        second = pltpu.roll(p_ref[:, c0 + HEAD_DIM:c0 + 2 * HEAD_DIM], nch - 1, 0)
        hid = b1_ref[slot:slot + 1, :] + first + second
        o_ref[cmb] = _dot(jax.nn.gelu(hid).astype(BF16), w2_ref[slot])


def _cmp_finish(p, b1, w2):
    b, nch, _ = p.shape
    return pl.pallas_call(
        _cmp_fin_kernel,
        grid=(b,),
        in_specs=[pl.BlockSpec((None, nch, p.shape[2]), lambda bi: (bi, 0, 0)),
                  pl.BlockSpec(b1.shape, lambda bi: (0, 0)),
                  pl.BlockSpec(w2.shape, lambda bi: (0, 0, 0))],
        out_specs=pl.BlockSpec((None, N_CMB, nch, HEAD_DIM), lambda bi: (bi, 0, 0, 0)),
        out_shape=jax.ShapeDtypeStruct((b, N_CMB, nch, HEAD_DIM), F32),
        compiler_params=_cparams("parallel"),
        name="cmp_finish",
    )(p, b1, w2)


def _softmax_init(m_scr, l_scr, acc_scr):
    m_scr[...] = jnp.full(m_scr.shape, M_INIT, F32)
    l_scr[...] = jnp.zeros(l_scr.shape, F32)
    acc_scr[...] = jnp.zeros(acc_scr.shape, F32)


def _softmax_update(s, mask, vb, m_scr, l_scr, acc_scr):
    s = jnp.where(mask, s, NEG_INF)
    m_prev = m_scr[...]
    m_new = jnp.maximum(m_prev, jnp.max(s, axis=1, keepdims=True))
    alpha = jnp.exp(m_prev - m_new)
    p = jnp.exp(s - m_new)
    l_scr[...] = alpha * l_scr[...] + jnp.sum(p, axis=1, keepdims=True)
    acc_scr[...] = alpha * acc_scr[...] + _dot(p.astype(BF16), vb)
    m_scr[...] = m_new


def _softmax_result(l_scr, acc_scr):
    l = l_scr[...]
    return acc_scr[...] / jnp.where(l > 0, l, 1.0)


def _masked_probs(s, mask):
    s = jnp.where(mask, s, NEG_INF)
    m = jnp.max(s, axis=1, keepdims=True)
    m = jnp.where(m > NEG_INF, m, 0.0)
    p = jnp.exp(s - m)
    den = jnp.sum(p, axis=1, keepdims=True)
    return p / jnp.where(den > 0, den, 1.0)


def _stack_heads(q):
    return jnp.concatenate([q[:, j * HEAD_DIM:(j + 1) * HEAD_DIM] for j in range(GROUP)], axis=0)


def _block_importance(pc, n_cmp, n_sel):
    ncp = pc.shape[1]
    nsp = -(-n_sel // LANES) * LANES
    c_start = lax.broadcasted_iota(jnp.int32, (ncp, nsp), 0) * CMP_STRIDE
    s_start = lax.broadcasted_iota(jnp.int32, (ncp, nsp), 1) * SEL_BLOCK
    overlap = ((c_start < s_start + SEL_BLOCK) & (c_start + CMP_BLOCK > s_start)
               & (c_start < n_cmp * CMP_STRIDE)).astype(F32)
    return jnp.dot(pc, overlap, preferred_element_type=F32, precision=lax.Precision.HIGHEST)


def _force_blocks(imp, blk, cur):
    forced = (blk == 0) | (blk == cur) | (blk == cur - 1)
    imp = jnp.where(forced, FORCE_SCORE, imp)
    return jnp.where(blk > cur, -1.0, imp)


def _select_blocks_cols(pc, cur, n_cmp, n_sel):
    imp = _block_importance(pc, n_cmp, n_sel)
    blk = lax.broadcasted_iota(jnp.int32, (1, imp.shape[1]), 1)
    imp = _force_blocks(imp, blk, cur)

    def body(sp, count):
        col = jnp.sum(jnp.where(blk == sp, imp, 0.0), axis=1, keepdims=True)
        ahead = (col > imp) | ((col == imp) & (sp < blk))
        return count + ahead.astype(F32)

    count = lax.fori_loop(0, n_sel, body, jnp.zeros(imp.shape, F32), unroll=64)
    return ((count < min(SEL_TOPK, n_sel)) & (blk < n_sel)).astype(F32)


def _select_blocks_rows(pc, cur_row, n_cmp, n_sel):
    rows = pc.shape[0]
    nrow = -(-n_sel // SUBLANES) * SUBLANES
    imp_t = _block_importance(pc, n_cmp, n_sel).T[0:nrow, :]
    blk = lax.broadcasted_iota(jnp.int32, (nrow, 1), 0)
    imp_t = _force_blocks(imp_t, blk, cur_row)
    count = jnp.zeros(imp_t.shape, F32)
    for sp in range(n_sel):
        row = imp_t[sp:sp + 1, :]
        count += ((row > imp_t) | ((row == imp_t) & (sp < blk))).astype(F32)
    sel_t = ((count < min(SEL_TOPK, n_sel)) & (blk < n_sel)).astype(F32)
    sel_t = jnp.concatenate([sel_t, jnp.zeros((LANES - nrow, rows), F32)], axis=0)
    return sel_t.T


def _gate(gt, col):
    lane = lax.broadcasted_iota(jnp.int32, (1, LANES), 1)
    return jax.nn.sigmoid(jnp.sum(jnp.where(lane == col, gt, 0.0), axis=1, keepdims=True))


def _nsa_prompt_kernel(q_ref, ks_ref, vs_ref, kw_ref, vw_ref, kc_ref, vc_ref, bc_ref, tz_ref, gt_ref, gb_ref,
                       o_ref, s_scr, selb_scr, mx_scr, l_scr, acc_scr, *, t_len):
    g = pl.program_id(1)
    qt = pl.program_id(2)
    q0 = qt * TQ
    n_cmp = t_len // CMP_STRIDE - (CMP_BLOCK // CMP_STRIDE) + 1
    n_sel = t_len // SEL_BLOCK
    ncp = kc_ref.shape[0]
    rows4 = GROUP * TQ
    qs = _stack_heads(q_ref[...] * QK_SCALE).astype(BF16)
    qpos = q0 + (lax.broadcasted_iota(jnp.int32, (rows4, 1), 0) & (TQ - 1))

    n_idx = lax.broadcasted_iota(jnp.int32, (1, ncp), 1)
    s_c = _dot_nt(qs, kc_ref[...].astype(BF16)) + bc_ref[...].reshape(rows4, ncp)
    cmask = (qpos - (n_idx * CMP_STRIDE + CMP_BLOCK - 1) >= 0) & (n_idx < n_cmp)
    p_c = _masked_probs(s_c, cmask)
    o_c = _dot(p_c.astype(BF16), vc_ref[...].astype(BF16))
    pc = p_c[0:TQ] + p_c[TQ:2 * TQ] + p_c[2 * TQ:3 * TQ] + p_c[3 * TQ:4 * TQ]
    cur_row = jnp.right_shift(q0 + lax.broadcasted_iota(jnp.int32, (1, TQ), 1), SEL_SHIFT)
    sel = _select_blocks_rows(pc, cur_row, n_cmp, n_sel)
    sel_add = ((sel - 1.0) * -MASKED).astype(BF16)

    kt_hi = (q0 + TQ - 1) // TK + 1
    kt_far = jnp.maximum(q0 - (FAR_DIST - 1), 0) // TK
    kidx = lax.broadcasted_iota(jnp.int32, (1, TK), 1)
    srow = lax.broadcasted_iota(jnp.int32, (LANES, TK), 0)

    def spread_selection(kt, carry):
        k0 = kt * TK
        expand = (jnp.right_shift(k0 + lax.broadcasted_iota(jnp.int32, (LANES, TK), 1), SEL_SHIFT) == srow).astype(BF16)
        selb_scr[kt] = _dot(sel_add, expand)
        return carry

    lax.fori_loop(0, kt_hi, spread_selection, 0)

    def tile_bias(k0):
        return tz_ref[jnp.clip((q0 - k0) // TQ, 0, N_TZ - 1)].reshape(rows4, TK)

    def attend(k_ref, v_ref, kt_lo, mask_tile):
        mx_scr[...] = jnp.full(mx_scr.shape, NEG_INF, F32)

        def scores(near):
            def body(kt, carry):
                k0 = pl.multiple_of(kt * TK, TK)
                s = _dot_nt(qs, k_ref[pl.ds(k0, TK), :].astype(BF16))
                if near:
                    s = s + tile_bias(k0)
                s = mask_tile(k0, s, near)
                s_scr[kt] = s
                mx_scr[...] = jnp.maximum(mx_scr[...], jnp.maximum(s[:, 0:LANES], s[:, LANES:TK]))
                return carry
            return body

        kt_mid = jnp.maximum(kt_far, kt_lo)
        lax.fori_loop(kt_lo, kt_mid, scores(False), 0)
        lax.fori_loop(kt_mid, kt_hi, scores(True), 0)
        m = jnp.max(mx_scr[...], axis=1, keepdims=True)
        mx_scr[...] = jnp.broadcast_to(jnp.where(m > NEG_INF, m, 0.0), mx_scr.shape)
        l_scr[...] = jnp.zeros(l_scr.shape, F32)
        acc_scr[...] = jnp.zeros(acc_scr.shape, F32)

        def probs(kt, carry):
            k0 = pl.multiple_of(kt * TK, TK)
            m_rep = mx_scr[...]
            p = jnp.exp(s_scr[kt] - jnp.concatenate([m_rep, m_rep], axis=1))
            l_scr[...] += p[:, 0:LANES] + p[:, LANES:TK]
            acc_scr[...] += _dot(p.astype(BF16), v_ref[pl.ds(k0, TK), :].astype(BF16))
            return carry

        lax.fori_loop(kt_lo, kt_hi, probs, 0)
        l = jnp.sum(l_scr[...], axis=1, keepdims=True)
        return acc_scr[...] / jnp.where(l > 0, l, 1.0)

    def sel_mask(k0, s, near):
        s = (s.reshape(GROUP, TQ, TK) + selb_scr[k0 // TK][None]).reshape(rows4, TK)
        return jnp.where(k0 + kidx <= qpos, s, NEG_INF) if near else s

    o_s = attend(ks_ref, vs_ref, 0, sel_mask)

    def win_mask(k0, s, near):
        dist = qpos - (k0 + kidx)
        return jnp.where((dist >= 0) & (dist <= WINDOW), s, NEG_INF)

    o_w = attend(kw_ref, vw_ref, jnp.maximum(q0 - WINDOW, 0) // TK, win_mask)

    gt = gt_ref[...] + gb_ref[...]
    for j in range(GROUP):
        col = GATE_NG + (g * GROUP + j) * 3
        r = slice(j * TQ, (j + 1) * TQ)
        o_ref[:, j * HEAD_DIM:(j + 1) * HEAD_DIM] = (
            _gate(gt, col) * o_c[r] + _gate(gt, col + 1) * o_s[r] + _gate(gt, col + 2) * o_w[r])


def _nsa_prompt(z, kcvc, bias_c, tz, gate_b):
    b, t, _ = z.shape
    ncp = kcvc.shape[2]
    nq = t // TQ
    kv = lambda slot: pl.BlockSpec((None, t, HEAD_DIM), lambda bi, g, qi: (bi, 0, C_NKV // HEAD_DIM + slot * KV_HEADS + g))
    cmp_blk = lambda slot: pl.BlockSpec((None, None, ncp, HEAD_DIM), lambda bi, g, qi: (bi, slot * KV_HEADS + g, 0, 0))
    return pl.pallas_call(
        functools.partial(_nsa_prompt_kernel, t_len=t),
        grid=(b, KV_HEADS, nq),
        in_specs=[pl.BlockSpec((None, TQ, GROUP * HEAD_DIM), lambda bi, g, qi: (bi, qi, C_NQ // (GROUP * HEAD_DIM) + g)),
                  kv(2), kv(3), kv(4), kv(5), cmp_blk(0), cmp_blk(1),
                  pl.BlockSpec((GROUP, TQ, ncp), lambda bi, g, qi: (g, qi, 0)),
                  pl.BlockSpec((N_TZ, GROUP, TQ, TK), lambda bi, g, qi: (0, g, 0, 0)),
                  pl.BlockSpec((None, TQ, LANES), lambda bi, g, qi: (bi, qi, C_GATE // LANES)),
                  pl.BlockSpec((1, LANES), lambda bi, g, qi: (0, 0))],
        out_specs=pl.BlockSpec((None, TQ, GROUP * HEAD_DIM), lambda bi, g, qi: (bi, qi, g)),
        out_shape=jax.ShapeDtypeStruct((b, t, D_NSA), F32),
        scratch_shapes=[pltpu.VMEM((t // TK, GROUP * TQ, TK), F32), pltpu.VMEM((t // TK, TQ, TK), F32),
                        pltpu.VMEM((GROUP * TQ, LANES), F32),
                        pltpu.VMEM((GROUP * TQ, LANES), F32), pltpu.VMEM((GROUP * TQ, HEAD_DIM), F32)],
        compiler_params=_cparams("parallel", "parallel", "arbitrary"),
        name="nsa_prompt",
    )(z, z, z, z, z, kcvc, kcvc, bias_c, tz, z, gate_b)


TS = SUBLANES
ROWS_S = GROUP * TS


def _nsa_s_cmp_kernel(q_ref, kc_ref, vc_ref, bc_ref, oc_ref, sel_ref, *, pos0, n_cmp, n_sel):
    nch = kc_ref.shape[0]
    qs = _stack_heads(q_ref[...] * QK_SCALE).astype(BF16)
    qpos = pos0 + (lax.broadcasted_iota(jnp.int32, (ROWS_S, 1), 0) & (TS - 1))
    n_idx = lax.broadcasted_iota(jnp.int32, (1, nch), 1)
    s_c = _dot_nt(qs, kc_ref[...].astype(BF16)) + bc_ref[...].reshape(ROWS_S, nch)
    cmask = (qpos - (n_idx * CMP_STRIDE + CMP_BLOCK - 1) >= 0) & (n_idx < n_cmp)
    p_c = _masked_probs(s_c, cmask)
    oc_ref[...] = _dot(p_c.astype(BF16), vc_ref[...].astype(BF16))
    pc = p_c[0:TS] + p_c[TS:2 * TS] + p_c[2 * TS:3 * TS] + p_c[3 * TS:4 * TS]
    sel_ref[...] = _select_blocks_cols(pc, jnp.right_shift(qpos[0:TS], SEL_SHIFT), n_cmp, n_sel)


def _nsa_sample_cmp(z8, kcvc, bias_c, *, pos0, n_cmp, n_sel):
    b = z8.shape[0]
    nch = kcvc.shape[2]
    nsp = -(-n_sel // LANES) * LANES
    cmp_blk = lambda slot: pl.BlockSpec((None, None, nch, HEAD_DIM), lambda bi, g: (bi, slot * KV_HEADS + g, 0, 0))
    return pl.pallas_call(
        functools.partial(_nsa_s_cmp_kernel, pos0=pos0, n_cmp=n_cmp, n_sel=n_sel),
        grid=(b, KV_HEADS),
        in_specs=[pl.BlockSpec((None, TS, GROUP * HEAD_DIM), lambda bi, g: (bi, 0, C_NQ // (GROUP * HEAD_DIM) + g)),
                  cmp_blk(0), cmp_blk(1),
                  pl.BlockSpec((GROUP, TS, nch), lambda bi, g: (g, 0, 0))],
        out_specs=[pl.BlockSpec((None, None, ROWS_S, HEAD_DIM), lambda bi, g: (bi, g, 0, 0)),
                   pl.BlockSpec((None, None, TS, nsp), lambda bi, g: (bi, g, 0, 0))],
        out_shape=[jax.ShapeDtypeStruct((b, KV_HEADS, ROWS_S, HEAD_DIM), F32),
                   jax.ShapeDtypeStruct((b, KV_HEADS, TS, nsp), F32)],
        compiler_params=_cparams("parallel", "parallel"),
        name="nsa_sample_cmp",
    )(z8, kcvc, kcvc, bias_c)


def _pad_rows(x, n):
    return jnp.concatenate([x, jnp.zeros((n - x.shape[0], x.shape[1]), x.dtype)], axis=0)


def _nsa_s_sel_kernel(pt_ref, q_ref, sel_ref, kn_ref, vn_ref, bt_ref, bn_ref, bf_ref, *refs, pos0, npages):
    del pt_ref
    rows_pp = PAGE_SIZE * KV_HEADS
    k_refs = [r.reshape(rows_pp, HEAD_DIM) for r in refs[:SEL_PAGES]]
    v_refs = [r.reshape(rows_pp, HEAD_DIM) for r in refs[SEL_PAGES:2 * SEL_PAGES]]
    o_ref, kbuf, vbuf, m_scr, l_scr, acc_scr = refs[2 * SEL_PAGES:]
    step = pl.program_id(1)
    last = step == npages // SEL_PAGES - 1
    rows = KV_HEADS * ROWS_S
    ncol = SEL_TILE * rows_pp
    head_shift = KV_HEADS.bit_length() - 1
    blk_per_page = PAGE_SIZE // SEL_BLOCK

    @pl.when(step == 0)
    def _():
        _softmax_init(m_scr, l_scr, acc_scr)

    nsp = sel_ref.shape[-1]
    ridx = lax.broadcasted_iota(jnp.int32, (rows, 1), 0)
    qpos = pos0 + (ridx & (TS - 1))
    qs = jnp.concatenate([_stack_heads(q_ref[:, g * GROUP * HEAD_DIM:(g + 1) * GROUP * HEAD_DIM] * QK_SCALE)
                          for g in range(KV_HEADS)], axis=0).astype(BF16)
    sel_rows = jnp.concatenate([sel_ref[g] for g in range(KV_HEADS) for _ in range(GROUP)], axis=0)
    bias_far = bf_ref[...].reshape(rows, PAGE_SIZE)[:, 0:1]
    col = lax.broadcasted_iota(jnp.int32, (1, ncol), 1)
    own_head = (col & (KV_HEADS - 1)) == ridx // ROWS_S
    expand = (jnp.right_shift(lax.broadcasted_iota(jnp.int32, (LANES, ncol), 1), SEL_SHIFT + head_shift)
              == lax.broadcasted_iota(jnp.int32, (LANES, ncol), 0)).astype(BF16)
    blk_r = lax.broadcasted_iota(jnp.int32, (nsp, LANES), 0)
    blk_c = lax.broadcasted_iota(jnp.int32, (nsp, LANES), 1)
    n_tiles = SEL_PAGES // SEL_TILE

    for tile in range(n_tiles):
        for i in range(SEL_TILE):
            p = tile * SEL_TILE + i
            kbuf[i * rows_pp:(i + 1) * rows_pp, :] = k_refs[p][...].astype(BF16)
            vbuf[i * rows_pp:(i + 1) * rows_pp, :] = v_refs[p][...].astype(BF16)
        page0 = step * SEL_PAGES + tile * SEL_TILE
        window = ((blk_r == page0 * blk_per_page + blk_c) & (blk_c < SEL_TILE * blk_per_page)).astype(BF16)
        chosen = _dot(_dot(sel_rows.astype(BF16), window).astype(BF16), expand) > 0.5
        s = _dot_nt(qs, kbuf[...]) + bias_far
        if tile == n_tiles - 1:
            tail = jnp.where(last, bt_ref[...].reshape(rows, rows_pp) - bias_far, 0.0)
            s = s + jnp.concatenate([jnp.zeros((rows, ncol - rows_pp), F32), tail], axis=1)
        key = page0 * PAGE_SIZE + jnp.right_shift(col, head_shift)
        _softmax_update(s, chosen & own_head & (key <= qpos), vbuf[...], m_scr, l_scr, acc_scr)

    @pl.when(last)
    def _():
        kidx = lax.broadcasted_iota(jnp.int32, (1, PAGE_SIZE), 1)
        new_blk = lax.broadcasted_iota(jnp.int32, (1, nsp), 1) == npages * blk_per_page
        flag = jnp.sum(jnp.where(new_blk, sel_rows, 0.0), axis=1, keepdims=True) > 0.5
        bias_new = bn_ref[...].reshape(rows, PAGE_SIZE)
        for g in range(KV_HEADS):
            r = slice(g * ROWS_S, (g + 1) * ROWS_S)
            kn = _pad_rows(kn_ref[:, g * HEAD_DIM:(g + 1) * HEAD_DIM], PAGE_SIZE).astype(BF16)
            vn = _pad_rows(vn_ref[:, g * HEAD_DIM:(g + 1) * HEAD_DIM], PAGE_SIZE).astype(BF16)
            mask = flag[r] & (pos0 + kidx <= qpos[r]) & (kidx < TS)
            _softmax_update(_dot_nt(qs[r], kn) + bias_new[r], mask, vn, m_scr.at[r], l_scr.at[r], acc_scr.at[r])
            o_ref[g] = _softmax_result(l_scr.at[r], acc_scr.at[r])


def _nsa_sample_sel(z8, sel, bias_tail, bias_new, bias_far, cache6, page_table, layer, *, pos0):
    b, npages = page_table.shape
    nsp = sel.shape[-1]
    steps = npages // SEL_PAGES
    slot_cols = KV_HEADS * HEAD_DIM
    new_rows = lambda slot: pl.BlockSpec((None, TS, slot_cols),
                                         lambda bi, si, pt: (bi, 0, (C_NKV + slot * slot_cols) // slot_cols))
    whole = lambda a: pl.BlockSpec(a.shape, lambda bi, si, pt: (0,) * a.ndim)
    grid_spec = pltpu.PrefetchScalarGridSpec(
        num_scalar_prefetch=1,
        grid=(b, steps),
        in_specs=[pl.BlockSpec((None, TS, D_NSA), lambda bi, si, pt: (bi, 0, C_NQ // D_NSA)),
                  pl.BlockSpec((None, KV_HEADS, TS, nsp), lambda bi, si, pt: (bi, 0, 0, 0)),
                  new_rows(2), new_rows(3), whole(bias_tail), whole(bias_new), whole(bias_far)]
                 + [_page_spec(layer, 2, p, SEL_PAGES, 2) for p in range(SEL_PAGES)]
                 + [_page_spec(layer, 3, p, SEL_PAGES, 2) for p in range(SEL_PAGES)],
        out_specs=pl.BlockSpec((None, KV_HEADS, ROWS_S, HEAD_DIM), lambda bi, si, pt: (bi, 0, 0, 0)),
        scratch_shapes=[pltpu.VMEM((SEL_TILE * PAGE_SIZE * KV_HEADS, HEAD_DIM), BF16),
                        pltpu.VMEM((SEL_TILE * PAGE_SIZE * KV_HEADS, HEAD_DIM), BF16),
                        pltpu.VMEM((KV_HEADS * ROWS_S, 1), F32), pltpu.VMEM((KV_HEADS * ROWS_S, 1), F32),
                        pltpu.VMEM((KV_HEADS * ROWS_S, HEAD_DIM), F32)],
    )
    return pl.pallas_call(
        functools.partial(_nsa_s_sel_kernel, pos0=pos0, npages=npages),
        grid_spec=grid_spec,
        out_shape=jax.ShapeDtypeStruct((b, KV_HEADS, ROWS_S, HEAD_DIM), F32),
        compiler_params=_cparams("parallel", "arbitrary"),
        name="nsa_sample_sel",
    )(page_table, z8, sel, z8, z8, bias_tail, bias_new, bias_far, *([cache6] * (2 * SEL_PAGES)))


def _nsa_s_win_kernel(q_ref, kw_ref, vw_ref, kn_ref, vn_ref, bw_ref, oc_ref, os_ref, gt_ref, gb_ref, o_ref,
                      m_scr, l_scr, acc_scr, *, pos0):
    g = pl.program_id(1)
    wb = kw_ref.shape[0]
    own = pl.ds(g, wb, stride=KV_HEADS)
    kw = kw_ref.reshape(wb * KV_HEADS, HEAD_DIM)[own, :]
    vw = vw_ref.reshape(wb * KV_HEADS, HEAD_DIM)[own, :]
    qs = _stack_heads(q_ref[...] * QK_SCALE).astype(BF16)
    qpos = pos0 + (lax.broadcasted_iota(jnp.int32, (ROWS_S, 1), 0) & (TS - 1))
    bias = bw_ref[...].reshape(ROWS_S, wb + PAGE_SIZE)
    _softmax_init(m_scr, l_scr, acc_scr)
    dist = qpos - (pos0 - wb + lax.broadcasted_iota(jnp.int32, (1, wb), 1))
    s = _dot_nt(qs, kw.astype(BF16)) + bias[:, 0:wb]
    _softmax_update(s, (dist >= 0) & (dist <= WINDOW), vw.astype(BF16), m_scr, l_scr, acc_scr)
    kidx = lax.broadcasted_iota(jnp.int32, (1, PAGE_SIZE), 1)
    dist = qpos - (pos0 + kidx)
    s = _dot_nt(qs, _pad_rows(kn_ref[...], PAGE_SIZE).astype(BF16)) + bias[:, wb:wb + PAGE_SIZE]
    _softmax_update(s, (dist >= 0) & (dist <= WINDOW) & (kidx < TS), _pad_rows(vn_ref[...], PAGE_SIZE).astype(BF16),
                    m_scr, l_scr, acc_scr)
    o_w = _softmax_result(l_scr, acc_scr)
    o_c = oc_ref[...]
    o_s = os_ref[...]
    gt = gt_ref[...] + gb_ref[...]
    for j in range(GROUP):
        col = GATE_NG + (g * GROUP + j) * 3
        r = slice(j * TS, (j + 1) * TS)
        o_ref[:, j * HEAD_DIM:(j + 1) * HEAD_DIM] = (
            _gate(gt, col) * o_c[r] + _gate(gt, col + 1) * o_s[r] + _gate(gt, col + 2) * o_w[r])


def _nsa_sample_win(z8, win6, layer, bias_w, o_c, o_s, gate_b, *, pos0):
    _, b, wb = win6.shape[:3]
    cached = lambda slot: pl.BlockSpec((None, None, wb, None, KV_HEADS, HEAD_DIM),
                                       lambda bi, g: (layer, bi, 0, slot, 0, 0))
    zcol = lambda col: (lambda bi, g: (bi, 0, col // HEAD_DIM + g))
    part = pl.BlockSpec((None, None, ROWS_S, HEAD_DIM), lambda bi, g: (bi, g, 0, 0))
    return pl.pallas_call(
        functools.partial(_nsa_s_win_kernel, pos0=pos0),
        grid=(b, KV_HEADS),
        in_specs=[pl.BlockSpec((None, TS, GROUP * HEAD_DIM), lambda bi, g: (bi, 0, C_NQ // (GROUP * HEAD_DIM) + g)),
                  cached(0), cached(1),
                  pl.BlockSpec((None, TS, HEAD_DIM), zcol(C_NKV + 4 * KV_HEADS * HEAD_DIM)),
                  pl.BlockSpec((None, TS, HEAD_DIM), zcol(C_NKV + 5 * KV_HEADS * HEAD_DIM)),
                  pl.BlockSpec((GROUP, TS, wb + PAGE_SIZE), lambda bi, g: (g, 0, 0)),
                  part, part,
                  pl.BlockSpec((None, TS, LANES), lambda bi, g: (bi, 0, C_GATE // LANES)),
                  pl.BlockSpec((1, LANES), lambda bi, g: (0, 0))],
        out_specs=pl.BlockSpec((None, TS, GROUP * HEAD_DIM), lambda bi, g: (bi, 0, g)),
        out_shape=jax.ShapeDtypeStruct((b, TS, D_NSA), F32),
        scratch_shapes=[pltpu.VMEM((ROWS_S, 1), F32), pltpu.VMEM((ROWS_S, 1), F32), pltpu.VMEM((ROWS_S, HEAD_DIM), F32)],
        compiler_params=_cparams("parallel", "parallel"),
        name="nsa_sample_win",
    )(z8, win6, win6, z8, z8, bias_w, o_c, o_s, z8, gate_b)


def _t5_bucket(dist):
    n = np.maximum(dist, 0)
    exact = NUM_BUCKETS // 2
    nf = np.maximum(n, 1).astype(np.float32)
    large = exact + (np.log(nf / np.float32(exact)) / np.float32(math.log(MAX_DISTANCE / exact))
                     * np.float32(NUM_BUCKETS - exact)).astype(np.int32)
    return np.where(n < exact, n, np.minimum(large, NUM_BUCKETS - 1)).astype(np.int32)


FAR_DIST = int(np.max(np.nonzero(_t5_bucket(np.arange(4 * MAX_DISTANCE)) < NUM_BUCKETS - 1)[0])) + 1
N_TZ = (FAR_DIST + TK - 2) // TQ + 1


def _bias_table(rel_bias, dist):
    onehot = jax.nn.one_hot(jnp.asarray(_t5_bucket(dist).astype(np.int8)), NUM_BUCKETS, dtype=F32)
    return jnp.einsum("rcb,bh->hrc", onehot, rel_bias, precision=lax.Precision.HIGHEST)


def _pack_weights(p):
    w_in = p["w_in"].astype(BF16)
    pad = jnp.zeros(w_in.shape[:2] + (N_IN - C_GATE - 32,), BF16)
    w_in_p = jnp.concatenate([w_in[..., 0:2048], w_in[..., 2056:3080], w_in[..., 3080:4616], w_in[..., 4640:5152],
                              w_in[..., 2048:2056], w_in[..., 4616:4640], pad], axis=2)
    return dict(
        w_in=w_in_p, w_out=p["w_out"].astype(BF16), w_down=p["ffn_w_down"].astype(BF16),
        w_up=jnp.pad(p["ffn_w_up"].astype(BF16), ((0, 0), (0, 0), (0, D_FFP - D_FF))))


def _pack_layer(p, big, l):
    gb = p["mlstm_gate_b"][l]
    mlstm_gate_b = jnp.zeros((1, LANES), F32).at[0, GATE_MI:GATE_MI + 4].set(gb[0]).at[0, GATE_MF:GATE_MF + 4].set(gb[1])
    nsa_gate_b = jnp.zeros((1, LANES), F32).at[0, GATE_NG:GATE_NG + 3 * NSA_HEADS].set(p["nsa_gate_b"][l].reshape(-1))
    w1 = p["cmp_w1"][l]
    wcat = jnp.concatenate([w1[:, 0:CMP_STRIDE], w1[:, CMP_STRIDE:CMP_BLOCK]], axis=-1).astype(BF16)
    lam = lax.complex(p["s5_a_re"][l], p["s5_a_im"][l])
    lam_bar = jnp.exp(lam * jnp.exp(p["s5_log_step"][l])[:, None])
    b_bar = ((lam_bar - 1.0) / lam)[..., None] * lax.complex(p["s5_b_re"][l], p["s5_b_im"][l])
    gi = S5_GROUPS // S5_IN_TILES
    bd_in = lambda m: jnp.einsum("cgph,gk->cghkp", m.reshape(S5_IN_TILES, gi, S5_STATE, S5_GROUP_WIDTH),
                                 jnp.eye(gi, dtype=F32)).reshape(S5_IN_TILES, S5_IN_FEATS, 256)
    wb = jnp.concatenate([bd_in(b_bar.real), bd_in(b_bar.imag)], axis=2).astype(BF16)
    go = S5_GROUPS // S5_OUT_TILES
    bd_out = lambda m: jnp.einsum("jghp,gk->jgpkh", m.reshape(S5_OUT_TILES, go, S5_GROUP_WIDTH, S5_STATE),
                                  jnp.eye(go, dtype=F32)).reshape(S5_OUT_TILES, S5_OUT_CH, 256)
    wc = jnp.stack([bd_out(p["s5_c_re"][l]), -bd_out(p["s5_c_im"][l])]).astype(BF16)
    half = lambda w: jnp.pad(w, ((0, 0), (0, D_FFP - D_FF)))
    conv_w = p["ffn_conv_w"][l]
    conv_b = p["ffn_conv_b"][l][None, :]
    return dict(
        big, layer=l, mlstm_gate_b=mlstm_gate_b, mlstm_norm_w=p["mlstm_norm_w"][l][None, :], nsa_gate_b=nsa_gate_b,
        wcat=wcat, cmp_b1=p["cmp_b1"][l], cmp_w2=p["cmp_w2"][l].astype(BF16),
        s5_wb=wb, s5_wc=wc, s5_lam_r=lam_bar.real.reshape(1, S5_CH), s5_lam_i=lam_bar.imag.reshape(1, S5_CH),
        s5_d=p["s5_d"][l][None, :], s5_glu_w=p["s5_glu_w"][l].astype(BF16), s5_glu_b=p["s5_glu_b"][l][None, :],
        ln1_w=p["ln1_w"][l][None, :], ln1_b=p["ln1_b"][l][None, :],
        conv_w8=jnp.pad(jnp.concatenate([half(conv_w[:, :D_FF]), half(conv_w[:, D_FF:])], axis=1),
                        ((0, SUBLANES - CONV_W), (0, 0))),
        conv_b=jnp.concatenate([half(conv_b[:, :D_FF]), half(conv_b[:, D_FF:])], axis=1),
        ln2_w=p["ln2_w"][l][None, :], ln2_b=p["ln2_b"][l][None, :],
    )


def _unpad_ff(x):
    return jnp.concatenate([x[..., :D_FF], x[..., D_FFP:D_FFP + D_FF]], axis=-1)


def _pad_ff(x):
    pad = [(0, 0)] * (x.ndim - 1) + [(0, D_FFP - D_FF)]
    return jnp.concatenate([jnp.pad(x[..., :D_FF], pad), jnp.pad(x[..., D_FF:], pad)], axis=-1)


def _mixer_tail(x2d, z, b, t, t_use, lw, o_mlstm, o_nsa, s5_state, conv_state8, tm):
    pad8 = lambda s: jnp.pad(s.reshape(b, S5_CH), ((0, SUBLANES - b), (0, 0)))
    o_s5, xr, xi = _s5(z.reshape(b, t, N_IN), lw["s5_wb"], lw["s5_wc"], lw["s5_lam_r"], lw["s5_lam_i"], lw["s5_d"],
                       lw["s5_glu_w"], lw["s5_glu_b"], pad8(s5_state[0]), pad8(s5_state[1]), lc=S5_LC, steps=t_use)
    o_s5 = o_s5.reshape(b * t, D_S5)
    layer = lw["layer"]
    x1 = _wout_ln(o_mlstm.reshape(b * t, D_MLSTM), o_nsa.reshape(b * t, D_NSA), o_s5, x2d,
                  lw["w_out"], layer, lw["ln1_w"], lw["ln1_b"], tm=256)
    s5_new = (xr[:b].reshape(b, S5_GROUPS, S5_STATE), xi[:b].reshape(b, S5_GROUPS, S5_STATE))
    if conv_state8 is None:
        x2, tail_a, tail_g = _ffn_prompt(x1.reshape(b, t, D_MODEL), lw["w_up"], lw["w_down"], layer, lw["conv_w8"],
                                         lw["conv_b"], lw["ln2_w"], lw["ln2_b"], tt=FFN_TT, tf=FFN_TF)
        keep = slice(SUBLANES - (CONV_W - 1), SUBLANES)
        conv_new = jnp.concatenate([tail_a[:, keep, :D_FF], tail_g[:, keep, :D_FF]], axis=-1)
        return x2.reshape(b * t, D_MODEL), s5_new, conv_new
    up = _matmul(x1, lw["w_up"], layer, tm=tm, tn=FFN_TF, n=2 * D_FFP, w_cols=_up_cols(FFN_TF))
    hgate = _convgate(up.reshape(b, t, 2 * D_FFP), conv_state8, lw["conv_w8"], lw["conv_b"], tt=256, tf=512)
    x2 = _down_ln(hgate.reshape(b * t, D_FFP), lw["w_down"], layer, x1, lw["ln2_w"], lw["ln2_b"], tm=512, tk=1408)
    conv_new = _unpad_ff(up.reshape(b, t, 2 * D_FFP)[:, t_use - (CONV_W - 1):t_use])
    return x2, s5_new, conv_new


def _prompt_layer(x2d, b, t, lw, rel_bias):
    z, kv_rows, win_rows = _proj_in(x2d, lw["w_in"], lw["layer"], tm=min(1024, b * t), tn=768)
    z3 = z.reshape(b, t, N_IN)
    zeros = lambda *s: jnp.zeros(s, F32)
    o_mlstm, c1, n1, m1 = _mlstm(z3, lw["mlstm_gate_b"], lw["mlstm_norm_w"],
                                 zeros(b, MLSTM_HEADS, HEAD_DIM, HEAD_DIM), zeros(b, MLSTM_HEADS, LANES),
                                 zeros(b, MLSTM_HEADS, LANES), L=MLSTM_L, lin=MLSTM_L, t_valid=t)
    kcvc = _cmp_finish(_cmp_project_prompt(z3, lw["wcat"]), lw["cmp_b1"], lw["cmp_w2"])
    ncp = t // CMP_STRIDE
    bias_c = _bias_table(rel_bias, np.arange(t)[:, None] - (np.arange(ncp) * CMP_STRIDE + CMP_BLOCK - 1)[None, :])
    ti = np.arange(TQ)[:, None] - np.arange(TK)[None, :]
    far = _bias_table(rel_bias, np.full((1, 1), FAR_DIST))
    tz = jnp.stack([_bias_table(rel_bias, d * TQ + ti) - far for d in range(N_TZ)])
    o_nsa = _nsa_prompt(z3, kcvc, bias_c, tz, lw["nsa_gate_b"])
    x2, s5_new, conv_new = _mixer_tail(
        x2d, z, b, t, t, lw, o_mlstm, o_nsa, (zeros(b, S5_GROUPS, S5_STATE), zeros(b, S5_GROUPS, S5_STATE)),
        None, tm=1024)
    wrows = min(WINDOW, t)
    state = (kv_rows.reshape(b, t, 4, KV_HEADS, HEAD_DIM), win_rows.reshape(b, t, 2, KV_HEADS, HEAD_DIM)[:, t - wrows:],
             c1, n1, m1[:, :, 0], s5_new[0], s5_new[1], conv_new)
    return x2, state


def _sample_layer(x2d, b, tn, lw, rel_bias, layer, cache6, page_table, win6, mlstm_state, s5_state, conv_state):
    npages = page_table.shape[1]
    pos0 = npages * PAGE_SIZE
    z = _matmul(x2d, lw["w_in"], lw["layer"], tm=1024, tn=768)
    z8 = z.reshape(b, TS, N_IN)
    c0, n0, m0 = mlstm_state
    o_mlstm, c1, n1, m1 = _mlstm(z8, lw["mlstm_gate_b"], lw["mlstm_norm_w"], c0, n0,
                                 jnp.broadcast_to(m0[:, :, None], (b, MLSTM_HEADS, LANES)),
                                 L=LANES, lin=TS, t_valid=tn)
    n_chunks = (pos0 + tn) // CMP_STRIDE
    n_cmp = n_chunks - CMP_BLOCK // CMP_STRIDE + 1
    n_sel = -(-(pos0 + tn) // SEL_BLOCK)
    kcvc = _cmp_finish(_cmp_project_sample(cache6, page_table, layer, lw["wcat"]), lw["cmp_b1"], lw["cmp_w2"])
    qpos = pos0 + np.arange(TS)[:, None]
    bias_c = _bias_table(rel_bias, qpos - (np.arange(n_chunks) * CMP_STRIDE + CMP_BLOCK - 1)[None, :])
    o_c, sel = _nsa_sample_cmp(z8, kcvc, bias_c, pos0=pos0, n_cmp=n_cmp, n_sel=n_sel)
    kk = np.arange(PAGE_SIZE)[None, :]
    bias_tail = _bias_table(rel_bias, np.repeat(qpos - (pos0 - PAGE_SIZE + kk), KV_HEADS, axis=1))
    bias_new = _bias_table(rel_bias, qpos - (pos0 + kk))
    bias_far = _bias_table(rel_bias, np.broadcast_to(qpos - (pos0 - 2 * PAGE_SIZE), (TS, PAGE_SIZE)))
    o_s = _nsa_sample_sel(z8, sel, bias_tail, bias_new, bias_far, cache6, page_table, layer, pos0=pos0)
    wb = win6.shape[2]
    wk = np.arange(wb)[None, :]
    bias_w = _bias_table(rel_bias, np.concatenate([qpos - (pos0 - wb + wk), qpos - (pos0 + kk)], axis=1))
    o_nsa = _nsa_sample_win(z8, win6, layer, bias_w, o_c, o_s, lw["nsa_gate_b"], pos0=pos0)
    conv_state8 = jnp.pad(_pad_ff(conv_state), ((0, 0), (SUBLANES - (CONV_W - 1), 0), (0, 0)))
    x2, s5_new, conv_new = _mixer_tail(x2d, z, b, TS, tn, lw, o_mlstm, o_nsa, s5_state, conv_state8, tm=1024)
    nkv = z8[:, :tn, C_NKV:C_NKV + N_KV_SLOTS * KV_HEADS * HEAD_DIM].reshape(b, tn, N_KV_SLOTS, KV_HEADS, HEAD_DIM)
    state = (nkv[:, :, :4], nkv[:, :, 4:], c1, n1, m1[:, :, 0], s5_new[0], s5_new[1], conv_new)
    return x2, state


def kernel(x_prompt, x_sample, cache_nsa_kv, cache_win_kv, state_mlstm_c, state_mlstm_n, state_mlstm_m,
           state_s5_re, state_s5_im, state_ffn_conv, page_table, w_in, mlstm_gate_b, mlstm_norm_w,
           nsa_gate_b, cmp_w1, cmp_b1, cmp_w2, rel_bias, s5_a_re, s5_a_im, s5_b_re, s5_b_im, s5_c_re,
           s5_c_im, s5_d, s5_log_step, s5_glu_w, s5_glu_b, w_out, ln1_w, ln1_b, ffn_w_up, ffn_conv_w,
           ffn_conv_b, ffn_w_down, ln2_w, ln2_b):
    params = dict(w_in=w_in, mlstm_gate_b=mlstm_gate_b, mlstm_norm_w=mlstm_norm_w, nsa_gate_b=nsa_gate_b,
                  cmp_w1=cmp_w1, cmp_b1=cmp_b1, cmp_w2=cmp_w2, s5_a_re=s5_a_re, s5_a_im=s5_a_im, s5_b_re=s5_b_re,
                  s5_b_im=s5_b_im, s5_c_re=s5_c_re, s5_c_im=s5_c_im, s5_d=s5_d, s5_log_step=s5_log_step,
                  s5_glu_w=s5_glu_w, s5_glu_b=s5_glu_b, w_out=w_out, ln1_w=ln1_w, ln1_b=ln1_b, ffn_w_up=ffn_w_up,
                  ffn_conv_w=ffn_conv_w, ffn_conv_b=ffn_conv_b, ffn_w_down=ffn_w_down, ln2_w=ln2_w, ln2_b=ln2_b)
    depth = w_in.shape[0]
    bp, tp, _ = x_prompt.shape
    bs, tn, _ = x_sample.shape
    assert tp % TK == 0 and tn < CMP_STRIDE and tn <= TS and SUBLANES % bp == 0 and SUBLANES % bs == 0
    assert (tn * bs) % SUBLANES == 0 and tp % min(FFN_TT, tp) == 0
    assert page_table.shape[1] % CMP_PAGES == 0 and cache_nsa_kv.shape[2] == PAGE_SIZE
    xp = x_prompt.reshape(bp * tp, D_MODEL)
    xs = jnp.pad(x_sample, ((0, 0), (0, TS - tn), (0, 0))).reshape(bs * TS, D_MODEL)
    p_states, s_states = [], []
    big = _pack_weights(params)
    for l in range(depth):
        lw = _pack_layer(params, big, l)
        xp, sp = _prompt_layer(xp, bp, tp, lw, rel_bias)
        xs, ss = _sample_layer(xs, bs, tn, lw, rel_bias, l, cache_nsa_kv, page_table, cache_win_kv,
                               (state_mlstm_c[l], jnp.pad(state_mlstm_n[l], ((0, 0), (0, 0), (0, LANES - HEAD_DIM))),
                                state_mlstm_m[l]),
                               (state_s5_re[l], state_s5_im[l]), state_ffn_conv[l])
        p_states.append(sp)
        s_states.append(ss)
    stk = lambda states, i: jnp.stack([s[i] for s in states])
    y_prompt = xp.reshape(bp, tp, D_MODEL)
    y_sample = xs.reshape(bs, TS, D_MODEL)[:, :tn]
    return (y_prompt, y_sample,
            stk(p_states, 0), stk(s_states, 0), stk(p_states, 1), stk(s_states, 1),
            stk(p_states, 2), stk(s_states, 2), stk(p_states, 3), stk(s_states, 3), stk(p_states, 4), stk(s_states, 4),
            stk(p_states, 5), stk(s_states, 5), stk(p_states, 6), stk(s_states, 6), stk(p_states, 7), stk(s_states, 7))
```

```python
import functools
import math

import jax
import jax.numpy as jnp
import numpy as np
from jax import lax
from jax.experimental import pallas as pl
from jax.experimental.pallas import tpu as pltpu

F32 = jnp.float32
BF16 = jnp.bfloat16
NEG_INF = float("-inf")
M_INIT = -1e30
MASKED = -1e30

D_MODEL = 2048
PAGE_SIZE = 128
D_MLSTM = D_MODEL // 4
D_NSA = D_MODEL // 2
D_S5 = D_MODEL - D_MLSTM - D_NSA
HEAD_DIM = 128
MLSTM_HEADS = D_MLSTM // HEAD_DIM
NSA_HEADS = D_NSA // HEAD_DIM
KV_HEADS = 2
GROUP = NSA_HEADS // KV_HEADS
N_KV_SLOTS = 6
CMP_BLOCK = 32
CMP_STRIDE = 16
SEL_BLOCK = 64
SEL_TOPK = 16
FORCE_SCORE = 1e4
WINDOW = 512
S5_GROUP_WIDTH = 16
S5_GROUPS = D_S5 // S5_GROUP_WIDTH
S5_STATE = 64
S5_CH = S5_GROUPS * S5_STATE
NUM_BUCKETS = 32
MAX_DISTANCE = 128
D_FF = ((8 * D_MODEL // 3 + 127) // 128) * 128
CONV_W = 3
LN_EPS = 1e-5
DEPTH = 2
DEEPNORM_ALPHA = (2 * DEPTH) ** 0.25
QK_SCALE = HEAD_DIM ** -0.5

LANES = 128
SUBLANES = 8
V7X_VMEM_LIMIT = 56 * 2 ** 20

C_MQ, C_MK, C_MV, C_MO = 0, 512, 1024, 1536
C_NQ = 2048
C_NKV = 3072
C_SU = 4608
C_GATE = 5120
N_IN = 5376
GATE_MI, GATE_MF, GATE_NG = 0, 4, 8
D_FFP = 5632
MLSTM_L = 256
TQ = 256
TK = 256
S5_LC = 256
FFN_TT = 1024
FFN_TF = 512
FFN_HALO = 16
CMP_PAGES = 32
SEL_PAGES = 32
SEL_TILE = 32
SEL_SHIFT = SEL_BLOCK.bit_length() - 1


def _cparams(*sem):
    return pltpu.CompilerParams(dimension_semantics=sem, vmem_limit_bytes=V7X_VMEM_LIMIT)


def _dot(a, b):
    return jnp.dot(a, b, preferred_element_type=F32)


def _dot_nt(a, b):
    return lax.dot_general(a, b, (((1,), (1,)), ((), ())), preferred_element_type=F32)


def _layer_norm(y, w, b):
    mu = jnp.mean(y, axis=-1, keepdims=True)
    d = y - mu
    var = jnp.mean(d * d, axis=-1, keepdims=True)
    return d * lax.rsqrt(var + LN_EPS) * w + b


def _mm_kernel(x_ref, w_ref, o_ref):
    o_ref[...] = _dot(x_ref[...].astype(BF16), w_ref[...]).astype(o_ref.dtype)


def _up_cols(tf):
    nf, per = D_FFP // tf, tf // LANES
    return lambda j: (j * per + (j >= nf).astype(jnp.int32) * (D_FF // LANES - nf * per)) * LANES


def _matmul(x, w, layer, *, tm, tn, n=None, w_cols=None):
    m, k = x.shape
    n = w.shape[2] if n is None else n
    tm = min(tm, m)
    if w_cols is None:
        w_spec = pl.BlockSpec((None, k, tn), lambda i, j: (layer, 0, j))
    else:
        w_spec = pl.BlockSpec((None, pl.Element(k), pl.Element(tn)), lambda i, j: (layer, 0, w_cols(j)))
    return pl.pallas_call(
        _mm_kernel,
        grid=(m // tm, n // tn),
        in_specs=[pl.BlockSpec((tm, k), lambda i, j: (i, 0)), w_spec],
        out_specs=pl.BlockSpec((tm, tn), lambda i, j: (i, j)),
        out_shape=jax.ShapeDtypeStruct((m, n), F32),
        compiler_params=_cparams("parallel", "parallel"),
        name="proj",
    )(x, w)


KV4_GROUPS = 4 * KV_HEADS
WIN_GROUPS = 2 * KV_HEADS


def _proj_in_kernel(x_ref, w_ref, z_ref, kv_ref, win_ref, *, tn):
    j = pl.program_id(1)
    acc = _dot(x_ref[...].astype(BF16), w_ref[...])
    z_ref[...] = acc
    tm = acc.shape[0]
    per_tile = tn // HEAD_DIM
    first_tile = C_NKV // tn
    for tile in range(first_tile, first_tile + N_KV_SLOTS * KV_HEADS // per_tile):
        @pl.when(j == tile)
        def _(tile=tile):
            for c in range(per_tile):
                grp = (tile - first_tile) * per_tile + c
                val = acc[:, c * HEAD_DIM:(c + 1) * HEAD_DIM]
                if grp < KV4_GROUPS:
                    kv_ref[pl.ds(grp, tm, stride=KV4_GROUPS), :] = val
                else:
                    win_ref[pl.ds(grp - KV4_GROUPS, tm, stride=WIN_GROUPS), :] = val


def _proj_in_carry_kernel(x_ref, w_ref, kv_in_ref, win_in_ref, z_ref, kv_ref, win_ref, *, tn):
    del kv_in_ref, win_in_ref
    _proj_in_kernel(x_ref, w_ref, z_ref, kv_ref, win_ref, tn=tn)


def _proj_in(x, w, layer, rows_so_far, *, tm, tn):
    m, k = x.shape
    depth, _, n = w.shape
    nm = m // tm
    assert C_NKV % tn == 0 and (N_KV_SLOTS * KV_HEADS * HEAD_DIM) % tn == 0
    carried = () if rows_so_far is None else tuple(rows_so_far)
    return pl.pallas_call(
        functools.partial(_proj_in_carry_kernel if carried else _proj_in_kernel, tn=tn),
        grid=(nm, n // tn),
        in_specs=[pl.BlockSpec((tm, k), lambda i, j: (i, 0)), pl.BlockSpec((None, k, tn), lambda i, j: (layer, 0, j))]
                 + [pl.BlockSpec(memory_space=pl.ANY)] * len(carried),
        out_specs=[pl.BlockSpec((tm, tn), lambda i, j: (i, j)),
                   pl.BlockSpec((tm * KV4_GROUPS, HEAD_DIM), lambda i, j: (layer * nm + i, 0)),
                   pl.BlockSpec((tm * WIN_GROUPS, HEAD_DIM), lambda i, j: (layer * nm + i, 0))],
        out_shape=[jax.ShapeDtypeStruct((m, n), F32),
                   jax.ShapeDtypeStruct((depth * m * KV4_GROUPS, HEAD_DIM), F32),
                   jax.ShapeDtypeStruct((depth * m * WIN_GROUPS, HEAD_DIM), F32)],
        input_output_aliases={2: 1, 3: 2} if carried else {},
        compiler_params=_cparams("parallel", "arbitrary"),
        name="proj_in",
    )(x, w, *carried)


def _wout_kernel(om_ref, on_ref, os_ref, x_ref, w_ref, lw_ref, lb_ref, o_ref):
    acc = _dot(om_ref[...].astype(BF16), w_ref[0:D_MLSTM, :])
    acc += _dot(on_ref[...].astype(BF16), w_ref[D_MLSTM:D_MLSTM + D_NSA, :])
    acc += _dot(os_ref[...].astype(BF16), w_ref[D_MLSTM + D_NSA:D_MODEL, :])
    o_ref[...] = _layer_norm(DEEPNORM_ALPHA * x_ref[...] + acc, lw_ref[...], lb_ref[...])


def _wout_ln(om, on, os_, x, w, layer, lw, lb, *, tm):
    m = x.shape[0]
    tm = min(tm, m)
    row = lambda width: pl.BlockSpec((tm, width), lambda i: (i, 0))
    full = lambda a: pl.BlockSpec(a.shape, lambda i: (0, 0))
    w_spec = pl.BlockSpec((None,) + w.shape[1:], lambda i: (layer, 0, 0))
    return pl.pallas_call(
        _wout_kernel,
        grid=(m // tm,),
        in_specs=[row(D_MLSTM), row(D_NSA), row(D_S5), row(D_MODEL), w_spec, full(lw), full(lb)],
        out_specs=row(D_MODEL),
        out_shape=jax.ShapeDtypeStruct((m, D_MODEL), F32),
        compiler_params=_cparams("parallel"),
        name="wout_ln",
    )(om, on, os_, x, w, lw, lb)


def _valid_rows(w_ref, tile):
    rows = w_ref.shape[0]
    row = tile * rows + lax.broadcasted_iota(jnp.int32, (rows, 1), 0)
    w = w_ref[...]
    return jnp.where(row < D_FF, w, jnp.zeros_like(w))


def _down_kernel(h_ref, w_ref, x_ref, lw_ref, lb_ref, o_ref, acc_ref, *, nk):
    k = pl.program_id(1)

    @pl.when(k == 0)
    def _():
        acc_ref[...] = jnp.zeros_like(acc_ref)

    acc_ref[...] += _dot(h_ref[...], _valid_rows(w_ref, k))

    @pl.when(k == nk - 1)
    def _():
        o_ref[...] = _layer_norm(DEEPNORM_ALPHA * x_ref[...] + acc_ref[...], lw_ref[...], lb_ref[...])


def _down_ln(h, w, layer, x, lw, lb, *, tm, tk):
    m, kk = h.shape
    tm = min(tm, m)
    nk = kk // tk
    return pl.pallas_call(
        functools.partial(_down_kernel, nk=nk),
        grid=(m // tm, nk),
        in_specs=[
            pl.BlockSpec((tm, tk), lambda i, k: (i, k)),
            pl.BlockSpec((None, tk, D_MODEL), lambda i, k: (layer, k, 0)),
            pl.BlockSpec((tm, D_MODEL), lambda i, k: (i, 0)),
            pl.BlockSpec((1, D_MODEL), lambda i, k: (0, 0)),
            pl.BlockSpec((1, D_MODEL), lambda i, k: (0, 0)),
        ],
        out_specs=pl.BlockSpec((tm, D_MODEL), lambda i, k: (i, 0)),
        out_shape=jax.ShapeDtypeStruct((m, D_MODEL), F32),
        scratch_shapes=[pltpu.VMEM((tm, D_MODEL), F32)],
        compiler_params=_cparams("parallel", "arbitrary"),
        name="down_ln",
    )(h, w, x, lw, lb)


def _convgate_kernel(a_ref, g_ref, sa_ref, sg_ref, wa_ref, wg_ref, ba_ref, bg_ref, o_ref):
    rows = lax.broadcasted_iota(jnp.int32, (1, SUBLANES, 1), 1)

    def conv(cur_ref, st_ref, w_ref, b_ref):
        cur = cur_ref[...]
        prev = st_ref[...]
        p1 = prev[:, 7:8, :]
        p2 = prev[:, 6:7, :]
        x1 = jnp.where(rows == 0, p1, pltpu.roll(cur, 1, 1))
        x2 = jnp.where(rows == 0, p2, jnp.where(rows == 1, p1, pltpu.roll(cur, 2, 1)))
        w = w_ref[...]
        return b_ref[...] + w[0:1, :] * x2 + w[1:2, :] * x1 + w[2:3, :] * cur

    a = conv(a_ref, sa_ref, wa_ref, ba_ref)
    g = conv(g_ref, sg_ref, wg_ref, bg_ref)
    o_ref[...] = (a * jax.nn.sigmoid(a) * g).astype(o_ref.dtype)


def _convgate(up, state8, conv_w8, conv_b, *, tf):
    b, t, _ = up.shape
    assert t == SUBLANES
    nf = D_FFP // tf
    cur_a = pl.BlockSpec((b, t, tf), lambda fi: (0, 0, fi))
    cur_g = pl.BlockSpec((b, t, tf), lambda fi: (0, 0, fi + nf))
    w_a = pl.BlockSpec((SUBLANES, tf), lambda fi: (0, fi))
    w_g = pl.BlockSpec((SUBLANES, tf), lambda fi: (0, fi + nf))
    b_a = pl.BlockSpec((1, tf), lambda fi: (0, fi))
    b_g = pl.BlockSpec((1, tf), lambda fi: (0, fi + nf))
    return pl.pallas_call(
        _convgate_kernel,
        grid=(nf,),
        in_specs=[cur_a, cur_g, cur_a, cur_g, w_a, w_g, b_a, b_g],
        out_specs=pl.BlockSpec((b, t, tf), lambda fi: (0, 0, fi)),
        out_shape=jax.ShapeDtypeStruct((b, t, D_FFP), BF16),
        compiler_params=_cparams("parallel"),
        name="convgate",
    )(up, up, state8, state8, conv_w8, conv_w8, conv_b, conv_b)


def _ffn_kernel(x_ref, xh_ref, wa_ref, wg_ref, wd_ref, cwa_ref, cwg_ref, cba_ref, cbg_ref, lw_ref, lb_ref,
                o_ref, sa_ref, sg_ref, xe_ref, *, nf):
    first = pl.program_id(1) == 0
    f = pl.program_id(2)
    tt = x_ref.shape[0]

    @pl.when(f == 0)
    def _():
        o_ref[...] = jnp.zeros_like(o_ref)
        halo = jnp.where(first, 0.0, xh_ref[...])
        xe_ref[0:FFN_HALO, :] = halo.astype(BF16)
        xe_ref[FFN_HALO:, :] = x_ref[...].astype(BF16)

    xe = xe_ref[...]

    def branch(w_ref, cw_ref, cb_ref, s_ref):
        up = _dot(xe, w_ref[...])
        s_ref[...] = up[FFN_HALO + tt - SUBLANES:FFN_HALO + tt, :]
        w = cw_ref[...]
        x1 = pltpu.roll(up, 1, 0)[FFN_HALO:, :]
        x2 = pltpu.roll(up, 2, 0)[FFN_HALO:, :]
        return cb_ref[...] + w[0:1, :] * x2 + w[1:2, :] * x1 + w[2:3, :] * up[FFN_HALO:, :]

    a = branch(wa_ref, cwa_ref, cba_ref, sa_ref)
    g = branch(wg_ref, cwg_ref, cbg_ref, sg_ref)
    o_ref[...] += _dot((a * jax.nn.sigmoid(a) * g).astype(BF16), _valid_rows(wd_ref, f))

    @pl.when(f == nf - 1)
    def _():
        o_ref[...] = _layer_norm(DEEPNORM_ALPHA * x_ref[...] + o_ref[...], lw_ref[...], lb_ref[...])


def _ffn_prompt(x3, w_up, w_down, layer, conv_w8, conv_b, lw, lb, *, tt, tf):
    b, t, _ = x3.shape
    nf = D_FFP // tf
    tt = min(tt, t)
    hb = tt // FFN_HALO
    half_a = lambda rows: pl.BlockSpec((rows, tf), lambda bi, ti, fi: (0, fi))
    half_g = lambda rows: pl.BlockSpec((rows, tf), lambda bi, ti, fi: (0, fi + nf))
    cols = _up_cols(tf)
    w_a = pl.BlockSpec((None, pl.Element(D_MODEL), pl.Element(tf)), lambda bi, ti, fi: (layer, 0, cols(fi)))
    w_g = pl.BlockSpec((None, pl.Element(D_MODEL), pl.Element(tf)), lambda bi, ti, fi: (layer, 0, cols(fi + nf)))
    vec = pl.BlockSpec((1, D_MODEL), lambda bi, ti, fi: (0, 0))
    tail = pl.BlockSpec((None, SUBLANES, tf), lambda bi, ti, fi: (bi, 0, fi))
    return pl.pallas_call(
        functools.partial(_ffn_kernel, nf=nf),
        grid=(b, t // tt, nf),
        in_specs=[pl.BlockSpec((None, tt, D_MODEL), lambda bi, ti, fi: (bi, ti, 0), pipeline_mode=pl.Buffered(1)),
                  pl.BlockSpec((None, FFN_HALO, D_MODEL), lambda bi, ti, fi: (bi, jnp.maximum(ti * hb - 1, 0), 0)),
                  w_a, w_g,
                  pl.BlockSpec((None, tf, D_MODEL), lambda bi, ti, fi: (layer, fi, 0)),
                  half_a(SUBLANES), half_g(SUBLANES), half_a(1), half_g(1), vec, vec],
        out_specs=[pl.BlockSpec((None, tt, D_MODEL), lambda bi, ti, fi: (bi, ti, 0)), tail, tail],
        out_shape=[jax.ShapeDtypeStruct((b, t, D_MODEL), F32),
                   jax.ShapeDtypeStruct((b, SUBLANES, D_FFP), F32),
                   jax.ShapeDtypeStruct((b, SUBLANES, D_FFP), F32)],
        scratch_shapes=[pltpu.VMEM((tt + FFN_HALO, D_MODEL), BF16)],
        compiler_params=_cparams("parallel", "arbitrary", "arbitrary"),
        name="ffn_prompt",
    )(x3, x3, w_up, w_up, w_down, conv_w8, conv_w8, conv_b, conv_b, lw, lb)


def _cumsum_rows(x):
    n = x.shape[0]
    rows = lax.broadcasted_iota(jnp.int32, (n, 1), 0)
    d = 1
    while d < n:
        x = x + jnp.where(rows >= d, pltpu.roll(x, d, 0), 0.0)
        d *= 2
    return x


def _log_sigmoid(x):
    return jnp.minimum(x, 0.0) - jnp.log1p(jnp.exp(-jnp.abs(x)))


def _mlstm_kernel(q_ref, k_ref, v_ref, og_ref, g_ref, gb_ref, nw_ref, c0_ref, n0_ref, m0_ref,
                  out_ref, c_ref, n_ref, m_ref, *, L, t_valid):
    ci = pl.program_id(1)
    lin = q_ref.shape[0]

    @pl.when(ci == 0)
    def _():
        c_ref[...] = c0_ref[...]
        n_ref[...] = n0_ref[...]
        m_ref[...] = m0_ref[...]

    def rows_of(ref):
        x = ref[...]
        if lin < L:
            x = jnp.concatenate([x, jnp.zeros((L - lin, x.shape[1]), x.dtype)], axis=0)
        return x

    rows = lax.broadcasted_iota(jnp.int32, (L, 1), 0)
    valid = (ci * L + rows) < t_valid
    pre = rows_of(g_ref) + gb_ref[...]
    lf = jnp.where(valid, _log_sigmoid(pre), 0.0)
    ig = jnp.where(valid, pre, NEG_INF)
    bcum = _cumsum_rows(lf)
    dt = (pltpu.roll(ig, GATE_MF - GATE_MI, 1) - bcum).T
    q_all, k_all, v_all, og_all = rows_of(q_ref), rows_of(k_ref), rows_of(v_ref), rows_of(og_ref)
    tri = lax.broadcasted_iota(jnp.int32, (L, L), 0) >= lax.broadcasted_iota(jnp.int32, (L, L), 1)
    nw = nw_ref[...]

    for h in range(MLSTM_HEADS):
        sl = slice(h * HEAD_DIM, (h + 1) * HEAD_DIM)
        q = q_all[:, sl]
        k = k_all[:, sl] * QK_SCALE
        v = v_all[:, sl]
        qb, kb, vb = q.astype(BF16), k.astype(BF16), v.astype(BF16)
        b_col = bcum[:, GATE_MF + h:GATE_MF + h + 1]
        ig_col = ig[:, GATE_MI + h:GATE_MI + h + 1]
        d_row = dt[GATE_MF + h:GATE_MF + h + 1, :]
        c_prev = c_ref[h]
        n_prev = n_ref[h:h + 1, :]
        m_prev = m_ref[h:h + 1, 0:1]

        dmat = jnp.where(tri, b_col + d_row, NEG_INF)
        g_col = b_col + m_prev
        m_row = jnp.maximum(jnp.max(dmat, axis=1, keepdims=True), g_col)
        a = jnp.exp(dmat - m_row) * _dot_nt(qb, kb)
        w_inter = jnp.exp(g_col - m_row)
        num = _dot(a.astype(BF16), vb) + w_inter * _dot(qb, c_prev.astype(BF16))
        den = jnp.sum(a, axis=1, keepdims=True) + w_inter * jnp.sum(q * n_prev, axis=1, keepdims=True)
        hid = num / jnp.maximum(jnp.abs(den), jnp.exp(-m_row))
        mu = jnp.mean(hid, axis=1, keepdims=True)
        dlt = hid - mu
        var = jnp.mean(dlt * dlt, axis=1, keepdims=True)
        hn = dlt * lax.rsqrt(var + LN_EPS) * nw[:, sl]
        res = jax.nn.sigmoid(og_all[:, sl]) * hn
        out_ref[:, sl] = res[0:lin, :]

        f_tot = b_col[L - 1:L, :]
        w_s = f_tot - b_col + ig_col
        m_new = jnp.maximum(f_tot + m_prev, jnp.max(w_s, axis=0, keepdims=True))
        ws = jnp.exp(w_s - m_new)
        decay = jnp.exp(f_tot + m_prev - m_new)
        c_ref[h] = decay * c_prev + _dot(k.T.astype(BF16), (ws * v).astype(BF16))
        n_ref[h:h + 1, :] = decay * n_prev + jnp.sum(ws * k, axis=0, keepdims=True)
        m_ref[h:h + 1, :] = jnp.broadcast_to(m_new, (1, LANES))


def _mlstm(z, gate_b, norm_w, c0, n0, m0, *, L, lin, t_valid):
    b, tz, _ = z.shape
    nchunks = tz // lin
    qblk = lambda col: pl.BlockSpec((None, lin, D_MLSTM), lambda bi, ci: (bi, ci, col // D_MLSTM))
    st4 = pl.BlockSpec((None, MLSTM_HEADS, HEAD_DIM, HEAD_DIM), lambda bi, ci: (bi, 0, 0, 0))
    st3 = pl.BlockSpec((None, MLSTM_HEADS, LANES), lambda bi, ci: (bi, 0, 0))
    return pl.pallas_call(
        functools.partial(_mlstm_kernel, L=L, t_valid=t_valid),
        grid=(b, nchunks),
        in_specs=[qblk(C_MQ), qblk(C_MK), qblk(C_MV), qblk(C_MO),
                  pl.BlockSpec((None, lin, LANES), lambda bi, ci: (bi, ci, C_GATE // LANES)),
                  pl.BlockSpec((1, LANES), lambda bi, ci: (0, 0)),
                  pl.BlockSpec((1, D_MLSTM), lambda bi, ci: (0, 0)),
                  st4, st3, st3],
        out_specs=[pl.BlockSpec((None, lin, D_MLSTM), lambda bi, ci: (bi, ci, 0)), st4, st3, st3],
        out_shape=[jax.ShapeDtypeStruct((b, tz, D_MLSTM), F32),
                   jax.ShapeDtypeStruct((b, MLSTM_HEADS, HEAD_DIM, HEAD_DIM), F32),
                   jax.ShapeDtypeStruct((b, MLSTM_HEADS, LANES), F32),
                   jax.ShapeDtypeStruct((b, MLSTM_HEADS, LANES), F32)],
        compiler_params=_cparams("parallel", "arbitrary"),
        name="mlstm",
    )(z, z, z, z, z, gate_b, norm_w, c0, n0, m0)


S5_IN_TILES = S5_CH // 256
S5_IN_FEATS = D_S5 // S5_IN_TILES
S5_OUT_TILES = D_S5 // 256
S5_OUT_CH = S5_CH // S5_OUT_TILES


S5_LT = S5_CH // LANES


def _lane_tiles(x):
    return jnp.stack([x[:, k * LANES:(k + 1) * LANES] for k in range(x.shape[1] // LANES)])


def _lane_untile(x):
    return jnp.concatenate([x[k] for k in range(x.shape[0])], axis=1)


def _s5_kernel(u_ref, wb_ref, wc_ref, lr_ref, li_ref, d_ref, gw_ref, gb_ref, x0r_ref, x0i_ref,
               o_ref, xr_ref, xi_ref, sr_ref, si_ref, y_ref, *, steps):
    nb, lc, _ = u_ref.shape

    @pl.when(pl.program_id(0) == 0)
    def _():
        xr_ref[...] = x0r_ref[...]
        xi_ref[...] = x0i_ref[...]

    for b in range(nb):
        ub = u_ref[b].astype(BF16)
        for c in range(S5_IN_TILES):
            bu = _dot(ub[:, c * S5_IN_FEATS:(c + 1) * S5_IN_FEATS], wb_ref[c])
            for k in range(256 // LANES):
                rows = pl.ds(b, lc, stride=nb)
                sr_ref.at[c * (256 // LANES) + k][rows, :] = bu[:, k * LANES:(k + 1) * LANES]
                si_ref.at[c * (256 // LANES) + k][rows, :] = bu[:, 256 + k * LANES:256 + (k + 1) * LANES]
    lam_r = _lane_tiles(jnp.broadcast_to(lr_ref[...], (SUBLANES, S5_CH)))
    lam_i = _lane_tiles(jnp.broadcast_to(li_ref[...], (SUBLANES, S5_CH)))
    row8 = lax.broadcasted_iota(jnp.int32, (1, SUBLANES, 1), 1)
    per_group = SUBLANES // nb

    def group(i, carry):
        xr, xi = carry
        r0 = pl.multiple_of(i * SUBLANES, SUBLANES)
        br = sr_ref[:, pl.ds(r0, SUBLANES), :]
        bi = si_ref[:, pl.ds(r0, SUBLANES), :]
        out_r, out_i = br, bi
        for k in range(per_group):
            nr = lam_r * xr - lam_i * xi + br
            ni = lam_r * xi + lam_i * xr + bi
            here = (row8 >= k * nb) & (row8 < (k + 1) * nb)
            out_r = jnp.where(here, nr, out_r)
            out_i = jnp.where(here, ni, out_i)
            if per_group > 1:
                xr = pltpu.roll(nr, nb, 1)
                xi = pltpu.roll(ni, nb, 1)
            else:
                xr, xi = nr, ni
        sr_ref[:, pl.ds(r0, SUBLANES), :] = out_r
        si_ref[:, pl.ds(r0, SUBLANES), :] = out_i
        return xr, xi

    xr, xi = lax.fori_loop(0, steps * nb // SUBLANES, group, (_lane_tiles(xr_ref[...]), _lane_tiles(xi_ref[...])))
    xr_ref[...] = _lane_untile(xr)
    xi_ref[...] = _lane_untile(xi)
    per_out = S5_OUT_CH // LANES
    for j in range(S5_OUT_TILES):
        lhs_r = jnp.concatenate([sr_ref[j * per_out + k] for k in range(per_out)], axis=1).astype(BF16)
        lhs_i = jnp.concatenate([si_ref[j * per_out + k] for k in range(per_out)], axis=1).astype(BF16)
        y = _dot(lhs_r, wc_ref[0, j]) + _dot(lhs_i, wc_ref[1, j])
        for k in range(256 // LANES):
            y_ref[j * (256 // LANES) + k] = y[:, k * LANES:(k + 1) * LANES]
    for b in range(nb):
        rows = pl.ds(b, lc, stride=nb)
        y = jnp.concatenate([y_ref.at[k][rows, :] for k in range(D_S5 // LANES)], axis=1) + d_ref[...] * u_ref[b]
        zz = jax.nn.gelu(y)
        o_ref[b] = zz * jax.nn.sigmoid(_dot(zz.astype(BF16), gw_ref[...]) + gb_ref[...])


def _s5(z3, wb, wc, lam_r, lam_i, d, glu_w, glu_b, x0r, x0i, *, lc, steps):
    b, tz, _ = z3.shape
    lc = min(lc, tz)
    steps = min(steps, lc)
    assert steps == lc or tz == lc
    full = lambda a: pl.BlockSpec(a.shape, lambda i: (0,) * a.ndim)
    st = pl.BlockSpec((SUBLANES, S5_CH), lambda i: (0, 0))
    return pl.pallas_call(
        functools.partial(_s5_kernel, steps=steps),
        grid=(tz // lc,),
        in_specs=[pl.BlockSpec((b, lc, D_S5), lambda i: (0, i, C_SU // D_S5)),
                  full(wb), full(wc), full(lam_r), full(lam_i), full(d), full(glu_w), full(glu_b), st, st],
        out_specs=[pl.BlockSpec((b, lc, D_S5), lambda i: (0, i, 0)), st, st],
        out_shape=[jax.ShapeDtypeStruct((b, tz, D_S5), F32),
                   jax.ShapeDtypeStruct((SUBLANES, S5_CH), F32),
                   jax.ShapeDtypeStruct((SUBLANES, S5_CH), F32)],
        scratch_shapes=[pltpu.VMEM((S5_LT, lc * b, LANES), F32), pltpu.VMEM((S5_LT, lc * b, LANES), F32),
                        pltpu.VMEM((D_S5 // LANES, lc * b, LANES), F32)],
        compiler_params=_cparams("arbitrary"),
        name="s5",
    )(z3, wb, wc, lam_r, lam_i, d, glu_w, glu_b, x0r, x0i)


N_CMB = 2 * KV_HEADS


def _cmp_project(rows_of, w_ref, o_ref):
    nch = o_ref.shape[0]
    for cmb in range(N_CMB):
        slot = cmb // KV_HEADS
        rows_ref = rows_of(cmb)
        acc = jnp.zeros((nch, 2 * HEAD_DIM), F32)
        for l in range(CMP_STRIDE):
            x = rows_ref[pl.ds(l, nch, stride=CMP_STRIDE), :]
            acc += _dot(x.astype(BF16), w_ref[slot, l])
        o_ref[:, cmb * 2 * HEAD_DIM:(cmb + 1) * 2 * HEAD_DIM] = acc


def _cmp_p_kernel(r0_ref, r1_ref, r2_ref, r3_ref, w_ref, o_ref):
    rows = (r0_ref, r1_ref, r2_ref, r3_ref)
    _cmp_project(lambda cmb: rows[cmb], w_ref, o_ref)


def _cmp_project_prompt(z, wcat):
    b, t, _ = z.shape
    nch = t // CMP_STRIDE
    rows = lambda cmb: pl.BlockSpec((None, t, HEAD_DIM), lambda bi: (bi, 0, C_NKV // HEAD_DIM + cmb))
    return pl.pallas_call(
        _cmp_p_kernel,
        grid=(b,),
        in_specs=[rows(cmb) for cmb in range(N_CMB)] + [pl.BlockSpec(wcat.shape, lambda bi: (0, 0, 0, 0))],
        out_specs=pl.BlockSpec((None, nch, N_CMB * 2 * HEAD_DIM), lambda bi: (bi, 0, 0)),
        out_shape=jax.ShapeDtypeStruct((b, nch, N_CMB * 2 * HEAD_DIM), F32),
        compiler_params=_cparams("parallel"),
        name="cmp_project_prompt",
    )(z, z, z, z, wcat)


def _cmp_s_kernel(pt_ref, *refs):
    del pt_ref
    per_pos = 2 * KV_HEADS
    page_refs = [r.reshape(PAGE_SIZE * per_pos, HEAD_DIM) for r in refs[:CMP_PAGES]]
    w_ref, o_ref = refs[CMP_PAGES:]
    per_page = PAGE_SIZE // CMP_STRIDE
    for cmb in range(N_CMB):
        acc = jnp.zeros((o_ref.shape[0], 2 * HEAD_DIM), F32)
        for l in range(CMP_STRIDE):
            x = jnp.concatenate(
                [page_refs[p][pl.ds(l * per_pos + cmb, per_page, stride=CMP_STRIDE * per_pos), :]
                 for p in range(CMP_PAGES)], axis=0)
            acc += _dot(x.astype(BF16), w_ref[cmb // KV_HEADS, l])
        o_ref[:, cmb * 2 * HEAD_DIM:(cmb + 1) * 2 * HEAD_DIM] = acc


def _page_spec(layer, slot, p, pages_per_step, grid_rank):
    def index_map(*idx):
        bi, si, pt = idx[0], idx[grid_rank - 1], idx[grid_rank]
        return (layer, pt[bi, si * pages_per_step + p], 0, slot, 0, 0)

    return pl.BlockSpec((None, None, PAGE_SIZE, None, KV_HEADS, HEAD_DIM), index_map)


def _cmp_project_sample(cache6, page_table, layer, wcat):
    b, npages = page_table.shape
    steps = npages // CMP_PAGES
    nch = CMP_PAGES * PAGE_SIZE // CMP_STRIDE
    def page_pair(p):
        return pl.BlockSpec((None, None, PAGE_SIZE, 2, KV_HEADS, HEAD_DIM),
                            lambda bi, si, pt: (layer, pt[bi, si * CMP_PAGES + p], 0, 0, 0, 0))

    pages = [page_pair(p) for p in range(CMP_PAGES)]

    grid_spec = pltpu.PrefetchScalarGridSpec(
        num_scalar_prefetch=1,
        grid=(b, steps),
        in_specs=pages + [pl.BlockSpec(wcat.shape, lambda bi, si, pt: (0, 0, 0, 0))],
        out_specs=pl.BlockSpec((None, nch, N_CMB * 2 * HEAD_DIM), lambda bi, si, pt: (bi, si, 0)),
    )
    return pl.pallas_call(
        _cmp_s_kernel,
        grid_spec=grid_spec,
        out_shape=jax.ShapeDtypeStruct((b, steps * nch, N_CMB * 2 * HEAD_DIM), F32),
        compiler_params=_cparams("parallel", "arbitrary"),
        name="cmp_project_sample",
    )(page_table, *([cache6] * CMP_PAGES), wcat)


def _cmp_fin_kernel(p_ref, b1_ref, w2_ref, o_ref):
    nch = p_ref.shape[0]
    for cmb in range(N_CMB):
        slot = cmb // KV_HEADS
        c0 = cmb * 2 * HEAD_DIM
        first = p_ref[:, c0:c0 + HEAD_DIM]
        second = pltpu.roll(p_ref[:, c0 + HEAD_DIM:c0 + 2 * HEAD_DIM], nch - 1, 0)
        hid = b1_ref[slot:slot + 1, :] + first + second
        o_ref[cmb] = _dot(jax.nn.gelu(hid).astype(BF16), w2_ref[slot])


def _cmp_finish(p, b1, w2):
    b, nch, _ = p.shape
    return pl.pallas_call(
        _cmp_fin_kernel,
        grid=(b,),
        in_specs=[pl.BlockSpec((None, nch, p.shape[2]), lambda bi: (bi, 0, 0)),
                  pl.BlockSpec(b1.shape, lambda bi: (0, 0)),
                  pl.BlockSpec(w2.shape, lambda bi: (0, 0, 0))],
        out_specs=pl.BlockSpec((None, N_CMB, nch, HEAD_DIM), lambda bi: (bi, 0, 0, 0)),
        out_shape=jax.ShapeDtypeStruct((b, N_CMB, nch, HEAD_DIM), F32),
        compiler_params=_cparams("parallel"),
        name="cmp_finish",
    )(p, b1, w2)


def _softmax_init(m_scr, l_scr, acc_scr):
    m_scr[...] = jnp.full(m_scr.shape, M_INIT, F32)
    l_scr[...] = jnp.zeros(l_scr.shape, F32)
    acc_scr[...] = jnp.zeros(acc_scr.shape, F32)


def _softmax_update(s, mask, vb, m_scr, l_scr, acc_scr):
    s = jnp.where(mask, s, NEG_INF)
    m_prev = m_scr[...]
    m_new = jnp.maximum(m_prev, jnp.max(s, axis=1, keepdims=True))
    alpha = jnp.exp(m_prev - m_new)
    p = jnp.exp(s - m_new)
    l_scr[...] = alpha * l_scr[...] + jnp.sum(p, axis=1, keepdims=True)
    acc_scr[...] = alpha * acc_scr[...] + _dot(p.astype(BF16), vb)
    m_scr[...] = m_new


def _softmax_result(l_scr, acc_scr):
    l = l_scr[...]
    return acc_scr[...] / jnp.where(l > 0, l, 1.0)


def _masked_probs(s, mask):
    s = jnp.where(mask, s, NEG_INF)
    m = jnp.max(s, axis=1, keepdims=True)
    m = jnp.where(m > NEG_INF, m, 0.0)
    p = jnp.exp(s - m)
    den = jnp.sum(p, axis=1, keepdims=True)
    return p / jnp.where(den > 0, den, 1.0)


def _stack_heads(q):
    return jnp.concatenate([q[:, j * HEAD_DIM:(j + 1) * HEAD_DIM] for j in range(GROUP)], axis=0)


def _block_importance(pc, n_cmp, n_sel):
    ncp = pc.shape[1]
    nsp = -(-n_sel // LANES) * LANES
    c_start = lax.broadcasted_iota(jnp.int32, (ncp, nsp), 0) * CMP_STRIDE
    s_start = lax.broadcasted_iota(jnp.int32, (ncp, nsp), 1) * SEL_BLOCK
    overlap = ((c_start < s_start + SEL_BLOCK) & (c_start + CMP_BLOCK > s_start)
               & (c_start < n_cmp * CMP_STRIDE)).astype(F32)
    return jnp.dot(pc, overlap, preferred_element_type=F32, precision=lax.Precision.HIGHEST)


def _force_blocks(imp, blk, cur):
    forced = (blk == 0) | (blk == cur) | (blk == cur - 1)
    imp = jnp.where(forced, FORCE_SCORE, imp)
    return jnp.where(blk > cur, -1.0, imp)


def _select_blocks_cols(pc, cur, n_cmp, n_sel):
    imp = _block_importance(pc, n_cmp, n_sel)
    blk = lax.broadcasted_iota(jnp.int32, (1, imp.shape[1]), 1)
    imp = _force_blocks(imp, blk, cur)

    def body(sp, count):
        col = jnp.sum(jnp.where(blk == sp, imp, 0.0), axis=1, keepdims=True)
        ahead = (col > imp) | ((col == imp) & (sp < blk))
        return count + ahead.astype(F32)

    count = lax.fori_loop(0, n_sel, body, jnp.zeros(imp.shape, F32), unroll=64)
    return ((count < min(SEL_TOPK, n_sel)) & (blk < n_sel)).astype(F32)


def _select_blocks_rows(pc, cur_row, n_cmp, n_sel):
    rows = pc.shape[0]
    nrow = -(-n_sel // SUBLANES) * SUBLANES
    imp_t = _block_importance(pc, n_cmp, n_sel).T[0:nrow, :]
    blk = lax.broadcasted_iota(jnp.int32, (nrow, 1), 0)
    imp_t = _force_blocks(imp_t, blk, cur_row)
    count = jnp.zeros(imp_t.shape, F32)
    for sp in range(n_sel):
        row = imp_t[sp:sp + 1, :]
        count += ((row > imp_t) | ((row == imp_t) & (sp < blk))).astype(F32)
    sel_t = ((count < min(SEL_TOPK, n_sel)) & (blk < n_sel)).astype(F32)
    sel_t = jnp.concatenate([sel_t, jnp.zeros((LANES - nrow, rows), F32)], axis=0)
    return sel_t.T


def _gate(gt, col):
    lane = lax.broadcasted_iota(jnp.int32, (1, LANES), 1)
    return jax.nn.sigmoid(jnp.sum(jnp.where(lane == col, gt, 0.0), axis=1, keepdims=True))


def _nsa_prompt_kernel(q_ref, ks_ref, vs_ref, kw_ref, vw_ref, kc_ref, vc_ref, bc_ref, tz_ref, gt_ref, gb_ref,
                       o_ref, s_scr, selb_scr, mx_scr, l_scr, acc_scr, *, t_len):
    g = pl.program_id(1)
    qt = pl.program_id(2)
    q0 = qt * TQ
    n_cmp = t_len // CMP_STRIDE - (CMP_BLOCK // CMP_STRIDE) + 1
    n_sel = t_len // SEL_BLOCK
    ncp = kc_ref.shape[0]
    rows4 = GROUP * TQ
    qs = _stack_heads(q_ref[...] * QK_SCALE).astype(BF16)
    qpos = q0 + (lax.broadcasted_iota(jnp.int32, (rows4, 1), 0) & (TQ - 1))

    n_idx = lax.broadcasted_iota(jnp.int32, (1, ncp), 1)
    s_c = _dot_nt(qs, kc_ref[...].astype(BF16)) + bc_ref[...].reshape(rows4, ncp)
    cmask = (qpos - (n_idx * CMP_STRIDE + CMP_BLOCK - 1) >= 0) & (n_idx < n_cmp)
    p_c = _masked_probs(s_c, cmask)
    o_c = _dot(p_c.astype(BF16), vc_ref[...].astype(BF16))
    pc = p_c[0:TQ] + p_c[TQ:2 * TQ] + p_c[2 * TQ:3 * TQ] + p_c[3 * TQ:4 * TQ]
    cur_row = jnp.right_shift(q0 + lax.broadcasted_iota(jnp.int32, (1, TQ), 1), SEL_SHIFT)
    sel = _select_blocks_rows(pc, cur_row, n_cmp, n_sel)
    sel_add = ((sel - 1.0) * -MASKED).astype(BF16)

    kt_hi = (q0 + TQ - 1) // TK + 1
    kt_far = jnp.maximum(q0 - (FAR_DIST - 1), 0) // TK
    kidx = lax.broadcasted_iota(jnp.int32, (1, TK), 1)
    srow = lax.broadcasted_iota(jnp.int32, (LANES, TK), 0)

    def spread_selection(kt, carry):
        k0 = kt * TK
        expand = (jnp.right_shift(k0 + lax.broadcasted_iota(jnp.int32, (LANES, TK), 1), SEL_SHIFT) == srow).astype(BF16)
        selb_scr[kt] = _dot(sel_add, expand)
        return carry

    lax.fori_loop(0, kt_hi, spread_selection, 0)

    def tile_bias(k0):
        return tz_ref[jnp.clip((q0 - k0) // TQ, 0, N_TZ - 1)].reshape(rows4, TK)

    def attend(k_ref, v_ref, kt_lo, mask_tile):
        mx_scr[...] = jnp.full(mx_scr.shape, NEG_INF, F32)

        def scores(near):
            def body(kt, carry):
                k0 = pl.multiple_of(kt * TK, TK)
                s = _dot_nt(qs, k_ref[pl.ds(k0, TK), :].astype(BF16))
                if near:
                    s = s + tile_bias(k0)
                s = mask_tile(k0, s, near)
                s_scr[kt] = s
                mx_scr[...] = jnp.maximum(mx_scr[...], jnp.maximum(s[:, 0:LANES], s[:, LANES:TK]))
                return carry
            return body

        kt_mid = jnp.maximum(kt_far, kt_lo)
        lax.fori_loop(kt_lo, kt_mid, scores(False), 0)
        lax.fori_loop(kt_mid, kt_hi, scores(True), 0)
        m = jnp.max(mx_scr[...], axis=1, keepdims=True)
        mx_scr[...] = jnp.broadcast_to(jnp.where(m > NEG_INF, m, 0.0), mx_scr.shape)
        l_scr[...] = jnp.zeros(l_scr.shape, F32)
        acc_scr[...] = jnp.zeros(acc_scr.shape, F32)

        def probs(kt, carry):
            k0 = pl.multiple_of(kt * TK, TK)
            m_rep = mx_scr[...]
            p = jnp.exp(s_scr[kt] - jnp.concatenate([m_rep, m_rep], axis=1))
            l_scr[...] += p[:, 0:LANES] + p[:, LANES:TK]
            acc_scr[...] += _dot(p.astype(BF16), v_ref[pl.ds(k0, TK), :].astype(BF16))
            return carry

        lax.fori_loop(kt_lo, kt_hi, probs, 0)
        l = jnp.sum(l_scr[...], axis=1, keepdims=True)
        return acc_scr[...] / jnp.where(l > 0, l, 1.0)

    def sel_mask(k0, s, near):
        s = (s.reshape(GROUP, TQ, TK) + selb_scr[k0 // TK][None]).reshape(rows4, TK)
        return jnp.where(k0 + kidx <= qpos, s, NEG_INF) if near else s

    o_s = attend(ks_ref, vs_ref, 0, sel_mask)

    def win_mask(k0, s, near):
        dist = qpos - (k0 + kidx)
        return jnp.where((dist >= 0) & (dist <= WINDOW), s, NEG_INF)

    o_w = attend(kw_ref, vw_ref, jnp.maximum(q0 - WINDOW, 0) // TK, win_mask)

    gt = gt_ref[...] + gb_ref[...]
    for j in range(GROUP):
        col = GATE_NG + (g * GROUP + j) * 3
        r = slice(j * TQ, (j + 1) * TQ)
        o_ref[:, j * HEAD_DIM:(j + 1) * HEAD_DIM] = (
            _gate(gt, col) * o_c[r] + _gate(gt, col + 1) * o_s[r] + _gate(gt, col + 2) * o_w[r])


def _nsa_prompt(z, kcvc, bias_c, tz, gate_b):
    b, t, _ = z.shape
    ncp = kcvc.shape[2]
    nq = t // TQ
    kv = lambda slot: pl.BlockSpec((None, t, HEAD_DIM), lambda bi, g, qi: (bi, 0, C_NKV // HEAD_DIM + slot * KV_HEADS + g))
    cmp_blk = lambda slot: pl.BlockSpec((None, None, ncp, HEAD_DIM), lambda bi, g, qi: (bi, slot * KV_HEADS + g, 0, 0))
    return pl.pallas_call(
        functools.partial(_nsa_prompt_kernel, t_len=t),
        grid=(b, KV_HEADS, nq),
        in_specs=[pl.BlockSpec((None, TQ, GROUP * HEAD_DIM), lambda bi, g, qi: (bi, qi, C_NQ // (GROUP * HEAD_DIM) + g)),
                  kv(2), kv(3), kv(4), kv(5), cmp_blk(0), cmp_blk(1),
                  pl.BlockSpec((GROUP, TQ, ncp), lambda bi, g, qi: (g, qi, 0)),
                  pl.BlockSpec((N_TZ, GROUP, TQ, TK), lambda bi, g, qi: (0, g, 0, 0)),
                  pl.BlockSpec((None, TQ, LANES), lambda bi, g, qi: (bi, qi, C_GATE // LANES)),
                  pl.BlockSpec((1, LANES), lambda bi, g, qi: (0, 0))],
        out_specs=pl.BlockSpec((None, TQ, GROUP * HEAD_DIM), lambda bi, g, qi: (bi, qi, g)),
        out_shape=jax.ShapeDtypeStruct((b, t, D_NSA), F32),
        scratch_shapes=[pltpu.VMEM((t // TK, GROUP * TQ, TK), F32), pltpu.VMEM((t // TK, TQ, TK), F32),
                        pltpu.VMEM((GROUP * TQ, LANES), F32),
                        pltpu.VMEM((GROUP * TQ, LANES), F32), pltpu.VMEM((GROUP * TQ, HEAD_DIM), F32)],
        compiler_params=_cparams("parallel", "parallel", "arbitrary"),
        name="nsa_prompt",
    )(z, z, z, z, z, kcvc, kcvc, bias_c, tz, z, gate_b)


TS = SUBLANES
ROWS_S = GROUP * TS


def _nsa_s_cmp_kernel(q_ref, kc_ref, vc_ref, bc_ref, oc_ref, sel_ref, *, pos0, n_cmp, n_sel):
    nch = kc_ref.shape[0]
    qs = _stack_heads(q_ref[...] * QK_SCALE).astype(BF16)
    qpos = pos0 + (lax.broadcasted_iota(jnp.int32, (ROWS_S, 1), 0) & (TS - 1))
    n_idx = lax.broadcasted_iota(jnp.int32, (1, nch), 1)
    s_c = _dot_nt(qs, kc_ref[...].astype(BF16)) + bc_ref[...].reshape(ROWS_S, nch)
    cmask = (qpos - (n_idx * CMP_STRIDE + CMP_BLOCK - 1) >= 0) & (n_idx < n_cmp)
    p_c = _masked_probs(s_c, cmask)
    oc_ref[...] = _dot(p_c.astype(BF16), vc_ref[...].astype(BF16))
    pc = p_c[0:TS] + p_c[TS:2 * TS] + p_c[2 * TS:3 * TS] + p_c[3 * TS:4 * TS]
    sel_ref[...] = _select_blocks_cols(pc, jnp.right_shift(qpos[0:TS], SEL_SHIFT), n_cmp, n_sel)


def _nsa_sample_cmp(z8, kcvc, bias_c, *, pos0, n_cmp, n_sel):
    b = z8.shape[0]
    nch = kcvc.shape[2]
    nsp = -(-n_sel // LANES) * LANES
    cmp_blk = lambda slot: pl.BlockSpec((None, None, nch, HEAD_DIM), lambda bi, g: (bi, slot * KV_HEADS + g, 0, 0))
    return pl.pallas_call(
        functools.partial(_nsa_s_cmp_kernel, pos0=pos0, n_cmp=n_cmp, n_sel=n_sel),
        grid=(b, KV_HEADS),
        in_specs=[pl.BlockSpec((None, TS, GROUP * HEAD_DIM), lambda bi, g: (bi, 0, C_NQ // (GROUP * HEAD_DIM) + g)),
                  cmp_blk(0), cmp_blk(1),
                  pl.BlockSpec((GROUP, TS, nch), lambda bi, g: (g, 0, 0))],
        out_specs=[pl.BlockSpec((None, None, ROWS_S, HEAD_DIM), lambda bi, g: (bi, g, 0, 0)),
                   pl.BlockSpec((None, None, TS, nsp), lambda bi, g: (bi, g, 0, 0))],
        out_shape=[jax.ShapeDtypeStruct((b, KV_HEADS, ROWS_S, HEAD_DIM), F32),
                   jax.ShapeDtypeStruct((b, KV_HEADS, TS, nsp), F32)],
        compiler_params=_cparams("parallel", "parallel"),
        name="nsa_sample_cmp",
    )(z8, kcvc, kcvc, bias_c)


def _pad_rows(x, n):
    return jnp.concatenate([x, jnp.zeros((n - x.shape[0], x.shape[1]), x.dtype)], axis=0)


def _nsa_s_sel_kernel(pt_ref, q_ref, sel_ref, kn_ref, vn_ref, bt_ref, bn_ref, bf_ref, *refs, pos0, npages):
    del pt_ref
    rows_pp = PAGE_SIZE * KV_HEADS
    k_refs = [r.reshape(rows_pp, HEAD_DIM) for r in refs[:SEL_PAGES]]
    v_refs = [r.reshape(rows_pp, HEAD_DIM) for r in refs[SEL_PAGES:2 * SEL_PAGES]]
    o_ref, kbuf, vbuf, m_scr, l_scr, acc_scr = refs[2 * SEL_PAGES:]
    step = pl.program_id(1)
    last = step == npages // SEL_PAGES - 1
    rows = KV_HEADS * ROWS_S
    ncol = SEL_TILE * rows_pp
    head_shift = KV_HEADS.bit_length() - 1
    blk_per_page = PAGE_SIZE // SEL_BLOCK

    @pl.when(step == 0)
    def _():
        _softmax_init(m_scr, l_scr, acc_scr)

    nsp = sel_ref.shape[-1]
    ridx = lax.broadcasted_iota(jnp.int32, (rows, 1), 0)
    qpos = pos0 + (ridx & (TS - 1))
    qs = jnp.concatenate([_stack_heads(q_ref[:, g * GROUP * HEAD_DIM:(g + 1) * GROUP * HEAD_DIM] * QK_SCALE)
                          for g in range(KV_HEADS)], axis=0).astype(BF16)
    sel_rows = jnp.concatenate([sel_ref[g] for g in range(KV_HEADS) for _ in range(GROUP)], axis=0)
    bias_far = bf_ref[...].reshape(rows, PAGE_SIZE)[:, 0:1]
    col = lax.broadcasted_iota(jnp.int32, (1, ncol), 1)
    own_head = (col & (KV_HEADS - 1)) == ridx // ROWS_S
    expand = (jnp.right_shift(lax.broadcasted_iota(jnp.int32, (LANES, ncol), 1), SEL_SHIFT + head_shift)
              == lax.broadcasted_iota(jnp.int32, (LANES, ncol), 0)).astype(BF16)
    blk_r = lax.broadcasted_iota(jnp.int32, (nsp, LANES), 0)
    blk_c = lax.broadcasted_iota(jnp.int32, (nsp, LANES), 1)
    n_tiles = SEL_PAGES // SEL_TILE

    for tile in range(n_tiles):
        for i in range(SEL_TILE):
            p = tile * SEL_TILE + i
            kbuf[i * rows_pp:(i + 1) * rows_pp, :] = k_refs[p][...].astype(BF16)
            vbuf[i * rows_pp:(i + 1) * rows_pp, :] = v_refs[p][...].astype(BF16)
        page0 = step * SEL_PAGES + tile * SEL_TILE
        window = ((blk_r == page0 * blk_per_page + blk_c) & (blk_c < SEL_TILE * blk_per_page)).astype(BF16)
        chosen = _dot(_dot(sel_rows.astype(BF16), window).astype(BF16), expand) > 0.5
        s = _dot_nt(qs, kbuf[...]) + bias_far
        if tile == n_tiles - 1:
            tail = jnp.where(last, bt_ref[...].reshape(rows, rows_pp) - bias_far, 0.0)
            s = s + jnp.concatenate([jnp.zeros((rows, ncol - rows_pp), F32), tail], axis=1)
        key = page0 * PAGE_SIZE + jnp.right_shift(col, head_shift)
        _softmax_update(s, chosen & own_head & (key <= qpos), vbuf[...], m_scr, l_scr, acc_scr)

    @pl.when(last)
    def _():
        kidx = lax.broadcasted_iota(jnp.int32, (1, PAGE_SIZE), 1)
        new_blk = lax.broadcasted_iota(jnp.int32, (1, nsp), 1) == npages * blk_per_page
        flag = jnp.sum(jnp.where(new_blk, sel_rows, 0.0), axis=1, keepdims=True) > 0.5
        bias_new = bn_ref[...].reshape(rows, PAGE_SIZE)
        for g in range(KV_HEADS):
            r = slice(g * ROWS_S, (g + 1) * ROWS_S)
            kn = _pad_rows(kn_ref[:, g * HEAD_DIM:(g + 1) * HEAD_DIM], PAGE_SIZE).astype(BF16)
            vn = _pad_rows(vn_ref[:, g * HEAD_DIM:(g + 1) * HEAD_DIM], PAGE_SIZE).astype(BF16)
            mask = flag[r] & (pos0 + kidx <= qpos[r]) & (kidx < TS)
            _softmax_update(_dot_nt(qs[r], kn) + bias_new[r], mask, vn, m_scr.at[r], l_scr.at[r], acc_scr.at[r])
            o_ref[g] = _softmax_result(l_scr.at[r], acc_scr.at[r])


def _nsa_sample_sel(z8, sel, bias_tail, bias_new, bias_far, cache6, page_table, layer, *, pos0):
    b, npages = page_table.shape
    nsp = sel.shape[-1]
    steps = npages // SEL_PAGES
    slot_cols = KV_HEADS * HEAD_DIM
    new_rows = lambda slot: pl.BlockSpec((None, TS, slot_cols),
                                         lambda bi, si, pt: (bi, 0, (C_NKV + slot * slot_cols) // slot_cols))
    whole = lambda a: pl.BlockSpec(a.shape, lambda bi, si, pt: (0,) * a.ndim)
    grid_spec = pltpu.PrefetchScalarGridSpec(
        num_scalar_prefetch=1,
        grid=(b, steps),
        in_specs=[pl.BlockSpec((None, TS, D_NSA), lambda bi, si, pt: (bi, 0, C_NQ // D_NSA)),
                  pl.BlockSpec((None, KV_HEADS, TS, nsp), lambda bi, si, pt: (bi, 0, 0, 0)),
                  new_rows(2), new_rows(3), whole(bias_tail), whole(bias_new), whole(bias_far)]
                 + [_page_spec(layer, 2, p, SEL_PAGES, 2) for p in range(SEL_PAGES)]
                 + [_page_spec(layer, 3, p, SEL_PAGES, 2) for p in range(SEL_PAGES)],
        out_specs=pl.BlockSpec((None, KV_HEADS, ROWS_S, HEAD_DIM), lambda bi, si, pt: (bi, 0, 0, 0)),
        scratch_shapes=[pltpu.VMEM((SEL_TILE * PAGE_SIZE * KV_HEADS, HEAD_DIM), BF16),
                        pltpu.VMEM((SEL_TILE * PAGE_SIZE * KV_HEADS, HEAD_DIM), BF16),
                        pltpu.VMEM((KV_HEADS * ROWS_S, 1), F32), pltpu.VMEM((KV_HEADS * ROWS_S, 1), F32),
                        pltpu.VMEM((KV_HEADS * ROWS_S, HEAD_DIM), F32)],
    )
    return pl.pallas_call(
        functools.partial(_nsa_s_sel_kernel, pos0=pos0, npages=npages),
        grid_spec=grid_spec,
        out_shape=jax.ShapeDtypeStruct((b, KV_HEADS, ROWS_S, HEAD_DIM), F32),
        compiler_params=_cparams("parallel", "arbitrary"),
        name="nsa_sample_sel",
    )(page_table, z8, sel, z8, z8, bias_tail, bias_new, bias_far, *([cache6] * (2 * SEL_PAGES)))


def _nsa_s_win_kernel(q_ref, kw_ref, vw_ref, kn_ref, vn_ref, bw_ref, oc_ref, os_ref, gt_ref, gb_ref, o_ref,
                      m_scr, l_scr, acc_scr, *, pos0):
    g = pl.program_id(1)
    wb = kw_ref.shape[0]
    own = pl.ds(g, wb, stride=KV_HEADS)
    kw = kw_ref.reshape(wb * KV_HEADS, HEAD_DIM)[own, :]
    vw = vw_ref.reshape(wb * KV_HEADS, HEAD_DIM)[own, :]
    qs = _stack_heads(q_ref[...] * QK_SCALE).astype(BF16)
    qpos = pos0 + (lax.broadcasted_iota(jnp.int32, (ROWS_S, 1), 0) & (TS - 1))
    bias = bw_ref[...].reshape(ROWS_S, wb + PAGE_SIZE)
    _softmax_init(m_scr, l_scr, acc_scr)
    dist = qpos - (pos0 - wb + lax.broadcasted_iota(jnp.int32, (1, wb), 1))
    s = _dot_nt(qs, kw.astype(BF16)) + bias[:, 0:wb]
    _softmax_update(s, (dist >= 0) & (dist <= WINDOW), vw.astype(BF16), m_scr, l_scr, acc_scr)
    kidx = lax.broadcasted_iota(jnp.int32, (1, PAGE_SIZE), 1)
    dist = qpos - (pos0 + kidx)
    s = _dot_nt(qs, _pad_rows(kn_ref[...], PAGE_SIZE).astype(BF16)) + bias[:, wb:wb + PAGE_SIZE]
    _softmax_update(s, (dist >= 0) & (dist <= WINDOW) & (kidx < TS), _pad_rows(vn_ref[...], PAGE_SIZE).astype(BF16),
                    m_scr, l_scr, acc_scr)
    o_w = _softmax_result(l_scr, acc_scr)
    o_c = oc_ref[...]
    o_s = os_ref[...]
    gt = gt_ref[...] + gb_ref[...]
    for j in range(GROUP):
        col = GATE_NG + (g * GROUP + j) * 3
        r = slice(j * TS, (j + 1) * TS)
        o_ref[:, j * HEAD_DIM:(j + 1) * HEAD_DIM] = (
            _gate(gt, col) * o_c[r] + _gate(gt, col + 1) * o_s[r] + _gate(gt, col + 2) * o_w[r])


def _nsa_sample_win(z8, win6, layer, bias_w, o_c, o_s, gate_b, *, pos0):
    _, b, wb = win6.shape[:3]
    cached = lambda slot: pl.BlockSpec((None, None, wb, None, KV_HEADS, HEAD_DIM),
                                       lambda bi, g: (layer, bi, 0, slot, 0, 0))
    zcol = lambda col: (lambda bi, g: (bi, 0, col // HEAD_DIM + g))
    part = pl.BlockSpec((None, None, ROWS_S, HEAD_DIM), lambda bi, g: (bi, g, 0, 0))
    return pl.pallas_call(
        functools.partial(_nsa_s_win_kernel, pos0=pos0),
        grid=(b, KV_HEADS),
        in_specs=[pl.BlockSpec((None, TS, GROUP * HEAD_DIM), lambda bi, g: (bi, 0, C_NQ // (GROUP * HEAD_DIM) + g)),
                  cached(0), cached(1),
                  pl.BlockSpec((None, TS, HEAD_DIM), zcol(C_NKV + 4 * KV_HEADS * HEAD_DIM)),
                  pl.BlockSpec((None, TS, HEAD_DIM), zcol(C_NKV + 5 * KV_HEADS * HEAD_DIM)),
                  pl.BlockSpec((GROUP, TS, wb + PAGE_SIZE), lambda bi, g: (g, 0, 0)),
                  part, part,
                  pl.BlockSpec((None, TS, LANES), lambda bi, g: (bi, 0, C_GATE // LANES)),
                  pl.BlockSpec((1, LANES), lambda bi, g: (0, 0))],
        out_specs=pl.BlockSpec((None, TS, GROUP * HEAD_DIM), lambda bi, g: (bi, 0, g)),
        out_shape=jax.ShapeDtypeStruct((b, TS, D_NSA), F32),
        scratch_shapes=[pltpu.VMEM((ROWS_S, 1), F32), pltpu.VMEM((ROWS_S, 1), F32), pltpu.VMEM((ROWS_S, HEAD_DIM), F32)],
        compiler_params=_cparams("parallel", "parallel"),
        name="nsa_sample_win",
    )(z8, win6, win6, z8, z8, bias_w, o_c, o_s, z8, gate_b)


def _t5_bucket(dist):
    n = np.maximum(dist, 0)
    exact = NUM_BUCKETS // 2
    nf = np.maximum(n, 1).astype(np.float32)
    large = exact + (np.log(nf / np.float32(exact)) / np.float32(math.log(MAX_DISTANCE / exact))
                     * np.float32(NUM_BUCKETS - exact)).astype(np.int32)
    return np.where(n < exact, n, np.minimum(large, NUM_BUCKETS - 1)).astype(np.int32)


FAR_DIST = int(np.max(np.nonzero(_t5_bucket(np.arange(4 * MAX_DISTANCE)) < NUM_BUCKETS - 1)[0])) + 1
N_TZ = (FAR_DIST + TK - 2) // TQ + 1


def _bias_table(rel_bias, dist):
    onehot = jax.nn.one_hot(jnp.asarray(_t5_bucket(dist).astype(np.int8)), NUM_BUCKETS, dtype=F32)
    return jnp.einsum("rcb,bh->hrc", onehot, rel_bias, precision=lax.Precision.HIGHEST)


def _pack_weights(p):
    w_in = p["w_in"].astype(BF16)
    pad = jnp.zeros(w_in.shape[:2] + (N_IN - C_GATE - 32,), BF16)
    w_in_p = jnp.concatenate([w_in[..., 0:2048], w_in[..., 2056:3080], w_in[..., 3080:4616], w_in[..., 4640:5152],
                              w_in[..., 2048:2056], w_in[..., 4616:4640], pad], axis=2)
    return dict(
        w_in=w_in_p, w_out=p["w_out"].astype(BF16), w_down=p["ffn_w_down"].astype(BF16),
        w_up=jnp.pad(p["ffn_w_up"].astype(BF16), ((0, 0), (0, 0), (0, D_FFP - D_FF))))


def _pack_layer(p, big, l):
    gb = p["mlstm_gate_b"][l]
    mlstm_gate_b = jnp.zeros((1, LANES), F32).at[0, GATE_MI:GATE_MI + 4].set(gb[0]).at[0, GATE_MF:GATE_MF + 4].set(gb[1])
    nsa_gate_b = jnp.zeros((1, LANES), F32).at[0, GATE_NG:GATE_NG + 3 * NSA_HEADS].set(p["nsa_gate_b"][l].reshape(-1))
    w1 = p["cmp_w1"][l]
    wcat = jnp.concatenate([w1[:, 0:CMP_STRIDE], w1[:, CMP_STRIDE:CMP_BLOCK]], axis=-1).astype(BF16)
    lam = lax.complex(p["s5_a_re"][l], p["s5_a_im"][l])
    lam_bar = jnp.exp(lam * jnp.exp(p["s5_log_step"][l])[:, None])
    b_bar = ((lam_bar - 1.0) / lam)[..., None] * lax.complex(p["s5_b_re"][l], p["s5_b_im"][l])
    gi = S5_GROUPS // S5_IN_TILES
    bd_in = lambda m: jnp.einsum("cgph,gk->cghkp", m.reshape(S5_IN_TILES, gi, S5_STATE, S5_GROUP_WIDTH),
                                 jnp.eye(gi, dtype=F32)).reshape(S5_IN_TILES, S5_IN_FEATS, 256)
    wb = jnp.concatenate([bd_in(b_bar.real), bd_in(b_bar.imag)], axis=2).astype(BF16)
    go = S5_GROUPS // S5_OUT_TILES
    bd_out = lambda m: jnp.einsum("jghp,gk->jgpkh", m.reshape(S5_OUT_TILES, go, S5_GROUP_WIDTH, S5_STATE),
                                  jnp.eye(go, dtype=F32)).reshape(S5_OUT_TILES, S5_OUT_CH, 256)
    wc = jnp.stack([bd_out(p["s5_c_re"][l]), -bd_out(p["s5_c_im"][l])]).astype(BF16)
    half = lambda w: jnp.pad(w, ((0, 0), (0, D_FFP - D_FF)))
    conv_w = p["ffn_conv_w"][l]
    conv_b = p["ffn_conv_b"][l][None, :]
    return dict(
        big, layer=l, mlstm_gate_b=mlstm_gate_b, mlstm_norm_w=p["mlstm_norm_w"][l][None, :], nsa_gate_b=nsa_gate_b,
        wcat=wcat, cmp_b1=p["cmp_b1"][l], cmp_w2=p["cmp_w2"][l].astype(BF16),
        s5_wb=wb, s5_wc=wc, s5_lam_r=lam_bar.real.reshape(1, S5_CH), s5_lam_i=lam_bar.imag.reshape(1, S5_CH),
        s5_d=p["s5_d"][l][None, :], s5_glu_w=p["s5_glu_w"][l].astype(BF16), s5_glu_b=p["s5_glu_b"][l][None, :],
        ln1_w=p["ln1_w"][l][None, :], ln1_b=p["ln1_b"][l][None, :],
        conv_w8=jnp.pad(jnp.concatenate([half(conv_w[:, :D_FF]), half(conv_w[:, D_FF:])], axis=1),
                        ((0, SUBLANES - CONV_W), (0, 0))),
        conv_b=jnp.concatenate([half(conv_b[:, :D_FF]), half(conv_b[:, D_FF:])], axis=1),
        ln2_w=p["ln2_w"][l][None, :], ln2_b=p["ln2_b"][l][None, :],
    )


def _unpad_ff(x):
    return jnp.concatenate([x[..., :D_FF], x[..., D_FFP:D_FFP + D_FF]], axis=-1)


def _pad_ff(x):
    pad = [(0, 0)] * (x.ndim - 1) + [(0, D_FFP - D_FF)]
    return jnp.concatenate([jnp.pad(x[..., :D_FF], pad), jnp.pad(x[..., D_FF:], pad)], axis=-1)


def _mixer_tail(x2d, z, b, t, t_use, lw, o_mlstm, o_nsa, s5_state, conv_state8, tm):
    pad8 = lambda s: jnp.pad(s.reshape(b, S5_CH), ((0, SUBLANES - b), (0, 0)))
    o_s5, xr, xi = _s5(z.reshape(b, t, N_IN), lw["s5_wb"], lw["s5_wc"], lw["s5_lam_r"], lw["s5_lam_i"], lw["s5_d"],
                       lw["s5_glu_w"], lw["s5_glu_b"], pad8(s5_state[0]), pad8(s5_state[1]), lc=S5_LC, steps=t_use)
    o_s5 = o_s5.reshape(b * t, D_S5)
    layer = lw["layer"]
    x1 = _wout_ln(o_mlstm.reshape(b * t, D_MLSTM), o_nsa.reshape(b * t, D_NSA), o_s5, x2d,
                  lw["w_out"], layer, lw["ln1_w"], lw["ln1_b"], tm=256)
    s5_new = (xr[:b].reshape(b, S5_GROUPS, S5_STATE), xi[:b].reshape(b, S5_GROUPS, S5_STATE))
    if conv_state8 is None:
        x2, tail_a, tail_g = _ffn_prompt(x1.reshape(b, t, D_MODEL), lw["w_up"], lw["w_down"], layer, lw["conv_w8"],
                                         lw["conv_b"], lw["ln2_w"], lw["ln2_b"], tt=FFN_TT, tf=FFN_TF)
        keep = slice(SUBLANES - (CONV_W - 1), SUBLANES)
        conv_new = jnp.concatenate([tail_a[:, keep, :D_FF], tail_g[:, keep, :D_FF]], axis=-1)
        return x2.reshape(b * t, D_MODEL), s5_new, conv_new
    up = _matmul(x1, lw["w_up"], layer, tm=tm, tn=FFN_TF, n=2 * D_FFP, w_cols=_up_cols(FFN_TF))
    hgate = _convgate(up.reshape(b, t, 2 * D_FFP), conv_state8, lw["conv_w8"], lw["conv_b"], tf=FFN_TF)
    x2 = _down_ln(hgate.reshape(b * t, D_FFP), lw["w_down"], layer, x1, lw["ln2_w"], lw["ln2_b"], tm=512, tk=1408)
    conv_new = _unpad_ff(up.reshape(b, t, 2 * D_FFP)[:, t_use - (CONV_W - 1):t_use])
    return x2, s5_new, conv_new


def _prompt_layer(x2d, b, t, lw, rel_bias, rows_so_far):
    z, kv_rows, win_rows = _proj_in(x2d, lw["w_in"], lw["layer"], rows_so_far, tm=min(1024, b * t), tn=768)
    z3 = z.reshape(b, t, N_IN)
    zeros = lambda *s: jnp.zeros(s, F32)
    o_mlstm, c1, n1, m1 = _mlstm(z3, lw["mlstm_gate_b"], lw["mlstm_norm_w"],
                                 zeros(b, MLSTM_HEADS, HEAD_DIM, HEAD_DIM), zeros(b, MLSTM_HEADS, LANES),
                                 zeros(b, MLSTM_HEADS, LANES), L=MLSTM_L, lin=MLSTM_L, t_valid=t)
    kcvc = _cmp_finish(_cmp_project_prompt(z3, lw["wcat"]), lw["cmp_b1"], lw["cmp_w2"])
    ncp = t // CMP_STRIDE
    bias_c = _bias_table(rel_bias, np.arange(t)[:, None] - (np.arange(ncp) * CMP_STRIDE + CMP_BLOCK - 1)[None, :])
    ti = np.arange(TQ)[:, None] - np.arange(TK)[None, :]
    far = _bias_table(rel_bias, np.full((1, 1), FAR_DIST))
    tz = jnp.stack([_bias_table(rel_bias, d * TQ + ti) - far for d in range(N_TZ)])
    o_nsa = _nsa_prompt(z3, kcvc, bias_c, tz, lw["nsa_gate_b"])
    x2, s5_new, conv_new = _mixer_tail(
        x2d, z, b, t, t, lw, o_mlstm, o_nsa, (zeros(b, S5_GROUPS, S5_STATE), zeros(b, S5_GROUPS, S5_STATE)),
        None, tm=1024)
    state = (None, None, c1, n1, m1[:, :, 0], s5_new[0], s5_new[1], conv_new)
    return x2, state, (kv_rows, win_rows)


def _sample_layer(x2d, b, tn, lw, rel_bias, layer, cache6, page_table, win6, mlstm_state, s5_state, conv_state):
    npages = page_table.shape[1]
    pos0 = npages * PAGE_SIZE
    z = _matmul(x2d, lw["w_in"], lw["layer"], tm=1024, tn=768)
    z8 = z.reshape(b, TS, N_IN)
    c0, n0, m0 = mlstm_state
    o_mlstm, c1, n1, m1 = _mlstm(z8, lw["mlstm_gate_b"], lw["mlstm_norm_w"], c0, n0,
                                 jnp.broadcast_to(m0[:, :, None], (b, MLSTM_HEADS, LANES)),
                                 L=LANES, lin=TS, t_valid=tn)
    n_chunks = (pos0 + tn) // CMP_STRIDE
    n_cmp = n_chunks - CMP_BLOCK // CMP_STRIDE + 1
    n_sel = -(-(pos0 + tn) // SEL_BLOCK)
    kcvc = _cmp_finish(_cmp_project_sample(cache6, page_table, layer, lw["wcat"]), lw["cmp_b1"], lw["cmp_w2"])
    qpos = pos0 + np.arange(TS)[:, None]
    bias_c = _bias_table(rel_bias, qpos - (np.arange(n_chunks) * CMP_STRIDE + CMP_BLOCK - 1)[None, :])
    o_c, sel = _nsa_sample_cmp(z8, kcvc, bias_c, pos0=pos0, n_cmp=n_cmp, n_sel=n_sel)
    kk = np.arange(PAGE_SIZE)[None, :]
    bias_tail = _bias_table(rel_bias, np.repeat(qpos - (pos0 - PAGE_SIZE + kk), KV_HEADS, axis=1))
    bias_new = _bias_table(rel_bias, qpos - (pos0 + kk))
    bias_far = _bias_table(rel_bias, np.broadcast_to(qpos - (pos0 - 2 * PAGE_SIZE), (TS, PAGE_SIZE)))
    o_s = _nsa_sample_sel(z8, sel, bias_tail, bias_new, bias_far, cache6, page_table, layer, pos0=pos0)
    wb = win6.shape[2]
    wk = np.arange(wb)[None, :]
    bias_w = _bias_table(rel_bias, np.concatenate([qpos - (pos0 - wb + wk), qpos - (pos0 + kk)], axis=1))
    o_nsa = _nsa_sample_win(z8, win6, layer, bias_w, o_c, o_s, lw["nsa_gate_b"], pos0=pos0)
    conv_state8 = jnp.pad(_pad_ff(conv_state), ((0, 0), (SUBLANES - (CONV_W - 1), 0), (0, 0)))
    x2, s5_new, conv_new = _mixer_tail(x2d, z, b, TS, tn, lw, o_mlstm, o_nsa, s5_state, conv_state8, tm=1024)
    nkv = z8[:, :tn, C_NKV:C_NKV + N_KV_SLOTS * KV_HEADS * HEAD_DIM].reshape(b, tn, N_KV_SLOTS, KV_HEADS, HEAD_DIM)
    state = (nkv[:, :, :4], nkv[:, :, 4:], c1, n1, m1[:, :, 0], s5_new[0], s5_new[1], conv_new)
    return x2, state


def kernel(x_prompt, x_sample, cache_nsa_kv, cache_win_kv, state_mlstm_c, state_mlstm_n, state_mlstm_m,
           state_s5_re, state_s5_im, state_ffn_conv, page_table, w_in, mlstm_gate_b, mlstm_norm_w,
           nsa_gate_b, cmp_w1, cmp_b1, cmp_w2, rel_bias, s5_a_re, s5_a_im, s5_b_re, s5_b_im, s5_c_re,
           s5_c_im, s5_d, s5_log_step, s5_glu_w, s5_glu_b, w_out, ln1_w, ln1_b, ffn_w_up, ffn_conv_w,
           ffn_conv_b, ffn_w_down, ln2_w, ln2_b):
    params = dict(w_in=w_in, mlstm_gate_b=mlstm_gate_b, mlstm_norm_w=mlstm_norm_w, nsa_gate_b=nsa_gate_b,
                  cmp_w1=cmp_w1, cmp_b1=cmp_b1, cmp_w2=cmp_w2, s5_a_re=s5_a_re, s5_a_im=s5_a_im, s5_b_re=s5_b_re,
                  s5_b_im=s5_b_im, s5_c_re=s5_c_re, s5_c_im=s5_c_im, s5_d=s5_d, s5_log_step=s5_log_step,
                  s5_glu_w=s5_glu_w, s5_glu_b=s5_glu_b, w_out=w_out, ln1_w=ln1_w, ln1_b=ln1_b, ffn_w_up=ffn_w_up,
                  ffn_conv_w=ffn_conv_w, ffn_conv_b=ffn_conv_b, ffn_w_down=ffn_w_down, ln2_w=ln2_w, ln2_b=ln2_b)
    depth = w_in.shape[0]
    bp, tp, _ = x_prompt.shape
    bs, tn, _ = x_sample.shape
    assert tp % TK == 0 and tn < CMP_STRIDE and tn <= TS and SUBLANES % bp == 0 and SUBLANES % bs == 0
    assert (tn * bs) % SUBLANES == 0 and tp % min(FFN_TT, tp) == 0
    assert page_table.shape[1] % CMP_PAGES == 0 and cache_nsa_kv.shape[2] == PAGE_SIZE
    xp = x_prompt.reshape(bp * tp, D_MODEL)
    xs = jnp.pad(x_sample, ((0, 0), (0, TS - tn), (0, 0))).reshape(bs * TS, D_MODEL)
    p_states, s_states = [], []
    big = _pack_weights(params)
    rows_so_far = None
    for l in range(depth):
        lw = _pack_layer(params, big, l)
        xp, sp, rows_so_far = _prompt_layer(xp, bp, tp, lw, rel_bias, rows_so_far)
        xs, ss = _sample_layer(xs, bs, tn, lw, rel_bias, l, cache_nsa_kv, page_table, cache_win_kv,
                               (state_mlstm_c[l], jnp.pad(state_mlstm_n[l], ((0, 0), (0, 0), (0, LANES - HEAD_DIM))),
                                state_mlstm_m[l]),
                               (state_s5_re[l], state_s5_im[l]), state_ffn_conv[l])
        p_states.append(sp)
        s_states.append(ss)
    stk = lambda states, i: jnp.stack([s[i] for s in states])
    y_prompt = xp.reshape(bp, tp, D_MODEL)
    y_sample = xs.reshape(bs, TS, D_MODEL)[:, :tn]
    wrows = min(WINDOW, tp)
    nsa_kv_p = rows_so_far[0].reshape(depth, bp, tp, 4, KV_HEADS, HEAD_DIM)
    win_kv_p = rows_so_far[1].reshape(depth, bp, tp, 2, KV_HEADS, HEAD_DIM)[:, :, tp - wrows:]
    return (y_prompt, y_sample,
            nsa_kv_p, stk(s_states, 0), win_kv_p, stk(s_states, 1),
            stk(p_states, 2), stk(s_states, 2), stk(p_states, 3), stk(s_states, 3), stk(p_states, 4), stk(s_states, 4),
            stk(p_states, 5), stk(s_states, 5), stk(p_states, 6), stk(s_states, 6), stk(p_states, 7), stk(s_states, 7))
```

```python
import functools
import math

import jax
import jax.numpy as jnp
import numpy as np
from jax import lax
from jax.experimental import pallas as pl
from jax.experimental.pallas import tpu as pltpu

F32 = jnp.float32
BF16 = jnp.bfloat16
NEG_INF = float("-inf")
M_INIT = -1e30
MASKED = -1e30

D_MODEL = 2048
PAGE_SIZE = 128
D_MLSTM = D_MODEL // 4
D_NSA = D_MODEL // 2
D_S5 = D_MODEL - D_MLSTM - D_NSA
HEAD_DIM = 128
MLSTM_HEADS = D_MLSTM // HEAD_DIM
NSA_HEADS = D_NSA // HEAD_DIM
KV_HEADS = 2
GROUP = NSA_HEADS // KV_HEADS
N_KV_SLOTS = 6
CMP_BLOCK = 32
CMP_STRIDE = 16
SEL_BLOCK = 64
SEL_TOPK = 16
FORCE_SCORE = 1e4
WINDOW = 512
S5_GROUP_WIDTH = 16
S5_GROUPS = D_S5 // S5_GROUP_WIDTH
S5_STATE = 64
S5_CH = S5_GROUPS * S5_STATE
NUM_BUCKETS = 32
MAX_DISTANCE = 128
D_FF = ((8 * D_MODEL // 3 + 127) // 128) * 128
CONV_W = 3
LN_EPS = 1e-5
DEPTH = 2
DEEPNORM_ALPHA = (2 * DEPTH) ** 0.25
QK_SCALE = HEAD_DIM ** -0.5

LANES = 128
SUBLANES = 8
V7X_VMEM_LIMIT = 56 * 2 ** 20

C_MQ, C_MK, C_MV, C_MO = 0, 512, 1024, 1536
C_NQ = 2048
C_NKV = 3072
C_SU = 4608
C_GATE = 5120
N_IN = 5376
GATE_MI, GATE_MF, GATE_NG = 0, 4, 8
D_FFP = 5632
MLSTM_L = 256
TQ = 256
TK = 256
S5_LC = 256
FFN_TT = 1024
FFN_TF = 512
FFN_HALO = 16
CMP_PAGES = 32
SEL_PAGES = 32
SEL_TILE = 32
SEL_SHIFT = SEL_BLOCK.bit_length() - 1


def _cparams(*sem):
    return pltpu.CompilerParams(dimension_semantics=sem, vmem_limit_bytes=V7X_VMEM_LIMIT)


def _dot(a, b):
    return jnp.dot(a, b, preferred_element_type=F32)


def _dot_nt(a, b):
    return lax.dot_general(a, b, (((1,), (1,)), ((), ())), preferred_element_type=F32)


def _layer_norm(y, w, b):
    mu = jnp.mean(y, axis=-1, keepdims=True)
    d = y - mu
    var = jnp.mean(d * d, axis=-1, keepdims=True)
    return d * lax.rsqrt(var + LN_EPS) * w + b


def _mm_kernel(x_ref, w_ref, o_ref, *, realign_tile=None):
    acc = _dot(x_ref[...].astype(BF16), w_ref[...])
    if realign_tile is not None:
        acc = _realign(acc, pl.program_id(1) == realign_tile)
    o_ref[...] = acc.astype(o_ref.dtype)


UP_OVER = D_FFP - D_FF


def _up_cols(tf):
    nf, per = D_FFP // tf, tf // LANES
    return lambda j: (j * per + (j >= nf).astype(jnp.int32) * (D_FF // LANES - nf * per)
                      - (j == 2 * nf - 1).astype(jnp.int32) * (UP_OVER // LANES)) * LANES


def _realign(up, is_last):
    shifted = jnp.concatenate([up[:, UP_OVER:], jnp.zeros((up.shape[0], UP_OVER), up.dtype)], axis=1)
    return jnp.where(is_last, shifted, up)


def _matmul(x, w, layer, *, tm, tn, n=None, w_cols=None):
    m, k = x.shape
    n = w.shape[2] if n is None else n
    tm = min(tm, m)
    if w_cols is None:
        w_spec = pl.BlockSpec((None, k, tn), lambda i, j: (layer, 0, j))
    else:
        w_spec = pl.BlockSpec((None, pl.Element(k), pl.Element(tn)), lambda i, j: (layer, 0, w_cols(j)))
    return pl.pallas_call(
        functools.partial(_mm_kernel, realign_tile=None if w_cols is None else n // tn - 1),
        grid=(m // tm, n // tn),
        in_specs=[pl.BlockSpec((tm, k), lambda i, j: (i, 0)), w_spec],
        out_specs=pl.BlockSpec((tm, tn), lambda i, j: (i, j)),
        out_shape=jax.ShapeDtypeStruct((m, n), F32),
        compiler_params=_cparams("parallel", "parallel"),
        name="proj",
    )(x, w)


KV4_GROUPS = 4 * KV_HEADS
WIN_GROUPS = 2 * KV_HEADS


def _proj_in_kernel(x_ref, w_ref, z_ref, kv_ref, win_ref, *, tn):
    j = pl.program_id(1)
    acc = _dot(x_ref[...].astype(BF16), w_ref[...])
    z_ref[...] = acc
    tm = acc.shape[0]
    per_tile = tn // HEAD_DIM
    first_tile = C_NKV // tn
    for tile in range(first_tile, first_tile + N_KV_SLOTS * KV_HEADS // per_tile):
        @pl.when(j == tile)
        def _(tile=tile):
            for c in range(per_tile):
                grp = (tile - first_tile) * per_tile + c
                val = acc[:, c * HEAD_DIM:(c + 1) * HEAD_DIM]
                if grp < KV4_GROUPS:
                    kv_ref[pl.ds(grp, tm, stride=KV4_GROUPS), :] = val
                else:
                    win_ref[pl.ds(grp - KV4_GROUPS, tm, stride=WIN_GROUPS), :] = val


def _proj_in_carry_kernel(x_ref, w_ref, kv_in_ref, win_in_ref, z_ref, kv_ref, win_ref, *, tn):
    del kv_in_ref, win_in_ref
    _proj_in_kernel(x_ref, w_ref, z_ref, kv_ref, win_ref, tn=tn)


def _proj_in(x, w, layer, rows_so_far, *, tm, tn):
    m, k = x.shape
    depth, _, n = w.shape
    nm = m // tm
    assert C_NKV % tn == 0 and (N_KV_SLOTS * KV_HEADS * HEAD_DIM) % tn == 0
    carried = () if rows_so_far is None else tuple(rows_so_far)
    return pl.pallas_call(
        functools.partial(_proj_in_carry_kernel if carried else _proj_in_kernel, tn=tn),
        grid=(nm, n // tn),
        in_specs=[pl.BlockSpec((tm, k), lambda i, j: (i, 0)), pl.BlockSpec((None, k, tn), lambda i, j: (layer, 0, j))]
                 + [pl.BlockSpec(memory_space=pl.ANY)] * len(carried),
        out_specs=[pl.BlockSpec((tm, tn), lambda i, j: (i, j)),
                   pl.BlockSpec((tm * KV4_GROUPS, HEAD_DIM), lambda i, j: (layer * nm + i, 0)),
                   pl.BlockSpec((tm * WIN_GROUPS, HEAD_DIM), lambda i, j: (layer * nm + i, 0))],
        out_shape=[jax.ShapeDtypeStruct((m, n), F32),
                   jax.ShapeDtypeStruct((depth * m * KV4_GROUPS, HEAD_DIM), F32),
                   jax.ShapeDtypeStruct((depth * m * WIN_GROUPS, HEAD_DIM), F32)],
        input_output_aliases={2: 1, 3: 2} if carried else {},
        compiler_params=_cparams("parallel", "arbitrary"),
        name="proj_in",
    )(x, w, *carried)


def _wout_kernel(om_ref, on_ref, os_ref, x_ref, w_ref, lw_ref, lb_ref, o_ref):
    acc = _dot(om_ref[...].astype(BF16), w_ref[0:D_MLSTM, :])
    acc += _dot(on_ref[...].astype(BF16), w_ref[D_MLSTM:D_MLSTM + D_NSA, :])
    acc += _dot(os_ref[...].astype(BF16), w_ref[D_MLSTM + D_NSA:D_MODEL, :])
    o_ref[...] = _layer_norm(DEEPNORM_ALPHA * x_ref[...] + acc, lw_ref[...], lb_ref[...])


def _wout_ln(om, on, os_, x, w, layer, lw, lb, *, tm):
    m = x.shape[0]
    tm = min(tm, m)
    row = lambda width: pl.BlockSpec((tm, width), lambda i: (i, 0))
    full = lambda a: pl.BlockSpec(a.shape, lambda i: (0, 0))
    w_spec = pl.BlockSpec((None,) + w.shape[1:], lambda i: (layer, 0, 0))
    return pl.pallas_call(
        _wout_kernel,
        grid=(m // tm,),
        in_specs=[row(D_MLSTM), row(D_NSA), row(D_S5), row(D_MODEL), w_spec, full(lw), full(lb)],
        out_specs=row(D_MODEL),
        out_shape=jax.ShapeDtypeStruct((m, D_MODEL), F32),
        compiler_params=_cparams("parallel"),
        name="wout_ln",
    )(om, on, os_, x, w, lw, lb)


def _valid_rows(w_ref, tile):
    rows = w_ref.shape[0]
    row = tile * rows + lax.broadcasted_iota(jnp.int32, (rows, 1), 0)
    w = w_ref[...]
    return jnp.where(row < D_FF, w, jnp.zeros_like(w))


def _down_kernel(h_ref, w_ref, x_ref, lw_ref, lb_ref, o_ref, acc_ref, *, nk):
    k = pl.program_id(1)

    @pl.when(k == 0)
    def _():
        acc_ref[...] = jnp.zeros_like(acc_ref)

    acc_ref[...] += _dot(h_ref[...], _valid_rows(w_ref, k))

    @pl.when(k == nk - 1)
    def _():
        o_ref[...] = _layer_norm(DEEPNORM_ALPHA * x_ref[...] + acc_ref[...], lw_ref[...], lb_ref[...])


def _down_ln(h, w, layer, x, lw, lb, *, tm, tk):
    m, kk = h.shape
    tm = min(tm, m)
    nk = kk // tk
    return pl.pallas_call(
        functools.partial(_down_kernel, nk=nk),
        grid=(m // tm, nk),
        in_specs=[
            pl.BlockSpec((tm, tk), lambda i, k: (i, k)),
            pl.BlockSpec((None, tk, D_MODEL), lambda i, k: (layer, k, 0)),
            pl.BlockSpec((tm, D_MODEL), lambda i, k: (i, 0)),
            pl.BlockSpec((1, D_MODEL), lambda i, k: (0, 0)),
            pl.BlockSpec((1, D_MODEL), lambda i, k: (0, 0)),
        ],
        out_specs=pl.BlockSpec((tm, D_MODEL), lambda i, k: (i, 0)),
        out_shape=jax.ShapeDtypeStruct((m, D_MODEL), F32),
        scratch_shapes=[pltpu.VMEM((tm, D_MODEL), F32)],
        compiler_params=_cparams("parallel", "arbitrary"),
        name="down_ln",
    )(h, w, x, lw, lb)


def _convgate_kernel(a_ref, g_ref, sa_ref, sg_ref, wa_ref, wg_ref, ba_ref, bg_ref, o_ref):
    rows = lax.broadcasted_iota(jnp.int32, (1, SUBLANES, 1), 1)

    def conv(cur_ref, st_ref, w_ref, b_ref):
        cur = cur_ref[...]
        prev = st_ref[...]
        p1 = prev[:, 7:8, :]
        p2 = prev[:, 6:7, :]
        x1 = jnp.where(rows == 0, p1, pltpu.roll(cur, 1, 1))
        x2 = jnp.where(rows == 0, p2, jnp.where(rows == 1, p1, pltpu.roll(cur, 2, 1)))
        w = w_ref[...]
        return b_ref[...] + w[0:1, :] * x2 + w[1:2, :] * x1 + w[2:3, :] * cur

    a = conv(a_ref, sa_ref, wa_ref, ba_ref)
    g = conv(g_ref, sg_ref, wg_ref, bg_ref)
    o_ref[...] = (a * jax.nn.sigmoid(a) * g).astype(o_ref.dtype)


def _convgate(up, state8, conv_w8, conv_b, *, tf):
    b, t, _ = up.shape
    assert t == SUBLANES
    nf = D_FFP // tf
    cur_a = pl.BlockSpec((b, t, tf), lambda fi: (0, 0, fi))
    cur_g = pl.BlockSpec((b, t, tf), lambda fi: (0, 0, fi + nf))
    w_a = pl.BlockSpec((SUBLANES, tf), lambda fi: (0, fi))
    w_g = pl.BlockSpec((SUBLANES, tf), lambda fi: (0, fi + nf))
    b_a = pl.BlockSpec((1, tf), lambda fi: (0, fi))
    b_g = pl.BlockSpec((1, tf), lambda fi: (0, fi + nf))
    return pl.pallas_call(
        _convgate_kernel,
        grid=(nf,),
        in_specs=[cur_a, cur_g, cur_a, cur_g, w_a, w_g, b_a, b_g],
        out_specs=pl.BlockSpec((b, t, tf), lambda fi: (0, 0, fi)),
        out_shape=jax.ShapeDtypeStruct((b, t, D_FFP), BF16),
        compiler_params=_cparams("parallel"),
        name="convgate",
    )(up, up, state8, state8, conv_w8, conv_w8, conv_b, conv_b)


def _ffn_kernel(x_ref, xh_ref, wa_ref, wg_ref, wd_ref, cwa_ref, cwg_ref, cba_ref, cbg_ref, lw_ref, lb_ref,
                o_ref, sa_ref, sg_ref, xe_ref, *, nf):
    first = pl.program_id(1) == 0
    f = pl.program_id(2)
    tt = x_ref.shape[0]

    @pl.when(f == 0)
    def _():
        o_ref[...] = jnp.zeros_like(o_ref)
        halo = jnp.where(first, 0.0, xh_ref[...])
        xe_ref[0:FFN_HALO, :] = halo.astype(BF16)
        xe_ref[FFN_HALO:, :] = x_ref[...].astype(BF16)

    xe = xe_ref[...]

    def branch(w_ref, cw_ref, cb_ref, s_ref, second_half):
        up = _dot(xe, w_ref[...])
        if second_half:
            up = _realign(up, f == nf - 1)
        s_ref[...] = up[FFN_HALO + tt - SUBLANES:FFN_HALO + tt, :]
        w = cw_ref[...]
        x1 = pltpu.roll(up, 1, 0)[FFN_HALO:, :]
        x2 = pltpu.roll(up, 2, 0)[FFN_HALO:, :]
        return cb_ref[...] + w[0:1, :] * x2 + w[1:2, :] * x1 + w[2:3, :] * up[FFN_HALO:, :]

    a = branch(wa_ref, cwa_ref, cba_ref, sa_ref, False)
    g = branch(wg_ref, cwg_ref, cbg_ref, sg_ref, True)
    o_ref[...] += _dot((a * jax.nn.sigmoid(a) * g).astype(BF16), _valid_rows(wd_ref, f))

    @pl.when(f == nf - 1)
    def _():
        o_ref[...] = _layer_norm(DEEPNORM_ALPHA * x_ref[...] + o_ref[...], lw_ref[...], lb_ref[...])


def _ffn_prompt(x3, w_up, w_down, layer, conv_w8, conv_b, lw, lb, *, tt, tf):
    b, t, _ = x3.shape
    nf = D_FFP // tf
    tt = min(tt, t)
    hb = tt // FFN_HALO
    half_a = lambda rows: pl.BlockSpec((rows, tf), lambda bi, ti, fi: (0, fi))
    half_g = lambda rows: pl.BlockSpec((rows, tf), lambda bi, ti, fi: (0, fi + nf))
    cols = _up_cols(tf)
    w_a = pl.BlockSpec((None, pl.Element(D_MODEL), pl.Element(tf)), lambda bi, ti, fi: (layer, 0, cols(fi)))
    w_g = pl.BlockSpec((None, pl.Element(D_MODEL), pl.Element(tf)), lambda bi, ti, fi: (layer, 0, cols(fi + nf)))
    vec = pl.BlockSpec((1, D_MODEL), lambda bi, ti, fi: (0, 0))
    tail = pl.BlockSpec((None, SUBLANES, tf), lambda bi, ti, fi: (bi, 0, fi))
    return pl.pallas_call(
        functools.partial(_ffn_kernel, nf=nf),
        grid=(b, t // tt, nf),
        in_specs=[pl.BlockSpec((None, tt, D_MODEL), lambda bi, ti, fi: (bi, ti, 0), pipeline_mode=pl.Buffered(1)),
                  pl.BlockSpec((None, FFN_HALO, D_MODEL), lambda bi, ti, fi: (bi, jnp.maximum(ti * hb - 1, 0), 0)),
                  w_a, w_g,
                  pl.BlockSpec((None, tf, D_MODEL), lambda bi, ti, fi: (layer, fi, 0)),
                  half_a(SUBLANES), half_g(SUBLANES), half_a(1), half_g(1), vec, vec],
        out_specs=[pl.BlockSpec((None, tt, D_MODEL), lambda bi, ti, fi: (bi, ti, 0)), tail, tail],
        out_shape=[jax.ShapeDtypeStruct((b, t, D_MODEL), F32),
                   jax.ShapeDtypeStruct((b, SUBLANES, D_FFP), F32),
                   jax.ShapeDtypeStruct((b, SUBLANES, D_FFP), F32)],
        scratch_shapes=[pltpu.VMEM((tt + FFN_HALO, D_MODEL), BF16)],
        compiler_params=_cparams("parallel", "arbitrary", "arbitrary"),
        name="ffn_prompt",
    )(x3, x3, w_up, w_up, w_down, conv_w8, conv_w8, conv_b, conv_b, lw, lb)


def _cumsum_rows(x):
    n = x.shape[0]
    rows = lax.broadcasted_iota(jnp.int32, (n, 1), 0)
    d = 1
    while d < n:
        x = x + jnp.where(rows >= d, pltpu.roll(x, d, 0), 0.0)
        d *= 2
    return x


def _log_sigmoid(x):
    return jnp.minimum(x, 0.0) - jnp.log1p(jnp.exp(-jnp.abs(x)))


def _mlstm_kernel(q_ref, k_ref, v_ref, og_ref, g_ref, gb_ref, nw_ref, c0_ref, n0_ref, m0_ref,
                  out_ref, c_ref, n_ref, m_ref, *, L, t_valid):
    ci = pl.program_id(1)
    lin = q_ref.shape[0]

    @pl.when(ci == 0)
    def _():
        c_ref[...] = c0_ref[...]
        n_ref[...] = n0_ref[...]
        m_ref[...] = m0_ref[...]

    def rows_of(ref):
        x = ref[...]
        if lin < L:
            x = jnp.concatenate([x, jnp.zeros((L - lin, x.shape[1]), x.dtype)], axis=0)
        return x

    rows = lax.broadcasted_iota(jnp.int32, (L, 1), 0)
    valid = (ci * L + rows) < t_valid
    pre = rows_of(g_ref) + gb_ref[...]
    lf = jnp.where(valid, _log_sigmoid(pre), 0.0)
    ig = jnp.where(valid, pre, NEG_INF)
    bcum = _cumsum_rows(lf)
    dt = (pltpu.roll(ig, GATE_MF - GATE_MI, 1) - bcum).T
    q_all, k_all, v_all, og_all = rows_of(q_ref), rows_of(k_ref), rows_of(v_ref), rows_of(og_ref)
    tri = lax.broadcasted_iota(jnp.int32, (L, L), 0) >= lax.broadcasted_iota(jnp.int32, (L, L), 1)
    nw = nw_ref[...]

    for h in range(MLSTM_HEADS):
        sl = slice(h * HEAD_DIM, (h + 1) * HEAD_DIM)
        q = q_all[:, sl]
        k = k_all[:, sl] * QK_SCALE
        v = v_all[:, sl]
        qb, kb, vb = q.astype(BF16), k.astype(BF16), v.astype(BF16)
        b_col = bcum[:, GATE_MF + h:GATE_MF + h + 1]
        ig_col = ig[:, GATE_MI + h:GATE_MI + h + 1]
        d_row = dt[GATE_MF + h:GATE_MF + h + 1, :]
        c_prev = c_ref[h]
        n_prev = n_ref[h:h + 1, :]
        m_prev = m_ref[h:h + 1, 0:1]

        dmat = jnp.where(tri, b_col + d_row, NEG_INF)
        g_col = b_col + m_prev
        m_row = jnp.maximum(jnp.max(dmat, axis=1, keepdims=True), g_col)
        a = jnp.exp(dmat - m_row) * _dot_nt(qb, kb)
        w_inter = jnp.exp(g_col - m_row)
        num = _dot(a.astype(BF16), vb) + w_inter * _dot(qb, c_prev.astype(BF16))
        den = jnp.sum(a, axis=1, keepdims=True) + w_inter * jnp.sum(q * n_prev, axis=1, keepdims=True)
        hid = num / jnp.maximum(jnp.abs(den), jnp.exp(-m_row))
        mu = jnp.mean(hid, axis=1, keepdims=True)
        dlt = hid - mu
        var = jnp.mean(dlt * dlt, axis=1, keepdims=True)
        hn = dlt * lax.rsqrt(var + LN_EPS) * nw[:, sl]
        res = jax.nn.sigmoid(og_all[:, sl]) * hn
        out_ref[:, sl] = res[0:lin, :]

        f_tot = b_col[L - 1:L, :]
        w_s = f_tot - b_col + ig_col
        m_new = jnp.maximum(f_tot + m_prev, jnp.max(w_s, axis=0, keepdims=True))
        ws = jnp.exp(w_s - m_new)
        decay = jnp.exp(f_tot + m_prev - m_new)
        c_ref[h] = decay * c_prev + _dot(k.T.astype(BF16), (ws * v).astype(BF16))
        n_ref[h:h + 1, :] = decay * n_prev + jnp.sum(ws * k, axis=0, keepdims=True)
        m_ref[h:h + 1, :] = jnp.broadcast_to(m_new, (1, LANES))


def _mlstm(z, gate_b, norm_w, c0, n0, m0, *, L, lin, t_valid):
    b, tz, _ = z.shape
    nchunks = tz // lin
    qblk = lambda col: pl.BlockSpec((None, lin, D_MLSTM), lambda bi, ci: (bi, ci, col // D_MLSTM))
    st4 = pl.BlockSpec((None, MLSTM_HEADS, HEAD_DIM, HEAD_DIM), lambda bi, ci: (bi, 0, 0, 0))
    st3 = pl.BlockSpec((None, MLSTM_HEADS, LANES), lambda bi, ci: (bi, 0, 0))
    return pl.pallas_call(
        functools.partial(_mlstm_kernel, L=L, t_valid=t_valid),
        grid=(b, nchunks),
        in_specs=[qblk(C_MQ), qblk(C_MK), qblk(C_MV), qblk(C_MO),
                  pl.BlockSpec((None, lin, LANES), lambda bi, ci: (bi, ci, C_GATE // LANES)),
                  pl.BlockSpec((1, LANES), lambda bi, ci: (0, 0)),
                  pl.BlockSpec((1, D_MLSTM), lambda bi, ci: (0, 0)),
                  st4, st3, st3],
        out_specs=[pl.BlockSpec((None, lin, D_MLSTM), lambda bi, ci: (bi, ci, 0)), st4, st3, st3],
        out_shape=[jax.ShapeDtypeStruct((b, tz, D_MLSTM), F32),
                   jax.ShapeDtypeStruct((b, MLSTM_HEADS, HEAD_DIM, HEAD_DIM), F32),
                   jax.ShapeDtypeStruct((b, MLSTM_HEADS, LANES), F32),
                   jax.ShapeDtypeStruct((b, MLSTM_HEADS, LANES), F32)],
        compiler_params=_cparams("parallel", "arbitrary"),
        name="mlstm",
    )(z, z, z, z, z, gate_b, norm_w, c0, n0, m0)


S5_IN_TILES = S5_CH // 256
S5_IN_FEATS = D_S5 // S5_IN_TILES
S5_OUT_TILES = D_S5 // 256
S5_OUT_CH = S5_CH // S5_OUT_TILES


S5_LT = S5_CH // LANES


def _lane_tiles(x):
    return jnp.stack([x[:, k * LANES:(k + 1) * LANES] for k in range(x.shape[1] // LANES)])


def _lane_untile(x):
    return jnp.concatenate([x[k] for k in range(x.shape[0])], axis=1)


def _s5_kernel(u_ref, wb_ref, wc_ref, lr_ref, li_ref, d_ref, gw_ref, gb_ref, x0r_ref, x0i_ref,
               o_ref, xr_ref, xi_ref, sr_ref, si_ref, y_ref, *, steps):
    nb, lc, _ = u_ref.shape

    @pl.when(pl.program_id(0) == 0)
    def _():
        xr_ref[...] = x0r_ref[...]
        xi_ref[...] = x0i_ref[...]

    for b in range(nb):
        ub = u_ref[b].astype(BF16)
        for c in range(S5_IN_TILES):
            bu = _dot(ub[:, c * S5_IN_FEATS:(c + 1) * S5_IN_FEATS], wb_ref[c])
            for k in range(256 // LANES):
                rows = pl.ds(b, lc, stride=nb)
                sr_ref.at[c * (256 // LANES) + k][rows, :] = bu[:, k * LANES:(k + 1) * LANES]
                si_ref.at[c * (256 // LANES) + k][rows, :] = bu[:, 256 + k * LANES:256 + (k + 1) * LANES]
    lam_r = _lane_tiles(jnp.broadcast_to(lr_ref[...], (SUBLANES, S5_CH)))
    lam_i = _lane_tiles(jnp.broadcast_to(li_ref[...], (SUBLANES, S5_CH)))
    row8 = lax.broadcasted_iota(jnp.int32, (1, SUBLANES, 1), 1)
    per_group = SUBLANES // nb

    def group(i, carry):
        xr, xi = carry
        r0 = pl.multiple_of(i * SUBLANES, SUBLANES)
        br = sr_ref[:, pl.ds(r0, SUBLANES), :]
        bi = si_ref[:, pl.ds(r0, SUBLANES), :]
        out_r, out_i = br, bi
        for k in range(per_group):
            nr = lam_r * xr - lam_i * xi + br
            ni = lam_r * xi + lam_i * xr + bi
            here = (row8 >= k * nb) & (row8 < (k + 1) * nb)
            out_r = jnp.where(here, nr, out_r)
            out_i = jnp.where(here, ni, out_i)
            if per_group > 1:
                xr = pltpu.roll(nr, nb, 1)
                xi = pltpu.roll(ni, nb, 1)
            else:
                xr, xi = nr, ni
        sr_ref[:, pl.ds(r0, SUBLANES), :] = out_r
        si_ref[:, pl.ds(r0, SUBLANES), :] = out_i
        return xr, xi

    xr, xi = lax.fori_loop(0, steps * nb // SUBLANES, group, (_lane_tiles(xr_ref[...]), _lane_tiles(xi_ref[...])))
    xr_ref[...] = _lane_untile(xr)
    xi_ref[...] = _lane_untile(xi)
    per_out = S5_OUT_CH // LANES
    for j in range(S5_OUT_TILES):
        lhs_r = jnp.concatenate([sr_ref[j * per_out + k] for k in range(per_out)], axis=1).astype(BF16)
        lhs_i = jnp.concatenate([si_ref[j * per_out + k] for k in range(per_out)], axis=1).astype(BF16)
        y = _dot(lhs_r, wc_ref[0, j]) + _dot(lhs_i, wc_ref[1, j])
        for k in range(256 // LANES):
            y_ref[j * (256 // LANES) + k] = y[:, k * LANES:(k + 1) * LANES]
    for b in range(nb):
        rows = pl.ds(b, lc, stride=nb)
        y = jnp.concatenate([y_ref.at[k][rows, :] for k in range(D_S5 // LANES)], axis=1) + d_ref[...] * u_ref[b]
        zz = jax.nn.gelu(y)
        o_ref[b] = zz * jax.nn.sigmoid(_dot(zz.astype(BF16), gw_ref[...]) + gb_ref[...])


def _s5(z3, wb, wc, lam_r, lam_i, d, glu_w, glu_b, x0r, x0i, *, lc, steps):
    b, tz, _ = z3.shape
    lc = min(lc, tz)
    steps = min(steps, lc)
    assert steps == lc or tz == lc
    full = lambda a: pl.BlockSpec(a.shape, lambda i: (0,) * a.ndim)
    st = pl.BlockSpec((SUBLANES, S5_CH), lambda i: (0, 0))
    return pl.pallas_call(
        functools.partial(_s5_kernel, steps=steps),
        grid=(tz // lc,),
        in_specs=[pl.BlockSpec((b, lc, D_S5), lambda i: (0, i, C_SU // D_S5)),
                  full(wb), full(wc), full(lam_r), full(lam_i), full(d), full(glu_w), full(glu_b), st, st],
        out_specs=[pl.BlockSpec((b, lc, D_S5), lambda i: (0, i, 0)), st, st],
        out_shape=[jax.ShapeDtypeStruct((b, tz, D_S5), F32),
                   jax.ShapeDtypeStruct((SUBLANES, S5_CH), F32),
                   jax.ShapeDtypeStruct((SUBLANES, S5_CH), F32)],
        scratch_shapes=[pltpu.VMEM((S5_LT, lc * b, LANES), F32), pltpu.VMEM((S5_LT, lc * b, LANES), F32),
                        pltpu.VMEM((D_S5 // LANES, lc * b, LANES), F32)],
        compiler_params=_cparams("arbitrary"),
        name="s5",
    )(z3, wb, wc, lam_r, lam_i, d, glu_w, glu_b, x0r, x0i)


N_CMB = 2 * KV_HEADS


def _cmp_project(rows_of, w_ref, o_ref):
    nch = o_ref.shape[0]
    for cmb in range(N_CMB):
        slot = cmb // KV_HEADS
        rows_ref = rows_of(cmb)
        acc = jnp.zeros((nch, 2 * HEAD_DIM), F32)
        for l in range(CMP_STRIDE):
            x = rows_ref[pl.ds(l, nch, stride=CMP_STRIDE), :]
            acc += _dot(x.astype(BF16), w_ref[slot, l])
        o_ref[:, cmb * 2 * HEAD_DIM:(cmb + 1) * 2 * HEAD_DIM] = acc


def _cmp_p_kernel(r0_ref, r1_ref, r2_ref, r3_ref, w_ref, o_ref):
    rows = (r0_ref, r1_ref, r2_ref, r3_ref)
    _cmp_project(lambda cmb: rows[cmb], w_ref, o_ref)


def _cmp_project_prompt(z, wcat):
    b, t, _ = z.shape
    nch = t // CMP_STRIDE
    rows = lambda cmb: pl.BlockSpec((None, t, HEAD_DIM), lambda bi: (bi, 0, C_NKV // HEAD_DIM + cmb))
    return pl.pallas_call(
        _cmp_p_kernel,
        grid=(b,),
        in_specs=[rows(cmb) for cmb in range(N_CMB)] + [pl.BlockSpec(wcat.shape, lambda bi: (0, 0, 0, 0))],
        out_specs=pl.BlockSpec((None, nch, N_CMB * 2 * HEAD_DIM), lambda bi: (bi, 0, 0)),
        out_shape=jax.ShapeDtypeStruct((b, nch, N_CMB * 2 * HEAD_DIM), F32),
        compiler_params=_cparams("parallel"),
        name="cmp_project_prompt",
    )(z, z, z, z, wcat)


def _cmp_s_kernel(pt_ref, *refs):
    del pt_ref
    per_pos = 2 * KV_HEADS
    page_refs = [r.reshape(PAGE_SIZE * per_pos, HEAD_DIM) for r in refs[:CMP_PAGES]]
    w_ref, o_ref = refs[CMP_PAGES:]
    per_page = PAGE_SIZE // CMP_STRIDE
    for cmb in range(N_CMB):
        acc = jnp.zeros((o_ref.shape[0], 2 * HEAD_DIM), F32)
        for l in range(CMP_STRIDE):
            x = jnp.concatenate(
                [page_refs[p][pl.ds(l * per_pos + cmb, per_page, stride=CMP_STRIDE * per_pos), :]
                 for p in range(CMP_PAGES)], axis=0)
            acc += _dot(x.astype(BF16), w_ref[cmb // KV_HEADS, l])
        o_ref[:, cmb * 2 * HEAD_DIM:(cmb + 1) * 2 * HEAD_DIM] = acc


def _page_spec(layer, slot, p, pages_per_step, grid_rank):
    def index_map(*idx):
        bi, si, pt = idx[0], idx[grid_rank - 1], idx[grid_rank]
        return (layer, pt[bi, si * pages_per_step + p], 0, slot, 0, 0)

    return pl.BlockSpec((None, None, PAGE_SIZE, None, KV_HEADS, HEAD_DIM), index_map)


def _cmp_project_sample(cache6, page_table, layer, wcat):
    b, npages = page_table.shape
    steps = npages // CMP_PAGES
    nch = CMP_PAGES * PAGE_SIZE // CMP_STRIDE
    def page_pair(p):
        return pl.BlockSpec((None, None, PAGE_SIZE, 2, KV_HEADS, HEAD_DIM),
                            lambda bi, si, pt: (layer, pt[bi, si * CMP_PAGES + p], 0, 0, 0, 0))

    pages = [page_pair(p) for p in range(CMP_PAGES)]

    grid_spec = pltpu.PrefetchScalarGridSpec(
        num_scalar_prefetch=1,
        grid=(b, steps),
        in_specs=pages + [pl.BlockSpec(wcat.shape, lambda bi, si, pt: (0, 0, 0, 0))],
        out_specs=pl.BlockSpec((None, nch, N_CMB * 2 * HEAD_DIM), lambda bi, si, pt: (bi, si, 0)),
    )
    return pl.pallas_call(
        _cmp_s_kernel,
        grid_spec=grid_spec,
        out_shape=jax.ShapeDtypeStruct((b, steps * nch, N_CMB * 2 * HEAD_DIM), F32),
        compiler_params=_cparams("parallel", "arbitrary"),
        name="cmp_project_sample",
    )(page_table, *([cache6] * CMP_PAGES), wcat)


def _cmp_fin_kernel(p_ref, b1_ref, w2_ref, o_ref):
    nch = p_ref.shape[0]
    for cmb in range(N_CMB):
        slot = cmb // KV_HEADS
        c0 = cmb * 2 * HEAD_DIM
        first = p_ref[:, c0:c0 + HEAD_DIM]
        second = pltpu.roll(p_ref[:, c0 + HEAD_DIM:c0 + 2 * HEAD_DIM], nch - 1, 0)
        hid = b1_ref[slot:slot + 1, :] + first + second
        o_ref[cmb] = _dot(jax.nn.gelu(hid).astype(BF16), w2_ref[slot])


def _cmp_finish(p, b1, w2):
    b, nch, _ = p.shape
    return pl.pallas_call(
        _cmp_fin_kernel,
        grid=(b,),
        in_specs=[pl.BlockSpec((None, nch, p.shape[2]), lambda bi: (bi, 0, 0)),
                  pl.BlockSpec(b1.shape, lambda bi: (0, 0)),
                  pl.BlockSpec(w2.shape, lambda bi: (0, 0, 0))],
        out_specs=pl.BlockSpec((None, N_CMB, nch, HEAD_DIM), lambda bi: (bi, 0, 0, 0)),
        out_shape=jax.ShapeDtypeStruct((b, N_CMB, nch, HEAD_DIM), F32),
        compiler_params=_cparams("parallel"),
        name="cmp_finish",
    )(p, b1, w2)


def _softmax_init(m_scr, l_scr, acc_scr):
    m_scr[...] = jnp.full(m_scr.shape, M_INIT, F32)
    l_scr[...] = jnp.zeros(l_scr.shape, F32)
    acc_scr[...] = jnp.zeros(acc_scr.shape, F32)


def _softmax_update(s, mask, vb, m_scr, l_scr, acc_scr):
    s = jnp.where(mask, s, NEG_INF)
    m_prev = m_scr[...]
    m_new = jnp.maximum(m_prev, jnp.max(s, axis=1, keepdims=True))
    alpha = jnp.exp(m_prev - m_new)
    p = jnp.exp(s - m_new)
    l_scr[...] = alpha * l_scr[...] + jnp.sum(p, axis=1, keepdims=True)
    acc_scr[...] = alpha * acc_scr[...] + _dot(p.astype(BF16), vb)
    m_scr[...] = m_new


def _softmax_result(l_scr, acc_scr):
    l = l_scr[...]
    return acc_scr[...] / jnp.where(l > 0, l, 1.0)


def _masked_probs(s, mask):
    s = jnp.where(mask, s, NEG_INF)
    m = jnp.max(s, axis=1, keepdims=True)
    m = jnp.where(m > NEG_INF, m, 0.0)
    p = jnp.exp(s - m)
    den = jnp.sum(p, axis=1, keepdims=True)
    return p / jnp.where(den > 0, den, 1.0)


def _stack_heads(q):
    return jnp.concatenate([q[:, j * HEAD_DIM:(j + 1) * HEAD_DIM] for j in range(GROUP)], axis=0)


def _block_importance(pc, n_cmp, n_sel):
    ncp = pc.shape[1]
    nsp = -(-n_sel // LANES) * LANES
    c_start = lax.broadcasted_iota(jnp.int32, (ncp, nsp), 0) * CMP_STRIDE
    s_start = lax.broadcasted_iota(jnp.int32, (ncp, nsp), 1) * SEL_BLOCK
    overlap = ((c_start < s_start + SEL_BLOCK) & (c_start + CMP_BLOCK > s_start)
               & (c_start < n_cmp * CMP_STRIDE)).astype(F32)
    return jnp.dot(pc, overlap, preferred_element_type=F32, precision=lax.Precision.HIGHEST)


def _force_blocks(imp, blk, cur):
    forced = (blk == 0) | (blk == cur) | (blk == cur - 1)
    imp = jnp.where(forced, FORCE_SCORE, imp)
    return jnp.where(blk > cur, -1.0, imp)


def _select_blocks_cols(pc, cur, n_cmp, n_sel):
    imp = _block_importance(pc, n_cmp, n_sel)
    blk = lax.broadcasted_iota(jnp.int32, (1, imp.shape[1]), 1)
    imp = _force_blocks(imp, blk, cur)

    def body(sp, count):
        col = jnp.sum(jnp.where(blk == sp, imp, 0.0), axis=1, keepdims=True)
        ahead = (col > imp) | ((col == imp) & (sp < blk))
        return count + ahead.astype(F32)

    count = lax.fori_loop(0, n_sel, body, jnp.zeros(imp.shape, F32), unroll=64)
    return ((count < min(SEL_TOPK, n_sel)) & (blk < n_sel)).astype(F32)


def _select_blocks_rows(pc, cur_row, n_cmp, n_sel):
    rows = pc.shape[0]
    nrow = -(-n_sel // SUBLANES) * SUBLANES
    imp_t = _block_importance(pc, n_cmp, n_sel).T[0:nrow, :]
    blk = lax.broadcasted_iota(jnp.int32, (nrow, 1), 0)
    imp_t = _force_blocks(imp_t, blk, cur_row)
    count = jnp.zeros(imp_t.shape, F32)
    for sp in range(n_sel):
        row = imp_t[sp:sp + 1, :]
        count += ((row > imp_t) | ((row == imp_t) & (sp < blk))).astype(F32)
    sel_t = ((count < min(SEL_TOPK, n_sel)) & (blk < n_sel)).astype(F32)
    sel_t = jnp.concatenate([sel_t, jnp.zeros((LANES - nrow, rows), F32)], axis=0)
    return sel_t.T


def _gate(gt, col):
    lane = lax.broadcasted_iota(jnp.int32, (1, LANES), 1)
    return jax.nn.sigmoid(jnp.sum(jnp.where(lane == col, gt, 0.0), axis=1, keepdims=True))


def _nsa_prompt_kernel(q_ref, ks_ref, vs_ref, kw_ref, vw_ref, kc_ref, vc_ref, bc_ref, tz_ref, gt_ref, gb_ref,
                       o_ref, s_scr, selb_scr, mx_scr, l_scr, acc_scr, *, t_len):
    g = pl.program_id(1)
    qt = pl.program_id(2)
    q0 = qt * TQ
    n_cmp = t_len // CMP_STRIDE - (CMP_BLOCK // CMP_STRIDE) + 1
    n_sel = t_len // SEL_BLOCK
    ncp = kc_ref.shape[0]
    rows4 = GROUP * TQ
    qs = _stack_heads(q_ref[...] * QK_SCALE).astype(BF16)
    qpos = q0 + (lax.broadcasted_iota(jnp.int32, (rows4, 1), 0) & (TQ - 1))

    n_idx = lax.broadcasted_iota(jnp.int32, (1, ncp), 1)
    s_c = _dot_nt(qs, kc_ref[...].astype(BF16)) + bc_ref[...].reshape(rows4, ncp)
    cmask = (qpos - (n_idx * CMP_STRIDE + CMP_BLOCK - 1) >= 0) & (n_idx < n_cmp)
    p_c = _masked_probs(s_c, cmask)
    o_c = _dot(p_c.astype(BF16), vc_ref[...].astype(BF16))
    pc = p_c[0:TQ] + p_c[TQ:2 * TQ] + p_c[2 * TQ:3 * TQ] + p_c[3 * TQ:4 * TQ]
    cur_row = jnp.right_shift(q0 + lax.broadcasted_iota(jnp.int32, (1, TQ), 1), SEL_SHIFT)
    sel = _select_blocks_rows(pc, cur_row, n_cmp, n_sel)
    sel_add = ((sel - 1.0) * -MASKED).astype(BF16)

    kt_hi = (q0 + TQ - 1) // TK + 1
    kt_far = jnp.maximum(q0 - (FAR_DIST - 1), 0) // TK
    kidx = lax.broadcasted_iota(jnp.int32, (1, TK), 1)
    srow = lax.broadcasted_iota(jnp.int32, (LANES, TK), 0)

    def spread_selection(kt, carry):
        k0 = kt * TK
        expand = (jnp.right_shift(k0 + lax.broadcasted_iota(jnp.int32, (LANES, TK), 1), SEL_SHIFT) == srow).astype(BF16)
        selb_scr[kt] = _dot(sel_add, expand)
        return carry

    lax.fori_loop(0, kt_hi, spread_selection, 0)

    def tile_bias(k0):
        return tz_ref[jnp.clip((q0 - k0) // TQ, 0, N_TZ - 1)].reshape(rows4, TK)

    def attend(k_ref, v_ref, kt_lo, mask_tile):
        mx_scr[...] = jnp.full(mx_scr.shape, NEG_INF, F32)

        def scores(near):
            def body(kt, carry):
                k0 = pl.multiple_of(kt * TK, TK)
                s = _dot_nt(qs, k_ref[pl.ds(k0, TK), :].astype(BF16))
                if near:
                    s = s + tile_bias(k0)
                s = mask_tile(k0, s, near)
                s_scr[kt] = s
                mx_scr[...] = jnp.maximum(mx_scr[...], jnp.maximum(s[:, 0:LANES], s[:, LANES:TK]))
                return carry
            return body

        kt_mid = jnp.maximum(kt_far, kt_lo)
        lax.fori_loop(kt_lo, kt_mid, scores(False), 0)
        lax.fori_loop(kt_mid, kt_hi, scores(True), 0)
        m = jnp.max(mx_scr[...], axis=1, keepdims=True)
        mx_scr[...] = jnp.broadcast_to(jnp.where(m > NEG_INF, m, 0.0), mx_scr.shape)
        l_scr[...] = jnp.zeros(l_scr.shape, F32)
        acc_scr[...] = jnp.zeros(acc_scr.shape, F32)

        def probs(kt, carry):
            k0 = pl.multiple_of(kt * TK, TK)
            m_rep = mx_scr[...]
            p = jnp.exp(s_scr[kt] - jnp.concatenate([m_rep, m_rep], axis=1))
            l_scr[...] += p[:, 0:LANES] + p[:, LANES:TK]
            acc_scr[...] += _dot(p.astype(BF16), v_ref[pl.ds(k0, TK), :].astype(BF16))
            return carry

        lax.fori_loop(kt_lo, kt_hi, probs, 0)
        l = jnp.sum(l_scr[...], axis=1, keepdims=True)
        return acc_scr[...] / jnp.where(l > 0, l, 1.0)

    def sel_mask(k0, s, near):
        s = (s.reshape(GROUP, TQ, TK) + selb_scr[k0 // TK][None]).reshape(rows4, TK)
        return jnp.where(k0 + kidx <= qpos, s, NEG_INF) if near else s

    o_s = attend(ks_ref, vs_ref, 0, sel_mask)

    def win_mask(k0, s, near):
        dist = qpos - (k0 + kidx)
        return jnp.where((dist >= 0) & (dist <= WINDOW), s, NEG_INF)

    o_w = attend(kw_ref, vw_ref, jnp.maximum(q0 - WINDOW, 0) // TK, win_mask)

    gt = gt_ref[...] + gb_ref[...]
    for j in range(GROUP):
        col = GATE_NG + (g * GROUP + j) * 3
        r = slice(j * TQ, (j + 1) * TQ)
        o_ref[:, j * HEAD_DIM:(j + 1) * HEAD_DIM] = (
            _gate(gt, col) * o_c[r] + _gate(gt, col + 1) * o_s[r] + _gate(gt, col + 2) * o_w[r])


def _nsa_prompt(z, kcvc, bias_c, tz, gate_b):
    b, t, _ = z.shape
    ncp = kcvc.shape[2]
    nq = t // TQ
    kv = lambda slot: pl.BlockSpec((None, t, HEAD_DIM), lambda bi, g, qi: (bi, 0, C_NKV // HEAD_DIM + slot * KV_HEADS + g))
    cmp_blk = lambda slot: pl.BlockSpec((None, None, ncp, HEAD_DIM), lambda bi, g, qi: (bi, slot * KV_HEADS + g, 0, 0))
    return pl.pallas_call(
        functools.partial(_nsa_prompt_kernel, t_len=t),
        grid=(b, KV_HEADS, nq),
        in_specs=[pl.BlockSpec((None, TQ, GROUP * HEAD_DIM), lambda bi, g, qi: (bi, qi, C_NQ // (GROUP * HEAD_DIM) + g)),
                  kv(2), kv(3), kv(4), kv(5), cmp_blk(0), cmp_blk(1),
                  pl.BlockSpec((GROUP, TQ, ncp), lambda bi, g, qi: (g, qi, 0)),
                  pl.BlockSpec((N_TZ, GROUP, TQ, TK), lambda bi, g, qi: (0, g, 0, 0)),
                  pl.BlockSpec((None, TQ, LANES), lambda bi, g, qi: (bi, qi, C_GATE // LANES)),
                  pl.BlockSpec((1, LANES), lambda bi, g, qi: (0, 0))],
        out_specs=pl.BlockSpec((None, TQ, GROUP * HEAD_DIM), lambda bi, g, qi: (bi, qi, g)),
        out_shape=jax.ShapeDtypeStruct((b, t, D_NSA), F32),
        scratch_shapes=[pltpu.VMEM((t // TK, GROUP * TQ, TK), F32), pltpu.VMEM((t // TK, TQ, TK), F32),
                        pltpu.VMEM((GROUP * TQ, LANES), F32),
                        pltpu.VMEM((GROUP * TQ, LANES), F32), pltpu.VMEM((GROUP * TQ, HEAD_DIM), F32)],
        compiler_params=_cparams("parallel", "parallel", "arbitrary"),
        name="nsa_prompt",
    )(z, z, z, z, z, kcvc, kcvc, bias_c, tz, z, gate_b)


TS = SUBLANES
ROWS_S = GROUP * TS


def _nsa_s_cmp_kernel(q_ref, kc_ref, vc_ref, bc_ref, oc_ref, sel_ref, *, pos0, n_cmp, n_sel):
    nch = kc_ref.shape[0]
    qs = _stack_heads(q_ref[...] * QK_SCALE).astype(BF16)
    qpos = pos0 + (lax.broadcasted_iota(jnp.int32, (ROWS_S, 1), 0) & (TS - 1))
    n_idx = lax.broadcasted_iota(jnp.int32, (1, nch), 1)
    s_c = _dot_nt(qs, kc_ref[...].astype(BF16)) + bc_ref[...].reshape(ROWS_S, nch)
    cmask = (qpos - (n_idx * CMP_STRIDE + CMP_BLOCK - 1) >= 0) & (n_idx < n_cmp)
    p_c = _masked_probs(s_c, cmask)
    oc_ref[...] = _dot(p_c.astype(BF16), vc_ref[...].astype(BF16))
    pc = p_c[0:TS] + p_c[TS:2 * TS] + p_c[2 * TS:3 * TS] + p_c[3 * TS:4 * TS]
    sel_ref[...] = _select_blocks_cols(pc, jnp.right_shift(qpos[0:TS], SEL_SHIFT), n_cmp, n_sel)


def _nsa_sample_cmp(z8, kcvc, bias_c, *, pos0, n_cmp, n_sel):
    b = z8.shape[0]
    nch = kcvc.shape[2]
    nsp = -(-n_sel // LANES) * LANES
    cmp_blk = lambda slot: pl.BlockSpec((None, None, nch, HEAD_DIM), lambda bi, g: (bi, slot * KV_HEADS + g, 0, 0))
    return pl.pallas_call(
        functools.partial(_nsa_s_cmp_kernel, pos0=pos0, n_cmp=n_cmp, n_sel=n_sel),
        grid=(b, KV_HEADS),
        in_specs=[pl.BlockSpec((None, TS, GROUP * HEAD_DIM), lambda bi, g: (bi, 0, C_NQ // (GROUP * HEAD_DIM) + g)),
                  cmp_blk(0), cmp_blk(1),
                  pl.BlockSpec((GROUP, TS, nch), lambda bi, g: (g, 0, 0))],
        out_specs=[pl.BlockSpec((None, None, ROWS_S, HEAD_DIM), lambda bi, g: (bi, g, 0, 0)),
                   pl.BlockSpec((None, None, TS, nsp), lambda bi, g: (bi, g, 0, 0))],
        out_shape=[jax.ShapeDtypeStruct((b, KV_HEADS, ROWS_S, HEAD_DIM), F32),
                   jax.ShapeDtypeStruct((b, KV_HEADS, TS, nsp), F32)],
        compiler_params=_cparams("parallel", "parallel"),
        name="nsa_sample_cmp",
    )(z8, kcvc, kcvc, bias_c)


def _pad_rows(x, n):
    return jnp.concatenate([x, jnp.zeros((n - x.shape[0], x.shape[1]), x.dtype)], axis=0)


def _nsa_s_sel_kernel(pt_ref, q_ref, sel_ref, kn_ref, vn_ref, bt_ref, bn_ref, bf_ref, *refs, pos0, npages):
    del pt_ref
    rows_pp = PAGE_SIZE * KV_HEADS
    k_refs = [r.reshape(rows_pp, HEAD_DIM) for r in refs[:SEL_PAGES]]
    v_refs = [r.reshape(rows_pp, HEAD_DIM) for r in refs[SEL_PAGES:2 * SEL_PAGES]]
    o_ref, kbuf, vbuf, m_scr, l_scr, acc_scr = refs[2 * SEL_PAGES:]
    step = pl.program_id(1)
    last = step == npages // SEL_PAGES - 1
    rows = KV_HEADS * ROWS_S
    ncol = SEL_TILE * rows_pp
    head_shift = KV_HEADS.bit_length() - 1
    blk_per_page = PAGE_SIZE // SEL_BLOCK

    @pl.when(step == 0)
    def _():
        _softmax_init(m_scr, l_scr, acc_scr)

    nsp = sel_ref.shape[-1]
    ridx = lax.broadcasted_iota(jnp.int32, (rows, 1), 0)
    qpos = pos0 + (ridx & (TS - 1))
    qs = jnp.concatenate([_stack_heads(q_ref[:, g * GROUP * HEAD_DIM:(g + 1) * GROUP * HEAD_DIM] * QK_SCALE)
                          for g in range(KV_HEADS)], axis=0).astype(BF16)
    sel_rows = jnp.concatenate([sel_ref[g] for g in range(KV_HEADS) for _ in range(GROUP)], axis=0)
    bias_far = bf_ref[...].reshape(rows, PAGE_SIZE)[:, 0:1]
    col = lax.broadcasted_iota(jnp.int32, (1, ncol), 1)
    own_head = (col & (KV_HEADS - 1)) == ridx // ROWS_S
    expand = (jnp.right_shift(lax.broadcasted_iota(jnp.int32, (LANES, ncol), 1), SEL_SHIFT + head_shift)
              == lax.broadcasted_iota(jnp.int32, (LANES, ncol), 0)).astype(BF16)
    blk_r = lax.broadcasted_iota(jnp.int32, (nsp, LANES), 0)
    blk_c = lax.broadcasted_iota(jnp.int32, (nsp, LANES), 1)
    n_tiles = SEL_PAGES // SEL_TILE

    for tile in range(n_tiles):
        for i in range(SEL_TILE):
            p = tile * SEL_TILE + i
            kbuf[i * rows_pp:(i + 1) * rows_pp, :] = k_refs[p][...].astype(BF16)
            vbuf[i * rows_pp:(i + 1) * rows_pp, :] = v_refs[p][...].astype(BF16)
        page0 = step * SEL_PAGES + tile * SEL_TILE
        window = ((blk_r == page0 * blk_per_page + blk_c) & (blk_c < SEL_TILE * blk_per_page)).astype(BF16)
        chosen = _dot(_dot(sel_rows.astype(BF16), window).astype(BF16), expand) > 0.5
        s = _dot_nt(qs, kbuf[...]) + bias_far
        if tile == n_tiles - 1:
            tail = jnp.where(last, bt_ref[...].reshape(rows, rows_pp) - bias_far, 0.0)
            s = s + jnp.concatenate([jnp.zeros((rows, ncol - rows_pp), F32), tail], axis=1)
        key = page0 * PAGE_SIZE + jnp.right_shift(col, head_shift)
        _softmax_update(s, chosen & own_head & (key <= qpos), vbuf[...], m_scr, l_scr, acc_scr)

    @pl.when(last)
    def _():
        kidx = lax.broadcasted_iota(jnp.int32, (1, PAGE_SIZE), 1)
        new_blk = lax.broadcasted_iota(jnp.int32, (1, nsp), 1) == npages * blk_per_page
        flag = jnp.sum(jnp.where(new_blk, sel_rows, 0.0), axis=1, keepdims=True) > 0.5
        bias_new = bn_ref[...].reshape(rows, PAGE_SIZE)
        for g in range(KV_HEADS):
            r = slice(g * ROWS_S, (g + 1) * ROWS_S)
            kn = _pad_rows(kn_ref[:, g * HEAD_DIM:(g + 1) * HEAD_DIM], PAGE_SIZE).astype(BF16)
            vn = _pad_rows(vn_ref[:, g * HEAD_DIM:(g + 1) * HEAD_DIM], PAGE_SIZE).astype(BF16)
            mask = flag[r] & (pos0 + kidx <= qpos[r]) & (kidx < TS)
            _softmax_update(_dot_nt(qs[r], kn) + bias_new[r], mask, vn, m_scr.at[r], l_scr.at[r], acc_scr.at[r])
            o_ref[g] = _softmax_result(l_scr.at[r], acc_scr.at[r])


def _nsa_sample_sel(z8, sel, bias_tail, bias_new, bias_far, cache6, page_table, layer, *, pos0):
    b, npages = page_table.shape
    nsp = sel.shape[-1]
    steps = npages // SEL_PAGES
    slot_cols = KV_HEADS * HEAD_DIM
    new_rows = lambda slot: pl.BlockSpec((None, TS, slot_cols),
                                         lambda bi, si, pt: (bi, 0, (C_NKV + slot * slot_cols) // slot_cols))
    whole = lambda a: pl.BlockSpec(a.shape, lambda bi, si, pt: (0,) * a.ndim)
    grid_spec = pltpu.PrefetchScalarGridSpec(
        num_scalar_prefetch=1,
        grid=(b, steps),
        in_specs=[pl.BlockSpec((None, TS, D_NSA), lambda bi, si, pt: (bi, 0, C_NQ // D_NSA)),
                  pl.BlockSpec((None, KV_HEADS, TS, nsp), lambda bi, si, pt: (bi, 0, 0, 0)),
                  new_rows(2), new_rows(3), whole(bias_tail), whole(bias_new), whole(bias_far)]
                 + [_page_spec(layer, 2, p, SEL_PAGES, 2) for p in range(SEL_PAGES)]
                 + [_page_spec(layer, 3, p, SEL_PAGES, 2) for p in range(SEL_PAGES)],
        out_specs=pl.BlockSpec((None, KV_HEADS, ROWS_S, HEAD_DIM), lambda bi, si, pt: (bi, 0, 0, 0)),
        scratch_shapes=[pltpu.VMEM((SEL_TILE * PAGE_SIZE * KV_HEADS, HEAD_DIM), BF16),
                        pltpu.VMEM((SEL_TILE * PAGE_SIZE * KV_HEADS, HEAD_DIM), BF16),
                        pltpu.VMEM((KV_HEADS * ROWS_S, 1), F32), pltpu.VMEM((KV_HEADS * ROWS_S, 1), F32),
                        pltpu.VMEM((KV_HEADS * ROWS_S, HEAD_DIM), F32)],
    )
    return pl.pallas_call(
        functools.partial(_nsa_s_sel_kernel, pos0=pos0, npages=npages),
        grid_spec=grid_spec,
        out_shape=jax.ShapeDtypeStruct((b, KV_HEADS, ROWS_S, HEAD_DIM), F32),
        compiler_params=_cparams("parallel", "arbitrary"),
        name="nsa_sample_sel",
    )(page_table, z8, sel, z8, z8, bias_tail, bias_new, bias_far, *([cache6] * (2 * SEL_PAGES)))


def _nsa_s_win_kernel(q_ref, kw_ref, vw_ref, kn_ref, vn_ref, bw_ref, oc_ref, os_ref, gt_ref, gb_ref, o_ref,
                      m_scr, l_scr, acc_scr, *, pos0):
    g = pl.program_id(1)
    wb = kw_ref.shape[0]
    own = pl.ds(g, wb, stride=KV_HEADS)
    kw = kw_ref.reshape(wb * KV_HEADS, HEAD_DIM)[own, :]
    vw = vw_ref.reshape(wb * KV_HEADS, HEAD_DIM)[own, :]
    qs = _stack_heads(q_ref[...] * QK_SCALE).astype(BF16)
    qpos = pos0 + (lax.broadcasted_iota(jnp.int32, (ROWS_S, 1), 0) & (TS - 1))
    bias = bw_ref[...].reshape(ROWS_S, wb + PAGE_SIZE)
    _softmax_init(m_scr, l_scr, acc_scr)
    dist = qpos - (pos0 - wb + lax.broadcasted_iota(jnp.int32, (1, wb), 1))
    s = _dot_nt(qs, kw.astype(BF16)) + bias[:, 0:wb]
    _softmax_update(s, (dist >= 0) & (dist <= WINDOW), vw.astype(BF16), m_scr, l_scr, acc_scr)
    kidx = lax.broadcasted_iota(jnp.int32, (1, PAGE_SIZE), 1)
    dist = qpos - (pos0 + kidx)
    s = _dot_nt(qs, _pad_rows(kn_ref[...], PAGE_SIZE).astype(BF16)) + bias[:, wb:wb + PAGE_SIZE]
    _softmax_update(s, (dist >= 0) & (dist <= WINDOW) & (kidx < TS), _pad_rows(vn_ref[...], PAGE_SIZE).astype(BF16),
                    m_scr, l_scr, acc_scr)
    o_w = _softmax_result(l_scr, acc_scr)
    o_c = oc_ref[...]
    o_s = os_ref[...]
    gt = gt_ref[...] + gb_ref[...]
    for j in range(GROUP):
        col = GATE_NG + (g * GROUP + j) * 3
        r = slice(j * TS, (j + 1) * TS)
        o_ref[:, j * HEAD_DIM:(j + 1) * HEAD_DIM] = (
            _gate(gt, col) * o_c[r] + _gate(gt, col + 1) * o_s[r] + _gate(gt, col + 2) * o_w[r])


def _nsa_sample_win(z8, win6, layer, bias_w, o_c, o_s, gate_b, *, pos0):
    _, b, wb = win6.shape[:3]
    cached = lambda slot: pl.BlockSpec((None, None, wb, None, KV_HEADS, HEAD_DIM),
                                       lambda bi, g: (layer, bi, 0, slot, 0, 0))
    zcol = lambda col: (lambda bi, g: (bi, 0, col // HEAD_DIM + g))
    part = pl.BlockSpec((None, None, ROWS_S, HEAD_DIM), lambda bi, g: (bi, g, 0, 0))
    return pl.pallas_call(
        functools.partial(_nsa_s_win_kernel, pos0=pos0),
        grid=(b, KV_HEADS),
        in_specs=[pl.BlockSpec((None, TS, GROUP * HEAD_DIM), lambda bi, g: (bi, 0, C_NQ // (GROUP * HEAD_DIM) + g)),
                  cached(0), cached(1),
                  pl.BlockSpec((None, TS, HEAD_DIM), zcol(C_NKV + 4 * KV_HEADS * HEAD_DIM)),
                  pl.BlockSpec((None, TS, HEAD_DIM), zcol(C_NKV + 5 * KV_HEADS * HEAD_DIM)),
                  pl.BlockSpec((GROUP, TS, wb + PAGE_SIZE), lambda bi, g: (g, 0, 0)),
                  part, part,
                  pl.BlockSpec((None, TS, LANES), lambda bi, g: (bi, 0, C_GATE // LANES)),
                  pl.BlockSpec((1, LANES), lambda bi, g: (0, 0))],
        out_specs=pl.BlockSpec((None, TS, GROUP * HEAD_DIM), lambda bi, g: (bi, 0, g)),
        out_shape=jax.ShapeDtypeStruct((b, TS, D_NSA), F32),
        scratch_shapes=[pltpu.VMEM((ROWS_S, 1), F32), pltpu.VMEM((ROWS_S, 1), F32), pltpu.VMEM((ROWS_S, HEAD_DIM), F32)],
        compiler_params=_cparams("parallel", "parallel"),
        name="nsa_sample_win",
    )(z8, win6, win6, z8, z8, bias_w, o_c, o_s, z8, gate_b)


def _t5_bucket(dist):
    n = np.maximum(dist, 0)
    exact = NUM_BUCKETS // 2
    nf = np.maximum(n, 1).astype(np.float32)
    large = exact + (np.log(nf / np.float32(exact)) / np.float32(math.log(MAX_DISTANCE / exact))
                     * np.float32(NUM_BUCKETS - exact)).astype(np.int32)
    return np.where(n < exact, n, np.minimum(large, NUM_BUCKETS - 1)).astype(np.int32)


FAR_DIST = int(np.max(np.nonzero(_t5_bucket(np.arange(4 * MAX_DISTANCE)) < NUM_BUCKETS - 1)[0])) + 1
N_TZ = (FAR_DIST + TK - 2) // TQ + 1


def _bias_table(rel_bias, dist):
    onehot = jax.nn.one_hot(jnp.asarray(_t5_bucket(dist).astype(np.int8)), NUM_BUCKETS, dtype=F32)
    return jnp.einsum("rcb,bh->hrc", onehot, rel_bias, precision=lax.Precision.HIGHEST)


def _pack_weights(p):
    w_in = p["w_in"].astype(BF16)
    pad = jnp.zeros(w_in.shape[:2] + (N_IN - C_GATE - 32,), BF16)
    w_in_p = jnp.concatenate([w_in[..., 0:2048], w_in[..., 2056:3080], w_in[..., 3080:4616], w_in[..., 4640:5152],
                              w_in[..., 2048:2056], w_in[..., 4616:4640], pad], axis=2)
    return dict(
        w_in=w_in_p, w_out=p["w_out"].astype(BF16), w_down=p["ffn_w_down"].astype(BF16),
        w_up=p["ffn_w_up"].astype(BF16))


def _pack_layer(p, big, l):
    gb = p["mlstm_gate_b"][l]
    mlstm_gate_b = jnp.zeros((1, LANES), F32).at[0, GATE_MI:GATE_MI + 4].set(gb[0]).at[0, GATE_MF:GATE_MF + 4].set(gb[1])
    nsa_gate_b = jnp.zeros((1, LANES), F32).at[0, GATE_NG:GATE_NG + 3 * NSA_HEADS].set(p["nsa_gate_b"][l].reshape(-1))
    w1 = p["cmp_w1"][l]
    wcat = jnp.concatenate([w1[:, 0:CMP_STRIDE], w1[:, CMP_STRIDE:CMP_BLOCK]], axis=-1).astype(BF16)
    lam = lax.complex(p["s5_a_re"][l], p["s5_a_im"][l])
    lam_bar = jnp.exp(lam * jnp.exp(p["s5_log_step"][l])[:, None])
    b_bar = ((lam_bar - 1.0) / lam)[..., None] * lax.complex(p["s5_b_re"][l], p["s5_b_im"][l])
    gi = S5_GROUPS // S5_IN_TILES
    bd_in = lambda m: jnp.einsum("cgph,gk->cghkp", m.reshape(S5_IN_TILES, gi, S5_STATE, S5_GROUP_WIDTH),
                                 jnp.eye(gi, dtype=F32)).reshape(S5_IN_TILES, S5_IN_FEATS, 256)
    wb = jnp.concatenate([bd_in(b_bar.real), bd_in(b_bar.imag)], axis=2).astype(BF16)
    go = S5_GROUPS // S5_OUT_TILES
    bd_out = lambda m: jnp.einsum("jghp,gk->jgpkh", m.reshape(S5_OUT_TILES, go, S5_GROUP_WIDTH, S5_STATE),
                                  jnp.eye(go, dtype=F32)).reshape(S5_OUT_TILES, S5_OUT_CH, 256)
    wc = jnp.stack([bd_out(p["s5_c_re"][l]), -bd_out(p["s5_c_im"][l])]).astype(BF16)
    half = lambda w: jnp.pad(w, ((0, 0), (0, D_FFP - D_FF)))
    conv_w = p["ffn_conv_w"][l]
    conv_b = p["ffn_conv_b"][l][None, :]
    return dict(
        big, layer=l, mlstm_gate_b=mlstm_gate_b, mlstm_norm_w=p["mlstm_norm_w"][l][None, :], nsa_gate_b=nsa_gate_b,
        wcat=wcat, cmp_b1=p["cmp_b1"][l], cmp_w2=p["cmp_w2"][l].astype(BF16),
        s5_wb=wb, s5_wc=wc, s5_lam_r=lam_bar.real.reshape(1, S5_CH), s5_lam_i=lam_bar.imag.reshape(1, S5_CH),
        s5_d=p["s5_d"][l][None, :], s5_glu_w=p["s5_glu_w"][l].astype(BF16), s5_glu_b=p["s5_glu_b"][l][None, :],
        ln1_w=p["ln1_w"][l][None, :], ln1_b=p["ln1_b"][l][None, :],
        conv_w8=jnp.pad(jnp.concatenate([half(conv_w[:, :D_FF]), half(conv_w[:, D_FF:])], axis=1),
                        ((0, SUBLANES - CONV_W), (0, 0))),
        conv_b=jnp.concatenate([half(conv_b[:, :D_FF]), half(conv_b[:, D_FF:])], axis=1),
        ln2_w=p["ln2_w"][l][None, :], ln2_b=p["ln2_b"][l][None, :],
    )


def _unpad_ff(x):
    return jnp.concatenate([x[..., :D_FF], x[..., D_FFP:D_FFP + D_FF]], axis=-1)


def _pad_ff(x):
    pad = [(0, 0)] * (x.ndim - 1) + [(0, D_FFP - D_FF)]
    return jnp.concatenate([jnp.pad(x[..., :D_FF], pad), jnp.pad(x[..., D_FF:], pad)], axis=-1)


def _mixer_tail(x2d, z, b, t, t_use, lw, o_mlstm, o_nsa, s5_state, conv_state8, tm):
    pad8 = lambda s: jnp.pad(s.reshape(b, S5_CH), ((0, SUBLANES - b), (0, 0)))
    o_s5, xr, xi = _s5(z.reshape(b, t, N_IN), lw["s5_wb"], lw["s5_wc"], lw["s5_lam_r"], lw["s5_lam_i"], lw["s5_d"],
                       lw["s5_glu_w"], lw["s5_glu_b"], pad8(s5_state[0]), pad8(s5_state[1]), lc=S5_LC, steps=t_use)
    o_s5 = o_s5.reshape(b * t, D_S5)
    layer = lw["layer"]
    x1 = _wout_ln(o_mlstm.reshape(b * t, D_MLSTM), o_nsa.reshape(b * t, D_NSA), o_s5, x2d,
                  lw["w_out"], layer, lw["ln1_w"], lw["ln1_b"], tm=256)
    s5_new = (xr[:b].reshape(b, S5_GROUPS, S5_STATE), xi[:b].reshape(b, S5_GROUPS, S5_STATE))
    if conv_state8 is None:
        x2, tail_a, tail_g = _ffn_prompt(x1.reshape(b, t, D_MODEL), lw["w_up"], lw["w_down"], layer, lw["conv_w8"],
                                         lw["conv_b"], lw["ln2_w"], lw["ln2_b"], tt=FFN_TT, tf=FFN_TF)
        keep = slice(SUBLANES - (CONV_W - 1), SUBLANES)
        conv_new = jnp.concatenate([tail_a[:, keep, :D_FF], tail_g[:, keep, :D_FF]], axis=-1)
        return x2.reshape(b * t, D_MODEL), s5_new, conv_new
    up = _matmul(x1, lw["w_up"], layer, tm=tm, tn=FFN_TF, n=2 * D_FFP, w_cols=_up_cols(FFN_TF))
    hgate = _convgate(up.reshape(b, t, 2 * D_FFP), conv_state8, lw["conv_w8"], lw["conv_b"], tf=FFN_TF)
    x2 = _down_ln(hgate.reshape(b * t, D_FFP), lw["w_down"], layer, x1, lw["ln2_w"], lw["ln2_b"], tm=512, tk=1408)
    conv_new = _unpad_ff(up.reshape(b, t, 2 * D_FFP)[:, t_use - (CONV_W - 1):t_use])
    return x2, s5_new, conv_new


def _prompt_layer(x2d, b, t, lw, rel_bias, rows_so_far):
    z, kv_rows, win_rows = _proj_in(x2d, lw["w_in"], lw["layer"], rows_so_far, tm=min(1024, b * t), tn=768)
    z3 = z.reshape(b, t, N_IN)
    zeros = lambda *s: jnp.zeros(s, F32)
    o_mlstm, c1, n1, m1 = _mlstm(z3, lw["mlstm_gate_b"], lw["mlstm_norm_w"],
                                 zeros(b, MLSTM_HEADS, HEAD_DIM, HEAD_DIM), zeros(b, MLSTM_HEADS, LANES),
                                 zeros(b, MLSTM_HEADS, LANES), L=MLSTM_L, lin=MLSTM_L, t_valid=t)
    kcvc = _cmp_finish(_cmp_project_prompt(z3, lw["wcat"]), lw["cmp_b1"], lw["cmp_w2"])
    ncp = t // CMP_STRIDE
    bias_c = _bias_table(rel_bias, np.arange(t)[:, None] - (np.arange(ncp) * CMP_STRIDE + CMP_BLOCK - 1)[None, :])
    ti = np.arange(TQ)[:, None] - np.arange(TK)[None, :]
    far = _bias_table(rel_bias, np.full((1, 1), FAR_DIST))
    tz = jnp.stack([_bias_table(rel_bias, d * TQ + ti) - far for d in range(N_TZ)])
    o_nsa = _nsa_prompt(z3, kcvc, bias_c, tz, lw["nsa_gate_b"])
    x2, s5_new, conv_new = _mixer_tail(
        x2d, z, b, t, t, lw, o_mlstm, o_nsa, (zeros(b, S5_GROUPS, S5_STATE), zeros(b, S5_GROUPS, S5_STATE)),
        None, tm=1024)
    state = (None, None, c1, n1, m1[:, :, 0], s5_new[0], s5_new[1], conv_new)
    return x2, state, (kv_rows, win_rows)


def _sample_layer(x2d, b, tn, lw, rel_bias, layer, cache6, page_table, win6, mlstm_state, s5_state, conv_state):
    npages = page_table.shape[1]
    pos0 = npages * PAGE_SIZE
    z = _matmul(x2d, lw["w_in"], lw["layer"], tm=1024, tn=768)
    z8 = z.reshape(b, TS, N_IN)
    c0, n0, m0 = mlstm_state
    o_mlstm, c1, n1, m1 = _mlstm(z8, lw["mlstm_gate_b"], lw["mlstm_norm_w"], c0, n0,
                                 jnp.broadcast_to(m0[:, :, None], (b, MLSTM_HEADS, LANES)),
                                 L=LANES, lin=TS, t_valid=tn)
    n_chunks = (pos0 + tn) // CMP_STRIDE
    n_cmp = n_chunks - CMP_BLOCK // CMP_STRIDE + 1
    n_sel = -(-(pos0 + tn) // SEL_BLOCK)
    kcvc = _cmp_finish(_cmp_project_sample(cache6, page_table, layer, lw["wcat"]), lw["cmp_b1"], lw["cmp_w2"])
    qpos = pos0 + np.arange(TS)[:, None]
    bias_c = _bias_table(rel_bias, qpos - (np.arange(n_chunks) * CMP_STRIDE + CMP_BLOCK - 1)[None, :])
    o_c, sel = _nsa_sample_cmp(z8, kcvc, bias_c, pos0=pos0, n_cmp=n_cmp, n_sel=n_sel)
    kk = np.arange(PAGE_SIZE)[None, :]
    bias_tail = _bias_table(rel_bias, np.repeat(qpos - (pos0 - PAGE_SIZE + kk), KV_HEADS, axis=1))
    bias_new = _bias_table(rel_bias, qpos - (pos0 + kk))
    bias_far = _bias_table(rel_bias, np.broadcast_to(qpos - (pos0 - 2 * PAGE_SIZE), (TS, PAGE_SIZE)))
    o_s = _nsa_sample_sel(z8, sel, bias_tail, bias_new, bias_far, cache6, page_table, layer, pos0=pos0)
    wb = win6.shape[2]
    wk = np.arange(wb)[None, :]
    bias_w = _bias_table(rel_bias, np.concatenate([qpos - (pos0 - wb + wk), qpos - (pos0 + kk)], axis=1))
    o_nsa = _nsa_sample_win(z8, win6, layer, bias_w, o_c, o_s, lw["nsa_gate_b"], pos0=pos0)
    conv_state8 = jnp.pad(_pad_ff(conv_state), ((0, 0), (SUBLANES - (CONV_W - 1), 0), (0, 0)))
    x2, s5_new, conv_new = _mixer_tail(x2d, z, b, TS, tn, lw, o_mlstm, o_nsa, s5_state, conv_state8, tm=1024)
    nkv = z8[:, :tn, C_NKV:C_NKV + N_KV_SLOTS * KV_HEADS * HEAD_DIM].reshape(b, tn, N_KV_SLOTS, KV_HEADS, HEAD_DIM)
    state = (nkv[:, :, :4], nkv[:, :, 4:], c1, n1, m1[:, :, 0], s5_new[0], s5_new[1], conv_new)
    return x2, state


def kernel(x_prompt, x_sample, cache_nsa_kv, cache_win_kv, state_mlstm_c, state_mlstm_n, state_mlstm_m,
           state_s5_re, state_s5_im, state_ffn_conv, page_table, w_in, mlstm_gate_b, mlstm_norm_w,
           nsa_gate_b, cmp_w1, cmp_b1, cmp_w2, rel_bias, s5_a_re, s5_a_im, s5_b_re, s5_b_im, s5_c_re,
           s5_c_im, s5_d, s5_log_step, s5_glu_w, s5_glu_b, w_out, ln1_w, ln1_b, ffn_w_up, ffn_conv_w,
           ffn_conv_b, ffn_w_down, ln2_w, ln2_b):
    params = dict(w_in=w_in, mlstm_gate_b=mlstm_gate_b, mlstm_norm_w=mlstm_norm_w, nsa_gate_b=nsa_gate_b,
                  cmp_w1=cmp_w1, cmp_b1=cmp_b1, cmp_w2=cmp_w2, s5_a_re=s5_a_re, s5_a_im=s5_a_im, s5_b_re=s5_b_re,
                  s5_b_im=s5_b_im, s5_c_re=s5_c_re, s5_c_im=s5_c_im, s5_d=s5_d, s5_log_step=s5_log_step,
                  s5_glu_w=s5_glu_w, s5_glu_b=s5_glu_b, w_out=w_out, ln1_w=ln1_w, ln1_b=ln1_b, ffn_w_up=ffn_w_up,
                  ffn_conv_w=ffn_conv_w, ffn_conv_b=ffn_conv_b, ffn_w_down=ffn_w_down, ln2_w=ln2_w, ln2_b=ln2_b)
    depth = w_in.shape[0]
    bp, tp, _ = x_prompt.shape
    bs, tn, _ = x_sample.shape
    assert tp % TK == 0 and tn < CMP_STRIDE and tn <= TS and SUBLANES % bp == 0 and SUBLANES % bs == 0
    assert (tn * bs) % SUBLANES == 0 and tp % min(FFN_TT, tp) == 0
    assert page_table.shape[1] % CMP_PAGES == 0 and cache_nsa_kv.shape[2] == PAGE_SIZE
    xp = x_prompt.reshape(bp * tp, D_MODEL)
    xs = jnp.pad(x_sample, ((0, 0), (0, TS - tn), (0, 0))).reshape(bs * TS, D_MODEL)
    p_states, s_states = [], []
    big = _pack_weights(params)
    rows_so_far = None
    for l in range(depth):
        lw = _pack_layer(params, big, l)
        xp, sp, rows_so_far = _prompt_layer(xp, bp, tp, lw, rel_bias, rows_so_far)
        xs, ss = _sample_layer(xs, bs, tn, lw, rel_bias, l, cache_nsa_kv, page_table, cache_win_kv,
                               (state_mlstm_c[l], jnp.pad(state_mlstm_n[l], ((0, 0), (0, 0), (0, LANES - HEAD_DIM))),
                                state_mlstm_m[l]),
                               (state_s5_re[l], state_s5_im[l]), state_ffn_conv[l])
        p_states.append(sp)
        s_states.append(ss)
    stk = lambda states, i: jnp.stack([s[i] for s in states])
    y_prompt = xp.reshape(bp, tp, D_MODEL)
    y_sample = xs.reshape(bs, TS, D_MODEL)[:, :tn]
    wrows = min(WINDOW, tp)
    nsa_kv_p = rows_so_far[0].reshape(depth, bp, tp, 4, KV_HEADS, HEAD_DIM)
    win_kv_p = rows_so_far[1].reshape(depth, bp, tp, 2, KV_HEADS, HEAD_DIM)[:, :, tp - wrows:]
    return (y_prompt, y_sample,
            nsa_kv_p, stk(s_states, 0), win_kv_p, stk(s_states, 1),
            stk(p_states, 2), stk(s_states, 2), stk(p_states, 3), stk(s_states, 3), stk(p_states, 4), stk(s_states, 4),
            stk(p_states, 5), stk(s_states, 5), stk(p_states, 6), stk(s_states, 6), stk(p_states, 7), stk(s_states, 7))
```

```python
import functools
import math

import jax
import jax.numpy as jnp
import numpy as np
from jax import lax
from jax.experimental import pallas as pl
from jax.experimental.pallas import tpu as pltpu

F32 = jnp.float32
BF16 = jnp.bfloat16
NEG_INF = float("-inf")
M_INIT = -1e30
MASKED = -1e30

D_MODEL = 2048
PAGE_SIZE = 128
D_MLSTM = D_MODEL // 4
D_NSA = D_MODEL // 2
D_S5 = D_MODEL - D_MLSTM - D_NSA
HEAD_DIM = 128
MLSTM_HEADS = D_MLSTM // HEAD_DIM
NSA_HEADS = D_NSA // HEAD_DIM
KV_HEADS = 2
GROUP = NSA_HEADS // KV_HEADS
N_KV_SLOTS = 6
CMP_BLOCK = 32
CMP_STRIDE = 16
SEL_BLOCK = 64
SEL_TOPK = 16
FORCE_SCORE = 1e4
WINDOW = 512
S5_GROUP_WIDTH = 16
S5_GROUPS = D_S5 // S5_GROUP_WIDTH
S5_STATE = 64
S5_CH = S5_GROUPS * S5_STATE
NUM_BUCKETS = 32
MAX_DISTANCE = 128
D_FF = ((8 * D_MODEL // 3 + 127) // 128) * 128
CONV_W = 3
LN_EPS = 1e-5
DEPTH = 2
DEEPNORM_ALPHA = (2 * DEPTH) ** 0.25
QK_SCALE = HEAD_DIM ** -0.5

LANES = 128
SUBLANES = 8
V7X_VMEM_LIMIT = 56 * 2 ** 20

C_MQ, C_MK, C_MV, C_MO = 0, 512, 1024, 1536
C_NQ = 2048
C_NKV = 3072
C_SU = 4608
C_GATE = 5120
N_IN = 5376
GATE_MI, GATE_MF, GATE_NG = 0, 4, 8
D_FFP = 5632
MLSTM_L = 256
TQ = 256
TK = 256
S5_LC = 256
FFN_TT = 1024
FFN_TF = 512
FFN_HALO = 16
CMP_PAGES = 32
SEL_PAGES = 32
SEL_TILE = 32
SEL_SHIFT = SEL_BLOCK.bit_length() - 1


def _cparams(*sem):
    return pltpu.CompilerParams(dimension_semantics=sem, vmem_limit_bytes=V7X_VMEM_LIMIT)


def _dot(a, b):
    return jnp.dot(a, b, preferred_element_type=F32)


def _dot_nt(a, b):
    return lax.dot_general(a, b, (((1,), (1,)), ((), ())), preferred_element_type=F32)


def _layer_norm(y, w, b):
    mu = jnp.mean(y, axis=-1, keepdims=True)
    d = y - mu
    var = jnp.mean(d * d, axis=-1, keepdims=True)
    return d * lax.rsqrt(var + LN_EPS) * w + b


def _mm_kernel(x_ref, w_ref, o_ref, *, realign_tile=None):
    acc = _dot(x_ref[...].astype(BF16), w_ref[...])
    if realign_tile is not None:
        acc = _realign(acc, pl.program_id(1) == realign_tile)
    o_ref[...] = acc.astype(o_ref.dtype)


UP_OVER = D_FFP - D_FF


def _up_cols(tf):
    nf, per = D_FFP // tf, tf // LANES
    return lambda j: (j * per + (j >= nf).astype(jnp.int32) * (D_FF // LANES - nf * per)
                      - (j == 2 * nf - 1).astype(jnp.int32) * (UP_OVER // LANES)) * LANES


def _realign(up, is_last):
    shifted = jnp.concatenate([up[:, UP_OVER:], jnp.zeros((up.shape[0], UP_OVER), up.dtype)], axis=1)
    return jnp.where(is_last, shifted, up)


def _matmul(x, w, layer, *, tm, tn, n=None, w_cols=None):
    m, k = x.shape
    n = w.shape[2] if n is None else n
    tm = min(tm, m)
    if w_cols is None:
        w_spec = pl.BlockSpec((None, k, tn), lambda i, j: (layer, 0, j))
    else:
        w_spec = pl.BlockSpec((None, pl.Element(k), pl.Element(tn)), lambda i, j: (layer, 0, w_cols(j)))
    return pl.pallas_call(
        functools.partial(_mm_kernel, realign_tile=None if w_cols is None else n // tn - 1),
        grid=(m // tm, n // tn),
        in_specs=[pl.BlockSpec((tm, k), lambda i, j: (i, 0)), w_spec],
        out_specs=pl.BlockSpec((tm, tn), lambda i, j: (i, j)),
        out_shape=jax.ShapeDtypeStruct((m, n), F32),
        compiler_params=_cparams("parallel", "parallel"),
        name="proj",
    )(x, w)


KV4_GROUPS = 4 * KV_HEADS
WIN_GROUPS = 2 * KV_HEADS


def _proj_in_kernel(x_ref, w_ref, z_ref, kv_ref, win_ref, xb_ref, *, tn):
    j = pl.program_id(1)

    @pl.when(j == 0)
    def _():
        xb_ref[...] = x_ref[...].astype(BF16)

    acc = _dot(xb_ref[...], w_ref[...])
    z_ref[...] = acc
    tm = acc.shape[0]
    per_tile = tn // HEAD_DIM
    first_tile = C_NKV // tn
    for tile in range(first_tile, first_tile + N_KV_SLOTS * KV_HEADS // per_tile):
        @pl.when(j == tile)
        def _(tile=tile):
            for c in range(per_tile):
                grp = (tile - first_tile) * per_tile + c
                val = acc[:, c * HEAD_DIM:(c + 1) * HEAD_DIM]
                if grp < KV4_GROUPS:
                    kv_ref[pl.ds(grp, tm, stride=KV4_GROUPS), :] = val
                else:
                    win_ref[pl.ds(grp - KV4_GROUPS, tm, stride=WIN_GROUPS), :] = val


def _proj_in_carry_kernel(x_ref, w_ref, kv_in_ref, win_in_ref, z_ref, kv_ref, win_ref, xb_ref, *, tn):
    del kv_in_ref, win_in_ref
    _proj_in_kernel(x_ref, w_ref, z_ref, kv_ref, win_ref, xb_ref, tn=tn)


def _proj_in(x, w, layer, rows_so_far, *, tm, tn):
    m, k = x.shape
    depth, _, n = w.shape
    nm = m // tm
    assert C_NKV % tn == 0 and (N_KV_SLOTS * KV_HEADS * HEAD_DIM) % tn == 0
    carried = () if rows_so_far is None else tuple(rows_so_far)
    return pl.pallas_call(
        functools.partial(_proj_in_carry_kernel if carried else _proj_in_kernel, tn=tn),
        grid=(nm, n // tn),
        in_specs=[pl.BlockSpec((tm, k), lambda i, j: (i, 0)), pl.BlockSpec((None, k, tn), lambda i, j: (layer, 0, j))]
                 + [pl.BlockSpec(memory_space=pl.ANY)] * len(carried),
        out_specs=[pl.BlockSpec((tm, tn), lambda i, j: (i, j)),
                   pl.BlockSpec((tm * KV4_GROUPS, HEAD_DIM), lambda i, j: (layer * nm + i, 0)),
                   pl.BlockSpec((tm * WIN_GROUPS, HEAD_DIM), lambda i, j: (layer * nm + i, 0))],
        out_shape=[jax.ShapeDtypeStruct((m, n), F32),
                   jax.ShapeDtypeStruct((depth * m * KV4_GROUPS, HEAD_DIM), F32),
                   jax.ShapeDtypeStruct((depth * m * WIN_GROUPS, HEAD_DIM), F32)],
        input_output_aliases={2: 1, 3: 2} if carried else {},
        scratch_shapes=[pltpu.VMEM((tm, k), BF16)],
        compiler_params=_cparams("parallel", "arbitrary"),
        name="proj_in",
    )(x, w, *carried)


def _wout_kernel(om_ref, on_ref, os_ref, x_ref, w_ref, lw_ref, lb_ref, o_ref):
    acc = _dot(om_ref[...].astype(BF16), w_ref[0:D_MLSTM, :])
    acc += _dot(on_ref[...].astype(BF16), w_ref[D_MLSTM:D_MLSTM + D_NSA, :])
    acc += _dot(os_ref[...].astype(BF16), w_ref[D_MLSTM + D_NSA:D_MODEL, :])
    o_ref[...] = _layer_norm(DEEPNORM_ALPHA * x_ref[...] + acc, lw_ref[...], lb_ref[...])


def _wout_ln(om, on, os_, x, w, layer, lw, lb, *, tm):
    m = x.shape[0]
    tm = min(tm, m)
    row = lambda width: pl.BlockSpec((tm, width), lambda i: (i, 0))
    full = lambda a: pl.BlockSpec(a.shape, lambda i: (0, 0))
    w_spec = pl.BlockSpec((None,) + w.shape[1:], lambda i: (layer, 0, 0))
    return pl.pallas_call(
        _wout_kernel,
        grid=(m // tm,),
        in_specs=[row(D_MLSTM), row(D_NSA), row(D_S5), row(D_MODEL), w_spec, full(lw), full(lb)],
        out_specs=row(D_MODEL),
        out_shape=jax.ShapeDtypeStruct((m, D_MODEL), F32),
        compiler_params=_cparams("parallel"),
        name="wout_ln",
    )(om, on, os_, x, w, lw, lb)


def _valid_rows(w_ref, tile):
    rows = w_ref.shape[0]
    row = tile * rows + lax.broadcasted_iota(jnp.int32, (rows, 1), 0)
    w = w_ref[...]
    return jnp.where(row < D_FF, w, jnp.zeros_like(w))


def _down_kernel(h_ref, w_ref, x_ref, lw_ref, lb_ref, o_ref, acc_ref, *, nk):
    k = pl.program_id(1)

    @pl.when(k == 0)
    def _():
        acc_ref[...] = jnp.zeros_like(acc_ref)

    acc_ref[...] += _dot(h_ref[...], _valid_rows(w_ref, k))

    @pl.when(k == nk - 1)
    def _():
        o_ref[...] = _layer_norm(DEEPNORM_ALPHA * x_ref[...] + acc_ref[...], lw_ref[...], lb_ref[...])


def _down_ln(h, w, layer, x, lw, lb, *, tm, tk):
    m, kk = h.shape
    tm = min(tm, m)
    nk = kk // tk
    return pl.pallas_call(
        functools.partial(_down_kernel, nk=nk),
        grid=(m // tm, nk),
        in_specs=[
            pl.BlockSpec((tm, tk), lambda i, k: (i, k)),
            pl.BlockSpec((None, tk, D_MODEL), lambda i, k: (layer, k, 0)),
            pl.BlockSpec((tm, D_MODEL), lambda i, k: (i, 0)),
            pl.BlockSpec((1, D_MODEL), lambda i, k: (0, 0)),
            pl.BlockSpec((1, D_MODEL), lambda i, k: (0, 0)),
        ],
        out_specs=pl.BlockSpec((tm, D_MODEL), lambda i, k: (i, 0)),
        out_shape=jax.ShapeDtypeStruct((m, D_MODEL), F32),
        scratch_shapes=[pltpu.VMEM((tm, D_MODEL), F32)],
        compiler_params=_cparams("parallel", "arbitrary"),
        name="down_ln",
    )(h, w, x, lw, lb)


def _convgate_kernel(a_ref, g_ref, sa_ref, sg_ref, wa_ref, wg_ref, ba_ref, bg_ref, o_ref):
    rows = lax.broadcasted_iota(jnp.int32, (1, SUBLANES, 1), 1)

    def conv(cur_ref, st_ref, w_ref, b_ref):
        cur = cur_ref[...]
        prev = st_ref[...]
        p1 = prev[:, 7:8, :]
        p2 = prev[:, 6:7, :]
        x1 = jnp.where(rows == 0, p1, pltpu.roll(cur, 1, 1))
        x2 = jnp.where(rows == 0, p2, jnp.where(rows == 1, p1, pltpu.roll(cur, 2, 1)))
        w = w_ref[...]
        return b_ref[...] + w[0:1, :] * x2 + w[1:2, :] * x1 + w[2:3, :] * cur

    a = conv(a_ref, sa_ref, wa_ref, ba_ref)
    g = conv(g_ref, sg_ref, wg_ref, bg_ref)
    o_ref[...] = (a * jax.nn.sigmoid(a) * g).astype(o_ref.dtype)


def _convgate(up, state8, conv_w8, conv_b, *, tf):
    b, t, _ = up.shape
    assert t == SUBLANES
    nf = D_FFP // tf
    cur_a = pl.BlockSpec((b, t, tf), lambda fi: (0, 0, fi))
    cur_g = pl.BlockSpec((b, t, tf), lambda fi: (0, 0, fi + nf))
    w_a = pl.BlockSpec((SUBLANES, tf), lambda fi: (0, fi))
    w_g = pl.BlockSpec((SUBLANES, tf), lambda fi: (0, fi + nf))
    b_a = pl.BlockSpec((1, tf), lambda fi: (0, fi))
    b_g = pl.BlockSpec((1, tf), lambda fi: (0, fi + nf))
    return pl.pallas_call(
        _convgate_kernel,
        grid=(nf,),
        in_specs=[cur_a, cur_g, cur_a, cur_g, w_a, w_g, b_a, b_g],
        out_specs=pl.BlockSpec((b, t, tf), lambda fi: (0, 0, fi)),
        out_shape=jax.ShapeDtypeStruct((b, t, D_FFP), BF16),
        compiler_params=_cparams("parallel"),
        name="convgate",
    )(up, up, state8, state8, conv_w8, conv_w8, conv_b, conv_b)


def _ffn_kernel(x_ref, xh_ref, wa_ref, wg_ref, wd_ref, cwa_ref, cwg_ref, cba_ref, cbg_ref, lw_ref, lb_ref,
                o_ref, sa_ref, sg_ref, xe_ref, *, nf):
    first = pl.program_id(1) == 0
    f = pl.program_id(2)
    tt = x_ref.shape[0]

    @pl.when(f == 0)
    def _():
        o_ref[...] = jnp.zeros_like(o_ref)
        halo = jnp.where(first, 0.0, xh_ref[...])
        xe_ref[0:FFN_HALO, :] = halo.astype(BF16)
        xe_ref[FFN_HALO:, :] = x_ref[...].astype(BF16)

    xe = xe_ref[...]

    def branch(w_ref, cw_ref, cb_ref, s_ref, second_half):
        up = _dot(xe, w_ref[...])
        if second_half:
            up = _realign(up, f == nf - 1)
        s_ref[...] = up[FFN_HALO + tt - SUBLANES:FFN_HALO + tt, :]
        w = cw_ref[...]
        x1 = pltpu.roll(up, 1, 0)[FFN_HALO:, :]
        x2 = pltpu.roll(up, 2, 0)[FFN_HALO:, :]
        return cb_ref[...] + w[0:1, :] * x2 + w[1:2, :] * x1 + w[2:3, :] * up[FFN_HALO:, :]

    a = branch(wa_ref, cwa_ref, cba_ref, sa_ref, False)
    g = branch(wg_ref, cwg_ref, cbg_ref, sg_ref, True)
    o_ref[...] += _dot((a * jax.nn.sigmoid(a) * g).astype(BF16), _valid_rows(wd_ref, f))

    @pl.when(f == nf - 1)
    def _():
        o_ref[...] = _layer_norm(DEEPNORM_ALPHA * x_ref[...] + o_ref[...], lw_ref[...], lb_ref[...])


def _ffn_prompt(x3, w_up, w_down, layer, conv_w8, conv_b, lw, lb, *, tt, tf):
    b, t, _ = x3.shape
    nf = D_FFP // tf
    tt = min(tt, t)
    hb = tt // FFN_HALO
    half_a = lambda rows: pl.BlockSpec((rows, tf), lambda bi, ti, fi: (0, fi))
    half_g = lambda rows: pl.BlockSpec((rows, tf), lambda bi, ti, fi: (0, fi + nf))
    cols = _up_cols(tf)
    w_a = pl.BlockSpec((None, pl.Element(D_MODEL), pl.Element(tf)), lambda bi, ti, fi: (layer, 0, cols(fi)))
    w_g = pl.BlockSpec((None, pl.Element(D_MODEL), pl.Element(tf)), lambda bi, ti, fi: (layer, 0, cols(fi + nf)))
    vec = pl.BlockSpec((1, D_MODEL), lambda bi, ti, fi: (0, 0))
    tail = pl.BlockSpec((None, SUBLANES, tf), lambda bi, ti, fi: (bi, 0, fi))
    return pl.pallas_call(
        functools.partial(_ffn_kernel, nf=nf),
        grid=(b, t // tt, nf),
        in_specs=[pl.BlockSpec((None, tt, D_MODEL), lambda bi, ti, fi: (bi, ti, 0), pipeline_mode=pl.Buffered(1)),
                  pl.BlockSpec((None, FFN_HALO, D_MODEL), lambda bi, ti, fi: (bi, jnp.maximum(ti * hb - 1, 0), 0)),
                  w_a, w_g,
                  pl.BlockSpec((None, tf, D_MODEL), lambda bi, ti, fi: (layer, fi, 0)),
                  half_a(SUBLANES), half_g(SUBLANES), half_a(1), half_g(1), vec, vec],
        out_specs=[pl.BlockSpec((None, tt, D_MODEL), lambda bi, ti, fi: (bi, ti, 0)), tail, tail],
        out_shape=[jax.ShapeDtypeStruct((b, t, D_MODEL), F32),
                   jax.ShapeDtypeStruct((b, SUBLANES, D_FFP), F32),
                   jax.ShapeDtypeStruct((b, SUBLANES, D_FFP), F32)],
        scratch_shapes=[pltpu.VMEM((tt + FFN_HALO, D_MODEL), BF16)],
        compiler_params=_cparams("parallel", "arbitrary", "arbitrary"),
        name="ffn_prompt",
    )(x3, x3, w_up, w_up, w_down, conv_w8, conv_w8, conv_b, conv_b, lw, lb)


def _cumsum_rows(x):
    n = x.shape[0]
    rows = lax.broadcasted_iota(jnp.int32, (n, 1), 0)
    d = 1
    while d < n:
        x = x + jnp.where(rows >= d, pltpu.roll(x, d, 0), 0.0)
        d *= 2
    return x


def _log_sigmoid(x):
    return jnp.minimum(x, 0.0) - jnp.log1p(jnp.exp(-jnp.abs(x)))


def _mlstm_kernel(q_ref, k_ref, v_ref, og_ref, g_ref, gb_ref, nw_ref, c0_ref, n0_ref, m0_ref,
                  out_ref, c_ref, n_ref, m_ref, *, L, t_valid):
    ci = pl.program_id(1)
    lin = q_ref.shape[0]

    @pl.when(ci == 0)
    def _():
        c_ref[...] = c0_ref[...]
        n_ref[...] = n0_ref[...]
        m_ref[...] = m0_ref[...]

    def rows_of(ref):
        x = ref[...]
        if lin < L:
            x = jnp.concatenate([x, jnp.zeros((L - lin, x.shape[1]), x.dtype)], axis=0)
        return x

    rows = lax.broadcasted_iota(jnp.int32, (L, 1), 0)
    valid = (ci * L + rows) < t_valid
    pre = rows_of(g_ref) + gb_ref[...]
    lf = jnp.where(valid, _log_sigmoid(pre), 0.0)
    ig = jnp.where(valid, pre, NEG_INF)
    bcum = _cumsum_rows(lf)
    dt = (pltpu.roll(ig, GATE_MF - GATE_MI, 1) - bcum).T
    q_all, k_all, v_all, og_all = rows_of(q_ref), rows_of(k_ref), rows_of(v_ref), rows_of(og_ref)
    tri = lax.broadcasted_iota(jnp.int32, (L, L), 0) >= lax.broadcasted_iota(jnp.int32, (L, L), 1)
    nw = nw_ref[...]

    for h in range(MLSTM_HEADS):
        sl = slice(h * HEAD_DIM, (h + 1) * HEAD_DIM)
        q = q_all[:, sl]
        k = k_all[:, sl] * QK_SCALE
        v = v_all[:, sl]
        qb, kb, vb = q.astype(BF16), k.astype(BF16), v.astype(BF16)
        b_col = bcum[:, GATE_MF + h:GATE_MF + h + 1]
        ig_col = ig[:, GATE_MI + h:GATE_MI + h + 1]
        d_row = dt[GATE_MF + h:GATE_MF + h + 1, :]
        c_prev = c_ref[h]
        n_prev = n_ref[h:h + 1, :]
        m_prev = m_ref[h:h + 1, 0:1]

        dmat = jnp.where(tri, b_col + d_row, NEG_INF)
        g_col = b_col + m_prev
        m_row = jnp.maximum(jnp.max(dmat, axis=1, keepdims=True), g_col)
        a = jnp.exp(dmat - m_row) * _dot_nt(qb, kb)
        w_inter = jnp.exp(g_col - m_row)
        num = _dot(a.astype(BF16), vb) + w_inter * _dot(qb, c_prev.astype(BF16))
        den = jnp.sum(a, axis=1, keepdims=True) + w_inter * jnp.sum(q * n_prev, axis=1, keepdims=True)
        hid = num / jnp.maximum(jnp.abs(den), jnp.exp(-m_row))
        mu = jnp.mean(hid, axis=1, keepdims=True)
        dlt = hid - mu
        var = jnp.mean(dlt * dlt, axis=1, keepdims=True)
        hn = dlt * lax.rsqrt(var + LN_EPS) * nw[:, sl]
        res = jax.nn.sigmoid(og_all[:, sl]) * hn
        out_ref[:, sl] = res[0:lin, :]

        f_tot = b_col[L - 1:L, :]
        w_s = f_tot - b_col + ig_col
        m_new = jnp.maximum(f_tot + m_prev, jnp.max(w_s, axis=0, keepdims=True))
        ws = jnp.exp(w_s - m_new)
        decay = jnp.exp(f_tot + m_prev - m_new)
        c_ref[h] = decay * c_prev + _dot(k.T.astype(BF16), (ws * v).astype(BF16))
        n_ref[h:h + 1, :] = decay * n_prev + jnp.sum(ws * k, axis=0, keepdims=True)
        m_ref[h:h + 1, :] = jnp.broadcast_to(m_new, (1, LANES))


def _mlstm(z, gate_b, norm_w, c0, n0, m0, *, L, lin, t_valid):
    b, tz, _ = z.shape
    nchunks = tz // lin
    qblk = lambda col: pl.BlockSpec((None, lin, D_MLSTM), lambda bi, ci: (bi, ci, col // D_MLSTM))
    st4 = pl.BlockSpec((None, MLSTM_HEADS, HEAD_DIM, HEAD_DIM), lambda bi, ci: (bi, 0, 0, 0))
    st3 = pl.BlockSpec((None, MLSTM_HEADS, LANES), lambda bi, ci: (bi, 0, 0))
    return pl.pallas_call(
        functools.partial(_mlstm_kernel, L=L, t_valid=t_valid),
        grid=(b, nchunks),
        in_specs=[qblk(C_MQ), qblk(C_MK), qblk(C_MV), qblk(C_MO),
                  pl.BlockSpec((None, lin, LANES), lambda bi, ci: (bi, ci, C_GATE // LANES)),
                  pl.BlockSpec((1, LANES), lambda bi, ci: (0, 0)),
                  pl.BlockSpec((1, D_MLSTM), lambda bi, ci: (0, 0)),
                  st4, st3, st3],
        out_specs=[pl.BlockSpec((None, lin, D_MLSTM), lambda bi, ci: (bi, ci, 0)), st4, st3, st3],
        out_shape=[jax.ShapeDtypeStruct((b, tz, D_MLSTM), F32),
                   jax.ShapeDtypeStruct((b, MLSTM_HEADS, HEAD_DIM, HEAD_DIM), F32),
                   jax.ShapeDtypeStruct((b, MLSTM_HEADS, LANES), F32),
                   jax.ShapeDtypeStruct((b, MLSTM_HEADS, LANES), F32)],
        compiler_params=_cparams("parallel", "arbitrary"),
        name="mlstm",
    )(z, z, z, z, z, gate_b, norm_w, c0, n0, m0)


S5_IN_TILES = S5_CH // 256
S5_IN_FEATS = D_S5 // S5_IN_TILES
S5_OUT_TILES = D_S5 // 256
S5_OUT_CH = S5_CH // S5_OUT_TILES


S5_LT = S5_CH // LANES


def _lane_tiles(x):
    return jnp.stack([x[:, k * LANES:(k + 1) * LANES] for k in range(x.shape[1] // LANES)])


def _lane_untile(x):
    return jnp.concatenate([x[k] for k in range(x.shape[0])], axis=1)


def _s5_kernel(u_ref, wb_ref, wc_ref, lr_ref, li_ref, d_ref, gw_ref, gb_ref, x0r_ref, x0i_ref,
               o_ref, xr_ref, xi_ref, sr_ref, si_ref, y_ref, *, steps):
    nb, lc, _ = u_ref.shape

    @pl.when(pl.program_id(0) == 0)
    def _():
        xr_ref[...] = x0r_ref[...]
        xi_ref[...] = x0i_ref[...]

    for b in range(nb):
        ub = u_ref[b].astype(BF16)
        for c in range(S5_IN_TILES):
            bu = _dot(ub[:, c * S5_IN_FEATS:(c + 1) * S5_IN_FEATS], wb_ref[c])
            for k in range(256 // LANES):
                rows = pl.ds(b, lc, stride=nb)
                sr_ref.at[c * (256 // LANES) + k][rows, :] = bu[:, k * LANES:(k + 1) * LANES]
                si_ref.at[c * (256 // LANES) + k][rows, :] = bu[:, 256 + k * LANES:256 + (k + 1) * LANES]
    lam_r = _lane_tiles(jnp.broadcast_to(lr_ref[...], (SUBLANES, S5_CH)))
    lam_i = _lane_tiles(jnp.broadcast_to(li_ref[...], (SUBLANES, S5_CH)))
    row8 = lax.broadcasted_iota(jnp.int32, (1, SUBLANES, 1), 1)
    per_group = SUBLANES // nb

    def group(i, carry):
        xr, xi = carry
        r0 = pl.multiple_of(i * SUBLANES, SUBLANES)
        br = sr_ref[:, pl.ds(r0, SUBLANES), :]
        bi = si_ref[:, pl.ds(r0, SUBLANES), :]
        out_r, out_i = br, bi
        for k in range(per_group):
            nr = lam_r * xr - lam_i * xi + br
            ni = lam_r * xi + lam_i * xr + bi
            here = (row8 >= k * nb) & (row8 < (k + 1) * nb)
            out_r = jnp.where(here, nr, out_r)
            out_i = jnp.where(here, ni, out_i)
            if per_group > 1:
                xr = pltpu.roll(nr, nb, 1)
                xi = pltpu.roll(ni, nb, 1)
            else:
                xr, xi = nr, ni
        sr_ref[:, pl.ds(r0, SUBLANES), :] = out_r
        si_ref[:, pl.ds(r0, SUBLANES), :] = out_i
        return xr, xi

    xr, xi = lax.fori_loop(0, steps * nb // SUBLANES, group, (_lane_tiles(xr_ref[...]), _lane_tiles(xi_ref[...])))
    xr_ref[...] = _lane_untile(xr)
    xi_ref[...] = _lane_untile(xi)
    per_out = S5_OUT_CH // LANES
    for j in range(S5_OUT_TILES):
        lhs_r = jnp.concatenate([sr_ref[j * per_out + k] for k in range(per_out)], axis=1).astype(BF16)
        lhs_i = jnp.concatenate([si_ref[j * per_out + k] for k in range(per_out)], axis=1).astype(BF16)
        y = _dot(lhs_r, wc_ref[0, j]) + _dot(lhs_i, wc_ref[1, j])
        for k in range(256 // LANES):
            y_ref[j * (256 // LANES) + k] = y[:, k * LANES:(k + 1) * LANES]
    for b in range(nb):
        rows = pl.ds(b, lc, stride=nb)
        y = jnp.concatenate([y_ref.at[k][rows, :] for k in range(D_S5 // LANES)], axis=1) + d_ref[...] * u_ref[b]
        zz = jax.nn.gelu(y)
        o_ref[b] = zz * jax.nn.sigmoid(_dot(zz.astype(BF16), gw_ref[...]) + gb_ref[...])


def _s5(z3, wb, wc, lam_r, lam_i, d, glu_w, glu_b, x0r, x0i, *, lc, steps):
    b, tz, _ = z3.shape
    lc = min(lc, tz)
    steps = min(steps, lc)
    assert steps == lc or tz == lc
    full = lambda a: pl.BlockSpec(a.shape, lambda i: (0,) * a.ndim)
    st = pl.BlockSpec((SUBLANES, S5_CH), lambda i: (0, 0))
    return pl.pallas_call(
        functools.partial(_s5_kernel, steps=steps),
        grid=(tz // lc,),
        in_specs=[pl.BlockSpec((b, lc, D_S5), lambda i: (0, i, C_SU // D_S5)),
                  full(wb), full(wc), full(lam_r), full(lam_i), full(d), full(glu_w), full(glu_b), st, st],
        out_specs=[pl.BlockSpec((b, lc, D_S5), lambda i: (0, i, 0)), st, st],
        out_shape=[jax.ShapeDtypeStruct((b, tz, D_S5), F32),
                   jax.ShapeDtypeStruct((SUBLANES, S5_CH), F32),
                   jax.ShapeDtypeStruct((SUBLANES, S5_CH), F32)],
        scratch_shapes=[pltpu.VMEM((S5_LT, lc * b, LANES), F32), pltpu.VMEM((S5_LT, lc * b, LANES), F32),
                        pltpu.VMEM((D_S5 // LANES, lc * b, LANES), F32)],
        compiler_params=_cparams("arbitrary"),
        name="s5",
    )(z3, wb, wc, lam_r, lam_i, d, glu_w, glu_b, x0r, x0i)


N_CMB = 2 * KV_HEADS


def _cmp_project(rows_of, w_ref, o_ref):
    nch = o_ref.shape[0]
    for cmb in range(N_CMB):
        slot = cmb // KV_HEADS
        rows_ref = rows_of(cmb)
        acc = jnp.zeros((nch, 2 * HEAD_DIM), F32)
        for l in range(CMP_STRIDE):
            x = rows_ref[pl.ds(l, nch, stride=CMP_STRIDE), :]
            acc += _dot(x.astype(BF16), w_ref[slot, l])
        o_ref[:, cmb * 2 * HEAD_DIM:(cmb + 1) * 2 * HEAD_DIM] = acc


def _cmp_p_kernel(r0_ref, r1_ref, r2_ref, r3_ref, w_ref, o_ref):
    rows = (r0_ref, r1_ref, r2_ref, r3_ref)
    _cmp_project(lambda cmb: rows[cmb], w_ref, o_ref)


def _cmp_project_prompt(z, wcat):
    b, t, _ = z.shape
    nch = t // CMP_STRIDE
    rows = lambda cmb: pl.BlockSpec((None, t, HEAD_DIM), lambda bi: (bi, 0, C_NKV // HEAD_DIM + cmb))
    return pl.pallas_call(
        _cmp_p_kernel,
        grid=(b,),
        in_specs=[rows(cmb) for cmb in range(N_CMB)] + [pl.BlockSpec(wcat.shape, lambda bi: (0, 0, 0, 0))],
        out_specs=pl.BlockSpec((None, nch, N_CMB * 2 * HEAD_DIM), lambda bi: (bi, 0, 0)),
        out_shape=jax.ShapeDtypeStruct((b, nch, N_CMB * 2 * HEAD_DIM), F32),
        compiler_params=_cparams("parallel"),
        name="cmp_project_prompt",
    )(z, z, z, z, wcat)


def _cmp_s_kernel(pt_ref, *refs):
    del pt_ref
    per_pos = 2 * KV_HEADS
    page_refs = [r.reshape(PAGE_SIZE * per_pos, HEAD_DIM) for r in refs[:CMP_PAGES]]
    w_ref, o_ref = refs[CMP_PAGES:]
    per_page = PAGE_SIZE // CMP_STRIDE
    for cmb in range(N_CMB):
        acc = jnp.zeros((o_ref.shape[0], 2 * HEAD_DIM), F32)
        for l in range(CMP_STRIDE):
            x = jnp.concatenate(
                [page_refs[p][pl.ds(l * per_pos + cmb, per_page, stride=CMP_STRIDE * per_pos), :]
                 for p in range(CMP_PAGES)], axis=0)
            acc += _dot(x.astype(BF16), w_ref[cmb // KV_HEADS, l])
        o_ref[:, cmb * 2 * HEAD_DIM:(cmb + 1) * 2 * HEAD_DIM] = acc


def _page_spec(layer, slot, p, pages_per_step, grid_rank):
    def index_map(*idx):
        bi, si, pt = idx[0], idx[grid_rank - 1], idx[grid_rank]
        return (layer, pt[bi, si * pages_per_step + p], 0, slot, 0, 0)

    return pl.BlockSpec((None, None, PAGE_SIZE, None, KV_HEADS, HEAD_DIM), index_map)


def _cmp_project_sample(cache6, page_table, layer, wcat):
    b, npages = page_table.shape
    steps = npages // CMP_PAGES
    nch = CMP_PAGES * PAGE_SIZE // CMP_STRIDE
    def page_pair(p):
        return pl.BlockSpec((None, None, PAGE_SIZE, 2, KV_HEADS, HEAD_DIM),
                            lambda bi, si, pt: (layer, pt[bi, si * CMP_PAGES + p], 0, 0, 0, 0))

    pages = [page_pair(p) for p in range(CMP_PAGES)]

    grid_spec = pltpu.PrefetchScalarGridSpec(
        num_scalar_prefetch=1,
        grid=(b, steps),
        in_specs=pages + [pl.BlockSpec(wcat.shape, lambda bi, si, pt: (0, 0, 0, 0))],
        out_specs=pl.BlockSpec((None, nch, N_CMB * 2 * HEAD_DIM), lambda bi, si, pt: (bi, si, 0)),
    )
    return pl.pallas_call(
        _cmp_s_kernel,
        grid_spec=grid_spec,
        out_shape=jax.ShapeDtypeStruct((b, steps * nch, N_CMB * 2 * HEAD_DIM), F32),
        compiler_params=_cparams("parallel", "arbitrary"),
        name="cmp_project_sample",
    )(page_table, *([cache6] * CMP_PAGES), wcat)


def _cmp_fin_kernel(p_ref, b1_ref, w2_ref, o_ref):
    nch = p_ref.shape[0]
    for cmb in range(N_CMB):
        slot = cmb // KV_HEADS
        c0 = cmb * 2 * HEAD_DIM
        first = p_ref[:, c0:c0 + HEAD_DIM]
        second = pltpu.roll(p_ref[:, c0 + HEAD_DIM:c0 + 2 * HEAD_DIM], nch - 1, 0)
        hid = b1_ref[slot:slot + 1, :] + first + second
        o_ref[cmb] = _dot(jax.nn.gelu(hid).astype(BF16), w2_ref[slot])


def _cmp_finish(p, b1, w2):
    b, nch, _ = p.shape
    return pl.pallas_call(
        _cmp_fin_kernel,
        grid=(b,),
        in_specs=[pl.BlockSpec((None, nch, p.shape[2]), lambda bi: (bi, 0, 0)),
                  pl.BlockSpec(b1.shape, lambda bi: (0, 0)),
                  pl.BlockSpec(w2.shape, lambda bi: (0, 0, 0))],
        out_specs=pl.BlockSpec((None, N_CMB, nch, HEAD_DIM), lambda bi: (bi, 0, 0, 0)),
        out_shape=jax.ShapeDtypeStruct((b, N_CMB, nch, HEAD_DIM), F32),
        compiler_params=_cparams("parallel"),
        name="cmp_finish",
    )(p, b1, w2)


def _softmax_init(m_scr, l_scr, acc_scr):
    m_scr[...] = jnp.full(m_scr.shape, M_INIT, F32)
    l_scr[...] = jnp.zeros(l_scr.shape, F32)
    acc_scr[...] = jnp.zeros(acc_scr.shape, F32)


def _softmax_update(s, mask, vb, m_scr, l_scr, acc_scr):
    s = jnp.where(mask, s, NEG_INF)
    m_prev = m_scr[...]
    m_new = jnp.maximum(m_prev, jnp.max(s, axis=1, keepdims=True))
    alpha = jnp.exp(m_prev - m_new)
    p = jnp.exp(s - m_new)
    l_scr[...] = alpha * l_scr[...] + jnp.sum(p, axis=1, keepdims=True)
    acc_scr[...] = alpha * acc_scr[...] + _dot(p.astype(BF16), vb)
    m_scr[...] = m_new


def _softmax_result(l_scr, acc_scr):
    l = l_scr[...]
    return acc_scr[...] / jnp.where(l > 0, l, 1.0)


def _masked_probs(s, mask):
    s = jnp.where(mask, s, NEG_INF)
    m = jnp.max(s, axis=1, keepdims=True)
    m = jnp.where(m > NEG_INF, m, 0.0)
    p = jnp.exp(s - m)
    den = jnp.sum(p, axis=1, keepdims=True)
    return p / jnp.where(den > 0, den, 1.0)


def _stack_heads(q):
    return jnp.concatenate([q[:, j * HEAD_DIM:(j + 1) * HEAD_DIM] for j in range(GROUP)], axis=0)


def _block_importance(pc, n_cmp, n_sel):
    ncp = pc.shape[1]
    nsp = -(-n_sel // LANES) * LANES
    c_start = lax.broadcasted_iota(jnp.int32, (ncp, nsp), 0) * CMP_STRIDE
    s_start = lax.broadcasted_iota(jnp.int32, (ncp, nsp), 1) * SEL_BLOCK
    overlap = ((c_start < s_start + SEL_BLOCK) & (c_start + CMP_BLOCK > s_start)
               & (c_start < n_cmp * CMP_STRIDE)).astype(F32)
    return jnp.dot(pc, overlap, preferred_element_type=F32, precision=lax.Precision.HIGHEST)


def _force_blocks(imp, blk, cur):
    forced = (blk == 0) | (blk == cur) | (blk == cur - 1)
    imp = jnp.where(forced, FORCE_SCORE, imp)
    return jnp.where(blk > cur, -1.0, imp)


def _select_blocks_cols(pc, cur, n_cmp, n_sel):
    imp = _block_importance(pc, n_cmp, n_sel)
    blk = lax.broadcasted_iota(jnp.int32, (1, imp.shape[1]), 1)
    imp = _force_blocks(imp, blk, cur)

    def body(sp, count):
        col = jnp.sum(jnp.where(blk == sp, imp, 0.0), axis=1, keepdims=True)
        ahead = (col > imp) | ((col == imp) & (sp < blk))
        return count + ahead.astype(F32)

    count = lax.fori_loop(0, n_sel, body, jnp.zeros(imp.shape, F32), unroll=64)
    return ((count < min(SEL_TOPK, n_sel)) & (blk < n_sel)).astype(F32)


def _select_blocks_rows(pc, cur_row, n_cmp, n_sel):
    rows = pc.shape[0]
    nrow = -(-n_sel // SUBLANES) * SUBLANES
    imp_t = _block_importance(pc, n_cmp, n_sel).T[0:nrow, :]
    blk = lax.broadcasted_iota(jnp.int32, (nrow, 1), 0)
    imp_t = _force_blocks(imp_t, blk, cur_row)
    count = jnp.zeros(imp_t.shape, F32)
    for sp in range(n_sel):
        row = imp_t[sp:sp + 1, :]
        count += ((row > imp_t) | ((row == imp_t) & (sp < blk))).astype(F32)
    sel_t = ((count < min(SEL_TOPK, n_sel)) & (blk < n_sel)).astype(F32)
    sel_t = jnp.concatenate([sel_t, jnp.zeros((LANES - nrow, rows), F32)], axis=0)
    return sel_t.T


def _group_gates(pre, g):
    first = GATE_NG + g * (GROUP * 3)
    return pltpu.roll(jax.nn.sigmoid(pre), LANES - first, 1)


def _nsa_prompt_kernel(q_ref, ks_ref, vs_ref, kw_ref, vw_ref, kc_ref, vc_ref, bc_ref, tz_ref, gt_ref, gb_ref,
                       o_ref, s_scr, selb_scr, mx_scr, l_scr, acc_scr, *, t_len):
    g = pl.program_id(1)
    qt = pl.program_id(2)
    q0 = qt * TQ
    n_cmp = t_len // CMP_STRIDE - (CMP_BLOCK // CMP_STRIDE) + 1
    n_sel = t_len // SEL_BLOCK
    ncp = kc_ref.shape[0]
    rows4 = GROUP * TQ
    qs = _stack_heads(q_ref[...] * QK_SCALE).astype(BF16)
    qpos = q0 + (lax.broadcasted_iota(jnp.int32, (rows4, 1), 0) & (TQ - 1))

    n_idx = lax.broadcasted_iota(jnp.int32, (1, ncp), 1)
    s_c = _dot_nt(qs, kc_ref[...].astype(BF16)) + bc_ref[...].reshape(rows4, ncp)
    cmask = (qpos - (n_idx * CMP_STRIDE + CMP_BLOCK - 1) >= 0) & (n_idx < n_cmp)
    p_c = _masked_probs(s_c, cmask)
    o_c = _dot(p_c.astype(BF16), vc_ref[...].astype(BF16))
    pc = p_c[0:TQ] + p_c[TQ:2 * TQ] + p_c[2 * TQ:3 * TQ] + p_c[3 * TQ:4 * TQ]
    cur_row = jnp.right_shift(q0 + lax.broadcasted_iota(jnp.int32, (1, TQ), 1), SEL_SHIFT)
    sel = _select_blocks_rows(pc, cur_row, n_cmp, n_sel)
    sel_add = ((sel - 1.0) * -MASKED).astype(BF16)

    kt_hi = (q0 + TQ - 1) // TK + 1
    kt_far = jnp.maximum(q0 - (FAR_DIST - 1), 0) // TK
    kidx = lax.broadcasted_iota(jnp.int32, (1, TK), 1)
    srow = lax.broadcasted_iota(jnp.int32, (LANES, TK), 0)

    def spread_selection(kt, carry):
        k0 = kt * TK
        expand = (jnp.right_shift(k0 + lax.broadcasted_iota(jnp.int32, (LANES, TK), 1), SEL_SHIFT) == srow).astype(BF16)
        selb_scr[kt] = _dot(sel_add, expand)
        return carry

    lax.fori_loop(0, kt_hi, spread_selection, 0)

    def tile_bias(k0):
        return tz_ref[jnp.clip((q0 - k0) // TQ, 0, N_TZ - 1)].reshape(rows4, TK)

    def attend(k_ref, v_ref, kt_lo, mask_tile):
        mx_scr[...] = jnp.full(mx_scr.shape, NEG_INF, F32)

        def scores(near):
            def body(kt, carry):
                k0 = pl.multiple_of(kt * TK, TK)
                s = _dot_nt(qs, k_ref[pl.ds(k0, TK), :].astype(BF16))
                if near:
                    s = s + tile_bias(k0)
                s = mask_tile(k0, s, near)
                s_scr[kt] = s
                mx_scr[...] = jnp.maximum(mx_scr[...], jnp.maximum(s[:, 0:LANES], s[:, LANES:TK]))
                return carry
            return body

        kt_mid = jnp.maximum(kt_far, kt_lo)
        lax.fori_loop(kt_lo, kt_mid, scores(False), 0)
        lax.fori_loop(kt_mid, kt_hi, scores(True), 0)
        m = jnp.max(mx_scr[...], axis=1, keepdims=True)
        mx_scr[...] = jnp.broadcast_to(jnp.where(m > NEG_INF, m, 0.0), mx_scr.shape)
        l_scr[...] = jnp.zeros(l_scr.shape, F32)
        acc_scr[...] = jnp.zeros(acc_scr.shape, F32)

        def probs(kt, carry):
            k0 = pl.multiple_of(kt * TK, TK)
            m_rep = mx_scr[...]
            p = jnp.exp(s_scr[kt] - jnp.concatenate([m_rep, m_rep], axis=1))
            l_scr[...] += p[:, 0:LANES] + p[:, LANES:TK]
            acc_scr[...] += _dot(p.astype(BF16), v_ref[pl.ds(k0, TK), :].astype(BF16))
            return carry

        lax.fori_loop(kt_lo, kt_hi, probs, 0)
        l = jnp.sum(l_scr[...], axis=1, keepdims=True)
        return acc_scr[...] / jnp.where(l > 0, l, 1.0)

    def sel_mask(k0, s, near):
        s = (s.reshape(GROUP, TQ, TK) + selb_scr[k0 // TK][None]).reshape(rows4, TK)
        return jnp.where(k0 + kidx <= qpos, s, NEG_INF) if near else s

    o_s = attend(ks_ref, vs_ref, 0, sel_mask)

    def win_mask(k0, s, near):
        dist = qpos - (k0 + kidx)
        return jnp.where((dist >= 0) & (dist <= WINDOW), s, NEG_INF)

    o_w = attend(kw_ref, vw_ref, jnp.maximum(q0 - WINDOW, 0) // TK, win_mask)

    gates = _group_gates(gt_ref[...] + gb_ref[...], g)
    for j in range(GROUP):
        r = slice(j * TQ, (j + 1) * TQ)
        o_ref[:, j * HEAD_DIM:(j + 1) * HEAD_DIM] = (gates[:, 3 * j:3 * j + 1] * o_c[r] + gates[:, 3 * j + 1:3 * j + 2] * o_s[r]
                                                     + gates[:, 3 * j + 2:3 * j + 3] * o_w[r])


def _nsa_prompt(z, kcvc, bias_c, tz, gate_b):
    b, t, _ = z.shape
    ncp = kcvc.shape[2]
    nq = t // TQ
    kv = lambda slot: pl.BlockSpec((None, t, HEAD_DIM), lambda bi, g, qi: (bi, 0, C_NKV // HEAD_DIM + slot * KV_HEADS + g))
    cmp_blk = lambda slot: pl.BlockSpec((None, None, ncp, HEAD_DIM), lambda bi, g, qi: (bi, slot * KV_HEADS + g, 0, 0))
    return pl.pallas_call(
        functools.partial(_nsa_prompt_kernel, t_len=t),
        grid=(b, KV_HEADS, nq),
        in_specs=[pl.BlockSpec((None, TQ, GROUP * HEAD_DIM), lambda bi, g, qi: (bi, qi, C_NQ // (GROUP * HEAD_DIM) + g)),
                  kv(2), kv(3), kv(4), kv(5), cmp_blk(0), cmp_blk(1),
                  pl.BlockSpec((GROUP, TQ, ncp), lambda bi, g, qi: (g, qi, 0)),
                  pl.BlockSpec((N_TZ, GROUP, TQ, TK), lambda bi, g, qi: (0, g, 0, 0)),
                  pl.BlockSpec((None, TQ, LANES), lambda bi, g, qi: (bi, qi, C_GATE // LANES)),
                  pl.BlockSpec((1, LANES), lambda bi, g, qi: (0, 0))],
        out_specs=pl.BlockSpec((None, TQ, GROUP * HEAD_DIM), lambda bi, g, qi: (bi, qi, g)),
        out_shape=jax.ShapeDtypeStruct((b, t, D_NSA), F32),
        scratch_shapes=[pltpu.VMEM((t // TK, GROUP * TQ, TK), F32), pltpu.VMEM((t // TK, TQ, TK), F32),
                        pltpu.VMEM((GROUP * TQ, LANES), F32),
                        pltpu.VMEM((GROUP * TQ, LANES), F32), pltpu.VMEM((GROUP * TQ, HEAD_DIM), F32)],
        compiler_params=_cparams("parallel", "parallel", "arbitrary"),
        name="nsa_prompt",
    )(z, z, z, z, z, kcvc, kcvc, bias_c, tz, z, gate_b)


TS = SUBLANES
ROWS_S = GROUP * TS


def _nsa_s_cmp_kernel(q_ref, kc_ref, vc_ref, bc_ref, oc_ref, sel_ref, *, pos0, n_cmp, n_sel):
    nch = kc_ref.shape[0]
    qs = _stack_heads(q_ref[...] * QK_SCALE).astype(BF16)
    qpos = pos0 + (lax.broadcasted_iota(jnp.int32, (ROWS_S, 1), 0) & (TS - 1))
    n_idx = lax.broadcasted_iota(jnp.int32, (1, nch), 1)
    s_c = _dot_nt(qs, kc_ref[...].astype(BF16)) + bc_ref[...].reshape(ROWS_S, nch)
    cmask = (qpos - (n_idx * CMP_STRIDE + CMP_BLOCK - 1) >= 0) & (n_idx < n_cmp)
    p_c = _masked_probs(s_c, cmask)
    oc_ref[...] = _dot(p_c.astype(BF16), vc_ref[...].astype(BF16))
    pc = p_c[0:TS] + p_c[TS:2 * TS] + p_c[2 * TS:3 * TS] + p_c[3 * TS:4 * TS]
    sel_ref[...] = _select_blocks_cols(pc, jnp.right_shift(qpos[0:TS], SEL_SHIFT), n_cmp, n_sel)


def _nsa_sample_cmp(z8, kcvc, bias_c, *, pos0, n_cmp, n_sel):
    b = z8.shape[0]
    nch = kcvc.shape[2]
    nsp = -(-n_sel // LANES) * LANES
    cmp_blk = lambda slot: pl.BlockSpec((None, None, nch, HEAD_DIM), lambda bi, g: (bi, slot * KV_HEADS + g, 0, 0))
    return pl.pallas_call(
        functools.partial(_nsa_s_cmp_kernel, pos0=pos0, n_cmp=n_cmp, n_sel=n_sel),
        grid=(b, KV_HEADS),
        in_specs=[pl.BlockSpec((None, TS, GROUP * HEAD_DIM), lambda bi, g: (bi, 0, C_NQ // (GROUP * HEAD_DIM) + g)),
                  cmp_blk(0), cmp_blk(1),
                  pl.BlockSpec((GROUP, TS, nch), lambda bi, g: (g, 0, 0))],
        out_specs=[pl.BlockSpec((None, None, ROWS_S, HEAD_DIM), lambda bi, g: (bi, g, 0, 0)),
                   pl.BlockSpec((None, None, TS, nsp), lambda bi, g: (bi, g, 0, 0))],
        out_shape=[jax.ShapeDtypeStruct((b, KV_HEADS, ROWS_S, HEAD_DIM), F32),
                   jax.ShapeDtypeStruct((b, KV_HEADS, TS, nsp), F32)],
        compiler_params=_cparams("parallel", "parallel"),
        name="nsa_sample_cmp",
    )(z8, kcvc, kcvc, bias_c)


def _pad_rows(x, n):
    return jnp.concatenate([x, jnp.zeros((n - x.shape[0], x.shape[1]), x.dtype)], axis=0)


def _nsa_s_sel_kernel(pt_ref, q_ref, sel_ref, kn_ref, vn_ref, bt_ref, bn_ref, bf_ref, *refs, pos0, npages):
    del pt_ref
    rows_pp = PAGE_SIZE * KV_HEADS
    k_refs = [r.reshape(rows_pp, HEAD_DIM) for r in refs[:SEL_PAGES]]
    v_refs = [r.reshape(rows_pp, HEAD_DIM) for r in refs[SEL_PAGES:2 * SEL_PAGES]]
    o_ref, kbuf, vbuf, m_scr, l_scr, acc_scr = refs[2 * SEL_PAGES:]
    step = pl.program_id(1)
    last = step == npages // SEL_PAGES - 1
    rows = KV_HEADS * ROWS_S
    ncol = SEL_TILE * rows_pp
    head_shift = KV_HEADS.bit_length() - 1
    blk_per_page = PAGE_SIZE // SEL_BLOCK

    @pl.when(step == 0)
    def _():
        _softmax_init(m_scr, l_scr, acc_scr)

    nsp = sel_ref.shape[-1]
    ridx = lax.broadcasted_iota(jnp.int32, (rows, 1), 0)
    qpos = pos0 + (ridx & (TS - 1))
    qs = jnp.concatenate([_stack_heads(q_ref[:, g * GROUP * HEAD_DIM:(g + 1) * GROUP * HEAD_DIM] * QK_SCALE)
                          for g in range(KV_HEADS)], axis=0).astype(BF16)
    sel_rows = jnp.concatenate([sel_ref[g] for g in range(KV_HEADS) for _ in range(GROUP)], axis=0)
    bias_far = bf_ref[...].reshape(rows, PAGE_SIZE)[:, 0:1]
    col = lax.broadcasted_iota(jnp.int32, (1, ncol), 1)
    own_head = (col & (KV_HEADS - 1)) == ridx // ROWS_S
    expand = (jnp.right_shift(lax.broadcasted_iota(jnp.int32, (LANES, ncol), 1), SEL_SHIFT + head_shift)
              == lax.broadcasted_iota(jnp.int32, (LANES, ncol), 0)).astype(BF16)
    blk_r = lax.broadcasted_iota(jnp.int32, (nsp, LANES), 0)
    blk_c = lax.broadcasted_iota(jnp.int32, (nsp, LANES), 1)
    n_tiles = SEL_PAGES // SEL_TILE

    for tile in range(n_tiles):
        for i in range(SEL_TILE):
            p = tile * SEL_TILE + i
            kbuf[i * rows_pp:(i + 1) * rows_pp, :] = k_refs[p][...].astype(BF16)
            vbuf[i * rows_pp:(i + 1) * rows_pp, :] = v_refs[p][...].astype(BF16)
        page0 = step * SEL_PAGES + tile * SEL_TILE
        window = ((blk_r == page0 * blk_per_page + blk_c) & (blk_c < SEL_TILE * blk_per_page)).astype(BF16)
        chosen = _dot(_dot(sel_rows.astype(BF16), window).astype(BF16), expand) > 0.5
        s = _dot_nt(qs, kbuf[...]) + bias_far
        if tile == n_tiles - 1:
            tail = jnp.where(last, bt_ref[...].reshape(rows, rows_pp) - bias_far, 0.0)
            s = s + jnp.concatenate([jnp.zeros((rows, ncol - rows_pp), F32), tail], axis=1)
        key = page0 * PAGE_SIZE + jnp.right_shift(col, head_shift)
        _softmax_update(s, chosen & own_head & (key <= qpos), vbuf[...], m_scr, l_scr, acc_scr)

    @pl.when(last)
    def _():
        kidx = lax.broadcasted_iota(jnp.int32, (1, PAGE_SIZE), 1)
        new_blk = lax.broadcasted_iota(jnp.int32, (1, nsp), 1) == npages * blk_per_page
        flag = jnp.sum(jnp.where(new_blk, sel_rows, 0.0), axis=1, keepdims=True) > 0.5
        bias_new = bn_ref[...].reshape(rows, PAGE_SIZE)
        for g in range(KV_HEADS):
            r = slice(g * ROWS_S, (g + 1) * ROWS_S)
            kn = _pad_rows(kn_ref[:, g * HEAD_DIM:(g + 1) * HEAD_DIM], PAGE_SIZE).astype(BF16)
            vn = _pad_rows(vn_ref[:, g * HEAD_DIM:(g + 1) * HEAD_DIM], PAGE_SIZE).astype(BF16)
            mask = flag[r] & (pos0 + kidx <= qpos[r]) & (kidx < TS)
            _softmax_update(_dot_nt(qs[r], kn) + bias_new[r], mask, vn, m_scr.at[r], l_scr.at[r], acc_scr.at[r])
            o_ref[g] = _softmax_result(l_scr.at[r], acc_scr.at[r])


def _nsa_sample_sel(z8, sel, bias_tail, bias_new, bias_far, cache6, page_table, layer, *, pos0):
    b, npages = page_table.shape
    nsp = sel.shape[-1]
    steps = npages // SEL_PAGES
    slot_cols = KV_HEADS * HEAD_DIM
    new_rows = lambda slot: pl.BlockSpec((None, TS, slot_cols),
                                         lambda bi, si, pt: (bi, 0, (C_NKV + slot * slot_cols) // slot_cols))
    whole = lambda a: pl.BlockSpec(a.shape, lambda bi, si, pt: (0,) * a.ndim)
    grid_spec = pltpu.PrefetchScalarGridSpec(
        num_scalar_prefetch=1,
        grid=(b, steps),
        in_specs=[pl.BlockSpec((None, TS, D_NSA), lambda bi, si, pt: (bi, 0, C_NQ // D_NSA)),
                  pl.BlockSpec((None, KV_HEADS, TS, nsp), lambda bi, si, pt: (bi, 0, 0, 0)),
                  new_rows(2), new_rows(3), whole(bias_tail), whole(bias_new), whole(bias_far)]
                 + [_page_spec(layer, 2, p, SEL_PAGES, 2) for p in range(SEL_PAGES)]
                 + [_page_spec(layer, 3, p, SEL_PAGES, 2) for p in range(SEL_PAGES)],
        out_specs=pl.BlockSpec((None, KV_HEADS, ROWS_S, HEAD_DIM), lambda bi, si, pt: (bi, 0, 0, 0)),
        scratch_shapes=[pltpu.VMEM((SEL_TILE * PAGE_SIZE * KV_HEADS, HEAD_DIM), BF16),
                        pltpu.VMEM((SEL_TILE * PAGE_SIZE * KV_HEADS, HEAD_DIM), BF16),
                        pltpu.VMEM((KV_HEADS * ROWS_S, 1), F32), pltpu.VMEM((KV_HEADS * ROWS_S, 1), F32),
                        pltpu.VMEM((KV_HEADS * ROWS_S, HEAD_DIM), F32)],
    )
    return pl.pallas_call(
        functools.partial(_nsa_s_sel_kernel, pos0=pos0, npages=npages),
        grid_spec=grid_spec,
        out_shape=jax.ShapeDtypeStruct((b, KV_HEADS, ROWS_S, HEAD_DIM), F32),
        compiler_params=_cparams("parallel", "arbitrary"),
        name="nsa_sample_sel",
    )(page_table, z8, sel, z8, z8, bias_tail, bias_new, bias_far, *([cache6] * (2 * SEL_PAGES)))


def _nsa_s_win_kernel(q_ref, kw_ref, vw_ref, kn_ref, vn_ref, bw_ref, oc_ref, os_ref, gt_ref, gb_ref, o_ref,
                      m_scr, l_scr, acc_scr, *, pos0):
    g = pl.program_id(1)
    wb = kw_ref.shape[0]
    own = pl.ds(g, wb, stride=KV_HEADS)
    kw = kw_ref.reshape(wb * KV_HEADS, HEAD_DIM)[own, :]
    vw = vw_ref.reshape(wb * KV_HEADS, HEAD_DIM)[own, :]
    qs = _stack_heads(q_ref[...] * QK_SCALE).astype(BF16)
    qpos = pos0 + (lax.broadcasted_iota(jnp.int32, (ROWS_S, 1), 0) & (TS - 1))
    bias = bw_ref[...].reshape(ROWS_S, wb + PAGE_SIZE)
    _softmax_init(m_scr, l_scr, acc_scr)
    dist = qpos - (pos0 - wb + lax.broadcasted_iota(jnp.int32, (1, wb), 1))
    s = _dot_nt(qs, kw.astype(BF16)) + bias[:, 0:wb]
    _softmax_update(s, (dist >= 0) & (dist <= WINDOW), vw.astype(BF16), m_scr, l_scr, acc_scr)
    kidx = lax.broadcasted_iota(jnp.int32, (1, PAGE_SIZE), 1)
    dist = qpos - (pos0 + kidx)
    s = _dot_nt(qs, _pad_rows(kn_ref[...], PAGE_SIZE).astype(BF16)) + bias[:, wb:wb + PAGE_SIZE]
    _softmax_update(s, (dist >= 0) & (dist <= WINDOW) & (kidx < TS), _pad_rows(vn_ref[...], PAGE_SIZE).astype(BF16),
                    m_scr, l_scr, acc_scr)
    o_w = _softmax_result(l_scr, acc_scr)
    o_c = oc_ref[...]
    o_s = os_ref[...]
    gates = _group_gates(gt_ref[...] + gb_ref[...], g)
    for j in range(GROUP):
        r = slice(j * TS, (j + 1) * TS)
        o_ref[:, j * HEAD_DIM:(j + 1) * HEAD_DIM] = (gates[:, 3 * j:3 * j + 1] * o_c[r] + gates[:, 3 * j + 1:3 * j + 2] * o_s[r]
                                                     + gates[:, 3 * j + 2:3 * j + 3] * o_w[r])


def _nsa_sample_win(z8, win6, layer, bias_w, o_c, o_s, gate_b, *, pos0):
    _, b, wb = win6.shape[:3]
    cached = lambda slot: pl.BlockSpec((None, None, wb, None, KV_HEADS, HEAD_DIM),
                                       lambda bi, g: (layer, bi, 0, slot, 0, 0))
    zcol = lambda col: (lambda bi, g: (bi, 0, col // HEAD_DIM + g))
    part = pl.BlockSpec((None, None, ROWS_S, HEAD_DIM), lambda bi, g: (bi, g, 0, 0))
    return pl.pallas_call(
        functools.partial(_nsa_s_win_kernel, pos0=pos0),
        grid=(b, KV_HEADS),
        in_specs=[pl.BlockSpec((None, TS, GROUP * HEAD_DIM), lambda bi, g: (bi, 0, C_NQ // (GROUP * HEAD_DIM) + g)),
                  cached(0), cached(1),
                  pl.BlockSpec((None, TS, HEAD_DIM), zcol(C_NKV + 4 * KV_HEADS * HEAD_DIM)),
                  pl.BlockSpec((None, TS, HEAD_DIM), zcol(C_NKV + 5 * KV_HEADS * HEAD_DIM)),
                  pl.BlockSpec((GROUP, TS, wb + PAGE_SIZE), lambda bi, g: (g, 0, 0)),
                  part, part,
                  pl.BlockSpec((None, TS, LANES), lambda bi, g: (bi, 0, C_GATE // LANES)),
                  pl.BlockSpec((1, LANES), lambda bi, g: (0, 0))],
        out_specs=pl.BlockSpec((None, TS, GROUP * HEAD_DIM), lambda bi, g: (bi, 0, g)),
        out_shape=jax.ShapeDtypeStruct((b, TS, D_NSA), F32),
        scratch_shapes=[pltpu.VMEM((ROWS_S, 1), F32), pltpu.VMEM((ROWS_S, 1), F32), pltpu.VMEM((ROWS_S, HEAD_DIM), F32)],
        compiler_params=_cparams("parallel", "parallel"),
        name="nsa_sample_win",
    )(z8, win6, win6, z8, z8, bias_w, o_c, o_s, z8, gate_b)


def _t5_bucket(dist):
    n = np.maximum(dist, 0)
    exact = NUM_BUCKETS // 2
    nf = np.maximum(n, 1).astype(np.float32)
    large = exact + (np.log(nf / np.float32(exact)) / np.float32(math.log(MAX_DISTANCE / exact))
                     * np.float32(NUM_BUCKETS - exact)).astype(np.int32)
    return np.where(n < exact, n, np.minimum(large, NUM_BUCKETS - 1)).astype(np.int32)


FAR_DIST = int(np.max(np.nonzero(_t5_bucket(np.arange(4 * MAX_DISTANCE)) < NUM_BUCKETS - 1)[0])) + 1
N_TZ = (FAR_DIST + TK - 2) // TQ + 1


def _bias_table(rel_bias, dist):
    onehot = jax.nn.one_hot(jnp.asarray(_t5_bucket(dist).astype(np.int8)), NUM_BUCKETS, dtype=F32)
    return jnp.einsum("rcb,bh->hrc", onehot, rel_bias, precision=lax.Precision.HIGHEST)


def _pack_weights(p):
    w_in = p["w_in"].astype(BF16)
    pad = jnp.zeros(w_in.shape[:2] + (N_IN - C_GATE - 32,), BF16)
    w_in_p = jnp.concatenate([w_in[..., 0:2048], w_in[..., 2056:3080], w_in[..., 3080:4616], w_in[..., 4640:5152],
                              w_in[..., 2048:2056], w_in[..., 4616:4640], pad], axis=2)
    return dict(
        w_in=w_in_p, w_out=p["w_out"].astype(BF16), w_down=p["ffn_w_down"].astype(BF16),
        w_up=p["ffn_w_up"].astype(BF16))


def _pack_layer(p, big, l):
    gb = p["mlstm_gate_b"][l]
    mlstm_gate_b = jnp.zeros((1, LANES), F32).at[0, GATE_MI:GATE_MI + 4].set(gb[0]).at[0, GATE_MF:GATE_MF + 4].set(gb[1])
    nsa_gate_b = jnp.zeros((1, LANES), F32).at[0, GATE_NG:GATE_NG + 3 * NSA_HEADS].set(p["nsa_gate_b"][l].reshape(-1))
    w1 = p["cmp_w1"][l]
    wcat = jnp.concatenate([w1[:, 0:CMP_STRIDE], w1[:, CMP_STRIDE:CMP_BLOCK]], axis=-1).astype(BF16)
    lam = lax.complex(p["s5_a_re"][l], p["s5_a_im"][l])
    lam_bar = jnp.exp(lam * jnp.exp(p["s5_log_step"][l])[:, None])
    b_bar = ((lam_bar - 1.0) / lam)[..., None] * lax.complex(p["s5_b_re"][l], p["s5_b_im"][l])
    gi = S5_GROUPS // S5_IN_TILES
    bd_in = lambda m: jnp.einsum("cgph,gk->cghkp", m.reshape(S5_IN_TILES, gi, S5_STATE, S5_GROUP_WIDTH),
                                 jnp.eye(gi, dtype=F32)).reshape(S5_IN_TILES, S5_IN_FEATS, 256)
    wb = jnp.concatenate([bd_in(b_bar.real), bd_in(b_bar.imag)], axis=2).astype(BF16)
    go = S5_GROUPS // S5_OUT_TILES
    bd_out = lambda m: jnp.einsum("jghp,gk->jgpkh", m.reshape(S5_OUT_TILES, go, S5_GROUP_WIDTH, S5_STATE),
                                  jnp.eye(go, dtype=F32)).reshape(S5_OUT_TILES, S5_OUT_CH, 256)
    wc = jnp.stack([bd_out(p["s5_c_re"][l]), -bd_out(p["s5_c_im"][l])]).astype(BF16)
    half = lambda w: jnp.pad(w, ((0, 0), (0, D_FFP - D_FF)))
    conv_w = p["ffn_conv_w"][l]
    conv_b = p["ffn_conv_b"][l][None, :]
    return dict(
        big, layer=l, mlstm_gate_b=mlstm_gate_b, mlstm_norm_w=p["mlstm_norm_w"][l][None, :], nsa_gate_b=nsa_gate_b,
        wcat=wcat, cmp_b1=p["cmp_b1"][l], cmp_w2=p["cmp_w2"][l].astype(BF16),
        s5_wb=wb, s5_wc=wc, s5_lam_r=lam_bar.real.reshape(1, S5_CH), s5_lam_i=lam_bar.imag.reshape(1, S5_CH),
        s5_d=p["s5_d"][l][None, :], s5_glu_w=p["s5_glu_w"][l].astype(BF16), s5_glu_b=p["s5_glu_b"][l][None, :],
        ln1_w=p["ln1_w"][l][None, :], ln1_b=p["ln1_b"][l][None, :],
        conv_w8=jnp.pad(jnp.concatenate([half(conv_w[:, :D_FF]), half(conv_w[:, D_FF:])], axis=1),
                        ((0, SUBLANES - CONV_W), (0, 0))),
        conv_b=jnp.concatenate([half(conv_b[:, :D_FF]), half(conv_b[:, D_FF:])], axis=1),
        ln2_w=p["ln2_w"][l][None, :], ln2_b=p["ln2_b"][l][None, :],
    )


def _unpad_ff(x):
    return jnp.concatenate([x[..., :D_FF], x[..., D_FFP:D_FFP + D_FF]], axis=-1)


def _pad_ff(x):
    pad = [(0, 0)] * (x.ndim - 1) + [(0, D_FFP - D_FF)]
    return jnp.concatenate([jnp.pad(x[..., :D_FF], pad), jnp.pad(x[..., D_FF:], pad)], axis=-1)


def _mixer_tail(x2d, z, b, t, t_use, lw, o_mlstm, o_nsa, s5_state, conv_state8, tm):
    pad8 = lambda s: jnp.pad(s.reshape(b, S5_CH), ((0, SUBLANES - b), (0, 0)))
    o_s5, xr, xi = _s5(z.reshape(b, t, N_IN), lw["s5_wb"], lw["s5_wc"], lw["s5_lam_r"], lw["s5_lam_i"], lw["s5_d"],
                       lw["s5_glu_w"], lw["s5_glu_b"], pad8(s5_state[0]), pad8(s5_state[1]), lc=S5_LC, steps=t_use)
    o_s5 = o_s5.reshape(b * t, D_S5)
    layer = lw["layer"]
    x1 = _wout_ln(o_mlstm.reshape(b * t, D_MLSTM), o_nsa.reshape(b * t, D_NSA), o_s5, x2d,
                  lw["w_out"], layer, lw["ln1_w"], lw["ln1_b"], tm=512)
    s5_new = (xr[:b].reshape(b, S5_GROUPS, S5_STATE), xi[:b].reshape(b, S5_GROUPS, S5_STATE))
    if conv_state8 is None:
        x2, tail_a, tail_g = _ffn_prompt(x1.reshape(b, t, D_MODEL), lw["w_up"], lw["w_down"], layer, lw["conv_w8"],
                                         lw["conv_b"], lw["ln2_w"], lw["ln2_b"], tt=FFN_TT, tf=FFN_TF)
        keep = slice(SUBLANES - (CONV_W - 1), SUBLANES)
        conv_new = jnp.concatenate([tail_a[:, keep, :D_FF], tail_g[:, keep, :D_FF]], axis=-1)
        return x2.reshape(b * t, D_MODEL), s5_new, conv_new
    up = _matmul(x1, lw["w_up"], layer, tm=tm, tn=FFN_TF, n=2 * D_FFP, w_cols=_up_cols(FFN_TF))
    hgate = _convgate(up.reshape(b, t, 2 * D_FFP), conv_state8, lw["conv_w8"], lw["conv_b"], tf=FFN_TF)
    x2 = _down_ln(hgate.reshape(b * t, D_FFP), lw["w_down"], layer, x1, lw["ln2_w"], lw["ln2_b"], tm=512, tk=1408)
    conv_new = _unpad_ff(up.reshape(b, t, 2 * D_FFP)[:, t_use - (CONV_W - 1):t_use])
    return x2, s5_new, conv_new


def _prompt_layer(x2d, b, t, lw, rel_bias, rows_so_far):
    z, kv_rows, win_rows = _proj_in(x2d, lw["w_in"], lw["layer"], rows_so_far, tm=min(1024, b * t), tn=768)
    z3 = z.reshape(b, t, N_IN)
    zeros = lambda *s: jnp.zeros(s, F32)
    o_mlstm, c1, n1, m1 = _mlstm(z3, lw["mlstm_gate_b"], lw["mlstm_norm_w"],
                                 zeros(b, MLSTM_HEADS, HEAD_DIM, HEAD_DIM), zeros(b, MLSTM_HEADS, LANES),
                                 zeros(b, MLSTM_HEADS, LANES), L=MLSTM_L, lin=MLSTM_L, t_valid=t)
    kcvc = _cmp_finish(_cmp_project_prompt(z3, lw["wcat"]), lw["cmp_b1"], lw["cmp_w2"])
    ncp = t // CMP_STRIDE
    bias_c = _bias_table(rel_bias, np.arange(t)[:, None] - (np.arange(ncp) * CMP_STRIDE + CMP_BLOCK - 1)[None, :])
    ti = np.arange(TQ)[:, None] - np.arange(TK)[None, :]
    far = _bias_table(rel_bias, np.full((1, 1), FAR_DIST))
    tz = jnp.stack([_bias_table(rel_bias, d * TQ + ti) - far for d in range(N_TZ)])
    o_nsa = _nsa_prompt(z3, kcvc, bias_c, tz, lw["nsa_gate_b"])
    x2, s5_new, conv_new = _mixer_tail(
        x2d, z, b, t, t, lw, o_mlstm, o_nsa, (zeros(b, S5_GROUPS, S5_STATE), zeros(b, S5_GROUPS, S5_STATE)),
        None, tm=1024)
    state = (None, None, c1, n1, m1[:, :, 0], s5_new[0], s5_new[1], conv_new)
    return x2, state, (kv_rows, win_rows)


def _sample_layer(x2d, b, tn, lw, rel_bias, layer, cache6, page_table, win6, mlstm_state, s5_state, conv_state):
    npages = page_table.shape[1]
    pos0 = npages * PAGE_SIZE
    z = _matmul(x2d, lw["w_in"], lw["layer"], tm=1024, tn=768)
    z8 = z.reshape(b, TS, N_IN)
    c0, n0, m0 = mlstm_state
    o_mlstm, c1, n1, m1 = _mlstm(z8, lw["mlstm_gate_b"], lw["mlstm_norm_w"], c0, n0,
                                 jnp.broadcast_to(m0[:, :, None], (b, MLSTM_HEADS, LANES)),
                                 L=LANES, lin=TS, t_valid=tn)
    n_chunks = (pos0 + tn) // CMP_STRIDE
    n_cmp = n_chunks - CMP_BLOCK // CMP_STRIDE + 1
    n_sel = -(-(pos0 + tn) // SEL_BLOCK)
    kcvc = _cmp_finish(_cmp_project_sample(cache6, page_table, layer, lw["wcat"]), lw["cmp_b1"], lw["cmp_w2"])
    qpos = pos0 + np.arange(TS)[:, None]
    bias_c = _bias_table(rel_bias, qpos - (np.arange(n_chunks) * CMP_STRIDE + CMP_BLOCK - 1)[None, :])
    o_c, sel = _nsa_sample_cmp(z8, kcvc, bias_c, pos0=pos0, n_cmp=n_cmp, n_sel=n_sel)
    kk = np.arange(PAGE_SIZE)[None, :]
    bias_tail = _bias_table(rel_bias, np.repeat(qpos - (pos0 - PAGE_SIZE + kk), KV_HEADS, axis=1))
    bias_new = _bias_table(rel_bias, qpos - (pos0 + kk))
    bias_far = _bias_table(rel_bias, np.broadcast_to(qpos - (pos0 - 2 * PAGE_SIZE), (TS, PAGE_SIZE)))
    o_s = _nsa_sample_sel(z8, sel, bias_tail, bias_new, bias_far, cache6, page_table, layer, pos0=pos0)
    wb = win6.shape[2]
    wk = np.arange(wb)[None, :]
    bias_w = _bias_table(rel_bias, np.concatenate([qpos - (pos0 - wb + wk), qpos - (pos0 + kk)], axis=1))
    o_nsa = _nsa_sample_win(z8, win6, layer, bias_w, o_c, o_s, lw["nsa_gate_b"], pos0=pos0)
    conv_state8 = jnp.pad(_pad_ff(conv_state), ((0, 0), (SUBLANES - (CONV_W - 1), 0), (0, 0)))
    x2, s5_new, conv_new = _mixer_tail(x2d, z, b, TS, tn, lw, o_mlstm, o_nsa, s5_state, conv_state8, tm=1024)
    nkv = z8[:, :tn, C_NKV:C_NKV + N_KV_SLOTS * KV_HEADS * HEAD_DIM].reshape(b, tn, N_KV_SLOTS, KV_HEADS, HEAD_DIM)
    state = (nkv[:, :, :4], nkv[:, :, 4:], c1, n1, m1[:, :, 0], s5_new[0], s5_new[1], conv_new)
    return x2, state


def kernel(x_prompt, x_sample, cache_nsa_kv, cache_win_kv, state_mlstm_c, state_mlstm_n, state_mlstm_m,
           state_s5_re, state_s5_im, state_ffn_conv, page_table, w_in, mlstm_gate_b, mlstm_norm_w,
           nsa_gate_b, cmp_w1, cmp_b1, cmp_w2, rel_bias, s5_a_re, s5_a_im, s5_b_re, s5_b_im, s5_c_re,
           s5_c_im, s5_d, s5_log_step, s5_glu_w, s5_glu_b, w_out, ln1_w, ln1_b, ffn_w_up, ffn_conv_w,
           ffn_conv_b, ffn_w_down, ln2_w, ln2_b):
    params = dict(w_in=w_in, mlstm_gate_b=mlstm_gate_b, mlstm_norm_w=mlstm_norm_w, nsa_gate_b=nsa_gate_b,
                  cmp_w1=cmp_w1, cmp_b1=cmp_b1, cmp_w2=cmp_w2, s5_a_re=s5_a_re, s5_a_im=s5_a_im, s5_b_re=s5_b_re,
                  s5_b_im=s5_b_im, s5_c_re=s5_c_re, s5_c_im=s5_c_im, s5_d=s5_d, s5_log_step=s5_log_step,
                  s5_glu_w=s5_glu_w, s5_glu_b=s5_glu_b, w_out=w_out, ln1_w=ln1_w, ln1_b=ln1_b, ffn_w_up=ffn_w_up,
                  ffn_conv_w=ffn_conv_w, ffn_conv_b=ffn_conv_b, ffn_w_down=ffn_w_down, ln2_w=ln2_w, ln2_b=ln2_b)
    depth = w_in.shape[0]
    bp, tp, _ = x_prompt.shape
    bs, tn, _ = x_sample.shape
    assert tp % TK == 0 and tn < CMP_STRIDE and tn <= TS and SUBLANES % bp == 0 and SUBLANES % bs == 0
    assert (tn * bs) % SUBLANES == 0 and tp % min(FFN_TT, tp) == 0
    assert page_table.shape[1] % CMP_PAGES == 0 and cache_nsa_kv.shape[2] == PAGE_SIZE
    xp = x_prompt.reshape(bp * tp, D_MODEL)
    xs = jnp.pad(x_sample, ((0, 0), (0, TS - tn), (0, 0))).reshape(bs * TS, D_MODEL)
    p_states, s_states = [], []
    big = _pack_weights(params)
    rows_so_far = None
    for l in range(depth):
        lw = _pack_layer(params, big, l)
        xp, sp, rows_so_far = _prompt_layer(xp, bp, tp, lw, rel_bias, rows_so_far)
        xs, ss = _sample_layer(xs, bs, tn, lw, rel_bias, l, cache_nsa_kv, page_table, cache_win_kv,
                               (state_mlstm_c[l], jnp.pad(state_mlstm_n[l], ((0, 0), (0, 0), (0, LANES - HEAD_DIM))),
                                state_mlstm_m[l]),
                               (state_s5_re[l], state_s5_im[l]), state_ffn_conv[l])
        p_states.append(sp)
        s_states.append(ss)
    stk = lambda states, i: jnp.stack([s[i] for s in states])
    y_prompt = xp.reshape(bp, tp, D_MODEL)
    y_sample = xs.reshape(bs, TS, D_MODEL)[:, :tn]
    wrows = min(WINDOW, tp)
    nsa_kv_p = rows_so_far[0].reshape(depth, bp, tp, 4, KV_HEADS, HEAD_DIM)
    win_kv_p = rows_so_far[1].reshape(depth, bp, tp, 2, KV_HEADS, HEAD_DIM)[:, :, tp - wrows:]
    return (y_prompt, y_sample,
            nsa_kv_p, stk(s_states, 0), win_kv_p, stk(s_states, 1),
            stk(p_states, 2), stk(s_states, 2), stk(p_states, 3), stk(s_states, 3), stk(p_states, 4), stk(s_states, 4),
            stk(p_states, 5), stk(s_states, 5), stk(p_states, 6), stk(s_states, 6), stk(p_states, 7), stk(s_states, 7))
```

```python
import functools
import math

import jax
import jax.numpy as jnp
import numpy as np
from jax import lax
from jax.experimental import pallas as pl
from jax.experimental.pallas import tpu as pltpu

F32 = jnp.float32
BF16 = jnp.bfloat16
NEG_INF = float("-inf")
M_INIT = -1e30
MASKED = -1e30

D_MODEL = 2048
PAGE_SIZE = 128
D_MLSTM = D_MODEL // 4
D_NSA = D_MODEL // 2
D_S5 = D_MODEL - D_MLSTM - D_NSA
HEAD_DIM = 128
MLSTM_HEADS = D_MLSTM // HEAD_DIM
NSA_HEADS = D_NSA // HEAD_DIM
KV_HEADS = 2
GROUP = NSA_HEADS // KV_HEADS
N_KV_SLOTS = 6
CMP_BLOCK = 32
CMP_STRIDE = 16
SEL_BLOCK = 64
SEL_TOPK = 16
FORCE_SCORE = 1e4
WINDOW = 512
S5_GROUP_WIDTH = 16
S5_GROUPS = D_S5 // S5_GROUP_WIDTH
S5_STATE = 64
S5_CH = S5_GROUPS * S5_STATE
NUM_BUCKETS = 32
MAX_DISTANCE = 128
D_FF = ((8 * D_MODEL // 3 + 127) // 128) * 128
CONV_W = 3
LN_EPS = 1e-5
DEPTH = 2
DEEPNORM_ALPHA = (2 * DEPTH) ** 0.25
QK_SCALE = HEAD_DIM ** -0.5

LANES = 128
SUBLANES = 8
V7X_VMEM_LIMIT = 56 * 2 ** 20

C_MQ, C_MK, C_MV, C_MO = 0, 512, 1024, 1536
C_NQ = 2048
C_NKV = 3072
C_SU = 4608
C_GATE = 5120
N_IN = 5376
GATE_MI, GATE_MF, GATE_NG = 0, 4, 8
D_FFP = 5632
MLSTM_L = 512
TQ = 256
TK = 512
S5_LC = 256
FFN_TT = 1024
FFN_TF = 512
FFN_HALO = 16
CMP_PAGES = 32
SEL_PAGES = 32
SEL_TILE = 32
SEL_SHIFT = SEL_BLOCK.bit_length() - 1


def _cparams(*sem):
    return pltpu.CompilerParams(dimension_semantics=sem, vmem_limit_bytes=V7X_VMEM_LIMIT)


def _dot(a, b):
    return jnp.dot(a, b, preferred_element_type=F32)


def _dot_nt(a, b):
    return lax.dot_general(a, b, (((1,), (1,)), ((), ())), preferred_element_type=F32)


def _layer_norm(y, w, b):
    mu = jnp.mean(y, axis=-1, keepdims=True)
    d = y - mu
    var = jnp.mean(d * d, axis=-1, keepdims=True)
    return d * lax.rsqrt(var + LN_EPS) * w + b


def _mm_kernel(x_ref, w_ref, o_ref, *, realign_tile=None):
    acc = _dot(x_ref[...].astype(BF16), w_ref[...])
    if realign_tile is not None:
        acc = _realign(acc, pl.program_id(1) == realign_tile)
    o_ref[...] = acc.astype(o_ref.dtype)


UP_OVER = D_FFP - D_FF


def _up_cols(tf):
    nf, per = D_FFP // tf, tf // LANES
    return lambda j: (j * per + (j >= nf).astype(jnp.int32) * (D_FF // LANES - nf * per)
                      - (j == 2 * nf - 1).astype(jnp.int32) * (UP_OVER // LANES)) * LANES


def _realign(up, is_last):
    shifted = jnp.concatenate([up[:, UP_OVER:], jnp.zeros((up.shape[0], UP_OVER), up.dtype)], axis=1)
    return jnp.where(is_last, shifted, up)


def _matmul(x, w, layer, *, tm, tn, n=None, w_cols=None):
    m, k = x.shape
    n = w.shape[2] if n is None else n
    tm = min(tm, m)
    if w_cols is None:
        w_spec = pl.BlockSpec((None, k, tn), lambda i, j: (layer, 0, j))
    else:
        w_spec = pl.BlockSpec((None, pl.Element(k), pl.Element(tn)), lambda i, j: (layer, 0, w_cols(j)))
    return pl.pallas_call(
        functools.partial(_mm_kernel, realign_tile=None if w_cols is None else n // tn - 1),
        grid=(m // tm, n // tn),
        in_specs=[pl.BlockSpec((tm, k), lambda i, j: (i, 0)), w_spec],
        out_specs=pl.BlockSpec((tm, tn), lambda i, j: (i, j)),
        out_shape=jax.ShapeDtypeStruct((m, n), F32),
        compiler_params=_cparams("parallel", "parallel"),
        name="proj",
    )(x, w)


KV4_GROUPS = 4 * KV_HEADS
WIN_GROUPS = 2 * KV_HEADS


def _proj_in_kernel(x_ref, w_ref, z_ref, kv_ref, win_ref, xb_ref, *, tn):
    j = pl.program_id(1)

    @pl.when(j == 0)
    def _():
        xb_ref[...] = x_ref[...].astype(BF16)

    acc = _dot(xb_ref[...], w_ref[...])
    z_ref[...] = acc
    tm = acc.shape[0]
    per_tile = tn // HEAD_DIM
    first_tile = C_NKV // tn
    for tile in range(first_tile, first_tile + N_KV_SLOTS * KV_HEADS // per_tile):
        @pl.when(j == tile)
        def _(tile=tile):
            for c in range(per_tile):
                grp = (tile - first_tile) * per_tile + c
                val = acc[:, c * HEAD_DIM:(c + 1) * HEAD_DIM]
                if grp < KV4_GROUPS:
                    kv_ref[pl.ds(grp, tm, stride=KV4_GROUPS), :] = val
                else:
                    win_ref[pl.ds(grp - KV4_GROUPS, tm, stride=WIN_GROUPS), :] = val


def _proj_in_carry_kernel(x_ref, w_ref, kv_in_ref, win_in_ref, z_ref, kv_ref, win_ref, xb_ref, *, tn):
    del kv_in_ref, win_in_ref
    _proj_in_kernel(x_ref, w_ref, z_ref, kv_ref, win_ref, xb_ref, tn=tn)


def _proj_in(x, w, layer, rows_so_far, *, tm, tn):
    m, k = x.shape
    depth, _, n = w.shape
    nm = m // tm
    assert C_NKV % tn == 0 and (N_KV_SLOTS * KV_HEADS * HEAD_DIM) % tn == 0
    carried = () if rows_so_far is None else tuple(rows_so_far)
    return pl.pallas_call(
        functools.partial(_proj_in_carry_kernel if carried else _proj_in_kernel, tn=tn),
        grid=(nm, n // tn),
        in_specs=[pl.BlockSpec((tm, k), lambda i, j: (i, 0)), pl.BlockSpec((None, k, tn), lambda i, j: (layer, 0, j))]
                 + [pl.BlockSpec(memory_space=pl.ANY)] * len(carried),
        out_specs=[pl.BlockSpec((tm, tn), lambda i, j: (i, j)),
                   pl.BlockSpec((tm * KV4_GROUPS, HEAD_DIM), lambda i, j: (layer * nm + i, 0)),
                   pl.BlockSpec((tm * WIN_GROUPS, HEAD_DIM), lambda i, j: (layer * nm + i, 0))],
        out_shape=[jax.ShapeDtypeStruct((m, n), F32),
                   jax.ShapeDtypeStruct((depth * m * KV4_GROUPS, HEAD_DIM), F32),
                   jax.ShapeDtypeStruct((depth * m * WIN_GROUPS, HEAD_DIM), F32)],
        input_output_aliases={2: 1, 3: 2} if carried else {},
        scratch_shapes=[pltpu.VMEM((tm, k), BF16)],
        compiler_params=_cparams("parallel", "arbitrary"),
        name="proj_in",
    )(x, w, *carried)


def _wout_kernel(om_ref, on_ref, os_ref, x_ref, w_ref, lw_ref, lb_ref, o_ref):
    acc = _dot(om_ref[...].astype(BF16), w_ref[0:D_MLSTM, :])
    acc += _dot(on_ref[...].astype(BF16), w_ref[D_MLSTM:D_MLSTM + D_NSA, :])
    acc += _dot(os_ref[...].astype(BF16), w_ref[D_MLSTM + D_NSA:D_MODEL, :])
    o_ref[...] = _layer_norm(DEEPNORM_ALPHA * x_ref[...] + acc, lw_ref[...], lb_ref[...])


def _wout_ln(om, on, os_, x, w, layer, lw, lb, *, tm):
    m = x.shape[0]
    tm = min(tm, m)
    row = lambda width: pl.BlockSpec((tm, width), lambda i: (i, 0))
    full = lambda a: pl.BlockSpec(a.shape, lambda i: (0, 0))
    w_spec = pl.BlockSpec((None,) + w.shape[1:], lambda i: (layer, 0, 0))
    return pl.pallas_call(
        _wout_kernel,
        grid=(m // tm,),
        in_specs=[row(D_MLSTM), row(D_NSA), row(D_S5), row(D_MODEL), w_spec, full(lw), full(lb)],
        out_specs=row(D_MODEL),
        out_shape=jax.ShapeDtypeStruct((m, D_MODEL), F32),
        compiler_params=_cparams("parallel"),
        name="wout_ln",
    )(om, on, os_, x, w, lw, lb)


def _valid_rows(w_ref, tile):
    rows = w_ref.shape[0]
    row = tile * rows + lax.broadcasted_iota(jnp.int32, (rows, 1), 0)
    w = w_ref[...]
    return jnp.where(row < D_FF, w, jnp.zeros_like(w))


def _down_kernel(h_ref, w_ref, x_ref, lw_ref, lb_ref, o_ref, acc_ref, *, nk):
    k = pl.program_id(1)

    @pl.when(k == 0)
    def _():
        acc_ref[...] = jnp.zeros_like(acc_ref)

    acc_ref[...] += _dot(h_ref[...], _valid_rows(w_ref, k))

    @pl.when(k == nk - 1)
    def _():
        o_ref[...] = _layer_norm(DEEPNORM_ALPHA * x_ref[...] + acc_ref[...], lw_ref[...], lb_ref[...])


def _down_ln(h, w, layer, x, lw, lb, *, tm, tk):
    m, kk = h.shape
    tm = min(tm, m)
    nk = kk // tk
    return pl.pallas_call(
        functools.partial(_down_kernel, nk=nk),
        grid=(m // tm, nk),
        in_specs=[
            pl.BlockSpec((tm, tk), lambda i, k: (i, k)),
            pl.BlockSpec((None, tk, D_MODEL), lambda i, k: (layer, k, 0)),
            pl.BlockSpec((tm, D_MODEL), lambda i, k: (i, 0)),
            pl.BlockSpec((1, D_MODEL), lambda i, k: (0, 0)),
            pl.BlockSpec((1, D_MODEL), lambda i, k: (0, 0)),
        ],
        out_specs=pl.BlockSpec((tm, D_MODEL), lambda i, k: (i, 0)),
        out_shape=jax.ShapeDtypeStruct((m, D_MODEL), F32),
        scratch_shapes=[pltpu.VMEM((tm, D_MODEL), F32)],
        compiler_params=_cparams("parallel", "arbitrary"),
        name="down_ln",
    )(h, w, x, lw, lb)


def _convgate_kernel(a_ref, g_ref, sa_ref, sg_ref, wa_ref, wg_ref, ba_ref, bg_ref, o_ref):
    rows = lax.broadcasted_iota(jnp.int32, (1, SUBLANES, 1), 1)

    def conv(cur_ref, st_ref, w_ref, b_ref):
        cur = cur_ref[...]
        prev = st_ref[...]
        p1 = prev[:, 7:8, :]
        p2 = prev[:, 6:7, :]
        x1 = jnp.where(rows == 0, p1, pltpu.roll(cur, 1, 1))
        x2 = jnp.where(rows == 0, p2, jnp.where(rows == 1, p1, pltpu.roll(cur, 2, 1)))
        w = w_ref[...]
        return b_ref[...] + w[0:1, :] * x2 + w[1:2, :] * x1 + w[2:3, :] * cur

    a = conv(a_ref, sa_ref, wa_ref, ba_ref)
    g = conv(g_ref, sg_ref, wg_ref, bg_ref)
    o_ref[...] = (a * jax.nn.sigmoid(a) * g).astype(o_ref.dtype)


def _convgate(up, state8, conv_w8, conv_b, *, tf):
    b, t, _ = up.shape
    assert t == SUBLANES
    nf = D_FFP // tf
    cur_a = pl.BlockSpec((b, t, tf), lambda fi: (0, 0, fi))
    cur_g = pl.BlockSpec((b, t, tf), lambda fi: (0, 0, fi + nf))
    w_a = pl.BlockSpec((SUBLANES, tf), lambda fi: (0, fi))
    w_g = pl.BlockSpec((SUBLANES, tf), lambda fi: (0, fi + nf))
    b_a = pl.BlockSpec((1, tf), lambda fi: (0, fi))
    b_g = pl.BlockSpec((1, tf), lambda fi: (0, fi + nf))
    return pl.pallas_call(
        _convgate_kernel,
        grid=(nf,),
        in_specs=[cur_a, cur_g, cur_a, cur_g, w_a, w_g, b_a, b_g],
        out_specs=pl.BlockSpec((b, t, tf), lambda fi: (0, 0, fi)),
        out_shape=jax.ShapeDtypeStruct((b, t, D_FFP), BF16),
        compiler_params=_cparams("parallel"),
        name="convgate",
    )(up, up, state8, state8, conv_w8, conv_w8, conv_b, conv_b)


def _ffn_kernel(x_ref, xh_ref, wa_ref, wg_ref, wd_ref, cwa_ref, cwg_ref, cba_ref, cbg_ref, lw_ref, lb_ref,
                o_ref, sa_ref, sg_ref, xe_ref, *, nf):
    first = pl.program_id(1) == 0
    f = pl.program_id(2)
    tt = x_ref.shape[0]

    @pl.when(f == 0)
    def _():
        o_ref[...] = jnp.zeros_like(o_ref)
        halo = jnp.where(first, 0.0, xh_ref[...])
        xe_ref[0:FFN_HALO, :] = halo.astype(BF16)
        xe_ref[FFN_HALO:, :] = x_ref[...].astype(BF16)

    xe = xe_ref[...]

    def branch(w_ref, cw_ref, cb_ref, s_ref, second_half):
        up = _dot(xe, w_ref[...])
        if second_half:
            up = _realign(up, f == nf - 1)
        s_ref[...] = up[FFN_HALO + tt - SUBLANES:FFN_HALO + tt, :]
        w = cw_ref[...]
        x1 = pltpu.roll(up, 1, 0)[FFN_HALO:, :]
        x2 = pltpu.roll(up, 2, 0)[FFN_HALO:, :]
        return cb_ref[...] + w[0:1, :] * x2 + w[1:2, :] * x1 + w[2:3, :] * up[FFN_HALO:, :]

    a = branch(wa_ref, cwa_ref, cba_ref, sa_ref, False)
    g = branch(wg_ref, cwg_ref, cbg_ref, sg_ref, True)
    o_ref[...] += _dot((a * jax.nn.sigmoid(a) * g).astype(BF16), _valid_rows(wd_ref, f))

    @pl.when(f == nf - 1)
    def _():
        o_ref[...] = _layer_norm(DEEPNORM_ALPHA * x_ref[...] + o_ref[...], lw_ref[...], lb_ref[...])


def _ffn_prompt(x3, w_up, w_down, layer, conv_w8, conv_b, lw, lb, *, tt, tf):
    b, t, _ = x3.shape
    nf = D_FFP // tf
    tt = min(tt, t)
    hb = tt // FFN_HALO
    half_a = lambda rows: pl.BlockSpec((rows, tf), lambda bi, ti, fi: (0, fi))
    half_g = lambda rows: pl.BlockSpec((rows, tf), lambda bi, ti, fi: (0, fi + nf))
    cols = _up_cols(tf)
    w_a = pl.BlockSpec((None, pl.Element(D_MODEL), pl.Element(tf)), lambda bi, ti, fi: (layer, 0, cols(fi)))
    w_g = pl.BlockSpec((None, pl.Element(D_MODEL), pl.Element(tf)), lambda bi, ti, fi: (layer, 0, cols(fi + nf)))
    vec = pl.BlockSpec((1, D_MODEL), lambda bi, ti, fi: (0, 0))
    tail = pl.BlockSpec((None, SUBLANES, tf), lambda bi, ti, fi: (bi, 0, fi))
    return pl.pallas_call(
        functools.partial(_ffn_kernel, nf=nf),
        grid=(b, t // tt, nf),
        in_specs=[pl.BlockSpec((None, tt, D_MODEL), lambda bi, ti, fi: (bi, ti, 0), pipeline_mode=pl.Buffered(1)),
                  pl.BlockSpec((None, FFN_HALO, D_MODEL), lambda bi, ti, fi: (bi, jnp.maximum(ti * hb - 1, 0), 0)),
                  w_a, w_g,
                  pl.BlockSpec((None, tf, D_MODEL), lambda bi, ti, fi: (layer, fi, 0)),
                  half_a(SUBLANES), half_g(SUBLANES), half_a(1), half_g(1), vec, vec],
        out_specs=[pl.BlockSpec((None, tt, D_MODEL), lambda bi, ti, fi: (bi, ti, 0)), tail, tail],
        out_shape=[jax.ShapeDtypeStruct((b, t, D_MODEL), F32),
                   jax.ShapeDtypeStruct((b, SUBLANES, D_FFP), F32),
                   jax.ShapeDtypeStruct((b, SUBLANES, D_FFP), F32)],
        scratch_shapes=[pltpu.VMEM((tt + FFN_HALO, D_MODEL), BF16)],
        compiler_params=_cparams("parallel", "arbitrary", "arbitrary"),
        name="ffn_prompt",
    )(x3, x3, w_up, w_up, w_down, conv_w8, conv_w8, conv_b, conv_b, lw, lb)


def _cumsum_rows(x):
    n = x.shape[0]
    rows = lax.broadcasted_iota(jnp.int32, (n, 1), 0)
    d = 1
    while d < n:
        x = x + jnp.where(rows >= d, pltpu.roll(x, d, 0), 0.0)
        d *= 2
    return x


def _log_sigmoid(x):
    return jnp.minimum(x, 0.0) - jnp.log1p(jnp.exp(-jnp.abs(x)))


def _mlstm_kernel(q_ref, k_ref, v_ref, og_ref, g_ref, gb_ref, nw_ref, c0_ref, n0_ref, m0_ref,
                  out_ref, c_ref, n_ref, m_ref, *, L, t_valid):
    ci = pl.program_id(1)
    lin = q_ref.shape[0]

    @pl.when(ci == 0)
    def _():
        c_ref[...] = c0_ref[...]
        n_ref[...] = n0_ref[...]
        m_ref[...] = m0_ref[...]

    def rows_of(ref):
        x = ref[...]
        if lin < L:
            x = jnp.concatenate([x, jnp.zeros((L - lin, x.shape[1]), x.dtype)], axis=0)
        return x

    rows = lax.broadcasted_iota(jnp.int32, (L, 1), 0)
    valid = (ci * L + rows) < t_valid
    pre = rows_of(g_ref) + gb_ref[...]
    lf = jnp.where(valid, _log_sigmoid(pre), 0.0)
    ig = jnp.where(valid, pre, NEG_INF)
    bcum = _cumsum_rows(lf)
    dt = (pltpu.roll(ig, GATE_MF - GATE_MI, 1) - bcum).T
    q_all, k_all, v_all, og_all = rows_of(q_ref), rows_of(k_ref), rows_of(v_ref), rows_of(og_ref)
    tri = lax.broadcasted_iota(jnp.int32, (L, L), 0) >= lax.broadcasted_iota(jnp.int32, (L, L), 1)
    nw = nw_ref[...]

    for h in range(MLSTM_HEADS):
        sl = slice(h * HEAD_DIM, (h + 1) * HEAD_DIM)
        q = q_all[:, sl]
        k = k_all[:, sl] * QK_SCALE
        v = v_all[:, sl]
        qb, kb, vb = q.astype(BF16), k.astype(BF16), v.astype(BF16)
        b_col = bcum[:, GATE_MF + h:GATE_MF + h + 1]
        ig_col = ig[:, GATE_MI + h:GATE_MI + h + 1]
        d_row = dt[GATE_MF + h:GATE_MF + h + 1, :]
        c_prev = c_ref[h]
        n_prev = n_ref[h:h + 1, :]
        m_prev = m_ref[h:h + 1, 0:1]

        dmat = jnp.where(tri, b_col + d_row, NEG_INF)
        g_col = b_col + m_prev
        m_row = jnp.maximum(jnp.max(dmat, axis=1, keepdims=True), g_col)
        a = jnp.exp(dmat - m_row) * _dot_nt(qb, kb)
        w_inter = jnp.exp(g_col - m_row)
        num = _dot(a.astype(BF16), vb) + w_inter * _dot(qb, c_prev.astype(BF16))
        den = jnp.sum(a, axis=1, keepdims=True) + w_inter * jnp.sum(q * n_prev, axis=1, keepdims=True)
        hid = num / jnp.maximum(jnp.abs(den), jnp.exp(-m_row))
        mu = jnp.mean(hid, axis=1, keepdims=True)
        dlt = hid - mu
        var = jnp.mean(dlt * dlt, axis=1, keepdims=True)
        hn = dlt * lax.rsqrt(var + LN_EPS) * nw[:, sl]
        res = jax.nn.sigmoid(og_all[:, sl]) * hn
        out_ref[:, sl] = res[0:lin, :]

        f_tot = b_col[L - 1:L, :]
        w_s = f_tot - b_col + ig_col
        m_new = jnp.maximum(f_tot + m_prev, jnp.max(w_s, axis=0, keepdims=True))
        ws = jnp.exp(w_s - m_new)
        decay = jnp.exp(f_tot + m_prev - m_new)
        c_ref[h] = decay * c_prev + _dot(k.T.astype(BF16), (ws * v).astype(BF16))
        n_ref[h:h + 1, :] = decay * n_prev + jnp.sum(ws * k, axis=0, keepdims=True)
        m_ref[h:h + 1, :] = jnp.broadcast_to(m_new, (1, LANES))


def _mlstm(z, gate_b, norm_w, c0, n0, m0, *, L, lin, t_valid):
    b, tz, _ = z.shape
    nchunks = tz // lin
    qblk = lambda col: pl.BlockSpec((None, lin, D_MLSTM), lambda bi, ci: (bi, ci, col // D_MLSTM))
    st4 = pl.BlockSpec((None, MLSTM_HEADS, HEAD_DIM, HEAD_DIM), lambda bi, ci: (bi, 0, 0, 0))
    st3 = pl.BlockSpec((None, MLSTM_HEADS, LANES), lambda bi, ci: (bi, 0, 0))
    return pl.pallas_call(
        functools.partial(_mlstm_kernel, L=L, t_valid=t_valid),
        grid=(b, nchunks),
        in_specs=[qblk(C_MQ), qblk(C_MK), qblk(C_MV), qblk(C_MO),
                  pl.BlockSpec((None, lin, LANES), lambda bi, ci: (bi, ci, C_GATE // LANES)),
                  pl.BlockSpec((1, LANES), lambda bi, ci: (0, 0)),
                  pl.BlockSpec((1, D_MLSTM), lambda bi, ci: (0, 0)),
                  st4, st3, st3],
        out_specs=[pl.BlockSpec((None, lin, D_MLSTM), lambda bi, ci: (bi, ci, 0)), st4, st3, st3],
        out_shape=[jax.ShapeDtypeStruct((b, tz, D_MLSTM), F32),
                   jax.ShapeDtypeStruct((b, MLSTM_HEADS, HEAD_DIM, HEAD_DIM), F32),
                   jax.ShapeDtypeStruct((b, MLSTM_HEADS, LANES), F32),
                   jax.ShapeDtypeStruct((b, MLSTM_HEADS, LANES), F32)],
        compiler_params=_cparams("parallel", "arbitrary"),
        name="mlstm",
    )(z, z, z, z, z, gate_b, norm_w, c0, n0, m0)


S5_IN_TILES = S5_CH // 256
S5_IN_FEATS = D_S5 // S5_IN_TILES
S5_OUT_TILES = D_S5 // 256
S5_OUT_CH = S5_CH // S5_OUT_TILES


S5_LT = S5_CH // LANES


def _lane_tiles(x):
    return jnp.stack([x[:, k * LANES:(k + 1) * LANES] for k in range(x.shape[1] // LANES)])


def _lane_untile(x):
    return jnp.concatenate([x[k] for k in range(x.shape[0])], axis=1)


def _s5_kernel(u_ref, wb_ref, wc_ref, lr_ref, li_ref, d_ref, gw_ref, gb_ref, x0r_ref, x0i_ref,
               o_ref, xr_ref, xi_ref, sr_ref, si_ref, y_ref, *, steps):
    nb, lc, _ = u_ref.shape

    @pl.when(pl.program_id(0) == 0)
    def _():
        xr_ref[...] = x0r_ref[...]
        xi_ref[...] = x0i_ref[...]

    for b in range(nb):
        ub = u_ref[b].astype(BF16)
        for c in range(S5_IN_TILES):
            bu = _dot(ub[:, c * S5_IN_FEATS:(c + 1) * S5_IN_FEATS], wb_ref[c])
            for k in range(256 // LANES):
                rows = pl.ds(b, lc, stride=nb)
                sr_ref.at[c * (256 // LANES) + k][rows, :] = bu[:, k * LANES:(k + 1) * LANES]
                si_ref.at[c * (256 // LANES) + k][rows, :] = bu[:, 256 + k * LANES:256 + (k + 1) * LANES]
    lam_r = _lane_tiles(jnp.broadcast_to(lr_ref[...], (SUBLANES, S5_CH)))
    lam_i = _lane_tiles(jnp.broadcast_to(li_ref[...], (SUBLANES, S5_CH)))
    row8 = lax.broadcasted_iota(jnp.int32, (1, SUBLANES, 1), 1)
    per_group = SUBLANES // nb

    def group(i, carry):
        xr, xi = carry
        r0 = pl.multiple_of(i * SUBLANES, SUBLANES)
        br = sr_ref[:, pl.ds(r0, SUBLANES), :]
        bi = si_ref[:, pl.ds(r0, SUBLANES), :]
        out_r, out_i = br, bi
        for k in range(per_group):
            nr = lam_r * xr - lam_i * xi + br
            ni = lam_r * xi + lam_i * xr + bi
            here = (row8 >= k * nb) & (row8 < (k + 1) * nb)
            out_r = jnp.where(here, nr, out_r)
            out_i = jnp.where(here, ni, out_i)
            if per_group > 1:
                xr = pltpu.roll(nr, nb, 1)
                xi = pltpu.roll(ni, nb, 1)
            else:
                xr, xi = nr, ni
        sr_ref[:, pl.ds(r0, SUBLANES), :] = out_r
        si_ref[:, pl.ds(r0, SUBLANES), :] = out_i
        return xr, xi

    xr, xi = lax.fori_loop(0, steps * nb // SUBLANES, group, (_lane_tiles(xr_ref[...]), _lane_tiles(xi_ref[...])))
    xr_ref[...] = _lane_untile(xr)
    xi_ref[...] = _lane_untile(xi)
    per_out = S5_OUT_CH // LANES
    for j in range(S5_OUT_TILES):
        lhs_r = jnp.concatenate([sr_ref[j * per_out + k] for k in range(per_out)], axis=1).astype(BF16)
        lhs_i = jnp.concatenate([si_ref[j * per_out + k] for k in range(per_out)], axis=1).astype(BF16)
        y = _dot(lhs_r, wc_ref[0, j]) + _dot(lhs_i, wc_ref[1, j])
        for k in range(256 // LANES):
            y_ref[j * (256 // LANES) + k] = y[:, k * LANES:(k + 1) * LANES]
    for b in range(nb):
        rows = pl.ds(b, lc, stride=nb)
        y = jnp.concatenate([y_ref.at[k][rows, :] for k in range(D_S5 // LANES)], axis=1) + d_ref[...] * u_ref[b]
        zz = jax.nn.gelu(y)
        o_ref[b] = zz * jax.nn.sigmoid(_dot(zz.astype(BF16), gw_ref[...]) + gb_ref[...])


def _s5(z3, wb, wc, lam_r, lam_i, d, glu_w, glu_b, x0r, x0i, *, lc, steps):
    b, tz, _ = z3.shape
    lc = min(lc, tz)
    steps = min(steps, lc)
    assert steps == lc or tz == lc
    full = lambda a: pl.BlockSpec(a.shape, lambda i: (0,) * a.ndim)
    st = pl.BlockSpec((SUBLANES, S5_CH), lambda i: (0, 0))
    return pl.pallas_call(
        functools.partial(_s5_kernel, steps=steps),
        grid=(tz // lc,),
        in_specs=[pl.BlockSpec((b, lc, D_S5), lambda i: (0, i, C_SU // D_S5)),
                  full(wb), full(wc), full(lam_r), full(lam_i), full(d), full(glu_w), full(glu_b), st, st],
        out_specs=[pl.BlockSpec((b, lc, D_S5), lambda i: (0, i, 0)), st, st],
        out_shape=[jax.ShapeDtypeStruct((b, tz, D_S5), F32),
                   jax.ShapeDtypeStruct((SUBLANES, S5_CH), F32),
                   jax.ShapeDtypeStruct((SUBLANES, S5_CH), F32)],
        scratch_shapes=[pltpu.VMEM((S5_LT, lc * b, LANES), F32), pltpu.VMEM((S5_LT, lc * b, LANES), F32),
                        pltpu.VMEM((D_S5 // LANES, lc * b, LANES), F32)],
        compiler_params=_cparams("arbitrary"),
        name="s5",
    )(z3, wb, wc, lam_r, lam_i, d, glu_w, glu_b, x0r, x0i)


N_CMB = 2 * KV_HEADS


def _cmp_project(rows_of, w_ref, o_ref):
    nch = o_ref.shape[0]
    for cmb in range(N_CMB):
        slot = cmb // KV_HEADS
        rows_ref = rows_of(cmb)
        acc = jnp.zeros((nch, 2 * HEAD_DIM), F32)
        for l in range(CMP_STRIDE):
            x = rows_ref[pl.ds(l, nch, stride=CMP_STRIDE), :]
            acc += _dot(x.astype(BF16), w_ref[slot, l])
        o_ref[:, cmb * 2 * HEAD_DIM:(cmb + 1) * 2 * HEAD_DIM] = acc


def _cmp_p_kernel(r0_ref, r1_ref, r2_ref, r3_ref, w_ref, o_ref):
    rows = (r0_ref, r1_ref, r2_ref, r3_ref)
    _cmp_project(lambda cmb: rows[cmb], w_ref, o_ref)


def _cmp_project_prompt(z, wcat):
    b, t, _ = z.shape
    nch = t // CMP_STRIDE
    rows = lambda cmb: pl.BlockSpec((None, t, HEAD_DIM), lambda bi: (bi, 0, C_NKV // HEAD_DIM + cmb))
    return pl.pallas_call(
        _cmp_p_kernel,
        grid=(b,),
        in_specs=[rows(cmb) for cmb in range(N_CMB)] + [pl.BlockSpec(wcat.shape, lambda bi: (0, 0, 0, 0))],
        out_specs=pl.BlockSpec((None, nch, N_CMB * 2 * HEAD_DIM), lambda bi: (bi, 0, 0)),
        out_shape=jax.ShapeDtypeStruct((b, nch, N_CMB * 2 * HEAD_DIM), F32),
        compiler_params=_cparams("parallel"),
        name="cmp_project_prompt",
    )(z, z, z, z, wcat)


def _cmp_s_kernel(pt_ref, *refs):
    del pt_ref
    per_pos = 2 * KV_HEADS
    page_refs = [r.reshape(PAGE_SIZE * per_pos, HEAD_DIM) for r in refs[:CMP_PAGES]]
    w_ref, o_ref = refs[CMP_PAGES:]
    per_page = PAGE_SIZE // CMP_STRIDE
    for cmb in range(N_CMB):
        acc = jnp.zeros((o_ref.shape[0], 2 * HEAD_DIM), F32)
        for l in range(CMP_STRIDE):
            x = jnp.concatenate(
                [page_refs[p][pl.ds(l * per_pos + cmb, per_page, stride=CMP_STRIDE * per_pos), :]
                 for p in range(CMP_PAGES)], axis=0)
            acc += _dot(x.astype(BF16), w_ref[cmb // KV_HEADS, l])
        o_ref[:, cmb * 2 * HEAD_DIM:(cmb + 1) * 2 * HEAD_DIM] = acc


def _page_spec(layer, slot, p, pages_per_step, grid_rank):
    def index_map(*idx):
        bi, si, pt = idx[0], idx[grid_rank - 1], idx[grid_rank]
        return (layer, pt[bi, si * pages_per_step + p], 0, slot, 0, 0)

    return pl.BlockSpec((None, None, PAGE_SIZE, None, KV_HEADS, HEAD_DIM), index_map)


def _cmp_project_sample(cache6, page_table, layer, wcat):
    b, npages = page_table.shape
    steps = npages // CMP_PAGES
    nch = CMP_PAGES * PAGE_SIZE // CMP_STRIDE
    def page_pair(p):
        return pl.BlockSpec((None, None, PAGE_SIZE, 2, KV_HEADS, HEAD_DIM),
                            lambda bi, si, pt: (layer, pt[bi, si * CMP_PAGES + p], 0, 0, 0, 0))

    pages = [page_pair(p) for p in range(CMP_PAGES)]

    grid_spec = pltpu.PrefetchScalarGridSpec(
        num_scalar_prefetch=1,
        grid=(b, steps),
        in_specs=pages + [pl.BlockSpec(wcat.shape, lambda bi, si, pt: (0, 0, 0, 0))],
        out_specs=pl.BlockSpec((None, nch, N_CMB * 2 * HEAD_DIM), lambda bi, si, pt: (bi, si, 0)),
    )
    return pl.pallas_call(
        _cmp_s_kernel,
        grid_spec=grid_spec,
        out_shape=jax.ShapeDtypeStruct((b, steps * nch, N_CMB * 2 * HEAD_DIM), F32),
        compiler_params=_cparams("parallel", "arbitrary"),
        name="cmp_project_sample",
    )(page_table, *([cache6] * CMP_PAGES), wcat)


def _cmp_fin_kernel(p_ref, b1_ref, w2_ref, o_ref):
    nch = p_ref.shape[0]
    for cmb in range(N_CMB):
        slot = cmb // KV_HEADS
        c0 = cmb * 2 * HEAD_DIM
        first = p_ref[:, c0:c0 + HEAD_DIM]
        second = pltpu.roll(p_ref[:, c0 + HEAD_DIM:c0 + 2 * HEAD_DIM], nch - 1, 0)
        hid = b1_ref[slot:slot + 1, :] + first + second
        o_ref[cmb] = _dot(jax.nn.gelu(hid).astype(BF16), w2_ref[slot])


def _cmp_finish(p, b1, w2):
    b, nch, _ = p.shape
    return pl.pallas_call(
        _cmp_fin_kernel,
        grid=(b,),
        in_specs=[pl.BlockSpec((None, nch, p.shape[2]), lambda bi: (bi, 0, 0)),
                  pl.BlockSpec(b1.shape, lambda bi: (0, 0)),
                  pl.BlockSpec(w2.shape, lambda bi: (0, 0, 0))],
        out_specs=pl.BlockSpec((None, N_CMB, nch, HEAD_DIM), lambda bi: (bi, 0, 0, 0)),
        out_shape=jax.ShapeDtypeStruct((b, N_CMB, nch, HEAD_DIM), F32),
        compiler_params=_cparams("parallel"),
        name="cmp_finish",
    )(p, b1, w2)


def _softmax_init(m_scr, l_scr, acc_scr):
    m_scr[...] = jnp.full(m_scr.shape, M_INIT, F32)
    l_scr[...] = jnp.zeros(l_scr.shape, F32)
    acc_scr[...] = jnp.zeros(acc_scr.shape, F32)


def _softmax_update(s, mask, vb, m_scr, l_scr, acc_scr):
    s = jnp.where(mask, s, NEG_INF)
    m_prev = m_scr[...]
    m_new = jnp.maximum(m_prev, jnp.max(s, axis=1, keepdims=True))
    alpha = jnp.exp(m_prev - m_new)
    p = jnp.exp(s - m_new)
    l_scr[...] = alpha * l_scr[...] + jnp.sum(p, axis=1, keepdims=True)
    acc_scr[...] = alpha * acc_scr[...] + _dot(p.astype(BF16), vb)
    m_scr[...] = m_new


def _softmax_result(l_scr, acc_scr):
    l = l_scr[...]
    return acc_scr[...] / jnp.where(l > 0, l, 1.0)


def _masked_probs(s, mask):
    s = jnp.where(mask, s, NEG_INF)
    m = jnp.max(s, axis=1, keepdims=True)
    m = jnp.where(m > NEG_INF, m, 0.0)
    p = jnp.exp(s - m)
    den = jnp.sum(p, axis=1, keepdims=True)
    return p / jnp.where(den > 0, den, 1.0)


def _stack_heads(q):
    return jnp.concatenate([q[:, j * HEAD_DIM:(j + 1) * HEAD_DIM] for j in range(GROUP)], axis=0)


def _fold_lanes(x, op):
    out = x[:, 0:LANES]
    for k in range(1, x.shape[1] // LANES):
        out = op(out, x[:, k * LANES:(k + 1) * LANES])
    return out


def _block_importance(pc, n_cmp, n_sel):
    ncp = pc.shape[1]
    nsp = -(-n_sel // LANES) * LANES
    c_start = lax.broadcasted_iota(jnp.int32, (ncp, nsp), 0) * CMP_STRIDE
    s_start = lax.broadcasted_iota(jnp.int32, (ncp, nsp), 1) * SEL_BLOCK
    overlap = ((c_start < s_start + SEL_BLOCK) & (c_start + CMP_BLOCK > s_start)
               & (c_start < n_cmp * CMP_STRIDE)).astype(F32)
    return jnp.dot(pc, overlap, preferred_element_type=F32, precision=lax.Precision.HIGHEST)


def _force_blocks(imp, blk, cur):
    forced = (blk == 0) | (blk == cur) | (blk == cur - 1)
    imp = jnp.where(forced, FORCE_SCORE, imp)
    return jnp.where(blk > cur, -1.0, imp)


def _select_blocks_cols(pc, cur, n_cmp, n_sel):
    imp = _block_importance(pc, n_cmp, n_sel)
    blk = lax.broadcasted_iota(jnp.int32, (1, imp.shape[1]), 1)
    imp = _force_blocks(imp, blk, cur)

    def body(sp, count):
        col = jnp.sum(jnp.where(blk == sp, imp, 0.0), axis=1, keepdims=True)
        ahead = (col > imp) | ((col == imp) & (sp < blk))
        return count + ahead.astype(F32)

    count = lax.fori_loop(0, n_sel, body, jnp.zeros(imp.shape, F32), unroll=64)
    return ((count < min(SEL_TOPK, n_sel)) & (blk < n_sel)).astype(F32)


def _select_blocks_rows(pc, cur_row, n_cmp, n_sel):
    rows = pc.shape[0]
    nrow = -(-n_sel // SUBLANES) * SUBLANES
    imp_t = _block_importance(pc, n_cmp, n_sel).T[0:nrow, :]
    blk = lax.broadcasted_iota(jnp.int32, (nrow, 1), 0)
    imp_t = _force_blocks(imp_t, blk, cur_row)
    count = jnp.zeros(imp_t.shape, F32)
    for sp in range(n_sel):
        row = imp_t[sp:sp + 1, :]
        count += ((row > imp_t) | ((row == imp_t) & (sp < blk))).astype(F32)
    sel_t = ((count < min(SEL_TOPK, n_sel)) & (blk < n_sel)).astype(F32)
    sel_t = jnp.concatenate([sel_t, jnp.zeros((LANES - nrow, rows), F32)], axis=0)
    return sel_t.T


def _gate(gt, col):
    lane = lax.broadcasted_iota(jnp.int32, (1, LANES), 1)
    return jax.nn.sigmoid(jnp.sum(jnp.where(lane == col, gt, 0.0), axis=1, keepdims=True))


def _nsa_prompt_kernel(q_ref, ks_ref, vs_ref, kw_ref, vw_ref, kc_ref, vc_ref, bc_ref, tz_ref, gt_ref, gb_ref,
                       o_ref, s_scr, selb_scr, mx_scr, l_scr, acc_scr, *, t_len):
    g = pl.program_id(1)
    qt = pl.program_id(2)
    q0 = qt * TQ
    n_cmp = t_len // CMP_STRIDE - (CMP_BLOCK // CMP_STRIDE) + 1
    n_sel = t_len // SEL_BLOCK
    ncp = kc_ref.shape[0]
    rows4 = GROUP * TQ
    qs = _stack_heads(q_ref[...] * QK_SCALE).astype(BF16)
    qpos = q0 + (lax.broadcasted_iota(jnp.int32, (rows4, 1), 0) & (TQ - 1))

    n_idx = lax.broadcasted_iota(jnp.int32, (1, ncp), 1)
    s_c = _dot_nt(qs, kc_ref[...].astype(BF16)) + bc_ref[...].reshape(rows4, ncp)
    cmask = (qpos - (n_idx * CMP_STRIDE + CMP_BLOCK - 1) >= 0) & (n_idx < n_cmp)
    p_c = _masked_probs(s_c, cmask)
    o_c = _dot(p_c.astype(BF16), vc_ref[...].astype(BF16))
    pc = p_c[0:TQ] + p_c[TQ:2 * TQ] + p_c[2 * TQ:3 * TQ] + p_c[3 * TQ:4 * TQ]
    cur_row = jnp.right_shift(q0 + lax.broadcasted_iota(jnp.int32, (1, TQ), 1), SEL_SHIFT)
    sel = _select_blocks_rows(pc, cur_row, n_cmp, n_sel)
    sel_add = ((sel - 1.0) * -MASKED).astype(BF16)

    kt_hi = (q0 + TQ - 1) // TK + 1
    kt_far = jnp.maximum(q0 - (FAR_DIST - 1), 0) // TK
    kidx = lax.broadcasted_iota(jnp.int32, (1, TK), 1)
    srow = lax.broadcasted_iota(jnp.int32, (LANES, TK), 0)

    def spread_selection(kt, carry):
        k0 = kt * TK
        expand = (jnp.right_shift(k0 + lax.broadcasted_iota(jnp.int32, (LANES, TK), 1), SEL_SHIFT) == srow).astype(BF16)
        selb_scr[kt] = _dot(sel_add, expand)
        return carry

    lax.fori_loop(0, kt_hi, spread_selection, 0)

    def tile_bias(k0):
        return tz_ref[jnp.clip((q0 - k0) // TQ, 0, N_TZ - 1)].reshape(rows4, TK)

    def attend(k_ref, v_ref, kt_lo, mask_tile):
        mx_scr[...] = jnp.full(mx_scr.shape, NEG_INF, F32)

        def scores(near):
            def body(kt, carry):
                k0 = pl.multiple_of(kt * TK, TK)
                s = _dot_nt(qs, k_ref[pl.ds(k0, TK), :].astype(BF16))
                if near:
                    s = s + tile_bias(k0)
                s = mask_tile(k0, s, near)
                s_scr[kt] = s
                mx_scr[...] = jnp.maximum(mx_scr[...], _fold_lanes(s, jnp.maximum))
                return carry
            return body

        kt_mid = jnp.maximum(kt_far, kt_lo)
        lax.fori_loop(kt_lo, kt_mid, scores(False), 0)
        lax.fori_loop(kt_mid, kt_hi, scores(True), 0)
        m = jnp.max(mx_scr[...], axis=1, keepdims=True)
        mx_scr[...] = jnp.broadcast_to(jnp.where(m > NEG_INF, m, 0.0), mx_scr.shape)
        l_scr[...] = jnp.zeros(l_scr.shape, F32)
        acc_scr[...] = jnp.zeros(acc_scr.shape, F32)

        def probs(kt, carry):
            k0 = pl.multiple_of(kt * TK, TK)
            m_rep = mx_scr[...]
            p = jnp.exp(s_scr[kt] - jnp.concatenate([m_rep] * (TK // LANES), axis=1))
            l_scr[...] += _fold_lanes(p, jnp.add)
            acc_scr[...] += _dot(p.astype(BF16), v_ref[pl.ds(k0, TK), :].astype(BF16))
            return carry

        lax.fori_loop(kt_lo, kt_hi, probs, 0)
        l = jnp.sum(l_scr[...], axis=1, keepdims=True)
        return acc_scr[...] / jnp.where(l > 0, l, 1.0)

    def sel_mask(k0, s, near):
        s = (s.reshape(GROUP, TQ, TK) + selb_scr[k0 // TK][None]).reshape(rows4, TK)
        return jnp.where(k0 + kidx <= qpos, s, NEG_INF) if near else s

    o_s = attend(ks_ref, vs_ref, 0, sel_mask)

    def win_mask(k0, s, near):
        dist = qpos - (k0 + kidx)
        return jnp.where((dist >= 0) & (dist <= WINDOW), s, NEG_INF)

    o_w = attend(kw_ref, vw_ref, jnp.maximum(q0 - WINDOW, 0) // TK, win_mask)

    gt = gt_ref[...] + gb_ref[...]
    for j in range(GROUP):
        col = GATE_NG + (g * GROUP + j) * 3
        r = slice(j * TQ, (j + 1) * TQ)
        o_ref[:, j * HEAD_DIM:(j + 1) * HEAD_DIM] = (
            _gate(gt, col) * o_c[r] + _gate(gt, col + 1) * o_s[r] + _gate(gt, col + 2) * o_w[r])


def _nsa_prompt(z, kcvc, bias_c, tz, gate_b):
    b, t, _ = z.shape
    ncp = kcvc.shape[2]
    nq = t // TQ
    kv = lambda slot: pl.BlockSpec((None, t, HEAD_DIM), lambda bi, g, qi: (bi, 0, C_NKV // HEAD_DIM + slot * KV_HEADS + g))
    cmp_blk = lambda slot: pl.BlockSpec((None, None, ncp, HEAD_DIM), lambda bi, g, qi: (bi, slot * KV_HEADS + g, 0, 0))
    return pl.pallas_call(
        functools.partial(_nsa_prompt_kernel, t_len=t),
        grid=(b, KV_HEADS, nq),
        in_specs=[pl.BlockSpec((None, TQ, GROUP * HEAD_DIM), lambda bi, g, qi: (bi, qi, C_NQ // (GROUP * HEAD_DIM) + g)),
                  kv(2), kv(3), kv(4), kv(5), cmp_blk(0), cmp_blk(1),
                  pl.BlockSpec((GROUP, TQ, ncp), lambda bi, g, qi: (g, qi, 0)),
                  pl.BlockSpec((N_TZ, GROUP, TQ, TK), lambda bi, g, qi: (0, g, 0, 0)),
                  pl.BlockSpec((None, TQ, LANES), lambda bi, g, qi: (bi, qi, C_GATE // LANES)),
                  pl.BlockSpec((1, LANES), lambda bi, g, qi: (0, 0))],
        out_specs=pl.BlockSpec((None, TQ, GROUP * HEAD_DIM), lambda bi, g, qi: (bi, qi, g)),
        out_shape=jax.ShapeDtypeStruct((b, t, D_NSA), F32),
        scratch_shapes=[pltpu.VMEM((t // TK, GROUP * TQ, TK), F32), pltpu.VMEM((t // TK, TQ, TK), F32),
                        pltpu.VMEM((GROUP * TQ, LANES), F32),
                        pltpu.VMEM((GROUP * TQ, LANES), F32), pltpu.VMEM((GROUP * TQ, HEAD_DIM), F32)],
        compiler_params=_cparams("parallel", "parallel", "arbitrary"),
        name="nsa_prompt",
    )(z, z, z, z, z, kcvc, kcvc, bias_c, tz, z, gate_b)


TS = SUBLANES
ROWS_S = GROUP * TS


def _nsa_s_cmp_kernel(q_ref, kc_ref, vc_ref, bc_ref, oc_ref, sel_ref, *, pos0, n_cmp, n_sel):
    nch = kc_ref.shape[0]
    qs = _stack_heads(q_ref[...] * QK_SCALE).astype(BF16)
    qpos = pos0 + (lax.broadcasted_iota(jnp.int32, (ROWS_S, 1), 0) & (TS - 1))
    n_idx = lax.broadcasted_iota(jnp.int32, (1, nch), 1)
    s_c = _dot_nt(qs, kc_ref[...].astype(BF16)) + bc_ref[...].reshape(ROWS_S, nch)
    cmask = (qpos - (n_idx * CMP_STRIDE + CMP_BLOCK - 1) >= 0) & (n_idx < n_cmp)
    p_c = _masked_probs(s_c, cmask)
    oc_ref[...] = _dot(p_c.astype(BF16), vc_ref[...].astype(BF16))
    pc = p_c[0:TS] + p_c[TS:2 * TS] + p_c[2 * TS:3 * TS] + p_c[3 * TS:4 * TS]
    sel_ref[...] = _select_blocks_cols(pc, jnp.right_shift(qpos[0:TS], SEL_SHIFT), n_cmp, n_sel)


def _nsa_sample_cmp(z8, kcvc, bias_c, *, pos0, n_cmp, n_sel):
    b = z8.shape[0]
    nch = kcvc.shape[2]
    nsp = -(-n_sel // LANES) * LANES
    cmp_blk = lambda slot: pl.BlockSpec((None, None, nch, HEAD_DIM), lambda bi, g: (bi, slot * KV_HEADS + g, 0, 0))
    return pl.pallas_call(
        functools.partial(_nsa_s_cmp_kernel, pos0=pos0, n_cmp=n_cmp, n_sel=n_sel),
        grid=(b, KV_HEADS),
        in_specs=[pl.BlockSpec((None, TS, GROUP * HEAD_DIM), lambda bi, g: (bi, 0, C_NQ // (GROUP * HEAD_DIM) + g)),
                  cmp_blk(0), cmp_blk(1),
                  pl.BlockSpec((GROUP, TS, nch), lambda bi, g: (g, 0, 0))],
        out_specs=[pl.BlockSpec((None, None, ROWS_S, HEAD_DIM), lambda bi, g: (bi, g, 0, 0)),
                   pl.BlockSpec((None, None, TS, nsp), lambda bi, g: (bi, g, 0, 0))],
        out_shape=[jax.ShapeDtypeStruct((b, KV_HEADS, ROWS_S, HEAD_DIM), F32),
                   jax.ShapeDtypeStruct((b, KV_HEADS, TS, nsp), F32)],
        compiler_params=_cparams("parallel", "parallel"),
        name="nsa_sample_cmp",
    )(z8, kcvc, kcvc, bias_c)


def _pad_rows(x, n):
    return jnp.concatenate([x, jnp.zeros((n - x.shape[0], x.shape[1]), x.dtype)], axis=0)


def _nsa_s_sel_kernel(pt_ref, q_ref, sel_ref, kn_ref, vn_ref, bt_ref, bn_ref, bf_ref, *refs, pos0, npages):
    del pt_ref
    rows_pp = PAGE_SIZE * KV_HEADS
    k_refs = [r.reshape(rows_pp, HEAD_DIM) for r in refs[:SEL_PAGES]]
    v_refs = [r.reshape(rows_pp, HEAD_DIM) for r in refs[SEL_PAGES:2 * SEL_PAGES]]
    o_ref, kbuf, vbuf, m_scr, l_scr, acc_scr = refs[2 * SEL_PAGES:]
    step = pl.program_id(1)
    last = step == npages // SEL_PAGES - 1
    rows = KV_HEADS * ROWS_S
    ncol = SEL_TILE * rows_pp
    head_shift = KV_HEADS.bit_length() - 1
    blk_per_page = PAGE_SIZE // SEL_BLOCK

    @pl.when(step == 0)
    def _():
        _softmax_init(m_scr, l_scr, acc_scr)

    nsp = sel_ref.shape[-1]
    ridx = lax.broadcasted_iota(jnp.int32, (rows, 1), 0)
    qpos = pos0 + (ridx & (TS - 1))
    qs = jnp.concatenate([_stack_heads(q_ref[:, g * GROUP * HEAD_DIM:(g + 1) * GROUP * HEAD_DIM] * QK_SCALE)
                          for g in range(KV_HEADS)], axis=0).astype(BF16)
    sel_rows = jnp.concatenate([sel_ref[g] for g in range(KV_HEADS) for _ in range(GROUP)], axis=0)
    bias_far = bf_ref[...].reshape(rows, PAGE_SIZE)[:, 0:1]
    col = lax.broadcasted_iota(jnp.int32, (1, ncol), 1)
    own_head = (col & (KV_HEADS - 1)) == ridx // ROWS_S
    expand = (jnp.right_shift(lax.broadcasted_iota(jnp.int32, (LANES, ncol), 1), SEL_SHIFT + head_shift)
              == lax.broadcasted_iota(jnp.int32, (LANES, ncol), 0)).astype(BF16)
    blk_r = lax.broadcasted_iota(jnp.int32, (nsp, LANES), 0)
    blk_c = lax.broadcasted_iota(jnp.int32, (nsp, LANES), 1)
    n_tiles = SEL_PAGES // SEL_TILE

    for tile in range(n_tiles):
        for i in range(SEL_TILE):
            p = tile * SEL_TILE + i
            kbuf[i * rows_pp:(i + 1) * rows_pp, :] = k_refs[p][...].astype(BF16)
            vbuf[i * rows_pp:(i + 1) * rows_pp, :] = v_refs[p][...].astype(BF16)
        page0 = step * SEL_PAGES + tile * SEL_TILE
        window = ((blk_r == page0 * blk_per_page + blk_c) & (blk_c < SEL_TILE * blk_per_page)).astype(BF16)
        chosen = _dot(_dot(sel_rows.astype(BF16), window).astype(BF16), expand) > 0.5
        s = _dot_nt(qs, kbuf[...]) + bias_far
        if tile == n_tiles - 1:
            tail = jnp.where(last, bt_ref[...].reshape(rows, rows_pp) - bias_far, 0.0)
            s = s + jnp.concatenate([jnp.zeros((rows, ncol - rows_pp), F32), tail], axis=1)
        key = page0 * PAGE_SIZE + jnp.right_shift(col, head_shift)
        _softmax_update(s, chosen & own_head & (key <= qpos), vbuf[...], m_scr, l_scr, acc_scr)

    @pl.when(last)
    def _():
        kidx = lax.broadcasted_iota(jnp.int32, (1, PAGE_SIZE), 1)
        new_blk = lax.broadcasted_iota(jnp.int32, (1, nsp), 1) == npages * blk_per_page
        flag = jnp.sum(jnp.where(new_blk, sel_rows, 0.0), axis=1, keepdims=True) > 0.5
        bias_new = bn_ref[...].reshape(rows, PAGE_SIZE)
        for g in range(KV_HEADS):
            r = slice(g * ROWS_S, (g + 1) * ROWS_S)
            kn = _pad_rows(kn_ref[:, g * HEAD_DIM:(g + 1) * HEAD_DIM], PAGE_SIZE).astype(BF16)
            vn = _pad_rows(vn_ref[:, g * HEAD_DIM:(g + 1) * HEAD_DIM], PAGE_SIZE).astype(BF16)
            mask = flag[r] & (pos0 + kidx <= qpos[r]) & (kidx < TS)
            _softmax_update(_dot_nt(qs[r], kn) + bias_new[r], mask, vn, m_scr.at[r], l_scr.at[r], acc_scr.at[r])
            o_ref[g] = _softmax_result(l_scr.at[r], acc_scr.at[r])


def _nsa_sample_sel(z8, sel, bias_tail, bias_new, bias_far, cache6, page_table, layer, *, pos0):
    b, npages = page_table.shape
    nsp = sel.shape[-1]
    steps = npages // SEL_PAGES
    slot_cols = KV_HEADS * HEAD_DIM
    new_rows = lambda slot: pl.BlockSpec((None, TS, slot_cols),
                                         lambda bi, si, pt: (bi, 0, (C_NKV + slot * slot_cols) // slot_cols))
    whole = lambda a: pl.BlockSpec(a.shape, lambda bi, si, pt: (0,) * a.ndim)
    grid_spec = pltpu.PrefetchScalarGridSpec(
        num_scalar_prefetch=1,
        grid=(b, steps),
        in_specs=[pl.BlockSpec((None, TS, D_NSA), lambda bi, si, pt: (bi, 0, C_NQ // D_NSA)),
                  pl.BlockSpec((None, KV_HEADS, TS, nsp), lambda bi, si, pt: (bi, 0, 0, 0)),
                  new_rows(2), new_rows(3), whole(bias_tail), whole(bias_new), whole(bias_far)]
                 + [_page_spec(layer, 2, p, SEL_PAGES, 2) for p in range(SEL_PAGES)]
                 + [_page_spec(layer, 3, p, SEL_PAGES, 2) for p in range(SEL_PAGES)],
        out_specs=pl.BlockSpec((None, KV_HEADS, ROWS_S, HEAD_DIM), lambda bi, si, pt: (bi, 0, 0, 0)),
        scratch_shapes=[pltpu.VMEM((SEL_TILE * PAGE_SIZE * KV_HEADS, HEAD_DIM), BF16),
                        pltpu.VMEM((SEL_TILE * PAGE_SIZE * KV_HEADS, HEAD_DIM), BF16),
                        pltpu.VMEM((KV_HEADS * ROWS_S, 1), F32), pltpu.VMEM((KV_HEADS * ROWS_S, 1), F32),
                        pltpu.VMEM((KV_HEADS * ROWS_S, HEAD_DIM), F32)],
    )
    return pl.pallas_call(
        functools.partial(_nsa_s_sel_kernel, pos0=pos0, npages=npages),
        grid_spec=grid_spec,
        out_shape=jax.ShapeDtypeStruct((b, KV_HEADS, ROWS_S, HEAD_DIM), F32),
        compiler_params=_cparams("parallel", "arbitrary"),
        name="nsa_sample_sel",
    )(page_table, z8, sel, z8, z8, bias_tail, bias_new, bias_far, *([cache6] * (2 * SEL_PAGES)))


def _nsa_s_win_kernel(q_ref, kw_ref, vw_ref, kn_ref, vn_ref, bw_ref, oc_ref, os_ref, gt_ref, gb_ref, o_ref,
                      m_scr, l_scr, acc_scr, *, pos0):
    g = pl.program_id(1)
    wb = kw_ref.shape[0]
    own = pl.ds(g, wb, stride=KV_HEADS)
    kw = kw_ref.reshape(wb * KV_HEADS, HEAD_DIM)[own, :]
    vw = vw_ref.reshape(wb * KV_HEADS, HEAD_DIM)[own, :]
    qs = _stack_heads(q_ref[...] * QK_SCALE).astype(BF16)
    qpos = pos0 + (lax.broadcasted_iota(jnp.int32, (ROWS_S, 1), 0) & (TS - 1))
    bias = bw_ref[...].reshape(ROWS_S, wb + PAGE_SIZE)
    _softmax_init(m_scr, l_scr, acc_scr)
    dist = qpos - (pos0 - wb + lax.broadcasted_iota(jnp.int32, (1, wb), 1))
    s = _dot_nt(qs, kw.astype(BF16)) + bias[:, 0:wb]
    _softmax_update(s, (dist >= 0) & (dist <= WINDOW), vw.astype(BF16), m_scr, l_scr, acc_scr)
    kidx = lax.broadcasted_iota(jnp.int32, (1, PAGE_SIZE), 1)
    dist = qpos - (pos0 + kidx)
    s = _dot_nt(qs, _pad_rows(kn_ref[...], PAGE_SIZE).astype(BF16)) + bias[:, wb:wb + PAGE_SIZE]
    _softmax_update(s, (dist >= 0) & (dist <= WINDOW) & (kidx < TS), _pad_rows(vn_ref[...], PAGE_SIZE).astype(BF16),
                    m_scr, l_scr, acc_scr)
    o_w = _softmax_result(l_scr, acc_scr)
    o_c = oc_ref[...]
    o_s = os_ref[...]
    gt = gt_ref[...] + gb_ref[...]
    for j in range(GROUP):
        col = GATE_NG + (g * GROUP + j) * 3
        r = slice(j * TS, (j + 1) * TS)
        o_ref[:, j * HEAD_DIM:(j + 1) * HEAD_DIM] = (
            _gate(gt, col) * o_c[r] + _gate(gt, col + 1) * o_s[r] + _gate(gt, col + 2) * o_w[r])


def _nsa_sample_win(z8, win6, layer, bias_w, o_c, o_s, gate_b, *, pos0):
    _, b, wb = win6.shape[:3]
    cached = lambda slot: pl.BlockSpec((None, None, wb, None, KV_HEADS, HEAD_DIM),
                                       lambda bi, g: (layer, bi, 0, slot, 0, 0))
    zcol = lambda col: (lambda bi, g: (bi, 0, col // HEAD_DIM + g))
    part = pl.BlockSpec((None, None, ROWS_S, HEAD_DIM), lambda bi, g: (bi, g, 0, 0))
    return pl.pallas_call(
        functools.partial(_nsa_s_win_kernel, pos0=pos0),
        grid=(b, KV_HEADS),
        in_specs=[pl.BlockSpec((None, TS, GROUP * HEAD_DIM), lambda bi, g: (bi, 0, C_NQ // (GROUP * HEAD_DIM) + g)),
                  cached(0), cached(1),
                  pl.BlockSpec((None, TS, HEAD_DIM), zcol(C_NKV + 4 * KV_HEADS * HEAD_DIM)),
                  pl.BlockSpec((None, TS, HEAD_DIM), zcol(C_NKV + 5 * KV_HEADS * HEAD_DIM)),
                  pl.BlockSpec((GROUP, TS, wb + PAGE_SIZE), lambda bi, g: (g, 0, 0)),
                  part, part,
                  pl.BlockSpec((None, TS, LANES), lambda bi, g: (bi, 0, C_GATE // LANES)),
                  pl.BlockSpec((1, LANES), lambda bi, g: (0, 0))],
        out_specs=pl.BlockSpec((None, TS, GROUP * HEAD_DIM), lambda bi, g: (bi, 0, g)),
        out_shape=jax.ShapeDtypeStruct((b, TS, D_NSA), F32),
        scratch_shapes=[pltpu.VMEM((ROWS_S, 1), F32), pltpu.VMEM((ROWS_S, 1), F32), pltpu.VMEM((ROWS_S, HEAD_DIM), F32)],
        compiler_params=_cparams("parallel", "parallel"),
        name="nsa_sample_win",
    )(z8, win6, win6, z8, z8, bias_w, o_c, o_s, z8, gate_b)


def _t5_bucket(dist):
    n = np.maximum(dist, 0)
    exact = NUM_BUCKETS // 2
    nf = np.maximum(n, 1).astype(np.float32)
    large = exact + (np.log(nf / np.float32(exact)) / np.float32(math.log(MAX_DISTANCE / exact))
                     * np.float32(NUM_BUCKETS - exact)).astype(np.int32)
    return np.where(n < exact, n, np.minimum(large, NUM_BUCKETS - 1)).astype(np.int32)


FAR_DIST = int(np.max(np.nonzero(_t5_bucket(np.arange(4 * MAX_DISTANCE)) < NUM_BUCKETS - 1)[0])) + 1
N_TZ = (FAR_DIST + TK - 2) // TQ + 1


def _bias_table(rel_bias, dist):
    onehot = jax.nn.one_hot(jnp.asarray(_t5_bucket(dist).astype(np.int8)), NUM_BUCKETS, dtype=F32)
    return jnp.einsum("rcb,bh->hrc", onehot, rel_bias, precision=lax.Precision.HIGHEST)


def _pack_weights(p):
    w_in = p["w_in"].astype(BF16)
    pad = jnp.zeros(w_in.shape[:2] + (N_IN - C_GATE - 32,), BF16)
    w_in_p = jnp.concatenate([w_in[..., 0:2048], w_in[..., 2056:3080], w_in[..., 3080:4616], w_in[..., 4640:5152],
                              w_in[..., 2048:2056], w_in[..., 4616:4640], pad], axis=2)
    return dict(
        w_in=w_in_p, w_out=p["w_out"].astype(BF16), w_down=p["ffn_w_down"].astype(BF16),
        w_up=p["ffn_w_up"].astype(BF16))


def _pack_layer(p, big, l):
    gb = p["mlstm_gate_b"][l]
    mlstm_gate_b = jnp.zeros((1, LANES), F32).at[0, GATE_MI:GATE_MI + 4].set(gb[0]).at[0, GATE_MF:GATE_MF + 4].set(gb[1])
    nsa_gate_b = jnp.zeros((1, LANES), F32).at[0, GATE_NG:GATE_NG + 3 * NSA_HEADS].set(p["nsa_gate_b"][l].reshape(-1))
    w1 = p["cmp_w1"][l]
    wcat = jnp.concatenate([w1[:, 0:CMP_STRIDE], w1[:, CMP_STRIDE:CMP_BLOCK]], axis=-1).astype(BF16)
    lam = lax.complex(p["s5_a_re"][l], p["s5_a_im"][l])
    lam_bar = jnp.exp(lam * jnp.exp(p["s5_log_step"][l])[:, None])
    b_bar = ((lam_bar - 1.0) / lam)[..., None] * lax.complex(p["s5_b_re"][l], p["s5_b_im"][l])
    gi = S5_GROUPS // S5_IN_TILES
    bd_in = lambda m: jnp.einsum("cgph,gk->cghkp", m.reshape(S5_IN_TILES, gi, S5_STATE, S5_GROUP_WIDTH),
                                 jnp.eye(gi, dtype=F32)).reshape(S5_IN_TILES, S5_IN_FEATS, 256)
    wb = jnp.concatenate([bd_in(b_bar.real), bd_in(b_bar.imag)], axis=2).astype(BF16)
    go = S5_GROUPS // S5_OUT_TILES
    bd_out = lambda m: jnp.einsum("jghp,gk->jgpkh", m.reshape(S5_OUT_TILES, go, S5_GROUP_WIDTH, S5_STATE),
                                  jnp.eye(go, dtype=F32)).reshape(S5_OUT_TILES, S5_OUT_CH, 256)
    wc = jnp.stack([bd_out(p["s5_c_re"][l]), -bd_out(p["s5_c_im"][l])]).astype(BF16)
    half = lambda w: jnp.pad(w, ((0, 0), (0, D_FFP - D_FF)))
    conv_w = p["ffn_conv_w"][l]
    conv_b = p["ffn_conv_b"][l][None, :]
    return dict(
        big, layer=l, mlstm_gate_b=mlstm_gate_b, mlstm_norm_w=p["mlstm_norm_w"][l][None, :], nsa_gate_b=nsa_gate_b,
        wcat=wcat, cmp_b1=p["cmp_b1"][l], cmp_w2=p["cmp_w2"][l].astype(BF16),
        s5_wb=wb, s5_wc=wc, s5_lam_r=lam_bar.real.reshape(1, S5_CH), s5_lam_i=lam_bar.imag.reshape(1, S5_CH),
        s5_d=p["s5_d"][l][None, :], s5_glu_w=p["s5_glu_w"][l].astype(BF16), s5_glu_b=p["s5_glu_b"][l][None, :],
        ln1_w=p["ln1_w"][l][None, :], ln1_b=p["ln1_b"][l][None, :],
        conv_w8=jnp.pad(jnp.concatenate([half(conv_w[:, :D_FF]), half(conv_w[:, D_FF:])], axis=1),
                        ((0, SUBLANES - CONV_W), (0, 0))),
        conv_b=jnp.concatenate([half(conv_b[:, :D_FF]), half(conv_b[:, D_FF:])], axis=1),
        ln2_w=p["ln2_w"][l][None, :], ln2_b=p["ln2_b"][l][None, :],
    )


def _unpad_ff(x):
    return jnp.concatenate([x[..., :D_FF], x[..., D_FFP:D_FFP + D_FF]], axis=-1)


def _pad_ff(x):
    pad = [(0, 0)] * (x.ndim - 1) + [(0, D_FFP - D_FF)]
    return jnp.concatenate([jnp.pad(x[..., :D_FF], pad), jnp.pad(x[..., D_FF:], pad)], axis=-1)


def _mixer_tail(x2d, z, b, t, t_use, lw, o_mlstm, o_nsa, s5_state, conv_state8, tm):
    pad8 = lambda s: jnp.pad(s.reshape(b, S5_CH), ((0, SUBLANES - b), (0, 0)))
    o_s5, xr, xi = _s5(z.reshape(b, t, N_IN), lw["s5_wb"], lw["s5_wc"], lw["s5_lam_r"], lw["s5_lam_i"], lw["s5_d"],
                       lw["s5_glu_w"], lw["s5_glu_b"], pad8(s5_state[0]), pad8(s5_state[1]), lc=S5_LC, steps=t_use)
    o_s5 = o_s5.reshape(b * t, D_S5)
    layer = lw["layer"]
    x1 = _wout_ln(o_mlstm.reshape(b * t, D_MLSTM), o_nsa.reshape(b * t, D_NSA), o_s5, x2d,
                  lw["w_out"], layer, lw["ln1_w"], lw["ln1_b"], tm=512)
    s5_new = (xr[:b].reshape(b, S5_GROUPS, S5_STATE), xi[:b].reshape(b, S5_GROUPS, S5_STATE))
    if conv_state8 is None:
        x2, tail_a, tail_g = _ffn_prompt(x1.reshape(b, t, D_MODEL), lw["w_up"], lw["w_down"], layer, lw["conv_w8"],
                                         lw["conv_b"], lw["ln2_w"], lw["ln2_b"], tt=FFN_TT, tf=FFN_TF)
        keep = slice(SUBLANES - (CONV_W - 1), SUBLANES)
        conv_new = jnp.concatenate([tail_a[:, keep, :D_FF], tail_g[:, keep, :D_FF]], axis=-1)
        return x2.reshape(b * t, D_MODEL), s5_new, conv_new
    up = _matmul(x1, lw["w_up"], layer, tm=tm, tn=FFN_TF, n=2 * D_FFP, w_cols=_up_cols(FFN_TF))
    hgate = _convgate(up.reshape(b, t, 2 * D_FFP), conv_state8, lw["conv_w8"], lw["conv_b"], tf=FFN_TF)
    x2 = _down_ln(hgate.reshape(b * t, D_FFP), lw["w_down"], layer, x1, lw["ln2_w"], lw["ln2_b"], tm=512, tk=1408)
    conv_new = _unpad_ff(up.reshape(b, t, 2 * D_FFP)[:, t_use - (CONV_W - 1):t_use])
    return x2, s5_new, conv_new


def _prompt_layer(x2d, b, t, lw, rel_bias, rows_so_far):
    z, kv_rows, win_rows = _proj_in(x2d, lw["w_in"], lw["layer"], rows_so_far, tm=min(1024, b * t), tn=768)
    z3 = z.reshape(b, t, N_IN)
    zeros = lambda *s: jnp.zeros(s, F32)
    o_mlstm, c1, n1, m1 = _mlstm(z3, lw["mlstm_gate_b"], lw["mlstm_norm_w"],
                                 zeros(b, MLSTM_HEADS, HEAD_DIM, HEAD_DIM), zeros(b, MLSTM_HEADS, LANES),
                                 zeros(b, MLSTM_HEADS, LANES), L=MLSTM_L, lin=MLSTM_L, t_valid=t)
    kcvc = _cmp_finish(_cmp_project_prompt(z3, lw["wcat"]), lw["cmp_b1"], lw["cmp_w2"])
    ncp = t // CMP_STRIDE
    bias_c = _bias_table(rel_bias, np.arange(t)[:, None] - (np.arange(ncp) * CMP_STRIDE + CMP_BLOCK - 1)[None, :])
    ti = np.arange(TQ)[:, None] - np.arange(TK)[None, :]
    far = _bias_table(rel_bias, np.full((1, 1), FAR_DIST))
    tz = jnp.stack([_bias_table(rel_bias, d * TQ + ti) - far for d in range(N_TZ)])
    o_nsa = _nsa_prompt(z3, kcvc, bias_c, tz, lw["nsa_gate_b"])
    x2, s5_new, conv_new = _mixer_tail(
        x2d, z, b, t, t, lw, o_mlstm, o_nsa, (zeros(b, S5_GROUPS, S5_STATE), zeros(b, S5_GROUPS, S5_STATE)),
        None, tm=1024)
    state = (None, None, c1, n1, m1[:, :, 0], s5_new[0], s5_new[1], conv_new)
    return x2, state, (kv_rows, win_rows)


def _sample_layer(x2d, b, tn, lw, rel_bias, layer, cache6, page_table, win6, mlstm_state, s5_state, conv_state):
    npages = page_table.shape[1]
    pos0 = npages * PAGE_SIZE
    z = _matmul(x2d, lw["w_in"], lw["layer"], tm=1024, tn=768)
    z8 = z.reshape(b, TS, N_IN)
    c0, n0, m0 = mlstm_state
    o_mlstm, c1, n1, m1 = _mlstm(z8, lw["mlstm_gate_b"], lw["mlstm_norm_w"], c0, n0,
                                 jnp.broadcast_to(m0[:, :, None], (b, MLSTM_HEADS, LANES)),
                                 L=LANES, lin=TS, t_valid=tn)
    n_chunks = (pos0 + tn) // CMP_STRIDE
    n_cmp = n_chunks - CMP_BLOCK // CMP_STRIDE + 1
    n_sel = -(-(pos0 + tn) // SEL_BLOCK)
    kcvc = _cmp_finish(_cmp_project_sample(cache6, page_table, layer, lw["wcat"]), lw["cmp_b1"], lw["cmp_w2"])
    qpos = pos0 + np.arange(TS)[:, None]
    bias_c = _bias_table(rel_bias, qpos - (np.arange(n_chunks) * CMP_STRIDE + CMP_BLOCK - 1)[None, :])
    o_c, sel = _nsa_sample_cmp(z8, kcvc, bias_c, pos0=pos0, n_cmp=n_cmp, n_sel=n_sel)
    kk = np.arange(PAGE_SIZE)[None, :]
    bias_tail = _bias_table(rel_bias, np.repeat(qpos - (pos0 - PAGE_SIZE + kk), KV_HEADS, axis=1))
    bias_new = _bias_table(rel_bias, qpos - (pos0 + kk))
    bias_far = _bias_table(rel_bias, np.broadcast_to(qpos - (pos0 - 2 * PAGE_SIZE), (TS, PAGE_SIZE)))
    o_s = _nsa_sample_sel(z8, sel, bias_tail, bias_new, bias_far, cache6, page_table, layer, pos0=pos0)
    wb = win6.shape[2]
    wk = np.arange(wb)[None, :]
    bias_w = _bias_table(rel_bias, np.concatenate([qpos - (pos0 - wb + wk), qpos - (pos0 + kk)], axis=1))
    o_nsa = _nsa_sample_win(z8, win6, layer, bias_w, o_c, o_s, lw["nsa_gate_b"], pos0=pos0)
    conv_state8 = jnp.pad(_pad_ff(conv_state), ((0, 0), (SUBLANES - (CONV_W - 1), 0), (0, 0)))
    x2, s5_new, conv_new = _mixer_tail(x2d, z, b, TS, tn, lw, o_mlstm, o_nsa, s5_state, conv_state8, tm=1024)
    nkv = z8[:, :tn, C_NKV:C_NKV + N_KV_SLOTS * KV_HEADS * HEAD_DIM].reshape(b, tn, N_KV_SLOTS, KV_HEADS, HEAD_DIM)
    state = (nkv[:, :, :4], nkv[:, :, 4:], c1, n1, m1[:, :, 0], s5_new[0], s5_new[1], conv_new)
    return x2, state


def kernel(x_prompt, x_sample, cache_nsa_kv, cache_win_kv, state_mlstm_c, state_mlstm_n, state_mlstm_m,
           state_s5_re, state_s5_im, state_ffn_conv, page_table, w_in, mlstm_gate_b, mlstm_norm_w,
           nsa_gate_b, cmp_w1, cmp_b1, cmp_w2, rel_bias, s5_a_re, s5_a_im, s5_b_re, s5_b_im, s5_c_re,
           s5_c_im, s5_d, s5_log_step, s5_glu_w, s5_glu_b, w_out, ln1_w, ln1_b, ffn_w_up, ffn_conv_w,
           ffn_conv_b, ffn_w_down, ln2_w, ln2_b):
    params = dict(w_in=w_in, mlstm_gate_b=mlstm_gate_b, mlstm_norm_w=mlstm_norm_w, nsa_gate_b=nsa_gate_b,
                  cmp_w1=cmp_w1, cmp_b1=cmp_b1, cmp_w2=cmp_w2, s5_a_re=s5_a_re, s5_a_im=s5_a_im, s5_b_re=s5_b_re,
                  s5_b_im=s5_b_im, s5_c_re=s5_c_re, s5_c_im=s5_c_im, s5_d=s5_d, s5_log_step=s5_log_step,
                  s5_glu_w=s5_glu_w, s5_glu_b=s5_glu_b, w_out=w_out, ln1_w=ln1_w, ln1_b=ln1_b, ffn_w_up=ffn_w_up,
                  ffn_conv_w=ffn_conv_w, ffn_conv_b=ffn_conv_b, ffn_w_down=ffn_w_down, ln2_w=ln2_w, ln2_b=ln2_b)
    depth = w_in.shape[0]
    bp, tp, _ = x_prompt.shape
    bs, tn, _ = x_sample.shape
    assert tp % TK == 0 and tn < CMP_STRIDE and tn <= TS and SUBLANES % bp == 0 and SUBLANES % bs == 0
    assert (tn * bs) % SUBLANES == 0 and tp % min(FFN_TT, tp) == 0
    assert page_table.shape[1] % CMP_PAGES == 0 and cache_nsa_kv.shape[2] == PAGE_SIZE
    xp = x_prompt.reshape(bp * tp, D_MODEL)
    xs = jnp.pad(x_sample, ((0, 0), (0, TS - tn), (0, 0))).reshape(bs * TS, D_MODEL)
    p_states, s_states = [], []
    big = _pack_weights(params)
    rows_so_far = None
    for l in range(depth):
        lw = _pack_layer(params, big, l)
        xp, sp, rows_so_far = _prompt_layer(xp, bp, tp, lw, rel_bias, rows_so_far)
        xs, ss = _sample_layer(xs, bs, tn, lw, rel_bias, l, cache_nsa_kv, page_table, cache_win_kv,
                               (state_mlstm_c[l], jnp.pad(state_mlstm_n[l], ((0, 0), (0, 0), (0, LANES - HEAD_DIM))),
                                state_mlstm_m[l]),
                               (state_s5_re[l], state_s5_im[l]), state_ffn_conv[l])
        p_states.append(sp)
        s_states.append(ss)
    stk = lambda states, i: jnp.stack([s[i] for s in states])
    y_prompt = xp.reshape(bp, tp, D_MODEL)
    y_sample = xs.reshape(bs, TS, D_MODEL)[:, :tn]
    wrows = min(WINDOW, tp)
    nsa_kv_p = rows_so_far[0].reshape(depth, bp, tp, 4, KV_HEADS, HEAD_DIM)
    win_kv_p = rows_so_far[1].reshape(depth, bp, tp, 2, KV_HEADS, HEAD_DIM)[:, :, tp - wrows:]
    return (y_prompt, y_sample,
            nsa_kv_p, stk(s_states, 0), win_kv_p, stk(s_states, 1),
            stk(p_states, 2), stk(s_states, 2), stk(p_states, 3), stk(s_states, 3), stk(p_states, 4), stk(s_states, 4),
            stk(p_states, 5), stk(s_states, 5), stk(p_states, 6), stk(s_states, 6), stk(p_states, 7), stk(s_states, 7))
```

```python
import functools
import math

import jax
import jax.numpy as jnp
import numpy as np
from jax import lax
from jax.experimental import pallas as pl
from jax.experimental.pallas import tpu as pltpu

F32 = jnp.float32
BF16 = jnp.bfloat16
NEG_INF = float("-inf")
M_INIT = -1e30
MASKED = -1e30

D_MODEL = 2048
PAGE_SIZE = 128
D_MLSTM = D_MODEL // 4
D_NSA = D_MODEL // 2
D_S5 = D_MODEL - D_MLSTM - D_NSA
HEAD_DIM = 128
MLSTM_HEADS = D_MLSTM // HEAD_DIM
NSA_HEADS = D_NSA // HEAD_DIM
KV_HEADS = 2
GROUP = NSA_HEADS // KV_HEADS
N_KV_SLOTS = 6
CMP_BLOCK = 32
CMP_STRIDE = 16
SEL_BLOCK = 64
SEL_TOPK = 16
FORCE_SCORE = 1e4
WINDOW = 512
S5_GROUP_WIDTH = 16
S5_GROUPS = D_S5 // S5_GROUP_WIDTH
S5_STATE = 64
S5_CH = S5_GROUPS * S5_STATE
NUM_BUCKETS = 32
MAX_DISTANCE = 128
D_FF = ((8 * D_MODEL // 3 + 127) // 128) * 128
CONV_W = 3
LN_EPS = 1e-5
DEPTH = 2
DEEPNORM_ALPHA = (2 * DEPTH) ** 0.25
QK_SCALE = HEAD_DIM ** -0.5

LANES = 128
SUBLANES = 8
V7X_VMEM_LIMIT = 56 * 2 ** 20

C_MQ, C_MK, C_MV, C_MO = 0, 512, 1024, 1536
C_NQ = 2048
C_NKV = 3072
C_SU = 4608
C_GATE = 5120
N_IN = 5376
GATE_MI, GATE_MF, GATE_NG = 0, 4, 8
D_FFP = 5632
MLSTM_L = 1024
TQ = 256
TK = 512
S5_LC = 256
FFN_TT = 1024
FFN_TF = 512
FFN_HALO = 16
CMP_PAGES = 32
SEL_PAGES = 32
SEL_TILE = 32
SEL_SHIFT = SEL_BLOCK.bit_length() - 1


def _cparams(*sem):
    return pltpu.CompilerParams(dimension_semantics=sem, vmem_limit_bytes=V7X_VMEM_LIMIT)


def _dot(a, b):
    return jnp.dot(a, b, preferred_element_type=F32)


def _dot_nt(a, b):
    return lax.dot_general(a, b, (((1,), (1,)), ((), ())), preferred_element_type=F32)


def _layer_norm(y, w, b):
    mu = jnp.mean(y, axis=-1, keepdims=True)
    d = y - mu
    var = jnp.mean(d * d, axis=-1, keepdims=True)
    return d * lax.rsqrt(var + LN_EPS) * w + b


def _mm_kernel(x_ref, w_ref, o_ref, *, realign_tile=None):
    acc = _dot(x_ref[...].astype(BF16), w_ref[...])
    if realign_tile is not None:
        acc = _realign(acc, pl.program_id(1) == realign_tile)
    o_ref[...] = acc.astype(o_ref.dtype)


UP_OVER = D_FFP - D_FF


def _up_cols(tf):
    nf, per = D_FFP // tf, tf // LANES
    return lambda j: (j * per + (j >= nf).astype(jnp.int32) * (D_FF // LANES - nf * per)
                      - (j == 2 * nf - 1).astype(jnp.int32) * (UP_OVER // LANES)) * LANES


def _realign(up, is_last):
    shifted = jnp.concatenate([up[:, UP_OVER:], jnp.zeros((up.shape[0], UP_OVER), up.dtype)], axis=1)
    return jnp.where(is_last, shifted, up)


def _matmul(x, w, layer, *, tm, tn, n=None, w_cols=None):
    m, k = x.shape
    n = w.shape[2] if n is None else n
    tm = min(tm, m)
    if w_cols is None:
        w_spec = pl.BlockSpec((None, k, tn), lambda i, j: (layer, 0, j))
    else:
        w_spec = pl.BlockSpec((None, pl.Element(k), pl.Element(tn)), lambda i, j: (layer, 0, w_cols(j)))
    return pl.pallas_call(
        functools.partial(_mm_kernel, realign_tile=None if w_cols is None else n // tn - 1),
        grid=(m // tm, n // tn),
        in_specs=[pl.BlockSpec((tm, k), lambda i, j: (i, 0)), w_spec],
        out_specs=pl.BlockSpec((tm, tn), lambda i, j: (i, j)),
        out_shape=jax.ShapeDtypeStruct((m, n), F32),
        compiler_params=_cparams("parallel", "parallel"),
        name="proj",
    )(x, w)


KV4_GROUPS = 4 * KV_HEADS
WIN_GROUPS = 2 * KV_HEADS


def _proj_in_kernel(x_ref, w_ref, z_ref, kv_ref, win_ref, xb_ref, *, tn):
    j = pl.program_id(1)

    @pl.when(j == 0)
    def _():
        xb_ref[...] = x_ref[...].astype(BF16)

    acc = _dot(xb_ref[...], w_ref[...])
    z_ref[...] = acc
    tm = acc.shape[0]
    per_tile = tn // HEAD_DIM
    first_tile = C_NKV // tn
    for tile in range(first_tile, first_tile + N_KV_SLOTS * KV_HEADS // per_tile):
        @pl.when(j == tile)
        def _(tile=tile):
            for c in range(per_tile):
                grp = (tile - first_tile) * per_tile + c
                val = acc[:, c * HEAD_DIM:(c + 1) * HEAD_DIM]
                if grp < KV4_GROUPS:
                    kv_ref[pl.ds(grp, tm, stride=KV4_GROUPS), :] = val
                else:
                    win_ref[pl.ds(grp - KV4_GROUPS, tm, stride=WIN_GROUPS), :] = val


def _proj_in_carry_kernel(x_ref, w_ref, kv_in_ref, win_in_ref, z_ref, kv_ref, win_ref, xb_ref, *, tn):
    del kv_in_ref, win_in_ref
    _proj_in_kernel(x_ref, w_ref, z_ref, kv_ref, win_ref, xb_ref, tn=tn)


def _proj_in(x, w, layer, rows_so_far, *, tm, tn):
    m, k = x.shape
    depth, _, n = w.shape
    nm = m // tm
    assert C_NKV % tn == 0 and (N_KV_SLOTS * KV_HEADS * HEAD_DIM) % tn == 0
    carried = () if rows_so_far is None else tuple(rows_so_far)
    return pl.pallas_call(
        functools.partial(_proj_in_carry_kernel if carried else _proj_in_kernel, tn=tn),
        grid=(nm, n // tn),
        in_specs=[pl.BlockSpec((tm, k), lambda i, j: (i, 0)), pl.BlockSpec((None, k, tn), lambda i, j: (layer, 0, j))]
                 + [pl.BlockSpec(memory_space=pl.ANY)] * len(carried),
        out_specs=[pl.BlockSpec((tm, tn), lambda i, j: (i, j)),
                   pl.BlockSpec((tm * KV4_GROUPS, HEAD_DIM), lambda i, j: (layer * nm + i, 0)),
                   pl.BlockSpec((tm * WIN_GROUPS, HEAD_DIM), lambda i, j: (layer * nm + i, 0))],
        out_shape=[jax.ShapeDtypeStruct((m, n), F32),
                   jax.ShapeDtypeStruct((depth * m * KV4_GROUPS, HEAD_DIM), F32),
                   jax.ShapeDtypeStruct((depth * m * WIN_GROUPS, HEAD_DIM), F32)],
        input_output_aliases={2: 1, 3: 2} if carried else {},
        scratch_shapes=[pltpu.VMEM((tm, k), BF16)],
        compiler_params=_cparams("parallel", "arbitrary"),
        name="proj_in",
    )(x, w, *carried)


def _wout_kernel(om_ref, on_ref, os_ref, x_ref, w_ref, lw_ref, lb_ref, o_ref):
    acc = _dot(om_ref[...].astype(BF16), w_ref[0:D_MLSTM, :])
    acc += _dot(on_ref[...].astype(BF16), w_ref[D_MLSTM:D_MLSTM + D_NSA, :])
    acc += _dot(os_ref[...].astype(BF16), w_ref[D_MLSTM + D_NSA:D_MODEL, :])
    o_ref[...] = _layer_norm(DEEPNORM_ALPHA * x_ref[...] + acc, lw_ref[...], lb_ref[...])


def _wout_ln(om, on, os_, x, w, layer, lw, lb, *, tm):
    m = x.shape[0]
    tm = min(tm, m)
    row = lambda width: pl.BlockSpec((tm, width), lambda i: (i, 0))
    full = lambda a: pl.BlockSpec(a.shape, lambda i: (0, 0))
    w_spec = pl.BlockSpec((None,) + w.shape[1:], lambda i: (layer, 0, 0))
    return pl.pallas_call(
        _wout_kernel,
        grid=(m // tm,),
        in_specs=[row(D_MLSTM), row(D_NSA), row(D_S5), row(D_MODEL), w_spec, full(lw), full(lb)],
        out_specs=row(D_MODEL),
        out_shape=jax.ShapeDtypeStruct((m, D_MODEL), F32),
        compiler_params=_cparams("parallel"),
        name="wout_ln",
    )(om, on, os_, x, w, lw, lb)


def _valid_rows(w_ref, tile):
    rows = w_ref.shape[0]
    row = tile * rows + lax.broadcasted_iota(jnp.int32, (rows, 1), 0)
    w = w_ref[...]
    return jnp.where(row < D_FF, w, jnp.zeros_like(w))


def _down_kernel(h_ref, w_ref, x_ref, lw_ref, lb_ref, o_ref, acc_ref, *, nk):
    k = pl.program_id(1)

    @pl.when(k == 0)
    def _():
        acc_ref[...] = jnp.zeros_like(acc_ref)

    acc_ref[...] += _dot(h_ref[...], _valid_rows(w_ref, k))

    @pl.when(k == nk - 1)
    def _():
        o_ref[...] = _layer_norm(DEEPNORM_ALPHA * x_ref[...] + acc_ref[...], lw_ref[...], lb_ref[...])


def _down_ln(h, w, layer, x, lw, lb, *, tm, tk):
    m, kk = h.shape
    tm = min(tm, m)
    nk = kk // tk
    return pl.pallas_call(
        functools.partial(_down_kernel, nk=nk),
        grid=(m // tm, nk),
        in_specs=[
            pl.BlockSpec((tm, tk), lambda i, k: (i, k)),
            pl.BlockSpec((None, tk, D_MODEL), lambda i, k: (layer, k, 0)),
            pl.BlockSpec((tm, D_MODEL), lambda i, k: (i, 0)),
            pl.BlockSpec((1, D_MODEL), lambda i, k: (0, 0)),
            pl.BlockSpec((1, D_MODEL), lambda i, k: (0, 0)),
        ],
        out_specs=pl.BlockSpec((tm, D_MODEL), lambda i, k: (i, 0)),
        out_shape=jax.ShapeDtypeStruct((m, D_MODEL), F32),
        scratch_shapes=[pltpu.VMEM((tm, D_MODEL), F32)],
        compiler_params=_cparams("parallel", "arbitrary"),
        name="down_ln",
    )(h, w, x, lw, lb)


def _convgate_kernel(a_ref, g_ref, sa_ref, sg_ref, wa_ref, wg_ref, ba_ref, bg_ref, o_ref):
    rows = lax.broadcasted_iota(jnp.int32, (1, SUBLANES, 1), 1)

    def conv(cur_ref, st_ref, w_ref, b_ref):
        cur = cur_ref[...]
        prev = st_ref[...]
        p1 = prev[:, 7:8, :]
        p2 = prev[:, 6:7, :]
        x1 = jnp.where(rows == 0, p1, pltpu.roll(cur, 1, 1))
        x2 = jnp.where(rows == 0, p2, jnp.where(rows == 1, p1, pltpu.roll(cur, 2, 1)))
        w = w_ref[...]
        return b_ref[...] + w[0:1, :] * x2 + w[1:2, :] * x1 + w[2:3, :] * cur

    a = conv(a_ref, sa_ref, wa_ref, ba_ref)
    g = conv(g_ref, sg_ref, wg_ref, bg_ref)
    o_ref[...] = (a * jax.nn.sigmoid(a) * g).astype(o_ref.dtype)


def _convgate(up, state8, conv_w8, conv_b, *, tf):
    b, t, _ = up.shape
    assert t == SUBLANES
    nf = D_FFP // tf
    cur_a = pl.BlockSpec((b, t, tf), lambda fi: (0, 0, fi))
    cur_g = pl.BlockSpec((b, t, tf), lambda fi: (0, 0, fi + nf))
    w_a = pl.BlockSpec((SUBLANES, tf), lambda fi: (0, fi))
    w_g = pl.BlockSpec((SUBLANES, tf), lambda fi: (0, fi + nf))
    b_a = pl.BlockSpec((1, tf), lambda fi: (0, fi))
    b_g = pl.BlockSpec((1, tf), lambda fi: (0, fi + nf))
    return pl.pallas_call(
        _convgate_kernel,
        grid=(nf,),
        in_specs=[cur_a, cur_g, cur_a, cur_g, w_a, w_g, b_a, b_g],
        out_specs=pl.BlockSpec((b, t, tf), lambda fi: (0, 0, fi)),
        out_shape=jax.ShapeDtypeStruct((b, t, D_FFP), BF16),
        compiler_params=_cparams("parallel"),
        name="convgate",
    )(up, up, state8, state8, conv_w8, conv_w8, conv_b, conv_b)


def _ffn_kernel(x_ref, xh_ref, wa_ref, wg_ref, wd_ref, cwa_ref, cwg_ref, cba_ref, cbg_ref, lw_ref, lb_ref,
                o_ref, sa_ref, sg_ref, xe_ref, *, nf):
    first = pl.program_id(1) == 0
    f = pl.program_id(2)
    tt = x_ref.shape[0]

    @pl.when(f == 0)
    def _():
        o_ref[...] = jnp.zeros_like(o_ref)
        halo = jnp.where(first, 0.0, xh_ref[...])
        xe_ref[0:FFN_HALO, :] = halo.astype(BF16)
        xe_ref[FFN_HALO:, :] = x_ref[...].astype(BF16)

    xe = xe_ref[...]

    def branch(w_ref, cw_ref, cb_ref, s_ref, second_half):
        up = _dot(xe, w_ref[...])
        if second_half:
            up = _realign(up, f == nf - 1)
        s_ref[...] = up[FFN_HALO + tt - SUBLANES:FFN_HALO + tt, :]
        w = cw_ref[...]
        x1 = pltpu.roll(up, 1, 0)[FFN_HALO:, :]
        x2 = pltpu.roll(up, 2, 0)[FFN_HALO:, :]
        return cb_ref[...] + w[0:1, :] * x2 + w[1:2, :] * x1 + w[2:3, :] * up[FFN_HALO:, :]

    a = branch(wa_ref, cwa_ref, cba_ref, sa_ref, False)
    g = branch(wg_ref, cwg_ref, cbg_ref, sg_ref, True)
    o_ref[...] += _dot((a * jax.nn.sigmoid(a) * g).astype(BF16), _valid_rows(wd_ref, f))

    @pl.when(f == nf - 1)
    def _():
        o_ref[...] = _layer_norm(DEEPNORM_ALPHA * x_ref[...] + o_ref[...], lw_ref[...], lb_ref[...])


def _ffn_prompt(x3, w_up, w_down, layer, conv_w8, conv_b, lw, lb, *, tt, tf):
    b, t, _ = x3.shape
    nf = D_FFP // tf
    tt = min(tt, t)
    hb = tt // FFN_HALO
    half_a = lambda rows: pl.BlockSpec((rows, tf), lambda bi, ti, fi: (0, fi))
    half_g = lambda rows: pl.BlockSpec((rows, tf), lambda bi, ti, fi: (0, fi + nf))
    cols = _up_cols(tf)
    w_a = pl.BlockSpec((None, pl.Element(D_MODEL), pl.Element(tf)), lambda bi, ti, fi: (layer, 0, cols(fi)))
    w_g = pl.BlockSpec((None, pl.Element(D_MODEL), pl.Element(tf)), lambda bi, ti, fi: (layer, 0, cols(fi + nf)))
    vec = pl.BlockSpec((1, D_MODEL), lambda bi, ti, fi: (0, 0))
    tail = pl.BlockSpec((None, SUBLANES, tf), lambda bi, ti, fi: (bi, 0, fi))
    return pl.pallas_call(
        functools.partial(_ffn_kernel, nf=nf),
        grid=(b, t // tt, nf),
        in_specs=[pl.BlockSpec((None, tt, D_MODEL), lambda bi, ti, fi: (bi, ti, 0), pipeline_mode=pl.Buffered(1)),
                  pl.BlockSpec((None, FFN_HALO, D_MODEL), lambda bi, ti, fi: (bi, jnp.maximum(ti * hb - 1, 0), 0)),
                  w_a, w_g,
                  pl.BlockSpec((None, tf, D_MODEL), lambda bi, ti, fi: (layer, fi, 0)),
                  half_a(SUBLANES), half_g(SUBLANES), half_a(1), half_g(1), vec, vec],
        out_specs=[pl.BlockSpec((None, tt, D_MODEL), lambda bi, ti, fi: (bi, ti, 0)), tail, tail],
        out_shape=[jax.ShapeDtypeStruct((b, t, D_MODEL), F32),
                   jax.ShapeDtypeStruct((b, SUBLANES, D_FFP), F32),
                   jax.ShapeDtypeStruct((b, SUBLANES, D_FFP), F32)],
        scratch_shapes=[pltpu.VMEM((tt + FFN_HALO, D_MODEL), BF16)],
        compiler_params=_cparams("parallel", "arbitrary", "arbitrary"),
        name="ffn_prompt",
    )(x3, x3, w_up, w_up, w_down, conv_w8, conv_w8, conv_b, conv_b, lw, lb)


def _cumsum_rows(x):
    n = x.shape[0]
    rows = lax.broadcasted_iota(jnp.int32, (n, 1), 0)
    d = 1
    while d < n:
        x = x + jnp.where(rows >= d, pltpu.roll(x, d, 0), 0.0)
        d *= 2
    return x


def _log_sigmoid(x):
    return jnp.minimum(x, 0.0) - jnp.log1p(jnp.exp(-jnp.abs(x)))


def _mlstm_kernel(q_ref, k_ref, v_ref, og_ref, g_ref, gb_ref, nw_ref, c0_ref, n0_ref, m0_ref,
                  out_ref, c_ref, n_ref, m_ref, *, L, t_valid):
    ci = pl.program_id(1)
    lin = q_ref.shape[0]

    @pl.when(ci == 0)
    def _():
        c_ref[...] = c0_ref[...]
        n_ref[...] = n0_ref[...]
        m_ref[...] = m0_ref[...]

    def rows_of(ref):
        x = ref[...]
        if lin < L:
            x = jnp.concatenate([x, jnp.zeros((L - lin, x.shape[1]), x.dtype)], axis=0)
        return x

    rows = lax.broadcasted_iota(jnp.int32, (L, 1), 0)
    valid = (ci * L + rows) < t_valid
    pre = rows_of(g_ref) + gb_ref[...]
    lf = jnp.where(valid, _log_sigmoid(pre), 0.0)
    ig = jnp.where(valid, pre, NEG_INF)
    bcum = _cumsum_rows(lf)
    dt = (pltpu.roll(ig, GATE_MF - GATE_MI, 1) - bcum).T
    q_all, k_all, v_all, og_all = rows_of(q_ref), rows_of(k_ref), rows_of(v_ref), rows_of(og_ref)
    tri = lax.broadcasted_iota(jnp.int32, (L, L), 0) >= lax.broadcasted_iota(jnp.int32, (L, L), 1)
    nw = nw_ref[...]

    for h in range(MLSTM_HEADS):
        sl = slice(h * HEAD_DIM, (h + 1) * HEAD_DIM)
        q = q_all[:, sl]
        k = k_all[:, sl] * QK_SCALE
        v = v_all[:, sl]
        qb, kb, vb = q.astype(BF16), k.astype(BF16), v.astype(BF16)
        b_col = bcum[:, GATE_MF + h:GATE_MF + h + 1]
        ig_col = ig[:, GATE_MI + h:GATE_MI + h + 1]
        d_row = dt[GATE_MF + h:GATE_MF + h + 1, :]
        c_prev = c_ref[h]
        n_prev = n_ref[h:h + 1, :]
        m_prev = m_ref[h:h + 1, 0:1]

        dmat = jnp.where(tri, b_col + d_row, NEG_INF)
        g_col = b_col + m_prev
        m_row = jnp.maximum(jnp.max(dmat, axis=1, keepdims=True), g_col)
        a = jnp.exp(dmat - m_row) * _dot_nt(qb, kb)
        w_inter = jnp.exp(g_col - m_row)
        num = _dot(a.astype(BF16), vb) + w_inter * _dot(qb, c_prev.astype(BF16))
        den = jnp.sum(a, axis=1, keepdims=True) + w_inter * jnp.sum(q * n_prev, axis=1, keepdims=True)
        hid = num / jnp.maximum(jnp.abs(den), jnp.exp(-m_row))
        mu = jnp.mean(hid, axis=1, keepdims=True)
        dlt = hid - mu
        var = jnp.mean(dlt * dlt, axis=1, keepdims=True)
        hn = dlt * lax.rsqrt(var + LN_EPS) * nw[:, sl]
        res = jax.nn.sigmoid(og_all[:, sl]) * hn
        out_ref[:, sl] = res[0:lin, :]

        f_tot = b_col[L - 1:L, :]
        w_s = f_tot - b_col + ig_col
        m_new = jnp.maximum(f_tot + m_prev, jnp.max(w_s, axis=0, keepdims=True))
        ws = jnp.exp(w_s - m_new)
        decay = jnp.exp(f_tot + m_prev - m_new)
        c_ref[h] = decay * c_prev + _dot(k.T.astype(BF16), (ws * v).astype(BF16))
        n_ref[h:h + 1, :] = decay * n_prev + jnp.sum(ws * k, axis=0, keepdims=True)
        m_ref[h:h + 1, :] = jnp.broadcast_to(m_new, (1, LANES))


def _mlstm(z, gate_b, norm_w, c0, n0, m0, *, L, lin, t_valid):
    b, tz, _ = z.shape
    nchunks = tz // lin
    qblk = lambda col: pl.BlockSpec((None, lin, D_MLSTM), lambda bi, ci: (bi, ci, col // D_MLSTM))
    st4 = pl.BlockSpec((None, MLSTM_HEADS, HEAD_DIM, HEAD_DIM), lambda bi, ci: (bi, 0, 0, 0))
    st3 = pl.BlockSpec((None, MLSTM_HEADS, LANES), lambda bi, ci: (bi, 0, 0))
    return pl.pallas_call(
        functools.partial(_mlstm_kernel, L=L, t_valid=t_valid),
        grid=(b, nchunks),
        in_specs=[qblk(C_MQ), qblk(C_MK), qblk(C_MV), qblk(C_MO),
                  pl.BlockSpec((None, lin, LANES), lambda bi, ci: (bi, ci, C_GATE // LANES)),
                  pl.BlockSpec((1, LANES), lambda bi, ci: (0, 0)),
                  pl.BlockSpec((1, D_MLSTM), lambda bi, ci: (0, 0)),
                  st4, st3, st3],
        out_specs=[pl.BlockSpec((None, lin, D_MLSTM), lambda bi, ci: (bi, ci, 0)), st4, st3, st3],
        out_shape=[jax.ShapeDtypeStruct((b, tz, D_MLSTM), F32),
                   jax.ShapeDtypeStruct((b, MLSTM_HEADS, HEAD_DIM, HEAD_DIM), F32),
                   jax.ShapeDtypeStruct((b, MLSTM_HEADS, LANES), F32),
                   jax.ShapeDtypeStruct((b, MLSTM_HEADS, LANES), F32)],
        compiler_params=_cparams("parallel", "arbitrary"),
        name="mlstm",
    )(z, z, z, z, z, gate_b, norm_w, c0, n0, m0)


S5_IN_TILES = S5_CH // 256
S5_IN_FEATS = D_S5 // S5_IN_TILES
S5_OUT_TILES = D_S5 // 256
S5_OUT_CH = S5_CH // S5_OUT_TILES


S5_LT = S5_CH // LANES


def _lane_tiles(x):
    return jnp.stack([x[:, k * LANES:(k + 1) * LANES] for k in range(x.shape[1] // LANES)])


def _lane_untile(x):
    return jnp.concatenate([x[k] for k in range(x.shape[0])], axis=1)


def _s5_kernel(u_ref, wb_ref, wc_ref, lr_ref, li_ref, d_ref, gw_ref, gb_ref, x0r_ref, x0i_ref,
               o_ref, xr_ref, xi_ref, sr_ref, si_ref, y_ref, *, steps):
    nb, lc, _ = u_ref.shape

    @pl.when(pl.program_id(0) == 0)
    def _():
        xr_ref[...] = x0r_ref[...]
        xi_ref[...] = x0i_ref[...]

    for b in range(nb):
        ub = u_ref[b].astype(BF16)
        for c in range(S5_IN_TILES):
            bu = _dot(ub[:, c * S5_IN_FEATS:(c + 1) * S5_IN_FEATS], wb_ref[c])
            for k in range(256 // LANES):
                rows = pl.ds(b, lc, stride=nb)
                sr_ref.at[c * (256 // LANES) + k][rows, :] = bu[:, k * LANES:(k + 1) * LANES]
                si_ref.at[c * (256 // LANES) + k][rows, :] = bu[:, 256 + k * LANES:256 + (k + 1) * LANES]
    lam_r = _lane_tiles(jnp.broadcast_to(lr_ref[...], (SUBLANES, S5_CH)))
    lam_i = _lane_tiles(jnp.broadcast_to(li_ref[...], (SUBLANES, S5_CH)))
    row8 = lax.broadcasted_iota(jnp.int32, (1, SUBLANES, 1), 1)
    per_group = SUBLANES // nb

    def group(i, carry):
        xr, xi = carry
        r0 = pl.multiple_of(i * SUBLANES, SUBLANES)
        br = sr_ref[:, pl.ds(r0, SUBLANES), :]
        bi = si_ref[:, pl.ds(r0, SUBLANES), :]
        out_r, out_i = br, bi
        for k in range(per_group):
            nr = lam_r * xr - lam_i * xi + br
            ni = lam_r * xi + lam_i * xr + bi
            here = (row8 >= k * nb) & (row8 < (k + 1) * nb)
            out_r = jnp.where(here, nr, out_r)
            out_i = jnp.where(here, ni, out_i)
            if per_group > 1:
                xr = pltpu.roll(nr, nb, 1)
                xi = pltpu.roll(ni, nb, 1)
            else:
                xr, xi = nr, ni
        sr_ref[:, pl.ds(r0, SUBLANES), :] = out_r
        si_ref[:, pl.ds(r0, SUBLANES), :] = out_i
        return xr, xi

    xr, xi = lax.fori_loop(0, steps * nb // SUBLANES, group, (_lane_tiles(xr_ref[...]), _lane_tiles(xi_ref[...])))
    xr_ref[...] = _lane_untile(xr)
    xi_ref[...] = _lane_untile(xi)
    per_out = S5_OUT_CH // LANES
    for j in range(S5_OUT_TILES):
        lhs_r = jnp.concatenate([sr_ref[j * per_out + k] for k in range(per_out)], axis=1).astype(BF16)
        lhs_i = jnp.concatenate([si_ref[j * per_out + k] for k in range(per_out)], axis=1).astype(BF16)
        y = _dot(lhs_r, wc_ref[0, j]) + _dot(lhs_i, wc_ref[1, j])
        for k in range(256 // LANES):
            y_ref[j * (256 // LANES) + k] = y[:, k * LANES:(k + 1) * LANES]
    for b in range(nb):
        rows = pl.ds(b, lc, stride=nb)
        y = jnp.concatenate([y_ref.at[k][rows, :] for k in range(D_S5 // LANES)], axis=1) + d_ref[...] * u_ref[b]
        zz = jax.nn.gelu(y)
        o_ref[b] = zz * jax.nn.sigmoid(_dot(zz.astype(BF16), gw_ref[...]) + gb_ref[...])


def _s5(z3, wb, wc, lam_r, lam_i, d, glu_w, glu_b, x0r, x0i, *, lc, steps):
    b, tz, _ = z3.shape
    lc = min(lc, tz)
    steps = min(steps, lc)
    assert steps == lc or tz == lc
    full = lambda a: pl.BlockSpec(a.shape, lambda i: (0,) * a.ndim)
    st = pl.BlockSpec((SUBLANES, S5_CH), lambda i: (0, 0))
    return pl.pallas_call(
        functools.partial(_s5_kernel, steps=steps),
        grid=(tz // lc,),
        in_specs=[pl.BlockSpec((b, lc, D_S5), lambda i: (0, i, C_SU // D_S5)),
                  full(wb), full(wc), full(lam_r), full(lam_i), full(d), full(glu_w), full(glu_b), st, st],
        out_specs=[pl.BlockSpec((b, lc, D_S5), lambda i: (0, i, 0)), st, st],
        out_shape=[jax.ShapeDtypeStruct((b, tz, D_S5), F32),
                   jax.ShapeDtypeStruct((SUBLANES, S5_CH), F32),
                   jax.ShapeDtypeStruct((SUBLANES, S5_CH), F32)],
        scratch_shapes=[pltpu.VMEM((S5_LT, lc * b, LANES), F32), pltpu.VMEM((S5_LT, lc * b, LANES), F32),
                        pltpu.VMEM((D_S5 // LANES, lc * b, LANES), F32)],
        compiler_params=_cparams("arbitrary"),
        name="s5",
    )(z3, wb, wc, lam_r, lam_i, d, glu_w, glu_b, x0r, x0i)


N_CMB = 2 * KV_HEADS


def _cmp_project(rows_of, w_ref, o_ref):
    nch = o_ref.shape[0]
    for cmb in range(N_CMB):
        slot = cmb // KV_HEADS
        rows_ref = rows_of(cmb)
        acc = jnp.zeros((nch, 2 * HEAD_DIM), F32)
        for l in range(CMP_STRIDE):
            x = rows_ref[pl.ds(l, nch, stride=CMP_STRIDE), :]
            acc += _dot(x.astype(BF16), w_ref[slot, l])
        o_ref[:, cmb * 2 * HEAD_DIM:(cmb + 1) * 2 * HEAD_DIM] = acc


def _cmp_p_kernel(r0_ref, r1_ref, r2_ref, r3_ref, w_ref, o_ref):
    rows = (r0_ref, r1_ref, r2_ref, r3_ref)
    _cmp_project(lambda cmb: rows[cmb], w_ref, o_ref)


def _cmp_project_prompt(z, wcat):
    b, t, _ = z.shape
    nch = t // CMP_STRIDE
    rows = lambda cmb: pl.BlockSpec((None, t, HEAD_DIM), lambda bi: (bi, 0, C_NKV // HEAD_DIM + cmb))
    return pl.pallas_call(
        _cmp_p_kernel,
        grid=(b,),
        in_specs=[rows(cmb) for cmb in range(N_CMB)] + [pl.BlockSpec(wcat.shape, lambda bi: (0, 0, 0, 0))],
        out_specs=pl.BlockSpec((None, nch, N_CMB * 2 * HEAD_DIM), lambda bi: (bi, 0, 0)),
        out_shape=jax.ShapeDtypeStruct((b, nch, N_CMB * 2 * HEAD_DIM), F32),
        compiler_params=_cparams("parallel"),
        name="cmp_project_prompt",
    )(z, z, z, z, wcat)


def _cmp_s_kernel(pt_ref, *refs):
    del pt_ref
    per_pos = 2 * KV_HEADS
    page_refs = [r.reshape(PAGE_SIZE * per_pos, HEAD_DIM) for r in refs[:CMP_PAGES]]
    w_ref, o_ref = refs[CMP_PAGES:]
    per_page = PAGE_SIZE // CMP_STRIDE
    for cmb in range(N_CMB):
        acc = jnp.zeros((o_ref.shape[0], 2 * HEAD_DIM), F32)
        for l in range(CMP_STRIDE):
            x = jnp.concatenate(
                [page_refs[p][pl.ds(l * per_pos + cmb, per_page, stride=CMP_STRIDE * per_pos), :]
                 for p in range(CMP_PAGES)], axis=0)
            acc += _dot(x.astype(BF16), w_ref[cmb // KV_HEADS, l])
        o_ref[:, cmb * 2 * HEAD_DIM:(cmb + 1) * 2 * HEAD_DIM] = acc


def _page_spec(layer, slot, p, pages_per_step, grid_rank):
    def index_map(*idx):
        bi, si, pt = idx[0], idx[grid_rank - 1], idx[grid_rank]
        return (layer, pt[bi, si * pages_per_step + p], 0, slot, 0, 0)

    return pl.BlockSpec((None, None, PAGE_SIZE, None, KV_HEADS, HEAD_DIM), index_map)


def _cmp_project_sample(cache6, page_table, layer, wcat):
    b, npages = page_table.shape
    steps = npages // CMP_PAGES
    nch = CMP_PAGES * PAGE_SIZE // CMP_STRIDE
    def page_pair(p):
        return pl.BlockSpec((None, None, PAGE_SIZE, 2, KV_HEADS, HEAD_DIM),
                            lambda bi, si, pt: (layer, pt[bi, si * CMP_PAGES + p], 0, 0, 0, 0))

    pages = [page_pair(p) for p in range(CMP_PAGES)]

    grid_spec = pltpu.PrefetchScalarGridSpec(
        num_scalar_prefetch=1,
        grid=(b, steps),
        in_specs=pages + [pl.BlockSpec(wcat.shape, lambda bi, si, pt: (0, 0, 0, 0))],
        out_specs=pl.BlockSpec((None, nch, N_CMB * 2 * HEAD_DIM), lambda bi, si, pt: (bi, si, 0)),
    )
    return pl.pallas_call(
        _cmp_s_kernel,
        grid_spec=grid_spec,
        out_shape=jax.ShapeDtypeStruct((b, steps * nch, N_CMB * 2 * HEAD_DIM), F32),
        compiler_params=_cparams("parallel", "arbitrary"),
        name="cmp_project_sample",
    )(page_table, *([cache6] * CMP_PAGES), wcat)


def _cmp_fin_kernel(p_ref, b1_ref, w2_ref, o_ref):
    nch = p_ref.shape[0]
    for cmb in range(N_CMB):
        slot = cmb // KV_HEADS
        c0 = cmb * 2 * HEAD_DIM
        first = p_ref[:, c0:c0 + HEAD_DIM]
        second = pltpu.roll(p_ref[:, c0 + HEAD_DIM:c0 + 2 * HEAD_DIM], nch - 1, 0)
        hid = b1_ref[slot:slot + 1, :] + first + second
        o_ref[cmb] = _dot(jax.nn.gelu(hid).astype(BF16), w2_ref[slot])


def _cmp_finish(p, b1, w2):
    b, nch, _ = p.shape
    return pl.pallas_call(
        _cmp_fin_kernel,
        grid=(b,),
        in_specs=[pl.BlockSpec((None, nch, p.shape[2]), lambda bi: (bi, 0, 0)),
                  pl.BlockSpec(b1.shape, lambda bi: (0, 0)),
                  pl.BlockSpec(w2.shape, lambda bi: (0, 0, 0))],
        out_specs=pl.BlockSpec((None, N_CMB, nch, HEAD_DIM), lambda bi: (bi, 0, 0, 0)),
        out_shape=jax.ShapeDtypeStruct((b, N_CMB, nch, HEAD_DIM), F32),
        compiler_params=_cparams("parallel"),
        name="cmp_finish",
    )(p, b1, w2)


def _softmax_init(m_scr, l_scr, acc_scr):
    m_scr[...] = jnp.full(m_scr.shape, M_INIT, F32)
    l_scr[...] = jnp.zeros(l_scr.shape, F32)
    acc_scr[...] = jnp.zeros(acc_scr.shape, F32)


def _softmax_update(s, mask, vb, m_scr, l_scr, acc_scr):
    s = jnp.where(mask, s, NEG_INF)
    m_prev = m_scr[...]
    m_new = jnp.maximum(m_prev, jnp.max(s, axis=1, keepdims=True))
    alpha = jnp.exp(m_prev - m_new)
    p = jnp.exp(s - m_new)
    l_scr[...] = alpha * l_scr[...] + jnp.sum(p, axis=1, keepdims=True)
    acc_scr[...] = alpha * acc_scr[...] + _dot(p.astype(BF16), vb)
    m_scr[...] = m_new


def _softmax_result(l_scr, acc_scr):
    l = l_scr[...]
    return acc_scr[...] / jnp.where(l > 0, l, 1.0)


def _masked_probs(s, mask):
    s = jnp.where(mask, s, NEG_INF)
    m = jnp.max(s, axis=1, keepdims=True)
    m = jnp.where(m > NEG_INF, m, 0.0)
    p = jnp.exp(s - m)
    den = jnp.sum(p, axis=1, keepdims=True)
    return p / jnp.where(den > 0, den, 1.0)


def _stack_heads(q):
    return jnp.concatenate([q[:, j * HEAD_DIM:(j + 1) * HEAD_DIM] for j in range(GROUP)], axis=0)


def _fold_lanes(x, op):
    out = x[:, 0:LANES]
    for k in range(1, x.shape[1] // LANES):
        out = op(out, x[:, k * LANES:(k + 1) * LANES])
    return out


def _block_importance(pc, n_cmp, n_sel):
    ncp = pc.shape[1]
    nsp = -(-n_sel // LANES) * LANES
    c_start = lax.broadcasted_iota(jnp.int32, (ncp, nsp), 0) * CMP_STRIDE
    s_start = lax.broadcasted_iota(jnp.int32, (ncp, nsp), 1) * SEL_BLOCK
    overlap = ((c_start < s_start + SEL_BLOCK) & (c_start + CMP_BLOCK > s_start)
               & (c_start < n_cmp * CMP_STRIDE)).astype(F32)
    return jnp.dot(pc, overlap, preferred_element_type=F32, precision=lax.Precision.HIGHEST)


def _force_blocks(imp, blk, cur):
    forced = (blk == 0) | (blk == cur) | (blk == cur - 1)
    imp = jnp.where(forced, FORCE_SCORE, imp)
    return jnp.where(blk > cur, -1.0, imp)


def _select_blocks_cols(pc, cur, n_cmp, n_sel):
    imp = _block_importance(pc, n_cmp, n_sel)
    blk = lax.broadcasted_iota(jnp.int32, (1, imp.shape[1]), 1)
    imp = _force_blocks(imp, blk, cur)

    def body(sp, count):
        col = jnp.sum(jnp.where(blk == sp, imp, 0.0), axis=1, keepdims=True)
        ahead = (col > imp) | ((col == imp) & (sp < blk))
        return count + ahead.astype(F32)

    count = lax.fori_loop(0, n_sel, body, jnp.zeros(imp.shape, F32), unroll=True)
    return ((count < min(SEL_TOPK, n_sel)) & (blk < n_sel)).astype(F32)


def _select_blocks_rows(pc, cur_row, n_cmp, n_sel):
    rows = pc.shape[0]
    nrow = -(-n_sel // SUBLANES) * SUBLANES
    imp_t = _block_importance(pc, n_cmp, n_sel).T[0:nrow, :]
    blk = lax.broadcasted_iota(jnp.int32, (nrow, 1), 0)
    imp_t = _force_blocks(imp_t, blk, cur_row)
    count = jnp.zeros(imp_t.shape, F32)
    for sp in range(n_sel):
        row = imp_t[sp:sp + 1, :]
        count += ((row > imp_t) | ((row == imp_t) & (sp < blk))).astype(F32)
    sel_t = ((count < min(SEL_TOPK, n_sel)) & (blk < n_sel)).astype(F32)
    sel_t = jnp.concatenate([sel_t, jnp.zeros((LANES - nrow, rows), F32)], axis=0)
    return sel_t.T


def _gate(gt, col):
    lane = lax.broadcasted_iota(jnp.int32, (1, LANES), 1)
    return jax.nn.sigmoid(jnp.sum(jnp.where(lane == col, gt, 0.0), axis=1, keepdims=True))


def _nsa_prompt_kernel(q_ref, ks_ref, vs_ref, kw_ref, vw_ref, kc_ref, vc_ref, bc_ref, tz_ref, gt_ref, gb_ref,
                       o_ref, s_scr, selb_scr, mx_scr, l_scr, acc_scr, *, t_len):
    g = pl.program_id(1)
    qt = pl.program_id(2)
    q0 = qt * TQ
    n_cmp = t_len // CMP_STRIDE - (CMP_BLOCK // CMP_STRIDE) + 1
    n_sel = t_len // SEL_BLOCK
    ncp = kc_ref.shape[0]
    rows4 = GROUP * TQ
    qs = _stack_heads(q_ref[...] * QK_SCALE).astype(BF16)
    qpos = q0 + (lax.broadcasted_iota(jnp.int32, (rows4, 1), 0) & (TQ - 1))

    n_idx = lax.broadcasted_iota(jnp.int32, (1, ncp), 1)
    s_c = _dot_nt(qs, kc_ref[...].astype(BF16)) + bc_ref[...].reshape(rows4, ncp)
    cmask = (qpos - (n_idx * CMP_STRIDE + CMP_BLOCK - 1) >= 0) & (n_idx < n_cmp)
    p_c = _masked_probs(s_c, cmask)
    o_c = _dot(p_c.astype(BF16), vc_ref[...].astype(BF16))
    pc = p_c[0:TQ] + p_c[TQ:2 * TQ] + p_c[2 * TQ:3 * TQ] + p_c[3 * TQ:4 * TQ]
    cur_row = jnp.right_shift(q0 + lax.broadcasted_iota(jnp.int32, (1, TQ), 1), SEL_SHIFT)
    sel = _select_blocks_rows(pc, cur_row, n_cmp, n_sel)
    sel_add = ((sel - 1.0) * -MASKED).astype(BF16)

    kt_hi = (q0 + TQ - 1) // TK + 1
    kt_far = jnp.maximum(q0 - (FAR_DIST - 1), 0) // TK
    kidx = lax.broadcasted_iota(jnp.int32, (1, TK), 1)
    srow = lax.broadcasted_iota(jnp.int32, (LANES, TK), 0)

    def spread_selection(kt, carry):
        k0 = kt * TK
        expand = (jnp.right_shift(k0 + lax.broadcasted_iota(jnp.int32, (LANES, TK), 1), SEL_SHIFT) == srow).astype(BF16)
        selb_scr[kt] = _dot(sel_add, expand)
        return carry

    lax.fori_loop(0, kt_hi, spread_selection, 0)

    def tile_bias(k0):
        return tz_ref[jnp.clip((q0 - k0) // TQ, 0, N_TZ - 1)].reshape(rows4, TK)

    def attend(k_ref, v_ref, kt_lo, mask_tile):
        mx_scr[...] = jnp.full(mx_scr.shape, NEG_INF, F32)

        def scores(near):
            def body(kt, carry):
                k0 = pl.multiple_of(kt * TK, TK)
                s = _dot_nt(qs, k_ref[pl.ds(k0, TK), :].astype(BF16))
                if near:
                    s = s + tile_bias(k0)
                s = mask_tile(k0, s, near)
                s_scr[kt] = s
                mx_scr[...] = jnp.maximum(mx_scr[...], _fold_lanes(s, jnp.maximum))
                return carry
            return body

        kt_mid = jnp.maximum(kt_far, kt_lo)
        lax.fori_loop(kt_lo, kt_mid, scores(False), 0)
        lax.fori_loop(kt_mid, kt_hi, scores(True), 0)
        m = jnp.max(mx_scr[...], axis=1, keepdims=True)
        mx_scr[...] = jnp.broadcast_to(jnp.where(m > NEG_INF, m, 0.0), mx_scr.shape)
        l_scr[...] = jnp.zeros(l_scr.shape, F32)
        acc_scr[...] = jnp.zeros(acc_scr.shape, F32)

        def probs(kt, carry):
            k0 = pl.multiple_of(kt * TK, TK)
            m_rep = mx_scr[...]
            p = jnp.exp(s_scr[kt] - jnp.concatenate([m_rep] * (TK // LANES), axis=1))
            l_scr[...] += _fold_lanes(p, jnp.add)
            acc_scr[...] += _dot(p.astype(BF16), v_ref[pl.ds(k0, TK), :].astype(BF16))
            return carry

        lax.fori_loop(kt_lo, kt_hi, probs, 0)
        l = jnp.sum(l_scr[...], axis=1, keepdims=True)
        return acc_scr[...] / jnp.where(l > 0, l, 1.0)

    def sel_mask(k0, s, near):
        s = (s.reshape(GROUP, TQ, TK) + selb_scr[k0 // TK][None]).reshape(rows4, TK)
        return jnp.where(k0 + kidx <= qpos, s, NEG_INF) if near else s

    o_s = attend(ks_ref, vs_ref, 0, sel_mask)

    def win_mask(k0, s, near):
        dist = qpos - (k0 + kidx)
        return jnp.where((dist >= 0) & (dist <= WINDOW), s, NEG_INF)

    o_w = attend(kw_ref, vw_ref, jnp.maximum(q0 - WINDOW, 0) // TK, win_mask)

    gt = gt_ref[...] + gb_ref[...]
    for j in range(GROUP):
        col = GATE_NG + (g * GROUP + j) * 3
        r = slice(j * TQ, (j + 1) * TQ)
        o_ref[:, j * HEAD_DIM:(j + 1) * HEAD_DIM] = (
            _gate(gt, col) * o_c[r] + _gate(gt, col + 1) * o_s[r] + _gate(gt, col + 2) * o_w[r])


def _nsa_prompt(z, kcvc, bias_c, tz, gate_b):
    b, t, _ = z.shape
    ncp = kcvc.shape[2]
    nq = t // TQ
    kv = lambda slot: pl.BlockSpec((None, t, HEAD_DIM), lambda bi, g, qi: (bi, 0, C_NKV // HEAD_DIM + slot * KV_HEADS + g))
    cmp_blk = lambda slot: pl.BlockSpec((None, None, ncp, HEAD_DIM), lambda bi, g, qi: (bi, slot * KV_HEADS + g, 0, 0))
    return pl.pallas_call(
        functools.partial(_nsa_prompt_kernel, t_len=t),
        grid=(b, KV_HEADS, nq),
        in_specs=[pl.BlockSpec((None, TQ, GROUP * HEAD_DIM), lambda bi, g, qi: (bi, qi, C_NQ // (GROUP * HEAD_DIM) + g)),
                  kv(2), kv(3), kv(4), kv(5), cmp_blk(0), cmp_blk(1),
                  pl.BlockSpec((GROUP, TQ, ncp), lambda bi, g, qi: (g, qi, 0)),
                  pl.BlockSpec((N_TZ, GROUP, TQ, TK), lambda bi, g, qi: (0, g, 0, 0)),
                  pl.BlockSpec((None, TQ, LANES), lambda bi, g, qi: (bi, qi, C_GATE // LANES)),
                  pl.BlockSpec((1, LANES), lambda bi, g, qi: (0, 0))],
        out_specs=pl.BlockSpec((None, TQ, GROUP * HEAD_DIM), lambda bi, g, qi: (bi, qi, g)),
        out_shape=jax.ShapeDtypeStruct((b, t, D_NSA), F32),
        scratch_shapes=[pltpu.VMEM((t // TK, GROUP * TQ, TK), F32), pltpu.VMEM((t // TK, TQ, TK), F32),
                        pltpu.VMEM((GROUP * TQ, LANES), F32),
                        pltpu.VMEM((GROUP * TQ, LANES), F32), pltpu.VMEM((GROUP * TQ, HEAD_DIM), F32)],
        compiler_params=_cparams("parallel", "parallel", "arbitrary"),
        name="nsa_prompt",
    )(z, z, z, z, z, kcvc, kcvc, bias_c, tz, z, gate_b)


TS = SUBLANES
ROWS_S = GROUP * TS


def _nsa_s_cmp_kernel(q_ref, kc_ref, vc_ref, bc_ref, oc_ref, sel_ref, *, pos0, n_cmp, n_sel):
    nch = kc_ref.shape[0]
    qs = _stack_heads(q_ref[...] * QK_SCALE).astype(BF16)
    qpos = pos0 + (lax.broadcasted_iota(jnp.int32, (ROWS_S, 1), 0) & (TS - 1))
    n_idx = lax.broadcasted_iota(jnp.int32, (1, nch), 1)
    s_c = _dot_nt(qs, kc_ref[...].astype(BF16)) + bc_ref[...].reshape(ROWS_S, nch)
    cmask = (qpos - (n_idx * CMP_STRIDE + CMP_BLOCK - 1) >= 0) & (n_idx < n_cmp)
    p_c = _masked_probs(s_c, cmask)
    oc_ref[...] = _dot(p_c.astype(BF16), vc_ref[...].astype(BF16))
    pc = p_c[0:TS] + p_c[TS:2 * TS] + p_c[2 * TS:3 * TS] + p_c[3 * TS:4 * TS]
    sel_ref[...] = _select_blocks_cols(pc, jnp.right_shift(qpos[0:TS], SEL_SHIFT), n_cmp, n_sel)


def _nsa_sample_cmp(z8, kcvc, bias_c, *, pos0, n_cmp, n_sel):
    b = z8.shape[0]
    nch = kcvc.shape[2]
    nsp = -(-n_sel // LANES) * LANES
    cmp_blk = lambda slot: pl.BlockSpec((None, None, nch, HEAD_DIM), lambda bi, g: (bi, slot * KV_HEADS + g, 0, 0))
    return pl.pallas_call(
        functools.partial(_nsa_s_cmp_kernel, pos0=pos0, n_cmp=n_cmp, n_sel=n_sel),
        grid=(b, KV_HEADS),
        in_specs=[pl.BlockSpec((None, TS, GROUP * HEAD_DIM), lambda bi, g: (bi, 0, C_NQ // (GROUP * HEAD_DIM) + g)),
                  cmp_blk(0), cmp_blk(1),
                  pl.BlockSpec((GROUP, TS, nch), lambda bi, g: (g, 0, 0))],
        out_specs=[pl.BlockSpec((None, None, ROWS_S, HEAD_DIM), lambda bi, g: (bi, g, 0, 0)),
                   pl.BlockSpec((None, None, TS, nsp), lambda bi, g: (bi, g, 0, 0))],
        out_shape=[jax.ShapeDtypeStruct((b, KV_HEADS, ROWS_S, HEAD_DIM), F32),
                   jax.ShapeDtypeStruct((b, KV_HEADS, TS, nsp), F32)],
        compiler_params=_cparams("parallel", "parallel"),
        name="nsa_sample_cmp",
    )(z8, kcvc, kcvc, bias_c)


def _pad_rows(x, n):
    return jnp.concatenate([x, jnp.zeros((n - x.shape[0], x.shape[1]), x.dtype)], axis=0)


def _nsa_s_sel_kernel(pt_ref, q_ref, sel_ref, kn_ref, vn_ref, bt_ref, bn_ref, bf_ref, *refs, pos0, npages):
    del pt_ref
    rows_pp = PAGE_SIZE * KV_HEADS
    k_refs = [r.reshape(rows_pp, HEAD_DIM) for r in refs[:SEL_PAGES]]
    v_refs = [r.reshape(rows_pp, HEAD_DIM) for r in refs[SEL_PAGES:2 * SEL_PAGES]]
    o_ref, kbuf, vbuf, m_scr, l_scr, acc_scr = refs[2 * SEL_PAGES:]
    step = pl.program_id(1)
    last = step == npages // SEL_PAGES - 1
    rows = KV_HEADS * ROWS_S
    ncol = SEL_TILE * rows_pp
    head_shift = KV_HEADS.bit_length() - 1
    blk_per_page = PAGE_SIZE // SEL_BLOCK

    @pl.when(step == 0)
    def _():
        _softmax_init(m_scr, l_scr, acc_scr)

    nsp = sel_ref.shape[-1]
    ridx = lax.broadcasted_iota(jnp.int32, (rows, 1), 0)
    qpos = pos0 + (ridx & (TS - 1))
    qs = jnp.concatenate([_stack_heads(q_ref[:, g * GROUP * HEAD_DIM:(g + 1) * GROUP * HEAD_DIM] * QK_SCALE)
                          for g in range(KV_HEADS)], axis=0).astype(BF16)
    sel_rows = jnp.concatenate([sel_ref[g] for g in range(KV_HEADS) for _ in range(GROUP)], axis=0)
    bias_far = bf_ref[...].reshape(rows, PAGE_SIZE)[:, 0:1]
    col = lax.broadcasted_iota(jnp.int32, (1, ncol), 1)
    own_head = (col & (KV_HEADS - 1)) == ridx // ROWS_S
    expand = (jnp.right_shift(lax.broadcasted_iota(jnp.int32, (LANES, ncol), 1), SEL_SHIFT + head_shift)
              == lax.broadcasted_iota(jnp.int32, (LANES, ncol), 0)).astype(BF16)
    blk_r = lax.broadcasted_iota(jnp.int32, (nsp, LANES), 0)
    blk_c = lax.broadcasted_iota(jnp.int32, (nsp, LANES), 1)
    n_tiles = SEL_PAGES // SEL_TILE

    for tile in range(n_tiles):
        for i in range(SEL_TILE):
            p = tile * SEL_TILE + i
            kbuf[i * rows_pp:(i + 1) * rows_pp, :] = k_refs[p][...].astype(BF16)
            vbuf[i * rows_pp:(i + 1) * rows_pp, :] = v_refs[p][...].astype(BF16)
        page0 = step * SEL_PAGES + tile * SEL_TILE
        window = ((blk_r == page0 * blk_per_page + blk_c) & (blk_c < SEL_TILE * blk_per_page)).astype(BF16)
        chosen = _dot(_dot(sel_rows.astype(BF16), window).astype(BF16), expand) > 0.5
        s = _dot_nt(qs, kbuf[...]) + bias_far
        if tile == n_tiles - 1:
            tail = jnp.where(last, bt_ref[...].reshape(rows, rows_pp) - bias_far, 0.0)
            s = s + jnp.concatenate([jnp.zeros((rows, ncol - rows_pp), F32), tail], axis=1)
        key = page0 * PAGE_SIZE + jnp.right_shift(col, head_shift)
        _softmax_update(s, chosen & own_head & (key <= qpos), vbuf[...], m_scr, l_scr, acc_scr)

    @pl.when(last)
    def _():
        kidx = lax.broadcasted_iota(jnp.int32, (1, PAGE_SIZE), 1)
        new_blk = lax.broadcasted_iota(jnp.int32, (1, nsp), 1) == npages * blk_per_page
        flag = jnp.sum(jnp.where(new_blk, sel_rows, 0.0), axis=1, keepdims=True) > 0.5
        bias_new = bn_ref[...].reshape(rows, PAGE_SIZE)
        for g in range(KV_HEADS):
            r = slice(g * ROWS_S, (g + 1) * ROWS_S)
            kn = _pad_rows(kn_ref[:, g * HEAD_DIM:(g + 1) * HEAD_DIM], PAGE_SIZE).astype(BF16)
            vn = _pad_rows(vn_ref[:, g * HEAD_DIM:(g + 1) * HEAD_DIM], PAGE_SIZE).astype(BF16)
            mask = flag[r] & (pos0 + kidx <= qpos[r]) & (kidx < TS)
            _softmax_update(_dot_nt(qs[r], kn) + bias_new[r], mask, vn, m_scr.at[r], l_scr.at[r], acc_scr.at[r])
            o_ref[g] = _softmax_result(l_scr.at[r], acc_scr.at[r])


def _nsa_sample_sel(z8, sel, bias_tail, bias_new, bias_far, cache6, page_table, layer, *, pos0):
    b, npages = page_table.shape
    nsp = sel.shape[-1]
    steps = npages // SEL_PAGES
    slot_cols = KV_HEADS * HEAD_DIM
    new_rows = lambda slot: pl.BlockSpec((None, TS, slot_cols),
                                         lambda bi, si, pt: (bi, 0, (C_NKV + slot * slot_cols) // slot_cols))
    whole = lambda a: pl.BlockSpec(a.shape, lambda bi, si, pt: (0,) * a.ndim)
    grid_spec = pltpu.PrefetchScalarGridSpec(
        num_scalar_prefetch=1,
        grid=(b, steps),
        in_specs=[pl.BlockSpec((None, TS, D_NSA), lambda bi, si, pt: (bi, 0, C_NQ // D_NSA)),
                  pl.BlockSpec((None, KV_HEADS, TS, nsp), lambda bi, si, pt: (bi, 0, 0, 0)),
                  new_rows(2), new_rows(3), whole(bias_tail), whole(bias_new), whole(bias_far)]
                 + [_page_spec(layer, 2, p, SEL_PAGES, 2) for p in range(SEL_PAGES)]
                 + [_page_spec(layer, 3, p, SEL_PAGES, 2) for p in range(SEL_PAGES)],
        out_specs=pl.BlockSpec((None, KV_HEADS, ROWS_S, HEAD_DIM), lambda bi, si, pt: (bi, 0, 0, 0)),
        scratch_shapes=[pltpu.VMEM((SEL_TILE * PAGE_SIZE * KV_HEADS, HEAD_DIM), BF16),
                        pltpu.VMEM((SEL_TILE * PAGE_SIZE * KV_HEADS, HEAD_DIM), BF16),
                        pltpu.VMEM((KV_HEADS * ROWS_S, 1), F32), pltpu.VMEM((KV_HEADS * ROWS_S, 1), F32),
                        pltpu.VMEM((KV_HEADS * ROWS_S, HEAD_DIM), F32)],
    )
    return pl.pallas_call(
        functools.partial(_nsa_s_sel_kernel, pos0=pos0, npages=npages),
        grid_spec=grid_spec,
        out_shape=jax.ShapeDtypeStruct((b, KV_HEADS, ROWS_S, HEAD_DIM), F32),
        compiler_params=_cparams("parallel", "arbitrary"),
        name="nsa_sample_sel",
    )(page_table, z8, sel, z8, z8, bias_tail, bias_new, bias_far, *([cache6] * (2 * SEL_PAGES)))


def _nsa_s_win_kernel(q_ref, kw_ref, vw_ref, kn_ref, vn_ref, bw_ref, oc_ref, os_ref, gt_ref, gb_ref, o_ref,
                      m_scr, l_scr, acc_scr, *, pos0):
    g = pl.program_id(1)
    wb = kw_ref.shape[0]
    own = pl.ds(g, wb, stride=KV_HEADS)
    kw = kw_ref.reshape(wb * KV_HEADS, HEAD_DIM)[own, :]
    vw = vw_ref.reshape(wb * KV_HEADS, HEAD_DIM)[own, :]
    qs = _stack_heads(q_ref[...] * QK_SCALE).astype(BF16)
    qpos = pos0 + (lax.broadcasted_iota(jnp.int32, (ROWS_S, 1), 0) & (TS - 1))
    bias = bw_ref[...].reshape(ROWS_S, wb + PAGE_SIZE)
    _softmax_init(m_scr, l_scr, acc_scr)
    dist = qpos - (pos0 - wb + lax.broadcasted_iota(jnp.int32, (1, wb), 1))
    s = _dot_nt(qs, kw.astype(BF16)) + bias[:, 0:wb]
    _softmax_update(s, (dist >= 0) & (dist <= WINDOW), vw.astype(BF16), m_scr, l_scr, acc_scr)
    kidx = lax.broadcasted_iota(jnp.int32, (1, PAGE_SIZE), 1)
    dist = qpos - (pos0 + kidx)
    s = _dot_nt(qs, _pad_rows(kn_ref[...], PAGE_SIZE).astype(BF16)) + bias[:, wb:wb + PAGE_SIZE]
    _softmax_update(s, (dist >= 0) & (dist <= WINDOW) & (kidx < TS), _pad_rows(vn_ref[...], PAGE_SIZE).astype(BF16),
                    m_scr, l_scr, acc_scr)
    o_w = _softmax_result(l_scr, acc_scr)
    o_c = oc_ref[...]
    o_s = os_ref[...]
    gt = gt_ref[...] + gb_ref[...]
    for j in range(GROUP):
        col = GATE_NG + (g * GROUP + j) * 3
        r = slice(j * TS, (j + 1) * TS)
        o_ref[:, j * HEAD_DIM:(j + 1) * HEAD_DIM] = (
            _gate(gt, col) * o_c[r] + _gate(gt, col + 1) * o_s[r] + _gate(gt, col + 2) * o_w[r])


def _nsa_sample_win(z8, win6, layer, bias_w, o_c, o_s, gate_b, *, pos0):
    _, b, wb = win6.shape[:3]
    cached = lambda slot: pl.BlockSpec((None, None, wb, None, KV_HEADS, HEAD_DIM),
                                       lambda bi, g: (layer, bi, 0, slot, 0, 0))
    zcol = lambda col: (lambda bi, g: (bi, 0, col // HEAD_DIM + g))
    part = pl.BlockSpec((None, None, ROWS_S, HEAD_DIM), lambda bi, g: (bi, g, 0, 0))
    return pl.pallas_call(
        functools.partial(_nsa_s_win_kernel, pos0=pos0),
        grid=(b, KV_HEADS),
        in_specs=[pl.BlockSpec((None, TS, GROUP * HEAD_DIM), lambda bi, g: (bi, 0, C_NQ // (GROUP * HEAD_DIM) + g)),
                  cached(0), cached(1),
                  pl.BlockSpec((None, TS, HEAD_DIM), zcol(C_NKV + 4 * KV_HEADS * HEAD_DIM)),
                  pl.BlockSpec((None, TS, HEAD_DIM), zcol(C_NKV + 5 * KV_HEADS * HEAD_DIM)),
                  pl.BlockSpec((GROUP, TS, wb + PAGE_SIZE), lambda bi, g: (g, 0, 0)),
                  part, part,
                  pl.BlockSpec((None, TS, LANES), lambda bi, g: (bi, 0, C_GATE // LANES)),
                  pl.BlockSpec((1, LANES), lambda bi, g: (0, 0))],
        out_specs=pl.BlockSpec((None, TS, GROUP * HEAD_DIM), lambda bi, g: (bi, 0, g)),
        out_shape=jax.ShapeDtypeStruct((b, TS, D_NSA), F32),
        scratch_shapes=[pltpu.VMEM((ROWS_S, 1), F32), pltpu.VMEM((ROWS_S, 1), F32), pltpu.VMEM((ROWS_S, HEAD_DIM), F32)],
        compiler_params=_cparams("parallel", "parallel"),
        name="nsa_sample_win",
    )(z8, win6, win6, z8, z8, bias_w, o_c, o_s, z8, gate_b)


def _t5_bucket(dist):
    n = np.maximum(dist, 0)
    exact = NUM_BUCKETS // 2
    nf = np.maximum(n, 1).astype(np.float32)
    large = exact + (np.log(nf / np.float32(exact)) / np.float32(math.log(MAX_DISTANCE / exact))
                     * np.float32(NUM_BUCKETS - exact)).astype(np.int32)
    return np.where(n < exact, n, np.minimum(large, NUM_BUCKETS - 1)).astype(np.int32)


FAR_DIST = int(np.max(np.nonzero(_t5_bucket(np.arange(4 * MAX_DISTANCE)) < NUM_BUCKETS - 1)[0])) + 1
N_TZ = (FAR_DIST + TK - 2) // TQ + 1


def _bias_table(rel_bias, dist):
    onehot = jax.nn.one_hot(jnp.asarray(_t5_bucket(dist).astype(np.int8)), NUM_BUCKETS, dtype=F32)
    return jnp.einsum("rcb,bh->hrc", onehot, rel_bias, precision=lax.Precision.HIGHEST)


def _pack_weights(p):
    w_in = p["w_in"].astype(BF16)
    pad = jnp.zeros(w_in.shape[:2] + (N_IN - C_GATE - 32,), BF16)
    w_in_p = jnp.concatenate([w_in[..., 0:2048], w_in[..., 2056:3080], w_in[..., 3080:4616], w_in[..., 4640:5152],
                              w_in[..., 2048:2056], w_in[..., 4616:4640], pad], axis=2)
    return dict(
        w_in=w_in_p, w_out=p["w_out"].astype(BF16), w_down=p["ffn_w_down"].astype(BF16),
        w_up=p["ffn_w_up"].astype(BF16))


def _pack_layer(p, big, l):
    gb = p["mlstm_gate_b"][l]
    mlstm_gate_b = jnp.zeros((1, LANES), F32).at[0, GATE_MI:GATE_MI + 4].set(gb[0]).at[0, GATE_MF:GATE_MF + 4].set(gb[1])
    nsa_gate_b = jnp.zeros((1, LANES), F32).at[0, GATE_NG:GATE_NG + 3 * NSA_HEADS].set(p["nsa_gate_b"][l].reshape(-1))
    w1 = p["cmp_w1"][l]
    wcat = jnp.concatenate([w1[:, 0:CMP_STRIDE], w1[:, CMP_STRIDE:CMP_BLOCK]], axis=-1).astype(BF16)
    lam = lax.complex(p["s5_a_re"][l], p["s5_a_im"][l])
    lam_bar = jnp.exp(lam * jnp.exp(p["s5_log_step"][l])[:, None])
    b_bar = ((lam_bar - 1.0) / lam)[..., None] * lax.complex(p["s5_b_re"][l], p["s5_b_im"][l])
    gi = S5_GROUPS // S5_IN_TILES
    bd_in = lambda m: jnp.einsum("cgph,gk->cghkp", m.reshape(S5_IN_TILES, gi, S5_STATE, S5_GROUP_WIDTH),
                                 jnp.eye(gi, dtype=F32)).reshape(S5_IN_TILES, S5_IN_FEATS, 256)
    wb = jnp.concatenate([bd_in(b_bar.real), bd_in(b_bar.imag)], axis=2).astype(BF16)
    go = S5_GROUPS // S5_OUT_TILES
    bd_out = lambda m: jnp.einsum("jghp,gk->jgpkh", m.reshape(S5_OUT_TILES, go, S5_GROUP_WIDTH, S5_STATE),
                                  jnp.eye(go, dtype=F32)).reshape(S5_OUT_TILES, S5_OUT_CH, 256)
    wc = jnp.stack([bd_out(p["s5_c_re"][l]), -bd_out(p["s5_c_im"][l])]).astype(BF16)
    half = lambda w: jnp.pad(w, ((0, 0), (0, D_FFP - D_FF)))
    conv_w = p["ffn_conv_w"][l]
    conv_b = p["ffn_conv_b"][l][None, :]
    return dict(
        big, layer=l, mlstm_gate_b=mlstm_gate_b, mlstm_norm_w=p["mlstm_norm_w"][l][None, :], nsa_gate_b=nsa_gate_b,
        wcat=wcat, cmp_b1=p["cmp_b1"][l], cmp_w2=p["cmp_w2"][l].astype(BF16),
        s5_wb=wb, s5_wc=wc, s5_lam_r=lam_bar.real.reshape(1, S5_CH), s5_lam_i=lam_bar.imag.reshape(1, S5_CH),
        s5_d=p["s5_d"][l][None, :], s5_glu_w=p["s5_glu_w"][l].astype(BF16), s5_glu_b=p["s5_glu_b"][l][None, :],
        ln1_w=p["ln1_w"][l][None, :], ln1_b=p["ln1_b"][l][None, :],
        conv_w8=jnp.pad(jnp.concatenate([half(conv_w[:, :D_FF]), half(conv_w[:, D_FF:])], axis=1),
                        ((0, SUBLANES - CONV_W), (0, 0))),
        conv_b=jnp.concatenate([half(conv_b[:, :D_FF]), half(conv_b[:, D_FF:])], axis=1),
        ln2_w=p["ln2_w"][l][None, :], ln2_b=p["ln2_b"][l][None, :],
    )


def _unpad_ff(x):
    return jnp.concatenate([x[..., :D_FF], x[..., D_FFP:D_FFP + D_FF]], axis=-1)


def _pad_ff(x):
    pad = [(0, 0)] * (x.ndim - 1) + [(0, D_FFP - D_FF)]
    return jnp.concatenate([jnp.pad(x[..., :D_FF], pad), jnp.pad(x[..., D_FF:], pad)], axis=-1)


def _mixer_tail(x2d, z, b, t, t_use, lw, o_mlstm, o_nsa, s5_state, conv_state8, tm):
    pad8 = lambda s: jnp.pad(s.reshape(b, S5_CH), ((0, SUBLANES - b), (0, 0)))
    o_s5, xr, xi = _s5(z.reshape(b, t, N_IN), lw["s5_wb"], lw["s5_wc"], lw["s5_lam_r"], lw["s5_lam_i"], lw["s5_d"],
                       lw["s5_glu_w"], lw["s5_glu_b"], pad8(s5_state[0]), pad8(s5_state[1]), lc=S5_LC, steps=t_use)
    o_s5 = o_s5.reshape(b * t, D_S5)
    layer = lw["layer"]
    x1 = _wout_ln(o_mlstm.reshape(b * t, D_MLSTM), o_nsa.reshape(b * t, D_NSA), o_s5, x2d,
                  lw["w_out"], layer, lw["ln1_w"], lw["ln1_b"], tm=512)
    s5_new = (xr[:b].reshape(b, S5_GROUPS, S5_STATE), xi[:b].reshape(b, S5_GROUPS, S5_STATE))
    if conv_state8 is None:
        x2, tail_a, tail_g = _ffn_prompt(x1.reshape(b, t, D_MODEL), lw["w_up"], lw["w_down"], layer, lw["conv_w8"],
                                         lw["conv_b"], lw["ln2_w"], lw["ln2_b"], tt=FFN_TT, tf=FFN_TF)
        keep = slice(SUBLANES - (CONV_W - 1), SUBLANES)
        conv_new = jnp.concatenate([tail_a[:, keep, :D_FF], tail_g[:, keep, :D_FF]], axis=-1)
        return x2.reshape(b * t, D_MODEL), s5_new, conv_new
    up = _matmul(x1, lw["w_up"], layer, tm=tm, tn=FFN_TF, n=2 * D_FFP, w_cols=_up_cols(FFN_TF))
    hgate = _convgate(up.reshape(b, t, 2 * D_FFP), conv_state8, lw["conv_w8"], lw["conv_b"], tf=FFN_TF)
    x2 = _down_ln(hgate.reshape(b * t, D_FFP), lw["w_down"], layer, x1, lw["ln2_w"], lw["ln2_b"], tm=512, tk=1408)
    conv_new = _unpad_ff(up.reshape(b, t, 2 * D_FFP)[:, t_use - (CONV_W - 1):t_use])
    return x2, s5_new, conv_new


def _prompt_layer(x2d, b, t, lw, rel_bias, rows_so_far):
    z, kv_rows, win_rows = _proj_in(x2d, lw["w_in"], lw["layer"], rows_so_far, tm=min(1024, b * t), tn=768)
    z3 = z.reshape(b, t, N_IN)
    zeros = lambda *s: jnp.zeros(s, F32)
    o_mlstm, c1, n1, m1 = _mlstm(z3, lw["mlstm_gate_b"], lw["mlstm_norm_w"],
                                 zeros(b, MLSTM_HEADS, HEAD_DIM, HEAD_DIM), zeros(b, MLSTM_HEADS, LANES),
                                 zeros(b, MLSTM_HEADS, LANES), L=MLSTM_L, lin=MLSTM_L, t_valid=t)
    kcvc = _cmp_finish(_cmp_project_prompt(z3, lw["wcat"]), lw["cmp_b1"], lw["cmp_w2"])
    ncp = t // CMP_STRIDE
    bias_c = _bias_table(rel_bias, np.arange(t)[:, None] - (np.arange(ncp) * CMP_STRIDE + CMP_BLOCK - 1)[None, :])
    ti = np.arange(TQ)[:, None] - np.arange(TK)[None, :]
    far = _bias_table(rel_bias, np.full((1, 1), FAR_DIST))
    tz = jnp.stack([_bias_table(rel_bias, d * TQ + ti) - far for d in range(N_TZ)])
    o_nsa = _nsa_prompt(z3, kcvc, bias_c, tz, lw["nsa_gate_b"])
    x2, s5_new, conv_new = _mixer_tail(
        x2d, z, b, t, t, lw, o_mlstm, o_nsa, (zeros(b, S5_GROUPS, S5_STATE), zeros(b, S5_GROUPS, S5_STATE)),
        None, tm=1024)
    state = (None, None, c1, n1, m1[:, :, 0], s5_new[0], s5_new[1], conv_new)
    return x2, state, (kv_rows, win_rows)


def _sample_layer(x2d, b, tn, lw, rel_bias, layer, cache6, page_table, win6, mlstm_state, s5_state, conv_state):
    npages = page_table.shape[1]
    pos0 = npages * PAGE_SIZE
    z = _matmul(x2d, lw["w_in"], lw["layer"], tm=1024, tn=768)
    z8 = z.reshape(b, TS, N_IN)
    c0, n0, m0 = mlstm_state
    o_mlstm, c1, n1, m1 = _mlstm(z8, lw["mlstm_gate_b"], lw["mlstm_norm_w"], c0, n0,
                                 jnp.broadcast_to(m0[:, :, None], (b, MLSTM_HEADS, LANES)),
                                 L=LANES, lin=TS, t_valid=tn)
    n_chunks = (pos0 + tn) // CMP_STRIDE
    n_cmp = n_chunks - CMP_BLOCK // CMP_STRIDE + 1
    n_sel = -(-(pos0 + tn) // SEL_BLOCK)
    kcvc = _cmp_finish(_cmp_project_sample(cache6, page_table, layer, lw["wcat"]), lw["cmp_b1"], lw["cmp_w2"])
    qpos = pos0 + np.arange(TS)[:, None]
    bias_c = _bias_table(rel_bias, qpos - (np.arange(n_chunks) * CMP_STRIDE + CMP_BLOCK - 1)[None, :])
    o_c, sel = _nsa_sample_cmp(z8, kcvc, bias_c, pos0=pos0, n_cmp=n_cmp, n_sel=n_sel)
    kk = np.arange(PAGE_SIZE)[None, :]
    bias_tail = _bias_table(rel_bias, np.repeat(qpos - (pos0 - PAGE_SIZE + kk), KV_HEADS, axis=1))
    bias_new = _bias_table(rel_bias, qpos - (pos0 + kk))
    bias_far = _bias_table(rel_bias, np.broadcast_to(qpos - (pos0 - 2 * PAGE_SIZE), (TS, PAGE_SIZE)))
    o_s = _nsa_sample_sel(z8, sel, bias_tail, bias_new, bias_far, cache6, page_table, layer, pos0=pos0)
    wb = win6.shape[2]
    wk = np.arange(wb)[None, :]
    bias_w = _bias_table(rel_bias, np.concatenate([qpos - (pos0 - wb + wk), qpos - (pos0 + kk)], axis=1))
    o_nsa = _nsa_sample_win(z8, win6, layer, bias_w, o_c, o_s, lw["nsa_gate_b"], pos0=pos0)
    conv_state8 = jnp.pad(_pad_ff(conv_state), ((0, 0), (SUBLANES - (CONV_W - 1), 0), (0, 0)))
    x2, s5_new, conv_new = _mixer_tail(x2d, z, b, TS, tn, lw, o_mlstm, o_nsa, s5_state, conv_state8, tm=1024)
    nkv = z8[:, :tn, C_NKV:C_NKV + N_KV_SLOTS * KV_HEADS * HEAD_DIM].reshape(b, tn, N_KV_SLOTS, KV_HEADS, HEAD_DIM)
    state = (nkv[:, :, :4], nkv[:, :, 4:], c1, n1, m1[:, :, 0], s5_new[0], s5_new[1], conv_new)
    return x2, state


def kernel(x_prompt, x_sample, cache_nsa_kv, cache_win_kv, state_mlstm_c, state_mlstm_n, state_mlstm_m,
           state_s5_re, state_s5_im, state_ffn_conv, page_table, w_in, mlstm_gate_b, mlstm_norm_w,
           nsa_gate_b, cmp_w1, cmp_b1, cmp_w2, rel_bias, s5_a_re, s5_a_im, s5_b_re, s5_b_im, s5_c_re,
           s5_c_im, s5_d, s5_log_step, s5_glu_w, s5_glu_b, w_out, ln1_w, ln1_b, ffn_w_up, ffn_conv_w,
           ffn_conv_b, ffn_w_down, ln2_w, ln2_b):
    params = dict(w_in=w_in, mlstm_gate_b=mlstm_gate_b, mlstm_norm_w=mlstm_norm_w, nsa_gate_b=nsa_gate_b,
                  cmp_w1=cmp_w1, cmp_b1=cmp_b1, cmp_w2=cmp_w2, s5_a_re=s5_a_re, s5_a_im=s5_a_im, s5_b_re=s5_b_re,
                  s5_b_im=s5_b_im, s5_c_re=s5_c_re, s5_c_im=s5_c_im, s5_d=s5_d, s5_log_step=s5_log_step,
                  s5_glu_w=s5_glu_w, s5_glu_b=s5_glu_b, w_out=w_out, ln1_w=ln1_w, ln1_b=ln1_b, ffn_w_up=ffn_w_up,
                  ffn_conv_w=ffn_conv_w, ffn_conv_b=ffn_conv_b, ffn_w_down=ffn_w_down, ln2_w=ln2_w, ln2_b=ln2_b)
    depth = w_in.shape[0]
    bp, tp, _ = x_prompt.shape
    bs, tn, _ = x_sample.shape
    assert tp % TK == 0 and tn < CMP_STRIDE and tn <= TS and SUBLANES % bp == 0 and SUBLANES % bs == 0
    assert (tn * bs) % SUBLANES == 0 and tp % min(FFN_TT, tp) == 0
    assert page_table.shape[1] % CMP_PAGES == 0 and cache_nsa_kv.shape[2] == PAGE_SIZE
    xp = x_prompt.reshape(bp * tp, D_MODEL)
    xs = jnp.pad(x_sample, ((0, 0), (0, TS - tn), (0, 0))).reshape(bs * TS, D_MODEL)
    p_states, s_states = [], []
    big = _pack_weights(params)
    rows_so_far = None
    for l in range(depth):
        lw = _pack_layer(params, big, l)
        xp, sp, rows_so_far = _prompt_layer(xp, bp, tp, lw, rel_bias, rows_so_far)
        xs, ss = _sample_layer(xs, bs, tn, lw, rel_bias, l, cache_nsa_kv, page_table, cache_win_kv,
                               (state_mlstm_c[l], jnp.pad(state_mlstm_n[l], ((0, 0), (0, 0), (0, LANES - HEAD_DIM))),
                                state_mlstm_m[l]),
                               (state_s5_re[l], state_s5_im[l]), state_ffn_conv[l])
        p_states.append(sp)
        s_states.append(ss)
    stk = lambda states, i: jnp.stack([s[i] for s in states])
    y_prompt = xp.reshape(bp, tp, D_MODEL)
    y_sample = xs.reshape(bs, TS, D_MODEL)[:, :tn]
    wrows = min(WINDOW, tp)
    nsa_kv_p = rows_so_far[0].reshape(depth, bp, tp, 4, KV_HEADS, HEAD_DIM)
    win_kv_p = rows_so_far[1].reshape(depth, bp, tp, 2, KV_HEADS, HEAD_DIM)[:, :, tp - wrows:]
    return (y_prompt, y_sample,
            nsa_kv_p, stk(s_states, 0), win_kv_p, stk(s_states, 1),
            stk(p_states, 2), stk(s_states, 2), stk(p_states, 3), stk(s_states, 3), stk(p_states, 4), stk(s_states, 4),
            stk(p_states, 5), stk(s_states, 5), stk(p_states, 6), stk(s_states, 6), stk(p_states, 7), stk(s_states, 7))
```

```python
import functools
import math

import jax
import jax.numpy as jnp
import numpy as np
from jax import lax
from jax.experimental import pallas as pl
from jax.experimental.pallas import tpu as pltpu

F32 = jnp.float32
BF16 = jnp.bfloat16
NEG_INF = float("-inf")
M_INIT = -1e30
MASKED = -1e30

D_MODEL = 2048
PAGE_SIZE = 128
D_MLSTM = D_MODEL // 4
D_NSA = D_MODEL // 2
D_S5 = D_MODEL - D_MLSTM - D_NSA
HEAD_DIM = 128
MLSTM_HEADS = D_MLSTM // HEAD_DIM
NSA_HEADS = D_NSA // HEAD_DIM
KV_HEADS = 2
GROUP = NSA_HEADS // KV_HEADS
N_KV_SLOTS = 6
CMP_BLOCK = 32
CMP_STRIDE = 16
SEL_BLOCK = 64
SEL_TOPK = 16
FORCE_SCORE = 1e4
WINDOW = 512
S5_GROUP_WIDTH = 16
S5_GROUPS = D_S5 // S5_GROUP_WIDTH
S5_STATE = 64
S5_CH = S5_GROUPS * S5_STATE
NUM_BUCKETS = 32
MAX_DISTANCE = 128
D_FF = ((8 * D_MODEL // 3 + 127) // 128) * 128
CONV_W = 3
LN_EPS = 1e-5
DEPTH = 2
DEEPNORM_ALPHA = (2 * DEPTH) ** 0.25
QK_SCALE = HEAD_DIM ** -0.5

LANES = 128
SUBLANES = 8
V7X_VMEM_LIMIT = 56 * 2 ** 20

C_MQ, C_MK, C_MV, C_MO = 0, 512, 1024, 1536
C_NQ = 2048
C_NKV = 3072
C_SU = 4608
C_GATE = 5120
N_IN = 5376
GATE_MI, GATE_MF, GATE_NG = 0, 4, 8
D_FFP = 5632
MLSTM_L = 512
TQ = 256
TK = 512
S5_LC = 256
FFN_TT = 1024
FFN_TF = 512
FFN_HALO = 16
CMP_PAGES = 32
SEL_PAGES = 32
SEL_TILE = 32
SEL_SHIFT = SEL_BLOCK.bit_length() - 1


def _cparams(*sem):
    return pltpu.CompilerParams(dimension_semantics=sem, vmem_limit_bytes=V7X_VMEM_LIMIT)


def _dot(a, b):
    return jnp.dot(a, b, preferred_element_type=F32)


def _dot_nt(a, b):
    return lax.dot_general(a, b, (((1,), (1,)), ((), ())), preferred_element_type=F32)


def _layer_norm(y, w, b):
    mu = jnp.mean(y, axis=-1, keepdims=True)
    d = y - mu
    var = jnp.mean(d * d, axis=-1, keepdims=True)
    return d * lax.rsqrt(var + LN_EPS) * w + b


def _mm_kernel(x_ref, w_ref, o_ref, *, realign_tile=None):
    acc = _dot(x_ref[...].astype(BF16), w_ref[...])
    if realign_tile is not None:
        acc = _realign(acc, pl.program_id(1) == realign_tile)
    o_ref[...] = acc.astype(o_ref.dtype)


UP_OVER = D_FFP - D_FF


def _up_cols(tf):
    nf, per = D_FFP // tf, tf // LANES
    return lambda j: (j * per + (j >= nf).astype(jnp.int32) * (D_FF // LANES - nf * per)
                      - (j == 2 * nf - 1).astype(jnp.int32) * (UP_OVER // LANES)) * LANES


def _realign(up, is_last):
    shifted = jnp.concatenate([up[:, UP_OVER:], jnp.zeros((up.shape[0], UP_OVER), up.dtype)], axis=1)
    return jnp.where(is_last, shifted, up)


def _matmul(x, w, layer, *, tm, tn, n=None, w_cols=None):
    m, k = x.shape
    n = w.shape[2] if n is None else n
    tm = min(tm, m)
    if w_cols is None:
        w_spec = pl.BlockSpec((None, k, tn), lambda i, j: (layer, 0, j))
    else:
        w_spec = pl.BlockSpec((None, pl.Element(k), pl.Element(tn)), lambda i, j: (layer, 0, w_cols(j)))
    return pl.pallas_call(
        functools.partial(_mm_kernel, realign_tile=None if w_cols is None else n // tn - 1),
        grid=(m // tm, n // tn),
        in_specs=[pl.BlockSpec((tm, k), lambda i, j: (i, 0)), w_spec],
        out_specs=pl.BlockSpec((tm, tn), lambda i, j: (i, j)),
        out_shape=jax.ShapeDtypeStruct((m, n), F32),
        compiler_params=_cparams("parallel", "parallel"),
        name="proj",
    )(x, w)


KV4_GROUPS = 4 * KV_HEADS
WIN_GROUPS = 2 * KV_HEADS


def _proj_in_kernel(x_ref, w_ref, z_ref, kv_ref, win_ref, xb_ref, *, tn):
    j = pl.program_id(1)

    @pl.when(j == 0)
    def _():
        xb_ref[...] = x_ref[...].astype(BF16)

    acc = _dot(xb_ref[...], w_ref[...])
    z_ref[...] = acc
    tm = acc.shape[0]
    per_tile = tn // HEAD_DIM
    first_tile = C_NKV // tn
    for tile in range(first_tile, first_tile + N_KV_SLOTS * KV_HEADS // per_tile):
        @pl.when(j == tile)
        def _(tile=tile):
            for c in range(per_tile):
                grp = (tile - first_tile) * per_tile + c
                val = acc[:, c * HEAD_DIM:(c + 1) * HEAD_DIM]
                if grp < KV4_GROUPS:
                    kv_ref[pl.ds(grp, tm, stride=KV4_GROUPS), :] = val
                else:
                    win_ref[pl.ds(grp - KV4_GROUPS, tm, stride=WIN_GROUPS), :] = val


def _proj_in_carry_kernel(x_ref, w_ref, kv_in_ref, win_in_ref, z_ref, kv_ref, win_ref, xb_ref, *, tn):
    del kv_in_ref, win_in_ref
    _proj_in_kernel(x_ref, w_ref, z_ref, kv_ref, win_ref, xb_ref, tn=tn)


def _proj_in(x, w, layer, rows_so_far, *, tm, tn):
    m, k = x.shape
    depth, _, n = w.shape
    nm = m // tm
    assert C_NKV % tn == 0 and (N_KV_SLOTS * KV_HEADS * HEAD_DIM) % tn == 0
    carried = () if rows_so_far is None else tuple(rows_so_far)
    return pl.pallas_call(
        functools.partial(_proj_in_carry_kernel if carried else _proj_in_kernel, tn=tn),
        grid=(nm, n // tn),
        in_specs=[pl.BlockSpec((tm, k), lambda i, j: (i, 0)), pl.BlockSpec((None, k, tn), lambda i, j: (layer, 0, j))]
                 + [pl.BlockSpec(memory_space=pl.ANY)] * len(carried),
        out_specs=[pl.BlockSpec((tm, tn), lambda i, j: (i, j)),
                   pl.BlockSpec((tm * KV4_GROUPS, HEAD_DIM), lambda i, j: (layer * nm + i, 0)),
                   pl.BlockSpec((tm * WIN_GROUPS, HEAD_DIM), lambda i, j: (layer * nm + i, 0))],
        out_shape=[jax.ShapeDtypeStruct((m, n), F32),
                   jax.ShapeDtypeStruct((depth * m * KV4_GROUPS, HEAD_DIM), F32),
                   jax.ShapeDtypeStruct((depth * m * WIN_GROUPS, HEAD_DIM), F32)],
        input_output_aliases={2: 1, 3: 2} if carried else {},
        scratch_shapes=[pltpu.VMEM((tm, k), BF16)],
        compiler_params=_cparams("parallel", "arbitrary"),
        name="proj_in",
    )(x, w, *carried)


def _wout_kernel(om_ref, on_ref, os_ref, x_ref, w_ref, lw_ref, lb_ref, o_ref):
    acc = _dot(om_ref[...].astype(BF16), w_ref[0:D_MLSTM, :])
    acc += _dot(on_ref[...].astype(BF16), w_ref[D_MLSTM:D_MLSTM + D_NSA, :])
    acc += _dot(os_ref[...].astype(BF16), w_ref[D_MLSTM + D_NSA:D_MODEL, :])
    o_ref[...] = _layer_norm(DEEPNORM_ALPHA * x_ref[...] + acc, lw_ref[...], lb_ref[...])


def _wout_ln(om, on, os_, x, w, layer, lw, lb, *, tm):
    m = x.shape[0]
    tm = min(tm, m)
    row = lambda width: pl.BlockSpec((tm, width), lambda i: (i, 0))
    full = lambda a: pl.BlockSpec(a.shape, lambda i: (0, 0))
    w_spec = pl.BlockSpec((None,) + w.shape[1:], lambda i: (layer, 0, 0))
    return pl.pallas_call(
        _wout_kernel,
        grid=(m // tm,),
        in_specs=[row(D_MLSTM), row(D_NSA), row(D_S5), row(D_MODEL), w_spec, full(lw), full(lb)],
        out_specs=row(D_MODEL),
        out_shape=jax.ShapeDtypeStruct((m, D_MODEL), F32),
        compiler_params=_cparams("parallel"),
        name="wout_ln",
    )(om, on, os_, x, w, lw, lb)


def _valid_rows(w_ref, tile):
    rows = w_ref.shape[0]
    row = tile * rows + lax.broadcasted_iota(jnp.int32, (rows, 1), 0)
    w = w_ref[...]
    return jnp.where(row < D_FF, w, jnp.zeros_like(w))


def _down_kernel(h_ref, w_ref, x_ref, lw_ref, lb_ref, o_ref, acc_ref, *, nk):
    k = pl.program_id(1)

    @pl.when(k == 0)
    def _():
        acc_ref[...] = jnp.zeros_like(acc_ref)

    acc_ref[...] += _dot(h_ref[...], _valid_rows(w_ref, k))

    @pl.when(k == nk - 1)
    def _():
        o_ref[...] = _layer_norm(DEEPNORM_ALPHA * x_ref[...] + acc_ref[...], lw_ref[...], lb_ref[...])


def _down_ln(h, w, layer, x, lw, lb, *, tm, tk):
    m, kk = h.shape
    tm = min(tm, m)
    nk = kk // tk
    return pl.pallas_call(
        functools.partial(_down_kernel, nk=nk),
        grid=(m // tm, nk),
        in_specs=[
            pl.BlockSpec((tm, tk), lambda i, k: (i, k)),
            pl.BlockSpec((None, tk, D_MODEL), lambda i, k: (layer, k, 0)),
            pl.BlockSpec((tm, D_MODEL), lambda i, k: (i, 0)),
            pl.BlockSpec((1, D_MODEL), lambda i, k: (0, 0)),
            pl.BlockSpec((1, D_MODEL), lambda i, k: (0, 0)),
        ],
        out_specs=pl.BlockSpec((tm, D_MODEL), lambda i, k: (i, 0)),
        out_shape=jax.ShapeDtypeStruct((m, D_MODEL), F32),
        scratch_shapes=[pltpu.VMEM((tm, D_MODEL), F32)],
        compiler_params=_cparams("parallel", "arbitrary"),
        name="down_ln",
    )(h, w, x, lw, lb)


def _convgate_kernel(a_ref, g_ref, sa_ref, sg_ref, wa_ref, wg_ref, ba_ref, bg_ref, o_ref):
    rows = lax.broadcasted_iota(jnp.int32, (1, SUBLANES, 1), 1)

    def conv(cur_ref, st_ref, w_ref, b_ref):
        cur = cur_ref[...]
        prev = st_ref[...]
        p1 = prev[:, 7:8, :]
        p2 = prev[:, 6:7, :]
        x1 = jnp.where(rows == 0, p1, pltpu.roll(cur, 1, 1))
        x2 = jnp.where(rows == 0, p2, jnp.where(rows == 1, p1, pltpu.roll(cur, 2, 1)))
        w = w_ref[...]
        return b_ref[...] + w[0:1, :] * x2 + w[1:2, :] * x1 + w[2:3, :] * cur

    a = conv(a_ref, sa_ref, wa_ref, ba_ref)
    g = conv(g_ref, sg_ref, wg_ref, bg_ref)
    o_ref[...] = (a * jax.nn.sigmoid(a) * g).astype(o_ref.dtype)


def _convgate(up, state8, conv_w8, conv_b, *, tf):
    b, t, _ = up.shape
    assert t == SUBLANES
    nf = D_FFP // tf
    cur_a = pl.BlockSpec((b, t, tf), lambda fi: (0, 0, fi))
    cur_g = pl.BlockSpec((b, t, tf), lambda fi: (0, 0, fi + nf))
    w_a = pl.BlockSpec((SUBLANES, tf), lambda fi: (0, fi))
    w_g = pl.BlockSpec((SUBLANES, tf), lambda fi: (0, fi + nf))
    b_a = pl.BlockSpec((1, tf), lambda fi: (0, fi))
    b_g = pl.BlockSpec((1, tf), lambda fi: (0, fi + nf))
    return pl.pallas_call(
        _convgate_kernel,
        grid=(nf,),
        in_specs=[cur_a, cur_g, cur_a, cur_g, w_a, w_g, b_a, b_g],
        out_specs=pl.BlockSpec((b, t, tf), lambda fi: (0, 0, fi)),
        out_shape=jax.ShapeDtypeStruct((b, t, D_FFP), BF16),
        compiler_params=_cparams("parallel"),
        name="convgate",
    )(up, up, state8, state8, conv_w8, conv_w8, conv_b, conv_b)


def _ffn_kernel(x_ref, xh_ref, wa_ref, wg_ref, wd_ref, cwa_ref, cwg_ref, cba_ref, cbg_ref, lw_ref, lb_ref,
                o_ref, sa_ref, sg_ref, xe_ref, *, nf):
    first = pl.program_id(1) == 0
    f = pl.program_id(2)
    tt = x_ref.shape[0]

    @pl.when(f == 0)
    def _():
        o_ref[...] = jnp.zeros_like(o_ref)
        halo = jnp.where(first, 0.0, xh_ref[...])
        xe_ref[0:FFN_HALO, :] = halo.astype(BF16)
        xe_ref[FFN_HALO:, :] = x_ref[...].astype(BF16)

    xe = xe_ref[...]

    def branch(w_ref, cw_ref, cb_ref, s_ref, second_half):
        up = _dot(xe, w_ref[...])
        if second_half:
            up = _realign(up, f == nf - 1)
        s_ref[...] = up[FFN_HALO + tt - SUBLANES:FFN_HALO + tt, :]
        w = cw_ref[...]
        x1 = pltpu.roll(up, 1, 0)[FFN_HALO:, :]
        x2 = pltpu.roll(up, 2, 0)[FFN_HALO:, :]
        return cb_ref[...] + w[0:1, :] * x2 + w[1:2, :] * x1 + w[2:3, :] * up[FFN_HALO:, :]

    a = branch(wa_ref, cwa_ref, cba_ref, sa_ref, False)
    g = branch(wg_ref, cwg_ref, cbg_ref, sg_ref, True)
    o_ref[...] += _dot((a * jax.nn.sigmoid(a) * g).astype(BF16), _valid_rows(wd_ref, f))

    @pl.when(f == nf - 1)
    def _():
        o_ref[...] = _layer_norm(DEEPNORM_ALPHA * x_ref[...] + o_ref[...], lw_ref[...], lb_ref[...])


def _ffn_prompt(x3, w_up, w_down, layer, conv_w8, conv_b, lw, lb, *, tt, tf):
    b, t, _ = x3.shape
    nf = D_FFP // tf
    tt = min(tt, t)
    hb = tt // FFN_HALO
    half_a = lambda rows: pl.BlockSpec((rows, tf), lambda bi, ti, fi: (0, fi))
    half_g = lambda rows: pl.BlockSpec((rows, tf), lambda bi, ti, fi: (0, fi + nf))
    cols = _up_cols(tf)
    w_a = pl.BlockSpec((None, pl.Element(D_MODEL), pl.Element(tf)), lambda bi, ti, fi: (layer, 0, cols(fi)))
    w_g = pl.BlockSpec((None, pl.Element(D_MODEL), pl.Element(tf)), lambda bi, ti, fi: (layer, 0, cols(fi + nf)))
    vec = pl.BlockSpec((1, D_MODEL), lambda bi, ti, fi: (0, 0))
    tail = pl.BlockSpec((None, SUBLANES, tf), lambda bi, ti, fi: (bi, 0, fi))
    return pl.pallas_call(
        functools.partial(_ffn_kernel, nf=nf),
        grid=(b, t // tt, nf),
        in_specs=[pl.BlockSpec((None, tt, D_MODEL), lambda bi, ti, fi: (bi, ti, 0), pipeline_mode=pl.Buffered(1)),
                  pl.BlockSpec((None, FFN_HALO, D_MODEL), lambda bi, ti, fi: (bi, jnp.maximum(ti * hb - 1, 0), 0)),
                  w_a, w_g,
                  pl.BlockSpec((None, tf, D_MODEL), lambda bi, ti, fi: (layer, fi, 0)),
                  half_a(SUBLANES), half_g(SUBLANES), half_a(1), half_g(1), vec, vec],
        out_specs=[pl.BlockSpec((None, tt, D_MODEL), lambda bi, ti, fi: (bi, ti, 0)), tail, tail],
        out_shape=[jax.ShapeDtypeStruct((b, t, D_MODEL), F32),
                   jax.ShapeDtypeStruct((b, SUBLANES, D_FFP), F32),
                   jax.ShapeDtypeStruct((b, SUBLANES, D_FFP), F32)],
        scratch_shapes=[pltpu.VMEM((tt + FFN_HALO, D_MODEL), BF16)],
        compiler_params=_cparams("parallel", "arbitrary", "arbitrary"),
        name="ffn_prompt",
    )(x3, x3, w_up, w_up, w_down, conv_w8, conv_w8, conv_b, conv_b, lw, lb)


def _cumsum_rows(x):
    n = x.shape[0]
    rows = lax.broadcasted_iota(jnp.int32, (n, 1), 0)
    d = 1
    while d < n:
        x = x + jnp.where(rows >= d, pltpu.roll(x, d, 0), 0.0)
        d *= 2
    return x


def _log_sigmoid(x):
    return jnp.minimum(x, 0.0) - jnp.log1p(jnp.exp(-jnp.abs(x)))


def _mlstm_kernel(q_ref, k_ref, v_ref, og_ref, g_ref, gb_ref, nw_ref, c0_ref, n0_ref, m0_ref,
                  out_ref, c_ref, n_ref, m_ref, *, L, t_valid):
    ci = pl.program_id(1)
    lin = q_ref.shape[0]

    @pl.when(ci == 0)
    def _():
        c_ref[...] = c0_ref[...]
        n_ref[...] = n0_ref[...]
        m_ref[...] = m0_ref[...]

    def rows_of(ref):
        x = ref[...]
        if lin < L:
            x = jnp.concatenate([x, jnp.zeros((L - lin, x.shape[1]), x.dtype)], axis=0)
        return x

    rows = lax.broadcasted_iota(jnp.int32, (L, 1), 0)
    valid = (ci * L + rows) < t_valid
    pre = rows_of(g_ref) + gb_ref[...]
    lf = jnp.where(valid, _log_sigmoid(pre), 0.0)
    ig = jnp.where(valid, pre, NEG_INF)
    bcum = _cumsum_rows(lf)
    dt = (pltpu.roll(ig, GATE_MF - GATE_MI, 1) - bcum).T
    q_all, k_all, v_all, og_all = rows_of(q_ref), rows_of(k_ref), rows_of(v_ref), rows_of(og_ref)
    tri = lax.broadcasted_iota(jnp.int32, (L, L), 0) >= lax.broadcasted_iota(jnp.int32, (L, L), 1)
    nw = nw_ref[...]

    for h in range(MLSTM_HEADS):
        sl = slice(h * HEAD_DIM, (h + 1) * HEAD_DIM)
        q = q_all[:, sl]
        k = k_all[:, sl] * QK_SCALE
        v = v_all[:, sl]
        qb, kb, vb = q.astype(BF16), k.astype(BF16), v.astype(BF16)
        b_col = bcum[:, GATE_MF + h:GATE_MF + h + 1]
        ig_col = ig[:, GATE_MI + h:GATE_MI + h + 1]
        d_row = dt[GATE_MF + h:GATE_MF + h + 1, :]
        c_prev = c_ref[h]
        n_prev = n_ref[h:h + 1, :]
        m_prev = m_ref[h:h + 1, 0:1]

        dmat = jnp.where(tri, b_col + d_row, NEG_INF)
        g_col = b_col + m_prev
        m_row = jnp.maximum(jnp.max(dmat, axis=1, keepdims=True), g_col)
        a = jnp.exp(dmat - m_row) * _dot_nt(qb, kb)
        w_inter = jnp.exp(g_col - m_row)
        num = _dot(a.astype(BF16), vb) + w_inter * _dot(qb, c_prev.astype(BF16))
        den = jnp.sum(a, axis=1, keepdims=True) + w_inter * jnp.sum(q * n_prev, axis=1, keepdims=True)
        hid = num / jnp.maximum(jnp.abs(den), jnp.exp(-m_row))
        mu = jnp.mean(hid, axis=1, keepdims=True)
        dlt = hid - mu
        var = jnp.mean(dlt * dlt, axis=1, keepdims=True)
        hn = dlt * lax.rsqrt(var + LN_EPS) * nw[:, sl]
        res = jax.nn.sigmoid(og_all[:, sl]) * hn
        out_ref[:, sl] = res[0:lin, :]

        f_tot = b_col[L - 1:L, :]
        w_s = f_tot - b_col + ig_col
        m_new = jnp.maximum(f_tot + m_prev, jnp.max(w_s, axis=0, keepdims=True))
        ws = jnp.exp(w_s - m_new)
        decay = jnp.exp(f_tot + m_prev - m_new)
        c_ref[h] = decay * c_prev + _dot(k.T.astype(BF16), (ws * v).astype(BF16))
        n_ref[h:h + 1, :] = decay * n_prev + jnp.sum(ws * k, axis=0, keepdims=True)
        m_ref[h:h + 1, :] = jnp.broadcast_to(m_new, (1, LANES))


def _mlstm(z, gate_b, norm_w, c0, n0, m0, *, L, lin, t_valid):
    b, tz, _ = z.shape
    nchunks = tz // lin
    qblk = lambda col: pl.BlockSpec((None, lin, D_MLSTM), lambda bi, ci: (bi, ci, col // D_MLSTM))
    st4 = pl.BlockSpec((None, MLSTM_HEADS, HEAD_DIM, HEAD_DIM), lambda bi, ci: (bi, 0, 0, 0))
    st3 = pl.BlockSpec((None, MLSTM_HEADS, LANES), lambda bi, ci: (bi, 0, 0))
    return pl.pallas_call(
        functools.partial(_mlstm_kernel, L=L, t_valid=t_valid),
        grid=(b, nchunks),
        in_specs=[qblk(C_MQ), qblk(C_MK), qblk(C_MV), qblk(C_MO),
                  pl.BlockSpec((None, lin, LANES), lambda bi, ci: (bi, ci, C_GATE // LANES)),
                  pl.BlockSpec((1, LANES), lambda bi, ci: (0, 0)),
                  pl.BlockSpec((1, D_MLSTM), lambda bi, ci: (0, 0)),
                  st4, st3, st3],
        out_specs=[pl.BlockSpec((None, lin, D_MLSTM), lambda bi, ci: (bi, ci, 0)), st4, st3, st3],
        out_shape=[jax.ShapeDtypeStruct((b, tz, D_MLSTM), F32),
                   jax.ShapeDtypeStruct((b, MLSTM_HEADS, HEAD_DIM, HEAD_DIM), F32),
                   jax.ShapeDtypeStruct((b, MLSTM_HEADS, LANES), F32),
                   jax.ShapeDtypeStruct((b, MLSTM_HEADS, LANES), F32)],
        compiler_params=_cparams("parallel", "arbitrary"),
        name="mlstm",
    )(z, z, z, z, z, gate_b, norm_w, c0, n0, m0)


S5_IN_TILES = S5_CH // 256
S5_IN_FEATS = D_S5 // S5_IN_TILES
S5_OUT_TILES = D_S5 // 256
S5_OUT_CH = S5_CH // S5_OUT_TILES


S5_LT = S5_CH // LANES


def _lane_tiles(x):
    return jnp.stack([x[:, k * LANES:(k + 1) * LANES] for k in range(x.shape[1] // LANES)])


def _lane_untile(x):
    return jnp.concatenate([x[k] for k in range(x.shape[0])], axis=1)


def _s5_kernel(u_ref, wb_ref, wc_ref, lr_ref, li_ref, d_ref, gw_ref, gb_ref, x0r_ref, x0i_ref,
               o_ref, xr_ref, xi_ref, sr_ref, si_ref, y_ref, *, steps):
    nb, lc, _ = u_ref.shape

    @pl.when(pl.program_id(0) == 0)
    def _():
        xr_ref[...] = x0r_ref[...]
        xi_ref[...] = x0i_ref[...]

    for b in range(nb):
        ub = u_ref[b].astype(BF16)
        for c in range(S5_IN_TILES):
            bu = _dot(ub[:, c * S5_IN_FEATS:(c + 1) * S5_IN_FEATS], wb_ref[c])
            for k in range(256 // LANES):
                rows = pl.ds(b, lc, stride=nb)
                sr_ref.at[c * (256 // LANES) + k][rows, :] = bu[:, k * LANES:(k + 1) * LANES]
                si_ref.at[c * (256 // LANES) + k][rows, :] = bu[:, 256 + k * LANES:256 + (k + 1) * LANES]
    lam_r = _lane_tiles(jnp.broadcast_to(lr_ref[...], (SUBLANES, S5_CH)))
    lam_i = _lane_tiles(jnp.broadcast_to(li_ref[...], (SUBLANES, S5_CH)))
    row8 = lax.broadcasted_iota(jnp.int32, (1, SUBLANES, 1), 1)
    per_group = SUBLANES // nb

    def group(i, carry):
        xr, xi = carry
        r0 = pl.multiple_of(i * SUBLANES, SUBLANES)
        br = sr_ref[:, pl.ds(r0, SUBLANES), :]
        bi = si_ref[:, pl.ds(r0, SUBLANES), :]
        out_r, out_i = br, bi
        for k in range(per_group):
            nr = lam_r * xr - lam_i * xi + br
            ni = lam_r * xi + lam_i * xr + bi
            here = (row8 >= k * nb) & (row8 < (k + 1) * nb)
            out_r = jnp.where(here, nr, out_r)
            out_i = jnp.where(here, ni, out_i)
            if per_group > 1:
                xr = pltpu.roll(nr, nb, 1)
                xi = pltpu.roll(ni, nb, 1)
            else:
                xr, xi = nr, ni
        sr_ref[:, pl.ds(r0, SUBLANES), :] = out_r
        si_ref[:, pl.ds(r0, SUBLANES), :] = out_i
        return xr, xi

    xr, xi = lax.fori_loop(0, steps * nb // SUBLANES, group, (_lane_tiles(xr_ref[...]), _lane_tiles(xi_ref[...])),
                           unroll=2)
    xr_ref[...] = _lane_untile(xr)
    xi_ref[...] = _lane_untile(xi)
    per_out = S5_OUT_CH // LANES
    for j in range(S5_OUT_TILES):
        lhs_r = jnp.concatenate([sr_ref[j * per_out + k] for k in range(per_out)], axis=1).astype(BF16)
        lhs_i = jnp.concatenate([si_ref[j * per_out + k] for k in range(per_out)], axis=1).astype(BF16)
        y = _dot(lhs_r, wc_ref[0, j]) + _dot(lhs_i, wc_ref[1, j])
        for k in range(256 // LANES):
            y_ref[j * (256 // LANES) + k] = y[:, k * LANES:(k + 1) * LANES]
    for b in range(nb):
        rows = pl.ds(b, lc, stride=nb)
        y = jnp.concatenate([y_ref.at[k][rows, :] for k in range(D_S5 // LANES)], axis=1) + d_ref[...] * u_ref[b]
        zz = jax.nn.gelu(y)
        o_ref[b] = zz * jax.nn.sigmoid(_dot(zz.astype(BF16), gw_ref[...]) + gb_ref[...])


def _s5(z3, wb, wc, lam_r, lam_i, d, glu_w, glu_b, x0r, x0i, *, lc, steps):
    b, tz, _ = z3.shape
    lc = min(lc, tz)
    steps = min(steps, lc)
    assert steps == lc or tz == lc
    full = lambda a: pl.BlockSpec(a.shape, lambda i: (0,) * a.ndim)
    st = pl.BlockSpec((SUBLANES, S5_CH), lambda i: (0, 0))
    return pl.pallas_call(
        functools.partial(_s5_kernel, steps=steps),
        grid=(tz // lc,),
        in_specs=[pl.BlockSpec((b, lc, D_S5), lambda i: (0, i, C_SU // D_S5)),
                  full(wb), full(wc), full(lam_r), full(lam_i), full(d), full(glu_w), full(glu_b), st, st],
        out_specs=[pl.BlockSpec((b, lc, D_S5), lambda i: (0, i, 0)), st, st],
        out_shape=[jax.ShapeDtypeStruct((b, tz, D_S5), F32),
                   jax.ShapeDtypeStruct((SUBLANES, S5_CH), F32),
                   jax.ShapeDtypeStruct((SUBLANES, S5_CH), F32)],
        scratch_shapes=[pltpu.VMEM((S5_LT, lc * b, LANES), F32), pltpu.VMEM((S5_LT, lc * b, LANES), F32),
                        pltpu.VMEM((D_S5 // LANES, lc * b, LANES), F32)],
        compiler_params=_cparams("arbitrary"),
        name="s5",
    )(z3, wb, wc, lam_r, lam_i, d, glu_w, glu_b, x0r, x0i)


N_CMB = 2 * KV_HEADS


def _cmp_project(rows_of, w_ref, o_ref):
    nch = o_ref.shape[0]
    for cmb in range(N_CMB):
        slot = cmb // KV_HEADS
        rows_ref = rows_of(cmb)
        acc = jnp.zeros((nch, 2 * HEAD_DIM), F32)
        for l in range(CMP_STRIDE):
            x = rows_ref[pl.ds(l, nch, stride=CMP_STRIDE), :]
            acc += _dot(x.astype(BF16), w_ref[slot, l])
        o_ref[:, cmb * 2 * HEAD_DIM:(cmb + 1) * 2 * HEAD_DIM] = acc


def _cmp_p_kernel(r0_ref, r1_ref, r2_ref, r3_ref, w_ref, o_ref):
    rows = (r0_ref, r1_ref, r2_ref, r3_ref)
    _cmp_project(lambda cmb: rows[cmb], w_ref, o_ref)


def _cmp_project_prompt(z, wcat):
    b, t, _ = z.shape
    nch = t // CMP_STRIDE
    rows = lambda cmb: pl.BlockSpec((None, t, HEAD_DIM), lambda bi: (bi, 0, C_NKV // HEAD_DIM + cmb))
    return pl.pallas_call(
        _cmp_p_kernel,
        grid=(b,),
        in_specs=[rows(cmb) for cmb in range(N_CMB)] + [pl.BlockSpec(wcat.shape, lambda bi: (0, 0, 0, 0))],
        out_specs=pl.BlockSpec((None, nch, N_CMB * 2 * HEAD_DIM), lambda bi: (bi, 0, 0)),
        out_shape=jax.ShapeDtypeStruct((b, nch, N_CMB * 2 * HEAD_DIM), F32),
        compiler_params=_cparams("parallel"),
        name="cmp_project_prompt",
    )(z, z, z, z, wcat)


def _cmp_s_kernel(pt_ref, *refs):
    del pt_ref
    per_pos = 2 * KV_HEADS
    page_refs = [r.reshape(PAGE_SIZE * per_pos, HEAD_DIM) for r in refs[:CMP_PAGES]]
    w_ref, o_ref = refs[CMP_PAGES:]
    per_page = PAGE_SIZE // CMP_STRIDE
    for cmb in range(N_CMB):
        acc = jnp.zeros((o_ref.shape[0], 2 * HEAD_DIM), F32)
        for l in range(CMP_STRIDE):
            x = jnp.concatenate(
                [page_refs[p][pl.ds(l * per_pos + cmb, per_page, stride=CMP_STRIDE * per_pos), :]
                 for p in range(CMP_PAGES)], axis=0)
            acc += _dot(x.astype(BF16), w_ref[cmb // KV_HEADS, l])
        o_ref[:, cmb * 2 * HEAD_DIM:(cmb + 1) * 2 * HEAD_DIM] = acc


def _page_spec(layer, slot, p, pages_per_step, grid_rank):
    def index_map(*idx):
        bi, si, pt = idx[0], idx[grid_rank - 1], idx[grid_rank]
        return (layer, pt[bi, si * pages_per_step + p], 0, slot, 0, 0)

    return pl.BlockSpec((None, None, PAGE_SIZE, None, KV_HEADS, HEAD_DIM), index_map)


def _cmp_project_sample(cache6, page_table, layer, wcat):
    b, npages = page_table.shape
    steps = npages // CMP_PAGES
    nch = CMP_PAGES * PAGE_SIZE // CMP_STRIDE
    def page_pair(p):
        return pl.BlockSpec((None, None, PAGE_SIZE, 2, KV_HEADS, HEAD_DIM),
                            lambda bi, si, pt: (layer, pt[bi, si * CMP_PAGES + p], 0, 0, 0, 0))

    pages = [page_pair(p) for p in range(CMP_PAGES)]

    grid_spec = pltpu.PrefetchScalarGridSpec(
        num_scalar_prefetch=1,
        grid=(b, steps),
        in_specs=pages + [pl.BlockSpec(wcat.shape, lambda bi, si, pt: (0, 0, 0, 0))],
        out_specs=pl.BlockSpec((None, nch, N_CMB * 2 * HEAD_DIM), lambda bi, si, pt: (bi, si, 0)),
    )
    return pl.pallas_call(
        _cmp_s_kernel,
        grid_spec=grid_spec,
        out_shape=jax.ShapeDtypeStruct((b, steps * nch, N_CMB * 2 * HEAD_DIM), F32),
        compiler_params=_cparams("parallel", "arbitrary"),
        name="cmp_project_sample",
    )(page_table, *([cache6] * CMP_PAGES), wcat)


def _cmp_fin_kernel(p_ref, b1_ref, w2_ref, o_ref):
    nch = p_ref.shape[0]
    for cmb in range(N_CMB):
        slot = cmb // KV_HEADS
        c0 = cmb * 2 * HEAD_DIM
        first = p_ref[:, c0:c0 + HEAD_DIM]
        second = pltpu.roll(p_ref[:, c0 + HEAD_DIM:c0 + 2 * HEAD_DIM], nch - 1, 0)
        hid = b1_ref[slot:slot + 1, :] + first + second
        o_ref[cmb] = _dot(jax.nn.gelu(hid).astype(BF16), w2_ref[slot])


def _cmp_finish(p, b1, w2):
    b, nch, _ = p.shape
    return pl.pallas_call(
        _cmp_fin_kernel,
        grid=(b,),
        in_specs=[pl.BlockSpec((None, nch, p.shape[2]), lambda bi: (bi, 0, 0)),
                  pl.BlockSpec(b1.shape, lambda bi: (0, 0)),
                  pl.BlockSpec(w2.shape, lambda bi: (0, 0, 0))],
        out_specs=pl.BlockSpec((None, N_CMB, nch, HEAD_DIM), lambda bi: (bi, 0, 0, 0)),
        out_shape=jax.ShapeDtypeStruct((b, N_CMB, nch, HEAD_DIM), F32),
        compiler_params=_cparams("parallel"),
        name="cmp_finish",
    )(p, b1, w2)


def _softmax_init(m_scr, l_scr, acc_scr):
    m_scr[...] = jnp.full(m_scr.shape, M_INIT, F32)
    l_scr[...] = jnp.zeros(l_scr.shape, F32)
    acc_scr[...] = jnp.zeros(acc_scr.shape, F32)


def _softmax_update(s, mask, vb, m_scr, l_scr, acc_scr):
    s = jnp.where(mask, s, NEG_INF)
    m_prev = m_scr[...]
    m_new = jnp.maximum(m_prev, jnp.max(s, axis=1, keepdims=True))
    alpha = jnp.exp(m_prev - m_new)
    p = jnp.exp(s - m_new)
    l_scr[...] = alpha * l_scr[...] + jnp.sum(p, axis=1, keepdims=True)
    acc_scr[...] = alpha * acc_scr[...] + _dot(p.astype(BF16), vb)
    m_scr[...] = m_new


def _softmax_result(l_scr, acc_scr):
    l = l_scr[...]
    return acc_scr[...] / jnp.where(l > 0, l, 1.0)


def _masked_probs(s, mask):
    s = jnp.where(mask, s, NEG_INF)
    m = jnp.max(s, axis=1, keepdims=True)
    m = jnp.where(m > NEG_INF, m, 0.0)
    p = jnp.exp(s - m)
    den = jnp.sum(p, axis=1, keepdims=True)
    return p / jnp.where(den > 0, den, 1.0)


def _stack_heads(q):
    return jnp.concatenate([q[:, j * HEAD_DIM:(j + 1) * HEAD_DIM] for j in range(GROUP)], axis=0)


def _fold_lanes(x, op):
    out = x[:, 0:LANES]
    for k in range(1, x.shape[1] // LANES):
        out = op(out, x[:, k * LANES:(k + 1) * LANES])
    return out


def _block_importance(pc, n_cmp, n_sel):
    ncp = pc.shape[1]
    nsp = -(-n_sel // LANES) * LANES
    c_start = lax.broadcasted_iota(jnp.int32, (ncp, nsp), 0) * CMP_STRIDE
    s_start = lax.broadcasted_iota(jnp.int32, (ncp, nsp), 1) * SEL_BLOCK
    overlap = ((c_start < s_start + SEL_BLOCK) & (c_start + CMP_BLOCK > s_start)
               & (c_start < n_cmp * CMP_STRIDE)).astype(F32)
    return jnp.dot(pc, overlap, preferred_element_type=F32, precision=lax.Precision.HIGHEST)


def _force_blocks(imp, blk, cur):
    forced = (blk == 0) | (blk == cur) | (blk == cur - 1)
    imp = jnp.where(forced, FORCE_SCORE, imp)
    return jnp.where(blk > cur, -1.0, imp)


def _select_blocks_cols(pc, cur, n_cmp, n_sel):
    imp = _block_importance(pc, n_cmp, n_sel)
    blk = lax.broadcasted_iota(jnp.int32, (1, imp.shape[1]), 1)
    imp = _force_blocks(imp, blk, cur)

    def body(sp, count):
        col = jnp.sum(jnp.where(blk == sp, imp, 0.0), axis=1, keepdims=True)
        ahead = (col > imp) | ((col == imp) & (sp < blk))
        return count + ahead.astype(F32)

    count = lax.fori_loop(0, n_sel, body, jnp.zeros(imp.shape, F32), unroll=True)
    return ((count < min(SEL_TOPK, n_sel)) & (blk < n_sel)).astype(F32)


def _select_blocks_rows(pc, cur_row, n_cmp, n_sel):
    rows = pc.shape[0]
    nrow = -(-n_sel // SUBLANES) * SUBLANES
    imp_t = _block_importance(pc, n_cmp, n_sel).T[0:nrow, :]
    blk = lax.broadcasted_iota(jnp.int32, (nrow, 1), 0)
    imp_t = _force_blocks(imp_t, blk, cur_row)
    count = jnp.zeros(imp_t.shape, F32)
    for sp in range(n_sel):
        row = imp_t[sp:sp + 1, :]
        count += ((row > imp_t) | ((row == imp_t) & (sp < blk))).astype(F32)
    sel_t = ((count < min(SEL_TOPK, n_sel)) & (blk < n_sel)).astype(F32)
    sel_t = jnp.concatenate([sel_t, jnp.zeros((LANES - nrow, rows), F32)], axis=0)
    return sel_t.T


def _gate(gt, col):
    lane = lax.broadcasted_iota(jnp.int32, (1, LANES), 1)
    return jax.nn.sigmoid(jnp.sum(jnp.where(lane == col, gt, 0.0), axis=1, keepdims=True))


def _nsa_prompt_kernel(q_ref, ks_ref, vs_ref, kw_ref, vw_ref, kc_ref, vc_ref, bc_ref, tz_ref, gt_ref, gb_ref,
                       o_ref, s_scr, selb_scr, mx_scr, l_scr, acc_scr, *, t_len):
    g = pl.program_id(1)
    qt = pl.program_id(2)
    q0 = qt * TQ
    n_cmp = t_len // CMP_STRIDE - (CMP_BLOCK // CMP_STRIDE) + 1
    n_sel = t_len // SEL_BLOCK
    ncp = kc_ref.shape[0]
    rows4 = GROUP * TQ
    qs = _stack_heads(q_ref[...] * QK_SCALE).astype(BF16)
    qpos = q0 + (lax.broadcasted_iota(jnp.int32, (rows4, 1), 0) & (TQ - 1))

    n_idx = lax.broadcasted_iota(jnp.int32, (1, ncp), 1)
    s_c = _dot_nt(qs, kc_ref[...].astype(BF16)) + bc_ref[...].reshape(rows4, ncp)
    cmask = (qpos - (n_idx * CMP_STRIDE + CMP_BLOCK - 1) >= 0) & (n_idx < n_cmp)
    p_c = _masked_probs(s_c, cmask)
    o_c = _dot(p_c.astype(BF16), vc_ref[...].astype(BF16))
    pc = p_c[0:TQ] + p_c[TQ:2 * TQ] + p_c[2 * TQ:3 * TQ] + p_c[3 * TQ:4 * TQ]
    cur_row = jnp.right_shift(q0 + lax.broadcasted_iota(jnp.int32, (1, TQ), 1), SEL_SHIFT)
    sel = _select_blocks_rows(pc, cur_row, n_cmp, n_sel)
    sel_add = ((sel - 1.0) * -MASKED).astype(BF16)

    kt_hi = (q0 + TQ - 1) // TK + 1
    kt_far = jnp.maximum(q0 - (FAR_DIST - 1), 0) // TK
    kidx = lax.broadcasted_iota(jnp.int32, (1, TK), 1)
    srow = lax.broadcasted_iota(jnp.int32, (LANES, TK), 0)

    def spread_selection(kt, carry):
        k0 = kt * TK
        expand = (jnp.right_shift(k0 + lax.broadcasted_iota(jnp.int32, (LANES, TK), 1), SEL_SHIFT) == srow).astype(BF16)
        selb_scr[kt] = _dot(sel_add, expand)
        return carry

    lax.fori_loop(0, kt_hi, spread_selection, 0)

    def tile_bias(k0):
        return tz_ref[jnp.clip((q0 - k0) // TQ, 0, N_TZ - 1)].reshape(rows4, TK)

    def attend(k_ref, v_ref, kt_lo, mask_tile):
        mx_scr[...] = jnp.full(mx_scr.shape, NEG_INF, F32)

        def scores(near):
            def body(kt, carry):
                k0 = pl.multiple_of(kt * TK, TK)
                s = _dot_nt(qs, k_ref[pl.ds(k0, TK), :].astype(BF16))
                if near:
                    s = s + tile_bias(k0)
                s = mask_tile(k0, s, near)
                s_scr[kt] = s
                mx_scr[...] = jnp.maximum(mx_scr[...], _fold_lanes(s, jnp.maximum))
                return carry
            return body

        kt_mid = jnp.maximum(kt_far, kt_lo)
        lax.fori_loop(kt_lo, kt_mid, scores(False), 0)
        lax.fori_loop(kt_mid, kt_hi, scores(True), 0)
        m = jnp.max(mx_scr[...], axis=1, keepdims=True)
        mx_scr[...] = jnp.broadcast_to(jnp.where(m > NEG_INF, m, 0.0), mx_scr.shape)
        l_scr[...] = jnp.zeros(l_scr.shape, F32)
        acc_scr[...] = jnp.zeros(acc_scr.shape, F32)

        def probs(kt, carry):
            k0 = pl.multiple_of(kt * TK, TK)
            m_rep = mx_scr[...]
            p = jnp.exp(s_scr[kt] - jnp.concatenate([m_rep] * (TK // LANES), axis=1))
            l_scr[...] += _fold_lanes(p, jnp.add)
            acc_scr[...] += _dot(p.astype(BF16), v_ref[pl.ds(k0, TK), :].astype(BF16))
            return carry

        lax.fori_loop(kt_lo, kt_hi, probs, 0)
        l = jnp.sum(l_scr[...], axis=1, keepdims=True)
        return acc_scr[...] / jnp.where(l > 0, l, 1.0)

    def sel_mask(k0, s, near):
        s = (s.reshape(GROUP, TQ, TK) + selb_scr[k0 // TK][None]).reshape(rows4, TK)
        return jnp.where(k0 + kidx <= qpos, s, NEG_INF) if near else s

    o_s = attend(ks_ref, vs_ref, 0, sel_mask)

    def win_mask(k0, s, near):
        dist = qpos - (k0 + kidx)
        return jnp.where((dist >= 0) & (dist <= WINDOW), s, NEG_INF)

    o_w = attend(kw_ref, vw_ref, jnp.maximum(q0 - WINDOW, 0) // TK, win_mask)

    gt = gt_ref[...] + gb_ref[...]
    for j in range(GROUP):
        col = GATE_NG + (g * GROUP + j) * 3
        r = slice(j * TQ, (j + 1) * TQ)
        o_ref[:, j * HEAD_DIM:(j + 1) * HEAD_DIM] = (
            _gate(gt, col) * o_c[r] + _gate(gt, col + 1) * o_s[r] + _gate(gt, col + 2) * o_w[r])


def _nsa_prompt(z, kcvc, bias_c, tz, gate_b):
    b, t, _ = z.shape
    ncp = kcvc.shape[2]
    nq = t // TQ
    kv = lambda slot: pl.BlockSpec((None, t, HEAD_DIM), lambda bi, g, qi: (bi, 0, C_NKV // HEAD_DIM + slot * KV_HEADS + g))
    cmp_blk = lambda slot: pl.BlockSpec((None, None, ncp, HEAD_DIM), lambda bi, g, qi: (bi, slot * KV_HEADS + g, 0, 0))
    return pl.pallas_call(
        functools.partial(_nsa_prompt_kernel, t_len=t),
        grid=(b, KV_HEADS, nq),
        in_specs=[pl.BlockSpec((None, TQ, GROUP * HEAD_DIM), lambda bi, g, qi: (bi, qi, C_NQ // (GROUP * HEAD_DIM) + g)),
                  kv(2), kv(3), kv(4), kv(5), cmp_blk(0), cmp_blk(1),
                  pl.BlockSpec((GROUP, TQ, ncp), lambda bi, g, qi: (g, qi, 0)),
                  pl.BlockSpec((N_TZ, GROUP, TQ, TK), lambda bi, g, qi: (0, g, 0, 0)),
                  pl.BlockSpec((None, TQ, LANES), lambda bi, g, qi: (bi, qi, C_GATE // LANES)),
                  pl.BlockSpec((1, LANES), lambda bi, g, qi: (0, 0))],
        out_specs=pl.BlockSpec((None, TQ, GROUP * HEAD_DIM), lambda bi, g, qi: (bi, qi, g)),
        out_shape=jax.ShapeDtypeStruct((b, t, D_NSA), F32),
        scratch_shapes=[pltpu.VMEM((t // TK, GROUP * TQ, TK), F32), pltpu.VMEM((t // TK, TQ, TK), F32),
                        pltpu.VMEM((GROUP * TQ, LANES), F32),
                        pltpu.VMEM((GROUP * TQ, LANES), F32), pltpu.VMEM((GROUP * TQ, HEAD_DIM), F32)],
        compiler_params=_cparams("parallel", "parallel", "arbitrary"),
        name="nsa_prompt",
    )(z, z, z, z, z, kcvc, kcvc, bias_c, tz, z, gate_b)


TS = SUBLANES
ROWS_S = GROUP * TS


def _nsa_s_cmp_kernel(q_ref, kc_ref, vc_ref, bc_ref, oc_ref, sel_ref, *, pos0, n_cmp, n_sel):
    nch = kc_ref.shape[0]
    qs = _stack_heads(q_ref[...] * QK_SCALE).astype(BF16)
    qpos = pos0 + (lax.broadcasted_iota(jnp.int32, (ROWS_S, 1), 0) & (TS - 1))
    n_idx = lax.broadcasted_iota(jnp.int32, (1, nch), 1)
    s_c = _dot_nt(qs, kc_ref[...].astype(BF16)) + bc_ref[...].reshape(ROWS_S, nch)
    cmask = (qpos - (n_idx * CMP_STRIDE + CMP_BLOCK - 1) >= 0) & (n_idx < n_cmp)
    p_c = _masked_probs(s_c, cmask)
    oc_ref[...] = _dot(p_c.astype(BF16), vc_ref[...].astype(BF16))
    pc = p_c[0:TS] + p_c[TS:2 * TS] + p_c[2 * TS:3 * TS] + p_c[3 * TS:4 * TS]
    sel_ref[...] = _select_blocks_cols(pc, jnp.right_shift(qpos[0:TS], SEL_SHIFT), n_cmp, n_sel)


def _nsa_sample_cmp(z8, kcvc, bias_c, *, pos0, n_cmp, n_sel):
    b = z8.shape[0]
    nch = kcvc.shape[2]
    nsp = -(-n_sel // LANES) * LANES
    cmp_blk = lambda slot: pl.BlockSpec((None, None, nch, HEAD_DIM), lambda bi, g: (bi, slot * KV_HEADS + g, 0, 0))
    return pl.pallas_call(
        functools.partial(_nsa_s_cmp_kernel, pos0=pos0, n_cmp=n_cmp, n_sel=n_sel),
        grid=(b, KV_HEADS),
        in_specs=[pl.BlockSpec((None, TS, GROUP * HEAD_DIM), lambda bi, g: (bi, 0, C_NQ // (GROUP * HEAD_DIM) + g)),
                  cmp_blk(0), cmp_blk(1),
                  pl.BlockSpec((GROUP, TS, nch), lambda bi, g: (g, 0, 0))],
        out_specs=[pl.BlockSpec((None, None, ROWS_S, HEAD_DIM), lambda bi, g: (bi, g, 0, 0)),
                   pl.BlockSpec((None, None, TS, nsp), lambda bi, g: (bi, g, 0, 0))],
        out_shape=[jax.ShapeDtypeStruct((b, KV_HEADS, ROWS_S, HEAD_DIM), F32),
                   jax.ShapeDtypeStruct((b, KV_HEADS, TS, nsp), F32)],
        compiler_params=_cparams("parallel", "parallel"),
        name="nsa_sample_cmp",
    )(z8, kcvc, kcvc, bias_c)


def _pad_rows(x, n):
    return jnp.concatenate([x, jnp.zeros((n - x.shape[0], x.shape[1]), x.dtype)], axis=0)


def _nsa_s_sel_kernel(pt_ref, q_ref, sel_ref, kn_ref, vn_ref, bt_ref, bn_ref, bf_ref, *refs, pos0, npages):
    del pt_ref
    rows_pp = PAGE_SIZE * KV_HEADS
    k_refs = [r.reshape(rows_pp, HEAD_DIM) for r in refs[:SEL_PAGES]]
    v_refs = [r.reshape(rows_pp, HEAD_DIM) for r in refs[SEL_PAGES:2 * SEL_PAGES]]
    o_ref, kbuf, vbuf, m_scr, l_scr, acc_scr = refs[2 * SEL_PAGES:]
    step = pl.program_id(1)
    last = step == npages // SEL_PAGES - 1
    rows = KV_HEADS * ROWS_S
    ncol = SEL_TILE * rows_pp
    head_shift = KV_HEADS.bit_length() - 1
    blk_per_page = PAGE_SIZE // SEL_BLOCK

    @pl.when(step == 0)
    def _():
        _softmax_init(m_scr, l_scr, acc_scr)

    nsp = sel_ref.shape[-1]
    ridx = lax.broadcasted_iota(jnp.int32, (rows, 1), 0)
    qpos = pos0 + (ridx & (TS - 1))
    qs = jnp.concatenate([_stack_heads(q_ref[:, g * GROUP * HEAD_DIM:(g + 1) * GROUP * HEAD_DIM] * QK_SCALE)
                          for g in range(KV_HEADS)], axis=0).astype(BF16)
    sel_rows = jnp.concatenate([sel_ref[g] for g in range(KV_HEADS) for _ in range(GROUP)], axis=0)
    bias_far = bf_ref[...].reshape(rows, PAGE_SIZE)[:, 0:1]
    col = lax.broadcasted_iota(jnp.int32, (1, ncol), 1)
    own_head = (col & (KV_HEADS - 1)) == ridx // ROWS_S
    expand = (jnp.right_shift(lax.broadcasted_iota(jnp.int32, (LANES, ncol), 1), SEL_SHIFT + head_shift)
              == lax.broadcasted_iota(jnp.int32, (LANES, ncol), 0)).astype(BF16)
    blk_r = lax.broadcasted_iota(jnp.int32, (nsp, LANES), 0)
    blk_c = lax.broadcasted_iota(jnp.int32, (nsp, LANES), 1)
    n_tiles = SEL_PAGES // SEL_TILE

    for tile in range(n_tiles):
        for i in range(SEL_TILE):
            p = tile * SEL_TILE + i
            kbuf[i * rows_pp:(i + 1) * rows_pp, :] = k_refs[p][...].astype(BF16)
            vbuf[i * rows_pp:(i + 1) * rows_pp, :] = v_refs[p][...].astype(BF16)
        page0 = step * SEL_PAGES + tile * SEL_TILE
        window = ((blk_r == page0 * blk_per_page + blk_c) & (blk_c < SEL_TILE * blk_per_page)).astype(BF16)
        chosen = _dot(_dot(sel_rows.astype(BF16), window).astype(BF16), expand) > 0.5
        s = _dot_nt(qs, kbuf[...]) + bias_far
        if tile == n_tiles - 1:
            tail = jnp.where(last, bt_ref[...].reshape(rows, rows_pp) - bias_far, 0.0)
            s = s + jnp.concatenate([jnp.zeros((rows, ncol - rows_pp), F32), tail], axis=1)
        key = page0 * PAGE_SIZE + jnp.right_shift(col, head_shift)
        _softmax_update(s, chosen & own_head & (key <= qpos), vbuf[...], m_scr, l_scr, acc_scr)

    @pl.when(last)
    def _():
        kidx = lax.broadcasted_iota(jnp.int32, (1, PAGE_SIZE), 1)
        new_blk = lax.broadcasted_iota(jnp.int32, (1, nsp), 1) == npages * blk_per_page
        flag = jnp.sum(jnp.where(new_blk, sel_rows, 0.0), axis=1, keepdims=True) > 0.5
        bias_new = bn_ref[...].reshape(rows, PAGE_SIZE)
        for g in range(KV_HEADS):
            r = slice(g * ROWS_S, (g + 1) * ROWS_S)
            kn = _pad_rows(kn_ref[:, g * HEAD_DIM:(g + 1) * HEAD_DIM], PAGE_SIZE).astype(BF16)
            vn = _pad_rows(vn_ref[:, g * HEAD_DIM:(g + 1) * HEAD_DIM], PAGE_SIZE).astype(BF16)
            mask = flag[r] & (pos0 + kidx <= qpos[r]) & (kidx < TS)
            _softmax_update(_dot_nt(qs[r], kn) + bias_new[r], mask, vn, m_scr.at[r], l_scr.at[r], acc_scr.at[r])
            o_ref[g] = _softmax_result(l_scr.at[r], acc_scr.at[r])


def _nsa_sample_sel(z8, sel, bias_tail, bias_new, bias_far, cache6, page_table, layer, *, pos0):
    b, npages = page_table.shape
    nsp = sel.shape[-1]
    steps = npages // SEL_PAGES
    slot_cols = KV_HEADS * HEAD_DIM
    new_rows = lambda slot: pl.BlockSpec((None, TS, slot_cols),
                                         lambda bi, si, pt: (bi, 0, (C_NKV + slot * slot_cols) // slot_cols))
    whole = lambda a: pl.BlockSpec(a.shape, lambda bi, si, pt: (0,) * a.ndim)
    grid_spec = pltpu.PrefetchScalarGridSpec(
        num_scalar_prefetch=1,
        grid=(b, steps),
        in_specs=[pl.BlockSpec((None, TS, D_NSA), lambda bi, si, pt: (bi, 0, C_NQ // D_NSA)),
                  pl.BlockSpec((None, KV_HEADS, TS, nsp), lambda bi, si, pt: (bi, 0, 0, 0)),
                  new_rows(2), new_rows(3), whole(bias_tail), whole(bias_new), whole(bias_far)]
                 + [_page_spec(layer, 2, p, SEL_PAGES, 2) for p in range(SEL_PAGES)]
                 + [_page_spec(layer, 3, p, SEL_PAGES, 2) for p in range(SEL_PAGES)],
        out_specs=pl.BlockSpec((None, KV_HEADS, ROWS_S, HEAD_DIM), lambda bi, si, pt: (bi, 0, 0, 0)),
        scratch_shapes=[pltpu.VMEM((SEL_TILE * PAGE_SIZE * KV_HEADS, HEAD_DIM), BF16),
                        pltpu.VMEM((SEL_TILE * PAGE_SIZE * KV_HEADS, HEAD_DIM), BF16),
                        pltpu.VMEM((KV_HEADS * ROWS_S, 1), F32), pltpu.VMEM((KV_HEADS * ROWS_S, 1), F32),
                        pltpu.VMEM((KV_HEADS * ROWS_S, HEAD_DIM), F32)],
    )
    return pl.pallas_call(
        functools.partial(_nsa_s_sel_kernel, pos0=pos0, npages=npages),
        grid_spec=grid_spec,
        out_shape=jax.ShapeDtypeStruct((b, KV_HEADS, ROWS_S, HEAD_DIM), F32),
        compiler_params=_cparams("parallel", "arbitrary"),
        name="nsa_sample_sel",
    )(page_table, z8, sel, z8, z8, bias_tail, bias_new, bias_far, *([cache6] * (2 * SEL_PAGES)))


def _nsa_s_win_kernel(q_ref, kw_ref, vw_ref, kn_ref, vn_ref, bw_ref, oc_ref, os_ref, gt_ref, gb_ref, o_ref,
                      m_scr, l_scr, acc_scr, *, pos0):
    g = pl.program_id(1)
    wb = kw_ref.shape[0]
    own = pl.ds(g, wb, stride=KV_HEADS)
    kw = kw_ref.reshape(wb * KV_HEADS, HEAD_DIM)[own, :]
    vw = vw_ref.reshape(wb * KV_HEADS, HEAD_DIM)[own, :]
    qs = _stack_heads(q_ref[...] * QK_SCALE).astype(BF16)
    qpos = pos0 + (lax.broadcasted_iota(jnp.int32, (ROWS_S, 1), 0) & (TS - 1))
    bias = bw_ref[...].reshape(ROWS_S, wb + PAGE_SIZE)
    _softmax_init(m_scr, l_scr, acc_scr)
    dist = qpos - (pos0 - wb + lax.broadcasted_iota(jnp.int32, (1, wb), 1))
    s = _dot_nt(qs, kw.astype(BF16)) + bias[:, 0:wb]
    _softmax_update(s, (dist >= 0) & (dist <= WINDOW), vw.astype(BF16), m_scr, l_scr, acc_scr)
    kidx = lax.broadcasted_iota(jnp.int32, (1, PAGE_SIZE), 1)
    dist = qpos - (pos0 + kidx)
    s = _dot_nt(qs, _pad_rows(kn_ref[...], PAGE_SIZE).astype(BF16)) + bias[:, wb:wb + PAGE_SIZE]
    _softmax_update(s, (dist >= 0) & (dist <= WINDOW) & (kidx < TS), _pad_rows(vn_ref[...], PAGE_SIZE).astype(BF16),
                    m_scr, l_scr, acc_scr)
    o_w = _softmax_result(l_scr, acc_scr)
    o_c = oc_ref[...]
    o_s = os_ref[...]
    gt = gt_ref[...] + gb_ref[...]
    for j in range(GROUP):
        col = GATE_NG + (g * GROUP + j) * 3
        r = slice(j * TS, (j + 1) * TS)
        o_ref[:, j * HEAD_DIM:(j + 1) * HEAD_DIM] = (
            _gate(gt, col) * o_c[r] + _gate(gt, col + 1) * o_s[r] + _gate(gt, col + 2) * o_w[r])


def _nsa_sample_win(z8, win6, layer, bias_w, o_c, o_s, gate_b, *, pos0):
    _, b, wb = win6.shape[:3]
    cached = lambda slot: pl.BlockSpec((None, None, wb, None, KV_HEADS, HEAD_DIM),
                                       lambda bi, g: (layer, bi, 0, slot, 0, 0))
    zcol = lambda col: (lambda bi, g: (bi, 0, col // HEAD_DIM + g))
    part = pl.BlockSpec((None, None, ROWS_S, HEAD_DIM), lambda bi, g: (bi, g, 0, 0))
    return pl.pallas_call(
        functools.partial(_nsa_s_win_kernel, pos0=pos0),
        grid=(b, KV_HEADS),
        in_specs=[pl.BlockSpec((None, TS, GROUP * HEAD_DIM), lambda bi, g: (bi, 0, C_NQ // (GROUP * HEAD_DIM) + g)),
                  cached(0), cached(1),
                  pl.BlockSpec((None, TS, HEAD_DIM), zcol(C_NKV + 4 * KV_HEADS * HEAD_DIM)),
                  pl.BlockSpec((None, TS, HEAD_DIM), zcol(C_NKV + 5 * KV_HEADS * HEAD_DIM)),
                  pl.BlockSpec((GROUP, TS, wb + PAGE_SIZE), lambda bi, g: (g, 0, 0)),
                  part, part,
                  pl.BlockSpec((None, TS, LANES), lambda bi, g: (bi, 0, C_GATE // LANES)),
                  pl.BlockSpec((1, LANES), lambda bi, g: (0, 0))],
        out_specs=pl.BlockSpec((None, TS, GROUP * HEAD_DIM), lambda bi, g: (bi, 0, g)),
        out_shape=jax.ShapeDtypeStruct((b, TS, D_NSA), F32),
        scratch_shapes=[pltpu.VMEM((ROWS_S, 1), F32), pltpu.VMEM((ROWS_S, 1), F32), pltpu.VMEM((ROWS_S, HEAD_DIM), F32)],
        compiler_params=_cparams("parallel", "parallel"),
        name="nsa_sample_win",
    )(z8, win6, win6, z8, z8, bias_w, o_c, o_s, z8, gate_b)


def _t5_bucket(dist):
    n = np.maximum(dist, 0)
    exact = NUM_BUCKETS // 2
    nf = np.maximum(n, 1).astype(np.float32)
    large = exact + (np.log(nf / np.float32(exact)) / np.float32(math.log(MAX_DISTANCE / exact))
                     * np.float32(NUM_BUCKETS - exact)).astype(np.int32)
    return np.where(n < exact, n, np.minimum(large, NUM_BUCKETS - 1)).astype(np.int32)


FAR_DIST = int(np.max(np.nonzero(_t5_bucket(np.arange(4 * MAX_DISTANCE)) < NUM_BUCKETS - 1)[0])) + 1
N_TZ = (FAR_DIST + TK - 2) // TQ + 1


def _bias_table(rel_bias, dist):
    onehot = jax.nn.one_hot(jnp.asarray(_t5_bucket(dist).astype(np.int8)), NUM_BUCKETS, dtype=F32)
    return jnp.einsum("rcb,bh->hrc", onehot, rel_bias, precision=lax.Precision.HIGHEST)


def _pack_weights(p):
    w_in = p["w_in"].astype(BF16)
    pad = jnp.zeros(w_in.shape[:2] + (N_IN - C_GATE - 32,), BF16)
    w_in_p = jnp.concatenate([w_in[..., 0:2048], w_in[..., 2056:3080], w_in[..., 3080:4616], w_in[..., 4640:5152],
                              w_in[..., 2048:2056], w_in[..., 4616:4640], pad], axis=2)
    return dict(
        w_in=w_in_p, w_out=p["w_out"].astype(BF16), w_down=p["ffn_w_down"].astype(BF16),
        w_up=p["ffn_w_up"].astype(BF16))


def _pack_layer(p, big, l):
    gb = p["mlstm_gate_b"][l]
    mlstm_gate_b = jnp.zeros((1, LANES), F32).at[0, GATE_MI:GATE_MI + 4].set(gb[0]).at[0, GATE_MF:GATE_MF + 4].set(gb[1])
    nsa_gate_b = jnp.zeros((1, LANES), F32).at[0, GATE_NG:GATE_NG + 3 * NSA_HEADS].set(p["nsa_gate_b"][l].reshape(-1))
    w1 = p["cmp_w1"][l]
    wcat = jnp.concatenate([w1[:, 0:CMP_STRIDE], w1[:, CMP_STRIDE:CMP_BLOCK]], axis=-1).astype(BF16)
    lam = lax.complex(p["s5_a_re"][l], p["s5_a_im"][l])
    lam_bar = jnp.exp(lam * jnp.exp(p["s5_log_step"][l])[:, None])
    b_bar = ((lam_bar - 1.0) / lam)[..., None] * lax.complex(p["s5_b_re"][l], p["s5_b_im"][l])
    gi = S5_GROUPS // S5_IN_TILES
    bd_in = lambda m: jnp.einsum("cgph,gk->cghkp", m.reshape(S5_IN_TILES, gi, S5_STATE, S5_GROUP_WIDTH),
                                 jnp.eye(gi, dtype=F32)).reshape(S5_IN_TILES, S5_IN_FEATS, 256)
    wb = jnp.concatenate([bd_in(b_bar.real), bd_in(b_bar.imag)], axis=2).astype(BF16)
    go = S5_GROUPS // S5_OUT_TILES
    bd_out = lambda m: jnp.einsum("jghp,gk->jgpkh", m.reshape(S5_OUT_TILES, go, S5_GROUP_WIDTH, S5_STATE),
                                  jnp.eye(go, dtype=F32)).reshape(S5_OUT_TILES, S5_OUT_CH, 256)
    wc = jnp.stack([bd_out(p["s5_c_re"][l]), -bd_out(p["s5_c_im"][l])]).astype(BF16)
    half = lambda w: jnp.pad(w, ((0, 0), (0, D_FFP - D_FF)))
    conv_w = p["ffn_conv_w"][l]
    conv_b = p["ffn_conv_b"][l][None, :]
    return dict(
        big, layer=l, mlstm_gate_b=mlstm_gate_b, mlstm_norm_w=p["mlstm_norm_w"][l][None, :], nsa_gate_b=nsa_gate_b,
        wcat=wcat, cmp_b1=p["cmp_b1"][l], cmp_w2=p["cmp_w2"][l].astype(BF16),
        s5_wb=wb, s5_wc=wc, s5_lam_r=lam_bar.real.reshape(1, S5_CH), s5_lam_i=lam_bar.imag.reshape(1, S5_CH),
        s5_d=p["s5_d"][l][None, :], s5_glu_w=p["s5_glu_w"][l].astype(BF16), s5_glu_b=p["s5_glu_b"][l][None, :],
        ln1_w=p["ln1_w"][l][None, :], ln1_b=p["ln1_b"][l][None, :],
        conv_w8=jnp.pad(jnp.concatenate([half(conv_w[:, :D_FF]), half(conv_w[:, D_FF:])], axis=1),
                        ((0, SUBLANES - CONV_W), (0, 0))),
        conv_b=jnp.concatenate([half(conv_b[:, :D_FF]), half(conv_b[:, D_FF:])], axis=1),
        ln2_w=p["ln2_w"][l][None, :], ln2_b=p["ln2_b"][l][None, :],
    )


def _unpad_ff(x):
    return jnp.concatenate([x[..., :D_FF], x[..., D_FFP:D_FFP + D_FF]], axis=-1)


def _pad_ff(x):
    pad = [(0, 0)] * (x.ndim - 1) + [(0, D_FFP - D_FF)]
    return jnp.concatenate([jnp.pad(x[..., :D_FF], pad), jnp.pad(x[..., D_FF:], pad)], axis=-1)


def _mixer_tail(x2d, z, b, t, t_use, lw, o_mlstm, o_nsa, s5_state, conv_state8, tm):
    pad8 = lambda s: jnp.pad(s.reshape(b, S5_CH), ((0, SUBLANES - b), (0, 0)))
    o_s5, xr, xi = _s5(z.reshape(b, t, N_IN), lw["s5_wb"], lw["s5_wc"], lw["s5_lam_r"], lw["s5_lam_i"], lw["s5_d"],
                       lw["s5_glu_w"], lw["s5_glu_b"], pad8(s5_state[0]), pad8(s5_state[1]), lc=S5_LC, steps=t_use)
    o_s5 = o_s5.reshape(b * t, D_S5)
    layer = lw["layer"]
    x1 = _wout_ln(o_mlstm.reshape(b * t, D_MLSTM), o_nsa.reshape(b * t, D_NSA), o_s5, x2d,
                  lw["w_out"], layer, lw["ln1_w"], lw["ln1_b"], tm=512)
    s5_new = (xr[:b].reshape(b, S5_GROUPS, S5_STATE), xi[:b].reshape(b, S5_GROUPS, S5_STATE))
    if conv_state8 is None:
        x2, tail_a, tail_g = _ffn_prompt(x1.reshape(b, t, D_MODEL), lw["w_up"], lw["w_down"], layer, lw["conv_w8"],
                                         lw["conv_b"], lw["ln2_w"], lw["ln2_b"], tt=FFN_TT, tf=FFN_TF)
        keep = slice(SUBLANES - (CONV_W - 1), SUBLANES)
        conv_new = jnp.concatenate([tail_a[:, keep, :D_FF], tail_g[:, keep, :D_FF]], axis=-1)
        return x2.reshape(b * t, D_MODEL), s5_new, conv_new
    up = _matmul(x1, lw["w_up"], layer, tm=tm, tn=FFN_TF, n=2 * D_FFP, w_cols=_up_cols(FFN_TF))
    hgate = _convgate(up.reshape(b, t, 2 * D_FFP), conv_state8, lw["conv_w8"], lw["conv_b"], tf=FFN_TF)
    x2 = _down_ln(hgate.reshape(b * t, D_FFP), lw["w_down"], layer, x1, lw["ln2_w"], lw["ln2_b"], tm=512, tk=1408)
    conv_new = _unpad_ff(up.reshape(b, t, 2 * D_FFP)[:, t_use - (CONV_W - 1):t_use])
    return x2, s5_new, conv_new


def _prompt_layer(x2d, b, t, lw, rel_bias, rows_so_far):
    z, kv_rows, win_rows = _proj_in(x2d, lw["w_in"], lw["layer"], rows_so_far, tm=min(1024, b * t), tn=768)
    z3 = z.reshape(b, t, N_IN)
    zeros = lambda *s: jnp.zeros(s, F32)
    o_mlstm, c1, n1, m1 = _mlstm(z3, lw["mlstm_gate_b"], lw["mlstm_norm_w"],
                                 zeros(b, MLSTM_HEADS, HEAD_DIM, HEAD_DIM), zeros(b, MLSTM_HEADS, LANES),
                                 zeros(b, MLSTM_HEADS, LANES), L=MLSTM_L, lin=MLSTM_L, t_valid=t)
    kcvc = _cmp_finish(_cmp_project_prompt(z3, lw["wcat"]), lw["cmp_b1"], lw["cmp_w2"])
    ncp = t // CMP_STRIDE
    bias_c = _bias_table(rel_bias, np.arange(t)[:, None] - (np.arange(ncp) * CMP_STRIDE + CMP_BLOCK - 1)[None, :])
    ti = np.arange(TQ)[:, None] - np.arange(TK)[None, :]
    far = _bias_table(rel_bias, np.full((1, 1), FAR_DIST))
    tz = jnp.stack([_bias_table(rel_bias, d * TQ + ti) - far for d in range(N_TZ)])
    o_nsa = _nsa_prompt(z3, kcvc, bias_c, tz, lw["nsa_gate_b"])
    x2, s5_new, conv_new = _mixer_tail(
        x2d, z, b, t, t, lw, o_mlstm, o_nsa, (zeros(b, S5_GROUPS, S5_STATE), zeros(b, S5_GROUPS, S5_STATE)),
        None, tm=1024)
    state = (None, None, c1, n1, m1[:, :, 0], s5_new[0], s5_new[1], conv_new)
    return x2, state, (kv_rows, win_rows)


def _sample_layer(x2d, b, tn, lw, rel_bias, layer, cache6, page_table, win6, mlstm_state, s5_state, conv_state):
    npages = page_table.shape[1]
    pos0 = npages * PAGE_SIZE
    z = _matmul(x2d, lw["w_in"], lw["layer"], tm=1024, tn=768)
    z8 = z.reshape(b, TS, N_IN)
    c0, n0, m0 = mlstm_state
    o_mlstm, c1, n1, m1 = _mlstm(z8, lw["mlstm_gate_b"], lw["mlstm_norm_w"], c0, n0,
                                 jnp.broadcast_to(m0[:, :, None], (b, MLSTM_HEADS, LANES)),
                                 L=LANES, lin=TS, t_valid=tn)
    n_chunks = (pos0 + tn) // CMP_STRIDE
    n_cmp = n_chunks - CMP_BLOCK // CMP_STRIDE + 1
    n_sel = -(-(pos0 + tn) // SEL_BLOCK)
    kcvc = _cmp_finish(_cmp_project_sample(cache6, page_table, layer, lw["wcat"]), lw["cmp_b1"], lw["cmp_w2"])
    qpos = pos0 + np.arange(TS)[:, None]
    bias_c = _bias_table(rel_bias, qpos - (np.arange(n_chunks) * CMP_STRIDE + CMP_BLOCK - 1)[None, :])
    o_c, sel = _nsa_sample_cmp(z8, kcvc, bias_c, pos0=pos0, n_cmp=n_cmp, n_sel=n_sel)
    kk = np.arange(PAGE_SIZE)[None, :]
    bias_tail = _bias_table(rel_bias, np.repeat(qpos - (pos0 - PAGE_SIZE + kk), KV_HEADS, axis=1))
    bias_new = _bias_table(rel_bias, qpos - (pos0 + kk))
    bias_far = _bias_table(rel_bias, np.broadcast_to(qpos - (pos0 - 2 * PAGE_SIZE), (TS, PAGE_SIZE)))
    o_s = _nsa_sample_sel(z8, sel, bias_tail, bias_new, bias_far, cache6, page_table, layer, pos0=pos0)
    wb = win6.shape[2]
    wk = np.arange(wb)[None, :]
    bias_w = _bias_table(rel_bias, np.concatenate([qpos - (pos0 - wb + wk), qpos - (pos0 + kk)], axis=1))
    o_nsa = _nsa_sample_win(z8, win6, layer, bias_w, o_c, o_s, lw["nsa_gate_b"], pos0=pos0)
    conv_state8 = jnp.pad(_pad_ff(conv_state), ((0, 0), (SUBLANES - (CONV_W - 1), 0), (0, 0)))
    x2, s5_new, conv_new = _mixer_tail(x2d, z, b, TS, tn, lw, o_mlstm, o_nsa, s5_state, conv_state8, tm=1024)
    nkv = z8[:, :tn, C_NKV:C_NKV + N_KV_SLOTS * KV_HEADS * HEAD_DIM].reshape(b, tn, N_KV_SLOTS, KV_HEADS, HEAD_DIM)
    state = (nkv[:, :, :4], nkv[:, :, 4:], c1, n1, m1[:, :, 0], s5_new[0], s5_new[1], conv_new)
    return x2, state


def kernel(x_prompt, x_sample, cache_nsa_kv, cache_win_kv, state_mlstm_c, state_mlstm_n, state_mlstm_m,
           state_s5_re, state_s5_im, state_ffn_conv, page_table, w_in, mlstm_gate_b, mlstm_norm_w,
           nsa_gate_b, cmp_w1, cmp_b1, cmp_w2, rel_bias, s5_a_re, s5_a_im, s5_b_re, s5_b_im, s5_c_re,
           s5_c_im, s5_d, s5_log_step, s5_glu_w, s5_glu_b, w_out, ln1_w, ln1_b, ffn_w_up, ffn_conv_w,
           ffn_conv_b, ffn_w_down, ln2_w, ln2_b):
    params = dict(w_in=w_in, mlstm_gate_b=mlstm_gate_b, mlstm_norm_w=mlstm_norm_w, nsa_gate_b=nsa_gate_b,
                  cmp_w1=cmp_w1, cmp_b1=cmp_b1, cmp_w2=cmp_w2, s5_a_re=s5_a_re, s5_a_im=s5_a_im, s5_b_re=s5_b_re,
                  s5_b_im=s5_b_im, s5_c_re=s5_c_re, s5_c_im=s5_c_im, s5_d=s5_d, s5_log_step=s5_log_step,
                  s5_glu_w=s5_glu_w, s5_glu_b=s5_glu_b, w_out=w_out, ln1_w=ln1_w, ln1_b=ln1_b, ffn_w_up=ffn_w_up,
                  ffn_conv_w=ffn_conv_w, ffn_conv_b=ffn_conv_b, ffn_w_down=ffn_w_down, ln2_w=ln2_w, ln2_b=ln2_b)
    depth = w_in.shape[0]
    bp, tp, _ = x_prompt.shape
    bs, tn, _ = x_sample.shape
    assert tp % TK == 0 and tn < CMP_STRIDE and tn <= TS and SUBLANES % bp == 0 and SUBLANES % bs == 0
    assert (tn * bs) % SUBLANES == 0 and tp % min(FFN_TT, tp) == 0
    assert page_table.shape[1] % CMP_PAGES == 0 and cache_nsa_kv.shape[2] == PAGE_SIZE
    xp = x_prompt.reshape(bp * tp, D_MODEL)
    xs = jnp.pad(x_sample, ((0, 0), (0, TS - tn), (0, 0))).reshape(bs * TS, D_MODEL)
    p_states, s_states = [], []
    big = _pack_weights(params)
    rows_so_far = None
    for l in range(depth):
        lw = _pack_layer(params, big, l)
        xp, sp, rows_so_far = _prompt_layer(xp, bp, tp, lw, rel_bias, rows_so_far)
        xs, ss = _sample_layer(xs, bs, tn, lw, rel_bias, l, cache_nsa_kv, page_table, cache_win_kv,
                               (state_mlstm_c[l], jnp.pad(state_mlstm_n[l], ((0, 0), (0, 0), (0, LANES - HEAD_DIM))),
                                state_mlstm_m[l]),
                               (state_s5_re[l], state_s5_im[l]), state_ffn_conv[l])
        p_states.append(sp)
        s_states.append(ss)
    stk = lambda states, i: jnp.stack([s[i] for s in states])
    y_prompt = xp.reshape(bp, tp, D_MODEL)
    y_sample = xs.reshape(bs, TS, D_MODEL)[:, :tn]
    wrows = min(WINDOW, tp)
    nsa_kv_p = rows_so_far[0].reshape(depth, bp, tp, 4, KV_HEADS, HEAD_DIM)
    win_kv_p = rows_so_far[1].reshape(depth, bp, tp, 2, KV_HEADS, HEAD_DIM)[:, :, tp - wrows:]
    return (y_prompt, y_sample,
            nsa_kv_p, stk(s_states, 0), win_kv_p, stk(s_states, 1),
            stk(p_states, 2), stk(s_states, 2), stk(p_states, 3), stk(s_states, 3), stk(p_states, 4), stk(s_states, 4),
            stk(p_states, 5), stk(s_states, 5), stk(p_states, 6), stk(s_states, 6), stk(p_states, 7), stk(s_states, 7))
```

```python
import functools
import math

import jax
import jax.numpy as jnp
import numpy as np
from jax import lax
from jax.experimental import pallas as pl
from jax.experimental.pallas import tpu as pltpu

F32 = jnp.float32
BF16 = jnp.bfloat16
NEG_INF = float("-inf")
M_INIT = -1e30
MASKED = -1e30

D_MODEL = 2048
PAGE_SIZE = 128
D_MLSTM = D_MODEL // 4
D_NSA = D_MODEL // 2
D_S5 = D_MODEL - D_MLSTM - D_NSA
HEAD_DIM = 128
MLSTM_HEADS = D_MLSTM // HEAD_DIM
NSA_HEADS = D_NSA // HEAD_DIM
KV_HEADS = 2
GROUP = NSA_HEADS // KV_HEADS
N_KV_SLOTS = 6
CMP_BLOCK = 32
CMP_STRIDE = 16
SEL_BLOCK = 64
SEL_TOPK = 16
FORCE_SCORE = 1e4
WINDOW = 512
S5_GROUP_WIDTH = 16
S5_GROUPS = D_S5 // S5_GROUP_WIDTH
S5_STATE = 64
S5_CH = S5_GROUPS * S5_STATE
NUM_BUCKETS = 32
MAX_DISTANCE = 128
D_FF = ((8 * D_MODEL // 3 + 127) // 128) * 128
CONV_W = 3
LN_EPS = 1e-5
DEPTH = 2
DEEPNORM_ALPHA = (2 * DEPTH) ** 0.25
QK_SCALE = HEAD_DIM ** -0.5

LANES = 128
SUBLANES = 8
V7X_VMEM_LIMIT = 56 * 2 ** 20

C_MQ, C_MK, C_MV, C_MO = 0, 512, 1024, 1536
C_NQ = 2048
C_NKV = 3072
C_SU = 4608
C_GATE = 5120
N_IN = 5376
GATE_MI, GATE_MF, GATE_NG = 0, 4, 8
D_FFP = 5632
MLSTM_L = 512
TQ = 512
TK = 512
S5_LC = 256
FFN_TT = 1024
FFN_TF = 512
FFN_HALO = 16
CMP_PAGES = 32
SEL_PAGES = 32
SEL_TILE = 32
SEL_SHIFT = SEL_BLOCK.bit_length() - 1


def _cparams(*sem):
    return pltpu.CompilerParams(dimension_semantics=sem, vmem_limit_bytes=V7X_VMEM_LIMIT)


def _dot(a, b):
    return jnp.dot(a, b, preferred_element_type=F32)


def _dot_nt(a, b):
    return lax.dot_general(a, b, (((1,), (1,)), ((), ())), preferred_element_type=F32)


def _layer_norm(y, w, b):
    mu = jnp.mean(y, axis=-1, keepdims=True)
    d = y - mu
    var = jnp.mean(d * d, axis=-1, keepdims=True)
    return d * lax.rsqrt(var + LN_EPS) * w + b


def _mm_kernel(x_ref, w_ref, o_ref, *, realign_tile=None):
    acc = _dot(x_ref[...].astype(BF16), w_ref[...])
    if realign_tile is not None:
        acc = _realign(acc, pl.program_id(1) == realign_tile)
    o_ref[...] = acc.astype(o_ref.dtype)


UP_OVER = D_FFP - D_FF


def _up_cols(tf):
    nf, per = D_FFP // tf, tf // LANES
    return lambda j: (j * per + (j >= nf).astype(jnp.int32) * (D_FF // LANES - nf * per)
                      - (j == 2 * nf - 1).astype(jnp.int32) * (UP_OVER // LANES)) * LANES


def _realign(up, is_last):
    shifted = jnp.concatenate([up[:, UP_OVER:], jnp.zeros((up.shape[0], UP_OVER), up.dtype)], axis=1)
    return jnp.where(is_last, shifted, up)


def _matmul(x, w, layer, *, tm, tn, n=None, w_cols=None):
    m, k = x.shape
    n = w.shape[2] if n is None else n
    tm = min(tm, m)
    if w_cols is None:
        w_spec = pl.BlockSpec((None, k, tn), lambda i, j: (layer, 0, j))
    else:
        w_spec = pl.BlockSpec((None, pl.Element(k), pl.Element(tn)), lambda i, j: (layer, 0, w_cols(j)))
    return pl.pallas_call(
        functools.partial(_mm_kernel, realign_tile=None if w_cols is None else n // tn - 1),
        grid=(m // tm, n // tn),
        in_specs=[pl.BlockSpec((tm, k), lambda i, j: (i, 0)), w_spec],
        out_specs=pl.BlockSpec((tm, tn), lambda i, j: (i, j)),
        out_shape=jax.ShapeDtypeStruct((m, n), F32),
        compiler_params=_cparams("parallel", "parallel"),
        name="proj",
    )(x, w)


KV4_GROUPS = 4 * KV_HEADS
WIN_GROUPS = 2 * KV_HEADS


def _proj_in_kernel(x_ref, w_ref, z_ref, kv_ref, win_ref, xb_ref, *, tn):
    j = pl.program_id(1)

    @pl.when(j == 0)
    def _():
        xb_ref[...] = x_ref[...].astype(BF16)

    acc = _dot(xb_ref[...], w_ref[...])
    z_ref[...] = acc
    tm = acc.shape[0]
    per_tile = tn // HEAD_DIM
    first_tile = C_NKV // tn
    for tile in range(first_tile, first_tile + N_KV_SLOTS * KV_HEADS // per_tile):
        @pl.when(j == tile)
        def _(tile=tile):
            for c in range(per_tile):
                grp = (tile - first_tile) * per_tile + c
                val = acc[:, c * HEAD_DIM:(c + 1) * HEAD_DIM]
                if grp < KV4_GROUPS:
                    kv_ref[pl.ds(grp, tm, stride=KV4_GROUPS), :] = val
                else:
                    win_ref[pl.ds(grp - KV4_GROUPS, tm, stride=WIN_GROUPS), :] = val


def _proj_in_carry_kernel(x_ref, w_ref, kv_in_ref, win_in_ref, z_ref, kv_ref, win_ref, xb_ref, *, tn):
    del kv_in_ref, win_in_ref
    _proj_in_kernel(x_ref, w_ref, z_ref, kv_ref, win_ref, xb_ref, tn=tn)


def _proj_in(x, w, layer, rows_so_far, *, tm, tn):
    m, k = x.shape
    depth, _, n = w.shape
    nm = m // tm
    assert C_NKV % tn == 0 and (N_KV_SLOTS * KV_HEADS * HEAD_DIM) % tn == 0
    carried = () if rows_so_far is None else tuple(rows_so_far)
    return pl.pallas_call(
        functools.partial(_proj_in_carry_kernel if carried else _proj_in_kernel, tn=tn),
        grid=(nm, n // tn),
        in_specs=[pl.BlockSpec((tm, k), lambda i, j: (i, 0)), pl.BlockSpec((None, k, tn), lambda i, j: (layer, 0, j))]
                 + [pl.BlockSpec(memory_space=pl.ANY)] * len(carried),
        out_specs=[pl.BlockSpec((tm, tn), lambda i, j: (i, j)),
                   pl.BlockSpec((tm * KV4_GROUPS, HEAD_DIM), lambda i, j: (layer * nm + i, 0)),
                   pl.BlockSpec((tm * WIN_GROUPS, HEAD_DIM), lambda i, j: (layer * nm + i, 0))],
        out_shape=[jax.ShapeDtypeStruct((m, n), F32),
                   jax.ShapeDtypeStruct((depth * m * KV4_GROUPS, HEAD_DIM), F32),
                   jax.ShapeDtypeStruct((depth * m * WIN_GROUPS, HEAD_DIM), F32)],
        input_output_aliases={2: 1, 3: 2} if carried else {},
        scratch_shapes=[pltpu.VMEM((tm, k), BF16)],
        compiler_params=_cparams("parallel", "arbitrary"),
        name="proj_in",
    )(x, w, *carried)


def _wout_kernel(om_ref, on_ref, os_ref, x_ref, w_ref, lw_ref, lb_ref, o_ref):
    acc = _dot(om_ref[...].astype(BF16), w_ref[0:D_MLSTM, :])
    acc += _dot(on_ref[...].astype(BF16), w_ref[D_MLSTM:D_MLSTM + D_NSA, :])
    acc += _dot(os_ref[...].astype(BF16), w_ref[D_MLSTM + D_NSA:D_MODEL, :])
    o_ref[...] = _layer_norm(DEEPNORM_ALPHA * x_ref[...] + acc, lw_ref[...], lb_ref[...])


def _wout_ln(om, on, os_, x, w, layer, lw, lb, *, tm):
    m = x.shape[0]
    tm = min(tm, m)
    row = lambda width: pl.BlockSpec((tm, width), lambda i: (i, 0))
    full = lambda a: pl.BlockSpec(a.shape, lambda i: (0, 0))
    w_spec = pl.BlockSpec((None,) + w.shape[1:], lambda i: (layer, 0, 0))
    return pl.pallas_call(
        _wout_kernel,
        grid=(m // tm,),
        in_specs=[row(D_MLSTM), row(D_NSA), row(D_S5), row(D_MODEL), w_spec, full(lw), full(lb)],
        out_specs=row(D_MODEL),
        out_shape=jax.ShapeDtypeStruct((m, D_MODEL), F32),
        compiler_params=_cparams("parallel"),
        name="wout_ln",
    )(om, on, os_, x, w, lw, lb)


def _valid_rows(w_ref, tile):
    rows = w_ref.shape[0]
    row = tile * rows + lax.broadcasted_iota(jnp.int32, (rows, 1), 0)
    w = w_ref[...]
    return jnp.where(row < D_FF, w, jnp.zeros_like(w))


def _down_kernel(h_ref, w_ref, x_ref, lw_ref, lb_ref, o_ref, acc_ref, *, nk):
    k = pl.program_id(1)

    @pl.when(k == 0)
    def _():
        acc_ref[...] = jnp.zeros_like(acc_ref)

    acc_ref[...] += _dot(h_ref[...], _valid_rows(w_ref, k))

    @pl.when(k == nk - 1)
    def _():
        o_ref[...] = _layer_norm(DEEPNORM_ALPHA * x_ref[...] + acc_ref[...], lw_ref[...], lb_ref[...])


def _down_ln(h, w, layer, x, lw, lb, *, tm, tk):
    m, kk = h.shape
    tm = min(tm, m)
    nk = kk // tk
    return pl.pallas_call(
        functools.partial(_down_kernel, nk=nk),
        grid=(m // tm, nk),
        in_specs=[
            pl.BlockSpec((tm, tk), lambda i, k: (i, k)),
            pl.BlockSpec((None, tk, D_MODEL), lambda i, k: (layer, k, 0)),
            pl.BlockSpec((tm, D_MODEL), lambda i, k: (i, 0)),
            pl.BlockSpec((1, D_MODEL), lambda i, k: (0, 0)),
            pl.BlockSpec((1, D_MODEL), lambda i, k: (0, 0)),
        ],
        out_specs=pl.BlockSpec((tm, D_MODEL), lambda i, k: (i, 0)),
        out_shape=jax.ShapeDtypeStruct((m, D_MODEL), F32),
        scratch_shapes=[pltpu.VMEM((tm, D_MODEL), F32)],
        compiler_params=_cparams("parallel", "arbitrary"),
        name="down_ln",
    )(h, w, x, lw, lb)


def _convgate_kernel(a_ref, g_ref, sa_ref, sg_ref, wa_ref, wg_ref, ba_ref, bg_ref, o_ref):
    rows = lax.broadcasted_iota(jnp.int32, (1, SUBLANES, 1), 1)

    def conv(cur_ref, st_ref, w_ref, b_ref):
        cur = cur_ref[...]
        prev = st_ref[...]
        p1 = prev[:, 7:8, :]
        p2 = prev[:, 6:7, :]
        x1 = jnp.where(rows == 0, p1, pltpu.roll(cur, 1, 1))
        x2 = jnp.where(rows == 0, p2, jnp.where(rows == 1, p1, pltpu.roll(cur, 2, 1)))
        w = w_ref[...]
        return b_ref[...] + w[0:1, :] * x2 + w[1:2, :] * x1 + w[2:3, :] * cur

    a = conv(a_ref, sa_ref, wa_ref, ba_ref)
    g = conv(g_ref, sg_ref, wg_ref, bg_ref)
    o_ref[...] = (a * jax.nn.sigmoid(a) * g).astype(o_ref.dtype)


def _convgate(up, state8, conv_w8, conv_b, *, tf):
    b, t, _ = up.shape
    assert t == SUBLANES
    nf = D_FFP // tf
    cur_a = pl.BlockSpec((b, t, tf), lambda fi: (0, 0, fi))
    cur_g = pl.BlockSpec((b, t, tf), lambda fi: (0, 0, fi + nf))
    w_a = pl.BlockSpec((SUBLANES, tf), lambda fi: (0, fi))
    w_g = pl.BlockSpec((SUBLANES, tf), lambda fi: (0, fi + nf))
    b_a = pl.BlockSpec((1, tf), lambda fi: (0, fi))
    b_g = pl.BlockSpec((1, tf), lambda fi: (0, fi + nf))
    return pl.pallas_call(
        _convgate_kernel,
        grid=(nf,),
        in_specs=[cur_a, cur_g, cur_a, cur_g, w_a, w_g, b_a, b_g],
        out_specs=pl.BlockSpec((b, t, tf), lambda fi: (0, 0, fi)),
        out_shape=jax.ShapeDtypeStruct((b, t, D_FFP), BF16),
        compiler_params=_cparams("parallel"),
        name="convgate",
    )(up, up, state8, state8, conv_w8, conv_w8, conv_b, conv_b)


def _ffn_kernel(x_ref, xh_ref, wa_ref, wg_ref, wd_ref, cwa_ref, cwg_ref, cba_ref, cbg_ref, lw_ref, lb_ref,
                o_ref, sa_ref, sg_ref, xe_ref, *, nf):
    first = pl.program_id(1) == 0
    f = pl.program_id(2)
    tt = x_ref.shape[0]

    @pl.when(f == 0)
    def _():
        o_ref[...] = jnp.zeros_like(o_ref)
        halo = jnp.where(first, 0.0, xh_ref[...])
        xe_ref[0:FFN_HALO, :] = halo.astype(BF16)
        xe_ref[FFN_HALO:, :] = x_ref[...].astype(BF16)

    xe = xe_ref[...]

    def branch(w_ref, cw_ref, cb_ref, s_ref, second_half):
        up = _dot(xe, w_ref[...])
        if second_half:
            up = _realign(up, f == nf - 1)
        s_ref[...] = up[FFN_HALO + tt - SUBLANES:FFN_HALO + tt, :]
        w = cw_ref[...]
        x1 = pltpu.roll(up, 1, 0)[FFN_HALO:, :]
        x2 = pltpu.roll(up, 2, 0)[FFN_HALO:, :]
        return cb_ref[...] + w[0:1, :] * x2 + w[1:2, :] * x1 + w[2:3, :] * up[FFN_HALO:, :]

    a = branch(wa_ref, cwa_ref, cba_ref, sa_ref, False)
    g = branch(wg_ref, cwg_ref, cbg_ref, sg_ref, True)
    o_ref[...] += _dot((a * jax.nn.sigmoid(a) * g).astype(BF16), _valid_rows(wd_ref, f))

    @pl.when(f == nf - 1)
    def _():
        o_ref[...] = _layer_norm(DEEPNORM_ALPHA * x_ref[...] + o_ref[...], lw_ref[...], lb_ref[...])


def _ffn_prompt(x3, w_up, w_down, layer, conv_w8, conv_b, lw, lb, *, tt, tf):
    b, t, _ = x3.shape
    nf = D_FFP // tf
    tt = min(tt, t)
    hb = tt // FFN_HALO
    half_a = lambda rows: pl.BlockSpec((rows, tf), lambda bi, ti, fi: (0, fi))
    half_g = lambda rows: pl.BlockSpec((rows, tf), lambda bi, ti, fi: (0, fi + nf))
    cols = _up_cols(tf)
    w_a = pl.BlockSpec((None, pl.Element(D_MODEL), pl.Element(tf)), lambda bi, ti, fi: (layer, 0, cols(fi)))
    w_g = pl.BlockSpec((None, pl.Element(D_MODEL), pl.Element(tf)), lambda bi, ti, fi: (layer, 0, cols(fi + nf)))
    vec = pl.BlockSpec((1, D_MODEL), lambda bi, ti, fi: (0, 0))
    tail = pl.BlockSpec((None, SUBLANES, tf), lambda bi, ti, fi: (bi, 0, fi))
    return pl.pallas_call(
        functools.partial(_ffn_kernel, nf=nf),
        grid=(b, t // tt, nf),
        in_specs=[pl.BlockSpec((None, tt, D_MODEL), lambda bi, ti, fi: (bi, ti, 0), pipeline_mode=pl.Buffered(1)),
                  pl.BlockSpec((None, FFN_HALO, D_MODEL), lambda bi, ti, fi: (bi, jnp.maximum(ti * hb - 1, 0), 0)),
                  w_a, w_g,
                  pl.BlockSpec((None, tf, D_MODEL), lambda bi, ti, fi: (layer, fi, 0)),
                  half_a(SUBLANES), half_g(SUBLANES), half_a(1), half_g(1), vec, vec],
        out_specs=[pl.BlockSpec((None, tt, D_MODEL), lambda bi, ti, fi: (bi, ti, 0)), tail, tail],
        out_shape=[jax.ShapeDtypeStruct((b, t, D_MODEL), F32),
                   jax.ShapeDtypeStruct((b, SUBLANES, D_FFP), F32),
                   jax.ShapeDtypeStruct((b, SUBLANES, D_FFP), F32)],
        scratch_shapes=[pltpu.VMEM((tt + FFN_HALO, D_MODEL), BF16)],
        compiler_params=_cparams("parallel", "arbitrary", "arbitrary"),
        name="ffn_prompt",
    )(x3, x3, w_up, w_up, w_down, conv_w8, conv_w8, conv_b, conv_b, lw, lb)


def _cumsum_rows(x):
    n = x.shape[0]
    rows = lax.broadcasted_iota(jnp.int32, (n, 1), 0)
    d = 1
    while d < n:
        x = x + jnp.where(rows >= d, pltpu.roll(x, d, 0), 0.0)
        d *= 2
    return x


def _log_sigmoid(x):
    return jnp.minimum(x, 0.0) - jnp.log1p(jnp.exp(-jnp.abs(x)))


def _mlstm_kernel(q_ref, k_ref, v_ref, og_ref, g_ref, gb_ref, nw_ref, c0_ref, n0_ref, m0_ref,
                  out_ref, c_ref, n_ref, m_ref, *, L, t_valid):
    ci = pl.program_id(1)
    lin = q_ref.shape[0]

    @pl.when(ci == 0)
    def _():
        c_ref[...] = c0_ref[...]
        n_ref[...] = n0_ref[...]
        m_ref[...] = m0_ref[...]

    def rows_of(ref):
        x = ref[...]
        if lin < L:
            x = jnp.concatenate([x, jnp.zeros((L - lin, x.shape[1]), x.dtype)], axis=0)
        return x

    rows = lax.broadcasted_iota(jnp.int32, (L, 1), 0)
    valid = (ci * L + rows) < t_valid
    pre = rows_of(g_ref) + gb_ref[...]
    lf = jnp.where(valid, _log_sigmoid(pre), 0.0)
    ig = jnp.where(valid, pre, NEG_INF)
    bcum = _cumsum_rows(lf)
    dt = (pltpu.roll(ig, GATE_MF - GATE_MI, 1) - bcum).T
    q_all, k_all, v_all, og_all = rows_of(q_ref), rows_of(k_ref), rows_of(v_ref), rows_of(og_ref)
    tri = lax.broadcasted_iota(jnp.int32, (L, L), 0) >= lax.broadcasted_iota(jnp.int32, (L, L), 1)
    nw = nw_ref[...]

    for h in range(MLSTM_HEADS):
        sl = slice(h * HEAD_DIM, (h + 1) * HEAD_DIM)
        q = q_all[:, sl]
        k = k_all[:, sl] * QK_SCALE
        v = v_all[:, sl]
        qb, kb, vb = q.astype(BF16), k.astype(BF16), v.astype(BF16)
        b_col = bcum[:, GATE_MF + h:GATE_MF + h + 1]
        ig_col = ig[:, GATE_MI + h:GATE_MI + h + 1]
        d_row = dt[GATE_MF + h:GATE_MF + h + 1, :]
        c_prev = c_ref[h]
        n_prev = n_ref[h:h + 1, :]
        m_prev = m_ref[h:h + 1, 0:1]

        dmat = jnp.where(tri, b_col + d_row, NEG_INF)
        g_col = b_col + m_prev
        m_row = jnp.maximum(jnp.max(dmat, axis=1, keepdims=True), g_col)
        a = jnp.exp(dmat - m_row) * _dot_nt(qb, kb)
        w_inter = jnp.exp(g_col - m_row)
        num = _dot(a.astype(BF16), vb) + w_inter * _dot(qb, c_prev.astype(BF16))
        den = jnp.sum(a, axis=1, keepdims=True) + w_inter * jnp.sum(q * n_prev, axis=1, keepdims=True)
        hid = num / jnp.maximum(jnp.abs(den), jnp.exp(-m_row))
        mu = jnp.mean(hid, axis=1, keepdims=True)
        dlt = hid - mu
        var = jnp.mean(dlt * dlt, axis=1, keepdims=True)
        hn = dlt * lax.rsqrt(var + LN_EPS) * nw[:, sl]
        res = jax.nn.sigmoid(og_all[:, sl]) * hn
        out_ref[:, sl] = res[0:lin, :]

        f_tot = b_col[L - 1:L, :]
        w_s = f_tot - b_col + ig_col
        m_new = jnp.maximum(f_tot + m_prev, jnp.max(w_s, axis=0, keepdims=True))
        ws = jnp.exp(w_s - m_new)
        decay = jnp.exp(f_tot + m_prev - m_new)
        c_ref[h] = decay * c_prev + _dot(k.T.astype(BF16), (ws * v).astype(BF16))
        n_ref[h:h + 1, :] = decay * n_prev + jnp.sum(ws * k, axis=0, keepdims=True)
        m_ref[h:h + 1, :] = jnp.broadcast_to(m_new, (1, LANES))


def _mlstm(z, gate_b, norm_w, c0, n0, m0, *, L, lin, t_valid):
    b, tz, _ = z.shape
    nchunks = tz // lin
    qblk = lambda col: pl.BlockSpec((None, lin, D_MLSTM), lambda bi, ci: (bi, ci, col // D_MLSTM))
    st4 = pl.BlockSpec((None, MLSTM_HEADS, HEAD_DIM, HEAD_DIM), lambda bi, ci: (bi, 0, 0, 0))
    st3 = pl.BlockSpec((None, MLSTM_HEADS, LANES), lambda bi, ci: (bi, 0, 0))
    return pl.pallas_call(
        functools.partial(_mlstm_kernel, L=L, t_valid=t_valid),
        grid=(b, nchunks),
        in_specs=[qblk(C_MQ), qblk(C_MK), qblk(C_MV), qblk(C_MO),
                  pl.BlockSpec((None, lin, LANES), lambda bi, ci: (bi, ci, C_GATE // LANES)),
                  pl.BlockSpec((1, LANES), lambda bi, ci: (0, 0)),
                  pl.BlockSpec((1, D_MLSTM), lambda bi, ci: (0, 0)),
                  st4, st3, st3],
        out_specs=[pl.BlockSpec((None, lin, D_MLSTM), lambda bi, ci: (bi, ci, 0)), st4, st3, st3],
        out_shape=[jax.ShapeDtypeStruct((b, tz, D_MLSTM), F32),
                   jax.ShapeDtypeStruct((b, MLSTM_HEADS, HEAD_DIM, HEAD_DIM), F32),
                   jax.ShapeDtypeStruct((b, MLSTM_HEADS, LANES), F32),
                   jax.ShapeDtypeStruct((b, MLSTM_HEADS, LANES), F32)],
        compiler_params=_cparams("parallel", "arbitrary"),
        name="mlstm",
    )(z, z, z, z, z, gate_b, norm_w, c0, n0, m0)


S5_IN_TILES = S5_CH // 256
S5_IN_FEATS = D_S5 // S5_IN_TILES
S5_OUT_TILES = D_S5 // 256
S5_OUT_CH = S5_CH // S5_OUT_TILES


S5_LT = S5_CH // LANES


def _lane_tiles(x):
    return jnp.stack([x[:, k * LANES:(k + 1) * LANES] for k in range(x.shape[1] // LANES)])


def _lane_untile(x):
    return jnp.concatenate([x[k] for k in range(x.shape[0])], axis=1)


def _s5_kernel(u_ref, wb_ref, wc_ref, lr_ref, li_ref, d_ref, gw_ref, gb_ref, x0r_ref, x0i_ref,
               o_ref, xr_ref, xi_ref, sr_ref, si_ref, y_ref, *, steps):
    nb, lc, _ = u_ref.shape

    @pl.when(pl.program_id(0) == 0)
    def _():
        xr_ref[...] = x0r_ref[...]
        xi_ref[...] = x0i_ref[...]

    for b in range(nb):
        ub = u_ref[b].astype(BF16)
        for c in range(S5_IN_TILES):
            bu = _dot(ub[:, c * S5_IN_FEATS:(c + 1) * S5_IN_FEATS], wb_ref[c])
            for k in range(256 // LANES):
                rows = pl.ds(b, lc, stride=nb)
                sr_ref.at[c * (256 // LANES) + k][rows, :] = bu[:, k * LANES:(k + 1) * LANES]
                si_ref.at[c * (256 // LANES) + k][rows, :] = bu[:, 256 + k * LANES:256 + (k + 1) * LANES]
    lam_r = _lane_tiles(jnp.broadcast_to(lr_ref[...], (SUBLANES, S5_CH)))
    lam_i = _lane_tiles(jnp.broadcast_to(li_ref[...], (SUBLANES, S5_CH)))
    row8 = lax.broadcasted_iota(jnp.int32, (1, SUBLANES, 1), 1)
    per_group = SUBLANES // nb

    def group(i, carry):
        xr, xi = carry
        r0 = pl.multiple_of(i * SUBLANES, SUBLANES)
        br = sr_ref[:, pl.ds(r0, SUBLANES), :]
        bi = si_ref[:, pl.ds(r0, SUBLANES), :]
        out_r, out_i = br, bi
        for k in range(per_group):
            nr = lam_r * xr - lam_i * xi + br
            ni = lam_r * xi + lam_i * xr + bi
            here = (row8 >= k * nb) & (row8 < (k + 1) * nb)
            out_r = jnp.where(here, nr, out_r)
            out_i = jnp.where(here, ni, out_i)
            if per_group > 1:
                xr = pltpu.roll(nr, nb, 1)
                xi = pltpu.roll(ni, nb, 1)
            else:
                xr, xi = nr, ni
        sr_ref[:, pl.ds(r0, SUBLANES), :] = out_r
        si_ref[:, pl.ds(r0, SUBLANES), :] = out_i
        return xr, xi

    xr, xi = lax.fori_loop(0, steps * nb // SUBLANES, group, (_lane_tiles(xr_ref[...]), _lane_tiles(xi_ref[...])),
                           unroll=4)
    xr_ref[...] = _lane_untile(xr)
    xi_ref[...] = _lane_untile(xi)
    per_out = S5_OUT_CH // LANES
    for j in range(S5_OUT_TILES):
        lhs_r = jnp.concatenate([sr_ref[j * per_out + k] for k in range(per_out)], axis=1).astype(BF16)
        lhs_i = jnp.concatenate([si_ref[j * per_out + k] for k in range(per_out)], axis=1).astype(BF16)
        y = _dot(lhs_r, wc_ref[0, j]) + _dot(lhs_i, wc_ref[1, j])
        for k in range(256 // LANES):
            y_ref[j * (256 // LANES) + k] = y[:, k * LANES:(k + 1) * LANES]
    for b in range(nb):
        rows = pl.ds(b, lc, stride=nb)
        y = jnp.concatenate([y_ref.at[k][rows, :] for k in range(D_S5 // LANES)], axis=1) + d_ref[...] * u_ref[b]
        zz = jax.nn.gelu(y)
        o_ref[b] = zz * jax.nn.sigmoid(_dot(zz.astype(BF16), gw_ref[...]) + gb_ref[...])


def _s5(z3, wb, wc, lam_r, lam_i, d, glu_w, glu_b, x0r, x0i, *, lc, steps):
    b, tz, _ = z3.shape
    lc = min(lc, tz)
    steps = min(steps, lc)
    assert steps == lc or tz == lc
    full = lambda a: pl.BlockSpec(a.shape, lambda i: (0,) * a.ndim)
    st = pl.BlockSpec((SUBLANES, S5_CH), lambda i: (0, 0))
    return pl.pallas_call(
        functools.partial(_s5_kernel, steps=steps),
        grid=(tz // lc,),
        in_specs=[pl.BlockSpec((b, lc, D_S5), lambda i: (0, i, C_SU // D_S5)),
                  full(wb), full(wc), full(lam_r), full(lam_i), full(d), full(glu_w), full(glu_b), st, st],
        out_specs=[pl.BlockSpec((b, lc, D_S5), lambda i: (0, i, 0)), st, st],
        out_shape=[jax.ShapeDtypeStruct((b, tz, D_S5), F32),
                   jax.ShapeDtypeStruct((SUBLANES, S5_CH), F32),
                   jax.ShapeDtypeStruct((SUBLANES, S5_CH), F32)],
        scratch_shapes=[pltpu.VMEM((S5_LT, lc * b, LANES), F32), pltpu.VMEM((S5_LT, lc * b, LANES), F32),
                        pltpu.VMEM((D_S5 // LANES, lc * b, LANES), F32)],
        compiler_params=_cparams("arbitrary"),
        name="s5",
    )(z3, wb, wc, lam_r, lam_i, d, glu_w, glu_b, x0r, x0i)


N_CMB = 2 * KV_HEADS


def _cmp_project(rows_of, w_ref, o_ref):
    nch = o_ref.shape[0]
    for cmb in range(N_CMB):
        slot = cmb // KV_HEADS
        rows_ref = rows_of(cmb)
        acc = jnp.zeros((nch, 2 * HEAD_DIM), F32)
        for l in range(CMP_STRIDE):
            x = rows_ref[pl.ds(l, nch, stride=CMP_STRIDE), :]
            acc += _dot(x.astype(BF16), w_ref[slot, l])
        o_ref[:, cmb * 2 * HEAD_DIM:(cmb + 1) * 2 * HEAD_DIM] = acc


def _cmp_p_kernel(r0_ref, r1_ref, r2_ref, r3_ref, w_ref, o_ref):
    rows = (r0_ref, r1_ref, r2_ref, r3_ref)
    _cmp_project(lambda cmb: rows[cmb], w_ref, o_ref)


def _cmp_project_prompt(z, wcat):
    b, t, _ = z.shape
    nch = t // CMP_STRIDE
    rows = lambda cmb: pl.BlockSpec((None, t, HEAD_DIM), lambda bi: (bi, 0, C_NKV // HEAD_DIM + cmb))
    return pl.pallas_call(
        _cmp_p_kernel,
        grid=(b,),
        in_specs=[rows(cmb) for cmb in range(N_CMB)] + [pl.BlockSpec(wcat.shape, lambda bi: (0, 0, 0, 0))],
        out_specs=pl.BlockSpec((None, nch, N_CMB * 2 * HEAD_DIM), lambda bi: (bi, 0, 0)),
        out_shape=jax.ShapeDtypeStruct((b, nch, N_CMB * 2 * HEAD_DIM), F32),
        compiler_params=_cparams("parallel"),
        name="cmp_project_prompt",
    )(z, z, z, z, wcat)


def _cmp_s_kernel(pt_ref, *refs):
    del pt_ref
    per_pos = 2 * KV_HEADS
    page_refs = [r.reshape(PAGE_SIZE * per_pos, HEAD_DIM) for r in refs[:CMP_PAGES]]
    w_ref, o_ref = refs[CMP_PAGES:]
    per_page = PAGE_SIZE // CMP_STRIDE
    for cmb in range(N_CMB):
        acc = jnp.zeros((o_ref.shape[0], 2 * HEAD_DIM), F32)
        for l in range(CMP_STRIDE):
            x = jnp.concatenate(
                [page_refs[p][pl.ds(l * per_pos + cmb, per_page, stride=CMP_STRIDE * per_pos), :]
                 for p in range(CMP_PAGES)], axis=0)
            acc += _dot(x.astype(BF16), w_ref[cmb // KV_HEADS, l])
        o_ref[:, cmb * 2 * HEAD_DIM:(cmb + 1) * 2 * HEAD_DIM] = acc


def _page_spec(layer, slot, p, pages_per_step, grid_rank):
    def index_map(*idx):
        bi, si, pt = idx[0], idx[grid_rank - 1], idx[grid_rank]
        return (layer, pt[bi, si * pages_per_step + p], 0, slot, 0, 0)

    return pl.BlockSpec((None, None, PAGE_SIZE, None, KV_HEADS, HEAD_DIM), index_map)


def _cmp_project_sample(cache6, page_table, layer, wcat):
    b, npages = page_table.shape
    steps = npages // CMP_PAGES
    nch = CMP_PAGES * PAGE_SIZE // CMP_STRIDE
    def page_pair(p):
        return pl.BlockSpec((None, None, PAGE_SIZE, 2, KV_HEADS, HEAD_DIM),
                            lambda bi, si, pt: (layer, pt[bi, si * CMP_PAGES + p], 0, 0, 0, 0))

    pages = [page_pair(p) for p in range(CMP_PAGES)]

    grid_spec = pltpu.PrefetchScalarGridSpec(
        num_scalar_prefetch=1,
        grid=(b, steps),
        in_specs=pages + [pl.BlockSpec(wcat.shape, lambda bi, si, pt: (0, 0, 0, 0))],
        out_specs=pl.BlockSpec((None, nch, N_CMB * 2 * HEAD_DIM), lambda bi, si, pt: (bi, si, 0)),
    )
    return pl.pallas_call(
        _cmp_s_kernel,
        grid_spec=grid_spec,
        out_shape=jax.ShapeDtypeStruct((b, steps * nch, N_CMB * 2 * HEAD_DIM), F32),
        compiler_params=_cparams("parallel", "arbitrary"),
        name="cmp_project_sample",
    )(page_table, *([cache6] * CMP_PAGES), wcat)


def _cmp_fin_kernel(p_ref, b1_ref, w2_ref, o_ref):
    nch = p_ref.shape[0]
    for cmb in range(N_CMB):
        slot = cmb // KV_HEADS
        c0 = cmb * 2 * HEAD_DIM
        first = p_ref[:, c0:c0 + HEAD_DIM]
        second = pltpu.roll(p_ref[:, c0 + HEAD_DIM:c0 + 2 * HEAD_DIM], nch - 1, 0)
        hid = b1_ref[slot:slot + 1, :] + first + second
        o_ref[cmb] = _dot(jax.nn.gelu(hid).astype(BF16), w2_ref[slot])


def _cmp_finish(p, b1, w2):
    b, nch, _ = p.shape
    return pl.pallas_call(
        _cmp_fin_kernel,
        grid=(b,),
        in_specs=[pl.BlockSpec((None, nch, p.shape[2]), lambda bi: (bi, 0, 0)),
                  pl.BlockSpec(b1.shape, lambda bi: (0, 0)),
                  pl.BlockSpec(w2.shape, lambda bi: (0, 0, 0))],
        out_specs=pl.BlockSpec((None, N_CMB, nch, HEAD_DIM), lambda bi: (bi, 0, 0, 0)),
        out_shape=jax.ShapeDtypeStruct((b, N_CMB, nch, HEAD_DIM), F32),
        compiler_params=_cparams("parallel"),
        name="cmp_finish",
    )(p, b1, w2)


def _softmax_init(m_scr, l_scr, acc_scr):
    m_scr[...] = jnp.full(m_scr.shape, M_INIT, F32)
    l_scr[...] = jnp.zeros(l_scr.shape, F32)
    acc_scr[...] = jnp.zeros(acc_scr.shape, F32)


def _softmax_update(s, mask, vb, m_scr, l_scr, acc_scr):
    s = jnp.where(mask, s, NEG_INF)
    m_prev = m_scr[...]
    m_new = jnp.maximum(m_prev, jnp.max(s, axis=1, keepdims=True))
    alpha = jnp.exp(m_prev - m_new)
    p = jnp.exp(s - m_new)
    l_scr[...] = alpha * l_scr[...] + jnp.sum(p, axis=1, keepdims=True)
    acc_scr[...] = alpha * acc_scr[...] + _dot(p.astype(BF16), vb)
    m_scr[...] = m_new


def _softmax_result(l_scr, acc_scr):
    l = l_scr[...]
    return acc_scr[...] / jnp.where(l > 0, l, 1.0)


def _masked_probs(s, mask):
    s = jnp.where(mask, s, NEG_INF)
    m = jnp.max(s, axis=1, keepdims=True)
    m = jnp.where(m > NEG_INF, m, 0.0)
    p = jnp.exp(s - m)
    den = jnp.sum(p, axis=1, keepdims=True)
    return p / jnp.where(den > 0, den, 1.0)


def _stack_heads(q):
    return jnp.concatenate([q[:, j * HEAD_DIM:(j + 1) * HEAD_DIM] for j in range(GROUP)], axis=0)


def _fold_lanes(x, op):
    out = x[:, 0:LANES]
    for k in range(1, x.shape[1] // LANES):
        out = op(out, x[:, k * LANES:(k + 1) * LANES])
    return out


def _block_importance(pc, n_cmp, n_sel):
    ncp = pc.shape[1]
    nsp = -(-n_sel // LANES) * LANES
    c_start = lax.broadcasted_iota(jnp.int32, (ncp, nsp), 0) * CMP_STRIDE
    s_start = lax.broadcasted_iota(jnp.int32, (ncp, nsp), 1) * SEL_BLOCK
    overlap = ((c_start < s_start + SEL_BLOCK) & (c_start + CMP_BLOCK > s_start)
               & (c_start < n_cmp * CMP_STRIDE)).astype(F32)
    return jnp.dot(pc, overlap, preferred_element_type=F32, precision=lax.Precision.HIGHEST)


def _force_blocks(imp, blk, cur):
    forced = (blk == 0) | (blk == cur) | (blk == cur - 1)
    imp = jnp.where(forced, FORCE_SCORE, imp)
    return jnp.where(blk > cur, -1.0, imp)


def _select_blocks_cols(pc, cur, n_cmp, n_sel):
    imp = _block_importance(pc, n_cmp, n_sel)
    blk = lax.broadcasted_iota(jnp.int32, (1, imp.shape[1]), 1)
    imp = _force_blocks(imp, blk, cur)

    def body(sp, count):
        col = jnp.sum(jnp.where(blk == sp, imp, 0.0), axis=1, keepdims=True)
        ahead = (col > imp) | ((col == imp) & (sp < blk))
        return count + ahead.astype(F32)

    count = lax.fori_loop(0, n_sel, body, jnp.zeros(imp.shape, F32), unroll=True)
    return ((count < min(SEL_TOPK, n_sel)) & (blk < n_sel)).astype(F32)


def _select_blocks_rows(pc, cur_row, n_cmp, n_sel):
    rows = pc.shape[0]
    nrow = -(-n_sel // SUBLANES) * SUBLANES
    imp_t = _block_importance(pc, n_cmp, n_sel).T[0:nrow, :]
    blk = lax.broadcasted_iota(jnp.int32, (nrow, 1), 0)
    imp_t = _force_blocks(imp_t, blk, cur_row)
    count = jnp.zeros(imp_t.shape, F32)
    for sp in range(n_sel):
        row = imp_t[sp:sp + 1, :]
        count += ((row > imp_t) | ((row == imp_t) & (sp < blk))).astype(F32)
    sel_t = ((count < min(SEL_TOPK, n_sel)) & (blk < n_sel)).astype(F32)
    sel_t = jnp.concatenate([sel_t, jnp.zeros((LANES - nrow, rows), F32)], axis=0)
    return sel_t.T


def _gate(gt, col):
    lane = lax.broadcasted_iota(jnp.int32, (1, LANES), 1)
    return jax.nn.sigmoid(jnp.sum(jnp.where(lane == col, gt, 0.0), axis=1, keepdims=True))


def _nsa_prompt_kernel(q_ref, ks_ref, vs_ref, kw_ref, vw_ref, kc_ref, vc_ref, bc_ref, tz_ref, gt_ref, gb_ref,
                       o_ref, s_scr, selb_scr, mx_scr, l_scr, acc_scr, *, t_len):
    g = pl.program_id(1)
    qt = pl.program_id(2)
    q0 = qt * TQ
    n_cmp = t_len // CMP_STRIDE - (CMP_BLOCK // CMP_STRIDE) + 1
    n_sel = t_len // SEL_BLOCK
    ncp = kc_ref.shape[0]
    rows4 = GROUP * TQ
    qs = _stack_heads(q_ref[...] * QK_SCALE).astype(BF16)
    qpos = q0 + (lax.broadcasted_iota(jnp.int32, (rows4, 1), 0) & (TQ - 1))

    n_idx = lax.broadcasted_iota(jnp.int32, (1, ncp), 1)
    s_c = _dot_nt(qs, kc_ref[...].astype(BF16)) + bc_ref[...].reshape(rows4, ncp)
    cmask = (qpos - (n_idx * CMP_STRIDE + CMP_BLOCK - 1) >= 0) & (n_idx < n_cmp)
    p_c = _masked_probs(s_c, cmask)
    o_c = _dot(p_c.astype(BF16), vc_ref[...].astype(BF16))
    pc = p_c[0:TQ] + p_c[TQ:2 * TQ] + p_c[2 * TQ:3 * TQ] + p_c[3 * TQ:4 * TQ]
    cur_row = jnp.right_shift(q0 + lax.broadcasted_iota(jnp.int32, (1, TQ), 1), SEL_SHIFT)
    sel = _select_blocks_rows(pc, cur_row, n_cmp, n_sel)
    sel_add = ((sel - 1.0) * -MASKED).astype(BF16)

    kt_hi = (q0 + TQ - 1) // TK + 1
    kt_far = jnp.maximum(q0 - (FAR_DIST - 1), 0) // TK
    kidx = lax.broadcasted_iota(jnp.int32, (1, TK), 1)
    srow = lax.broadcasted_iota(jnp.int32, (LANES, TK), 0)

    def spread_selection(kt, carry):
        k0 = kt * TK
        expand = (jnp.right_shift(k0 + lax.broadcasted_iota(jnp.int32, (LANES, TK), 1), SEL_SHIFT) == srow).astype(BF16)
        selb_scr[kt] = _dot(sel_add, expand)
        return carry

    lax.fori_loop(0, kt_hi, spread_selection, 0)

    def tile_bias(k0):
        return tz_ref[jnp.clip((q0 - k0) // TQ, 0, N_TZ - 1)].reshape(rows4, TK)

    def attend(k_ref, v_ref, kt_lo, mask_tile):
        mx_scr[...] = jnp.full(mx_scr.shape, NEG_INF, F32)

        def scores(near):
            def body(kt, carry):
                k0 = pl.multiple_of(kt * TK, TK)
                s = _dot_nt(qs, k_ref[pl.ds(k0, TK), :].astype(BF16))
                if near:
                    s = s + tile_bias(k0)
                s = mask_tile(k0, s, near)
                s_scr[kt] = s
                mx_scr[...] = jnp.maximum(mx_scr[...], _fold_lanes(s, jnp.maximum))
                return carry
            return body

        kt_mid = jnp.maximum(kt_far, kt_lo)
        lax.fori_loop(kt_lo, kt_mid, scores(False), 0)
        lax.fori_loop(kt_mid, kt_hi, scores(True), 0)
        m = jnp.max(mx_scr[...], axis=1, keepdims=True)
        mx_scr[...] = jnp.broadcast_to(jnp.where(m > NEG_INF, m, 0.0), mx_scr.shape)
        l_scr[...] = jnp.zeros(l_scr.shape, F32)
        acc_scr[...] = jnp.zeros(acc_scr.shape, F32)

        def probs(kt, carry):
            k0 = pl.multiple_of(kt * TK, TK)
            m_rep = mx_scr[...]
            p = jnp.exp(s_scr[kt] - jnp.concatenate([m_rep] * (TK // LANES), axis=1))
            l_scr[...] += _fold_lanes(p, jnp.add)
            acc_scr[...] += _dot(p.astype(BF16), v_ref[pl.ds(k0, TK), :].astype(BF16))
            return carry

        lax.fori_loop(kt_lo, kt_hi, probs, 0)
        l = jnp.sum(l_scr[...], axis=1, keepdims=True)
        return acc_scr[...] / jnp.where(l > 0, l, 1.0)

    def sel_mask(k0, s, near):
        s = (s.reshape(GROUP, TQ, TK) + selb_scr[k0 // TK][None]).reshape(rows4, TK)
        return jnp.where(k0 + kidx <= qpos, s, NEG_INF) if near else s

    o_s = attend(ks_ref, vs_ref, 0, sel_mask)

    def win_mask(k0, s, near):
        dist = qpos - (k0 + kidx)
        return jnp.where((dist >= 0) & (dist <= WINDOW), s, NEG_INF)

    o_w = attend(kw_ref, vw_ref, jnp.maximum(q0 - WINDOW, 0) // TK, win_mask)

    gt = gt_ref[...] + gb_ref[...]
    for j in range(GROUP):
        col = GATE_NG + (g * GROUP + j) * 3
        r = slice(j * TQ, (j + 1) * TQ)
        o_ref[:, j * HEAD_DIM:(j + 1) * HEAD_DIM] = (
            _gate(gt, col) * o_c[r] + _gate(gt, col + 1) * o_s[r] + _gate(gt, col + 2) * o_w[r])


def _nsa_prompt(z, kcvc, bias_c, tz, gate_b):
    b, t, _ = z.shape
    ncp = kcvc.shape[2]
    nq = t // TQ
    kv = lambda slot: pl.BlockSpec((None, t, HEAD_DIM), lambda bi, g, qi: (bi, 0, C_NKV // HEAD_DIM + slot * KV_HEADS + g),
                                   pipeline_mode=pl.Buffered(1))
    cmp_blk = lambda slot: pl.BlockSpec((None, None, ncp, HEAD_DIM), lambda bi, g, qi: (bi, slot * KV_HEADS + g, 0, 0))
    return pl.pallas_call(
        functools.partial(_nsa_prompt_kernel, t_len=t),
        grid=(b, KV_HEADS, nq),
        in_specs=[pl.BlockSpec((None, TQ, GROUP * HEAD_DIM), lambda bi, g, qi: (bi, qi, C_NQ // (GROUP * HEAD_DIM) + g)),
                  kv(2), kv(3), kv(4), kv(5), cmp_blk(0), cmp_blk(1),
                  pl.BlockSpec((GROUP, TQ, ncp), lambda bi, g, qi: (g, qi, 0)),
                  pl.BlockSpec((N_TZ, GROUP, TQ, TK), lambda bi, g, qi: (0, g, 0, 0), pipeline_mode=pl.Buffered(1)),
                  pl.BlockSpec((None, TQ, LANES), lambda bi, g, qi: (bi, qi, C_GATE // LANES)),
                  pl.BlockSpec((1, LANES), lambda bi, g, qi: (0, 0))],
        out_specs=pl.BlockSpec((None, TQ, GROUP * HEAD_DIM), lambda bi, g, qi: (bi, qi, g)),
        out_shape=jax.ShapeDtypeStruct((b, t, D_NSA), F32),
        scratch_shapes=[pltpu.VMEM((t // TK, GROUP * TQ, TK), F32), pltpu.VMEM((t // TK, TQ, TK), F32),
                        pltpu.VMEM((GROUP * TQ, LANES), F32),
                        pltpu.VMEM((GROUP * TQ, LANES), F32), pltpu.VMEM((GROUP * TQ, HEAD_DIM), F32)],
        compiler_params=_cparams("parallel", "parallel", "arbitrary"),
        name="nsa_prompt",
    )(z, z, z, z, z, kcvc, kcvc, bias_c, tz, z, gate_b)


TS = SUBLANES
ROWS_S = GROUP * TS


def _nsa_s_cmp_kernel(q_ref, kc_ref, vc_ref, bc_ref, oc_ref, sel_ref, *, pos0, n_cmp, n_sel):
    nch = kc_ref.shape[0]
    qs = _stack_heads(q_ref[...] * QK_SCALE).astype(BF16)
    qpos = pos0 + (lax.broadcasted_iota(jnp.int32, (ROWS_S, 1), 0) & (TS - 1))
    n_idx = lax.broadcasted_iota(jnp.int32, (1, nch), 1)
    s_c = _dot_nt(qs, kc_ref[...].astype(BF16)) + bc_ref[...].reshape(ROWS_S, nch)
    cmask = (qpos - (n_idx * CMP_STRIDE + CMP_BLOCK - 1) >= 0) & (n_idx < n_cmp)
    p_c = _masked_probs(s_c, cmask)
    oc_ref[...] = _dot(p_c.astype(BF16), vc_ref[...].astype(BF16))
    pc = p_c[0:TS] + p_c[TS:2 * TS] + p_c[2 * TS:3 * TS] + p_c[3 * TS:4 * TS]
    sel_ref[...] = _select_blocks_cols(pc, jnp.right_shift(qpos[0:TS], SEL_SHIFT), n_cmp, n_sel)


def _nsa_sample_cmp(z8, kcvc, bias_c, *, pos0, n_cmp, n_sel):
    b = z8.shape[0]
    nch = kcvc.shape[2]
    nsp = -(-n_sel // LANES) * LANES
    cmp_blk = lambda slot: pl.BlockSpec((None, None, nch, HEAD_DIM), lambda bi, g: (bi, slot * KV_HEADS + g, 0, 0))
    return pl.pallas_call(
        functools.partial(_nsa_s_cmp_kernel, pos0=pos0, n_cmp=n_cmp, n_sel=n_sel),
        grid=(b, KV_HEADS),
        in_specs=[pl.BlockSpec((None, TS, GROUP * HEAD_DIM), lambda bi, g: (bi, 0, C_NQ // (GROUP * HEAD_DIM) + g)),
                  cmp_blk(0), cmp_blk(1),
                  pl.BlockSpec((GROUP, TS, nch), lambda bi, g: (g, 0, 0))],
        out_specs=[pl.BlockSpec((None, None, ROWS_S, HEAD_DIM), lambda bi, g: (bi, g, 0, 0)),
                   pl.BlockSpec((None, None, TS, nsp), lambda bi, g: (bi, g, 0, 0))],
        out_shape=[jax.ShapeDtypeStruct((b, KV_HEADS, ROWS_S, HEAD_DIM), F32),
                   jax.ShapeDtypeStruct((b, KV_HEADS, TS, nsp), F32)],
        compiler_params=_cparams("parallel", "parallel"),
        name="nsa_sample_cmp",
    )(z8, kcvc, kcvc, bias_c)


def _pad_rows(x, n):
    return jnp.concatenate([x, jnp.zeros((n - x.shape[0], x.shape[1]), x.dtype)], axis=0)


def _nsa_s_sel_kernel(pt_ref, q_ref, sel_ref, kn_ref, vn_ref, bt_ref, bn_ref, bf_ref, *refs, pos0, npages):
    del pt_ref
    rows_pp = PAGE_SIZE * KV_HEADS
    k_refs = [r.reshape(rows_pp, HEAD_DIM) for r in refs[:SEL_PAGES]]
    v_refs = [r.reshape(rows_pp, HEAD_DIM) for r in refs[SEL_PAGES:2 * SEL_PAGES]]
    o_ref, kbuf, vbuf, m_scr, l_scr, acc_scr = refs[2 * SEL_PAGES:]
    step = pl.program_id(1)
    last = step == npages // SEL_PAGES - 1
    rows = KV_HEADS * ROWS_S
    ncol = SEL_TILE * rows_pp
    head_shift = KV_HEADS.bit_length() - 1
    blk_per_page = PAGE_SIZE // SEL_BLOCK

    @pl.when(step == 0)
    def _():
        _softmax_init(m_scr, l_scr, acc_scr)

    nsp = sel_ref.shape[-1]
    ridx = lax.broadcasted_iota(jnp.int32, (rows, 1), 0)
    qpos = pos0 + (ridx & (TS - 1))
    qs = jnp.concatenate([_stack_heads(q_ref[:, g * GROUP * HEAD_DIM:(g + 1) * GROUP * HEAD_DIM] * QK_SCALE)
                          for g in range(KV_HEADS)], axis=0).astype(BF16)
    sel_rows = jnp.concatenate([sel_ref[g] for g in range(KV_HEADS) for _ in range(GROUP)], axis=0)
    bias_far = bf_ref[...].reshape(rows, PAGE_SIZE)[:, 0:1]
    col = lax.broadcasted_iota(jnp.int32, (1, ncol), 1)
    own_head = (col & (KV_HEADS - 1)) == ridx // ROWS_S
    expand = (jnp.right_shift(lax.broadcasted_iota(jnp.int32, (LANES, ncol), 1), SEL_SHIFT + head_shift)
              == lax.broadcasted_iota(jnp.int32, (LANES, ncol), 0)).astype(BF16)
    blk_r = lax.broadcasted_iota(jnp.int32, (nsp, LANES), 0)
    blk_c = lax.broadcasted_iota(jnp.int32, (nsp, LANES), 1)
    n_tiles = SEL_PAGES // SEL_TILE

    for tile in range(n_tiles):
        for i in range(SEL_TILE):
            p = tile * SEL_TILE + i
            kbuf[i * rows_pp:(i + 1) * rows_pp, :] = k_refs[p][...].astype(BF16)
            vbuf[i * rows_pp:(i + 1) * rows_pp, :] = v_refs[p][...].astype(BF16)
        page0 = step * SEL_PAGES + tile * SEL_TILE
        window = ((blk_r == page0 * blk_per_page + blk_c) & (blk_c < SEL_TILE * blk_per_page)).astype(BF16)
        chosen = _dot(_dot(sel_rows.astype(BF16), window).astype(BF16), expand) > 0.5
        s = _dot_nt(qs, kbuf[...]) + bias_far
        if tile == n_tiles - 1:
            tail = jnp.where(last, bt_ref[...].reshape(rows, rows_pp) - bias_far, 0.0)
            s = s + jnp.concatenate([jnp.zeros((rows, ncol - rows_pp), F32), tail], axis=1)
        key = page0 * PAGE_SIZE + jnp.right_shift(col, head_shift)
        _softmax_update(s, chosen & own_head & (key <= qpos), vbuf[...], m_scr, l_scr, acc_scr)

    @pl.when(last)
    def _():
        kidx = lax.broadcasted_iota(jnp.int32, (1, PAGE_SIZE), 1)
        new_blk = lax.broadcasted_iota(jnp.int32, (1, nsp), 1) == npages * blk_per_page
        flag = jnp.sum(jnp.where(new_blk, sel_rows, 0.0), axis=1, keepdims=True) > 0.5
        bias_new = bn_ref[...].reshape(rows, PAGE_SIZE)
        for g in range(KV_HEADS):
            r = slice(g * ROWS_S, (g + 1) * ROWS_S)
            kn = _pad_rows(kn_ref[:, g * HEAD_DIM:(g + 1) * HEAD_DIM], PAGE_SIZE).astype(BF16)
            vn = _pad_rows(vn_ref[:, g * HEAD_DIM:(g + 1) * HEAD_DIM], PAGE_SIZE).astype(BF16)
            mask = flag[r] & (pos0 + kidx <= qpos[r]) & (kidx < TS)
            _softmax_update(_dot_nt(qs[r], kn) + bias_new[r], mask, vn, m_scr.at[r], l_scr.at[r], acc_scr.at[r])
            o_ref[g] = _softmax_result(l_scr.at[r], acc_scr.at[r])


def _nsa_sample_sel(z8, sel, bias_tail, bias_new, bias_far, cache6, page_table, layer, *, pos0):
    b, npages = page_table.shape
    nsp = sel.shape[-1]
    steps = npages // SEL_PAGES
    slot_cols = KV_HEADS * HEAD_DIM
    new_rows = lambda slot: pl.BlockSpec((None, TS, slot_cols),
                                         lambda bi, si, pt: (bi, 0, (C_NKV + slot * slot_cols) // slot_cols))
    whole = lambda a: pl.BlockSpec(a.shape, lambda bi, si, pt: (0,) * a.ndim)
    grid_spec = pltpu.PrefetchScalarGridSpec(
        num_scalar_prefetch=1,
        grid=(b, steps),
        in_specs=[pl.BlockSpec((None, TS, D_NSA), lambda bi, si, pt: (bi, 0, C_NQ // D_NSA)),
                  pl.BlockSpec((None, KV_HEADS, TS, nsp), lambda bi, si, pt: (bi, 0, 0, 0)),
                  new_rows(2), new_rows(3), whole(bias_tail), whole(bias_new), whole(bias_far)]
                 + [_page_spec(layer, 2, p, SEL_PAGES, 2) for p in range(SEL_PAGES)]
                 + [_page_spec(layer, 3, p, SEL_PAGES, 2) for p in range(SEL_PAGES)],
        out_specs=pl.BlockSpec((None, KV_HEADS, ROWS_S, HEAD_DIM), lambda bi, si, pt: (bi, 0, 0, 0)),
        scratch_shapes=[pltpu.VMEM((SEL_TILE * PAGE_SIZE * KV_HEADS, HEAD_DIM), BF16),
                        pltpu.VMEM((SEL_TILE * PAGE_SIZE * KV_HEADS, HEAD_DIM), BF16),
                        pltpu.VMEM((KV_HEADS * ROWS_S, 1), F32), pltpu.VMEM((KV_HEADS * ROWS_S, 1), F32),
                        pltpu.VMEM((KV_HEADS * ROWS_S, HEAD_DIM), F32)],
    )
    return pl.pallas_call(
        functools.partial(_nsa_s_sel_kernel, pos0=pos0, npages=npages),
        grid_spec=grid_spec,
        out_shape=jax.ShapeDtypeStruct((b, KV_HEADS, ROWS_S, HEAD_DIM), F32),
        compiler_params=_cparams("parallel", "arbitrary"),
        name="nsa_sample_sel",
    )(page_table, z8, sel, z8, z8, bias_tail, bias_new, bias_far, *([cache6] * (2 * SEL_PAGES)))


def _nsa_s_win_kernel(q_ref, kw_ref, vw_ref, kn_ref, vn_ref, bw_ref, oc_ref, os_ref, gt_ref, gb_ref, o_ref,
                      m_scr, l_scr, acc_scr, *, pos0):
    g = pl.program_id(1)
    wb = kw_ref.shape[0]
    own = pl.ds(g, wb, stride=KV_HEADS)
    kw = kw_ref.reshape(wb * KV_HEADS, HEAD_DIM)[own, :]
    vw = vw_ref.reshape(wb * KV_HEADS, HEAD_DIM)[own, :]
    qs = _stack_heads(q_ref[...] * QK_SCALE).astype(BF16)
    qpos = pos0 + (lax.broadcasted_iota(jnp.int32, (ROWS_S, 1), 0) & (TS - 1))
    bias = bw_ref[...].reshape(ROWS_S, wb + PAGE_SIZE)
    _softmax_init(m_scr, l_scr, acc_scr)
    dist = qpos - (pos0 - wb + lax.broadcasted_iota(jnp.int32, (1, wb), 1))
    s = _dot_nt(qs, kw.astype(BF16)) + bias[:, 0:wb]
    _softmax_update(s, (dist >= 0) & (dist <= WINDOW), vw.astype(BF16), m_scr, l_scr, acc_scr)
    kidx = lax.broadcasted_iota(jnp.int32, (1, PAGE_SIZE), 1)
    dist = qpos - (pos0 + kidx)
    s = _dot_nt(qs, _pad_rows(kn_ref[...], PAGE_SIZE).astype(BF16)) + bias[:, wb:wb + PAGE_SIZE]
    _softmax_update(s, (dist >= 0) & (dist <= WINDOW) & (kidx < TS), _pad_rows(vn_ref[...], PAGE_SIZE).astype(BF16),
                    m_scr, l_scr, acc_scr)
    o_w = _softmax_result(l_scr, acc_scr)
    o_c = oc_ref[...]
    o_s = os_ref[...]
    gt = gt_ref[...] + gb_ref[...]
    for j in range(GROUP):
        col = GATE_NG + (g * GROUP + j) * 3
        r = slice(j * TS, (j + 1) * TS)
        o_ref[:, j * HEAD_DIM:(j + 1) * HEAD_DIM] = (
            _gate(gt, col) * o_c[r] + _gate(gt, col + 1) * o_s[r] + _gate(gt, col + 2) * o_w[r])


def _nsa_sample_win(z8, win6, layer, bias_w, o_c, o_s, gate_b, *, pos0):
    _, b, wb = win6.shape[:3]
    cached = lambda slot: pl.BlockSpec((None, None, wb, None, KV_HEADS, HEAD_DIM),
                                       lambda bi, g: (layer, bi, 0, slot, 0, 0))
    zcol = lambda col: (lambda bi, g: (bi, 0, col // HEAD_DIM + g))
    part = pl.BlockSpec((None, None, ROWS_S, HEAD_DIM), lambda bi, g: (bi, g, 0, 0))
    return pl.pallas_call(
        functools.partial(_nsa_s_win_kernel, pos0=pos0),
        grid=(b, KV_HEADS),
        in_specs=[pl.BlockSpec((None, TS, GROUP * HEAD_DIM), lambda bi, g: (bi, 0, C_NQ // (GROUP * HEAD_DIM) + g)),
                  cached(0), cached(1),
                  pl.BlockSpec((None, TS, HEAD_DIM), zcol(C_NKV + 4 * KV_HEADS * HEAD_DIM)),
                  pl.BlockSpec((None, TS, HEAD_DIM), zcol(C_NKV + 5 * KV_HEADS * HEAD_DIM)),
                  pl.BlockSpec((GROUP, TS, wb + PAGE_SIZE), lambda bi, g: (g, 0, 0)),
                  part, part,
                  pl.BlockSpec((None, TS, LANES), lambda bi, g: (bi, 0, C_GATE // LANES)),
                  pl.BlockSpec((1, LANES), lambda bi, g: (0, 0))],
        out_specs=pl.BlockSpec((None, TS, GROUP * HEAD_DIM), lambda bi, g: (bi, 0, g)),
        out_shape=jax.ShapeDtypeStruct((b, TS, D_NSA), F32),
        scratch_shapes=[pltpu.VMEM((ROWS_S, 1), F32), pltpu.VMEM((ROWS_S, 1), F32), pltpu.VMEM((ROWS_S, HEAD_DIM), F32)],
        compiler_params=_cparams("parallel", "parallel"),
        name="nsa_sample_win",
    )(z8, win6, win6, z8, z8, bias_w, o_c, o_s, z8, gate_b)


def _t5_bucket(dist):
    n = np.maximum(dist, 0)
    exact = NUM_BUCKETS // 2
    nf = np.maximum(n, 1).astype(np.float32)
    large = exact + (np.log(nf / np.float32(exact)) / np.float32(math.log(MAX_DISTANCE / exact))
                     * np.float32(NUM_BUCKETS - exact)).astype(np.int32)
    return np.where(n < exact, n, np.minimum(large, NUM_BUCKETS - 1)).astype(np.int32)


FAR_DIST = int(np.max(np.nonzero(_t5_bucket(np.arange(4 * MAX_DISTANCE)) < NUM_BUCKETS - 1)[0])) + 1
N_TZ = (FAR_DIST + TK - 2) // TQ + 1


def _bias_table(rel_bias, dist):
    onehot = jax.nn.one_hot(jnp.asarray(_t5_bucket(dist).astype(np.int8)), NUM_BUCKETS, dtype=F32)
    return jnp.einsum("rcb,bh->hrc", onehot, rel_bias, precision=lax.Precision.HIGHEST)


def _pack_weights(p):
    w_in = p["w_in"].astype(BF16)
    pad = jnp.zeros(w_in.shape[:2] + (N_IN - C_GATE - 32,), BF16)
    w_in_p = jnp.concatenate([w_in[..., 0:2048], w_in[..., 2056:3080], w_in[..., 3080:4616], w_in[..., 4640:5152],
                              w_in[..., 2048:2056], w_in[..., 4616:4640], pad], axis=2)
    return dict(
        w_in=w_in_p, w_out=p["w_out"].astype(BF16), w_down=p["ffn_w_down"].astype(BF16),
        w_up=p["ffn_w_up"].astype(BF16))


def _pack_layer(p, big, l):
    gb = p["mlstm_gate_b"][l]
    mlstm_gate_b = jnp.zeros((1, LANES), F32).at[0, GATE_MI:GATE_MI + 4].set(gb[0]).at[0, GATE_MF:GATE_MF + 4].set(gb[1])
    nsa_gate_b = jnp.zeros((1, LANES), F32).at[0, GATE_NG:GATE_NG + 3 * NSA_HEADS].set(p["nsa_gate_b"][l].reshape(-1))
    w1 = p["cmp_w1"][l]
    wcat = jnp.concatenate([w1[:, 0:CMP_STRIDE], w1[:, CMP_STRIDE:CMP_BLOCK]], axis=-1).astype(BF16)
    lam = lax.complex(p["s5_a_re"][l], p["s5_a_im"][l])
    lam_bar = jnp.exp(lam * jnp.exp(p["s5_log_step"][l])[:, None])
    b_bar = ((lam_bar - 1.0) / lam)[..., None] * lax.complex(p["s5_b_re"][l], p["s5_b_im"][l])
    gi = S5_GROUPS // S5_IN_TILES
    bd_in = lambda m: jnp.einsum("cgph,gk->cghkp", m.reshape(S5_IN_TILES, gi, S5_STATE, S5_GROUP_WIDTH),
                                 jnp.eye(gi, dtype=F32)).reshape(S5_IN_TILES, S5_IN_FEATS, 256)
    wb = jnp.concatenate([bd_in(b_bar.real), bd_in(b_bar.imag)], axis=2).astype(BF16)
    go = S5_GROUPS // S5_OUT_TILES
    bd_out = lambda m: jnp.einsum("jghp,gk->jgpkh", m.reshape(S5_OUT_TILES, go, S5_GROUP_WIDTH, S5_STATE),
                                  jnp.eye(go, dtype=F32)).reshape(S5_OUT_TILES, S5_OUT_CH, 256)
    wc = jnp.stack([bd_out(p["s5_c_re"][l]), -bd_out(p["s5_c_im"][l])]).astype(BF16)
    half = lambda w: jnp.pad(w, ((0, 0), (0, D_FFP - D_FF)))
    conv_w = p["ffn_conv_w"][l]
    conv_b = p["ffn_conv_b"][l][None, :]
    return dict(
        big, layer=l, mlstm_gate_b=mlstm_gate_b, mlstm_norm_w=p["mlstm_norm_w"][l][None, :], nsa_gate_b=nsa_gate_b,
        wcat=wcat, cmp_b1=p["cmp_b1"][l], cmp_w2=p["cmp_w2"][l].astype(BF16),
        s5_wb=wb, s5_wc=wc, s5_lam_r=lam_bar.real.reshape(1, S5_CH), s5_lam_i=lam_bar.imag.reshape(1, S5_CH),
        s5_d=p["s5_d"][l][None, :], s5_glu_w=p["s5_glu_w"][l].astype(BF16), s5_glu_b=p["s5_glu_b"][l][None, :],
        ln1_w=p["ln1_w"][l][None, :], ln1_b=p["ln1_b"][l][None, :],
        conv_w8=jnp.pad(jnp.concatenate([half(conv_w[:, :D_FF]), half(conv_w[:, D_FF:])], axis=1),
                        ((0, SUBLANES - CONV_W), (0, 0))),
        conv_b=jnp.concatenate([half(conv_b[:, :D_FF]), half(conv_b[:, D_FF:])], axis=1),
        ln2_w=p["ln2_w"][l][None, :], ln2_b=p["ln2_b"][l][None, :],
    )


def _unpad_ff(x):
    return jnp.concatenate([x[..., :D_FF], x[..., D_FFP:D_FFP + D_FF]], axis=-1)


def _pad_ff(x):
    pad = [(0, 0)] * (x.ndim - 1) + [(0, D_FFP - D_FF)]
    return jnp.concatenate([jnp.pad(x[..., :D_FF], pad), jnp.pad(x[..., D_FF:], pad)], axis=-1)


def _mixer_tail(x2d, z, b, t, t_use, lw, o_mlstm, o_nsa, s5_state, conv_state8, tm):
    pad8 = lambda s: jnp.pad(s.reshape(b, S5_CH), ((0, SUBLANES - b), (0, 0)))
    o_s5, xr, xi = _s5(z.reshape(b, t, N_IN), lw["s5_wb"], lw["s5_wc"], lw["s5_lam_r"], lw["s5_lam_i"], lw["s5_d"],
                       lw["s5_glu_w"], lw["s5_glu_b"], pad8(s5_state[0]), pad8(s5_state[1]), lc=S5_LC, steps=t_use)
    o_s5 = o_s5.reshape(b * t, D_S5)
    layer = lw["layer"]
    x1 = _wout_ln(o_mlstm.reshape(b * t, D_MLSTM), o_nsa.reshape(b * t, D_NSA), o_s5, x2d,
                  lw["w_out"], layer, lw["ln1_w"], lw["ln1_b"], tm=512)
    s5_new = (xr[:b].reshape(b, S5_GROUPS, S5_STATE), xi[:b].reshape(b, S5_GROUPS, S5_STATE))
    if conv_state8 is None:
        x2, tail_a, tail_g = _ffn_prompt(x1.reshape(b, t, D_MODEL), lw["w_up"], lw["w_down"], layer, lw["conv_w8"],
                                         lw["conv_b"], lw["ln2_w"], lw["ln2_b"], tt=FFN_TT, tf=FFN_TF)
        keep = slice(SUBLANES - (CONV_W - 1), SUBLANES)
        conv_new = jnp.concatenate([tail_a[:, keep, :D_FF], tail_g[:, keep, :D_FF]], axis=-1)
        return x2.reshape(b * t, D_MODEL), s5_new, conv_new
    up = _matmul(x1, lw["w_up"], layer, tm=tm, tn=FFN_TF, n=2 * D_FFP, w_cols=_up_cols(FFN_TF))
    hgate = _convgate(up.reshape(b, t, 2 * D_FFP), conv_state8, lw["conv_w8"], lw["conv_b"], tf=FFN_TF)
    x2 = _down_ln(hgate.reshape(b * t, D_FFP), lw["w_down"], layer, x1, lw["ln2_w"], lw["ln2_b"], tm=512, tk=1408)
    conv_new = _unpad_ff(up.reshape(b, t, 2 * D_FFP)[:, t_use - (CONV_W - 1):t_use])
    return x2, s5_new, conv_new


def _prompt_layer(x2d, b, t, lw, rel_bias, rows_so_far):
    z, kv_rows, win_rows = _proj_in(x2d, lw["w_in"], lw["layer"], rows_so_far, tm=min(1024, b * t), tn=768)
    z3 = z.reshape(b, t, N_IN)
    zeros = lambda *s: jnp.zeros(s, F32)
    o_mlstm, c1, n1, m1 = _mlstm(z3, lw["mlstm_gate_b"], lw["mlstm_norm_w"],
                                 zeros(b, MLSTM_HEADS, HEAD_DIM, HEAD_DIM), zeros(b, MLSTM_HEADS, LANES),
                                 zeros(b, MLSTM_HEADS, LANES), L=MLSTM_L, lin=MLSTM_L, t_valid=t)
    kcvc = _cmp_finish(_cmp_project_prompt(z3, lw["wcat"]), lw["cmp_b1"], lw["cmp_w2"])
    ncp = t // CMP_STRIDE
    bias_c = _bias_table(rel_bias, np.arange(t)[:, None] - (np.arange(ncp) * CMP_STRIDE + CMP_BLOCK - 1)[None, :])
    ti = np.arange(TQ)[:, None] - np.arange(TK)[None, :]
    far = _bias_table(rel_bias, np.full((1, 1), FAR_DIST))
    tz = jnp.stack([_bias_table(rel_bias, d * TQ + ti) - far for d in range(N_TZ)])
    o_nsa = _nsa_prompt(z3, kcvc, bias_c, tz, lw["nsa_gate_b"])
    x2, s5_new, conv_new = _mixer_tail(
        x2d, z, b, t, t, lw, o_mlstm, o_nsa, (zeros(b, S5_GROUPS, S5_STATE), zeros(b, S5_GROUPS, S5_STATE)),
        None, tm=1024)
    state = (None, None, c1, n1, m1[:, :, 0], s5_new[0], s5_new[1], conv_new)
    return x2, state, (kv_rows, win_rows)


def _sample_layer(x2d, b, tn, lw, rel_bias, layer, cache6, page_table, win6, mlstm_state, s5_state, conv_state):
    npages = page_table.shape[1]
    pos0 = npages * PAGE_SIZE
    z = _matmul(x2d, lw["w_in"], lw["layer"], tm=1024, tn=768)
    z8 = z.reshape(b, TS, N_IN)
    c0, n0, m0 = mlstm_state
    o_mlstm, c1, n1, m1 = _mlstm(z8, lw["mlstm_gate_b"], lw["mlstm_norm_w"], c0, n0,
                                 jnp.broadcast_to(m0[:, :, None], (b, MLSTM_HEADS, LANES)),
                                 L=LANES, lin=TS, t_valid=tn)
    n_chunks = (pos0 + tn) // CMP_STRIDE
    n_cmp = n_chunks - CMP_BLOCK // CMP_STRIDE + 1
    n_sel = -(-(pos0 + tn) // SEL_BLOCK)
    kcvc = _cmp_finish(_cmp_project_sample(cache6, page_table, layer, lw["wcat"]), lw["cmp_b1"], lw["cmp_w2"])
    qpos = pos0 + np.arange(TS)[:, None]
    bias_c = _bias_table(rel_bias, qpos - (np.arange(n_chunks) * CMP_STRIDE + CMP_BLOCK - 1)[None, :])
    o_c, sel = _nsa_sample_cmp(z8, kcvc, bias_c, pos0=pos0, n_cmp=n_cmp, n_sel=n_sel)
    kk = np.arange(PAGE_SIZE)[None, :]
    bias_tail = _bias_table(rel_bias, np.repeat(qpos - (pos0 - PAGE_SIZE + kk), KV_HEADS, axis=1))
    bias_new = _bias_table(rel_bias, qpos - (pos0 + kk))
    bias_far = _bias_table(rel_bias, np.broadcast_to(qpos - (pos0 - 2 * PAGE_SIZE), (TS, PAGE_SIZE)))
    o_s = _nsa_sample_sel(z8, sel, bias_tail, bias_new, bias_far, cache6, page_table, layer, pos0=pos0)
    wb = win6.shape[2]
    wk = np.arange(wb)[None, :]
    bias_w = _bias_table(rel_bias, np.concatenate([qpos - (pos0 - wb + wk), qpos - (pos0 + kk)], axis=1))
    o_nsa = _nsa_sample_win(z8, win6, layer, bias_w, o_c, o_s, lw["nsa_gate_b"], pos0=pos0)
    conv_state8 = jnp.pad(_pad_ff(conv_state), ((0, 0), (SUBLANES - (CONV_W - 1), 0), (0, 0)))
    x2, s5_new, conv_new = _mixer_tail(x2d, z, b, TS, tn, lw, o_mlstm, o_nsa, s5_state, conv_state8, tm=1024)
    nkv = z8[:, :tn, C_NKV:C_NKV + N_KV_SLOTS * KV_HEADS * HEAD_DIM].reshape(b, tn, N_KV_SLOTS, KV_HEADS, HEAD_DIM)
    state = (nkv[:, :, :4], nkv[:, :, 4:], c1, n1, m1[:, :, 0], s5_new[0], s5_new[1], conv_new)
    return x2, state


def kernel(x_prompt, x_sample, cache_nsa_kv, cache_win_kv, state_mlstm_c, state_mlstm_n, state_mlstm_m,
           state_s5_re, state_s5_im, state_ffn_conv, page_table, w_in, mlstm_gate_b, mlstm_norm_w,
           nsa_gate_b, cmp_w1, cmp_b1, cmp_w2, rel_bias, s5_a_re, s5_a_im, s5_b_re, s5_b_im, s5_c_re,
           s5_c_im, s5_d, s5_log_step, s5_glu_w, s5_glu_b, w_out, ln1_w, ln1_b, ffn_w_up, ffn_conv_w,
           ffn_conv_b, ffn_w_down, ln2_w, ln2_b):
    params = dict(w_in=w_in, mlstm_gate_b=mlstm_gate_b, mlstm_norm_w=mlstm_norm_w, nsa_gate_b=nsa_gate_b,
                  cmp_w1=cmp_w1, cmp_b1=cmp_b1, cmp_w2=cmp_w2, s5_a_re=s5_a_re, s5_a_im=s5_a_im, s5_b_re=s5_b_re,
                  s5_b_im=s5_b_im, s5_c_re=s5_c_re, s5_c_im=s5_c_im, s5_d=s5_d, s5_log_step=s5_log_step,
                  s5_glu_w=s5_glu_w, s5_glu_b=s5_glu_b, w_out=w_out, ln1_w=ln1_w, ln1_b=ln1_b, ffn_w_up=ffn_w_up,
                  ffn_conv_w=ffn_conv_w, ffn_conv_b=ffn_conv_b, ffn_w_down=ffn_w_down, ln2_w=ln2_w, ln2_b=ln2_b)
    depth = w_in.shape[0]
    bp, tp, _ = x_prompt.shape
    bs, tn, _ = x_sample.shape
    assert tp % TK == 0 and tn < CMP_STRIDE and tn <= TS and SUBLANES % bp == 0 and SUBLANES % bs == 0
    assert (tn * bs) % SUBLANES == 0 and tp % min(FFN_TT, tp) == 0
    assert page_table.shape[1] % CMP_PAGES == 0 and cache_nsa_kv.shape[2] == PAGE_SIZE
    xp = x_prompt.reshape(bp * tp, D_MODEL)
    xs = jnp.pad(x_sample, ((0, 0), (0, TS - tn), (0, 0))).reshape(bs * TS, D_MODEL)
    p_states, s_states = [], []
    big = _pack_weights(params)
    rows_so_far = None
    for l in range(depth):
        lw = _pack_layer(params, big, l)
        xp, sp, rows_so_far = _prompt_layer(xp, bp, tp, lw, rel_bias, rows_so_far)
        xs, ss = _sample_layer(xs, bs, tn, lw, rel_bias, l, cache_nsa_kv, page_table, cache_win_kv,
                               (state_mlstm_c[l], jnp.pad(state_mlstm_n[l], ((0, 0), (0, 0), (0, LANES - HEAD_DIM))),
                                state_mlstm_m[l]),
                               (state_s5_re[l], state_s5_im[l]), state_ffn_conv[l])
        p_states.append(sp)
        s_states.append(ss)
    stk = lambda states, i: jnp.stack([s[i] for s in states])
    y_prompt = xp.reshape(bp, tp, D_MODEL)
    y_sample = xs.reshape(bs, TS, D_MODEL)[:, :tn]
    wrows = min(WINDOW, tp)
    nsa_kv_p = rows_so_far[0].reshape(depth, bp, tp, 4, KV_HEADS, HEAD_DIM)
    win_kv_p = rows_so_far[1].reshape(depth, bp, tp, 2, KV_HEADS, HEAD_DIM)[:, :, tp - wrows:]
    return (y_prompt, y_sample,
            nsa_kv_p, stk(s_states, 0), win_kv_p, stk(s_states, 1),
            stk(p_states, 2), stk(s_states, 2), stk(p_states, 3), stk(s_states, 3), stk(p_states, 4), stk(s_states, 4),
            stk(p_states, 5), stk(s_states, 5), stk(p_states, 6), stk(s_states, 6), stk(p_states, 7), stk(s_states, 7))
```

```python
import functools
import math

import jax
import jax.numpy as jnp
import numpy as np
from jax import lax
from jax.experimental import pallas as pl
from jax.experimental.pallas import tpu as pltpu

F32 = jnp.float32
BF16 = jnp.bfloat16
NEG_INF = float("-inf")
M_INIT = -1e30
MASKED = -1e30

D_MODEL = 2048
PAGE_SIZE = 128
D_MLSTM = D_MODEL // 4
D_NSA = D_MODEL // 2
D_S5 = D_MODEL - D_MLSTM - D_NSA
HEAD_DIM = 128
MLSTM_HEADS = D_MLSTM // HEAD_DIM
NSA_HEADS = D_NSA // HEAD_DIM
KV_HEADS = 2
GROUP = NSA_HEADS // KV_HEADS
N_KV_SLOTS = 6
CMP_BLOCK = 32
CMP_STRIDE = 16
SEL_BLOCK = 64
SEL_TOPK = 16
FORCE_SCORE = 1e4
WINDOW = 512
S5_GROUP_WIDTH = 16
S5_GROUPS = D_S5 // S5_GROUP_WIDTH
S5_STATE = 64
S5_CH = S5_GROUPS * S5_STATE
NUM_BUCKETS = 32
MAX_DISTANCE = 128
D_FF = ((8 * D_MODEL // 3 + 127) // 128) * 128
CONV_W = 3
LN_EPS = 1e-5
DEPTH = 2
DEEPNORM_ALPHA = (2 * DEPTH) ** 0.25
QK_SCALE = HEAD_DIM ** -0.5

LANES = 128
SUBLANES = 8
V7X_VMEM_LIMIT = 56 * 2 ** 20

C_MQ, C_MK, C_MV, C_MO = 0, 512, 1024, 1536
C_NQ = 2048
C_NKV = 3072
C_SU = 4608
C_GATE = 5120
N_IN = 5376
GATE_MI, GATE_MF, GATE_NG = 0, 4, 8
D_FFP = 5632
MLSTM_L = 512
TQ = 512
TK = 512
S5_LC = 256
FFN_TT = 1024
FFN_TF = 512
FFN_HALO = 16
CMP_PAGES = 32
SEL_PAGES = 32
SEL_TILE = 32
SEL_SHIFT = SEL_BLOCK.bit_length() - 1


def _cparams(*sem):
    return pltpu.CompilerParams(dimension_semantics=sem, vmem_limit_bytes=V7X_VMEM_LIMIT)


def _dot(a, b):
    return jnp.dot(a, b, preferred_element_type=F32)


def _dot_nt(a, b):
    return lax.dot_general(a, b, (((1,), (1,)), ((), ())), preferred_element_type=F32)


def _layer_norm(y, w, b):
    mu = jnp.mean(y, axis=-1, keepdims=True)
    d = y - mu
    var = jnp.mean(d * d, axis=-1, keepdims=True)
    return d * lax.rsqrt(var + LN_EPS) * w + b


def _mm_kernel(x_ref, w_ref, o_ref):
    o_ref[...] = _dot(x_ref[...].astype(BF16), w_ref[...]).astype(o_ref.dtype)


FF_OVER = D_FFP - D_FF


def _ff_start(f, tf):
    nf, per = D_FFP // tf, tf // LANES
    return (f * per - (f == nf - 1).astype(jnp.int32) * (FF_OVER // LANES)) * LANES


def _up_cols(tf):
    nf, per = D_FFP // tf, tf // LANES
    return lambda j: (j * per + (j >= nf).astype(jnp.int32) * (D_FF // LANES - nf * per)
                      - ((j == nf - 1) | (j == 2 * nf - 1)).astype(jnp.int32) * (FF_OVER // LANES)) * LANES


def _to_tiled_ff(x, tf):
    cut = D_FFP - tf
    zeros = jnp.zeros(x.shape[:-1] + (FF_OVER,), x.dtype)
    return jnp.concatenate([x[..., :cut], zeros, x[..., cut:]], axis=-1)


def _from_tiled_ff(x, tf):
    cut = D_FFP - tf
    return jnp.concatenate([x[..., :cut], x[..., cut + FF_OVER:]], axis=-1)


def _matmul(x, w, layer, *, tm, tn, n=None, w_cols=None):
    m, k = x.shape
    n = w.shape[2] if n is None else n
    tm = min(tm, m)
    if w_cols is None:
        w_spec = pl.BlockSpec((None, k, tn), lambda i, j: (layer, 0, j))
    else:
        w_spec = pl.BlockSpec((None, pl.Element(k), pl.Element(tn)), lambda i, j: (layer, 0, w_cols(j)))
    return pl.pallas_call(
        _mm_kernel,
        grid=(m // tm, n // tn),
        in_specs=[pl.BlockSpec((tm, k), lambda i, j: (i, 0)), w_spec],
        out_specs=pl.BlockSpec((tm, tn), lambda i, j: (i, j)),
        out_shape=jax.ShapeDtypeStruct((m, n), F32),
        compiler_params=_cparams("parallel", "parallel"),
        name="proj",
    )(x, w)


KV4_GROUPS = 4 * KV_HEADS
WIN_GROUPS = 2 * KV_HEADS


def _proj_in_kernel(x_ref, w_ref, z_ref, kv_ref, win_ref, xb_ref, *, tn):
    j = pl.program_id(1)

    @pl.when(j == 0)
    def _():
        xb_ref[...] = x_ref[...].astype(BF16)

    acc = _dot(xb_ref[...], w_ref[...])
    z_ref[...] = acc
    tm = acc.shape[0]
    per_tile = tn // HEAD_DIM
    first_tile = C_NKV // tn
    for tile in range(first_tile, first_tile + N_KV_SLOTS * KV_HEADS // per_tile):
        @pl.when(j == tile)
        def _(tile=tile):
            for c in range(per_tile):
                grp = (tile - first_tile) * per_tile + c
                val = acc[:, c * HEAD_DIM:(c + 1) * HEAD_DIM]
                if grp < KV4_GROUPS:
                    kv_ref[pl.ds(grp, tm, stride=KV4_GROUPS), :] = val
                else:
                    win_ref[pl.ds(grp - KV4_GROUPS, tm, stride=WIN_GROUPS), :] = val


def _proj_in_carry_kernel(x_ref, w_ref, kv_in_ref, win_in_ref, z_ref, kv_ref, win_ref, xb_ref, *, tn):
    del kv_in_ref, win_in_ref
    _proj_in_kernel(x_ref, w_ref, z_ref, kv_ref, win_ref, xb_ref, tn=tn)


def _proj_in(x, w, layer, rows_so_far, *, tm, tn):
    m, k = x.shape
    depth, _, n = w.shape
    nm = m // tm
    assert C_NKV % tn == 0 and (N_KV_SLOTS * KV_HEADS * HEAD_DIM) % tn == 0
    carried = () if rows_so_far is None else tuple(rows_so_far)
    return pl.pallas_call(
        functools.partial(_proj_in_carry_kernel if carried else _proj_in_kernel, tn=tn),
        grid=(nm, n // tn),
        in_specs=[pl.BlockSpec((tm, k), lambda i, j: (i, 0)), pl.BlockSpec((None, k, tn), lambda i, j: (layer, 0, j))]
                 + [pl.BlockSpec(memory_space=pl.ANY)] * len(carried),
        out_specs=[pl.BlockSpec((tm, tn), lambda i, j: (i, j)),
                   pl.BlockSpec((tm * KV4_GROUPS, HEAD_DIM), lambda i, j: (layer * nm + i, 0)),
                   pl.BlockSpec((tm * WIN_GROUPS, HEAD_DIM), lambda i, j: (layer * nm + i, 0))],
        out_shape=[jax.ShapeDtypeStruct((m, n), F32),
                   jax.ShapeDtypeStruct((depth * m * KV4_GROUPS, HEAD_DIM), F32),
                   jax.ShapeDtypeStruct((depth * m * WIN_GROUPS, HEAD_DIM), F32)],
        input_output_aliases={2: 1, 3: 2} if carried else {},
        scratch_shapes=[pltpu.VMEM((tm, k), BF16)],
        compiler_params=_cparams("parallel", "arbitrary"),
        name="proj_in",
    )(x, w, *carried)


def _wout_kernel(om_ref, on_ref, os_ref, x_ref, w_ref, lw_ref, lb_ref, o_ref):
    acc = _dot(om_ref[...].astype(BF16), w_ref[0:D_MLSTM, :])
    acc += _dot(on_ref[...].astype(BF16), w_ref[D_MLSTM:D_MLSTM + D_NSA, :])
    acc += _dot(os_ref[...].astype(BF16), w_ref[D_MLSTM + D_NSA:D_MODEL, :])
    o_ref[...] = _layer_norm(DEEPNORM_ALPHA * x_ref[...] + acc, lw_ref[...], lb_ref[...])


def _wout_ln(om, on, os_, x, w, layer, lw, lb, *, tm):
    m = x.shape[0]
    tm = min(tm, m)
    row = lambda width: pl.BlockSpec((tm, width), lambda i: (i, 0))
    full = lambda a: pl.BlockSpec(a.shape, lambda i: (0, 0))
    w_spec = pl.BlockSpec((None,) + w.shape[1:], lambda i: (layer, 0, 0))
    return pl.pallas_call(
        _wout_kernel,
        grid=(m // tm,),
        in_specs=[row(D_MLSTM), row(D_NSA), row(D_S5), row(D_MODEL), w_spec, full(lw), full(lb)],
        out_specs=row(D_MODEL),
        out_shape=jax.ShapeDtypeStruct((m, D_MODEL), F32),
        compiler_params=_cparams("parallel"),
        name="wout_ln",
    )(om, on, os_, x, w, lw, lb)


def _down_kernel(h_ref, w_ref, x_ref, lw_ref, lb_ref, o_ref, acc_ref, *, nk):
    k = pl.program_id(1)

    @pl.when(k == 0)
    def _():
        acc_ref[...] = jnp.zeros_like(acc_ref)

    acc_ref[...] += _dot(h_ref[...], w_ref[...])

    @pl.when(k == nk - 1)
    def _():
        o_ref[...] = _layer_norm(DEEPNORM_ALPHA * x_ref[...] + acc_ref[...], lw_ref[...], lb_ref[...])


def _down_ln(h, w, layer, x, lw, lb, *, tm, tk):
    m, kk = h.shape
    tm = min(tm, m)
    nk = kk // tk
    return pl.pallas_call(
        functools.partial(_down_kernel, nk=nk),
        grid=(m // tm, nk),
        in_specs=[
            pl.BlockSpec((tm, tk), lambda i, k: (i, k)),
            pl.BlockSpec((None, pl.Element(tk), pl.Element(D_MODEL)), lambda i, k: (layer, _ff_start(k, tk), 0)),
            pl.BlockSpec((tm, D_MODEL), lambda i, k: (i, 0)),
            pl.BlockSpec((1, D_MODEL), lambda i, k: (0, 0)),
            pl.BlockSpec((1, D_MODEL), lambda i, k: (0, 0)),
        ],
        out_specs=pl.BlockSpec((tm, D_MODEL), lambda i, k: (i, 0)),
        out_shape=jax.ShapeDtypeStruct((m, D_MODEL), F32),
        scratch_shapes=[pltpu.VMEM((tm, D_MODEL), F32)],
        compiler_params=_cparams("parallel", "arbitrary"),
        name="down_ln",
    )(h, w, x, lw, lb)


def _convgate_kernel(a_ref, g_ref, sa_ref, sg_ref, wa_ref, wg_ref, ba_ref, bg_ref, o_ref):
    rows = lax.broadcasted_iota(jnp.int32, (1, SUBLANES, 1), 1)

    def conv(cur_ref, st_ref, w_ref, b_ref):
        cur = cur_ref[...]
        prev = st_ref[...]
        p1 = prev[:, 7:8, :]
        p2 = prev[:, 6:7, :]
        x1 = jnp.where(rows == 0, p1, pltpu.roll(cur, 1, 1))
        x2 = jnp.where(rows == 0, p2, jnp.where(rows == 1, p1, pltpu.roll(cur, 2, 1)))
        w = w_ref[...]
        return b_ref[...] + w[0:1, :] * x2 + w[1:2, :] * x1 + w[2:3, :] * cur

    a = conv(a_ref, sa_ref, wa_ref, ba_ref)
    g = conv(g_ref, sg_ref, wg_ref, bg_ref)
    o_ref[...] = (a * jax.nn.sigmoid(a) * g).astype(o_ref.dtype)


def _convgate(up, state8, conv_w8, conv_b, *, tf):
    b, t, _ = up.shape
    assert t == SUBLANES
    nf = D_FFP // tf
    cur_a = pl.BlockSpec((b, t, tf), lambda fi: (0, 0, fi))
    cur_g = pl.BlockSpec((b, t, tf), lambda fi: (0, 0, fi + nf))
    w_a = pl.BlockSpec((SUBLANES, tf), lambda fi: (0, fi))
    w_g = pl.BlockSpec((SUBLANES, tf), lambda fi: (0, fi + nf))
    b_a = pl.BlockSpec((1, tf), lambda fi: (0, fi))
    b_g = pl.BlockSpec((1, tf), lambda fi: (0, fi + nf))
    return pl.pallas_call(
        _convgate_kernel,
        grid=(nf,),
        in_specs=[cur_a, cur_g, cur_a, cur_g, w_a, w_g, b_a, b_g],
        out_specs=pl.BlockSpec((b, t, tf), lambda fi: (0, 0, fi)),
        out_shape=jax.ShapeDtypeStruct((b, t, D_FFP), BF16),
        compiler_params=_cparams("parallel"),
        name="convgate",
    )(up, up, state8, state8, conv_w8, conv_w8, conv_b, conv_b)


def _ffn_kernel(x_ref, xh_ref, wa_ref, wg_ref, wd_ref, cwa_ref, cwg_ref, cba_ref, cbg_ref, lw_ref, lb_ref,
                o_ref, sa_ref, sg_ref, xe_ref, *, nf):
    first = pl.program_id(1) == 0
    f = pl.program_id(2)
    tt = x_ref.shape[0]

    @pl.when(f == 0)
    def _():
        o_ref[...] = jnp.zeros_like(o_ref)
        halo = jnp.where(first, 0.0, xh_ref[...])
        xe_ref[0:FFN_HALO, :] = halo.astype(BF16)
        xe_ref[FFN_HALO:, :] = x_ref[...].astype(BF16)

    xe = xe_ref[...]

    def branch(w_ref, cw_ref, cb_ref, s_ref):
        up = _dot(xe, w_ref[...])
        s_ref[...] = up[FFN_HALO + tt - SUBLANES:FFN_HALO + tt, :]
        w = cw_ref[...]
        x1 = pltpu.roll(up, 1, 0)[FFN_HALO:, :]
        x2 = pltpu.roll(up, 2, 0)[FFN_HALO:, :]
        return cb_ref[...] + w[0:1, :] * x2 + w[1:2, :] * x1 + w[2:3, :] * up[FFN_HALO:, :]

    a = branch(wa_ref, cwa_ref, cba_ref, sa_ref)
    g = branch(wg_ref, cwg_ref, cbg_ref, sg_ref)
    o_ref[...] += _dot((a * jax.nn.sigmoid(a) * g).astype(BF16), wd_ref[...])

    @pl.when(f == nf - 1)
    def _():
        o_ref[...] = _layer_norm(DEEPNORM_ALPHA * x_ref[...] + o_ref[...], lw_ref[...], lb_ref[...])


def _ffn_prompt(x3, w_up, w_down, layer, conv_w8, conv_b, lw, lb, *, tt, tf):
    b, t, _ = x3.shape
    nf = D_FFP // tf
    tt = min(tt, t)
    hb = tt // FFN_HALO
    half_a = lambda rows: pl.BlockSpec((rows, tf), lambda bi, ti, fi: (0, fi))
    half_g = lambda rows: pl.BlockSpec((rows, tf), lambda bi, ti, fi: (0, fi + nf))
    cols = _up_cols(tf)
    w_a = pl.BlockSpec((None, pl.Element(D_MODEL), pl.Element(tf)), lambda bi, ti, fi: (layer, 0, cols(fi)))
    w_g = pl.BlockSpec((None, pl.Element(D_MODEL), pl.Element(tf)), lambda bi, ti, fi: (layer, 0, cols(fi + nf)))
    vec = pl.BlockSpec((1, D_MODEL), lambda bi, ti, fi: (0, 0))
    tail = pl.BlockSpec((None, SUBLANES, tf), lambda bi, ti, fi: (bi, 0, fi))
    return pl.pallas_call(
        functools.partial(_ffn_kernel, nf=nf),
        grid=(b, t // tt, nf),
        in_specs=[pl.BlockSpec((None, tt, D_MODEL), lambda bi, ti, fi: (bi, ti, 0), pipeline_mode=pl.Buffered(1)),
                  pl.BlockSpec((None, FFN_HALO, D_MODEL), lambda bi, ti, fi: (bi, jnp.maximum(ti * hb - 1, 0), 0)),
                  w_a, w_g,
                  pl.BlockSpec((None, pl.Element(tf), pl.Element(D_MODEL)),
                               lambda bi, ti, fi: (layer, _ff_start(fi, tf), 0)),
                  half_a(SUBLANES), half_g(SUBLANES), half_a(1), half_g(1), vec, vec],
        out_specs=[pl.BlockSpec((None, tt, D_MODEL), lambda bi, ti, fi: (bi, ti, 0)), tail, tail],
        out_shape=[jax.ShapeDtypeStruct((b, t, D_MODEL), F32),
                   jax.ShapeDtypeStruct((b, SUBLANES, D_FFP), F32),
                   jax.ShapeDtypeStruct((b, SUBLANES, D_FFP), F32)],
        scratch_shapes=[pltpu.VMEM((tt + FFN_HALO, D_MODEL), BF16)],
        compiler_params=_cparams("parallel", "arbitrary", "arbitrary"),
        name="ffn_prompt",
    )(x3, x3, w_up, w_up, w_down, conv_w8, conv_w8, conv_b, conv_b, lw, lb)


def _cumsum_rows(x):
    n = x.shape[0]
    rows = lax.broadcasted_iota(jnp.int32, (n, 1), 0)
    d = 1
    while d < n:
        x = x + jnp.where(rows >= d, pltpu.roll(x, d, 0), 0.0)
        d *= 2
    return x


def _log_sigmoid(x):
    return jnp.minimum(x, 0.0) - jnp.log1p(jnp.exp(-jnp.abs(x)))


def _mlstm_kernel(q_ref, k_ref, v_ref, og_ref, g_ref, gb_ref, nw_ref, c0_ref, n0_ref, m0_ref,
                  out_ref, c_ref, n_ref, m_ref, *, L, t_valid):
    ci = pl.program_id(1)
    lin = q_ref.shape[0]

    @pl.when(ci == 0)
    def _():
        c_ref[...] = c0_ref[...]
        n_ref[...] = n0_ref[...]
        m_ref[...] = m0_ref[...]

    def rows_of(ref):
        x = ref[...]
        if lin < L:
            x = jnp.concatenate([x, jnp.zeros((L - lin, x.shape[1]), x.dtype)], axis=0)
        return x

    rows = lax.broadcasted_iota(jnp.int32, (L, 1), 0)
    valid = (ci * L + rows) < t_valid
    pre = rows_of(g_ref) + gb_ref[...]
    lf = jnp.where(valid, _log_sigmoid(pre), 0.0)
    ig = jnp.where(valid, pre, NEG_INF)
    bcum = _cumsum_rows(lf)
    dt = (pltpu.roll(ig, GATE_MF - GATE_MI, 1) - bcum).T
    q_all, k_all, v_all, og_all = rows_of(q_ref), rows_of(k_ref), rows_of(v_ref), rows_of(og_ref)
    tri = lax.broadcasted_iota(jnp.int32, (L, L), 0) >= lax.broadcasted_iota(jnp.int32, (L, L), 1)
    nw = nw_ref[...]

    for h in range(MLSTM_HEADS):
        sl = slice(h * HEAD_DIM, (h + 1) * HEAD_DIM)
        q = q_all[:, sl]
        k = k_all[:, sl] * QK_SCALE
        v = v_all[:, sl]
        qb, kb, vb = q.astype(BF16), k.astype(BF16), v.astype(BF16)
        b_col = bcum[:, GATE_MF + h:GATE_MF + h + 1]
        ig_col = ig[:, GATE_MI + h:GATE_MI + h + 1]
        d_row = dt[GATE_MF + h:GATE_MF + h + 1, :]
        c_prev = c_ref[h]
        n_prev = n_ref[h:h + 1, :]
        m_prev = m_ref[h:h + 1, 0:1]

        dmat = jnp.where(tri, b_col + d_row, NEG_INF)
        g_col = b_col + m_prev
        m_row = jnp.maximum(jnp.max(dmat, axis=1, keepdims=True), g_col)
        a = jnp.exp(dmat - m_row) * _dot_nt(qb, kb)
        w_inter = jnp.exp(g_col - m_row)
        num = _dot(a.astype(BF16), vb) + w_inter * _dot(qb, c_prev.astype(BF16))
        den = jnp.sum(a, axis=1, keepdims=True) + w_inter * jnp.sum(q * n_prev, axis=1, keepdims=True)
        hid = num / jnp.maximum(jnp.abs(den), jnp.exp(-m_row))
        mu = jnp.mean(hid, axis=1, keepdims=True)
        dlt = hid - mu
        var = jnp.mean(dlt * dlt, axis=1, keepdims=True)
        hn = dlt * lax.rsqrt(var + LN_EPS) * nw[:, sl]
        res = jax.nn.sigmoid(og_all[:, sl]) * hn
        out_ref[:, sl] = res[0:lin, :]

        f_tot = b_col[L - 1:L, :]
        w_s = f_tot - b_col + ig_col
        m_new = jnp.maximum(f_tot + m_prev, jnp.max(w_s, axis=0, keepdims=True))
        ws = jnp.exp(w_s - m_new)
        decay = jnp.exp(f_tot + m_prev - m_new)
        c_ref[h] = decay * c_prev + _dot(k.T.astype(BF16), (ws * v).astype(BF16))
        n_ref[h:h + 1, :] = decay * n_prev + jnp.sum(ws * k, axis=0, keepdims=True)
        m_ref[h:h + 1, :] = jnp.broadcast_to(m_new, (1, LANES))


def _mlstm(z, gate_b, norm_w, c0, n0, m0, *, L, lin, t_valid):
    b, tz, _ = z.shape
    nchunks = tz // lin
    qblk = lambda col: pl.BlockSpec((None, lin, D_MLSTM), lambda bi, ci: (bi, ci, col // D_MLSTM))
    st4 = pl.BlockSpec((None, MLSTM_HEADS, HEAD_DIM, HEAD_DIM), lambda bi, ci: (bi, 0, 0, 0))
    st3 = pl.BlockSpec((None, MLSTM_HEADS, LANES), lambda bi, ci: (bi, 0, 0))
    return pl.pallas_call(
        functools.partial(_mlstm_kernel, L=L, t_valid=t_valid),
        grid=(b, nchunks),
        in_specs=[qblk(C_MQ), qblk(C_MK), qblk(C_MV), qblk(C_MO),
                  pl.BlockSpec((None, lin, LANES), lambda bi, ci: (bi, ci, C_GATE // LANES)),
                  pl.BlockSpec((1, LANES), lambda bi, ci: (0, 0)),
                  pl.BlockSpec((1, D_MLSTM), lambda bi, ci: (0, 0)),
                  st4, st3, st3],
        out_specs=[pl.BlockSpec((None, lin, D_MLSTM), lambda bi, ci: (bi, ci, 0)), st4, st3, st3],
        out_shape=[jax.ShapeDtypeStruct((b, tz, D_MLSTM), F32),
                   jax.ShapeDtypeStruct((b, MLSTM_HEADS, HEAD_DIM, HEAD_DIM), F32),
                   jax.ShapeDtypeStruct((b, MLSTM_HEADS, LANES), F32),
                   jax.ShapeDtypeStruct((b, MLSTM_HEADS, LANES), F32)],
        compiler_params=_cparams("parallel", "arbitrary"),
        name="mlstm",
    )(z, z, z, z, z, gate_b, norm_w, c0, n0, m0)


S5_IN_TILES = S5_CH // 256
S5_IN_FEATS = D_S5 // S5_IN_TILES
S5_OUT_TILES = D_S5 // 256
S5_OUT_CH = S5_CH // S5_OUT_TILES


S5_LT = S5_CH // LANES


def _lane_tiles(x):
    return jnp.stack([x[:, k * LANES:(k + 1) * LANES] for k in range(x.shape[1] // LANES)])


def _lane_untile(x):
    return jnp.concatenate([x[k] for k in range(x.shape[0])], axis=1)


def _s5_kernel(u_ref, wb_ref, wc_ref, lr_ref, li_ref, d_ref, gw_ref, gb_ref, x0r_ref, x0i_ref,
               o_ref, xr_ref, xi_ref, sr_ref, si_ref, y_ref, *, steps):
    nb, lc, _ = u_ref.shape

    @pl.when(pl.program_id(0) == 0)
    def _():
        xr_ref[...] = x0r_ref[...]
        xi_ref[...] = x0i_ref[...]

    for b in range(nb):
        ub = u_ref[b].astype(BF16)
        for c in range(S5_IN_TILES):
            bu = _dot(ub[:, c * S5_IN_FEATS:(c + 1) * S5_IN_FEATS], wb_ref[c])
            for k in range(256 // LANES):
                rows = pl.ds(b, lc, stride=nb)
                sr_ref.at[c * (256 // LANES) + k][rows, :] = bu[:, k * LANES:(k + 1) * LANES]
                si_ref.at[c * (256 // LANES) + k][rows, :] = bu[:, 256 + k * LANES:256 + (k + 1) * LANES]
    lam_r = _lane_tiles(jnp.broadcast_to(lr_ref[...], (SUBLANES, S5_CH)))
    lam_i = _lane_tiles(jnp.broadcast_to(li_ref[...], (SUBLANES, S5_CH)))
    row8 = lax.broadcasted_iota(jnp.int32, (1, SUBLANES, 1), 1)
    per_group = SUBLANES // nb

    def group(i, carry):
        xr, xi = carry
        r0 = pl.multiple_of(i * SUBLANES, SUBLANES)
        br = sr_ref[:, pl.ds(r0, SUBLANES), :]
        bi = si_ref[:, pl.ds(r0, SUBLANES), :]
        out_r, out_i = br, bi
        for k in range(per_group):
            nr = lam_r * xr - lam_i * xi + br
            ni = lam_r * xi + lam_i * xr + bi
            here = (row8 >= k * nb) & (row8 < (k + 1) * nb)
            out_r = jnp.where(here, nr, out_r)
            out_i = jnp.where(here, ni, out_i)
            if per_group > 1:
                xr = pltpu.roll(nr, nb, 1)
                xi = pltpu.roll(ni, nb, 1)
            else:
                xr, xi = nr, ni
        sr_ref[:, pl.ds(r0, SUBLANES), :] = out_r
        si_ref[:, pl.ds(r0, SUBLANES), :] = out_i
        return xr, xi

    xr, xi = lax.fori_loop(0, steps * nb // SUBLANES, group, (_lane_tiles(xr_ref[...]), _lane_tiles(xi_ref[...])),
                           unroll=4)
    xr_ref[...] = _lane_untile(xr)
    xi_ref[...] = _lane_untile(xi)
    per_out = S5_OUT_CH // LANES
    for j in range(S5_OUT_TILES):
        lhs_r = jnp.concatenate([sr_ref[j * per_out + k] for k in range(per_out)], axis=1).astype(BF16)
        lhs_i = jnp.concatenate([si_ref[j * per_out + k] for k in range(per_out)], axis=1).astype(BF16)
        y = _dot(lhs_r, wc_ref[0, j]) + _dot(lhs_i, wc_ref[1, j])
        for k in range(256 // LANES):
            y_ref[j * (256 // LANES) + k] = y[:, k * LANES:(k + 1) * LANES]
    for b in range(nb):
        rows = pl.ds(b, lc, stride=nb)
        y = jnp.concatenate([y_ref.at[k][rows, :] for k in range(D_S5 // LANES)], axis=1) + d_ref[...] * u_ref[b]
        zz = jax.nn.gelu(y)
        o_ref[b] = zz * jax.nn.sigmoid(_dot(zz.astype(BF16), gw_ref[...]) + gb_ref[...])


def _s5(z3, wb, wc, lam_r, lam_i, d, glu_w, glu_b, x0r, x0i, *, lc, steps):
    b, tz, _ = z3.shape
    lc = min(lc, tz)
    steps = min(steps, lc)
    assert steps == lc or tz == lc
    full = lambda a: pl.BlockSpec(a.shape, lambda i: (0,) * a.ndim)
    st = pl.BlockSpec((SUBLANES, S5_CH), lambda i: (0, 0))
    return pl.pallas_call(
        functools.partial(_s5_kernel, steps=steps),
        grid=(tz // lc,),
        in_specs=[pl.BlockSpec((b, lc, D_S5), lambda i: (0, i, C_SU // D_S5)),
                  full(wb), full(wc), full(lam_r), full(lam_i), full(d), full(glu_w), full(glu_b), st, st],
        out_specs=[pl.BlockSpec((b, lc, D_S5), lambda i: (0, i, 0)), st, st],
        out_shape=[jax.ShapeDtypeStruct((b, tz, D_S5), F32),
                   jax.ShapeDtypeStruct((SUBLANES, S5_CH), F32),
                   jax.ShapeDtypeStruct((SUBLANES, S5_CH), F32)],
        scratch_shapes=[pltpu.VMEM((S5_LT, lc * b, LANES), F32), pltpu.VMEM((S5_LT, lc * b, LANES), F32),
                        pltpu.VMEM((D_S5 // LANES, lc * b, LANES), F32)],
        compiler_params=_cparams("arbitrary"),
        name="s5",
    )(z3, wb, wc, lam_r, lam_i, d, glu_w, glu_b, x0r, x0i)


N_CMB = 2 * KV_HEADS


def _cmp_project(rows_of, w_ref, o_ref):
    nch = o_ref.shape[0]
    for cmb in range(N_CMB):
        slot = cmb // KV_HEADS
        rows_ref = rows_of(cmb)
        acc = jnp.zeros((nch, 2 * HEAD_DIM), F32)
        for l in range(CMP_STRIDE):
            x = rows_ref[pl.ds(l, nch, stride=CMP_STRIDE), :]
            acc += _dot(x.astype(BF16), w_ref[slot, l])
        o_ref[:, cmb * 2 * HEAD_DIM:(cmb + 1) * 2 * HEAD_DIM] = acc


def _cmp_p_kernel(r0_ref, r1_ref, r2_ref, r3_ref, w_ref, o_ref):
    rows = (r0_ref, r1_ref, r2_ref, r3_ref)
    _cmp_project(lambda cmb: rows[cmb], w_ref, o_ref)


def _cmp_project_prompt(z, wcat):
    b, t, _ = z.shape
    nch = t // CMP_STRIDE
    rows = lambda cmb: pl.BlockSpec((None, t, HEAD_DIM), lambda bi: (bi, 0, C_NKV // HEAD_DIM + cmb))
    return pl.pallas_call(
        _cmp_p_kernel,
        grid=(b,),
        in_specs=[rows(cmb) for cmb in range(N_CMB)] + [pl.BlockSpec(wcat.shape, lambda bi: (0, 0, 0, 0))],
        out_specs=pl.BlockSpec((None, nch, N_CMB * 2 * HEAD_DIM), lambda bi: (bi, 0, 0)),
        out_shape=jax.ShapeDtypeStruct((b, nch, N_CMB * 2 * HEAD_DIM), F32),
        compiler_params=_cparams("parallel"),
        name="cmp_project_prompt",
    )(z, z, z, z, wcat)


def _cmp_s_kernel(pt_ref, *refs):
    del pt_ref
    per_pos = 2 * KV_HEADS
    page_refs = [r.reshape(PAGE_SIZE * per_pos, HEAD_DIM) for r in refs[:CMP_PAGES]]
    w_ref, o_ref = refs[CMP_PAGES:]
    per_page = PAGE_SIZE // CMP_STRIDE
    for cmb in range(N_CMB):
        acc = jnp.zeros((o_ref.shape[0], 2 * HEAD_DIM), F32)
        for l in range(CMP_STRIDE):
            x = jnp.concatenate(
                [page_refs[p][pl.ds(l * per_pos + cmb, per_page, stride=CMP_STRIDE * per_pos), :]
                 for p in range(CMP_PAGES)], axis=0)
            acc += _dot(x.astype(BF16), w_ref[cmb // KV_HEADS, l])
        o_ref[:, cmb * 2 * HEAD_DIM:(cmb + 1) * 2 * HEAD_DIM] = acc


def _page_spec(layer, slot, p, pages_per_step, grid_rank):
    def index_map(*idx):
        bi, si, pt = idx[0], idx[grid_rank - 1], idx[grid_rank]
        return (layer, pt[bi, si * pages_per_step + p], 0, slot, 0, 0)

    return pl.BlockSpec((None, None, PAGE_SIZE, None, KV_HEADS, HEAD_DIM), index_map)


def _cmp_project_sample(cache6, page_table, layer, wcat):
    b, npages = page_table.shape
    steps = npages // CMP_PAGES
    nch = CMP_PAGES * PAGE_SIZE // CMP_STRIDE
    def page_pair(p):
        return pl.BlockSpec((None, None, PAGE_SIZE, 2, KV_HEADS, HEAD_DIM),
                            lambda bi, si, pt: (layer, pt[bi, si * CMP_PAGES + p], 0, 0, 0, 0))

    pages = [page_pair(p) for p in range(CMP_PAGES)]

    grid_spec = pltpu.PrefetchScalarGridSpec(
        num_scalar_prefetch=1,
        grid=(b, steps),
        in_specs=pages + [pl.BlockSpec(wcat.shape, lambda bi, si, pt: (0, 0, 0, 0))],
        out_specs=pl.BlockSpec((None, nch, N_CMB * 2 * HEAD_DIM), lambda bi, si, pt: (bi, si, 0)),
    )
    return pl.pallas_call(
        _cmp_s_kernel,
        grid_spec=grid_spec,
        out_shape=jax.ShapeDtypeStruct((b, steps * nch, N_CMB * 2 * HEAD_DIM), F32),
        compiler_params=_cparams("parallel", "arbitrary"),
        name="cmp_project_sample",
    )(page_table, *([cache6] * CMP_PAGES), wcat)


def _cmp_fin_kernel(p_ref, b1_ref, w2_ref, o_ref):
    nch = p_ref.shape[0]
    for cmb in range(N_CMB):
        slot = cmb // KV_HEADS
        c0 = cmb * 2 * HEAD_DIM
        first = p_ref[:, c0:c0 + HEAD_DIM]
        second = pltpu.roll(p_ref[:, c0 + HEAD_DIM:c0 + 2 * HEAD_DIM], nch - 1, 0)
        hid = b1_ref[slot:slot + 1, :] + first + second
        o_ref[cmb] = _dot(jax.nn.gelu(hid).astype(BF16), w2_ref[slot])


def _cmp_finish(p, b1, w2):
    b, nch, _ = p.shape
    return pl.pallas_call(
        _cmp_fin_kernel,
        grid=(b,),
        in_specs=[pl.BlockSpec((None, nch, p.shape[2]), lambda bi: (bi, 0, 0)),
                  pl.BlockSpec(b1.shape, lambda bi: (0, 0)),
                  pl.BlockSpec(w2.shape, lambda bi: (0, 0, 0))],
        out_specs=pl.BlockSpec((None, N_CMB, nch, HEAD_DIM), lambda bi: (bi, 0, 0, 0)),
        out_shape=jax.ShapeDtypeStruct((b, N_CMB, nch, HEAD_DIM), F32),
        compiler_params=_cparams("parallel"),
        name="cmp_finish",
    )(p, b1, w2)


def _softmax_init(m_scr, l_scr, acc_scr):
    m_scr[...] = jnp.full(m_scr.shape, M_INIT, F32)
    l_scr[...] = jnp.zeros(l_scr.shape, F32)
    acc_scr[...] = jnp.zeros(acc_scr.shape, F32)


def _softmax_update(s, mask, vb, m_scr, l_scr, acc_scr):
    s = jnp.where(mask, s, NEG_INF)
    m_prev = m_scr[...]
    m_new = jnp.maximum(m_prev, jnp.max(s, axis=1, keepdims=True))
    alpha = jnp.exp(m_prev - m_new)
    p = jnp.exp(s - m_new)
    l_scr[...] = alpha * l_scr[...] + jnp.sum(p, axis=1, keepdims=True)
    acc_scr[...] = alpha * acc_scr[...] + _dot(p.astype(BF16), vb)
    m_scr[...] = m_new


def _softmax_result(l_scr, acc_scr):
    l = l_scr[...]
    return acc_scr[...] / jnp.where(l > 0, l, 1.0)


def _masked_probs(s, mask):
    s = jnp.where(mask, s, NEG_INF)
    m = jnp.max(s, axis=1, keepdims=True)
    m = jnp.where(m > NEG_INF, m, 0.0)
    p = jnp.exp(s - m)
    den = jnp.sum(p, axis=1, keepdims=True)
    return p / jnp.where(den > 0, den, 1.0)


def _stack_heads(q):
    return jnp.concatenate([q[:, j * HEAD_DIM:(j + 1) * HEAD_DIM] for j in range(GROUP)], axis=0)


def _fold_lanes(x, op):
    out = x[:, 0:LANES]
    for k in range(1, x.shape[1] // LANES):
        out = op(out, x[:, k * LANES:(k + 1) * LANES])
    return out


def _block_importance(pc, n_cmp, n_sel):
    ncp = pc.shape[1]
    nsp = -(-n_sel // LANES) * LANES
    c_start = lax.broadcasted_iota(jnp.int32, (ncp, nsp), 0) * CMP_STRIDE
    s_start = lax.broadcasted_iota(jnp.int32, (ncp, nsp), 1) * SEL_BLOCK
    overlap = ((c_start < s_start + SEL_BLOCK) & (c_start + CMP_BLOCK > s_start)
               & (c_start < n_cmp * CMP_STRIDE)).astype(F32)
    return jnp.dot(pc, overlap, preferred_element_type=F32, precision=lax.Precision.HIGHEST)


def _force_blocks(imp, blk, cur):
    forced = (blk == 0) | (blk == cur) | (blk == cur - 1)
    imp = jnp.where(forced, FORCE_SCORE, imp)
    return jnp.where(blk > cur, -1.0, imp)


def _select_blocks_cols(pc, cur, n_cmp, n_sel):
    imp = _block_importance(pc, n_cmp, n_sel)
    blk = lax.broadcasted_iota(jnp.int32, (1, imp.shape[1]), 1)
    imp = _force_blocks(imp, blk, cur)

    def body(sp, count):
        col = jnp.sum(jnp.where(blk == sp, imp, 0.0), axis=1, keepdims=True)
        ahead = (col > imp) | ((col == imp) & (sp < blk))
        return count + ahead.astype(F32)

    count = lax.fori_loop(0, n_sel, body, jnp.zeros(imp.shape, F32), unroll=True)
    return ((count < min(SEL_TOPK, n_sel)) & (blk < n_sel)).astype(F32)


def _select_blocks_rows(pc, cur_row, n_cmp, n_sel):
    rows = pc.shape[0]
    nrow = -(-n_sel // SUBLANES) * SUBLANES
    imp_t = _block_importance(pc, n_cmp, n_sel).T[0:nrow, :]
    blk = lax.broadcasted_iota(jnp.int32, (nrow, 1), 0)
    imp_t = _force_blocks(imp_t, blk, cur_row)
    count = jnp.zeros(imp_t.shape, F32)
    for sp in range(n_sel):
        row = imp_t[sp:sp + 1, :]
        count += ((row > imp_t) | ((row == imp_t) & (sp < blk))).astype(F32)
    sel_t = ((count < min(SEL_TOPK, n_sel)) & (blk < n_sel)).astype(F32)
    sel_t = jnp.concatenate([sel_t, jnp.zeros((LANES - nrow, rows), F32)], axis=0)
    return sel_t.T


def _gate(gt, col):
    lane = lax.broadcasted_iota(jnp.int32, (1, LANES), 1)
    return jax.nn.sigmoid(jnp.sum(jnp.where(lane == col, gt, 0.0), axis=1, keepdims=True))


def _nsa_prompt_kernel(q_ref, ks_ref, vs_ref, kw_ref, vw_ref, kc_ref, vc_ref, bc_ref, tz_ref, gt_ref, gb_ref,
                       o_ref, s_scr, selb_scr, mx_scr, l_scr, acc_scr, *, t_len):
    g = pl.program_id(1)
    qt = pl.program_id(2)
    q0 = qt * TQ
    n_cmp = t_len // CMP_STRIDE - (CMP_BLOCK // CMP_STRIDE) + 1
    n_sel = t_len // SEL_BLOCK
    ncp = kc_ref.shape[0]
    rows4 = GROUP * TQ
    qs = _stack_heads(q_ref[...] * QK_SCALE).astype(BF16)
    qpos = q0 + (lax.broadcasted_iota(jnp.int32, (rows4, 1), 0) & (TQ - 1))

    n_idx = lax.broadcasted_iota(jnp.int32, (1, ncp), 1)
    s_c = _dot_nt(qs, kc_ref[...].astype(BF16)) + bc_ref[...].reshape(rows4, ncp)
    cmask = (qpos - (n_idx * CMP_STRIDE + CMP_BLOCK - 1) >= 0) & (n_idx < n_cmp)
    p_c = _masked_probs(s_c, cmask)
    o_c = _dot(p_c.astype(BF16), vc_ref[...].astype(BF16))
    pc = p_c[0:TQ] + p_c[TQ:2 * TQ] + p_c[2 * TQ:3 * TQ] + p_c[3 * TQ:4 * TQ]
    cur_row = jnp.right_shift(q0 + lax.broadcasted_iota(jnp.int32, (1, TQ), 1), SEL_SHIFT)
    sel = _select_blocks_rows(pc, cur_row, n_cmp, n_sel)
    sel_add = ((sel - 1.0) * -MASKED).astype(BF16)

    kt_hi = (q0 + TQ - 1) // TK + 1
    kt_far = jnp.maximum(q0 - (FAR_DIST - 1), 0) // TK
    kidx = lax.broadcasted_iota(jnp.int32, (1, TK), 1)
    srow = lax.broadcasted_iota(jnp.int32, (LANES, TK), 0)

    def spread_selection(kt, carry):
        k0 = kt * TK
        expand = (jnp.right_shift(k0 + lax.broadcasted_iota(jnp.int32, (LANES, TK), 1), SEL_SHIFT) == srow).astype(BF16)
        selb_scr[kt] = _dot(sel_add, expand)
        return carry

    lax.fori_loop(0, kt_hi, spread_selection, 0)

    def tile_bias(k0):
        return tz_ref[jnp.clip((q0 - k0) // TQ, 0, N_TZ - 1)].reshape(rows4, TK)

    def attend(k_ref, v_ref, kt_lo, mask_tile):
        mx_scr[...] = jnp.full(mx_scr.shape, NEG_INF, F32)

        def scores(near):
            def body(kt, carry):
                k0 = pl.multiple_of(kt * TK, TK)
                s = _dot_nt(qs, k_ref[pl.ds(k0, TK), :].astype(BF16))
                if near:
                    s = s + tile_bias(k0)
                s = mask_tile(k0, s, near)
                s_scr[kt] = s
                mx_scr[...] = jnp.maximum(mx_scr[...], _fold_lanes(s, jnp.maximum))
                return carry
            return body

        kt_mid = jnp.maximum(kt_far, kt_lo)
        lax.fori_loop(kt_lo, kt_mid, scores(False), 0)
        lax.fori_loop(kt_mid, kt_hi, scores(True), 0)
        m = jnp.max(mx_scr[...], axis=1, keepdims=True)
        mx_scr[...] = jnp.broadcast_to(jnp.where(m > NEG_INF, m, 0.0), mx_scr.shape)
        l_scr[...] = jnp.zeros(l_scr.shape, F32)
        acc_scr[...] = jnp.zeros(acc_scr.shape, F32)

        def probs(kt, carry):
            k0 = pl.multiple_of(kt * TK, TK)
            m_rep = mx_scr[...]
            p = jnp.exp(s_scr[kt] - jnp.concatenate([m_rep] * (TK // LANES), axis=1))
            l_scr[...] += _fold_lanes(p, jnp.add)
            acc_scr[...] += _dot(p.astype(BF16), v_ref[pl.ds(k0, TK), :].astype(BF16))
            return carry

        lax.fori_loop(kt_lo, kt_hi, probs, 0)
        l = jnp.sum(l_scr[...], axis=1, keepdims=True)
        return acc_scr[...] / jnp.where(l > 0, l, 1.0)

    def sel_mask(k0, s, near):
        s = (s.reshape(GROUP, TQ, TK) + selb_scr[k0 // TK][None]).reshape(rows4, TK)
        return jnp.where(k0 + kidx <= qpos, s, NEG_INF) if near else s

    o_s = attend(ks_ref, vs_ref, 0, sel_mask)

    def win_mask(k0, s, near):
        dist = qpos - (k0 + kidx)
        return jnp.where((dist >= 0) & (dist <= WINDOW), s, NEG_INF)

    o_w = attend(kw_ref, vw_ref, jnp.maximum(q0 - WINDOW, 0) // TK, win_mask)

    gt = gt_ref[...] + gb_ref[...]
    for j in range(GROUP):
        col = GATE_NG + (g * GROUP + j) * 3
        r = slice(j * TQ, (j + 1) * TQ)
        o_ref[:, j * HEAD_DIM:(j + 1) * HEAD_DIM] = (
            _gate(gt, col) * o_c[r] + _gate(gt, col + 1) * o_s[r] + _gate(gt, col + 2) * o_w[r])


def _nsa_prompt(z, kcvc, bias_c, tz, gate_b):
    b, t, _ = z.shape
    ncp = kcvc.shape[2]
    nq = t // TQ
    kv = lambda slot: pl.BlockSpec((None, t, HEAD_DIM), lambda bi, g, qi: (bi, 0, C_NKV // HEAD_DIM + slot * KV_HEADS + g),
                                   pipeline_mode=pl.Buffered(1))
    cmp_blk = lambda slot: pl.BlockSpec((None, None, ncp, HEAD_DIM), lambda bi, g, qi: (bi, slot * KV_HEADS + g, 0, 0))
    return pl.pallas_call(
        functools.partial(_nsa_prompt_kernel, t_len=t),
        grid=(b, KV_HEADS, nq),
        in_specs=[pl.BlockSpec((None, TQ, GROUP * HEAD_DIM), lambda bi, g, qi: (bi, qi, C_NQ // (GROUP * HEAD_DIM) + g)),
                  kv(2), kv(3), kv(4), kv(5), cmp_blk(0), cmp_blk(1),
                  pl.BlockSpec((GROUP, TQ, ncp), lambda bi, g, qi: (g, qi, 0)),
                  pl.BlockSpec((N_TZ, GROUP, TQ, TK), lambda bi, g, qi: (0, g, 0, 0), pipeline_mode=pl.Buffered(1)),
                  pl.BlockSpec((None, TQ, LANES), lambda bi, g, qi: (bi, qi, C_GATE // LANES)),
                  pl.BlockSpec((1, LANES), lambda bi, g, qi: (0, 0))],
        out_specs=pl.BlockSpec((None, TQ, GROUP * HEAD_DIM), lambda bi, g, qi: (bi, qi, g)),
        out_shape=jax.ShapeDtypeStruct((b, t, D_NSA), F32),
        scratch_shapes=[pltpu.VMEM((t // TK, GROUP * TQ, TK), F32), pltpu.VMEM((t // TK, TQ, TK), F32),
                        pltpu.VMEM((GROUP * TQ, LANES), F32),
                        pltpu.VMEM((GROUP * TQ, LANES), F32), pltpu.VMEM((GROUP * TQ, HEAD_DIM), F32)],
        compiler_params=_cparams("parallel", "parallel", "arbitrary"),
        name="nsa_prompt",
    )(z, z, z, z, z, kcvc, kcvc, bias_c, tz, z, gate_b)


TS = SUBLANES
ROWS_S = GROUP * TS


def _nsa_s_cmp_kernel(q_ref, kc_ref, vc_ref, bc_ref, oc_ref, sel_ref, *, pos0, n_cmp, n_sel):
    nch = kc_ref.shape[0]
    qs = _stack_heads(q_ref[...] * QK_SCALE).astype(BF16)
    qpos = pos0 + (lax.broadcasted_iota(jnp.int32, (ROWS_S, 1), 0) & (TS - 1))
    n_idx = lax.broadcasted_iota(jnp.int32, (1, nch), 1)
    s_c = _dot_nt(qs, kc_ref[...].astype(BF16)) + bc_ref[...].reshape(ROWS_S, nch)
    cmask = (qpos - (n_idx * CMP_STRIDE + CMP_BLOCK - 1) >= 0) & (n_idx < n_cmp)
    p_c = _masked_probs(s_c, cmask)
    oc_ref[...] = _dot(p_c.astype(BF16), vc_ref[...].astype(BF16))
    pc = p_c[0:TS] + p_c[TS:2 * TS] + p_c[2 * TS:3 * TS] + p_c[3 * TS:4 * TS]
    sel_ref[...] = _select_blocks_cols(pc, jnp.right_shift(qpos[0:TS], SEL_SHIFT), n_cmp, n_sel)


def _nsa_sample_cmp(z8, kcvc, bias_c, *, pos0, n_cmp, n_sel):
    b = z8.shape[0]
    nch = kcvc.shape[2]
    nsp = -(-n_sel // LANES) * LANES
    cmp_blk = lambda slot: pl.BlockSpec((None, None, nch, HEAD_DIM), lambda bi, g: (bi, slot * KV_HEADS + g, 0, 0))
    return pl.pallas_call(
        functools.partial(_nsa_s_cmp_kernel, pos0=pos0, n_cmp=n_cmp, n_sel=n_sel),
        grid=(b, KV_HEADS),
        in_specs=[pl.BlockSpec((None, TS, GROUP * HEAD_DIM), lambda bi, g: (bi, 0, C_NQ // (GROUP * HEAD_DIM) + g)),
                  cmp_blk(0), cmp_blk(1),
                  pl.BlockSpec((GROUP, TS, nch), lambda bi, g: (g, 0, 0))],
        out_specs=[pl.BlockSpec((None, None, ROWS_S, HEAD_DIM), lambda bi, g: (bi, g, 0, 0)),
                   pl.BlockSpec((None, None, TS, nsp), lambda bi, g: (bi, g, 0, 0))],
        out_shape=[jax.ShapeDtypeStruct((b, KV_HEADS, ROWS_S, HEAD_DIM), F32),
                   jax.ShapeDtypeStruct((b, KV_HEADS, TS, nsp), F32)],
        compiler_params=_cparams("parallel", "parallel"),
        name="nsa_sample_cmp",
    )(z8, kcvc, kcvc, bias_c)


def _pad_rows(x, n):
    return jnp.concatenate([x, jnp.zeros((n - x.shape[0], x.shape[1]), x.dtype)], axis=0)


def _nsa_s_sel_kernel(pt_ref, q_ref, sel_ref, kn_ref, vn_ref, bt_ref, bn_ref, bf_ref, *refs, pos0, npages):
    del pt_ref
    rows_pp = PAGE_SIZE * KV_HEADS
    k_refs = [r.reshape(rows_pp, HEAD_DIM) for r in refs[:SEL_PAGES]]
    v_refs = [r.reshape(rows_pp, HEAD_DIM) for r in refs[SEL_PAGES:2 * SEL_PAGES]]
    o_ref, kbuf, vbuf, m_scr, l_scr, acc_scr = refs[2 * SEL_PAGES:]
    step = pl.program_id(1)
    last = step == npages // SEL_PAGES - 1
    rows = KV_HEADS * ROWS_S
    ncol = SEL_TILE * rows_pp
    head_shift = KV_HEADS.bit_length() - 1
    blk_per_page = PAGE_SIZE // SEL_BLOCK

    @pl.when(step == 0)
    def _():
        _softmax_init(m_scr, l_scr, acc_scr)

    nsp = sel_ref.shape[-1]
    ridx = lax.broadcasted_iota(jnp.int32, (rows, 1), 0)
    qpos = pos0 + (ridx & (TS - 1))
    qs = jnp.concatenate([_stack_heads(q_ref[:, g * GROUP * HEAD_DIM:(g + 1) * GROUP * HEAD_DIM] * QK_SCALE)
                          for g in range(KV_HEADS)], axis=0).astype(BF16)
    sel_rows = jnp.concatenate([sel_ref[g] for g in range(KV_HEADS) for _ in range(GROUP)], axis=0)
    bias_far = bf_ref[...].reshape(rows, PAGE_SIZE)[:, 0:1]
    col = lax.broadcasted_iota(jnp.int32, (1, ncol), 1)
    own_head = (col & (KV_HEADS - 1)) == ridx // ROWS_S
    expand = (jnp.right_shift(lax.broadcasted_iota(jnp.int32, (LANES, ncol), 1), SEL_SHIFT + head_shift)
              == lax.broadcasted_iota(jnp.int32, (LANES, ncol), 0)).astype(BF16)
    blk_r = lax.broadcasted_iota(jnp.int32, (nsp, LANES), 0)
    blk_c = lax.broadcasted_iota(jnp.int32, (nsp, LANES), 1)
    n_tiles = SEL_PAGES // SEL_TILE

    for tile in range(n_tiles):
        for i in range(SEL_TILE):
            p = tile * SEL_TILE + i
            kbuf[i * rows_pp:(i + 1) * rows_pp, :] = k_refs[p][...].astype(BF16)
            vbuf[i * rows_pp:(i + 1) * rows_pp, :] = v_refs[p][...].astype(BF16)
        page0 = step * SEL_PAGES + tile * SEL_TILE
        window = ((blk_r == page0 * blk_per_page + blk_c) & (blk_c < SEL_TILE * blk_per_page)).astype(BF16)
        chosen = _dot(_dot(sel_rows.astype(BF16), window).astype(BF16), expand) > 0.5
        s = _dot_nt(qs, kbuf[...]) + bias_far
        if tile == n_tiles - 1:
            tail = jnp.where(last, bt_ref[...].reshape(rows, rows_pp) - bias_far, 0.0)
            s = s + jnp.concatenate([jnp.zeros((rows, ncol - rows_pp), F32), tail], axis=1)
        key = page0 * PAGE_SIZE + jnp.right_shift(col, head_shift)
        _softmax_update(s, chosen & own_head & (key <= qpos), vbuf[...], m_scr, l_scr, acc_scr)

    @pl.when(last)
    def _():
        kidx = lax.broadcasted_iota(jnp.int32, (1, PAGE_SIZE), 1)
        new_blk = lax.broadcasted_iota(jnp.int32, (1, nsp), 1) == npages * blk_per_page
        flag = jnp.sum(jnp.where(new_blk, sel_rows, 0.0), axis=1, keepdims=True) > 0.5
        bias_new = bn_ref[...].reshape(rows, PAGE_SIZE)
        for g in range(KV_HEADS):
            r = slice(g * ROWS_S, (g + 1) * ROWS_S)
            kn = _pad_rows(kn_ref[:, g * HEAD_DIM:(g + 1) * HEAD_DIM], PAGE_SIZE).astype(BF16)
            vn = _pad_rows(vn_ref[:, g * HEAD_DIM:(g + 1) * HEAD_DIM], PAGE_SIZE).astype(BF16)
            mask = flag[r] & (pos0 + kidx <= qpos[r]) & (kidx < TS)
            _softmax_update(_dot_nt(qs[r], kn) + bias_new[r], mask, vn, m_scr.at[r], l_scr.at[r], acc_scr.at[r])
            o_ref[g] = _softmax_result(l_scr.at[r], acc_scr.at[r])


def _nsa_sample_sel(z8, sel, bias_tail, bias_new, bias_far, cache6, page_table, layer, *, pos0):
    b, npages = page_table.shape
    nsp = sel.shape[-1]
    steps = npages // SEL_PAGES
    slot_cols = KV_HEADS * HEAD_DIM
    new_rows = lambda slot: pl.BlockSpec((None, TS, slot_cols),
                                         lambda bi, si, pt: (bi, 0, (C_NKV + slot * slot_cols) // slot_cols))
    whole = lambda a: pl.BlockSpec(a.shape, lambda bi, si, pt: (0,) * a.ndim)
    grid_spec = pltpu.PrefetchScalarGridSpec(
        num_scalar_prefetch=1,
        grid=(b, steps),
        in_specs=[pl.BlockSpec((None, TS, D_NSA), lambda bi, si, pt: (bi, 0, C_NQ // D_NSA)),
                  pl.BlockSpec((None, KV_HEADS, TS, nsp), lambda bi, si, pt: (bi, 0, 0, 0)),
                  new_rows(2), new_rows(3), whole(bias_tail), whole(bias_new), whole(bias_far)]
                 + [_page_spec(layer, 2, p, SEL_PAGES, 2) for p in range(SEL_PAGES)]
                 + [_page_spec(layer, 3, p, SEL_PAGES, 2) for p in range(SEL_PAGES)],
        out_specs=pl.BlockSpec((None, KV_HEADS, ROWS_S, HEAD_DIM), lambda bi, si, pt: (bi, 0, 0, 0)),
        scratch_shapes=[pltpu.VMEM((SEL_TILE * PAGE_SIZE * KV_HEADS, HEAD_DIM), BF16),
                        pltpu.VMEM((SEL_TILE * PAGE_SIZE * KV_HEADS, HEAD_DIM), BF16),
                        pltpu.VMEM((KV_HEADS * ROWS_S, 1), F32), pltpu.VMEM((KV_HEADS * ROWS_S, 1), F32),
                        pltpu.VMEM((KV_HEADS * ROWS_S, HEAD_DIM), F32)],
    )
    return pl.pallas_call(
        functools.partial(_nsa_s_sel_kernel, pos0=pos0, npages=npages),
        grid_spec=grid_spec,
        out_shape=jax.ShapeDtypeStruct((b, KV_HEADS, ROWS_S, HEAD_DIM), F32),
        compiler_params=_cparams("parallel", "arbitrary"),
        name="nsa_sample_sel",
    )(page_table, z8, sel, z8, z8, bias_tail, bias_new, bias_far, *([cache6] * (2 * SEL_PAGES)))


def _nsa_s_win_kernel(q_ref, kw_ref, vw_ref, kn_ref, vn_ref, bw_ref, oc_ref, os_ref, gt_ref, gb_ref, o_ref,
                      m_scr, l_scr, acc_scr, *, pos0):
    g = pl.program_id(1)
    wb = kw_ref.shape[0]
    own = pl.ds(g, wb, stride=KV_HEADS)
    kw = kw_ref.reshape(wb * KV_HEADS, HEAD_DIM)[own, :]
    vw = vw_ref.reshape(wb * KV_HEADS, HEAD_DIM)[own, :]
    qs = _stack_heads(q_ref[...] * QK_SCALE).astype(BF16)
    qpos = pos0 + (lax.broadcasted_iota(jnp.int32, (ROWS_S, 1), 0) & (TS - 1))
    bias = bw_ref[...].reshape(ROWS_S, wb + PAGE_SIZE)
    _softmax_init(m_scr, l_scr, acc_scr)
    dist = qpos - (pos0 - wb + lax.broadcasted_iota(jnp.int32, (1, wb), 1))
    s = _dot_nt(qs, kw.astype(BF16)) + bias[:, 0:wb]
    _softmax_update(s, (dist >= 0) & (dist <= WINDOW), vw.astype(BF16), m_scr, l_scr, acc_scr)
    kidx = lax.broadcasted_iota(jnp.int32, (1, PAGE_SIZE), 1)
    dist = qpos - (pos0 + kidx)
    s = _dot_nt(qs, _pad_rows(kn_ref[...], PAGE_SIZE).astype(BF16)) + bias[:, wb:wb + PAGE_SIZE]
    _softmax_update(s, (dist >= 0) & (dist <= WINDOW) & (kidx < TS), _pad_rows(vn_ref[...], PAGE_SIZE).astype(BF16),
                    m_scr, l_scr, acc_scr)
    o_w = _softmax_result(l_scr, acc_scr)
    o_c = oc_ref[...]
    o_s = os_ref[...]
    gt = gt_ref[...] + gb_ref[...]
    for j in range(GROUP):
        col = GATE_NG + (g * GROUP + j) * 3
        r = slice(j * TS, (j + 1) * TS)
        o_ref[:, j * HEAD_DIM:(j + 1) * HEAD_DIM] = (
            _gate(gt, col) * o_c[r] + _gate(gt, col + 1) * o_s[r] + _gate(gt, col + 2) * o_w[r])


def _nsa_sample_win(z8, win6, layer, bias_w, o_c, o_s, gate_b, *, pos0):
    _, b, wb = win6.shape[:3]
    cached = lambda slot: pl.BlockSpec((None, None, wb, None, KV_HEADS, HEAD_DIM),
                                       lambda bi, g: (layer, bi, 0, slot, 0, 0))
    zcol = lambda col: (lambda bi, g: (bi, 0, col // HEAD_DIM + g))
    part = pl.BlockSpec((None, None, ROWS_S, HEAD_DIM), lambda bi, g: (bi, g, 0, 0))
    return pl.pallas_call(
        functools.partial(_nsa_s_win_kernel, pos0=pos0),
        grid=(b, KV_HEADS),
        in_specs=[pl.BlockSpec((None, TS, GROUP * HEAD_DIM), lambda bi, g: (bi, 0, C_NQ // (GROUP * HEAD_DIM) + g)),
                  cached(0), cached(1),
                  pl.BlockSpec((None, TS, HEAD_DIM), zcol(C_NKV + 4 * KV_HEADS * HEAD_DIM)),
                  pl.BlockSpec((None, TS, HEAD_DIM), zcol(C_NKV + 5 * KV_HEADS * HEAD_DIM)),
                  pl.BlockSpec((GROUP, TS, wb + PAGE_SIZE), lambda bi, g: (g, 0, 0)),
                  part, part,
                  pl.BlockSpec((None, TS, LANES), lambda bi, g: (bi, 0, C_GATE // LANES)),
                  pl.BlockSpec((1, LANES), lambda bi, g: (0, 0))],
        out_specs=pl.BlockSpec((None, TS, GROUP * HEAD_DIM), lambda bi, g: (bi, 0, g)),
        out_shape=jax.ShapeDtypeStruct((b, TS, D_NSA), F32),
        scratch_shapes=[pltpu.VMEM((ROWS_S, 1), F32), pltpu.VMEM((ROWS_S, 1), F32), pltpu.VMEM((ROWS_S, HEAD_DIM), F32)],
        compiler_params=_cparams("parallel", "parallel"),
        name="nsa_sample_win",
    )(z8, win6, win6, z8, z8, bias_w, o_c, o_s, z8, gate_b)


def _t5_bucket(dist):
    n = np.maximum(dist, 0)
    exact = NUM_BUCKETS // 2
    nf = np.maximum(n, 1).astype(np.float32)
    large = exact + (np.log(nf / np.float32(exact)) / np.float32(math.log(MAX_DISTANCE / exact))
                     * np.float32(NUM_BUCKETS - exact)).astype(np.int32)
    return np.where(n < exact, n, np.minimum(large, NUM_BUCKETS - 1)).astype(np.int32)


FAR_DIST = int(np.max(np.nonzero(_t5_bucket(np.arange(4 * MAX_DISTANCE)) < NUM_BUCKETS - 1)[0])) + 1
N_TZ = (FAR_DIST + TK - 2) // TQ + 1


def _bias_table(rel_bias, dist):
    onehot = jax.nn.one_hot(jnp.asarray(_t5_bucket(dist).astype(np.int8)), NUM_BUCKETS, dtype=F32)
    return jnp.einsum("rcb,bh->hrc", onehot, rel_bias, precision=lax.Precision.HIGHEST)


def _pack_weights(p):
    w_in = p["w_in"].astype(BF16)
    pad = jnp.zeros(w_in.shape[:2] + (N_IN - C_GATE - 32,), BF16)
    w_in_p = jnp.concatenate([w_in[..., 0:2048], w_in[..., 2056:3080], w_in[..., 3080:4616], w_in[..., 4640:5152],
                              w_in[..., 2048:2056], w_in[..., 4616:4640], pad], axis=2)
    return dict(
        w_in=w_in_p, w_out=p["w_out"].astype(BF16), w_down=p["ffn_w_down"].astype(BF16),
        w_up=p["ffn_w_up"].astype(BF16))


def _pack_layer(p, big, l):
    gb = p["mlstm_gate_b"][l]
    mlstm_gate_b = jnp.zeros((1, LANES), F32).at[0, GATE_MI:GATE_MI + 4].set(gb[0]).at[0, GATE_MF:GATE_MF + 4].set(gb[1])
    nsa_gate_b = jnp.zeros((1, LANES), F32).at[0, GATE_NG:GATE_NG + 3 * NSA_HEADS].set(p["nsa_gate_b"][l].reshape(-1))
    w1 = p["cmp_w1"][l]
    wcat = jnp.concatenate([w1[:, 0:CMP_STRIDE], w1[:, CMP_STRIDE:CMP_BLOCK]], axis=-1).astype(BF16)
    lam = lax.complex(p["s5_a_re"][l], p["s5_a_im"][l])
    lam_bar = jnp.exp(lam * jnp.exp(p["s5_log_step"][l])[:, None])
    b_bar = ((lam_bar - 1.0) / lam)[..., None] * lax.complex(p["s5_b_re"][l], p["s5_b_im"][l])
    gi = S5_GROUPS // S5_IN_TILES
    bd_in = lambda m: jnp.einsum("cgph,gk->cghkp", m.reshape(S5_IN_TILES, gi, S5_STATE, S5_GROUP_WIDTH),
                                 jnp.eye(gi, dtype=F32)).reshape(S5_IN_TILES, S5_IN_FEATS, 256)
    wb = jnp.concatenate([bd_in(b_bar.real), bd_in(b_bar.imag)], axis=2).astype(BF16)
    go = S5_GROUPS // S5_OUT_TILES
    bd_out = lambda m: jnp.einsum("jghp,gk->jgpkh", m.reshape(S5_OUT_TILES, go, S5_GROUP_WIDTH, S5_STATE),
                                  jnp.eye(go, dtype=F32)).reshape(S5_OUT_TILES, S5_OUT_CH, 256)
    wc = jnp.stack([bd_out(p["s5_c_re"][l]), -bd_out(p["s5_c_im"][l])]).astype(BF16)
    half = lambda w: _to_tiled_ff(w, FFN_TF)
    conv_w = p["ffn_conv_w"][l]
    conv_b = p["ffn_conv_b"][l][None, :]
    return dict(
        big, layer=l, mlstm_gate_b=mlstm_gate_b, mlstm_norm_w=p["mlstm_norm_w"][l][None, :], nsa_gate_b=nsa_gate_b,
        wcat=wcat, cmp_b1=p["cmp_b1"][l], cmp_w2=p["cmp_w2"][l].astype(BF16),
        s5_wb=wb, s5_wc=wc, s5_lam_r=lam_bar.real.reshape(1, S5_CH), s5_lam_i=lam_bar.imag.reshape(1, S5_CH),
        s5_d=p["s5_d"][l][None, :], s5_glu_w=p["s5_glu_w"][l].astype(BF16), s5_glu_b=p["s5_glu_b"][l][None, :],
        ln1_w=p["ln1_w"][l][None, :], ln1_b=p["ln1_b"][l][None, :],
        conv_w8=jnp.pad(jnp.concatenate([half(conv_w[:, :D_FF]), half(conv_w[:, D_FF:])], axis=1),
                        ((0, SUBLANES - CONV_W), (0, 0))),
        conv_b=jnp.concatenate([half(conv_b[:, :D_FF]), half(conv_b[:, D_FF:])], axis=1),
        ln2_w=p["ln2_w"][l][None, :], ln2_b=p["ln2_b"][l][None, :],
    )


def _unpad_ff(x):
    return jnp.concatenate([_from_tiled_ff(x[..., :D_FFP], FFN_TF), _from_tiled_ff(x[..., D_FFP:], FFN_TF)], axis=-1)


def _pad_ff(x):
    return jnp.concatenate([_to_tiled_ff(x[..., :D_FF], FFN_TF), _to_tiled_ff(x[..., D_FF:], FFN_TF)], axis=-1)


def _mixer_tail(x2d, z, b, t, t_use, lw, o_mlstm, o_nsa, s5_state, conv_state8, tm):
    pad8 = lambda s: jnp.pad(s.reshape(b, S5_CH), ((0, SUBLANES - b), (0, 0)))
    o_s5, xr, xi = _s5(z.reshape(b, t, N_IN), lw["s5_wb"], lw["s5_wc"], lw["s5_lam_r"], lw["s5_lam_i"], lw["s5_d"],
                       lw["s5_glu_w"], lw["s5_glu_b"], pad8(s5_state[0]), pad8(s5_state[1]), lc=S5_LC, steps=t_use)
    o_s5 = o_s5.reshape(b * t, D_S5)
    layer = lw["layer"]
    x1 = _wout_ln(o_mlstm.reshape(b * t, D_MLSTM), o_nsa.reshape(b * t, D_NSA), o_s5, x2d,
                  lw["w_out"], layer, lw["ln1_w"], lw["ln1_b"], tm=512)
    s5_new = (xr[:b].reshape(b, S5_GROUPS, S5_STATE), xi[:b].reshape(b, S5_GROUPS, S5_STATE))
    if conv_state8 is None:
        x2, tail_a, tail_g = _ffn_prompt(x1.reshape(b, t, D_MODEL), lw["w_up"], lw["w_down"], layer, lw["conv_w8"],
                                         lw["conv_b"], lw["ln2_w"], lw["ln2_b"], tt=FFN_TT, tf=FFN_TF)
        keep = slice(SUBLANES - (CONV_W - 1), SUBLANES)
        conv_new = jnp.concatenate([_from_tiled_ff(tail_a[:, keep], FFN_TF), _from_tiled_ff(tail_g[:, keep], FFN_TF)],
                                   axis=-1)
        return x2.reshape(b * t, D_MODEL), s5_new, conv_new
    up = _matmul(x1, lw["w_up"], layer, tm=tm, tn=FFN_TF, n=2 * D_FFP, w_cols=_up_cols(FFN_TF))
    hgate = _convgate(up.reshape(b, t, 2 * D_FFP), conv_state8, lw["conv_w8"], lw["conv_b"], tf=FFN_TF)
    x2 = _down_ln(hgate.reshape(b * t, D_FFP), lw["w_down"], layer, x1, lw["ln2_w"], lw["ln2_b"], tm=512, tk=FFN_TF)
    conv_new = _unpad_ff(up.reshape(b, t, 2 * D_FFP)[:, t_use - (CONV_W - 1):t_use])
    return x2, s5_new, conv_new


def _prompt_layer(x2d, b, t, lw, rel_bias, rows_so_far):
    z, kv_rows, win_rows = _proj_in(x2d, lw["w_in"], lw["layer"], rows_so_far, tm=min(1024, b * t), tn=768)
    z3 = z.reshape(b, t, N_IN)
    zeros = lambda *s: jnp.zeros(s, F32)
    o_mlstm, c1, n1, m1 = _mlstm(z3, lw["mlstm_gate_b"], lw["mlstm_norm_w"],
                                 zeros(b, MLSTM_HEADS, HEAD_DIM, HEAD_DIM), zeros(b, MLSTM_HEADS, LANES),
                                 zeros(b, MLSTM_HEADS, LANES), L=MLSTM_L, lin=MLSTM_L, t_valid=t)
    kcvc = _cmp_finish(_cmp_project_prompt(z3, lw["wcat"]), lw["cmp_b1"], lw["cmp_w2"])
    ncp = t // CMP_STRIDE
    bias_c = _bias_table(rel_bias, np.arange(t)[:, None] - (np.arange(ncp) * CMP_STRIDE + CMP_BLOCK - 1)[None, :])
    ti = np.arange(TQ)[:, None] - np.arange(TK)[None, :]
    far = _bias_table(rel_bias, np.full((1, 1), FAR_DIST))
    tz = jnp.stack([_bias_table(rel_bias, d * TQ + ti) - far for d in range(N_TZ)])
    o_nsa = _nsa_prompt(z3, kcvc, bias_c, tz, lw["nsa_gate_b"])
    x2, s5_new, conv_new = _mixer_tail(
        x2d, z, b, t, t, lw, o_mlstm, o_nsa, (zeros(b, S5_GROUPS, S5_STATE), zeros(b, S5_GROUPS, S5_STATE)),
        None, tm=1024)
    state = (None, None, c1, n1, m1[:, :, 0], s5_new[0], s5_new[1], conv_new)
    return x2, state, (kv_rows, win_rows)


def _sample_layer(x2d, b, tn, lw, rel_bias, layer, cache6, page_table, win6, mlstm_state, s5_state, conv_state):
    npages = page_table.shape[1]
    pos0 = npages * PAGE_SIZE
    z = _matmul(x2d, lw["w_in"], lw["layer"], tm=1024, tn=768)
    z8 = z.reshape(b, TS, N_IN)
    c0, n0, m0 = mlstm_state
    o_mlstm, c1, n1, m1 = _mlstm(z8, lw["mlstm_gate_b"], lw["mlstm_norm_w"], c0, n0,
                                 jnp.broadcast_to(m0[:, :, None], (b, MLSTM_HEADS, LANES)),
                                 L=LANES, lin=TS, t_valid=tn)
    n_chunks = (pos0 + tn) // CMP_STRIDE
    n_cmp = n_chunks - CMP_BLOCK // CMP_STRIDE + 1
    n_sel = -(-(pos0 + tn) // SEL_BLOCK)
    kcvc = _cmp_finish(_cmp_project_sample(cache6, page_table, layer, lw["wcat"]), lw["cmp_b1"], lw["cmp_w2"])
    qpos = pos0 + np.arange(TS)[:, None]
    bias_c = _bias_table(rel_bias, qpos - (np.arange(n_chunks) * CMP_STRIDE + CMP_BLOCK - 1)[None, :])
    o_c, sel = _nsa_sample_cmp(z8, kcvc, bias_c, pos0=pos0, n_cmp=n_cmp, n_sel=n_sel)
    kk = np.arange(PAGE_SIZE)[None, :]
    bias_tail = _bias_table(rel_bias, np.repeat(qpos - (pos0 - PAGE_SIZE + kk), KV_HEADS, axis=1))
    bias_new = _bias_table(rel_bias, qpos - (pos0 + kk))
    bias_far = _bias_table(rel_bias, np.broadcast_to(qpos - (pos0 - 2 * PAGE_SIZE), (TS, PAGE_SIZE)))
    o_s = _nsa_sample_sel(z8, sel, bias_tail, bias_new, bias_far, cache6, page_table, layer, pos0=pos0)
    wb = win6.shape[2]
    wk = np.arange(wb)[None, :]
    bias_w = _bias_table(rel_bias, np.concatenate([qpos - (pos0 - wb + wk), qpos - (pos0 + kk)], axis=1))
    o_nsa = _nsa_sample_win(z8, win6, layer, bias_w, o_c, o_s, lw["nsa_gate_b"], pos0=pos0)
    conv_state8 = jnp.pad(_pad_ff(conv_state), ((0, 0), (SUBLANES - (CONV_W - 1), 0), (0, 0)))
    x2, s5_new, conv_new = _mixer_tail(x2d, z, b, TS, tn, lw, o_mlstm, o_nsa, s5_state, conv_state8, tm=1024)
    nkv = z8[:, :tn, C_NKV:C_NKV + N_KV_SLOTS * KV_HEADS * HEAD_DIM].reshape(b, tn, N_KV_SLOTS, KV_HEADS, HEAD_DIM)
    state = (nkv[:, :, :4], nkv[:, :, 4:], c1, n1, m1[:, :, 0], s5_new[0], s5_new[1], conv_new)
    return x2, state


def kernel(x_prompt, x_sample, cache_nsa_kv, cache_win_kv, state_mlstm_c, state_mlstm_n, state_mlstm_m,
           state_s5_re, state_s5_im, state_ffn_conv, page_table, w_in, mlstm_gate_b, mlstm_norm_w,
           nsa_gate_b, cmp_w1, cmp_b1, cmp_w2, rel_bias, s5_a_re, s5_a_im, s5_b_re, s5_b_im, s5_c_re,
           s5_c_im, s5_d, s5_log_step, s5_glu_w, s5_glu_b, w_out, ln1_w, ln1_b, ffn_w_up, ffn_conv_w,
           ffn_conv_b, ffn_w_down, ln2_w, ln2_b):
    params = dict(w_in=w_in, mlstm_gate_b=mlstm_gate_b, mlstm_norm_w=mlstm_norm_w, nsa_gate_b=nsa_gate_b,
                  cmp_w1=cmp_w1, cmp_b1=cmp_b1, cmp_w2=cmp_w2, s5_a_re=s5_a_re, s5_a_im=s5_a_im, s5_b_re=s5_b_re,
                  s5_b_im=s5_b_im, s5_c_re=s5_c_re, s5_c_im=s5_c_im, s5_d=s5_d, s5_log_step=s5_log_step,
                  s5_glu_w=s5_glu_w, s5_glu_b=s5_glu_b, w_out=w_out, ln1_w=ln1_w, ln1_b=ln1_b, ffn_w_up=ffn_w_up,
                  ffn_conv_w=ffn_conv_w, ffn_conv_b=ffn_conv_b, ffn_w_down=ffn_w_down, ln2_w=ln2_w, ln2_b=ln2_b)
    depth = w_in.shape[0]
    bp, tp, _ = x_prompt.shape
    bs, tn, _ = x_sample.shape
    assert tp % TK == 0 and tn < CMP_STRIDE and tn <= TS and SUBLANES % bp == 0 and SUBLANES % bs == 0
    assert (tn * bs) % SUBLANES == 0 and tp % min(FFN_TT, tp) == 0
    assert page_table.shape[1] % CMP_PAGES == 0 and cache_nsa_kv.shape[2] == PAGE_SIZE
    xp = x_prompt.reshape(bp * tp, D_MODEL)
    xs = jnp.pad(x_sample, ((0, 0), (0, TS - tn), (0, 0))).reshape(bs * TS, D_MODEL)
    p_states, s_states = [], []
    big = _pack_weights(params)
    rows_so_far = None
    for l in range(depth):
        lw = _pack_layer(params, big, l)
        xp, sp, rows_so_far = _prompt_layer(xp, bp, tp, lw, rel_bias, rows_so_far)
        xs, ss = _sample_layer(xs, bs, tn, lw, rel_bias, l, cache_nsa_kv, page_table, cache_win_kv,
                               (state_mlstm_c[l], jnp.pad(state_mlstm_n[l], ((0, 0), (0, 0), (0, LANES - HEAD_DIM))),
                                state_mlstm_m[l]),
                               (state_s5_re[l], state_s5_im[l]), state_ffn_conv[l])
        p_states.append(sp)
        s_states.append(ss)
    stk = lambda states, i: jnp.stack([s[i] for s in states])
    y_prompt = xp.reshape(bp, tp, D_MODEL)
    y_sample = xs.reshape(bs, TS, D_MODEL)[:, :tn]
    wrows = min(WINDOW, tp)
    nsa_kv_p = rows_so_far[0].reshape(depth, bp, tp, 4, KV_HEADS, HEAD_DIM)
    win_kv_p = rows_so_far[1].reshape(depth, bp, tp, 2, KV_HEADS, HEAD_DIM)[:, :, tp - wrows:]
    return (y_prompt, y_sample,
            nsa_kv_p, stk(s_states, 0), win_kv_p, stk(s_states, 1),
            stk(p_states, 2), stk(s_states, 2), stk(p_states, 3), stk(s_states, 3), stk(p_states, 4), stk(s_states, 4),
            stk(p_states, 5), stk(s_states, 5), stk(p_states, 6), stk(s_states, 6), stk(p_states, 7), stk(s_states, 7))
```

```python
import functools
import math

import jax
import jax.numpy as jnp
import numpy as np
from jax import lax
from jax.experimental import pallas as pl
from jax.experimental.pallas import tpu as pltpu

F32 = jnp.float32
BF16 = jnp.bfloat16
NEG_INF = float("-inf")
M_INIT = -1e30
MASKED = -1e30

D_MODEL = 2048
PAGE_SIZE = 128
D_MLSTM = D_MODEL // 4
D_NSA = D_MODEL // 2
D_S5 = D_MODEL - D_MLSTM - D_NSA
HEAD_DIM = 128
MLSTM_HEADS = D_MLSTM // HEAD_DIM
NSA_HEADS = D_NSA // HEAD_DIM
KV_HEADS = 2
GROUP = NSA_HEADS // KV_HEADS
N_KV_SLOTS = 6
CMP_BLOCK = 32
CMP_STRIDE = 16
SEL_BLOCK = 64
SEL_TOPK = 16
FORCE_SCORE = 1e4
WINDOW = 512
S5_GROUP_WIDTH = 16
S5_GROUPS = D_S5 // S5_GROUP_WIDTH
S5_STATE = 64
S5_CH = S5_GROUPS * S5_STATE
NUM_BUCKETS = 32
MAX_DISTANCE = 128
D_FF = ((8 * D_MODEL // 3 + 127) // 128) * 128
CONV_W = 3
LN_EPS = 1e-5
DEPTH = 2
DEEPNORM_ALPHA = (2 * DEPTH) ** 0.25
QK_SCALE = HEAD_DIM ** -0.5

LANES = 128
SUBLANES = 8
V7X_VMEM_LIMIT = 56 * 2 ** 20

C_MQ, C_MK, C_MV, C_MO = 0, 512, 1024, 1536
C_NQ = 2048
C_NKV = 3072
C_SU = 4608
C_GATE = 5120
N_IN = 5376
GATE_MI, GATE_MF, GATE_NG = 0, 4, 8
D_FFP = 5632
MLSTM_L = 512
TQ = 512
TK = 512
S5_LC = 256
FFN_TT = 1024
FFN_TF = 512
FFN_HALO = 16
CMP_PAGES = 64
SEL_PAGES = 64
SEL_TILE = 32
SEL_SHIFT = SEL_BLOCK.bit_length() - 1


def _cparams(*sem):
    return pltpu.CompilerParams(dimension_semantics=sem, vmem_limit_bytes=V7X_VMEM_LIMIT)


def _dot(a, b):
    return jnp.dot(a, b, preferred_element_type=F32)


def _dot_nt(a, b):
    return lax.dot_general(a, b, (((1,), (1,)), ((), ())), preferred_element_type=F32)


def _layer_norm(y, w, b):
    mu = jnp.mean(y, axis=-1, keepdims=True)
    d = y - mu
    var = jnp.mean(d * d, axis=-1, keepdims=True)
    return d * lax.rsqrt(var + LN_EPS) * w + b


def _mm_kernel(x_ref, w_ref, o_ref):
    o_ref[...] = _dot(x_ref[...].astype(BF16), w_ref[...]).astype(o_ref.dtype)


FF_OVER = D_FFP - D_FF


def _ff_start(f, tf):
    nf, per = D_FFP // tf, tf // LANES
    return (f * per - (f == nf - 1).astype(jnp.int32) * (FF_OVER // LANES)) * LANES


def _up_cols(tf):
    nf, per = D_FFP // tf, tf // LANES
    return lambda j: (j * per + (j >= nf).astype(jnp.int32) * (D_FF // LANES - nf * per)
                      - ((j == nf - 1) | (j == 2 * nf - 1)).astype(jnp.int32) * (FF_OVER // LANES)) * LANES


def _to_tiled_ff(x, tf):
    cut = D_FFP - tf
    zeros = jnp.zeros(x.shape[:-1] + (FF_OVER,), x.dtype)
    return jnp.concatenate([x[..., :cut], zeros, x[..., cut:]], axis=-1)


def _from_tiled_ff(x, tf):
    cut = D_FFP - tf
    return jnp.concatenate([x[..., :cut], x[..., cut + FF_OVER:]], axis=-1)


def _matmul(x, w, layer, *, tm, tn, n=None, w_cols=None):
    m, k = x.shape
    n = w.shape[2] if n is None else n
    tm = min(tm, m)
    if w_cols is None:
        w_spec = pl.BlockSpec((None, k, tn), lambda i, j: (layer, 0, j))
    else:
        w_spec = pl.BlockSpec((None, pl.Element(k), pl.Element(tn)), lambda i, j: (layer, 0, w_cols(j)))
    return pl.pallas_call(
        _mm_kernel,
        grid=(m // tm, n // tn),
        in_specs=[pl.BlockSpec((tm, k), lambda i, j: (i, 0)), w_spec],
        out_specs=pl.BlockSpec((tm, tn), lambda i, j: (i, j)),
        out_shape=jax.ShapeDtypeStruct((m, n), F32),
        compiler_params=_cparams("parallel", "parallel"),
        name="proj",
    )(x, w)


KV4_GROUPS = 4 * KV_HEADS
WIN_GROUPS = 2 * KV_HEADS


def _proj_in_kernel(x_ref, w_ref, z_ref, kv_ref, win_ref, xb_ref, *, tn):
    j = pl.program_id(1)

    @pl.when(j == 0)
    def _():
        xb_ref[...] = x_ref[...].astype(BF16)

    acc = _dot(xb_ref[...], w_ref[...])
    z_ref[...] = acc
    tm = acc.shape[0]
    per_tile = tn // HEAD_DIM
    first_tile = C_NKV // tn
    for tile in range(first_tile, first_tile + N_KV_SLOTS * KV_HEADS // per_tile):
        @pl.when(j == tile)
        def _(tile=tile):
            for c in range(per_tile):
                grp = (tile - first_tile) * per_tile + c
                val = acc[:, c * HEAD_DIM:(c + 1) * HEAD_DIM]
                if grp < KV4_GROUPS:
                    kv_ref[pl.ds(grp, tm, stride=KV4_GROUPS), :] = val
                else:
                    win_ref[pl.ds(grp - KV4_GROUPS, tm, stride=WIN_GROUPS), :] = val


def _proj_in_carry_kernel(x_ref, w_ref, kv_in_ref, win_in_ref, z_ref, kv_ref, win_ref, xb_ref, *, tn):
    del kv_in_ref, win_in_ref
    _proj_in_kernel(x_ref, w_ref, z_ref, kv_ref, win_ref, xb_ref, tn=tn)


def _proj_in(x, w, layer, rows_so_far, *, tm, tn):
    m, k = x.shape
    depth, _, n = w.shape
    nm = m // tm
    assert C_NKV % tn == 0 and (N_KV_SLOTS * KV_HEADS * HEAD_DIM) % tn == 0
    carried = () if rows_so_far is None else tuple(rows_so_far)
    return pl.pallas_call(
        functools.partial(_proj_in_carry_kernel if carried else _proj_in_kernel, tn=tn),
        grid=(nm, n // tn),
        in_specs=[pl.BlockSpec((tm, k), lambda i, j: (i, 0)), pl.BlockSpec((None, k, tn), lambda i, j: (layer, 0, j))]
                 + [pl.BlockSpec(memory_space=pl.ANY)] * len(carried),
        out_specs=[pl.BlockSpec((tm, tn), lambda i, j: (i, j)),
                   pl.BlockSpec((tm * KV4_GROUPS, HEAD_DIM), lambda i, j: (layer * nm + i, 0)),
                   pl.BlockSpec((tm * WIN_GROUPS, HEAD_DIM), lambda i, j: (layer * nm + i, 0))],
        out_shape=[jax.ShapeDtypeStruct((m, n), F32),
                   jax.ShapeDtypeStruct((depth * m * KV4_GROUPS, HEAD_DIM), F32),
                   jax.ShapeDtypeStruct((depth * m * WIN_GROUPS, HEAD_DIM), F32)],
        input_output_aliases={2: 1, 3: 2} if carried else {},
        scratch_shapes=[pltpu.VMEM((tm, k), BF16)],
        compiler_params=_cparams("parallel", "arbitrary"),
        name="proj_in",
    )(x, w, *carried)


def _wout_kernel(om_ref, on_ref, os_ref, x_ref, w_ref, lw_ref, lb_ref, o_ref):
    acc = _dot(om_ref[...].astype(BF16), w_ref[0:D_MLSTM, :])
    acc += _dot(on_ref[...].astype(BF16), w_ref[D_MLSTM:D_MLSTM + D_NSA, :])
    acc += _dot(os_ref[...].astype(BF16), w_ref[D_MLSTM + D_NSA:D_MODEL, :])
    o_ref[...] = _layer_norm(DEEPNORM_ALPHA * x_ref[...] + acc, lw_ref[...], lb_ref[...])


def _wout_ln(om, on, os_, x, w, layer, lw, lb, *, tm):
    m = x.shape[0]
    tm = min(tm, m)
    row = lambda width: pl.BlockSpec((tm, width), lambda i: (i, 0))
    full = lambda a: pl.BlockSpec(a.shape, lambda i: (0, 0))
    w_spec = pl.BlockSpec((None,) + w.shape[1:], lambda i: (layer, 0, 0))
    return pl.pallas_call(
        _wout_kernel,
        grid=(m // tm,),
        in_specs=[row(D_MLSTM), row(D_NSA), row(D_S5), row(D_MODEL), w_spec, full(lw), full(lb)],
        out_specs=row(D_MODEL),
        out_shape=jax.ShapeDtypeStruct((m, D_MODEL), F32),
        compiler_params=_cparams("parallel"),
        name="wout_ln",
    )(om, on, os_, x, w, lw, lb)


def _down_kernel(h_ref, w_ref, x_ref, lw_ref, lb_ref, o_ref, acc_ref, *, nk):
    k = pl.program_id(1)

    @pl.when(k == 0)
    def _():
        acc_ref[...] = jnp.zeros_like(acc_ref)

    acc_ref[...] += _dot(h_ref[...], w_ref[...])

    @pl.when(k == nk - 1)
    def _():
        o_ref[...] = _layer_norm(DEEPNORM_ALPHA * x_ref[...] + acc_ref[...], lw_ref[...], lb_ref[...])


def _down_ln(h, w, layer, x, lw, lb, *, tm, tk):
    m, kk = h.shape
    tm = min(tm, m)
    nk = kk // tk
    return pl.pallas_call(
        functools.partial(_down_kernel, nk=nk),
        grid=(m // tm, nk),
        in_specs=[
            pl.BlockSpec((tm, tk), lambda i, k: (i, k)),
            pl.BlockSpec((None, pl.Element(tk), pl.Element(D_MODEL)), lambda i, k: (layer, _ff_start(k, tk), 0)),
            pl.BlockSpec((tm, D_MODEL), lambda i, k: (i, 0)),
            pl.BlockSpec((1, D_MODEL), lambda i, k: (0, 0)),
            pl.BlockSpec((1, D_MODEL), lambda i, k: (0, 0)),
        ],
        out_specs=pl.BlockSpec((tm, D_MODEL), lambda i, k: (i, 0)),
        out_shape=jax.ShapeDtypeStruct((m, D_MODEL), F32),
        scratch_shapes=[pltpu.VMEM((tm, D_MODEL), F32)],
        compiler_params=_cparams("parallel", "arbitrary"),
        name="down_ln",
    )(h, w, x, lw, lb)


def _convgate_kernel(a_ref, g_ref, sa_ref, sg_ref, wa_ref, wg_ref, ba_ref, bg_ref, o_ref):
    rows = lax.broadcasted_iota(jnp.int32, (1, SUBLANES, 1), 1)

    def conv(cur_ref, st_ref, w_ref, b_ref):
        cur = cur_ref[...]
        prev = st_ref[...]
        p1 = prev[:, 7:8, :]
        p2 = prev[:, 6:7, :]
        x1 = jnp.where(rows == 0, p1, pltpu.roll(cur, 1, 1))
        x2 = jnp.where(rows == 0, p2, jnp.where(rows == 1, p1, pltpu.roll(cur, 2, 1)))
        w = w_ref[...]
        return b_ref[...] + w[0:1, :] * x2 + w[1:2, :] * x1 + w[2:3, :] * cur

    a = conv(a_ref, sa_ref, wa_ref, ba_ref)
    g = conv(g_ref, sg_ref, wg_ref, bg_ref)
    o_ref[...] = (a * jax.nn.sigmoid(a) * g).astype(o_ref.dtype)


def _convgate(up, state8, conv_w8, conv_b, *, tf):
    b, t, _ = up.shape
    assert t == SUBLANES
    nf = D_FFP // tf
    cur_a = pl.BlockSpec((b, t, tf), lambda fi: (0, 0, fi))
    cur_g = pl.BlockSpec((b, t, tf), lambda fi: (0, 0, fi + nf))
    w_a = pl.BlockSpec((SUBLANES, tf), lambda fi: (0, fi))
    w_g = pl.BlockSpec((SUBLANES, tf), lambda fi: (0, fi + nf))
    b_a = pl.BlockSpec((1, tf), lambda fi: (0, fi))
    b_g = pl.BlockSpec((1, tf), lambda fi: (0, fi + nf))
    return pl.pallas_call(
        _convgate_kernel,
        grid=(nf,),
        in_specs=[cur_a, cur_g, cur_a, cur_g, w_a, w_g, b_a, b_g],
        out_specs=pl.BlockSpec((b, t, tf), lambda fi: (0, 0, fi)),
        out_shape=jax.ShapeDtypeStruct((b, t, D_FFP), BF16),
        compiler_params=_cparams("parallel"),
        name="convgate",
    )(up, up, state8, state8, conv_w8, conv_w8, conv_b, conv_b)


def _ffn_kernel(x_ref, xh_ref, wa_ref, wg_ref, wd_ref, cwa_ref, cwg_ref, cba_ref, cbg_ref, lw_ref, lb_ref,
                o_ref, sa_ref, sg_ref, xe_ref, *, nf):
    first = pl.program_id(1) == 0
    f = pl.program_id(2)
    tt = x_ref.shape[0]

    @pl.when(f == 0)
    def _():
        o_ref[...] = jnp.zeros_like(o_ref)
        halo = jnp.where(first, 0.0, xh_ref[...])
        xe_ref[0:FFN_HALO, :] = halo.astype(BF16)
        xe_ref[FFN_HALO:, :] = x_ref[...].astype(BF16)

    xe = xe_ref[...]

    def branch(w_ref, cw_ref, cb_ref, s_ref):
        up = _dot(xe, w_ref[...])
        s_ref[...] = up[FFN_HALO + tt - SUBLANES:FFN_HALO + tt, :]
        w = cw_ref[...]
        x1 = pltpu.roll(up, 1, 0)[FFN_HALO:, :]
        x2 = pltpu.roll(up, 2, 0)[FFN_HALO:, :]
        return cb_ref[...] + w[0:1, :] * x2 + w[1:2, :] * x1 + w[2:3, :] * up[FFN_HALO:, :]

    a = branch(wa_ref, cwa_ref, cba_ref, sa_ref)
    g = branch(wg_ref, cwg_ref, cbg_ref, sg_ref)
    o_ref[...] += _dot((a * jax.nn.sigmoid(a) * g).astype(BF16), wd_ref[...])

    @pl.when(f == nf - 1)
    def _():
        o_ref[...] = _layer_norm(DEEPNORM_ALPHA * x_ref[...] + o_ref[...], lw_ref[...], lb_ref[...])


def _ffn_prompt(x3, w_up, w_down, layer, conv_w8, conv_b, lw, lb, *, tt, tf):
    b, t, _ = x3.shape
    nf = D_FFP // tf
    tt = min(tt, t)
    hb = tt // FFN_HALO
    half_a = lambda rows: pl.BlockSpec((rows, tf), lambda bi, ti, fi: (0, fi))
    half_g = lambda rows: pl.BlockSpec((rows, tf), lambda bi, ti, fi: (0, fi + nf))
    cols = _up_cols(tf)
    w_a = pl.BlockSpec((None, pl.Element(D_MODEL), pl.Element(tf)), lambda bi, ti, fi: (layer, 0, cols(fi)))
    w_g = pl.BlockSpec((None, pl.Element(D_MODEL), pl.Element(tf)), lambda bi, ti, fi: (layer, 0, cols(fi + nf)))
    vec = pl.BlockSpec((1, D_MODEL), lambda bi, ti, fi: (0, 0))
    tail = pl.BlockSpec((None, SUBLANES, tf), lambda bi, ti, fi: (bi, 0, fi))
    return pl.pallas_call(
        functools.partial(_ffn_kernel, nf=nf),
        grid=(b, t // tt, nf),
        in_specs=[pl.BlockSpec((None, tt, D_MODEL), lambda bi, ti, fi: (bi, ti, 0), pipeline_mode=pl.Buffered(1)),
                  pl.BlockSpec((None, FFN_HALO, D_MODEL), lambda bi, ti, fi: (bi, jnp.maximum(ti * hb - 1, 0), 0)),
                  w_a, w_g,
                  pl.BlockSpec((None, pl.Element(tf), pl.Element(D_MODEL)),
                               lambda bi, ti, fi: (layer, _ff_start(fi, tf), 0)),
                  half_a(SUBLANES), half_g(SUBLANES), half_a(1), half_g(1), vec, vec],
        out_specs=[pl.BlockSpec((None, tt, D_MODEL), lambda bi, ti, fi: (bi, ti, 0)), tail, tail],
        out_shape=[jax.ShapeDtypeStruct((b, t, D_MODEL), F32),
                   jax.ShapeDtypeStruct((b, SUBLANES, D_FFP), F32),
                   jax.ShapeDtypeStruct((b, SUBLANES, D_FFP), F32)],
        scratch_shapes=[pltpu.VMEM((tt + FFN_HALO, D_MODEL), BF16)],
        compiler_params=_cparams("parallel", "arbitrary", "arbitrary"),
        name="ffn_prompt",
    )(x3, x3, w_up, w_up, w_down, conv_w8, conv_w8, conv_b, conv_b, lw, lb)


def _cumsum_rows(x):
    n = x.shape[0]
    rows = lax.broadcasted_iota(jnp.int32, (n, 1), 0)
    d = 1
    while d < n:
        x = x + jnp.where(rows >= d, pltpu.roll(x, d, 0), 0.0)
        d *= 2
    return x


def _log_sigmoid(x):
    return jnp.minimum(x, 0.0) - jnp.log1p(jnp.exp(-jnp.abs(x)))


def _mlstm_kernel(q_ref, k_ref, v_ref, og_ref, g_ref, gb_ref, nw_ref, c0_ref, n0_ref, m0_ref,
                  out_ref, c_ref, n_ref, m_ref, *, L, t_valid):
    ci = pl.program_id(1)
    lin = q_ref.shape[0]

    @pl.when(ci == 0)
    def _():
        c_ref[...] = c0_ref[...]
        n_ref[...] = n0_ref[...]
        m_ref[...] = m0_ref[...]

    def rows_of(ref):
        x = ref[...]
        if lin < L:
            x = jnp.concatenate([x, jnp.zeros((L - lin, x.shape[1]), x.dtype)], axis=0)
        return x

    rows = lax.broadcasted_iota(jnp.int32, (L, 1), 0)
    valid = (ci * L + rows) < t_valid
    pre = rows_of(g_ref) + gb_ref[...]
    lf = jnp.where(valid, _log_sigmoid(pre), 0.0)
    ig = jnp.where(valid, pre, NEG_INF)
    bcum = _cumsum_rows(lf)
    dt = (pltpu.roll(ig, GATE_MF - GATE_MI, 1) - bcum).T
    q_all, k_all, v_all, og_all = rows_of(q_ref), rows_of(k_ref), rows_of(v_ref), rows_of(og_ref)
    tri = lax.broadcasted_iota(jnp.int32, (L, L), 0) >= lax.broadcasted_iota(jnp.int32, (L, L), 1)
    nw = nw_ref[...]

    for h in range(MLSTM_HEADS):
        sl = slice(h * HEAD_DIM, (h + 1) * HEAD_DIM)
        q = q_all[:, sl]
        k = k_all[:, sl] * QK_SCALE
        v = v_all[:, sl]
        qb, kb, vb = q.astype(BF16), k.astype(BF16), v.astype(BF16)
        b_col = bcum[:, GATE_MF + h:GATE_MF + h + 1]
        ig_col = ig[:, GATE_MI + h:GATE_MI + h + 1]
        d_row = dt[GATE_MF + h:GATE_MF + h + 1, :]
        c_prev = c_ref[h]
        n_prev = n_ref[h:h + 1, :]
        m_prev = m_ref[h:h + 1, 0:1]

        dmat = jnp.where(tri, b_col + d_row, NEG_INF)
        g_col = b_col + m_prev
        m_row = jnp.maximum(jnp.max(dmat, axis=1, keepdims=True), g_col)
        a = jnp.exp(dmat - m_row) * _dot_nt(qb, kb)
        w_inter = jnp.exp(g_col - m_row)
        num = _dot(a.astype(BF16), vb) + w_inter * _dot(qb, c_prev.astype(BF16))
        den = jnp.sum(a, axis=1, keepdims=True) + w_inter * jnp.sum(q * n_prev, axis=1, keepdims=True)
        hid = num / jnp.maximum(jnp.abs(den), jnp.exp(-m_row))
        mu = jnp.mean(hid, axis=1, keepdims=True)
        dlt = hid - mu
        var = jnp.mean(dlt * dlt, axis=1, keepdims=True)
        hn = dlt * lax.rsqrt(var + LN_EPS) * nw[:, sl]
        res = jax.nn.sigmoid(og_all[:, sl]) * hn
        out_ref[:, sl] = res[0:lin, :]

        f_tot = b_col[L - 1:L, :]
        w_s = f_tot - b_col + ig_col
        m_new = jnp.maximum(f_tot + m_prev, jnp.max(w_s, axis=0, keepdims=True))
        ws = jnp.exp(w_s - m_new)
        decay = jnp.exp(f_tot + m_prev - m_new)
        c_ref[h] = decay * c_prev + _dot(k.T.astype(BF16), (ws * v).astype(BF16))
        n_ref[h:h + 1, :] = decay * n_prev + jnp.sum(ws * k, axis=0, keepdims=True)
        m_ref[h:h + 1, :] = jnp.broadcast_to(m_new, (1, LANES))


def _mlstm(z, gate_b, norm_w, c0, n0, m0, *, L, lin, t_valid):
    b, tz, _ = z.shape
    nchunks = tz // lin
    qblk = lambda col: pl.BlockSpec((None, lin, D_MLSTM), lambda bi, ci: (bi, ci, col // D_MLSTM))
    st4 = pl.BlockSpec((None, MLSTM_HEADS, HEAD_DIM, HEAD_DIM), lambda bi, ci: (bi, 0, 0, 0))
    st3 = pl.BlockSpec((None, MLSTM_HEADS, LANES), lambda bi, ci: (bi, 0, 0))
    return pl.pallas_call(
        functools.partial(_mlstm_kernel, L=L, t_valid=t_valid),
        grid=(b, nchunks),
        in_specs=[qblk(C_MQ), qblk(C_MK), qblk(C_MV), qblk(C_MO),
                  pl.BlockSpec((None, lin, LANES), lambda bi, ci: (bi, ci, C_GATE // LANES)),
                  pl.BlockSpec((1, LANES), lambda bi, ci: (0, 0)),
                  pl.BlockSpec((1, D_MLSTM), lambda bi, ci: (0, 0)),
                  st4, st3, st3],
        out_specs=[pl.BlockSpec((None, lin, D_MLSTM), lambda bi, ci: (bi, ci, 0)), st4, st3, st3],
        out_shape=[jax.ShapeDtypeStruct((b, tz, D_MLSTM), F32),
                   jax.ShapeDtypeStruct((b, MLSTM_HEADS, HEAD_DIM, HEAD_DIM), F32),
                   jax.ShapeDtypeStruct((b, MLSTM_HEADS, LANES), F32),
                   jax.ShapeDtypeStruct((b, MLSTM_HEADS, LANES), F32)],
        compiler_params=_cparams("parallel", "arbitrary"),
        name="mlstm",
    )(z, z, z, z, z, gate_b, norm_w, c0, n0, m0)


S5_IN_TILES = S5_CH // 256
S5_IN_FEATS = D_S5 // S5_IN_TILES
S5_OUT_TILES = D_S5 // 256
S5_OUT_CH = S5_CH // S5_OUT_TILES


S5_LT = S5_CH // LANES


def _lane_tiles(x):
    return jnp.stack([x[:, k * LANES:(k + 1) * LANES] for k in range(x.shape[1] // LANES)])


def _lane_untile(x):
    return jnp.concatenate([x[k] for k in range(x.shape[0])], axis=1)


def _s5_kernel(u_ref, wb_ref, wc_ref, lr_ref, li_ref, d_ref, gw_ref, gb_ref, x0r_ref, x0i_ref,
               o_ref, xr_ref, xi_ref, sr_ref, si_ref, y_ref, *, steps):
    nb, lc, _ = u_ref.shape

    @pl.when(pl.program_id(0) == 0)
    def _():
        xr_ref[...] = x0r_ref[...]
        xi_ref[...] = x0i_ref[...]

    for b in range(nb):
        ub = u_ref[b].astype(BF16)
        for c in range(S5_IN_TILES):
            bu = _dot(ub[:, c * S5_IN_FEATS:(c + 1) * S5_IN_FEATS], wb_ref[c])
            for k in range(256 // LANES):
                rows = pl.ds(b, lc, stride=nb)
                sr_ref.at[c * (256 // LANES) + k][rows, :] = bu[:, k * LANES:(k + 1) * LANES]
                si_ref.at[c * (256 // LANES) + k][rows, :] = bu[:, 256 + k * LANES:256 + (k + 1) * LANES]
    lam_r = _lane_tiles(jnp.broadcast_to(lr_ref[...], (SUBLANES, S5_CH)))
    lam_i = _lane_tiles(jnp.broadcast_to(li_ref[...], (SUBLANES, S5_CH)))
    row8 = lax.broadcasted_iota(jnp.int32, (1, SUBLANES, 1), 1)
    per_group = SUBLANES // nb

    def group(i, carry):
        xr, xi = carry
        r0 = pl.multiple_of(i * SUBLANES, SUBLANES)
        br = sr_ref[:, pl.ds(r0, SUBLANES), :]
        bi = si_ref[:, pl.ds(r0, SUBLANES), :]
        out_r, out_i = br, bi
        for k in range(per_group):
            nr = lam_r * xr - lam_i * xi + br
            ni = lam_r * xi + lam_i * xr + bi
            here = (row8 >= k * nb) & (row8 < (k + 1) * nb)
            out_r = jnp.where(here, nr, out_r)
            out_i = jnp.where(here, ni, out_i)
            if per_group > 1:
                xr = pltpu.roll(nr, nb, 1)
                xi = pltpu.roll(ni, nb, 1)
            else:
                xr, xi = nr, ni
        sr_ref[:, pl.ds(r0, SUBLANES), :] = out_r
        si_ref[:, pl.ds(r0, SUBLANES), :] = out_i
        return xr, xi

    xr, xi = lax.fori_loop(0, steps * nb // SUBLANES, group, (_lane_tiles(xr_ref[...]), _lane_tiles(xi_ref[...])),
                           unroll=4)
    xr_ref[...] = _lane_untile(xr)
    xi_ref[...] = _lane_untile(xi)
    per_out = S5_OUT_CH // LANES
    for j in range(S5_OUT_TILES):
        lhs_r = jnp.concatenate([sr_ref[j * per_out + k] for k in range(per_out)], axis=1).astype(BF16)
        lhs_i = jnp.concatenate([si_ref[j * per_out + k] for k in range(per_out)], axis=1).astype(BF16)
        y = _dot(lhs_r, wc_ref[0, j]) + _dot(lhs_i, wc_ref[1, j])
        for k in range(256 // LANES):
            y_ref[j * (256 // LANES) + k] = y[:, k * LANES:(k + 1) * LANES]
    for b in range(nb):
        rows = pl.ds(b, lc, stride=nb)
        y = jnp.concatenate([y_ref.at[k][rows, :] for k in range(D_S5 // LANES)], axis=1) + d_ref[...] * u_ref[b]
        zz = jax.nn.gelu(y)
        o_ref[b] = zz * jax.nn.sigmoid(_dot(zz.astype(BF16), gw_ref[...]) + gb_ref[...])


def _s5(z3, wb, wc, lam_r, lam_i, d, glu_w, glu_b, x0r, x0i, *, lc, steps):
    b, tz, _ = z3.shape
    lc = min(lc, tz)
    steps = min(steps, lc)
    assert steps == lc or tz == lc
    full = lambda a: pl.BlockSpec(a.shape, lambda i: (0,) * a.ndim)
    st = pl.BlockSpec((SUBLANES, S5_CH), lambda i: (0, 0))
    return pl.pallas_call(
        functools.partial(_s5_kernel, steps=steps),
        grid=(tz // lc,),
        in_specs=[pl.BlockSpec((b, lc, D_S5), lambda i: (0, i, C_SU // D_S5)),
                  full(wb), full(wc), full(lam_r), full(lam_i), full(d), full(glu_w), full(glu_b), st, st],
        out_specs=[pl.BlockSpec((b, lc, D_S5), lambda i: (0, i, 0)), st, st],
        out_shape=[jax.ShapeDtypeStruct((b, tz, D_S5), F32),
                   jax.ShapeDtypeStruct((SUBLANES, S5_CH), F32),
                   jax.ShapeDtypeStruct((SUBLANES, S5_CH), F32)],
        scratch_shapes=[pltpu.VMEM((S5_LT, lc * b, LANES), F32), pltpu.VMEM((S5_LT, lc * b, LANES), F32),
                        pltpu.VMEM((D_S5 // LANES, lc * b, LANES), F32)],
        compiler_params=_cparams("arbitrary"),
        name="s5",
    )(z3, wb, wc, lam_r, lam_i, d, glu_w, glu_b, x0r, x0i)


N_CMB = 2 * KV_HEADS


def _cmp_project(rows_of, w_ref, o_ref):
    nch = o_ref.shape[0]
    for cmb in range(N_CMB):
        slot = cmb // KV_HEADS
        rows_ref = rows_of(cmb)
        acc = jnp.zeros((nch, 2 * HEAD_DIM), F32)
        for l in range(CMP_STRIDE):
            x = rows_ref[pl.ds(l, nch, stride=CMP_STRIDE), :]
            acc += _dot(x.astype(BF16), w_ref[slot, l])
        o_ref[:, cmb * 2 * HEAD_DIM:(cmb + 1) * 2 * HEAD_DIM] = acc


def _cmp_p_kernel(r0_ref, r1_ref, r2_ref, r3_ref, w_ref, o_ref):
    rows = (r0_ref, r1_ref, r2_ref, r3_ref)
    _cmp_project(lambda cmb: rows[cmb], w_ref, o_ref)


def _cmp_project_prompt(z, wcat):
    b, t, _ = z.shape
    nch = t // CMP_STRIDE
    rows = lambda cmb: pl.BlockSpec((None, t, HEAD_DIM), lambda bi: (bi, 0, C_NKV // HEAD_DIM + cmb))
    return pl.pallas_call(
        _cmp_p_kernel,
        grid=(b,),
        in_specs=[rows(cmb) for cmb in range(N_CMB)] + [pl.BlockSpec(wcat.shape, lambda bi: (0, 0, 0, 0))],
        out_specs=pl.BlockSpec((None, nch, N_CMB * 2 * HEAD_DIM), lambda bi: (bi, 0, 0)),
        out_shape=jax.ShapeDtypeStruct((b, nch, N_CMB * 2 * HEAD_DIM), F32),
        compiler_params=_cparams("parallel"),
        name="cmp_project_prompt",
    )(z, z, z, z, wcat)


def _cmp_s_kernel(pt_ref, *refs):
    del pt_ref
    per_pos = 2 * KV_HEADS
    page_refs = [r.reshape(PAGE_SIZE * per_pos, HEAD_DIM) for r in refs[:CMP_PAGES]]
    w_ref, o_ref = refs[CMP_PAGES:]
    per_page = PAGE_SIZE // CMP_STRIDE
    for cmb in range(N_CMB):
        acc = jnp.zeros((o_ref.shape[0], 2 * HEAD_DIM), F32)
        for l in range(CMP_STRIDE):
            x = jnp.concatenate(
                [page_refs[p][pl.ds(l * per_pos + cmb, per_page, stride=CMP_STRIDE * per_pos), :]
                 for p in range(CMP_PAGES)], axis=0)
            acc += _dot(x.astype(BF16), w_ref[cmb // KV_HEADS, l])
        o_ref[:, cmb * 2 * HEAD_DIM:(cmb + 1) * 2 * HEAD_DIM] = acc


def _page_spec(layer, slot, p, pages_per_step, grid_rank):
    def index_map(*idx):
        bi, si, pt = idx[0], idx[grid_rank - 1], idx[grid_rank]
        return (layer, pt[bi, si * pages_per_step + p], 0, slot, 0, 0)

    return pl.BlockSpec((None, None, PAGE_SIZE, None, KV_HEADS, HEAD_DIM), index_map)


def _cmp_project_sample(cache6, page_table, layer, wcat):
    b, npages = page_table.shape
    steps = npages // CMP_PAGES
    nch = CMP_PAGES * PAGE_SIZE // CMP_STRIDE
    def page_pair(p):
        return pl.BlockSpec((None, None, PAGE_SIZE, 2, KV_HEADS, HEAD_DIM),
                            lambda bi, si, pt: (layer, pt[bi, si * CMP_PAGES + p], 0, 0, 0, 0))

    pages = [page_pair(p) for p in range(CMP_PAGES)]

    grid_spec = pltpu.PrefetchScalarGridSpec(
        num_scalar_prefetch=1,
        grid=(b, steps),
        in_specs=pages + [pl.BlockSpec(wcat.shape, lambda bi, si, pt: (0, 0, 0, 0))],
        out_specs=pl.BlockSpec((None, nch, N_CMB * 2 * HEAD_DIM), lambda bi, si, pt: (bi, si, 0)),
    )
    return pl.pallas_call(
        _cmp_s_kernel,
        grid_spec=grid_spec,
        out_shape=jax.ShapeDtypeStruct((b, steps * nch, N_CMB * 2 * HEAD_DIM), F32),
        compiler_params=_cparams("parallel", "arbitrary"),
        name="cmp_project_sample",
    )(page_table, *([cache6] * CMP_PAGES), wcat)


def _cmp_fin_kernel(p_ref, b1_ref, w2_ref, o_ref):
    nch = p_ref.shape[0]
    for cmb in range(N_CMB):
        slot = cmb // KV_HEADS
        c0 = cmb * 2 * HEAD_DIM
        first = p_ref[:, c0:c0 + HEAD_DIM]
        second = pltpu.roll(p_ref[:, c0 + HEAD_DIM:c0 + 2 * HEAD_DIM], nch - 1, 0)
        hid = b1_ref[slot:slot + 1, :] + first + second
        o_ref[cmb] = _dot(jax.nn.gelu(hid).astype(BF16), w2_ref[slot])


def _cmp_finish(p, b1, w2):
    b, nch, _ = p.shape
    return pl.pallas_call(
        _cmp_fin_kernel,
        grid=(b,),
        in_specs=[pl.BlockSpec((None, nch, p.shape[2]), lambda bi: (bi, 0, 0)),
                  pl.BlockSpec(b1.shape, lambda bi: (0, 0)),
                  pl.BlockSpec(w2.shape, lambda bi: (0, 0, 0))],
        out_specs=pl.BlockSpec((None, N_CMB, nch, HEAD_DIM), lambda bi: (bi, 0, 0, 0)),
        out_shape=jax.ShapeDtypeStruct((b, N_CMB, nch, HEAD_DIM), F32),
        compiler_params=_cparams("parallel"),
        name="cmp_finish",
    )(p, b1, w2)


def _softmax_init(m_scr, l_scr, acc_scr):
    m_scr[...] = jnp.full(m_scr.shape, M_INIT, F32)
    l_scr[...] = jnp.zeros(l_scr.shape, F32)
    acc_scr[...] = jnp.zeros(acc_scr.shape, F32)


def _softmax_update(s, mask, vb, m_scr, l_scr, acc_scr):
    s = jnp.where(mask, s, NEG_INF)
    m_prev = m_scr[...]
    m_new = jnp.maximum(m_prev, jnp.max(s, axis=1, keepdims=True))
    alpha = jnp.exp(m_prev - m_new)
    p = jnp.exp(s - m_new)
    l_scr[...] = alpha * l_scr[...] + jnp.sum(p, axis=1, keepdims=True)
    acc_scr[...] = alpha * acc_scr[...] + _dot(p.astype(BF16), vb)
    m_scr[...] = m_new


def _softmax_result(l_scr, acc_scr):
    l = l_scr[...]
    return acc_scr[...] / jnp.where(l > 0, l, 1.0)


def _masked_probs(s, mask):
    s = jnp.where(mask, s, NEG_INF)
    m = jnp.max(s, axis=1, keepdims=True)
    m = jnp.where(m > NEG_INF, m, 0.0)
    p = jnp.exp(s - m)
    den = jnp.sum(p, axis=1, keepdims=True)
    return p / jnp.where(den > 0, den, 1.0)


def _stack_heads(q):
    return jnp.concatenate([q[:, j * HEAD_DIM:(j + 1) * HEAD_DIM] for j in range(GROUP)], axis=0)


def _fold_lanes(x, op):
    out = x[:, 0:LANES]
    for k in range(1, x.shape[1] // LANES):
        out = op(out, x[:, k * LANES:(k + 1) * LANES])
    return out


def _block_importance(pc, n_cmp, n_sel):
    ncp = pc.shape[1]
    nsp = -(-n_sel // LANES) * LANES
    c_start = lax.broadcasted_iota(jnp.int32, (ncp, nsp), 0) * CMP_STRIDE
    s_start = lax.broadcasted_iota(jnp.int32, (ncp, nsp), 1) * SEL_BLOCK
    overlap = ((c_start < s_start + SEL_BLOCK) & (c_start + CMP_BLOCK > s_start)
               & (c_start < n_cmp * CMP_STRIDE)).astype(F32)
    return jnp.dot(pc, overlap, preferred_element_type=F32, precision=lax.Precision.HIGHEST)


def _force_blocks(imp, blk, cur):
    forced = (blk == 0) | (blk == cur) | (blk == cur - 1)
    imp = jnp.where(forced, FORCE_SCORE, imp)
    return jnp.where(blk > cur, -1.0, imp)


def _select_blocks_cols(pc, cur, n_cmp, n_sel):
    imp = _block_importance(pc, n_cmp, n_sel)
    blk = lax.broadcasted_iota(jnp.int32, (1, imp.shape[1]), 1)
    imp = _force_blocks(imp, blk, cur)

    def body(sp, count):
        col = jnp.sum(jnp.where(blk == sp, imp, 0.0), axis=1, keepdims=True)
        ahead = (col > imp) | ((col == imp) & (sp < blk))
        return count + ahead.astype(F32)

    count = lax.fori_loop(0, n_sel, body, jnp.zeros(imp.shape, F32), unroll=True)
    return ((count < min(SEL_TOPK, n_sel)) & (blk < n_sel)).astype(F32)


def _select_blocks_rows(pc, cur_row, n_cmp, n_sel):
    rows = pc.shape[0]
    nrow = -(-n_sel // SUBLANES) * SUBLANES
    imp_t = _block_importance(pc, n_cmp, n_sel).T[0:nrow, :]
    blk = lax.broadcasted_iota(jnp.int32, (nrow, 1), 0)
    imp_t = _force_blocks(imp_t, blk, cur_row)
    count = jnp.zeros(imp_t.shape, F32)
    for sp in range(n_sel):
        row = imp_t[sp:sp + 1, :]
        count += ((row > imp_t) | ((row == imp_t) & (sp < blk))).astype(F32)
    sel_t = ((count < min(SEL_TOPK, n_sel)) & (blk < n_sel)).astype(F32)
    sel_t = jnp.concatenate([sel_t, jnp.zeros((LANES - nrow, rows), F32)], axis=0)
    return sel_t.T


def _gate(gt, col):
    lane = lax.broadcasted_iota(jnp.int32, (1, LANES), 1)
    return jax.nn.sigmoid(jnp.sum(jnp.where(lane == col, gt, 0.0), axis=1, keepdims=True))


def _nsa_prompt_kernel(q_ref, ks_ref, vs_ref, kw_ref, vw_ref, kc_ref, vc_ref, bc_ref, tz_ref, gt_ref, gb_ref,
                       o_ref, s_scr, selb_scr, mx_scr, l_scr, acc_scr, *, t_len):
    g = pl.program_id(1)
    qt = pl.program_id(2)
    q0 = qt * TQ
    n_cmp = t_len // CMP_STRIDE - (CMP_BLOCK // CMP_STRIDE) + 1
    n_sel = t_len // SEL_BLOCK
    ncp = kc_ref.shape[0]
    rows4 = GROUP * TQ
    qs = _stack_heads(q_ref[...] * QK_SCALE).astype(BF16)
    qpos = q0 + (lax.broadcasted_iota(jnp.int32, (rows4, 1), 0) & (TQ - 1))

    n_idx = lax.broadcasted_iota(jnp.int32, (1, ncp), 1)
    s_c = _dot_nt(qs, kc_ref[...].astype(BF16)) + bc_ref[...].reshape(rows4, ncp)
    cmask = (qpos - (n_idx * CMP_STRIDE + CMP_BLOCK - 1) >= 0) & (n_idx < n_cmp)
    p_c = _masked_probs(s_c, cmask)
    o_c = _dot(p_c.astype(BF16), vc_ref[...].astype(BF16))
    pc = p_c[0:TQ] + p_c[TQ:2 * TQ] + p_c[2 * TQ:3 * TQ] + p_c[3 * TQ:4 * TQ]
    cur_row = jnp.right_shift(q0 + lax.broadcasted_iota(jnp.int32, (1, TQ), 1), SEL_SHIFT)
    sel = _select_blocks_rows(pc, cur_row, n_cmp, n_sel)
    sel_add = ((sel - 1.0) * -MASKED).astype(BF16)

    kt_hi = (q0 + TQ - 1) // TK + 1
    kt_far = jnp.maximum(q0 - (FAR_DIST - 1), 0) // TK
    kidx = lax.broadcasted_iota(jnp.int32, (1, TK), 1)
    srow = lax.broadcasted_iota(jnp.int32, (LANES, TK), 0)

    def spread_selection(kt, carry):
        k0 = kt * TK
        expand = (jnp.right_shift(k0 + lax.broadcasted_iota(jnp.int32, (LANES, TK), 1), SEL_SHIFT) == srow).astype(BF16)
        selb_scr[kt] = _dot(sel_add, expand)
        return carry

    lax.fori_loop(0, kt_hi, spread_selection, 0)

    def tile_bias(k0):
        return tz_ref[jnp.clip((q0 - k0) // TQ, 0, N_TZ - 1)].reshape(rows4, TK)

    def attend(k_ref, v_ref, kt_lo, mask_tile):
        mx_scr[...] = jnp.full(mx_scr.shape, NEG_INF, F32)

        def scores(near):
            def body(kt, carry):
                k0 = pl.multiple_of(kt * TK, TK)
                s = _dot_nt(qs, k_ref[pl.ds(k0, TK), :].astype(BF16))
                if near:
                    s = s + tile_bias(k0)
                s = mask_tile(k0, s, near)
                s_scr[kt] = s
                mx_scr[...] = jnp.maximum(mx_scr[...], _fold_lanes(s, jnp.maximum))
                return carry
            return body

        kt_mid = jnp.maximum(kt_far, kt_lo)
        lax.fori_loop(kt_lo, kt_mid, scores(False), 0)
        lax.fori_loop(kt_mid, kt_hi, scores(True), 0)
        m = jnp.max(mx_scr[...], axis=1, keepdims=True)
        mx_scr[...] = jnp.broadcast_to(jnp.where(m > NEG_INF, m, 0.0), mx_scr.shape)
        l_scr[...] = jnp.zeros(l_scr.shape, F32)
        acc_scr[...] = jnp.zeros(acc_scr.shape, F32)

        def probs(kt, carry):
            k0 = pl.multiple_of(kt * TK, TK)
            m_rep = mx_scr[...]
            p = jnp.exp(s_scr[kt] - jnp.concatenate([m_rep] * (TK // LANES), axis=1))
            l_scr[...] += _fold_lanes(p, jnp.add)
            acc_scr[...] += _dot(p.astype(BF16), v_ref[pl.ds(k0, TK), :].astype(BF16))
            return carry

        lax.fori_loop(kt_lo, kt_hi, probs, 0)
        l = jnp.sum(l_scr[...], axis=1, keepdims=True)
        return acc_scr[...] / jnp.where(l > 0, l, 1.0)

    def sel_mask(k0, s, near):
        s = (s.reshape(GROUP, TQ, TK) + selb_scr[k0 // TK][None]).reshape(rows4, TK)
        return jnp.where(k0 + kidx <= qpos, s, NEG_INF) if near else s

    o_s = attend(ks_ref, vs_ref, 0, sel_mask)

    def win_mask(k0, s, near):
        dist = qpos - (k0 + kidx)
        return jnp.where((dist >= 0) & (dist <= WINDOW), s, NEG_INF)

    o_w = attend(kw_ref, vw_ref, jnp.maximum(q0 - WINDOW, 0) // TK, win_mask)

    gt = gt_ref[...] + gb_ref[...]
    for j in range(GROUP):
        col = GATE_NG + (g * GROUP + j) * 3
        r = slice(j * TQ, (j + 1) * TQ)
        o_ref[:, j * HEAD_DIM:(j + 1) * HEAD_DIM] = (
            _gate(gt, col) * o_c[r] + _gate(gt, col + 1) * o_s[r] + _gate(gt, col + 2) * o_w[r])


def _nsa_prompt(z, kcvc, bias_c, tz, gate_b):
    b, t, _ = z.shape
    ncp = kcvc.shape[2]
    nq = t // TQ
    kv = lambda slot: pl.BlockSpec((None, t, HEAD_DIM), lambda bi, g, qi: (bi, 0, C_NKV // HEAD_DIM + slot * KV_HEADS + g),
                                   pipeline_mode=pl.Buffered(1))
    cmp_blk = lambda slot: pl.BlockSpec((None, None, ncp, HEAD_DIM), lambda bi, g, qi: (bi, slot * KV_HEADS + g, 0, 0))
    return pl.pallas_call(
        functools.partial(_nsa_prompt_kernel, t_len=t),
        grid=(b, KV_HEADS, nq),
        in_specs=[pl.BlockSpec((None, TQ, GROUP * HEAD_DIM), lambda bi, g, qi: (bi, qi, C_NQ // (GROUP * HEAD_DIM) + g)),
                  kv(2), kv(3), kv(4), kv(5), cmp_blk(0), cmp_blk(1),
                  pl.BlockSpec((GROUP, TQ, ncp), lambda bi, g, qi: (g, qi, 0)),
                  pl.BlockSpec((N_TZ, GROUP, TQ, TK), lambda bi, g, qi: (0, g, 0, 0), pipeline_mode=pl.Buffered(1)),
                  pl.BlockSpec((None, TQ, LANES), lambda bi, g, qi: (bi, qi, C_GATE // LANES)),
                  pl.BlockSpec((1, LANES), lambda bi, g, qi: (0, 0))],
        out_specs=pl.BlockSpec((None, TQ, GROUP * HEAD_DIM), lambda bi, g, qi: (bi, qi, g)),
        out_shape=jax.ShapeDtypeStruct((b, t, D_NSA), F32),
        scratch_shapes=[pltpu.VMEM((t // TK, GROUP * TQ, TK), F32), pltpu.VMEM((t // TK, TQ, TK), F32),
                        pltpu.VMEM((GROUP * TQ, LANES), F32),
                        pltpu.VMEM((GROUP * TQ, LANES), F32), pltpu.VMEM((GROUP * TQ, HEAD_DIM), F32)],
        compiler_params=_cparams("parallel", "parallel", "arbitrary"),
        name="nsa_prompt",
    )(z, z, z, z, z, kcvc, kcvc, bias_c, tz, z, gate_b)


TS = SUBLANES
ROWS_S = GROUP * TS


def _nsa_s_cmp_kernel(q_ref, kc_ref, vc_ref, bc_ref, oc_ref, sel_ref, *, pos0, n_cmp, n_sel):
    nch = kc_ref.shape[0]
    qs = _stack_heads(q_ref[...] * QK_SCALE).astype(BF16)
    qpos = pos0 + (lax.broadcasted_iota(jnp.int32, (ROWS_S, 1), 0) & (TS - 1))
    n_idx = lax.broadcasted_iota(jnp.int32, (1, nch), 1)
    s_c = _dot_nt(qs, kc_ref[...].astype(BF16)) + bc_ref[...].reshape(ROWS_S, nch)
    cmask = (qpos - (n_idx * CMP_STRIDE + CMP_BLOCK - 1) >= 0) & (n_idx < n_cmp)
    p_c = _masked_probs(s_c, cmask)
    oc_ref[...] = _dot(p_c.astype(BF16), vc_ref[...].astype(BF16))
    pc = p_c[0:TS] + p_c[TS:2 * TS] + p_c[2 * TS:3 * TS] + p_c[3 * TS:4 * TS]
    sel_ref[...] = _select_blocks_cols(pc, jnp.right_shift(qpos[0:TS], SEL_SHIFT), n_cmp, n_sel)


def _nsa_sample_cmp(z8, kcvc, bias_c, *, pos0, n_cmp, n_sel):
    b = z8.shape[0]
    nch = kcvc.shape[2]
    nsp = -(-n_sel // LANES) * LANES
    cmp_blk = lambda slot: pl.BlockSpec((None, None, nch, HEAD_DIM), lambda bi, g: (bi, slot * KV_HEADS + g, 0, 0))
    return pl.pallas_call(
        functools.partial(_nsa_s_cmp_kernel, pos0=pos0, n_cmp=n_cmp, n_sel=n_sel),
        grid=(b, KV_HEADS),
        in_specs=[pl.BlockSpec((None, TS, GROUP * HEAD_DIM), lambda bi, g: (bi, 0, C_NQ // (GROUP * HEAD_DIM) + g)),
                  cmp_blk(0), cmp_blk(1),
                  pl.BlockSpec((GROUP, TS, nch), lambda bi, g: (g, 0, 0))],
        out_specs=[pl.BlockSpec((None, None, ROWS_S, HEAD_DIM), lambda bi, g: (bi, g, 0, 0)),
                   pl.BlockSpec((None, None, TS, nsp), lambda bi, g: (bi, g, 0, 0))],
        out_shape=[jax.ShapeDtypeStruct((b, KV_HEADS, ROWS_S, HEAD_DIM), F32),
                   jax.ShapeDtypeStruct((b, KV_HEADS, TS, nsp), F32)],
        compiler_params=_cparams("parallel", "parallel"),
        name="nsa_sample_cmp",
    )(z8, kcvc, kcvc, bias_c)


def _pad_rows(x, n):
    return jnp.concatenate([x, jnp.zeros((n - x.shape[0], x.shape[1]), x.dtype)], axis=0)


def _nsa_s_sel_kernel(pt_ref, q_ref, sel_ref, kn_ref, vn_ref, bt_ref, bn_ref, bf_ref, *refs, pos0, npages):
    del pt_ref
    rows_pp = PAGE_SIZE * KV_HEADS
    k_refs = [r.reshape(rows_pp, HEAD_DIM) for r in refs[:SEL_PAGES]]
    v_refs = [r.reshape(rows_pp, HEAD_DIM) for r in refs[SEL_PAGES:2 * SEL_PAGES]]
    o_ref, kbuf, vbuf, m_scr, l_scr, acc_scr = refs[2 * SEL_PAGES:]
    step = pl.program_id(1)
    last = step == npages // SEL_PAGES - 1
    rows = KV_HEADS * ROWS_S
    ncol = SEL_TILE * rows_pp
    head_shift = KV_HEADS.bit_length() - 1
    blk_per_page = PAGE_SIZE // SEL_BLOCK

    @pl.when(step == 0)
    def _():
        _softmax_init(m_scr, l_scr, acc_scr)

    nsp = sel_ref.shape[-1]
    ridx = lax.broadcasted_iota(jnp.int32, (rows, 1), 0)
    qpos = pos0 + (ridx & (TS - 1))
    qs = jnp.concatenate([_stack_heads(q_ref[:, g * GROUP * HEAD_DIM:(g + 1) * GROUP * HEAD_DIM] * QK_SCALE)
                          for g in range(KV_HEADS)], axis=0).astype(BF16)
    sel_rows = jnp.concatenate([sel_ref[g] for g in range(KV_HEADS) for _ in range(GROUP)], axis=0)
    bias_far = bf_ref[...].reshape(rows, PAGE_SIZE)[:, 0:1]
    col = lax.broadcasted_iota(jnp.int32, (1, ncol), 1)
    own_head = (col & (KV_HEADS - 1)) == ridx // ROWS_S
    expand = (jnp.right_shift(lax.broadcasted_iota(jnp.int32, (LANES, ncol), 1), SEL_SHIFT + head_shift)
              == lax.broadcasted_iota(jnp.int32, (LANES, ncol), 0)).astype(BF16)
    blk_r = lax.broadcasted_iota(jnp.int32, (nsp, LANES), 0)
    blk_c = lax.broadcasted_iota(jnp.int32, (nsp, LANES), 1)
    n_tiles = SEL_PAGES // SEL_TILE

    for tile in range(n_tiles):
        for i in range(SEL_TILE):
            p = tile * SEL_TILE + i
            kbuf[i * rows_pp:(i + 1) * rows_pp, :] = k_refs[p][...].astype(BF16)
            vbuf[i * rows_pp:(i + 1) * rows_pp, :] = v_refs[p][...].astype(BF16)
        page0 = step * SEL_PAGES + tile * SEL_TILE
        window = ((blk_r == page0 * blk_per_page + blk_c) & (blk_c < SEL_TILE * blk_per_page)).astype(BF16)
        chosen = _dot(_dot(sel_rows.astype(BF16), window).astype(BF16), expand) > 0.5
        s = _dot_nt(qs, kbuf[...]) + bias_far
        if tile == n_tiles - 1:
            tail = jnp.where(last, bt_ref[...].reshape(rows, rows_pp) - bias_far, 0.0)
            s = s + jnp.concatenate([jnp.zeros((rows, ncol - rows_pp), F32), tail], axis=1)
        key = page0 * PAGE_SIZE + jnp.right_shift(col, head_shift)
        _softmax_update(s, chosen & own_head & (key <= qpos), vbuf[...], m_scr, l_scr, acc_scr)

    @pl.when(last)
    def _():
        kidx = lax.broadcasted_iota(jnp.int32, (1, PAGE_SIZE), 1)
        new_blk = lax.broadcasted_iota(jnp.int32, (1, nsp), 1) == npages * blk_per_page
        flag = jnp.sum(jnp.where(new_blk, sel_rows, 0.0), axis=1, keepdims=True) > 0.5
        bias_new = bn_ref[...].reshape(rows, PAGE_SIZE)
        for g in range(KV_HEADS):
            r = slice(g * ROWS_S, (g + 1) * ROWS_S)
            kn = _pad_rows(kn_ref[:, g * HEAD_DIM:(g + 1) * HEAD_DIM], PAGE_SIZE).astype(BF16)
            vn = _pad_rows(vn_ref[:, g * HEAD_DIM:(g + 1) * HEAD_DIM], PAGE_SIZE).astype(BF16)
            mask = flag[r] & (pos0 + kidx <= qpos[r]) & (kidx < TS)
            _softmax_update(_dot_nt(qs[r], kn) + bias_new[r], mask, vn, m_scr.at[r], l_scr.at[r], acc_scr.at[r])
            o_ref[g] = _softmax_result(l_scr.at[r], acc_scr.at[r])


def _nsa_sample_sel(z8, sel, bias_tail, bias_new, bias_far, cache6, page_table, layer, *, pos0):
    b, npages = page_table.shape
    nsp = sel.shape[-1]
    steps = npages // SEL_PAGES
    slot_cols = KV_HEADS * HEAD_DIM
    new_rows = lambda slot: pl.BlockSpec((None, TS, slot_cols),
                                         lambda bi, si, pt: (bi, 0, (C_NKV + slot * slot_cols) // slot_cols))
    whole = lambda a: pl.BlockSpec(a.shape, lambda bi, si, pt: (0,) * a.ndim)
    grid_spec = pltpu.PrefetchScalarGridSpec(
        num_scalar_prefetch=1,
        grid=(b, steps),
        in_specs=[pl.BlockSpec((None, TS, D_NSA), lambda bi, si, pt: (bi, 0, C_NQ // D_NSA)),
                  pl.BlockSpec((None, KV_HEADS, TS, nsp), lambda bi, si, pt: (bi, 0, 0, 0)),
                  new_rows(2), new_rows(3), whole(bias_tail), whole(bias_new), whole(bias_far)]
                 + [_page_spec(layer, 2, p, SEL_PAGES, 2) for p in range(SEL_PAGES)]
                 + [_page_spec(layer, 3, p, SEL_PAGES, 2) for p in range(SEL_PAGES)],
        out_specs=pl.BlockSpec((None, KV_HEADS, ROWS_S, HEAD_DIM), lambda bi, si, pt: (bi, 0, 0, 0)),
        scratch_shapes=[pltpu.VMEM((SEL_TILE * PAGE_SIZE * KV_HEADS, HEAD_DIM), BF16),
                        pltpu.VMEM((SEL_TILE * PAGE_SIZE * KV_HEADS, HEAD_DIM), BF16),
                        pltpu.VMEM((KV_HEADS * ROWS_S, 1), F32), pltpu.VMEM((KV_HEADS * ROWS_S, 1), F32),
                        pltpu.VMEM((KV_HEADS * ROWS_S, HEAD_DIM), F32)],
    )
    return pl.pallas_call(
        functools.partial(_nsa_s_sel_kernel, pos0=pos0, npages=npages),
        grid_spec=grid_spec,
        out_shape=jax.ShapeDtypeStruct((b, KV_HEADS, ROWS_S, HEAD_DIM), F32),
        compiler_params=_cparams("parallel", "arbitrary"),
        name="nsa_sample_sel",
    )(page_table, z8, sel, z8, z8, bias_tail, bias_new, bias_far, *([cache6] * (2 * SEL_PAGES)))


def _nsa_s_win_kernel(q_ref, kw_ref, vw_ref, kn_ref, vn_ref, bw_ref, oc_ref, os_ref, gt_ref, gb_ref, o_ref,
                      m_scr, l_scr, acc_scr, *, pos0):
    g = pl.program_id(1)
    wb = kw_ref.shape[0]
    own = pl.ds(g, wb, stride=KV_HEADS)
    kw = kw_ref.reshape(wb * KV_HEADS, HEAD_DIM)[own, :]
    vw = vw_ref.reshape(wb * KV_HEADS, HEAD_DIM)[own, :]
    qs = _stack_heads(q_ref[...] * QK_SCALE).astype(BF16)
    qpos = pos0 + (lax.broadcasted_iota(jnp.int32, (ROWS_S, 1), 0) & (TS - 1))
    bias = bw_ref[...].reshape(ROWS_S, wb + PAGE_SIZE)
    _softmax_init(m_scr, l_scr, acc_scr)
    dist = qpos - (pos0 - wb + lax.broadcasted_iota(jnp.int32, (1, wb), 1))
    s = _dot_nt(qs, kw.astype(BF16)) + bias[:, 0:wb]
    _softmax_update(s, (dist >= 0) & (dist <= WINDOW), vw.astype(BF16), m_scr, l_scr, acc_scr)
    kidx = lax.broadcasted_iota(jnp.int32, (1, PAGE_SIZE), 1)
    dist = qpos - (pos0 + kidx)
    s = _dot_nt(qs, _pad_rows(kn_ref[...], PAGE_SIZE).astype(BF16)) + bias[:, wb:wb + PAGE_SIZE]
    _softmax_update(s, (dist >= 0) & (dist <= WINDOW) & (kidx < TS), _pad_rows(vn_ref[...], PAGE_SIZE).astype(BF16),
                    m_scr, l_scr, acc_scr)
    o_w = _softmax_result(l_scr, acc_scr)
    o_c = oc_ref[...]
    o_s = os_ref[...]
    gt = gt_ref[...] + gb_ref[...]
    for j in range(GROUP):
        col = GATE_NG + (g * GROUP + j) * 3
        r = slice(j * TS, (j + 1) * TS)
        o_ref[:, j * HEAD_DIM:(j + 1) * HEAD_DIM] = (
            _gate(gt, col) * o_c[r] + _gate(gt, col + 1) * o_s[r] + _gate(gt, col + 2) * o_w[r])


def _nsa_sample_win(z8, win6, layer, bias_w, o_c, o_s, gate_b, *, pos0):
    _, b, wb = win6.shape[:3]
    cached = lambda slot: pl.BlockSpec((None, None, wb, None, KV_HEADS, HEAD_DIM),
                                       lambda bi, g: (layer, bi, 0, slot, 0, 0))
    zcol = lambda col: (lambda bi, g: (bi, 0, col // HEAD_DIM + g))
    part = pl.BlockSpec((None, None, ROWS_S, HEAD_DIM), lambda bi, g: (bi, g, 0, 0))
    return pl.pallas_call(
        functools.partial(_nsa_s_win_kernel, pos0=pos0),
        grid=(b, KV_HEADS),
        in_specs=[pl.BlockSpec((None, TS, GROUP * HEAD_DIM), lambda bi, g: (bi, 0, C_NQ // (GROUP * HEAD_DIM) + g)),
                  cached(0), cached(1),
                  pl.BlockSpec((None, TS, HEAD_DIM), zcol(C_NKV + 4 * KV_HEADS * HEAD_DIM)),
                  pl.BlockSpec((None, TS, HEAD_DIM), zcol(C_NKV + 5 * KV_HEADS * HEAD_DIM)),
                  pl.BlockSpec((GROUP, TS, wb + PAGE_SIZE), lambda bi, g: (g, 0, 0)),
                  part, part,
                  pl.BlockSpec((None, TS, LANES), lambda bi, g: (bi, 0, C_GATE // LANES)),
                  pl.BlockSpec((1, LANES), lambda bi, g: (0, 0))],
        out_specs=pl.BlockSpec((None, TS, GROUP * HEAD_DIM), lambda bi, g: (bi, 0, g)),
        out_shape=jax.ShapeDtypeStruct((b, TS, D_NSA), F32),
        scratch_shapes=[pltpu.VMEM((ROWS_S, 1), F32), pltpu.VMEM((ROWS_S, 1), F32), pltpu.VMEM((ROWS_S, HEAD_DIM), F32)],
        compiler_params=_cparams("parallel", "parallel"),
        name="nsa_sample_win",
    )(z8, win6, win6, z8, z8, bias_w, o_c, o_s, z8, gate_b)


def _t5_bucket(dist):
    n = np.maximum(dist, 0)
    exact = NUM_BUCKETS // 2
    nf = np.maximum(n, 1).astype(np.float32)
    large = exact + (np.log(nf / np.float32(exact)) / np.float32(math.log(MAX_DISTANCE / exact))
                     * np.float32(NUM_BUCKETS - exact)).astype(np.int32)
    return np.where(n < exact, n, np.minimum(large, NUM_BUCKETS - 1)).astype(np.int32)


FAR_DIST = int(np.max(np.nonzero(_t5_bucket(np.arange(4 * MAX_DISTANCE)) < NUM_BUCKETS - 1)[0])) + 1
N_TZ = (FAR_DIST + TK - 2) // TQ + 1


def _bias_table(rel_bias, dist):
    onehot = jax.nn.one_hot(jnp.asarray(_t5_bucket(dist).astype(np.int8)), NUM_BUCKETS, dtype=F32)
    return jnp.einsum("rcb,bh->hrc", onehot, rel_bias, precision=lax.Precision.HIGHEST)


def _pack_weights(p):
    w_in = p["w_in"].astype(BF16)
    pad = jnp.zeros(w_in.shape[:2] + (N_IN - C_GATE - 32,), BF16)
    w_in_p = jnp.concatenate([w_in[..., 0:2048], w_in[..., 2056:3080], w_in[..., 3080:4616], w_in[..., 4640:5152],
                              w_in[..., 2048:2056], w_in[..., 4616:4640], pad], axis=2)
    return dict(
        w_in=w_in_p, w_out=p["w_out"].astype(BF16), w_down=p["ffn_w_down"].astype(BF16),
        w_up=p["ffn_w_up"].astype(BF16))


def _pack_layer(p, big, l):
    gb = p["mlstm_gate_b"][l]
    mlstm_gate_b = jnp.zeros((1, LANES), F32).at[0, GATE_MI:GATE_MI + 4].set(gb[0]).at[0, GATE_MF:GATE_MF + 4].set(gb[1])
    nsa_gate_b = jnp.zeros((1, LANES), F32).at[0, GATE_NG:GATE_NG + 3 * NSA_HEADS].set(p["nsa_gate_b"][l].reshape(-1))
    w1 = p["cmp_w1"][l]
    wcat = jnp.concatenate([w1[:, 0:CMP_STRIDE], w1[:, CMP_STRIDE:CMP_BLOCK]], axis=-1).astype(BF16)
    lam = lax.complex(p["s5_a_re"][l], p["s5_a_im"][l])
    lam_bar = jnp.exp(lam * jnp.exp(p["s5_log_step"][l])[:, None])
    b_bar = ((lam_bar - 1.0) / lam)[..., None] * lax.complex(p["s5_b_re"][l], p["s5_b_im"][l])
    gi = S5_GROUPS // S5_IN_TILES
    bd_in = lambda m: jnp.einsum("cgph,gk->cghkp", m.reshape(S5_IN_TILES, gi, S5_STATE, S5_GROUP_WIDTH),
                                 jnp.eye(gi, dtype=F32)).reshape(S5_IN_TILES, S5_IN_FEATS, 256)
    wb = jnp.concatenate([bd_in(b_bar.real), bd_in(b_bar.imag)], axis=2).astype(BF16)
    go = S5_GROUPS // S5_OUT_TILES
    bd_out = lambda m: jnp.einsum("jghp,gk->jgpkh", m.reshape(S5_OUT_TILES, go, S5_GROUP_WIDTH, S5_STATE),
                                  jnp.eye(go, dtype=F32)).reshape(S5_OUT_TILES, S5_OUT_CH, 256)
    wc = jnp.stack([bd_out(p["s5_c_re"][l]), -bd_out(p["s5_c_im"][l])]).astype(BF16)
    half = lambda w: _to_tiled_ff(w, FFN_TF)
    conv_w = p["ffn_conv_w"][l]
    conv_b = p["ffn_conv_b"][l][None, :]
    return dict(
        big, layer=l, mlstm_gate_b=mlstm_gate_b, mlstm_norm_w=p["mlstm_norm_w"][l][None, :], nsa_gate_b=nsa_gate_b,
        wcat=wcat, cmp_b1=p["cmp_b1"][l], cmp_w2=p["cmp_w2"][l].astype(BF16),
        s5_wb=wb, s5_wc=wc, s5_lam_r=lam_bar.real.reshape(1, S5_CH), s5_lam_i=lam_bar.imag.reshape(1, S5_CH),
        s5_d=p["s5_d"][l][None, :], s5_glu_w=p["s5_glu_w"][l].astype(BF16), s5_glu_b=p["s5_glu_b"][l][None, :],
        ln1_w=p["ln1_w"][l][None, :], ln1_b=p["ln1_b"][l][None, :],
        conv_w8=jnp.pad(jnp.concatenate([half(conv_w[:, :D_FF]), half(conv_w[:, D_FF:])], axis=1),
                        ((0, SUBLANES - CONV_W), (0, 0))),
        conv_b=jnp.concatenate([half(conv_b[:, :D_FF]), half(conv_b[:, D_FF:])], axis=1),
        ln2_w=p["ln2_w"][l][None, :], ln2_b=p["ln2_b"][l][None, :],
    )


def _unpad_ff(x):
    return jnp.concatenate([_from_tiled_ff(x[..., :D_FFP], FFN_TF), _from_tiled_ff(x[..., D_FFP:], FFN_TF)], axis=-1)


def _pad_ff(x):
    return jnp.concatenate([_to_tiled_ff(x[..., :D_FF], FFN_TF), _to_tiled_ff(x[..., D_FF:], FFN_TF)], axis=-1)


def _mixer_tail(x2d, z, b, t, t_use, lw, o_mlstm, o_nsa, s5_state, conv_state8, tm):
    pad8 = lambda s: jnp.pad(s.reshape(b, S5_CH), ((0, SUBLANES - b), (0, 0)))
    o_s5, xr, xi = _s5(z.reshape(b, t, N_IN), lw["s5_wb"], lw["s5_wc"], lw["s5_lam_r"], lw["s5_lam_i"], lw["s5_d"],
                       lw["s5_glu_w"], lw["s5_glu_b"], pad8(s5_state[0]), pad8(s5_state[1]), lc=S5_LC, steps=t_use)
    o_s5 = o_s5.reshape(b * t, D_S5)
    layer = lw["layer"]
    x1 = _wout_ln(o_mlstm.reshape(b * t, D_MLSTM), o_nsa.reshape(b * t, D_NSA), o_s5, x2d,
                  lw["w_out"], layer, lw["ln1_w"], lw["ln1_b"], tm=512)
    s5_new = (xr[:b].reshape(b, S5_GROUPS, S5_STATE), xi[:b].reshape(b, S5_GROUPS, S5_STATE))
    if conv_state8 is None:
        x2, tail_a, tail_g = _ffn_prompt(x1.reshape(b, t, D_MODEL), lw["w_up"], lw["w_down"], layer, lw["conv_w8"],
                                         lw["conv_b"], lw["ln2_w"], lw["ln2_b"], tt=FFN_TT, tf=FFN_TF)
        keep = slice(SUBLANES - (CONV_W - 1), SUBLANES)
        conv_new = jnp.concatenate([_from_tiled_ff(tail_a[:, keep], FFN_TF), _from_tiled_ff(tail_g[:, keep], FFN_TF)],
                                   axis=-1)
        return x2.reshape(b * t, D_MODEL), s5_new, conv_new
    up = _matmul(x1, lw["w_up"], layer, tm=tm, tn=FFN_TF, n=2 * D_FFP, w_cols=_up_cols(FFN_TF))
    hgate = _convgate(up.reshape(b, t, 2 * D_FFP), conv_state8, lw["conv_w8"], lw["conv_b"], tf=FFN_TF)
    x2 = _down_ln(hgate.reshape(b * t, D_FFP), lw["w_down"], layer, x1, lw["ln2_w"], lw["ln2_b"], tm=512, tk=FFN_TF)
    conv_new = _unpad_ff(up.reshape(b, t, 2 * D_FFP)[:, t_use - (CONV_W - 1):t_use])
    return x2, s5_new, conv_new


def _prompt_layer(x2d, b, t, lw, rel_bias, rows_so_far):
    z, kv_rows, win_rows = _proj_in(x2d, lw["w_in"], lw["layer"], rows_so_far, tm=min(1024, b * t), tn=768)
    z3 = z.reshape(b, t, N_IN)
    zeros = lambda *s: jnp.zeros(s, F32)
    o_mlstm, c1, n1, m1 = _mlstm(z3, lw["mlstm_gate_b"], lw["mlstm_norm_w"],
                                 zeros(b, MLSTM_HEADS, HEAD_DIM, HEAD_DIM), zeros(b, MLSTM_HEADS, LANES),
                                 zeros(b, MLSTM_HEADS, LANES), L=MLSTM_L, lin=MLSTM_L, t_valid=t)
    kcvc = _cmp_finish(_cmp_project_prompt(z3, lw["wcat"]), lw["cmp_b1"], lw["cmp_w2"])
    ncp = t // CMP_STRIDE
    bias_c = _bias_table(rel_bias, np.arange(t)[:, None] - (np.arange(ncp) * CMP_STRIDE + CMP_BLOCK - 1)[None, :])
    ti = np.arange(TQ)[:, None] - np.arange(TK)[None, :]
    far = _bias_table(rel_bias, np.full((1, 1), FAR_DIST))
    tz = jnp.stack([_bias_table(rel_bias, d * TQ + ti) - far for d in range(N_TZ)])
    o_nsa = _nsa_prompt(z3, kcvc, bias_c, tz, lw["nsa_gate_b"])
    x2, s5_new, conv_new = _mixer_tail(
        x2d, z, b, t, t, lw, o_mlstm, o_nsa, (zeros(b, S5_GROUPS, S5_STATE), zeros(b, S5_GROUPS, S5_STATE)),
        None, tm=1024)
    state = (None, None, c1, n1, m1[:, :, 0], s5_new[0], s5_new[1], conv_new)
    return x2, state, (kv_rows, win_rows)


def _sample_layer(x2d, b, tn, lw, rel_bias, layer, cache6, page_table, win6, mlstm_state, s5_state, conv_state):
    npages = page_table.shape[1]
    pos0 = npages * PAGE_SIZE
    z = _matmul(x2d, lw["w_in"], lw["layer"], tm=1024, tn=768)
    z8 = z.reshape(b, TS, N_IN)
    c0, n0, m0 = mlstm_state
    o_mlstm, c1, n1, m1 = _mlstm(z8, lw["mlstm_gate_b"], lw["mlstm_norm_w"], c0, n0,
                                 jnp.broadcast_to(m0[:, :, None], (b, MLSTM_HEADS, LANES)),
                                 L=LANES, lin=TS, t_valid=tn)
    n_chunks = (pos0 + tn) // CMP_STRIDE
    n_cmp = n_chunks - CMP_BLOCK // CMP_STRIDE + 1
    n_sel = -(-(pos0 + tn) // SEL_BLOCK)
    kcvc = _cmp_finish(_cmp_project_sample(cache6, page_table, layer, lw["wcat"]), lw["cmp_b1"], lw["cmp_w2"])
    qpos = pos0 + np.arange(TS)[:, None]
    bias_c = _bias_table(rel_bias, qpos - (np.arange(n_chunks) * CMP_STRIDE + CMP_BLOCK - 1)[None, :])
    o_c, sel = _nsa_sample_cmp(z8, kcvc, bias_c, pos0=pos0, n_cmp=n_cmp, n_sel=n_sel)
    kk = np.arange(PAGE_SIZE)[None, :]
    bias_tail = _bias_table(rel_bias, np.repeat(qpos - (pos0 - PAGE_SIZE + kk), KV_HEADS, axis=1))
    bias_new = _bias_table(rel_bias, qpos - (pos0 + kk))
    bias_far = _bias_table(rel_bias, np.broadcast_to(qpos - (pos0 - 2 * PAGE_SIZE), (TS, PAGE_SIZE)))
    o_s = _nsa_sample_sel(z8, sel, bias_tail, bias_new, bias_far, cache6, page_table, layer, pos0=pos0)
    wb = win6.shape[2]
    wk = np.arange(wb)[None, :]
    bias_w = _bias_table(rel_bias, np.concatenate([qpos - (pos0 - wb + wk), qpos - (pos0 + kk)], axis=1))
    o_nsa = _nsa_sample_win(z8, win6, layer, bias_w, o_c, o_s, lw["nsa_gate_b"], pos0=pos0)
    conv_state8 = jnp.pad(_pad_ff(conv_state), ((0, 0), (SUBLANES - (CONV_W - 1), 0), (0, 0)))
    x2, s5_new, conv_new = _mixer_tail(x2d, z, b, TS, tn, lw, o_mlstm, o_nsa, s5_state, conv_state8, tm=1024)
    nkv = z8[:, :tn, C_NKV:C_NKV + N_KV_SLOTS * KV_HEADS * HEAD_DIM].reshape(b, tn, N_KV_SLOTS, KV_HEADS, HEAD_DIM)
    state = (nkv[:, :, :4], nkv[:, :, 4:], c1, n1, m1[:, :, 0], s5_new[0], s5_new[1], conv_new)
    return x2, state


def kernel(x_prompt, x_sample, cache_nsa_kv, cache_win_kv, state_mlstm_c, state_mlstm_n, state_mlstm_m,
           state_s5_re, state_s5_im, state_ffn_conv, page_table, w_in, mlstm_gate_b, mlstm_norm_w,
           nsa_gate_b, cmp_w1, cmp_b1, cmp_w2, rel_bias, s5_a_re, s5_a_im, s5_b_re, s5_b_im, s5_c_re,
           s5_c_im, s5_d, s5_log_step, s5_glu_w, s5_glu_b, w_out, ln1_w, ln1_b, ffn_w_up, ffn_conv_w,
           ffn_conv_b, ffn_w_down, ln2_w, ln2_b):
    params = dict(w_in=w_in, mlstm_gate_b=mlstm_gate_b, mlstm_norm_w=mlstm_norm_w, nsa_gate_b=nsa_gate_b,
                  cmp_w1=cmp_w1, cmp_b1=cmp_b1, cmp_w2=cmp_w2, s5_a_re=s5_a_re, s5_a_im=s5_a_im, s5_b_re=s5_b_re,
                  s5_b_im=s5_b_im, s5_c_re=s5_c_re, s5_c_im=s5_c_im, s5_d=s5_d, s5_log_step=s5_log_step,
                  s5_glu_w=s5_glu_w, s5_glu_b=s5_glu_b, w_out=w_out, ln1_w=ln1_w, ln1_b=ln1_b, ffn_w_up=ffn_w_up,
                  ffn_conv_w=ffn_conv_w, ffn_conv_b=ffn_conv_b, ffn_w_down=ffn_w_down, ln2_w=ln2_w, ln2_b=ln2_b)
    depth = w_in.shape[0]
    bp, tp, _ = x_prompt.shape
    bs, tn, _ = x_sample.shape
    assert tp % TK == 0 and tn < CMP_STRIDE and tn <= TS and SUBLANES % bp == 0 and SUBLANES % bs == 0
    assert (tn * bs) % SUBLANES == 0 and tp % min(FFN_TT, tp) == 0
    assert page_table.shape[1] % CMP_PAGES == 0 and cache_nsa_kv.shape[2] == PAGE_SIZE
    xp = x_prompt.reshape(bp * tp, D_MODEL)
    xs = jnp.pad(x_sample, ((0, 0), (0, TS - tn), (0, 0))).reshape(bs * TS, D_MODEL)
    p_states, s_states = [], []
    big = _pack_weights(params)
    rows_so_far = None
    for l in range(depth):
        lw = _pack_layer(params, big, l)
        xp, sp, rows_so_far = _prompt_layer(xp, bp, tp, lw, rel_bias, rows_so_far)
        xs, ss = _sample_layer(xs, bs, tn, lw, rel_bias, l, cache_nsa_kv, page_table, cache_win_kv,
                               (state_mlstm_c[l], jnp.pad(state_mlstm_n[l], ((0, 0), (0, 0), (0, LANES - HEAD_DIM))),
                                state_mlstm_m[l]),
                               (state_s5_re[l], state_s5_im[l]), state_ffn_conv[l])
        p_states.append(sp)
        s_states.append(ss)
    stk = lambda states, i: jnp.stack([s[i] for s in states])
    y_prompt = xp.reshape(bp, tp, D_MODEL)
    y_sample = xs.reshape(bs, TS, D_MODEL)[:, :tn]
    wrows = min(WINDOW, tp)
    nsa_kv_p = rows_so_far[0].reshape(depth, bp, tp, 4, KV_HEADS, HEAD_DIM)
    win_kv_p = rows_so_far[1].reshape(depth, bp, tp, 2, KV_HEADS, HEAD_DIM)[:, :, tp - wrows:]
    return (y_prompt, y_sample,
            nsa_kv_p, stk(s_states, 0), win_kv_p, stk(s_states, 1),
            stk(p_states, 2), stk(s_states, 2), stk(p_states, 3), stk(s_states, 3), stk(p_states, 4), stk(s_states, 4),
            stk(p_states, 5), stk(s_states, 5), stk(p_states, 6), stk(s_states, 6), stk(p_states, 7), stk(s_states, 7))
```

```python
import functools
import math

import jax
import jax.numpy as jnp
import numpy as np
from jax import lax
from jax.experimental import pallas as pl
from jax.experimental.pallas import tpu as pltpu

F32 = jnp.float32
BF16 = jnp.bfloat16
NEG_INF = float("-inf")
M_INIT = -1e30
MASKED = -1e30

D_MODEL = 2048
PAGE_SIZE = 128
D_MLSTM = D_MODEL // 4
D_NSA = D_MODEL // 2
D_S5 = D_MODEL - D_MLSTM - D_NSA
HEAD_DIM = 128
MLSTM_HEADS = D_MLSTM // HEAD_DIM
NSA_HEADS = D_NSA // HEAD_DIM
KV_HEADS = 2
GROUP = NSA_HEADS // KV_HEADS
N_KV_SLOTS = 6
CMP_BLOCK = 32
CMP_STRIDE = 16
SEL_BLOCK = 64
SEL_TOPK = 16
FORCE_SCORE = 1e4
WINDOW = 512
S5_GROUP_WIDTH = 16
S5_GROUPS = D_S5 // S5_GROUP_WIDTH
S5_STATE = 64
S5_CH = S5_GROUPS * S5_STATE
NUM_BUCKETS = 32
MAX_DISTANCE = 128
D_FF = ((8 * D_MODEL // 3 + 127) // 128) * 128
CONV_W = 3
LN_EPS = 1e-5
DEPTH = 2
DEEPNORM_ALPHA = (2 * DEPTH) ** 0.25
QK_SCALE = HEAD_DIM ** -0.5

LANES = 128
SUBLANES = 8
V7X_VMEM_LIMIT = 56 * 2 ** 20

C_MQ, C_MK, C_MV, C_MO = 0, 512, 1024, 1536
C_NQ = 2048
C_NKV = 3072
C_SU = 4608
C_GATE = 5120
N_IN = 5376
GATE_MI, GATE_MF, GATE_NG = 0, 4, 8
D_FFP = 5632
MLSTM_L = 512
TQ = 512
TK = 512
S5_LC = 256
FFN_TT = 1024
FFN_TF = 512
FFN_TC = 256
FFN_HALO = 16
CMP_PAGES = 64
SEL_PAGES = 64
SEL_TILE = 32
SEL_SHIFT = SEL_BLOCK.bit_length() - 1


def _cparams(*sem):
    return pltpu.CompilerParams(dimension_semantics=sem, vmem_limit_bytes=V7X_VMEM_LIMIT)


def _dot(a, b):
    return jnp.dot(a, b, preferred_element_type=F32)


def _dot_nt(a, b):
    return lax.dot_general(a, b, (((1,), (1,)), ((), ())), preferred_element_type=F32)


def _layer_norm(y, w, b):
    mu = jnp.mean(y, axis=-1, keepdims=True)
    d = y - mu
    var = jnp.mean(d * d, axis=-1, keepdims=True)
    return d * lax.rsqrt(var + LN_EPS) * w + b


def _mm_kernel(x_ref, w_ref, o_ref):
    o_ref[...] = _dot(x_ref[...].astype(BF16), w_ref[...]).astype(o_ref.dtype)


FF_OVER = D_FFP - D_FF


def _ff_start(f, tf):
    nf, per = D_FFP // tf, tf // LANES
    return (f * per - (f == nf - 1).astype(jnp.int32) * (FF_OVER // LANES)) * LANES


def _up_cols(tf):
    nf, per = D_FFP // tf, tf // LANES
    return lambda j: (j * per + (j >= nf).astype(jnp.int32) * (D_FF // LANES - nf * per)
                      - ((j == nf - 1) | (j == 2 * nf - 1)).astype(jnp.int32) * (FF_OVER // LANES)) * LANES


def _to_tiled_ff(x, tf):
    cut = D_FFP - tf
    zeros = jnp.zeros(x.shape[:-1] + (FF_OVER,), x.dtype)
    return jnp.concatenate([x[..., :cut], zeros, x[..., cut:]], axis=-1)


def _from_tiled_ff(x, tf):
    cut = D_FFP - tf
    return jnp.concatenate([x[..., :cut], x[..., cut + FF_OVER:]], axis=-1)


def _matmul(x, w, layer, *, tm, tn, n=None, w_cols=None):
    m, k = x.shape
    n = w.shape[2] if n is None else n
    tm = min(tm, m)
    if w_cols is None:
        w_spec = pl.BlockSpec((None, k, tn), lambda i, j: (layer, 0, j))
    else:
        w_spec = pl.BlockSpec((None, pl.Element(k), pl.Element(tn)), lambda i, j: (layer, 0, w_cols(j)))
    return pl.pallas_call(
        _mm_kernel,
        grid=(m // tm, n // tn),
        in_specs=[pl.BlockSpec((tm, k), lambda i, j: (i, 0)), w_spec],
        out_specs=pl.BlockSpec((tm, tn), lambda i, j: (i, j)),
        out_shape=jax.ShapeDtypeStruct((m, n), F32),
        compiler_params=_cparams("parallel", "parallel"),
        name="proj",
    )(x, w)


KV4_GROUPS = 4 * KV_HEADS
WIN_GROUPS = 2 * KV_HEADS


def _proj_in_kernel(x_ref, w_ref, z_ref, kv_ref, win_ref, xb_ref, *, tn):
    j = pl.program_id(1)

    @pl.when(j == 0)
    def _():
        xb_ref[...] = x_ref[...].astype(BF16)

    acc = _dot(xb_ref[...], w_ref[...])
    z_ref[...] = acc
    tm = acc.shape[0]
    per_tile = tn // HEAD_DIM
    first_tile = C_NKV // tn
    for tile in range(first_tile, first_tile + N_KV_SLOTS * KV_HEADS // per_tile):
        @pl.when(j == tile)
        def _(tile=tile):
            for c in range(per_tile):
                grp = (tile - first_tile) * per_tile + c
                val = acc[:, c * HEAD_DIM:(c + 1) * HEAD_DIM]
                if grp < KV4_GROUPS:
                    kv_ref[pl.ds(grp, tm, stride=KV4_GROUPS), :] = val
                else:
                    win_ref[pl.ds(grp - KV4_GROUPS, tm, stride=WIN_GROUPS), :] = val


def _proj_in_carry_kernel(x_ref, w_ref, kv_in_ref, win_in_ref, z_ref, kv_ref, win_ref, xb_ref, *, tn):
    del kv_in_ref, win_in_ref
    _proj_in_kernel(x_ref, w_ref, z_ref, kv_ref, win_ref, xb_ref, tn=tn)


def _proj_in(x, w, layer, rows_so_far, *, tm, tn):
    m, k = x.shape
    depth, _, n = w.shape
    nm = m // tm
    assert C_NKV % tn == 0 and (N_KV_SLOTS * KV_HEADS * HEAD_DIM) % tn == 0
    carried = () if rows_so_far is None else tuple(rows_so_far)
    return pl.pallas_call(
        functools.partial(_proj_in_carry_kernel if carried else _proj_in_kernel, tn=tn),
        grid=(nm, n // tn),
        in_specs=[pl.BlockSpec((tm, k), lambda i, j: (i, 0)), pl.BlockSpec((None, k, tn), lambda i, j: (layer, 0, j))]
                 + [pl.BlockSpec(memory_space=pl.ANY)] * len(carried),
        out_specs=[pl.BlockSpec((tm, tn), lambda i, j: (i, j)),
                   pl.BlockSpec((tm * KV4_GROUPS, HEAD_DIM), lambda i, j: (layer * nm + i, 0)),
                   pl.BlockSpec((tm * WIN_GROUPS, HEAD_DIM), lambda i, j: (layer * nm + i, 0))],
        out_shape=[jax.ShapeDtypeStruct((m, n), F32),
                   jax.ShapeDtypeStruct((depth * m * KV4_GROUPS, HEAD_DIM), F32),
                   jax.ShapeDtypeStruct((depth * m * WIN_GROUPS, HEAD_DIM), F32)],
        input_output_aliases={2: 1, 3: 2} if carried else {},
        scratch_shapes=[pltpu.VMEM((tm, k), BF16)],
        compiler_params=_cparams("parallel", "arbitrary"),
        name="proj_in",
    )(x, w, *carried)


def _wout_kernel(om_ref, on_ref, os_ref, x_ref, w_ref, lw_ref, lb_ref, o_ref):
    acc = _dot(om_ref[...].astype(BF16), w_ref[0:D_MLSTM, :])
    acc += _dot(on_ref[...].astype(BF16), w_ref[D_MLSTM:D_MLSTM + D_NSA, :])
    acc += _dot(os_ref[...].astype(BF16), w_ref[D_MLSTM + D_NSA:D_MODEL, :])
    o_ref[...] = _layer_norm(DEEPNORM_ALPHA * x_ref[...] + acc, lw_ref[...], lb_ref[...])


def _wout_ln(om, on, os_, x, w, layer, lw, lb, *, tm):
    m = x.shape[0]
    tm = min(tm, m)
    row = lambda width: pl.BlockSpec((tm, width), lambda i: (i, 0))
    full = lambda a: pl.BlockSpec(a.shape, lambda i: (0, 0))
    w_spec = pl.BlockSpec((None,) + w.shape[1:], lambda i: (layer, 0, 0))
    return pl.pallas_call(
        _wout_kernel,
        grid=(m // tm,),
        in_specs=[row(D_MLSTM), row(D_NSA), row(D_S5), row(D_MODEL), w_spec, full(lw), full(lb)],
        out_specs=row(D_MODEL),
        out_shape=jax.ShapeDtypeStruct((m, D_MODEL), F32),
        compiler_params=_cparams("parallel"),
        name="wout_ln",
    )(om, on, os_, x, w, lw, lb)


def _down_kernel(h_ref, w_ref, x_ref, lw_ref, lb_ref, o_ref, acc_ref, *, nk):
    k = pl.program_id(1)

    @pl.when(k == 0)
    def _():
        acc_ref[...] = jnp.zeros_like(acc_ref)

    acc_ref[...] += _dot(h_ref[...], w_ref[...])

    @pl.when(k == nk - 1)
    def _():
        o_ref[...] = _layer_norm(DEEPNORM_ALPHA * x_ref[...] + acc_ref[...], lw_ref[...], lb_ref[...])


def _down_ln(h, w, layer, x, lw, lb, *, tm, tk):
    m, kk = h.shape
    tm = min(tm, m)
    nk = kk // tk
    return pl.pallas_call(
        functools.partial(_down_kernel, nk=nk),
        grid=(m // tm, nk),
        in_specs=[
            pl.BlockSpec((tm, tk), lambda i, k: (i, k)),
            pl.BlockSpec((None, pl.Element(tk), pl.Element(D_MODEL)), lambda i, k: (layer, _ff_start(k, tk), 0)),
            pl.BlockSpec((tm, D_MODEL), lambda i, k: (i, 0)),
            pl.BlockSpec((1, D_MODEL), lambda i, k: (0, 0)),
            pl.BlockSpec((1, D_MODEL), lambda i, k: (0, 0)),
        ],
        out_specs=pl.BlockSpec((tm, D_MODEL), lambda i, k: (i, 0)),
        out_shape=jax.ShapeDtypeStruct((m, D_MODEL), F32),
        scratch_shapes=[pltpu.VMEM((tm, D_MODEL), F32)],
        compiler_params=_cparams("parallel", "arbitrary"),
        name="down_ln",
    )(h, w, x, lw, lb)


def _convgate_kernel(a_ref, g_ref, sa_ref, sg_ref, wa_ref, wg_ref, ba_ref, bg_ref, o_ref):
    rows = lax.broadcasted_iota(jnp.int32, (1, SUBLANES, 1), 1)

    def conv(cur_ref, st_ref, w_ref, b_ref):
        cur = cur_ref[...]
        prev = st_ref[...]
        p1 = prev[:, 7:8, :]
        p2 = prev[:, 6:7, :]
        x1 = jnp.where(rows == 0, p1, pltpu.roll(cur, 1, 1))
        x2 = jnp.where(rows == 0, p2, jnp.where(rows == 1, p1, pltpu.roll(cur, 2, 1)))
        w = w_ref[...]
        return b_ref[...] + w[0:1, :] * x2 + w[1:2, :] * x1 + w[2:3, :] * cur

    a = conv(a_ref, sa_ref, wa_ref, ba_ref)
    g = conv(g_ref, sg_ref, wg_ref, bg_ref)
    o_ref[...] = (a * jax.nn.sigmoid(a) * g).astype(o_ref.dtype)


def _convgate(up, state8, conv_w8, conv_b, *, tf):
    b, t, _ = up.shape
    assert t == SUBLANES
    nf = D_FFP // tf
    cur_a = pl.BlockSpec((b, t, tf), lambda fi: (0, 0, fi))
    cur_g = pl.BlockSpec((b, t, tf), lambda fi: (0, 0, fi + nf))
    w_a = pl.BlockSpec((SUBLANES, tf), lambda fi: (0, fi))
    w_g = pl.BlockSpec((SUBLANES, tf), lambda fi: (0, fi + nf))
    b_a = pl.BlockSpec((1, tf), lambda fi: (0, fi))
    b_g = pl.BlockSpec((1, tf), lambda fi: (0, fi + nf))
    return pl.pallas_call(
        _convgate_kernel,
        grid=(nf,),
        in_specs=[cur_a, cur_g, cur_a, cur_g, w_a, w_g, b_a, b_g],
        out_specs=pl.BlockSpec((b, t, tf), lambda fi: (0, 0, fi)),
        out_shape=jax.ShapeDtypeStruct((b, t, D_FFP), BF16),
        compiler_params=_cparams("parallel"),
        name="convgate",
    )(up, up, state8, state8, conv_w8, conv_w8, conv_b, conv_b)


def _ffn_kernel(x_ref, xh_ref, wa_ref, wg_ref, wd_ref, cwa_ref, cwg_ref, cba_ref, cbg_ref, lw_ref, lb_ref,
                o_ref, sa_ref, sg_ref, xe_ref, *, nf):
    first = pl.program_id(1) == 0
    f = pl.program_id(2)
    tt = x_ref.shape[0]

    @pl.when(f == 0)
    def _():
        o_ref[...] = jnp.zeros_like(o_ref)
        halo = jnp.where(first, 0.0, xh_ref[...])
        xe_ref[0:FFN_HALO, :] = halo.astype(BF16)
        xe_ref[FFN_HALO:, :] = x_ref[...].astype(BF16)

    xe = xe_ref[...]

    def branch(w_ref, cw_ref, cb_ref, s_ref, cs):
        up = _dot(xe, w_ref[:, cs])
        s_ref[:, cs] = up[FFN_HALO + tt - SUBLANES:FFN_HALO + tt, :]
        w = cw_ref[:, cs]
        x1 = pltpu.roll(up, 1, 0)[FFN_HALO:, :]
        x2 = pltpu.roll(up, 2, 0)[FFN_HALO:, :]
        return cb_ref[:, cs] + w[0:1, :] * x2 + w[1:2, :] * x1 + w[2:3, :] * up[FFN_HALO:, :]

    hs = []
    for c in range(wa_ref.shape[1] // FFN_TC):
        cs = slice(c * FFN_TC, (c + 1) * FFN_TC)
        a = branch(wa_ref, cwa_ref, cba_ref, sa_ref, cs)
        g = branch(wg_ref, cwg_ref, cbg_ref, sg_ref, cs)
        hs.append((a * jax.nn.sigmoid(a) * g).astype(BF16))
    o_ref[...] += _dot(jnp.concatenate(hs, axis=1), wd_ref[...])

    @pl.when(f == nf - 1)
    def _():
        o_ref[...] = _layer_norm(DEEPNORM_ALPHA * x_ref[...] + o_ref[...], lw_ref[...], lb_ref[...])


def _ffn_prompt(x3, w_up, w_down, layer, conv_w8, conv_b, lw, lb, *, tt, tf):
    b, t, _ = x3.shape
    nf = D_FFP // tf
    tt = min(tt, t)
    hb = tt // FFN_HALO
    half_a = lambda rows: pl.BlockSpec((rows, tf), lambda bi, ti, fi: (0, fi))
    half_g = lambda rows: pl.BlockSpec((rows, tf), lambda bi, ti, fi: (0, fi + nf))
    cols = _up_cols(tf)
    w_a = pl.BlockSpec((None, pl.Element(D_MODEL), pl.Element(tf)), lambda bi, ti, fi: (layer, 0, cols(fi)))
    w_g = pl.BlockSpec((None, pl.Element(D_MODEL), pl.Element(tf)), lambda bi, ti, fi: (layer, 0, cols(fi + nf)))
    vec = pl.BlockSpec((1, D_MODEL), lambda bi, ti, fi: (0, 0))
    tail = pl.BlockSpec((None, SUBLANES, tf), lambda bi, ti, fi: (bi, 0, fi))
    return pl.pallas_call(
        functools.partial(_ffn_kernel, nf=nf),
        grid=(b, t // tt, nf),
        in_specs=[pl.BlockSpec((None, tt, D_MODEL), lambda bi, ti, fi: (bi, ti, 0), pipeline_mode=pl.Buffered(1)),
                  pl.BlockSpec((None, FFN_HALO, D_MODEL), lambda bi, ti, fi: (bi, jnp.maximum(ti * hb - 1, 0), 0)),
                  w_a, w_g,
                  pl.BlockSpec((None, pl.Element(tf), pl.Element(D_MODEL)),
                               lambda bi, ti, fi: (layer, _ff_start(fi, tf), 0)),
                  half_a(SUBLANES), half_g(SUBLANES), half_a(1), half_g(1), vec, vec],
        out_specs=[pl.BlockSpec((None, tt, D_MODEL), lambda bi, ti, fi: (bi, ti, 0)), tail, tail],
        out_shape=[jax.ShapeDtypeStruct((b, t, D_MODEL), F32),
                   jax.ShapeDtypeStruct((b, SUBLANES, D_FFP), F32),
                   jax.ShapeDtypeStruct((b, SUBLANES, D_FFP), F32)],
        scratch_shapes=[pltpu.VMEM((tt + FFN_HALO, D_MODEL), BF16)],
        compiler_params=_cparams("parallel", "arbitrary", "arbitrary"),
        name="ffn_prompt",
    )(x3, x3, w_up, w_up, w_down, conv_w8, conv_w8, conv_b, conv_b, lw, lb)


def _cumsum_rows(x):
    n = x.shape[0]
    rows = lax.broadcasted_iota(jnp.int32, (n, 1), 0)
    d = 1
    while d < n:
        x = x + jnp.where(rows >= d, pltpu.roll(x, d, 0), 0.0)
        d *= 2
    return x


def _log_sigmoid(x):
    return jnp.minimum(x, 0.0) - jnp.log1p(jnp.exp(-jnp.abs(x)))


def _mlstm_kernel(q_ref, k_ref, v_ref, og_ref, g_ref, gb_ref, nw_ref, c0_ref, n0_ref, m0_ref,
                  out_ref, c_ref, n_ref, m_ref, *, L, t_valid):
    ci = pl.program_id(1)
    lin = q_ref.shape[0]

    @pl.when(ci == 0)
    def _():
        c_ref[...] = c0_ref[...]
        n_ref[...] = n0_ref[...]
        m_ref[...] = m0_ref[...]

    def rows_of(ref):
        x = ref[...]
        if lin < L:
            x = jnp.concatenate([x, jnp.zeros((L - lin, x.shape[1]), x.dtype)], axis=0)
        return x

    rows = lax.broadcasted_iota(jnp.int32, (L, 1), 0)
    valid = (ci * L + rows) < t_valid
    pre = rows_of(g_ref) + gb_ref[...]
    lf = jnp.where(valid, _log_sigmoid(pre), 0.0)
    ig = jnp.where(valid, pre, NEG_INF)
    bcum = _cumsum_rows(lf)
    dt = (pltpu.roll(ig, GATE_MF - GATE_MI, 1) - bcum).T
    q_all, k_all, v_all, og_all = rows_of(q_ref), rows_of(k_ref), rows_of(v_ref), rows_of(og_ref)
    tri = lax.broadcasted_iota(jnp.int32, (L, L), 0) >= lax.broadcasted_iota(jnp.int32, (L, L), 1)
    nw = nw_ref[...]

    for h in range(MLSTM_HEADS):
        sl = slice(h * HEAD_DIM, (h + 1) * HEAD_DIM)
        q = q_all[:, sl]
        k = k_all[:, sl] * QK_SCALE
        v = v_all[:, sl]
        qb, kb, vb = q.astype(BF16), k.astype(BF16), v.astype(BF16)
        b_col = bcum[:, GATE_MF + h:GATE_MF + h + 1]
        ig_col = ig[:, GATE_MI + h:GATE_MI + h + 1]
        d_row = dt[GATE_MF + h:GATE_MF + h + 1, :]
        c_prev = c_ref[h]
        n_prev = n_ref[h:h + 1, :]
        m_prev = m_ref[h:h + 1, 0:1]

        dmat = jnp.where(tri, b_col + d_row, NEG_INF)
        g_col = b_col + m_prev
        m_row = jnp.maximum(jnp.max(dmat, axis=1, keepdims=True), g_col)
        a = jnp.exp(dmat - m_row) * _dot_nt(qb, kb)
        w_inter = jnp.exp(g_col - m_row)
        num = _dot(a.astype(BF16), vb) + w_inter * _dot(qb, c_prev.astype(BF16))
        den = jnp.sum(a, axis=1, keepdims=True) + w_inter * jnp.sum(q * n_prev, axis=1, keepdims=True)
        hid = num / jnp.maximum(jnp.abs(den), jnp.exp(-m_row))
        mu = jnp.mean(hid, axis=1, keepdims=True)
        dlt = hid - mu
        var = jnp.mean(dlt * dlt, axis=1, keepdims=True)
        hn = dlt * lax.rsqrt(var + LN_EPS) * nw[:, sl]
        res = jax.nn.sigmoid(og_all[:, sl]) * hn
        out_ref[:, sl] = res[0:lin, :]

        f_tot = b_col[L - 1:L, :]
        w_s = f_tot - b_col + ig_col
        m_new = jnp.maximum(f_tot + m_prev, jnp.max(w_s, axis=0, keepdims=True))
        ws = jnp.exp(w_s - m_new)
        decay = jnp.exp(f_tot + m_prev - m_new)
        c_ref[h] = decay * c_prev + _dot(k.T.astype(BF16), (ws * v).astype(BF16))
        n_ref[h:h + 1, :] = decay * n_prev + jnp.sum(ws * k, axis=0, keepdims=True)
        m_ref[h:h + 1, :] = jnp.broadcast_to(m_new, (1, LANES))


def _mlstm(z, gate_b, norm_w, c0, n0, m0, *, L, lin, t_valid):
    b, tz, _ = z.shape
    nchunks = tz // lin
    qblk = lambda col: pl.BlockSpec((None, lin, D_MLSTM), lambda bi, ci: (bi, ci, col // D_MLSTM))
    st4 = pl.BlockSpec((None, MLSTM_HEADS, HEAD_DIM, HEAD_DIM), lambda bi, ci: (bi, 0, 0, 0))
    st3 = pl.BlockSpec((None, MLSTM_HEADS, LANES), lambda bi, ci: (bi, 0, 0))
    return pl.pallas_call(
        functools.partial(_mlstm_kernel, L=L, t_valid=t_valid),
        grid=(b, nchunks),
        in_specs=[qblk(C_MQ), qblk(C_MK), qblk(C_MV), qblk(C_MO),
                  pl.BlockSpec((None, lin, LANES), lambda bi, ci: (bi, ci, C_GATE // LANES)),
                  pl.BlockSpec((1, LANES), lambda bi, ci: (0, 0)),
                  pl.BlockSpec((1, D_MLSTM), lambda bi, ci: (0, 0)),
                  st4, st3, st3],
        out_specs=[pl.BlockSpec((None, lin, D_MLSTM), lambda bi, ci: (bi, ci, 0)), st4, st3, st3],
        out_shape=[jax.ShapeDtypeStruct((b, tz, D_MLSTM), F32),
                   jax.ShapeDtypeStruct((b, MLSTM_HEADS, HEAD_DIM, HEAD_DIM), F32),
                   jax.ShapeDtypeStruct((b, MLSTM_HEADS, LANES), F32),
                   jax.ShapeDtypeStruct((b, MLSTM_HEADS, LANES), F32)],
        compiler_params=_cparams("parallel", "arbitrary"),
        name="mlstm",
    )(z, z, z, z, z, gate_b, norm_w, c0, n0, m0)


S5_IN_TILES = S5_CH // 256
S5_IN_FEATS = D_S5 // S5_IN_TILES
S5_OUT_TILES = D_S5 // 256
S5_OUT_CH = S5_CH // S5_OUT_TILES


S5_LT = S5_CH // LANES


def _lane_tiles(x):
    return jnp.stack([x[:, k * LANES:(k + 1) * LANES] for k in range(x.shape[1] // LANES)])


def _lane_untile(x):
    return jnp.concatenate([x[k] for k in range(x.shape[0])], axis=1)


def _s5_kernel(u_ref, wb_ref, wc_ref, lr_ref, li_ref, d_ref, gw_ref, gb_ref, x0r_ref, x0i_ref,
               o_ref, xr_ref, xi_ref, sr_ref, si_ref, y_ref, *, steps):
    nb, lc, _ = u_ref.shape

    @pl.when(pl.program_id(0) == 0)
    def _():
        xr_ref[...] = x0r_ref[...]
        xi_ref[...] = x0i_ref[...]

    for b in range(nb):
        ub = u_ref[b].astype(BF16)
        for c in range(S5_IN_TILES):
            bu = _dot(ub[:, c * S5_IN_FEATS:(c + 1) * S5_IN_FEATS], wb_ref[c])
            for k in range(256 // LANES):
                rows = pl.ds(b, lc, stride=nb)
                sr_ref.at[c * (256 // LANES) + k][rows, :] = bu[:, k * LANES:(k + 1) * LANES]
                si_ref.at[c * (256 // LANES) + k][rows, :] = bu[:, 256 + k * LANES:256 + (k + 1) * LANES]
    lam_r = _lane_tiles(jnp.broadcast_to(lr_ref[...], (SUBLANES, S5_CH)))
    lam_i = _lane_tiles(jnp.broadcast_to(li_ref[...], (SUBLANES, S5_CH)))
    row8 = lax.broadcasted_iota(jnp.int32, (1, SUBLANES, 1), 1)
    per_group = SUBLANES // nb

    def group(i, carry):
        xr, xi = carry
        r0 = pl.multiple_of(i * SUBLANES, SUBLANES)
        br = sr_ref[:, pl.ds(r0, SUBLANES), :]
        bi = si_ref[:, pl.ds(r0, SUBLANES), :]
        out_r, out_i = br, bi
        for k in range(per_group):
            nr = lam_r * xr - lam_i * xi + br
            ni = lam_r * xi + lam_i * xr + bi
            here = (row8 >= k * nb) & (row8 < (k + 1) * nb)
            out_r = jnp.where(here, nr, out_r)
            out_i = jnp.where(here, ni, out_i)
            if per_group > 1:
                xr = pltpu.roll(nr, nb, 1)
                xi = pltpu.roll(ni, nb, 1)
            else:
                xr, xi = nr, ni
        sr_ref[:, pl.ds(r0, SUBLANES), :] = out_r
        si_ref[:, pl.ds(r0, SUBLANES), :] = out_i
        return xr, xi

    xr, xi = lax.fori_loop(0, steps * nb // SUBLANES, group, (_lane_tiles(xr_ref[...]), _lane_tiles(xi_ref[...])),
                           unroll=4)
    xr_ref[...] = _lane_untile(xr)
    xi_ref[...] = _lane_untile(xi)
    per_out = S5_OUT_CH // LANES
    for j in range(S5_OUT_TILES):
        lhs_r = jnp.concatenate([sr_ref[j * per_out + k] for k in range(per_out)], axis=1).astype(BF16)
        lhs_i = jnp.concatenate([si_ref[j * per_out + k] for k in range(per_out)], axis=1).astype(BF16)
        y = _dot(lhs_r, wc_ref[0, j]) + _dot(lhs_i, wc_ref[1, j])
        for k in range(256 // LANES):
            y_ref[j * (256 // LANES) + k] = y[:, k * LANES:(k + 1) * LANES]
    for b in range(nb):
        rows = pl.ds(b, lc, stride=nb)
        y = jnp.concatenate([y_ref.at[k][rows, :] for k in range(D_S5 // LANES)], axis=1) + d_ref[...] * u_ref[b]
        zz = jax.nn.gelu(y)
        o_ref[b] = zz * jax.nn.sigmoid(_dot(zz.astype(BF16), gw_ref[...]) + gb_ref[...])


def _s5(z3, wb, wc, lam_r, lam_i, d, glu_w, glu_b, x0r, x0i, *, lc, steps):
    b, tz, _ = z3.shape
    lc = min(lc, tz)
    steps = min(steps, lc)
    assert steps == lc or tz == lc
    full = lambda a: pl.BlockSpec(a.shape, lambda i: (0,) * a.ndim)
    st = pl.BlockSpec((SUBLANES, S5_CH), lambda i: (0, 0))
    return pl.pallas_call(
        functools.partial(_s5_kernel, steps=steps),
        grid=(tz // lc,),
        in_specs=[pl.BlockSpec((b, lc, D_S5), lambda i: (0, i, C_SU // D_S5)),
                  full(wb), full(wc), full(lam_r), full(lam_i), full(d), full(glu_w), full(glu_b), st, st],
        out_specs=[pl.BlockSpec((b, lc, D_S5), lambda i: (0, i, 0)), st, st],
        out_shape=[jax.ShapeDtypeStruct((b, tz, D_S5), F32),
                   jax.ShapeDtypeStruct((SUBLANES, S5_CH), F32),
                   jax.ShapeDtypeStruct((SUBLANES, S5_CH), F32)],
        scratch_shapes=[pltpu.VMEM((S5_LT, lc * b, LANES), F32), pltpu.VMEM((S5_LT, lc * b, LANES), F32),
                        pltpu.VMEM((D_S5 // LANES, lc * b, LANES), F32)],
        compiler_params=_cparams("arbitrary"),
        name="s5",
    )(z3, wb, wc, lam_r, lam_i, d, glu_w, glu_b, x0r, x0i)


N_CMB = 2 * KV_HEADS


def _cmp_project(rows_of, w_ref, o_ref):
    nch = o_ref.shape[0]
    for cmb in range(N_CMB):
        slot = cmb // KV_HEADS
        rows_ref = rows_of(cmb)
        acc = jnp.zeros((nch, 2 * HEAD_DIM), F32)
        for l in range(CMP_STRIDE):
            x = rows_ref[pl.ds(l, nch, stride=CMP_STRIDE), :]
            acc += _dot(x.astype(BF16), w_ref[slot, l])
        o_ref[:, cmb * 2 * HEAD_DIM:(cmb + 1) * 2 * HEAD_DIM] = acc


def _cmp_p_kernel(r0_ref, r1_ref, r2_ref, r3_ref, w_ref, o_ref):
    rows = (r0_ref, r1_ref, r2_ref, r3_ref)
    _cmp_project(lambda cmb: rows[cmb], w_ref, o_ref)


def _cmp_project_prompt(z, wcat):
    b, t, _ = z.shape
    nch = t // CMP_STRIDE
    rows = lambda cmb: pl.BlockSpec((None, t, HEAD_DIM), lambda bi: (bi, 0, C_NKV // HEAD_DIM + cmb))
    return pl.pallas_call(
        _cmp_p_kernel,
        grid=(b,),
        in_specs=[rows(cmb) for cmb in range(N_CMB)] + [pl.BlockSpec(wcat.shape, lambda bi: (0, 0, 0, 0))],
        out_specs=pl.BlockSpec((None, nch, N_CMB * 2 * HEAD_DIM), lambda bi: (bi, 0, 0)),
        out_shape=jax.ShapeDtypeStruct((b, nch, N_CMB * 2 * HEAD_DIM), F32),
        compiler_params=_cparams("parallel"),
        name="cmp_project_prompt",
    )(z, z, z, z, wcat)


def _cmp_s_kernel(pt_ref, *refs):
    del pt_ref
    per_pos = 2 * KV_HEADS
    page_refs = [r.reshape(PAGE_SIZE * per_pos, HEAD_DIM) for r in refs[:CMP_PAGES]]
    w_ref, o_ref = refs[CMP_PAGES:]
    per_page = PAGE_SIZE // CMP_STRIDE
    for cmb in range(N_CMB):
        acc = jnp.zeros((o_ref.shape[0], 2 * HEAD_DIM), F32)
        for l in range(CMP_STRIDE):
            x = jnp.concatenate(
                [page_refs[p][pl.ds(l * per_pos + cmb, per_page, stride=CMP_STRIDE * per_pos), :]
                 for p in range(CMP_PAGES)], axis=0)
            acc += _dot(x.astype(BF16), w_ref[cmb // KV_HEADS, l])
        o_ref[:, cmb * 2 * HEAD_DIM:(cmb + 1) * 2 * HEAD_DIM] = acc


def _page_spec(layer, slot, p, pages_per_step, grid_rank):
    def index_map(*idx):
        bi, si, pt = idx[0], idx[grid_rank - 1], idx[grid_rank]
        return (layer, pt[bi, si * pages_per_step + p], 0, slot, 0, 0)

    return pl.BlockSpec((None, None, PAGE_SIZE, None, KV_HEADS, HEAD_DIM), index_map)


def _cmp_project_sample(cache6, page_table, layer, wcat):
    b, npages = page_table.shape
    steps = npages // CMP_PAGES
    nch = CMP_PAGES * PAGE_SIZE // CMP_STRIDE
    def page_pair(p):
        return pl.BlockSpec((None, None, PAGE_SIZE, 2, KV_HEADS, HEAD_DIM),
                            lambda bi, si, pt: (layer, pt[bi, si * CMP_PAGES + p], 0, 0, 0, 0))

    pages = [page_pair(p) for p in range(CMP_PAGES)]

    grid_spec = pltpu.PrefetchScalarGridSpec(
        num_scalar_prefetch=1,
        grid=(b, steps),
        in_specs=pages + [pl.BlockSpec(wcat.shape, lambda bi, si, pt: (0, 0, 0, 0))],
        out_specs=pl.BlockSpec((None, nch, N_CMB * 2 * HEAD_DIM), lambda bi, si, pt: (bi, si, 0)),
    )
    return pl.pallas_call(
        _cmp_s_kernel,
        grid_spec=grid_spec,
        out_shape=jax.ShapeDtypeStruct((b, steps * nch, N_CMB * 2 * HEAD_DIM), F32),
        compiler_params=_cparams("parallel", "arbitrary"),
        name="cmp_project_sample",
    )(page_table, *([cache6] * CMP_PAGES), wcat)


def _cmp_fin_kernel(p_ref, b1_ref, w2_ref, o_ref):
    nch = p_ref.shape[0]
    for cmb in range(N_CMB):
        slot = cmb // KV_HEADS
        c0 = cmb * 2 * HEAD_DIM
        first = p_ref[:, c0:c0 + HEAD_DIM]
        second = pltpu.roll(p_ref[:, c0 + HEAD_DIM:c0 + 2 * HEAD_DIM], nch - 1, 0)
        hid = b1_ref[slot:slot + 1, :] + first + second
        o_ref[cmb] = _dot(jax.nn.gelu(hid).astype(BF16), w2_ref[slot])


def _cmp_finish(p, b1, w2):
    b, nch, _ = p.shape
    return pl.pallas_call(
        _cmp_fin_kernel,
        grid=(b,),
        in_specs=[pl.BlockSpec((None, nch, p.shape[2]), lambda bi: (bi, 0, 0)),
                  pl.BlockSpec(b1.shape, lambda bi: (0, 0)),
                  pl.BlockSpec(w2.shape, lambda bi: (0, 0, 0))],
        out_specs=pl.BlockSpec((None, N_CMB, nch, HEAD_DIM), lambda bi: (bi, 0, 0, 0)),
        out_shape=jax.ShapeDtypeStruct((b, N_CMB, nch, HEAD_DIM), F32),
        compiler_params=_cparams("parallel"),
        name="cmp_finish",
    )(p, b1, w2)


def _softmax_init(m_scr, l_scr, acc_scr):
    m_scr[...] = jnp.full(m_scr.shape, M_INIT, F32)
    l_scr[...] = jnp.zeros(l_scr.shape, F32)
    acc_scr[...] = jnp.zeros(acc_scr.shape, F32)


def _softmax_update(s, mask, vb, m_scr, l_scr, acc_scr):
    s = jnp.where(mask, s, NEG_INF)
    m_prev = m_scr[...]
    m_new = jnp.maximum(m_prev, jnp.max(s, axis=1, keepdims=True))
    alpha = jnp.exp(m_prev - m_new)
    p = jnp.exp(s - m_new)
    l_scr[...] = alpha * l_scr[...] + jnp.sum(p, axis=1, keepdims=True)
    acc_scr[...] = alpha * acc_scr[...] + _dot(p.astype(BF16), vb)
    m_scr[...] = m_new


def _softmax_result(l_scr, acc_scr):
    l = l_scr[...]
    return acc_scr[...] / jnp.where(l > 0, l, 1.0)


def _masked_probs(s, mask):
    s = jnp.where(mask, s, NEG_INF)
    m = jnp.max(s, axis=1, keepdims=True)
    m = jnp.where(m > NEG_INF, m, 0.0)
    p = jnp.exp(s - m)
    den = jnp.sum(p, axis=1, keepdims=True)
    return p / jnp.where(den > 0, den, 1.0)


def _stack_heads(q):
    return jnp.concatenate([q[:, j * HEAD_DIM:(j + 1) * HEAD_DIM] for j in range(GROUP)], axis=0)


def _fold_lanes(x, op):
    out = x[:, 0:LANES]
    for k in range(1, x.shape[1] // LANES):
        out = op(out, x[:, k * LANES:(k + 1) * LANES])
    return out


def _block_importance(pc, n_cmp, n_sel):
    ncp = pc.shape[1]
    nsp = -(-n_sel // LANES) * LANES
    c_start = lax.broadcasted_iota(jnp.int32, (ncp, nsp), 0) * CMP_STRIDE
    s_start = lax.broadcasted_iota(jnp.int32, (ncp, nsp), 1) * SEL_BLOCK
    overlap = ((c_start < s_start + SEL_BLOCK) & (c_start + CMP_BLOCK > s_start)
               & (c_start < n_cmp * CMP_STRIDE)).astype(F32)
    return jnp.dot(pc, overlap, preferred_element_type=F32, precision=lax.Precision.HIGHEST)


def _force_blocks(imp, blk, cur):
    forced = (blk == 0) | (blk == cur) | (blk == cur - 1)
    imp = jnp.where(forced, FORCE_SCORE, imp)
    return jnp.where(blk > cur, -1.0, imp)


def _select_blocks_cols(pc, cur, n_cmp, n_sel):
    imp = _block_importance(pc, n_cmp, n_sel)
    blk = lax.broadcasted_iota(jnp.int32, (1, imp.shape[1]), 1)
    imp = _force_blocks(imp, blk, cur)

    def body(sp, count):
        col = jnp.sum(jnp.where(blk == sp, imp, 0.0), axis=1, keepdims=True)
        ahead = (col > imp) | ((col == imp) & (sp < blk))
        return count + ahead.astype(F32)

    count = lax.fori_loop(0, n_sel, body, jnp.zeros(imp.shape, F32), unroll=True)
    return ((count < min(SEL_TOPK, n_sel)) & (blk < n_sel)).astype(F32)


def _select_blocks_rows(pc, cur_row, n_cmp, n_sel):
    rows = pc.shape[0]
    nrow = -(-n_sel // SUBLANES) * SUBLANES
    imp_t = _block_importance(pc, n_cmp, n_sel).T[0:nrow, :]
    blk = lax.broadcasted_iota(jnp.int32, (nrow, 1), 0)
    imp_t = _force_blocks(imp_t, blk, cur_row)
    count = jnp.zeros(imp_t.shape, F32)
    for sp in range(n_sel):
        row = imp_t[sp:sp + 1, :]
        count += ((row > imp_t) | ((row == imp_t) & (sp < blk))).astype(F32)
    sel_t = ((count < min(SEL_TOPK, n_sel)) & (blk < n_sel)).astype(F32)
    sel_t = jnp.concatenate([sel_t, jnp.zeros((LANES - nrow, rows), F32)], axis=0)
    return sel_t.T


def _gate(gt, col):
    lane = lax.broadcasted_iota(jnp.int32, (1, LANES), 1)
    return jax.nn.sigmoid(jnp.sum(jnp.where(lane == col, gt, 0.0), axis=1, keepdims=True))


def _nsa_prompt_kernel(q_ref, ks_ref, vs_ref, kw_ref, vw_ref, kc_ref, vc_ref, bc_ref, tz_ref, gt_ref, gb_ref,
                       o_ref, s_scr, selb_scr, mx_scr, l_scr, acc_scr, *, t_len):
    g = pl.program_id(1)
    qt = pl.program_id(2)
    q0 = qt * TQ
    n_cmp = t_len // CMP_STRIDE - (CMP_BLOCK // CMP_STRIDE) + 1
    n_sel = t_len // SEL_BLOCK
    ncp = kc_ref.shape[0]
    rows4 = GROUP * TQ
    qs = _stack_heads(q_ref[...] * QK_SCALE).astype(BF16)
    qpos = q0 + (lax.broadcasted_iota(jnp.int32, (rows4, 1), 0) & (TQ - 1))

    n_idx = lax.broadcasted_iota(jnp.int32, (1, ncp), 1)
    s_c = _dot_nt(qs, kc_ref[...].astype(BF16)) + bc_ref[...].reshape(rows4, ncp)
    cmask = (qpos - (n_idx * CMP_STRIDE + CMP_BLOCK - 1) >= 0) & (n_idx < n_cmp)
    p_c = _masked_probs(s_c, cmask)
    o_c = _dot(p_c.astype(BF16), vc_ref[...].astype(BF16))
    pc = p_c[0:TQ] + p_c[TQ:2 * TQ] + p_c[2 * TQ:3 * TQ] + p_c[3 * TQ:4 * TQ]
    cur_row = jnp.right_shift(q0 + lax.broadcasted_iota(jnp.int32, (1, TQ), 1), SEL_SHIFT)
    sel = _select_blocks_rows(pc, cur_row, n_cmp, n_sel)
    sel_add = ((sel - 1.0) * -MASKED).astype(BF16)

    kt_hi = (q0 + TQ - 1) // TK + 1
    kt_far = jnp.maximum(q0 - (FAR_DIST - 1), 0) // TK
    kidx = lax.broadcasted_iota(jnp.int32, (1, TK), 1)
    srow = lax.broadcasted_iota(jnp.int32, (LANES, TK), 0)

    def spread_selection(kt, carry):
        k0 = kt * TK
        expand = (jnp.right_shift(k0 + lax.broadcasted_iota(jnp.int32, (LANES, TK), 1), SEL_SHIFT) == srow).astype(BF16)
        selb_scr[kt] = _dot(sel_add, expand)
        return carry

    lax.fori_loop(0, kt_hi, spread_selection, 0)

    def tile_bias(k0):
        return tz_ref[jnp.clip((q0 - k0) // TQ, 0, N_TZ - 1)].reshape(rows4, TK)

    def attend(k_ref, v_ref, kt_lo, mask_tile):
        mx_scr[...] = jnp.full(mx_scr.shape, NEG_INF, F32)

        def scores(near):
            def body(kt, carry):
                k0 = pl.multiple_of(kt * TK, TK)
                s = _dot_nt(qs, k_ref[pl.ds(k0, TK), :].astype(BF16))
                if near:
                    s = s + tile_bias(k0)
                s = mask_tile(k0, s, near)
                s_scr[kt] = s
                mx_scr[...] = jnp.maximum(mx_scr[...], _fold_lanes(s, jnp.maximum))
                return carry
            return body

        kt_mid = jnp.maximum(kt_far, kt_lo)
        lax.fori_loop(kt_lo, kt_mid, scores(False), 0)
        lax.fori_loop(kt_mid, kt_hi, scores(True), 0)
        m = jnp.max(mx_scr[...], axis=1, keepdims=True)
        mx_scr[...] = jnp.broadcast_to(jnp.where(m > NEG_INF, m, 0.0), mx_scr.shape)
        l_scr[...] = jnp.zeros(l_scr.shape, F32)
        acc_scr[...] = jnp.zeros(acc_scr.shape, F32)

        def probs(kt, carry):
            k0 = pl.multiple_of(kt * TK, TK)
            m_rep = mx_scr[...]
            p = jnp.exp(s_scr[kt] - jnp.concatenate([m_rep] * (TK // LANES), axis=1))
            l_scr[...] += _fold_lanes(p, jnp.add)
            acc_scr[...] += _dot(p.astype(BF16), v_ref[pl.ds(k0, TK), :].astype(BF16))
            return carry

        lax.fori_loop(kt_lo, kt_hi, probs, 0)
        l = jnp.sum(l_scr[...], axis=1, keepdims=True)
        return acc_scr[...] / jnp.where(l > 0, l, 1.0)

    def sel_mask(k0, s, near):
        s = (s.reshape(GROUP, TQ, TK) + selb_scr[k0 // TK][None]).reshape(rows4, TK)
        return jnp.where(k0 + kidx <= qpos, s, NEG_INF) if near else s

    o_s = attend(ks_ref, vs_ref, 0, sel_mask)

    def win_mask(k0, s, near):
        dist = qpos - (k0 + kidx)
        return jnp.where((dist >= 0) & (dist <= WINDOW), s, NEG_INF)

    o_w = attend(kw_ref, vw_ref, jnp.maximum(q0 - WINDOW, 0) // TK, win_mask)

    gt = gt_ref[...] + gb_ref[...]
    for j in range(GROUP):
        col = GATE_NG + (g * GROUP + j) * 3
        r = slice(j * TQ, (j + 1) * TQ)
        o_ref[:, j * HEAD_DIM:(j + 1) * HEAD_DIM] = (
            _gate(gt, col) * o_c[r] + _gate(gt, col + 1) * o_s[r] + _gate(gt, col + 2) * o_w[r])


def _nsa_prompt(z, kcvc, bias_c, tz, gate_b):
    b, t, _ = z.shape
    ncp = kcvc.shape[2]
    nq = t // TQ
    kv = lambda slot: pl.BlockSpec((None, t, HEAD_DIM), lambda bi, g, qi: (bi, 0, C_NKV // HEAD_DIM + slot * KV_HEADS + g),
                                   pipeline_mode=pl.Buffered(1))
    cmp_blk = lambda slot: pl.BlockSpec((None, None, ncp, HEAD_DIM), lambda bi, g, qi: (bi, slot * KV_HEADS + g, 0, 0))
    return pl.pallas_call(
        functools.partial(_nsa_prompt_kernel, t_len=t),
        grid=(b, KV_HEADS, nq),
        in_specs=[pl.BlockSpec((None, TQ, GROUP * HEAD_DIM), lambda bi, g, qi: (bi, qi, C_NQ // (GROUP * HEAD_DIM) + g)),
                  kv(2), kv(3), kv(4), kv(5), cmp_blk(0), cmp_blk(1),
                  pl.BlockSpec((GROUP, TQ, ncp), lambda bi, g, qi: (g, qi, 0)),
                  pl.BlockSpec((N_TZ, GROUP, TQ, TK), lambda bi, g, qi: (0, g, 0, 0), pipeline_mode=pl.Buffered(1)),
                  pl.BlockSpec((None, TQ, LANES), lambda bi, g, qi: (bi, qi, C_GATE // LANES)),
                  pl.BlockSpec((1, LANES), lambda bi, g, qi: (0, 0))],
        out_specs=pl.BlockSpec((None, TQ, GROUP * HEAD_DIM), lambda bi, g, qi: (bi, qi, g)),
        out_shape=jax.ShapeDtypeStruct((b, t, D_NSA), F32),
        scratch_shapes=[pltpu.VMEM((t // TK, GROUP * TQ, TK), F32), pltpu.VMEM((t // TK, TQ, TK), F32),
                        pltpu.VMEM((GROUP * TQ, LANES), F32),
                        pltpu.VMEM((GROUP * TQ, LANES), F32), pltpu.VMEM((GROUP * TQ, HEAD_DIM), F32)],
        compiler_params=_cparams("parallel", "parallel", "arbitrary"),
        name="nsa_prompt",
    )(z, z, z, z, z, kcvc, kcvc, bias_c, tz, z, gate_b)


TS = SUBLANES
ROWS_S = GROUP * TS


def _nsa_s_cmp_kernel(q_ref, kc_ref, vc_ref, bc_ref, oc_ref, sel_ref, *, pos0, n_cmp, n_sel):
    nch = kc_ref.shape[0]
    qs = _stack_heads(q_ref[...] * QK_SCALE).astype(BF16)
    qpos = pos0 + (lax.broadcasted_iota(jnp.int32, (ROWS_S, 1), 0) & (TS - 1))
    n_idx = lax.broadcasted_iota(jnp.int32, (1, nch), 1)
    s_c = _dot_nt(qs, kc_ref[...].astype(BF16)) + bc_ref[...].reshape(ROWS_S, nch)
    cmask = (qpos - (n_idx * CMP_STRIDE + CMP_BLOCK - 1) >= 0) & (n_idx < n_cmp)
    p_c = _masked_probs(s_c, cmask)
    oc_ref[...] = _dot(p_c.astype(BF16), vc_ref[...].astype(BF16))
    pc = p_c[0:TS] + p_c[TS:2 * TS] + p_c[2 * TS:3 * TS] + p_c[3 * TS:4 * TS]
    sel_ref[...] = _select_blocks_cols(pc, jnp.right_shift(qpos[0:TS], SEL_SHIFT), n_cmp, n_sel)


def _nsa_sample_cmp(z8, kcvc, bias_c, *, pos0, n_cmp, n_sel):
    b = z8.shape[0]
    nch = kcvc.shape[2]
    nsp = -(-n_sel // LANES) * LANES
    cmp_blk = lambda slot: pl.BlockSpec((None, None, nch, HEAD_DIM), lambda bi, g: (bi, slot * KV_HEADS + g, 0, 0))
    return pl.pallas_call(
        functools.partial(_nsa_s_cmp_kernel, pos0=pos0, n_cmp=n_cmp, n_sel=n_sel),
        grid=(b, KV_HEADS),
        in_specs=[pl.BlockSpec((None, TS, GROUP * HEAD_DIM), lambda bi, g: (bi, 0, C_NQ // (GROUP * HEAD_DIM) + g)),
                  cmp_blk(0), cmp_blk(1),
                  pl.BlockSpec((GROUP, TS, nch), lambda bi, g: (g, 0, 0))],
        out_specs=[pl.BlockSpec((None, None, ROWS_S, HEAD_DIM), lambda bi, g: (bi, g, 0, 0)),
                   pl.BlockSpec((None, None, TS, nsp), lambda bi, g: (bi, g, 0, 0))],
        out_shape=[jax.ShapeDtypeStruct((b, KV_HEADS, ROWS_S, HEAD_DIM), F32),
                   jax.ShapeDtypeStruct((b, KV_HEADS, TS, nsp), F32)],
        compiler_params=_cparams("parallel", "parallel"),
        name="nsa_sample_cmp",
    )(z8, kcvc, kcvc, bias_c)


def _pad_rows(x, n):
    return jnp.concatenate([x, jnp.zeros((n - x.shape[0], x.shape[1]), x.dtype)], axis=0)


def _nsa_s_sel_kernel(pt_ref, q_ref, sel_ref, kn_ref, vn_ref, bt_ref, bn_ref, bf_ref, *refs, pos0, npages):
    del pt_ref
    rows_pp = PAGE_SIZE * KV_HEADS
    k_refs = [r.reshape(rows_pp, HEAD_DIM) for r in refs[:SEL_PAGES]]
    v_refs = [r.reshape(rows_pp, HEAD_DIM) for r in refs[SEL_PAGES:2 * SEL_PAGES]]
    o_ref, kbuf, vbuf, m_scr, l_scr, acc_scr = refs[2 * SEL_PAGES:]
    step = pl.program_id(1)
    last = step == npages // SEL_PAGES - 1
    rows = KV_HEADS * ROWS_S
    ncol = SEL_TILE * rows_pp
    head_shift = KV_HEADS.bit_length() - 1
    blk_per_page = PAGE_SIZE // SEL_BLOCK

    @pl.when(step == 0)
    def _():
        _softmax_init(m_scr, l_scr, acc_scr)

    nsp = sel_ref.shape[-1]
    ridx = lax.broadcasted_iota(jnp.int32, (rows, 1), 0)
    qpos = pos0 + (ridx & (TS - 1))
    qs = jnp.concatenate([_stack_heads(q_ref[:, g * GROUP * HEAD_DIM:(g + 1) * GROUP * HEAD_DIM] * QK_SCALE)
                          for g in range(KV_HEADS)], axis=0).astype(BF16)
    sel_rows = jnp.concatenate([sel_ref[g] for g in range(KV_HEADS) for _ in range(GROUP)], axis=0)
    bias_far = bf_ref[...].reshape(rows, PAGE_SIZE)[:, 0:1]
    col = lax.broadcasted_iota(jnp.int32, (1, ncol), 1)
    own_head = (col & (KV_HEADS - 1)) == ridx // ROWS_S
    expand = (jnp.right_shift(lax.broadcasted_iota(jnp.int32, (LANES, ncol), 1), SEL_SHIFT + head_shift)
              == lax.broadcasted_iota(jnp.int32, (LANES, ncol), 0)).astype(BF16)
    blk_r = lax.broadcasted_iota(jnp.int32, (nsp, LANES), 0)
    blk_c = lax.broadcasted_iota(jnp.int32, (nsp, LANES), 1)
    n_tiles = SEL_PAGES // SEL_TILE

    for tile in range(n_tiles):
        for i in range(SEL_TILE):
            p = tile * SEL_TILE + i
            kbuf[i * rows_pp:(i + 1) * rows_pp, :] = k_refs[p][...].astype(BF16)
            vbuf[i * rows_pp:(i + 1) * rows_pp, :] = v_refs[p][...].astype(BF16)
        page0 = step * SEL_PAGES + tile * SEL_TILE
        window = ((blk_r == page0 * blk_per_page + blk_c) & (blk_c < SEL_TILE * blk_per_page)).astype(BF16)
        chosen = _dot(_dot(sel_rows.astype(BF16), window).astype(BF16), expand) > 0.5
        s = _dot_nt(qs, kbuf[...]) + bias_far
        if tile == n_tiles - 1:
            tail = jnp.where(last, bt_ref[...].reshape(rows, rows_pp) - bias_far, 0.0)
            s = s + jnp.concatenate([jnp.zeros((rows, ncol - rows_pp), F32), tail], axis=1)
        key = page0 * PAGE_SIZE + jnp.right_shift(col, head_shift)
        _softmax_update(s, chosen & own_head & (key <= qpos), vbuf[...], m_scr, l_scr, acc_scr)

    @pl.when(last)
    def _():
        kidx = lax.broadcasted_iota(jnp.int32, (1, PAGE_SIZE), 1)
        new_blk = lax.broadcasted_iota(jnp.int32, (1, nsp), 1) == npages * blk_per_page
        flag = jnp.sum(jnp.where(new_blk, sel_rows, 0.0), axis=1, keepdims=True) > 0.5
        bias_new = bn_ref[...].reshape(rows, PAGE_SIZE)
        for g in range(KV_HEADS):
            r = slice(g * ROWS_S, (g + 1) * ROWS_S)
            kn = _pad_rows(kn_ref[:, g * HEAD_DIM:(g + 1) * HEAD_DIM], PAGE_SIZE).astype(BF16)
            vn = _pad_rows(vn_ref[:, g * HEAD_DIM:(g + 1) * HEAD_DIM], PAGE_SIZE).astype(BF16)
            mask = flag[r] & (pos0 + kidx <= qpos[r]) & (kidx < TS)
            _softmax_update(_dot_nt(qs[r], kn) + bias_new[r], mask, vn, m_scr.at[r], l_scr.at[r], acc_scr.at[r])
            o_ref[g] = _softmax_result(l_scr.at[r], acc_scr.at[r])


def _nsa_sample_sel(z8, sel, bias_tail, bias_new, bias_far, cache6, page_table, layer, *, pos0):
    b, npages = page_table.shape
    nsp = sel.shape[-1]
    steps = npages // SEL_PAGES
    slot_cols = KV_HEADS * HEAD_DIM
    new_rows = lambda slot: pl.BlockSpec((None, TS, slot_cols),
                                         lambda bi, si, pt: (bi, 0, (C_NKV + slot * slot_cols) // slot_cols))
    whole = lambda a: pl.BlockSpec(a.shape, lambda bi, si, pt: (0,) * a.ndim)
    grid_spec = pltpu.PrefetchScalarGridSpec(
        num_scalar_prefetch=1,
        grid=(b, steps),
        in_specs=[pl.BlockSpec((None, TS, D_NSA), lambda bi, si, pt: (bi, 0, C_NQ // D_NSA)),
                  pl.BlockSpec((None, KV_HEADS, TS, nsp), lambda bi, si, pt: (bi, 0, 0, 0)),
                  new_rows(2), new_rows(3), whole(bias_tail), whole(bias_new), whole(bias_far)]
                 + [_page_spec(layer, 2, p, SEL_PAGES, 2) for p in range(SEL_PAGES)]
                 + [_page_spec(layer, 3, p, SEL_PAGES, 2) for p in range(SEL_PAGES)],
        out_specs=pl.BlockSpec((None, KV_HEADS, ROWS_S, HEAD_DIM), lambda bi, si, pt: (bi, 0, 0, 0)),
        scratch_shapes=[pltpu.VMEM((SEL_TILE * PAGE_SIZE * KV_HEADS, HEAD_DIM), BF16),
                        pltpu.VMEM((SEL_TILE * PAGE_SIZE * KV_HEADS, HEAD_DIM), BF16),
                        pltpu.VMEM((KV_HEADS * ROWS_S, 1), F32), pltpu.VMEM((KV_HEADS * ROWS_S, 1), F32),
                        pltpu.VMEM((KV_HEADS * ROWS_S, HEAD_DIM), F32)],
    )
    return pl.pallas_call(
        functools.partial(_nsa_s_sel_kernel, pos0=pos0, npages=npages),
        grid_spec=grid_spec,
        out_shape=jax.ShapeDtypeStruct((b, KV_HEADS, ROWS_S, HEAD_DIM), F32),
        compiler_params=_cparams("parallel", "arbitrary"),
        name="nsa_sample_sel",
    )(page_table, z8, sel, z8, z8, bias_tail, bias_new, bias_far, *([cache6] * (2 * SEL_PAGES)))


def _nsa_s_win_kernel(q_ref, kw_ref, vw_ref, kn_ref, vn_ref, bw_ref, oc_ref, os_ref, gt_ref, gb_ref, o_ref,
                      m_scr, l_scr, acc_scr, *, pos0):
    g = pl.program_id(1)
    wb = kw_ref.shape[0]
    own = pl.ds(g, wb, stride=KV_HEADS)
    kw = kw_ref.reshape(wb * KV_HEADS, HEAD_DIM)[own, :]
    vw = vw_ref.reshape(wb * KV_HEADS, HEAD_DIM)[own, :]
    qs = _stack_heads(q_ref[...] * QK_SCALE).astype(BF16)
    qpos = pos0 + (lax.broadcasted_iota(jnp.int32, (ROWS_S, 1), 0) & (TS - 1))
    bias = bw_ref[...].reshape(ROWS_S, wb + PAGE_SIZE)
    _softmax_init(m_scr, l_scr, acc_scr)
    dist = qpos - (pos0 - wb + lax.broadcasted_iota(jnp.int32, (1, wb), 1))
    s = _dot_nt(qs, kw.astype(BF16)) + bias[:, 0:wb]
    _softmax_update(s, (dist >= 0) & (dist <= WINDOW), vw.astype(BF16), m_scr, l_scr, acc_scr)
    kidx = lax.broadcasted_iota(jnp.int32, (1, PAGE_SIZE), 1)
    dist = qpos - (pos0 + kidx)
    s = _dot_nt(qs, _pad_rows(kn_ref[...], PAGE_SIZE).astype(BF16)) + bias[:, wb:wb + PAGE_SIZE]
    _softmax_update(s, (dist >= 0) & (dist <= WINDOW) & (kidx < TS), _pad_rows(vn_ref[...], PAGE_SIZE).astype(BF16),
                    m_scr, l_scr, acc_scr)
    o_w = _softmax_result(l_scr, acc_scr)
    o_c = oc_ref[...]
    o_s = os_ref[...]
    gt = gt_ref[...] + gb_ref[...]
    for j in range(GROUP):
        col = GATE_NG + (g * GROUP + j) * 3
        r = slice(j * TS, (j + 1) * TS)
        o_ref[:, j * HEAD_DIM:(j + 1) * HEAD_DIM] = (
            _gate(gt, col) * o_c[r] + _gate(gt, col + 1) * o_s[r] + _gate(gt, col + 2) * o_w[r])


def _nsa_sample_win(z8, win6, layer, bias_w, o_c, o_s, gate_b, *, pos0):
    _, b, wb = win6.shape[:3]
    cached = lambda slot: pl.BlockSpec((None, None, wb, None, KV_HEADS, HEAD_DIM),
                                       lambda bi, g: (layer, bi, 0, slot, 0, 0))
    zcol = lambda col: (lambda bi, g: (bi, 0, col // HEAD_DIM + g))
    part = pl.BlockSpec((None, None, ROWS_S, HEAD_DIM), lambda bi, g: (bi, g, 0, 0))
    return pl.pallas_call(
        functools.partial(_nsa_s_win_kernel, pos0=pos0),
        grid=(b, KV_HEADS),
        in_specs=[pl.BlockSpec((None, TS, GROUP * HEAD_DIM), lambda bi, g: (bi, 0, C_NQ // (GROUP * HEAD_DIM) + g)),
                  cached(0), cached(1),
                  pl.BlockSpec((None, TS, HEAD_DIM), zcol(C_NKV + 4 * KV_HEADS * HEAD_DIM)),
                  pl.BlockSpec((None, TS, HEAD_DIM), zcol(C_NKV + 5 * KV_HEADS * HEAD_DIM)),
                  pl.BlockSpec((GROUP, TS, wb + PAGE_SIZE), lambda bi, g: (g, 0, 0)),
                  part, part,
                  pl.BlockSpec((None, TS, LANES), lambda bi, g: (bi, 0, C_GATE // LANES)),
                  pl.BlockSpec((1, LANES), lambda bi, g: (0, 0))],
        out_specs=pl.BlockSpec((None, TS, GROUP * HEAD_DIM), lambda bi, g: (bi, 0, g)),
        out_shape=jax.ShapeDtypeStruct((b, TS, D_NSA), F32),
        scratch_shapes=[pltpu.VMEM((ROWS_S, 1), F32), pltpu.VMEM((ROWS_S, 1), F32), pltpu.VMEM((ROWS_S, HEAD_DIM), F32)],
        compiler_params=_cparams("parallel", "parallel"),
        name="nsa_sample_win",
    )(z8, win6, win6, z8, z8, bias_w, o_c, o_s, z8, gate_b)


def _t5_bucket(dist):
    n = np.maximum(dist, 0)
    exact = NUM_BUCKETS // 2
    nf = np.maximum(n, 1).astype(np.float32)
    large = exact + (np.log(nf / np.float32(exact)) / np.float32(math.log(MAX_DISTANCE / exact))
                     * np.float32(NUM_BUCKETS - exact)).astype(np.int32)
    return np.where(n < exact, n, np.minimum(large, NUM_BUCKETS - 1)).astype(np.int32)


FAR_DIST = int(np.max(np.nonzero(_t5_bucket(np.arange(4 * MAX_DISTANCE)) < NUM_BUCKETS - 1)[0])) + 1
N_TZ = (FAR_DIST + TK - 2) // TQ + 1


def _bias_table(rel_bias, dist):
    onehot = jax.nn.one_hot(jnp.asarray(_t5_bucket(dist).astype(np.int8)), NUM_BUCKETS, dtype=F32)
    return jnp.einsum("rcb,bh->hrc", onehot, rel_bias, precision=lax.Precision.HIGHEST)


def _pack_weights(p):
    w_in = p["w_in"].astype(BF16)
    pad = jnp.zeros(w_in.shape[:2] + (N_IN - C_GATE - 32,), BF16)
    w_in_p = jnp.concatenate([w_in[..., 0:2048], w_in[..., 2056:3080], w_in[..., 3080:4616], w_in[..., 4640:5152],
                              w_in[..., 2048:2056], w_in[..., 4616:4640], pad], axis=2)
    return dict(
        w_in=w_in_p, w_out=p["w_out"].astype(BF16), w_down=p["ffn_w_down"].astype(BF16),
        w_up=p["ffn_w_up"].astype(BF16))


def _pack_layer(p, big, l):
    gb = p["mlstm_gate_b"][l]
    mlstm_gate_b = jnp.zeros((1, LANES), F32).at[0, GATE_MI:GATE_MI + 4].set(gb[0]).at[0, GATE_MF:GATE_MF + 4].set(gb[1])
    nsa_gate_b = jnp.zeros((1, LANES), F32).at[0, GATE_NG:GATE_NG + 3 * NSA_HEADS].set(p["nsa_gate_b"][l].reshape(-1))
    w1 = p["cmp_w1"][l]
    wcat = jnp.concatenate([w1[:, 0:CMP_STRIDE], w1[:, CMP_STRIDE:CMP_BLOCK]], axis=-1).astype(BF16)
    lam = lax.complex(p["s5_a_re"][l], p["s5_a_im"][l])
    lam_bar = jnp.exp(lam * jnp.exp(p["s5_log_step"][l])[:, None])
    b_bar = ((lam_bar - 1.0) / lam)[..., None] * lax.complex(p["s5_b_re"][l], p["s5_b_im"][l])
    gi = S5_GROUPS // S5_IN_TILES
    bd_in = lambda m: jnp.einsum("cgph,gk->cghkp", m.reshape(S5_IN_TILES, gi, S5_STATE, S5_GROUP_WIDTH),
                                 jnp.eye(gi, dtype=F32)).reshape(S5_IN_TILES, S5_IN_FEATS, 256)
    wb = jnp.concatenate([bd_in(b_bar.real), bd_in(b_bar.imag)], axis=2).astype(BF16)
    go = S5_GROUPS // S5_OUT_TILES
    bd_out = lambda m: jnp.einsum("jghp,gk->jgpkh", m.reshape(S5_OUT_TILES, go, S5_GROUP_WIDTH, S5_STATE),
                                  jnp.eye(go, dtype=F32)).reshape(S5_OUT_TILES, S5_OUT_CH, 256)
    wc = jnp.stack([bd_out(p["s5_c_re"][l]), -bd_out(p["s5_c_im"][l])]).astype(BF16)
    half = lambda w: _to_tiled_ff(w, FFN_TF)
    conv_w = p["ffn_conv_w"][l]
    conv_b = p["ffn_conv_b"][l][None, :]
    return dict(
        big, layer=l, mlstm_gate_b=mlstm_gate_b, mlstm_norm_w=p["mlstm_norm_w"][l][None, :], nsa_gate_b=nsa_gate_b,
        wcat=wcat, cmp_b1=p["cmp_b1"][l], cmp_w2=p["cmp_w2"][l].astype(BF16),
        s5_wb=wb, s5_wc=wc, s5_lam_r=lam_bar.real.reshape(1, S5_CH), s5_lam_i=lam_bar.imag.reshape(1, S5_CH),
        s5_d=p["s5_d"][l][None, :], s5_glu_w=p["s5_glu_w"][l].astype(BF16), s5_glu_b=p["s5_glu_b"][l][None, :],
        ln1_w=p["ln1_w"][l][None, :], ln1_b=p["ln1_b"][l][None, :],
        conv_w8=jnp.pad(jnp.concatenate([half(conv_w[:, :D_FF]), half(conv_w[:, D_FF:])], axis=1),
                        ((0, SUBLANES - CONV_W), (0, 0))),
        conv_b=jnp.concatenate([half(conv_b[:, :D_FF]), half(conv_b[:, D_FF:])], axis=1),
        ln2_w=p["ln2_w"][l][None, :], ln2_b=p["ln2_b"][l][None, :],
    )


def _unpad_ff(x):
    return jnp.concatenate([_from_tiled_ff(x[..., :D_FFP], FFN_TF), _from_tiled_ff(x[..., D_FFP:], FFN_TF)], axis=-1)


def _pad_ff(x):
    return jnp.concatenate([_to_tiled_ff(x[..., :D_FF], FFN_TF), _to_tiled_ff(x[..., D_FF:], FFN_TF)], axis=-1)


def _mixer_tail(x2d, z, b, t, t_use, lw, o_mlstm, o_nsa, s5_state, conv_state8, tm):
    pad8 = lambda s: jnp.pad(s.reshape(b, S5_CH), ((0, SUBLANES - b), (0, 0)))
    o_s5, xr, xi = _s5(z.reshape(b, t, N_IN), lw["s5_wb"], lw["s5_wc"], lw["s5_lam_r"], lw["s5_lam_i"], lw["s5_d"],
                       lw["s5_glu_w"], lw["s5_glu_b"], pad8(s5_state[0]), pad8(s5_state[1]), lc=S5_LC, steps=t_use)
    o_s5 = o_s5.reshape(b * t, D_S5)
    layer = lw["layer"]
    x1 = _wout_ln(o_mlstm.reshape(b * t, D_MLSTM), o_nsa.reshape(b * t, D_NSA), o_s5, x2d,
                  lw["w_out"], layer, lw["ln1_w"], lw["ln1_b"], tm=512)
    s5_new = (xr[:b].reshape(b, S5_GROUPS, S5_STATE), xi[:b].reshape(b, S5_GROUPS, S5_STATE))
    if conv_state8 is None:
        x2, tail_a, tail_g = _ffn_prompt(x1.reshape(b, t, D_MODEL), lw["w_up"], lw["w_down"], layer, lw["conv_w8"],
                                         lw["conv_b"], lw["ln2_w"], lw["ln2_b"], tt=FFN_TT, tf=FFN_TF)
        keep = slice(SUBLANES - (CONV_W - 1), SUBLANES)
        conv_new = jnp.concatenate([_from_tiled_ff(tail_a[:, keep], FFN_TF), _from_tiled_ff(tail_g[:, keep], FFN_TF)],
                                   axis=-1)
        return x2.reshape(b * t, D_MODEL), s5_new, conv_new
    up = _matmul(x1, lw["w_up"], layer, tm=tm, tn=FFN_TF, n=2 * D_FFP, w_cols=_up_cols(FFN_TF))
    hgate = _convgate(up.reshape(b, t, 2 * D_FFP), conv_state8, lw["conv_w8"], lw["conv_b"], tf=FFN_TF)
    x2 = _down_ln(hgate.reshape(b * t, D_FFP), lw["w_down"], layer, x1, lw["ln2_w"], lw["ln2_b"], tm=512, tk=FFN_TF)
    conv_new = _unpad_ff(up.reshape(b, t, 2 * D_FFP)[:, t_use - (CONV_W - 1):t_use])
    return x2, s5_new, conv_new


def _prompt_layer(x2d, b, t, lw, rel_bias, rows_so_far):
    z, kv_rows, win_rows = _proj_in(x2d, lw["w_in"], lw["layer"], rows_so_far, tm=min(1024, b * t), tn=768)
    z3 = z.reshape(b, t, N_IN)
    zeros = lambda *s: jnp.zeros(s, F32)
    o_mlstm, c1, n1, m1 = _mlstm(z3, lw["mlstm_gate_b"], lw["mlstm_norm_w"],
                                 zeros(b, MLSTM_HEADS, HEAD_DIM, HEAD_DIM), zeros(b, MLSTM_HEADS, LANES),
                                 zeros(b, MLSTM_HEADS, LANES), L=MLSTM_L, lin=MLSTM_L, t_valid=t)
    kcvc = _cmp_finish(_cmp_project_prompt(z3, lw["wcat"]), lw["cmp_b1"], lw["cmp_w2"])
    ncp = t // CMP_STRIDE
    bias_c = _bias_table(rel_bias, np.arange(t)[:, None] - (np.arange(ncp) * CMP_STRIDE + CMP_BLOCK - 1)[None, :])
    ti = np.arange(TQ)[:, None] - np.arange(TK)[None, :]
    far = _bias_table(rel_bias, np.full((1, 1), FAR_DIST))
    tz = jnp.stack([_bias_table(rel_bias, d * TQ + ti) - far for d in range(N_TZ)])
    o_nsa = _nsa_prompt(z3, kcvc, bias_c, tz, lw["nsa_gate_b"])
    x2, s5_new, conv_new = _mixer_tail(
        x2d, z, b, t, t, lw, o_mlstm, o_nsa, (zeros(b, S5_GROUPS, S5_STATE), zeros(b, S5_GROUPS, S5_STATE)),
        None, tm=1024)
    state = (None, None, c1, n1, m1[:, :, 0], s5_new[0], s5_new[1], conv_new)
    return x2, state, (kv_rows, win_rows)


def _sample_layer(x2d, b, tn, lw, rel_bias, layer, cache6, page_table, win6, mlstm_state, s5_state, conv_state):
    npages = page_table.shape[1]
    pos0 = npages * PAGE_SIZE
    z = _matmul(x2d, lw["w_in"], lw["layer"], tm=1024, tn=768)
    z8 = z.reshape(b, TS, N_IN)
    c0, n0, m0 = mlstm_state
    o_mlstm, c1, n1, m1 = _mlstm(z8, lw["mlstm_gate_b"], lw["mlstm_norm_w"], c0, n0,
                                 jnp.broadcast_to(m0[:, :, None], (b, MLSTM_HEADS, LANES)),
                                 L=LANES, lin=TS, t_valid=tn)
    n_chunks = (pos0 + tn) // CMP_STRIDE
    n_cmp = n_chunks - CMP_BLOCK // CMP_STRIDE + 1
    n_sel = -(-(pos0 + tn) // SEL_BLOCK)
    kcvc = _cmp_finish(_cmp_project_sample(cache6, page_table, layer, lw["wcat"]), lw["cmp_b1"], lw["cmp_w2"])
    qpos = pos0 + np.arange(TS)[:, None]
    bias_c = _bias_table(rel_bias, qpos - (np.arange(n_chunks) * CMP_STRIDE + CMP_BLOCK - 1)[None, :])
    o_c, sel = _nsa_sample_cmp(z8, kcvc, bias_c, pos0=pos0, n_cmp=n_cmp, n_sel=n_sel)
    kk = np.arange(PAGE_SIZE)[None, :]
    bias_tail = _bias_table(rel_bias, np.repeat(qpos - (pos0 - PAGE_SIZE + kk), KV_HEADS, axis=1))
    bias_new = _bias_table(rel_bias, qpos - (pos0 + kk))
    bias_far = _bias_table(rel_bias, np.broadcast_to(qpos - (pos0 - 2 * PAGE_SIZE), (TS, PAGE_SIZE)))
    o_s = _nsa_sample_sel(z8, sel, bias_tail, bias_new, bias_far, cache6, page_table, layer, pos0=pos0)
    wb = win6.shape[2]
    wk = np.arange(wb)[None, :]
    bias_w = _bias_table(rel_bias, np.concatenate([qpos - (pos0 - wb + wk), qpos - (pos0 + kk)], axis=1))
    o_nsa = _nsa_sample_win(z8, win6, layer, bias_w, o_c, o_s, lw["nsa_gate_b"], pos0=pos0)
    conv_state8 = jnp.pad(_pad_ff(conv_state), ((0, 0), (SUBLANES - (CONV_W - 1), 0), (0, 0)))
    x2, s5_new, conv_new = _mixer_tail(x2d, z, b, TS, tn, lw, o_mlstm, o_nsa, s5_state, conv_state8, tm=1024)
    nkv = z8[:, :tn, C_NKV:C_NKV + N_KV_SLOTS * KV_HEADS * HEAD_DIM].reshape(b, tn, N_KV_SLOTS, KV_HEADS, HEAD_DIM)
    state = (nkv[:, :, :4], nkv[:, :, 4:], c1, n1, m1[:, :, 0], s5_new[0], s5_new[1], conv_new)
    return x2, state


def kernel(x_prompt, x_sample, cache_nsa_kv, cache_win_kv, state_mlstm_c, state_mlstm_n, state_mlstm_m,
           state_s5_re, state_s5_im, state_ffn_conv, page_table, w_in, mlstm_gate_b, mlstm_norm_w,
           nsa_gate_b, cmp_w1, cmp_b1, cmp_w2, rel_bias, s5_a_re, s5_a_im, s5_b_re, s5_b_im, s5_c_re,
           s5_c_im, s5_d, s5_log_step, s5_glu_w, s5_glu_b, w_out, ln1_w, ln1_b, ffn_w_up, ffn_conv_w,
           ffn_conv_b, ffn_w_down, ln2_w, ln2_b):
    params = dict(w_in=w_in, mlstm_gate_b=mlstm_gate_b, mlstm_norm_w=mlstm_norm_w, nsa_gate_b=nsa_gate_b,
                  cmp_w1=cmp_w1, cmp_b1=cmp_b1, cmp_w2=cmp_w2, s5_a_re=s5_a_re, s5_a_im=s5_a_im, s5_b_re=s5_b_re,
                  s5_b_im=s5_b_im, s5_c_re=s5_c_re, s5_c_im=s5_c_im, s5_d=s5_d, s5_log_step=s5_log_step,
                  s5_glu_w=s5_glu_w, s5_glu_b=s5_glu_b, w_out=w_out, ln1_w=ln1_w, ln1_b=ln1_b, ffn_w_up=ffn_w_up,
                  ffn_conv_w=ffn_conv_w, ffn_conv_b=ffn_conv_b, ffn_w_down=ffn_w_down, ln2_w=ln2_w, ln2_b=ln2_b)
    depth = w_in.shape[0]
    bp, tp, _ = x_prompt.shape
    bs, tn, _ = x_sample.shape
    assert tp % TK == 0 and tn < CMP_STRIDE and tn <= TS and SUBLANES % bp == 0 and SUBLANES % bs == 0
    assert (tn * bs) % SUBLANES == 0 and tp % min(FFN_TT, tp) == 0
    assert page_table.shape[1] % CMP_PAGES == 0 and cache_nsa_kv.shape[2] == PAGE_SIZE
    xp = x_prompt.reshape(bp * tp, D_MODEL)
    xs = jnp.pad(x_sample, ((0, 0), (0, TS - tn), (0, 0))).reshape(bs * TS, D_MODEL)
    p_states, s_states = [], []
    big = _pack_weights(params)
    rows_so_far = None
    for l in range(depth):
        lw = _pack_layer(params, big, l)
        xp, sp, rows_so_far = _prompt_layer(xp, bp, tp, lw, rel_bias, rows_so_far)
        xs, ss = _sample_layer(xs, bs, tn, lw, rel_bias, l, cache_nsa_kv, page_table, cache_win_kv,
                               (state_mlstm_c[l], jnp.pad(state_mlstm_n[l], ((0, 0), (0, 0), (0, LANES - HEAD_DIM))),
                                state_mlstm_m[l]),
                               (state_s5_re[l], state_s5_im[l]), state_ffn_conv[l])
        p_states.append(sp)
        s_states.append(ss)
    stk = lambda states, i: jnp.stack([s[i] for s in states])
    y_prompt = xp.reshape(bp, tp, D_MODEL)
    y_sample = xs.reshape(bs, TS, D_MODEL)[:, :tn]
    wrows = min(WINDOW, tp)
    nsa_kv_p = rows_so_far[0].reshape(depth, bp, tp, 4, KV_HEADS, HEAD_DIM)
    win_kv_p = rows_so_far[1].reshape(depth, bp, tp, 2, KV_HEADS, HEAD_DIM)[:, :, tp - wrows:]
    return (y_prompt, y_sample,
            nsa_kv_p, stk(s_states, 0), win_kv_p, stk(s_states, 1),
            stk(p_states, 2), stk(s_states, 2), stk(p_states, 3), stk(s_states, 3), stk(p_states, 4), stk(s_states, 4),
            stk(p_states, 5), stk(s_states, 5), stk(p_states, 6), stk(s_states, 6), stk(p_states, 7), stk(s_states, 7))
```
